```python
import jax, jax.numpy as jnp
from jax import lax
import numpy as np

D_MODEL = 1024
BATCH = 8
SEQ = 8192
DEPTH = 4

N_MIXERS = 2
N_LRU_LAYERS = (DEPTH + 1) // 2
N_POOL_LAYERS = DEPTH // 2
D_FF = 11 * D_MODEL // 4
D_RNN = 5 * D_MODEL // 4
LRU_HEADS = 16
LRU_HEAD_DIM = D_RNN // LRU_HEADS
CONV_WIDTH = 4
LRU_C = 8.0
POOL_WINDOWS = (2, 4, 8, 16)
POOL_GROUPS = len(POOL_WINDOWS)
POOL_GROUP_DIM = D_MODEL // POOL_GROUPS
PLE_DIM = 256
RMS_EPS = 1e-6

kernel_name = "hybrid_rglru_pool_macaron_ple"


def rms_norm(x, g):
    xf = x.astype(jnp.float32)
    y = xf * lax.rsqrt(jnp.mean(xf * xf, axis=-1, keepdims=True) + RMS_EPS)
    return (y * g.astype(jnp.float32)).astype(x.dtype)


def swiglu(x, w_gate, w_up, w_down):
    return (jax.nn.silu(x @ w_gate) * (x @ w_up)) @ w_down


def _lin_combine(c1, c2):
    a1, b1 = c1
    a2, b2 = c2
    return a1 * a2, a2 * b1 + b2


def rglru_mixer(x, w_in, conv_w, conv_b, w_a, b_a, w_x, b_x, a_param, w_out):
    B, S, _ = x.shape
    z = x @ w_in
    gate_branch, xb = z[..., :D_RNN], z[..., D_RNN:]
    xp = jnp.pad(xb, ((0, 0), (CONV_WIDTH - 1, 0), (0, 0)))
    xc = conv_b + conv_w[0] * xp[:, 0:S]
    for k in range(1, CONV_WIDTH):
        xc = xc + conv_w[k] * xp[:, k:k + S]
    xh = xc.reshape(B, S, LRU_HEADS, LRU_HEAD_DIM)
    r = jax.nn.sigmoid(jnp.einsum('bshi,hij->bshj', xh, w_a).reshape(B, S, D_RNN) + b_a)
    ig = jax.nn.sigmoid(jnp.einsum('bshi,hij->bshj', xh, w_x).reshape(B, S, D_RNN) + b_x)
    log_a = -LRU_C * r.astype(jnp.float32) * jax.nn.softplus(-a_param.astype(jnp.float32))
    a = jnp.exp(log_a)
    mult = jnp.sqrt(-jnp.expm1(2.0 * log_a))
    bterm = mult * (ig * xc).astype(jnp.float32)
    _, h = lax.associative_scan(_lin_combine, (a, bterm), axis=1)
    y = h.astype(x.dtype) * jax.nn.gelu(gate_branch)
    return y @ w_out


def pool_mixer(x, w, b, scale):
    B, S, _ = x.shape
    xf = x.astype(jnp.float32)
    cs = jnp.cumsum(xf, axis=1)
    t = jnp.arange(S)
    outs = []
    for g, win in enumerate(POOL_WINDOWS):
        lo, hi = g * POOL_GROUP_DIM, (g + 1) * POOL_GROUP_DIM
        c = cs[..., lo:hi]
        prev = jnp.pad(c[:, :S - win], ((0, 0), (win, 0), (0, 0)))
        count = jnp.minimum(t + 1, win).astype(jnp.float32)[None, :, None]
        outs.append((c - prev) / count - xf[..., lo:hi])
    u = jnp.stack(outs, axis=2).astype(x.dtype)
    y = jnp.einsum('bsgc,gcd->bsgd', u, w).reshape(B, S, D_MODEL)
    return (y + b) * scale


def _fwd_setup_inputs(seed: int = 0) -> dict:
    key = jax.random.key(seed)
    ks = iter(jax.random.split(key, 40))

    def nrm(shape, scale):
        return scale * jax.random.normal(next(ks), shape, jnp.float32)

    D, F, L = D_MODEL, D_FF, DEPTH
    NL, NP = N_LRU_LAYERS, N_POOL_LAYERS
    x = nrm((BATCH, SEQ, D), 1.0)
    p = nrm((DEPTH, BATCH, SEQ, PLE_DIM), 1.0)
    ffn1_norm = 1.0 + nrm((L, D), 0.1)
    ffn1_w_gate = nrm((L, D, F), D ** -0.5)
    ffn1_w_up = nrm((L, D, F), D ** -0.5)
    ffn1_w_down = nrm((L, F, D), F ** -0.5)
    mix_norm = 1.0 + nrm((L, D), 0.1)
    lru_w_in = nrm((NL, D, 2 * D_RNN), D ** -0.5)
    lru_conv_w = nrm((NL, CONV_WIDTH, D_RNN), CONV_WIDTH ** -0.5)
    lru_conv_b = nrm((NL, D_RNN), 0.01)
    lru_w_a = nrm((NL, LRU_HEADS, LRU_HEAD_DIM, LRU_HEAD_DIM), LRU_HEAD_DIM ** -0.5)
    lru_b_a = nrm((NL, D_RNN), 0.01)
    lru_w_x = nrm((NL, LRU_HEADS, LRU_HEAD_DIM, LRU_HEAD_DIM), LRU_HEAD_DIM ** -0.5)
    lru_b_x = nrm((NL, D_RNN), 0.01)
    u = jax.random.uniform(next(ks), (NL, D_RNN), jnp.float32, minval=0.9, maxval=0.999)
    a0 = u ** (1.0 / LRU_C)
    lru_a_param = jnp.log(a0) - jnp.log1p(-a0)
    lru_w_out = nrm((NL, D_RNN, D), D_RNN ** -0.5)
    pool_w = nrm((NP, POOL_GROUPS, POOL_GROUP_DIM, POOL_GROUP_DIM), POOL_GROUP_DIM ** -0.5)
    pool_b = nrm((NP, D), 0.01)
    pool_scale = 1.0 + nrm((NP, D), 0.1)
    ffn2_norm = 1.0 + nrm((L, D), 0.1)
    ffn2_w_gate = nrm((L, D, F), D ** -0.5)
    ffn2_w_up = nrm((L, D, F), D ** -0.5)
    ffn2_w_down = nrm((L, F, D), F ** -0.5)
    ple_norm = 1.0 + nrm((L, D), 0.1)
    ple_w_gate = nrm((L, D, D), D ** -0.5)
    ple_w_proj = nrm((L, PLE_DIM, D), PLE_DIM ** -0.5)
    final_norm = 1.0 + nrm((D,), 0.1)
    return {"x": x, "p": p,
            "ffn1_norm": ffn1_norm, "ffn1_w_gate": ffn1_w_gate, "ffn1_w_up": ffn1_w_up, "ffn1_w_down": ffn1_w_down,
            "mix_norm": mix_norm,
            "lru_w_in": lru_w_in, "lru_conv_w": lru_conv_w, "lru_conv_b": lru_conv_b,
            "lru_w_a": lru_w_a, "lru_b_a": lru_b_a, "lru_w_x": lru_w_x, "lru_b_x": lru_b_x,
            "lru_a_param": lru_a_param, "lru_w_out": lru_w_out,
            "pool_w": pool_w, "pool_b": pool_b, "pool_scale": pool_scale,
            "ffn2_norm": ffn2_norm, "ffn2_w_gate": ffn2_w_gate, "ffn2_w_up": ffn2_w_up, "ffn2_w_down": ffn2_w_down,
            "ple_norm": ple_norm, "ple_w_gate": ple_w_gate, "ple_w_proj": ple_w_proj,
            "final_norm": final_norm}


def _fwd_reference(x, p, ffn1_norm, ffn1_w_gate, ffn1_w_up, ffn1_w_down, mix_norm,
              lru_w_in, lru_conv_w, lru_conv_b, lru_w_a, lru_b_a, lru_w_x, lru_b_x, lru_a_param, lru_w_out,
              pool_w, pool_b, pool_scale,
              ffn2_norm, ffn2_w_gate, ffn2_w_up, ffn2_w_down,
              ple_norm, ple_w_gate, ple_w_proj, final_norm):
    h = x
    for i in range(DEPTH):
        h = h + 0.5 * swiglu(rms_norm(h, ffn1_norm[i]), ffn1_w_gate[i], ffn1_w_up[i], ffn1_w_down[i])
        hn = rms_norm(h, mix_norm[i])
        j = i // N_MIXERS
        if i % N_MIXERS == 0:
            m = rglru_mixer(hn, lru_w_in[j], lru_conv_w[j], lru_conv_b[j], lru_w_a[j], lru_b_a[j],
                            lru_w_x[j], lru_b_x[j], lru_a_param[j], lru_w_out[j])
        else:
            m = pool_mixer(hn, pool_w[j], pool_b[j], pool_scale[j])
        h = h + m
        h = h + 0.5 * swiglu(rms_norm(h, ffn2_norm[i]), ffn2_w_gate[i], ffn2_w_up[i], ffn2_w_down[i])
        gate = jax.nn.sigmoid(rms_norm(h, ple_norm[i]) @ ple_w_gate[i])
        h = h + gate * (p[i].astype(h.dtype) @ ple_w_proj[i])
    return rms_norm(h, final_norm)


import jax as _jax
import jax.numpy as _jnp

TWIN_FORMAT = 'train_step'
FWD_PARAMS = ['x', 'p', 'ffn1_norm', 'ffn1_w_gate', 'ffn1_w_up', 'ffn1_w_down', 'mix_norm', 'lru_w_in', 'lru_conv_w', 'lru_conv_b', 'lru_w_a', 'lru_b_a', 'lru_w_x', 'lru_b_x', 'lru_a_param', 'lru_w_out', 'pool_w', 'pool_b', 'pool_scale', 'ffn2_norm', 'ffn2_w_gate', 'ffn2_w_up', 'ffn2_w_down', 'ple_norm', 'ple_w_gate', 'ple_w_proj', 'final_norm']
TWIN_WEIGHTS = ['ffn1_norm', 'ffn1_w_gate', 'ffn1_w_up', 'ffn1_w_down', 'mix_norm', 'lru_w_in', 'lru_conv_w', 'lru_conv_b', 'lru_w_a', 'lru_b_a', 'lru_w_x', 'lru_b_x', 'lru_a_param', 'lru_w_out', 'pool_w', 'pool_b', 'pool_scale', 'ffn2_norm', 'ffn2_w_gate', 'ffn2_w_up', 'ffn2_w_down', 'ple_norm', 'ple_w_gate', 'ple_w_proj', 'final_norm']
TWIN_DIFF_INPUT = 'x'
TWIN_INPUTS = ['x', 'p', 'ffn1_norm', 'ffn1_w_gate', 'ffn1_w_up', 'ffn1_w_down', 'mix_norm', 'lru_w_in', 'lru_conv_w', 'lru_conv_b', 'lru_w_a', 'lru_b_a', 'lru_w_x', 'lru_b_x', 'lru_a_param', 'lru_w_out', 'pool_w', 'pool_b', 'pool_scale', 'ffn2_norm', 'ffn2_w_gate', 'ffn2_w_up', 'ffn2_w_down', 'ple_norm', 'ple_w_gate', 'ple_w_proj', 'final_norm', 'loss_target', 'm_ffn1_norm', 'm_ffn1_w_gate', 'm_ffn1_w_up', 'm_ffn1_w_down', 'm_mix_norm', 'm_lru_w_in', 'm_lru_conv_w', 'm_lru_conv_b', 'm_lru_w_a', 'm_lru_b_a', 'm_lru_w_x', 'm_lru_b_x', 'm_lru_a_param', 'm_lru_w_out', 'm_pool_w', 'm_pool_b', 'm_pool_scale', 'm_ffn2_norm', 'm_ffn2_w_gate', 'm_ffn2_w_up', 'm_ffn2_w_down', 'm_ple_norm', 'm_ple_w_gate', 'm_ple_w_proj', 'm_final_norm', 'v_ffn1_norm', 'v_ffn1_w_gate', 'v_ffn1_w_up', 'v_ffn1_w_down', 'v_mix_norm', 'v_lru_w_in', 'v_lru_conv_w', 'v_lru_conv_b', 'v_lru_w_a', 'v_lru_b_a', 'v_lru_w_x', 'v_lru_b_x', 'v_lru_a_param', 'v_lru_w_out', 'v_pool_w', 'v_pool_b', 'v_pool_scale', 'v_ffn2_norm', 'v_ffn2_w_gate', 'v_ffn2_w_up', 'v_ffn2_w_down', 'v_ple_norm', 'v_ple_w_gate', 'v_ple_w_proj', 'v_final_norm']
TWIN_OUTPUTS = ['loss', 'grad_x', 'grad_ffn1_norm', 'grad_ffn1_w_gate', 'grad_ffn1_w_up', 'grad_ffn1_w_down', 'grad_mix_norm', 'grad_lru_w_in', 'grad_lru_conv_w', 'grad_lru_conv_b', 'grad_lru_w_a', 'grad_lru_b_a', 'grad_lru_w_x', 'grad_lru_b_x', 'grad_lru_a_param', 'grad_lru_w_out', 'grad_pool_w', 'grad_pool_b', 'grad_pool_scale', 'grad_ffn2_norm', 'grad_ffn2_w_gate', 'grad_ffn2_w_up', 'grad_ffn2_w_down', 'grad_ple_norm', 'grad_ple_w_gate', 'grad_ple_w_proj', 'grad_final_norm', 'delta_ffn1_norm', 'delta_ffn1_w_gate', 'delta_ffn1_w_up', 'delta_ffn1_w_down', 'delta_mix_norm', 'delta_lru_w_in', 'delta_lru_conv_w', 'delta_lru_conv_b', 'delta_lru_w_a', 'delta_lru_b_a', 'delta_lru_w_x', 'delta_lru_b_x', 'delta_lru_a_param', 'delta_lru_w_out', 'delta_pool_w', 'delta_pool_b', 'delta_pool_scale', 'delta_ffn2_norm', 'delta_ffn2_w_gate', 'delta_ffn2_w_up', 'delta_ffn2_w_down', 'delta_ple_norm', 'delta_ple_w_gate', 'delta_ple_w_proj', 'delta_final_norm', 'new_m_ffn1_norm', 'new_m_ffn1_w_gate', 'new_m_ffn1_w_up', 'new_m_ffn1_w_down', 'new_m_mix_norm', 'new_m_lru_w_in', 'new_m_lru_conv_w', 'new_m_lru_conv_b', 'new_m_lru_w_a', 'new_m_lru_b_a', 'new_m_lru_w_x', 'new_m_lru_b_x', 'new_m_lru_a_param', 'new_m_lru_w_out', 'new_m_pool_w', 'new_m_pool_b', 'new_m_pool_scale', 'new_m_ffn2_norm', 'new_m_ffn2_w_gate', 'new_m_ffn2_w_up', 'new_m_ffn2_w_down', 'new_m_ple_norm', 'new_m_ple_w_gate', 'new_m_ple_w_proj', 'new_m_final_norm', 'new_v_ffn1_norm', 'new_v_ffn1_w_gate', 'new_v_ffn1_w_up', 'new_v_ffn1_w_down', 'new_v_mix_norm', 'new_v_lru_w_in', 'new_v_lru_conv_w', 'new_v_lru_conv_b', 'new_v_lru_w_a', 'new_v_lru_b_a', 'new_v_lru_w_x', 'new_v_lru_b_x', 'new_v_lru_a_param', 'new_v_lru_w_out', 'new_v_pool_w', 'new_v_pool_b', 'new_v_pool_scale', 'new_v_ffn2_norm', 'new_v_ffn2_w_gate', 'new_v_ffn2_w_up', 'new_v_ffn2_w_down', 'new_v_ple_norm', 'new_v_ple_w_gate', 'new_v_ple_w_proj', 'new_v_final_norm']
TWIN_LEAF_KINDS = {'loss': 'loss', 'grad_x': 'grad_x', 'grad_ffn1_norm': 'grad_w', 'grad_ffn1_w_gate': 'grad_w', 'grad_ffn1_w_up': 'grad_w', 'grad_ffn1_w_down': 'grad_w', 'grad_mix_norm': 'grad_w', 'grad_lru_w_in': 'grad_w', 'grad_lru_conv_w': 'grad_w', 'grad_lru_conv_b': 'grad_w', 'grad_lru_w_a': 'grad_w', 'grad_lru_b_a': 'grad_w', 'grad_lru_w_x': 'grad_w', 'grad_lru_b_x': 'grad_w', 'grad_lru_a_param': 'grad_w', 'grad_lru_w_out': 'grad_w', 'grad_pool_w': 'grad_w', 'grad_pool_b': 'grad_w', 'grad_pool_scale': 'grad_w', 'grad_ffn2_norm': 'grad_w', 'grad_ffn2_w_gate': 'grad_w', 'grad_ffn2_w_up': 'grad_w', 'grad_ffn2_w_down': 'grad_w', 'grad_ple_norm': 'grad_w', 'grad_ple_w_gate': 'grad_w', 'grad_ple_w_proj': 'grad_w', 'grad_final_norm': 'grad_w', 'delta_ffn1_norm': 'delta_w', 'delta_ffn1_w_gate': 'delta_w', 'delta_ffn1_w_up': 'delta_w', 'delta_ffn1_w_down': 'delta_w', 'delta_mix_norm': 'delta_w', 'delta_lru_w_in': 'delta_w', 'delta_lru_conv_w': 'delta_w', 'delta_lru_conv_b': 'delta_w', 'delta_lru_w_a': 'delta_w', 'delta_lru_b_a': 'delta_w', 'delta_lru_w_x': 'delta_w', 'delta_lru_b_x': 'delta_w', 'delta_lru_a_param': 'delta_w', 'delta_lru_w_out': 'delta_w', 'delta_pool_w': 'delta_w', 'delta_pool_b': 'delta_w', 'delta_pool_scale': 'delta_w', 'delta_ffn2_norm': 'delta_w', 'delta_ffn2_w_gate': 'delta_w', 'delta_ffn2_w_up': 'delta_w', 'delta_ffn2_w_down': 'delta_w', 'delta_ple_norm': 'delta_w', 'delta_ple_w_gate': 'delta_w', 'delta_ple_w_proj': 'delta_w', 'delta_final_norm': 'delta_w', 'new_m_ffn1_norm': 'new_m', 'new_m_ffn1_w_gate': 'new_m', 'new_m_ffn1_w_up': 'new_m', 'new_m_ffn1_w_down': 'new_m', 'new_m_mix_norm': 'new_m', 'new_m_lru_w_in': 'new_m', 'new_m_lru_conv_w': 'new_m', 'new_m_lru_conv_b': 'new_m', 'new_m_lru_w_a': 'new_m', 'new_m_lru_b_a': 'new_m', 'new_m_lru_w_x': 'new_m', 'new_m_lru_b_x': 'new_m', 'new_m_lru_a_param': 'new_m', 'new_m_lru_w_out': 'new_m', 'new_m_pool_w': 'new_m', 'new_m_pool_b': 'new_m', 'new_m_pool_scale': 'new_m', 'new_m_ffn2_norm': 'new_m', 'new_m_ffn2_w_gate': 'new_m', 'new_m_ffn2_w_up': 'new_m', 'new_m_ffn2_w_down': 'new_m', 'new_m_ple_norm': 'new_m', 'new_m_ple_w_gate': 'new_m', 'new_m_ple_w_proj': 'new_m', 'new_m_final_norm': 'new_m', 'new_v_ffn1_norm': 'new_v', 'new_v_ffn1_w_gate': 'new_v', 'new_v_ffn1_w_up': 'new_v', 'new_v_ffn1_w_down': 'new_v', 'new_v_mix_norm': 'new_v', 'new_v_lru_w_in': 'new_v', 'new_v_lru_conv_w': 'new_v', 'new_v_lru_conv_b': 'new_v', 'new_v_lru_w_a': 'new_v', 'new_v_lru_b_a': 'new_v', 'new_v_lru_w_x': 'new_v', 'new_v_lru_b_x': 'new_v', 'new_v_lru_a_param': 'new_v', 'new_v_lru_w_out': 'new_v', 'new_v_pool_w': 'new_v', 'new_v_pool_b': 'new_v', 'new_v_pool_scale': 'new_v', 'new_v_ffn2_norm': 'new_v', 'new_v_ffn2_w_gate': 'new_v', 'new_v_ffn2_w_up': 'new_v', 'new_v_ffn2_w_down': 'new_v', 'new_v_ple_norm': 'new_v', 'new_v_ple_w_gate': 'new_v', 'new_v_ple_w_proj': 'new_v', 'new_v_final_norm': 'new_v'}


def _forward(args):
    return _fwd_reference(*[args[k] for k in FWD_PARAMS])


def _output_shape():
    def fwd():
        inp = _fwd_setup_inputs(0)
        return _fwd_reference(*[inp[k] for k in FWD_PARAMS])
    out = _jax.eval_shape(fwd)
    return out.shape, out.dtype

N_MICROBATCH = 1
ADAM_LR = 0.001
ADAM_B1 = 0.9
ADAM_B2 = 0.999
ADAM_EPS = 1e-08
ADAM_WD = 0.01
ADAM_STEP = 10
PER_EXAMPLE_BATCH_AXIS = {'x': 0, 'p': 1, 'loss_target': 0}
SHARED_INPUTS = []
_WEIGHT_DTYPES = {'ffn1_norm': _jnp.float32, 'ffn1_w_gate': _jnp.float32, 'ffn1_w_up': _jnp.float32, 'ffn1_w_down': _jnp.float32, 'mix_norm': _jnp.float32, 'lru_w_in': _jnp.float32, 'lru_conv_w': _jnp.float32, 'lru_conv_b': _jnp.float32, 'lru_w_a': _jnp.float32, 'lru_b_a': _jnp.float32, 'lru_w_x': _jnp.float32, 'lru_b_x': _jnp.float32, 'lru_a_param': _jnp.float32, 'lru_w_out': _jnp.float32, 'pool_w': _jnp.float32, 'pool_b': _jnp.float32, 'pool_scale': _jnp.float32, 'ffn2_norm': _jnp.float32, 'ffn2_w_gate': _jnp.float32, 'ffn2_w_up': _jnp.float32, 'ffn2_w_down': _jnp.float32, 'ple_norm': _jnp.float32, 'ple_w_gate': _jnp.float32, 'ple_w_proj': _jnp.float32, 'final_norm': _jnp.float32}
MOMENT_SCALE = {'ffn1_norm': 8.363381e-02, 'ffn1_w_gate': 3.553653e-02, 'ffn1_w_up': 3.505074e-02, 'ffn1_w_down': 5.824482e-02, 'mix_norm': 1.978719e-01, 'lru_w_in': 9.240788e-02, 'lru_conv_w': 1.038781e-01, 'lru_conv_b': 1.420391e+00, 'lru_w_a': 4.047453e-02, 'lru_b_a': 2.724299e-02, 'lru_w_x': 7.358761e-02, 'lru_b_x': 3.842400e-02, 'lru_a_param': 4.706231e-02, 'lru_w_out': 1.383862e-01, 'pool_w': 2.185666e-01, 'pool_b': 6.000827e-01, 'pool_scale': 1.960973e+00, 'ffn2_norm': 7.220789e-02, 'ffn2_w_gate': 2.979292e-02, 'ffn2_w_up': 2.903352e-02, 'ffn2_w_down': 4.831796e-02, 'ple_norm': 3.433961e-02, 'ple_w_gate': 3.460353e-02, 'ple_w_proj': 8.731298e-02, 'final_norm': 6.464947e+01}


def _to_microbatches(a, axis):
    t = _jnp.moveaxis(a, axis, 0)
    t = t.reshape((N_MICROBATCH, t.shape[0] // N_MICROBATCH) + t.shape[1:])
    return _jnp.moveaxis(t, 1, axis + 1)


def setup_inputs(seed: int = 0) -> dict:
    inp = _fwd_setup_inputs(seed)
    key = _jax.random.fold_in(_jax.random.key(seed), 7919)
    shape, _ = _output_shape()
    out = dict(inp)
    out["loss_target"] = _jax.random.normal(_jax.random.fold_in(key, 0), shape, _jnp.float32)
    for i, name in enumerate(TWIN_WEIGHTS):
        w = inp[name].astype(_jnp.float32)
        if MOMENT_SCALE is None:
            s = _jnp.sqrt(_jnp.mean(_jnp.square(w)) + 1e-30)
        else:
            s = MOMENT_SCALE[name]
        km, kv = _jax.random.split(_jax.random.fold_in(key, i + 1))
        out[name] = w
        out["m_" + name] = s * _jax.random.normal(km, w.shape, _jnp.float32)
        out["v_" + name] = (s * s) * _jax.random.uniform(kv, w.shape, _jnp.float32, 0.5, 1.5)
    if N_MICROBATCH > 1:
        for name, axis in PER_EXAMPLE_BATCH_AXIS.items():
            out[name] = _to_microbatches(out[name], axis)
    return {'x': out['x'], 'p': out['p'], 'ffn1_norm': out['ffn1_norm'], 'ffn1_w_gate': out['ffn1_w_gate'], 'ffn1_w_up': out['ffn1_w_up'], 'ffn1_w_down': out['ffn1_w_down'], 'mix_norm': out['mix_norm'], 'lru_w_in': out['lru_w_in'], 'lru_conv_w': out['lru_conv_w'], 'lru_conv_b': out['lru_conv_b'], 'lru_w_a': out['lru_w_a'], 'lru_b_a': out['lru_b_a'], 'lru_w_x': out['lru_w_x'], 'lru_b_x': out['lru_b_x'], 'lru_a_param': out['lru_a_param'], 'lru_w_out': out['lru_w_out'], 'pool_w': out['pool_w'], 'pool_b': out['pool_b'], 'pool_scale': out['pool_scale'], 'ffn2_norm': out['ffn2_norm'], 'ffn2_w_gate': out['ffn2_w_gate'], 'ffn2_w_up': out['ffn2_w_up'], 'ffn2_w_down': out['ffn2_w_down'], 'ple_norm': out['ple_norm'], 'ple_w_gate': out['ple_w_gate'], 'ple_w_proj': out['ple_w_proj'], 'final_norm': out['final_norm'], 'loss_target': out['loss_target'], 'm_ffn1_norm': out['m_ffn1_norm'], 'm_ffn1_w_gate': out['m_ffn1_w_gate'], 'm_ffn1_w_up': out['m_ffn1_w_up'], 'm_ffn1_w_down': out['m_ffn1_w_down'], 'm_mix_norm': out['m_mix_norm'], 'm_lru_w_in': out['m_lru_w_in'], 'm_lru_conv_w': out['m_lru_conv_w'], 'm_lru_conv_b': out['m_lru_conv_b'], 'm_lru_w_a': out['m_lru_w_a'], 'm_lru_b_a': out['m_lru_b_a'], 'm_lru_w_x': out['m_lru_w_x'], 'm_lru_b_x': out['m_lru_b_x'], 'm_lru_a_param': out['m_lru_a_param'], 'm_lru_w_out': out['m_lru_w_out'], 'm_pool_w': out['m_pool_w'], 'm_pool_b': out['m_pool_b'], 'm_pool_scale': out['m_pool_scale'], 'm_ffn2_norm': out['m_ffn2_norm'], 'm_ffn2_w_gate': out['m_ffn2_w_gate'], 'm_ffn2_w_up': out['m_ffn2_w_up'], 'm_ffn2_w_down': out['m_ffn2_w_down'], 'm_ple_norm': out['m_ple_norm'], 'm_ple_w_gate': out['m_ple_w_gate'], 'm_ple_w_proj': out['m_ple_w_proj'], 'm_final_norm': out['m_final_norm'], 'v_ffn1_norm': out['v_ffn1_norm'], 'v_ffn1_w_gate': out['v_ffn1_w_gate'], 'v_ffn1_w_up': out['v_ffn1_w_up'], 'v_ffn1_w_down': out['v_ffn1_w_down'], 'v_mix_norm': out['v_mix_norm'], 'v_lru_w_in': out['v_lru_w_in'], 'v_lru_conv_w': out['v_lru_conv_w'], 'v_lru_conv_b': out['v_lru_conv_b'], 'v_lru_w_a': out['v_lru_w_a'], 'v_lru_b_a': out['v_lru_b_a'], 'v_lru_w_x': out['v_lru_w_x'], 'v_lru_b_x': out['v_lru_b_x'], 'v_lru_a_param': out['v_lru_a_param'], 'v_lru_w_out': out['v_lru_w_out'], 'v_pool_w': out['v_pool_w'], 'v_pool_b': out['v_pool_b'], 'v_pool_scale': out['v_pool_scale'], 'v_ffn2_norm': out['v_ffn2_norm'], 'v_ffn2_w_gate': out['v_ffn2_w_gate'], 'v_ffn2_w_up': out['v_ffn2_w_up'], 'v_ffn2_w_down': out['v_ffn2_w_down'], 'v_ple_norm': out['v_ple_norm'], 'v_ple_w_gate': out['v_ple_w_gate'], 'v_ple_w_proj': out['v_ple_w_proj'], 'v_final_norm': out['v_final_norm']}


def _loss(weights, diff, rest, loss_target):
    with _jax.named_scope("forward"):
        args = {**rest, TWIN_DIFF_INPUT: diff, **{k: w.astype(_WEIGHT_DTYPES[k]) for k, w in weights.items()}}
        y = _forward(args)
    with _jax.named_scope("loss_head"):
        err = _jnp.square(y.astype(_jnp.float32) - loss_target)
        return 0.5 * _jnp.sum(_jnp.mean(err, axis=-1)) if err.ndim else 0.5 * err


def _adamw(w, g, m, v):
    m = ADAM_B1 * m + (1.0 - ADAM_B1) * g
    v = ADAM_B2 * v + (1.0 - ADAM_B2) * _jnp.square(g)
    m_hat = m / (1.0 - ADAM_B1 ** ADAM_STEP)
    v_hat = v / (1.0 - ADAM_B2 ** ADAM_STEP)
    delta = -ADAM_LR * (m_hat / (_jnp.sqrt(v_hat) + ADAM_EPS) + ADAM_WD * w)
    return delta, m, v


def reference(x, p, ffn1_norm, ffn1_w_gate, ffn1_w_up, ffn1_w_down, mix_norm, lru_w_in, lru_conv_w, lru_conv_b, lru_w_a, lru_b_a, lru_w_x, lru_b_x, lru_a_param, lru_w_out, pool_w, pool_b, pool_scale, ffn2_norm, ffn2_w_gate, ffn2_w_up, ffn2_w_down, ple_norm, ple_w_gate, ple_w_proj, final_norm, loss_target, m_ffn1_norm, m_ffn1_w_gate, m_ffn1_w_up, m_ffn1_w_down, m_mix_norm, m_lru_w_in, m_lru_conv_w, m_lru_conv_b, m_lru_w_a, m_lru_b_a, m_lru_w_x, m_lru_b_x, m_lru_a_param, m_lru_w_out, m_pool_w, m_pool_b, m_pool_scale, m_ffn2_norm, m_ffn2_w_gate, m_ffn2_w_up, m_ffn2_w_down, m_ple_norm, m_ple_w_gate, m_ple_w_proj, m_final_norm, v_ffn1_norm, v_ffn1_w_gate, v_ffn1_w_up, v_ffn1_w_down, v_mix_norm, v_lru_w_in, v_lru_conv_w, v_lru_conv_b, v_lru_w_a, v_lru_b_a, v_lru_w_x, v_lru_b_x, v_lru_a_param, v_lru_w_out, v_pool_w, v_pool_b, v_pool_scale, v_ffn2_norm, v_ffn2_w_gate, v_ffn2_w_up, v_ffn2_w_down, v_ple_norm, v_ple_w_gate, v_ple_w_proj, v_final_norm):
    given = dict(x=x, p=p, ffn1_norm=ffn1_norm, ffn1_w_gate=ffn1_w_gate, ffn1_w_up=ffn1_w_up, ffn1_w_down=ffn1_w_down, mix_norm=mix_norm, lru_w_in=lru_w_in, lru_conv_w=lru_conv_w, lru_conv_b=lru_conv_b, lru_w_a=lru_w_a, lru_b_a=lru_b_a, lru_w_x=lru_w_x, lru_b_x=lru_b_x, lru_a_param=lru_a_param, lru_w_out=lru_w_out, pool_w=pool_w, pool_b=pool_b, pool_scale=pool_scale, ffn2_norm=ffn2_norm, ffn2_w_gate=ffn2_w_gate, ffn2_w_up=ffn2_w_up, ffn2_w_down=ffn2_w_down, ple_norm=ple_norm, ple_w_gate=ple_w_gate, ple_w_proj=ple_w_proj, final_norm=final_norm, loss_target=loss_target, m_ffn1_norm=m_ffn1_norm, m_ffn1_w_gate=m_ffn1_w_gate, m_ffn1_w_up=m_ffn1_w_up, m_ffn1_w_down=m_ffn1_w_down, m_mix_norm=m_mix_norm, m_lru_w_in=m_lru_w_in, m_lru_conv_w=m_lru_conv_w, m_lru_conv_b=m_lru_conv_b, m_lru_w_a=m_lru_w_a, m_lru_b_a=m_lru_b_a, m_lru_w_x=m_lru_w_x, m_lru_b_x=m_lru_b_x, m_lru_a_param=m_lru_a_param, m_lru_w_out=m_lru_w_out, m_pool_w=m_pool_w, m_pool_b=m_pool_b, m_pool_scale=m_pool_scale, m_ffn2_norm=m_ffn2_norm, m_ffn2_w_gate=m_ffn2_w_gate, m_ffn2_w_up=m_ffn2_w_up, m_ffn2_w_down=m_ffn2_w_down, m_ple_norm=m_ple_norm, m_ple_w_gate=m_ple_w_gate, m_ple_w_proj=m_ple_w_proj, m_final_norm=m_final_norm, v_ffn1_norm=v_ffn1_norm, v_ffn1_w_gate=v_ffn1_w_gate, v_ffn1_w_up=v_ffn1_w_up, v_ffn1_w_down=v_ffn1_w_down, v_mix_norm=v_mix_norm, v_lru_w_in=v_lru_w_in, v_lru_conv_w=v_lru_conv_w, v_lru_conv_b=v_lru_conv_b, v_lru_w_a=v_lru_w_a, v_lru_b_a=v_lru_b_a, v_lru_w_x=v_lru_w_x, v_lru_b_x=v_lru_b_x, v_lru_a_param=v_lru_a_param, v_lru_w_out=v_lru_w_out, v_pool_w=v_pool_w, v_pool_b=v_pool_b, v_pool_scale=v_pool_scale, v_ffn2_norm=v_ffn2_norm, v_ffn2_w_gate=v_ffn2_w_gate, v_ffn2_w_up=v_ffn2_w_up, v_ffn2_w_down=v_ffn2_w_down, v_ple_norm=v_ple_norm, v_ple_w_gate=v_ple_w_gate, v_ple_w_proj=v_ple_w_proj, v_final_norm=v_final_norm)
    weights = {n: given[n] for n in TWIN_WEIGHTS}
    shared = {n: given[n] for n in SHARED_INPUTS}
    per_example = {n: given[n] for n in ['x', 'p']}
    grad_fn = _jax.value_and_grad(_loss, argnums=(0, 1))

    def one_microbatch(ex, loss_target):
        ex = dict(ex)
        diff = ex.pop(TWIN_DIFF_INPUT)
        return grad_fn(weights, diff, {**shared, **ex}, loss_target)

    if N_MICROBATCH == 1:
        loss, (grad_w, grad_x) = one_microbatch(per_example, given["loss_target"])
    else:
        def body(carry, xs):
            loss_sum, grad_sum = carry
            l_k, (gw_k, gx_k) = one_microbatch(xs[0], xs[1])
            with _jax.named_scope("update"):
                return (loss_sum + l_k, _jax.tree.map(_jnp.add, grad_sum, gw_k)), gx_k

        init = (_jnp.zeros((), _jnp.float32), _jax.tree.map(_jnp.zeros_like, weights))
        (loss, grad_w), grad_x = _jax.lax.scan(body, init, (per_example, given["loss_target"]))
    with _jax.named_scope("update"):
        delta_w, new_m, new_v = {}, {}, {}
        for n in TWIN_WEIGHTS:
            delta_w[n], new_m[n], new_v[n] = _adamw(weights[n], grad_w[n], given["m_" + n], given["v_" + n])
    return (loss, grad_x, *[grad_w[n] for n in TWIN_WEIGHTS], *[delta_w[n] for n in TWIN_WEIGHTS],
            *[new_m[n] for n in TWIN_WEIGHTS], *[new_v[n] for n in TWIN_WEIGHTS])
```

```python
import functools

import jax
import jax.numpy as jnp
from jax import lax
from jax.experimental import pallas as pl
from jax.experimental.pallas import tpu as pltpu

F32 = jnp.float32
BF16 = jnp.bfloat16

D_MODEL = 1024
D_FF = 2816
D_RNN = 1280
DEPTH = 4
N_CHIPS = 4
FF_CHUNK = D_FF // N_CHIPS
RNN_IN_CHUNK = 2 * D_RNN // N_CHIPS
GATE_CHUNK = 640
LRU_HEADS = 16
LRU_HEAD_DIM = 80
CONV_WIDTH = 4
LRU_C = 8.0
POOL_WINDOWS = (2, 4, 8, 16)
POOL_GROUP_DIM = 256
PLE_DIM = 256
RMS_EPS = 1e-6
POOL_HALO = 16
SUBLANES = 8
LANES = 128

ADAM_LR = 0.001
ADAM_B1 = 0.9
ADAM_B2 = 0.999
ADAM_EPS = 1e-08
ADAM_WD = 0.01
ADAM_STEP = 10

VMEM_LIMIT_MB = 56
MESH_ID = pl.DeviceIdType.MESH
CHIP_FLIPS = ((1, 0), (0, 1), (1, 1))


def _cparams(semantics):
    return pltpu.CompilerParams(dimension_semantics=semantics, vmem_limit_bytes=VMEM_LIMIT_MB * 2 ** 20)


def _dot(a, b):
    return lax.dot_general(a, b, (((1,), (0,)), ((), ())), preferred_element_type=F32)


def _dot_nt(a, b):
    return lax.dot_general(a, b, (((1,), (1,)), ((), ())), preferred_element_type=F32)


def _dot_tn(a, b):
    return lax.dot_general(a, b, (((0,), (0,)), ((), ())), preferred_element_type=F32)


def _sigmoid(x):
    return 1.0 / (1.0 + jnp.exp(-x))


def _rms(hf, gamma):
    rstd = lax.rsqrt(jnp.mean(hf * hf, axis=-1, keepdims=True) + RMS_EPS)
    xhat = hf * rstd
    return xhat, rstd, xhat * gamma


def _rms_bwd(xhat, rstd, gamma, dxn):
    dxhat = dxn * gamma
    m = jnp.mean(dxhat * xhat, axis=-1, keepdims=True)
    return rstd * (dxhat - xhat * m), _rowsum8(dxn * xhat)


def _rowsum8(v):
    tm, n = v.shape
    return jnp.sum(v.reshape(tm // SUBLANES, SUBLANES, n), axis=0)


def _gelu(x):
    u = 0.7978845608028654 * (x + 0.044715 * x * x * x)
    return 0.5 * x * (1.0 + jnp.tanh(u))


def _gelu_and_grad(x):
    c = 0.7978845608028654
    u = c * (x + 0.044715 * x * x * x)
    th = jnp.tanh(u)
    g = 0.5 * x * (1.0 + th)
    dg = 0.5 * (1.0 + th) + 0.5 * x * (1.0 - th * th) * c * (1.0 + 3.0 * 0.044715 * x * x)
    return g, dg


def _softplus(z):
    e = jnp.exp(-jnp.abs(z))
    u = 1.0 + e
    log1p = jnp.where(u == 1.0, e, jnp.log(u) * e / jnp.where(u == 1.0, 1.0, u - 1.0))
    return jnp.maximum(z, 0.0) + log1p


def _neg_expm1(x):
    series = -(x + 0.5 * x * x + (1.0 / 6.0) * x * x * x)
    return jnp.where(x > -1e-2, series, 1.0 - jnp.exp(x))


def _shift_down(ext, j, halo):
    return pltpu.roll(ext, j, 0)[halo:]


def _shift_up(ext, j, tm):
    n = ext.shape[0]
    return pltpu.roll(ext, n - j, 0)[:tm]


def _scan_causal(a, b):
    tm = a.shape[0]
    rows = lax.broadcasted_iota(jnp.int32, a.shape, 0)
    s = 1
    while s < tm:
        keep = rows >= s
        a_sh = jnp.where(keep, pltpu.roll(a, s, 0), 1.0)
        b_sh = jnp.where(keep, pltpu.roll(b, s, 0), 0.0)
        b = a * b_sh + b
        a = a * a_sh
        s *= 2
    return a, b


def _scan_anticausal(c, d):
    tm = c.shape[0]
    rows = lax.broadcasted_iota(jnp.int32, c.shape, 0)
    s = 1
    while s < tm:
        keep = rows < tm - s
        c_sh = jnp.where(keep, pltpu.roll(c, tm - s, 0), 1.0)
        d_sh = jnp.where(keep, pltpu.roll(d, tm - s, 0), 0.0)
        d = d + c * d_sh
        c = c * c_sh
        s *= 2
    return c, d


def _tile(n, want):
    t = min(n, want)
    assert n % t == 0, (n, t)
    return t


def _ffn_fwd(h, gamma, wg, wu, wd, layer):
    S = h.shape[0]
    tm = _tile(S, 1024)

    def body(h_ref, g_ref, wg_ref, wu_ref, wd_ref, ho_ref, xn_ref, gg_ref, uu_ref, xn_s, acc_s):
        k = pl.program_id(1)

        @pl.when(k == 0)
        def _():
            _, _, xn = _rms(h_ref[...], g_ref[...])
            xnb = xn.astype(BF16)
            xn_s[...] = xnb
            xn_ref[...] = xnb
            acc_s[...] = jnp.zeros_like(acc_s)

        xnb = xn_s[...]
        g = _dot(xnb, wg_ref[...])
        u = _dot(xnb, wu_ref[...])
        gg_ref[...] = g.astype(BF16)
        uu_ref[...] = u.astype(BF16)
        hid = (g * _sigmoid(g)) * u
        acc_s[...] += _dot(hid.astype(BF16), wd_ref[...])

        @pl.when(k == N_CHIPS - 1)
        def _():
            ho_ref[...] = h_ref[...] + 0.5 * acc_s[...]

    return pl.pallas_call(
        body, name="ffn_fwd",
        grid=(S // tm, N_CHIPS),
        in_specs=[
            pl.BlockSpec((tm, D_MODEL), lambda t, k: (t, 0)),
            pl.BlockSpec((1, D_MODEL), lambda t, k: (0, 0)),
            pl.BlockSpec((None, None, D_MODEL, FF_CHUNK), lambda t, k: (layer, k, 0, 0)),
            pl.BlockSpec((None, None, D_MODEL, FF_CHUNK), lambda t, k: (layer, k, 0, 0)),
            pl.BlockSpec((None, FF_CHUNK, D_MODEL), lambda t, k: (layer * N_CHIPS + k, 0, 0)),
        ],
        out_specs=[
            pl.BlockSpec((tm, D_MODEL), lambda t, k: (t, 0)),
            pl.BlockSpec((tm, D_MODEL), lambda t, k: (t, 0)),
            pl.BlockSpec((None, tm, FF_CHUNK), lambda t, k: (k, t, 0)),
            pl.BlockSpec((None, tm, FF_CHUNK), lambda t, k: (k, t, 0)),
        ],
        out_shape=[
            jax.ShapeDtypeStruct((S, D_MODEL), F32),
            jax.ShapeDtypeStruct((S, D_MODEL), BF16),
            jax.ShapeDtypeStruct((N_CHIPS, S, FF_CHUNK), BF16),
            jax.ShapeDtypeStruct((N_CHIPS, S, FF_CHUNK), BF16),
        ],
        scratch_shapes=[pltpu.VMEM((tm, D_MODEL), BF16), pltpu.VMEM((tm, D_MODEL), F32)],
        compiler_params=_cparams(("arbitrary", "arbitrary")),
    )(h, gamma, wg, wu, wd.reshape(wd.shape[0] * N_CHIPS, FF_CHUNK, D_MODEL))


def _ffn_bwd(xn, dout, gg, uu, wg, wu, wd, layer):
    S = xn.shape[0]
    tm = _tile(S, 512)

    def body(xn_ref, do_ref, gg_ref, uu_ref, wg_ref, wu_ref, wd_ref, dwg_ref, dwu_ref, dwd_ref, slab_ref):
        t = pl.program_id(1)

        @pl.when(t == 0)
        def _():
            dwg_ref[...] = jnp.zeros_like(dwg_ref)
            dwu_ref[...] = jnp.zeros_like(dwu_ref)
            dwd_ref[...] = jnp.zeros_like(dwd_ref)

        xnb = xn_ref[...]
        dob = (0.5 * do_ref[...]).astype(BF16)
        g = gg_ref[...].astype(F32)
        u = uu_ref[...].astype(F32)
        s = _sigmoid(g)
        sil = g * s
        dhid = _dot_nt(dob, wd_ref[...])
        dwd_ref[...] += _dot_tn((sil * u).astype(BF16), dob)
        du = (dhid * sil).astype(BF16)
        dg = (dhid * u * (s * (1.0 + g * (1.0 - s)))).astype(BF16)
        dwg_ref[...] += _dot_tn(xnb, dg)
        dwu_ref[...] += _dot_tn(xnb, du)
        slab_ref[...] = (_dot_nt(dg, wg_ref[...]) + _dot_nt(du, wu_ref[...])).astype(BF16)

    return pl.pallas_call(
        body, name="ffn_bwd",
        grid=(N_CHIPS, S // tm),
        in_specs=[
            pl.BlockSpec((tm, D_MODEL), lambda k, t: (t, 0)),
            pl.BlockSpec((tm, D_MODEL), lambda k, t: (t, 0)),
            pl.BlockSpec((None, tm, FF_CHUNK), lambda k, t: (k, t, 0)),
            pl.BlockSpec((None, tm, FF_CHUNK), lambda k, t: (k, t, 0)),
            pl.BlockSpec((None, None, D_MODEL, FF_CHUNK), lambda k, t: (layer, k, 0, 0)),
            pl.BlockSpec((None, None, D_MODEL, FF_CHUNK), lambda k, t: (layer, k, 0, 0)),
            pl.BlockSpec((None, FF_CHUNK, D_MODEL), lambda k, t: (layer * N_CHIPS + k, 0, 0)),
        ],
        out_specs=[
            pl.BlockSpec((None, D_MODEL, FF_CHUNK), lambda k, t: (k, 0, 0)),
            pl.BlockSpec((None, D_MODEL, FF_CHUNK), lambda k, t: (k, 0, 0)),
            pl.BlockSpec((None, FF_CHUNK, D_MODEL), lambda k, t: (k, 0, 0)),
            pl.BlockSpec((None, tm, D_MODEL), lambda k, t: (k, t, 0)),
        ],
        out_shape=[
            jax.ShapeDtypeStruct((N_CHIPS, D_MODEL, FF_CHUNK), F32),
            jax.ShapeDtypeStruct((N_CHIPS, D_MODEL, FF_CHUNK), F32),
            jax.ShapeDtypeStruct((N_CHIPS, FF_CHUNK, D_MODEL), F32),
            jax.ShapeDtypeStruct((N_CHIPS, S, D_MODEL), BF16),
        ],
        compiler_params=_cparams(("arbitrary", "arbitrary")),
    )(xn, dout, gg, uu, wg, wu, wd.reshape(wd.shape[0] * N_CHIPS, FF_CHUNK, D_MODEL))


def _norm_bwd_combine(dres, slabs, h, gamma):
    S = h.shape[0]
    tm = _tile(S, 512)
    nslab = slabs.shape[0]

    def body(dres_ref, slab_ref, h_ref, g_ref, dh_ref, dgam_ref):
        t = pl.program_id(0)
        dxn = slab_ref[0].astype(F32)
        for i in range(1, nslab):
            dxn = dxn + slab_ref[i].astype(F32)
        xhat, rstd, _ = _rms(h_ref[...], g_ref[...])
        dhn, dgam = _rms_bwd(xhat, rstd, g_ref[...], dxn)
        dh_ref[...] = dres_ref[...] + dhn

        @pl.when(t == 0)
        def _():
            dgam_ref[...] = jnp.zeros_like(dgam_ref)

        dgam_ref[...] += dgam

    return pl.pallas_call(
        body, name="norm_bwd_combine",
        grid=(S // tm,),
        in_specs=[
            pl.BlockSpec((tm, D_MODEL), lambda t: (t, 0)),
            pl.BlockSpec((nslab, tm, D_MODEL), lambda t: (0, t, 0)),
            pl.BlockSpec((tm, D_MODEL), lambda t: (t, 0)),
            pl.BlockSpec((1, D_MODEL), lambda t: (0, 0)),
        ],
        out_specs=[
            pl.BlockSpec((tm, D_MODEL), lambda t: (t, 0)),
            pl.BlockSpec((SUBLANES, D_MODEL), lambda t: (0, 0)),
        ],
        out_shape=[jax.ShapeDtypeStruct((S, D_MODEL), F32), jax.ShapeDtypeStruct((SUBLANES, D_MODEL), F32)],
        compiler_params=_cparams(("arbitrary",)),
    )(dres, slabs, h, gamma)


def _xt_dy(name, x, dy, nchunk, kb, nb, x_by_chunk, y_by_chunk):
    S = x.shape[0]
    tm = _tile(S, 512)

    def body(x_ref, dy_ref, o_ref):
        @pl.when(pl.program_id(1) == 0)
        def _():
            o_ref[...] = jnp.zeros_like(o_ref)

        o_ref[...] += _dot_tn(x_ref[...].astype(BF16), dy_ref[...].astype(BF16))

    return pl.pallas_call(
        body, name=name,
        grid=(nchunk, S // tm),
        in_specs=[
            pl.BlockSpec((tm, kb), (lambda c, t: (t, c)) if x_by_chunk else (lambda c, t: (t, 0))),
            pl.BlockSpec((tm, nb), (lambda c, t: (t, c)) if y_by_chunk else (lambda c, t: (t, 0))),
        ],
        out_specs=pl.BlockSpec((None, kb, nb), lambda c, t: (c, 0, 0)),
        out_shape=jax.ShapeDtypeStruct((nchunk, kb, nb), F32),
        compiler_params=_cparams(("arbitrary", "arbitrary")),
    )(x, dy)


def _lru_gates(xc, wbd_ref, ba, bx, apar):
    xcb = xc.astype(BF16)
    r_parts, ig_parts = [], []
    for q in range(D_RNN // GATE_CHUNK):
        lo, hi = q * GATE_CHUNK, (q + 1) * GATE_CHUNK
        pre = _dot(xcb[:, lo:hi], wbd_ref[q])
        r_parts.append(_sigmoid(pre[:, :GATE_CHUNK] + ba[:, lo:hi]))
        ig_parts.append(_sigmoid(pre[:, GATE_CHUNK:] + bx[:, lo:hi]))
    r = jnp.concatenate(r_parts, axis=1)
    ig = jnp.concatenate(ig_parts, axis=1)
    sp = LRU_C * _softplus(-apar)
    log_a = -(r * sp)
    a = jnp.exp(log_a)
    mult = jnp.sqrt(_neg_expm1(2.0 * log_a))
    return r, ig, a, mult, sp


def _conv_causal(xb, tail, cw_ref, cb):
    ext = jnp.concatenate([tail, xb], axis=0)
    xc = cb + cw_ref[CONV_WIDTH - 1:CONV_WIDTH, :] * xb
    for j in range(1, CONV_WIDTH):
        xc = xc + cw_ref[CONV_WIDTH - 1 - j:CONV_WIDTH - j, :] * _shift_down(ext, j, SUBLANES)
    return xc, ext


def _lru_fwd(h, gamma, win, layer, convw, convb, wbd, ba, bx, apar, wout):
    S = h.shape[0]
    tm = _tile(S, 256)

    def body(h_ref, g_ref, win_ref, cw_ref, cb_ref, wbd_ref, ba_ref, bx_ref, ap_ref, wout_ref,
             ho_ref, xn_ref, z_ref, hs_ref, tail_s, carry_s):
        @pl.when(pl.program_id(0) == 0)
        def _():
            tail_s[...] = jnp.zeros_like(tail_s)
            carry_s[...] = jnp.zeros_like(carry_s)

        hf = h_ref[...]
        _, _, xn = _rms(hf, g_ref[...])
        xnb = xn.astype(BF16)
        xn_ref[...] = xnb
        for k in range(N_CHIPS):
            z_ref[:, k * RNN_IN_CHUNK:(k + 1) * RNN_IN_CHUNK] = _dot(xnb, win_ref[k])
        gate = z_ref[:, :D_RNN]
        xb = z_ref[:, D_RNN:]
        xc, _ = _conv_causal(xb, tail_s[...], cw_ref, cb_ref[...])
        tail_s[...] = xb[tm - SUBLANES:, :]
        _, ig, a, mult, _ = _lru_gates(xc, wbd_ref, ba_ref[...], bx_ref[...], ap_ref[...])
        big_a, big_b = _scan_causal(a, mult * (ig * xc))
        hs = big_a * carry_s[SUBLANES - 1:SUBLANES, :] + big_b
        hs_ref[...] = hs
        carry_s[...] = hs[tm - SUBLANES:, :]
        y = hs * _gelu(gate)
        ho_ref[...] = hf + _dot(y.astype(BF16), wout_ref[...])

    row = lambda n: pl.BlockSpec((1, n), lambda t: (0, 0))
    return pl.pallas_call(
        body, name="lru_fwd",
        grid=(S // tm,),
        in_specs=[
            pl.BlockSpec((tm, D_MODEL), lambda t: (t, 0)),
            row(D_MODEL),
            pl.BlockSpec((None, N_CHIPS, D_MODEL, RNN_IN_CHUNK), lambda t: (layer, 0, 0, 0)),
            pl.BlockSpec((CONV_WIDTH, D_RNN), lambda t: (0, 0)),
            row(D_RNN),
            pl.BlockSpec((D_RNN // GATE_CHUNK, GATE_CHUNK, 2 * GATE_CHUNK), lambda t: (0, 0, 0)),
            row(D_RNN), row(D_RNN), row(D_RNN),
            pl.BlockSpec((None, D_RNN, D_MODEL), lambda t: (layer, 0, 0)),
        ],
        out_specs=[
            pl.BlockSpec((tm, D_MODEL), lambda t: (t, 0)),
            pl.BlockSpec((tm, D_MODEL), lambda t: (t, 0)),
            pl.BlockSpec((tm, 2 * D_RNN), lambda t: (t, 0)),
            pl.BlockSpec((tm, D_RNN), lambda t: (t, 0)),
        ],
        out_shape=[
            jax.ShapeDtypeStruct((S, D_MODEL), F32),
            jax.ShapeDtypeStruct((S, D_MODEL), BF16),
            jax.ShapeDtypeStruct((S, 2 * D_RNN), F32),
            jax.ShapeDtypeStruct((S, D_RNN), F32),
        ],
        scratch_shapes=[pltpu.VMEM((SUBLANES, D_RNN), F32), pltpu.VMEM((SUBLANES, D_RNN), F32)],
        compiler_params=_cparams(("arbitrary",)),
    )(h, gamma, win, convw, convb, wbd, ba, bx, apar, wout)


def _lru_bwd_seq(dout, z, hs, convw, convb, wbd, ba, bx, apar, wout, layer):
    S = dout.shape[0]
    tm = _tile(S, 256)
    nt = S // tm
    per8 = tm // SUBLANES
    rev = lambda i: nt - 1 - i
    prev8 = lambda i: jnp.maximum(rev(i) * per8 - 1, 0)

    def body(do_ref, z_ref, hs_ref, ztail_ref, hstail_ref, cw_ref, cb_ref, wbd_ref, ba_ref, bx_ref, ap_ref, wout_ref,
             dz_ref, dpre_ref, xc_ref, y_ref, dcw_ref, vec_ref, a_first_s, g_first_s, dxc_head_s):
        i = pl.program_id(0)
        first_in_time = rev(i) == 0

        @pl.when(i == 0)
        def _():
            a_first_s[...] = jnp.zeros_like(a_first_s)
            g_first_s[...] = jnp.zeros_like(g_first_s)
            dxc_head_s[...] = jnp.zeros_like(dxc_head_s)
            dcw_ref[...] = jnp.zeros_like(dcw_ref)
            vec_ref[...] = jnp.zeros_like(vec_ref)

        gate = z_ref[:, :D_RNN]
        xb = z_ref[:, D_RNN:]
        hist = jnp.where(first_in_time, 0.0, 1.0)
        xc, xext = _conv_causal(xb, ztail_ref[:, D_RNN:] * hist, cw_ref, cb_ref[...])
        r, ig, a, mult, sp = _lru_gates(xc, wbd_ref, ba_ref[...], bx_ref[...], ap_ref[...])
        hs = hs_ref[...]
        gel, dgel = _gelu_and_grad(gate)
        y = hs * gel
        y_ref[...] = y.astype(BF16)
        xc_ref[...] = xc.astype(BF16)

        dy = _dot_nt(do_ref[...].astype(BF16), wout_ref[...])
        dhs = dy * gel
        dgate = dy * hs * dgel

        coef = _shift_up(jnp.concatenate([a, a_first_s[...]], axis=0), 1, tm)
        big_c, big_d = _scan_anticausal(coef, dhs)
        g = big_d + big_c * g_first_s[0:1, :]
        g_first_s[...] = g[:SUBLANES, :]
        a_first_s[...] = a[:SUBLANES, :]

        hs_prev = _shift_down(jnp.concatenate([hstail_ref[...] * hist, hs], axis=0), 1, SUBLANES)
        da = g * hs_prev
        dmult = g * ig * xc
        dig = g * mult * xc
        dxc = g * mult * ig
        dlog_a = da * a - dmult * (a * a) / mult
        dr = -(dlog_a * sp)
        dpre_a = dr * r * (1.0 - r)
        dpre_x = dig * ig * (1.0 - ig)
        d_apar = dlog_a * r * (LRU_C * _sigmoid(-ap_ref[...]))

        for q in range(D_RNN // GATE_CHUNK):
            lo, hi = q * GATE_CHUNK, (q + 1) * GATE_CHUNK
            dpre_q = jnp.concatenate([dpre_a[:, lo:hi], dpre_x[:, lo:hi]], axis=1).astype(BF16)
            dpre_ref[:, 2 * lo:2 * hi] = dpre_q
            dxc_q = _dot_nt(dpre_q, wbd_ref[q])
            if q == 0:
                dxc_parts = [dxc_q]
            else:
                dxc_parts.append(dxc_q)
        dxc = dxc + jnp.concatenate(dxc_parts, axis=1)

        dext = jnp.concatenate([dxc, dxc_head_s[...]], axis=0)
        dxb = cw_ref[CONV_WIDTH - 1:CONV_WIDTH, :] * dxc
        for j in range(1, CONV_WIDTH):
            dxb = dxb + cw_ref[CONV_WIDTH - 1 - j:CONV_WIDTH - j, :] * _shift_up(dext, j, tm)
        dxc_head_s[...] = dxc[:SUBLANES, :]
        dz_ref[:, :D_RNN] = dgate.astype(BF16)
        dz_ref[:, D_RNN:] = dxb.astype(BF16)

        dcw_ref[CONV_WIDTH - 1] += _rowsum8(dxc * xb)
        for j in range(1, CONV_WIDTH):
            dcw_ref[CONV_WIDTH - 1 - j] += _rowsum8(dxc * _shift_down(xext, j, SUBLANES))
        vec_ref[0] += _rowsum8(dxc)
        vec_ref[1] += _rowsum8(dpre_a)
        vec_ref[2] += _rowsum8(dpre_x)
        vec_ref[3] += _rowsum8(d_apar)

    row = lambda n: pl.BlockSpec((1, n), lambda i: (0, 0))
    return pl.pallas_call(
        body, name="lru_bwd_seq",
        grid=(nt,),
        in_specs=[
            pl.BlockSpec((tm, D_MODEL), lambda i: (rev(i), 0)),
            pl.BlockSpec((tm, 2 * D_RNN), lambda i: (rev(i), 0)),
            pl.BlockSpec((tm, D_RNN), lambda i: (rev(i), 0)),
            pl.BlockSpec((SUBLANES, 2 * D_RNN), lambda i: (prev8(i), 0)),
            pl.BlockSpec((SUBLANES, D_RNN), lambda i: (prev8(i), 0)),
            pl.BlockSpec((CONV_WIDTH, D_RNN), lambda i: (0, 0)),
            row(D_RNN),
            pl.BlockSpec((D_RNN // GATE_CHUNK, GATE_CHUNK, 2 * GATE_CHUNK), lambda i: (0, 0, 0)),
            row(D_RNN), row(D_RNN), row(D_RNN),
            pl.BlockSpec((None, D_RNN, D_MODEL), lambda i: (layer, 0, 0)),
        ],
        out_specs=[
            pl.BlockSpec((tm, 2 * D_RNN), lambda i: (rev(i), 0)),
            pl.BlockSpec((tm, 2 * D_RNN), lambda i: (rev(i), 0)),
            pl.BlockSpec((tm, D_RNN), lambda i: (rev(i), 0)),
            pl.BlockSpec((tm, D_RNN), lambda i: (rev(i), 0)),
            pl.BlockSpec((CONV_WIDTH, SUBLANES, D_RNN), lambda i: (0, 0, 0)),
            pl.BlockSpec((4, SUBLANES, D_RNN), lambda i: (0, 0, 0)),
        ],
        out_shape=[
            jax.ShapeDtypeStruct((S, 2 * D_RNN), BF16),
            jax.ShapeDtypeStruct((S, 2 * D_RNN), BF16),
            jax.ShapeDtypeStruct((S, D_RNN), BF16),
            jax.ShapeDtypeStruct((S, D_RNN), BF16),
            jax.ShapeDtypeStruct((CONV_WIDTH, SUBLANES, D_RNN), F32),
            jax.ShapeDtypeStruct((4, SUBLANES, D_RNN), F32),
        ],
        scratch_shapes=[pltpu.VMEM((SUBLANES, D_RNN), F32)] * 3,
        compiler_params=_cparams(("arbitrary",)),
    )(dout, z, hs, z, hs, convw, convb, wbd, ba, bx, apar, wout)


def _lru_bwd_in(dz, h, gamma, dres, win, layer):
    S = h.shape[0]
    tm = _tile(S, 512)

    def body(dz_ref, h_ref, g_ref, dres_ref, win_ref, dh_ref, dgam_ref):
        dxn = _dot_nt(dz_ref[:, :RNN_IN_CHUNK], win_ref[0])
        for k in range(1, N_CHIPS):
            dxn = dxn + _dot_nt(dz_ref[:, k * RNN_IN_CHUNK:(k + 1) * RNN_IN_CHUNK], win_ref[k])
        xhat, rstd, _ = _rms(h_ref[...], g_ref[...])
        dhn, dgam = _rms_bwd(xhat, rstd, g_ref[...], dxn)
        dh_ref[...] = dres_ref[...] + dhn

        @pl.when(pl.program_id(0) == 0)
        def _():
            dgam_ref[...] = jnp.zeros_like(dgam_ref)

        dgam_ref[...] += dgam

    return pl.pallas_call(
        body, name="lru_bwd_in",
        grid=(S // tm,),
        in_specs=[
            pl.BlockSpec((tm, 2 * D_RNN), lambda t: (t, 0)),
            pl.BlockSpec((tm, D_MODEL), lambda t: (t, 0)),
            pl.BlockSpec((1, D_MODEL), lambda t: (0, 0)),
            pl.BlockSpec((tm, D_MODEL), lambda t: (t, 0)),
            pl.BlockSpec((None, N_CHIPS, D_MODEL, RNN_IN_CHUNK), lambda t: (layer, 0, 0, 0)),
        ],
        out_specs=[
            pl.BlockSpec((tm, D_MODEL), lambda t: (t, 0)),
            pl.BlockSpec((SUBLANES, D_MODEL), lambda t: (0, 0)),
        ],
        out_shape=[jax.ShapeDtypeStruct((S, D_MODEL), F32), jax.ShapeDtypeStruct((SUBLANES, D_MODEL), F32)],
        compiler_params=_cparams(("arbitrary",)),
    )(dz, h, gamma, dres, win)


def _pool_inv_count(t_index, tm):
    rows = (lax.broadcasted_iota(jnp.int32, (tm, D_MODEL), 0) + t_index * tm + 1).astype(F32)
    col = lax.broadcasted_iota(jnp.int32, (tm, D_MODEL), 1)
    win = jnp.where(col < POOL_GROUP_DIM, float(POOL_WINDOWS[0]),
                    jnp.where(col < 2 * POOL_GROUP_DIM, float(POOL_WINDOWS[1]),
                              jnp.where(col < 3 * POOL_GROUP_DIM, float(POOL_WINDOWS[2]), float(POOL_WINDOWS[3]))))
    return 1.0 / jnp.minimum(rows, win)


def _window_sums(ext, shift, take):
    gd = POOL_GROUP_DIM
    s2 = ext + shift(ext, 1)
    s4 = s2[:, gd:] + shift(s2[:, gd:], 2)
    s8 = s4[:, gd:] + shift(s4[:, gd:], 4)
    s16 = s8[:, gd:] + shift(s8[:, gd:], 8)
    return jnp.concatenate([take(s2[:, :gd]), take(s4[:, :gd]), take(s8[:, :gd]), take(s16)], axis=1)


def _pool_fwd(h, gamma, pw, pb, pscale, layer):
    S = h.shape[0]
    tm = _tile(S, 512)

    def body(h_ref, g_ref, pw_ref, pb_ref, ps_ref, ho_ref, u_ref, tail_s):
        t = pl.program_id(0)

        @pl.when(t == 0)
        def _():
            tail_s[...] = jnp.zeros_like(tail_s)

        hf = h_ref[...]
        _, _, hn = _rms(hf, g_ref[...])
        ext = jnp.concatenate([tail_s[...], hn], axis=0)
        tail_s[...] = hn[tm - POOL_HALO:, :]
        sums = _window_sums(ext, lambda v, j: pltpu.roll(v, j, 0), lambda v: v[POOL_HALO:])
        ub = (sums * _pool_inv_count(t, tm) - hn).astype(BF16)
        u_ref[...] = ub
        ys = [_dot(ub[:, g * POOL_GROUP_DIM:(g + 1) * POOL_GROUP_DIM], pw_ref[g]) for g in range(len(POOL_WINDOWS))]
        y = jnp.concatenate(ys, axis=1)
        ho_ref[...] = hf + (y + pb_ref[...]) * ps_ref[...]

    row = pl.BlockSpec((1, D_MODEL), lambda t: (0, 0))
    return pl.pallas_call(
        body, name="pool_fwd",
        grid=(S // tm,),
        in_specs=[
            pl.BlockSpec((tm, D_MODEL), lambda t: (t, 0)), row,
            pl.BlockSpec((None, len(POOL_WINDOWS), POOL_GROUP_DIM, POOL_GROUP_DIM), lambda t: (layer, 0, 0, 0)),
            row, row,
        ],
        out_specs=[pl.BlockSpec((tm, D_MODEL), lambda t: (t, 0)), pl.BlockSpec((tm, D_MODEL), lambda t: (t, 0))],
        out_shape=[jax.ShapeDtypeStruct((S, D_MODEL), F32), jax.ShapeDtypeStruct((S, D_MODEL), BF16)],
        scratch_shapes=[pltpu.VMEM((POOL_HALO, D_MODEL), F32)],
        compiler_params=_cparams(("arbitrary",)),
    )(h, gamma, pw, pb, pscale)


def _pool_bwd(dout, h, u, gamma, pw, pb, pscale, layer):
    S = h.shape[0]
    tm = _tile(S, 512)
    nt = S // tm
    rev = lambda i: nt - 1 - i
    ngroup = len(POOL_WINDOWS)

    def body(do_ref, h_ref, u_ref, g_ref, pw_ref, pb_ref, ps_ref, dh_ref, dpre_ref, vec_ref, head_s):
        i = pl.program_id(0)

        @pl.when(i == 0)
        def _():
            head_s[...] = jnp.zeros_like(head_s)
            vec_ref[...] = jnp.zeros_like(vec_ref)

        do = do_ref[...]
        ub = u_ref[...]
        gsl = lambda v, g: v[:, g * POOL_GROUP_DIM:(g + 1) * POOL_GROUP_DIM]
        y = jnp.concatenate([_dot(gsl(ub, g), pw_ref[g]) for g in range(ngroup)], axis=1)
        dpre = do * ps_ref[...]
        dpb = dpre.astype(BF16)
        dpre_ref[...] = dpb
        du = jnp.concatenate([_dot_nt(gsl(dpb, g), pw_ref[g]) for g in range(ngroup)], axis=1)
        v = du * _pool_inv_count(rev(i), tm)
        ext = jnp.concatenate([v, head_s[...]], axis=0)
        head_s[...] = v[:POOL_HALO, :]
        n = tm + POOL_HALO
        dhn = _window_sums(ext, lambda w, j: pltpu.roll(w, n - j, 0), lambda w: w[:tm]) - du
        xhat, rstd, _ = _rms(h_ref[...], g_ref[...])
        dh_in, dgam = _rms_bwd(xhat, rstd, g_ref[...], dhn)
        dh_ref[...] = do + dh_in
        vec_ref[0] += dgam
        vec_ref[1] += _rowsum8(dpre)
        vec_ref[2] += _rowsum8(do * (y + pb_ref[...]))

    row = pl.BlockSpec((1, D_MODEL), lambda i: (0, 0))
    tile = pl.BlockSpec((tm, D_MODEL), lambda i: (rev(i), 0))
    return pl.pallas_call(
        body, name="pool_bwd",
        grid=(nt,),
        in_specs=[tile, tile, tile, row,
                  pl.BlockSpec((None, ngroup, POOL_GROUP_DIM, POOL_GROUP_DIM), lambda i: (layer, 0, 0, 0)), row, row],
        out_specs=[tile, tile, pl.BlockSpec((3, SUBLANES, D_MODEL), lambda i: (0, 0, 0))],
        out_shape=[jax.ShapeDtypeStruct((S, D_MODEL), F32), jax.ShapeDtypeStruct((S, D_MODEL), BF16),
                   jax.ShapeDtypeStruct((3, SUBLANES, D_MODEL), F32)],
        scratch_shapes=[pltpu.VMEM((POOL_HALO, D_MODEL), F32)],
        compiler_params=_cparams(("arbitrary",)),
    )(dout, h, u, gamma, pw, pb, pscale)


def _ple_parts(hf, gamma, p_tile, wgate_ref, wproj_ref):
    xhat, rstd, xn = _rms(hf, gamma)
    xnb = xn.astype(BF16)
    gate = _sigmoid(_dot(xnb, wgate_ref[...]))
    pb = p_tile.astype(BF16)
    proj = jnp.concatenate([_dot(pb, wproj_ref[k]) for k in range(N_CHIPS)], axis=1)
    return xhat, rstd, xnb, gate, proj


def _ple_fwd(h, gamma, p_l, wgate, wproj, layer):
    S = h.shape[0]
    tm = _tile(S, 512)

    def body(h_ref, g_ref, p_ref, wgate_ref, wproj_ref, ho_ref):
        hf = h_ref[...]
        _, _, _, gate, proj = _ple_parts(hf, g_ref[...], p_ref[...], wgate_ref, wproj_ref)
        ho_ref[...] = hf + gate * proj

    return pl.pallas_call(
        body, name="ple_fwd",
        grid=(S // tm,),
        in_specs=[
            pl.BlockSpec((tm, D_MODEL), lambda t: (t, 0)),
            pl.BlockSpec((1, D_MODEL), lambda t: (0, 0)),
            pl.BlockSpec((tm, PLE_DIM), lambda t: (t, 0)),
            pl.BlockSpec((None, D_MODEL, D_MODEL), lambda t: (layer, 0, 0)),
            pl.BlockSpec((None, N_CHIPS, PLE_DIM, PLE_DIM), lambda t: (layer, 0, 0, 0)),
        ],
        out_specs=pl.BlockSpec((tm, D_MODEL), lambda t: (t, 0)),
        out_shape=jax.ShapeDtypeStruct((S, D_MODEL), F32),
        compiler_params=_cparams(("arbitrary",)),
    )(h, gamma, p_l, wgate, wproj)


def _ple_bwd(dout, h, gamma, p_l, wgate, wproj, layer):
    S = h.shape[0]
    tm = _tile(S, 512)

    def body(do_ref, h_ref, g_ref, p_ref, wgate_ref, wproj_ref, dh_ref, xn_ref, dpre_ref, dproj_ref, dgam_ref):
        do = do_ref[...]
        xhat, rstd, xnb, gate, proj = _ple_parts(h_ref[...], g_ref[...], p_ref[...], wgate_ref, wproj_ref)
        xn_ref[...] = xnb
        dproj_ref[...] = (do * gate).astype(BF16)
        dpre = (do * proj * gate * (1.0 - gate)).astype(BF16)
        dpre_ref[...] = dpre
        dhn, dgam = _rms_bwd(xhat, rstd, g_ref[...], _dot_nt(dpre, wgate_ref[...]))
        dh_ref[...] = do + dhn

        @pl.when(pl.program_id(0) == 0)
        def _():
            dgam_ref[...] = jnp.zeros_like(dgam_ref)

        dgam_ref[...] += dgam

    tile = pl.BlockSpec((tm, D_MODEL), lambda t: (t, 0))
    return pl.pallas_call(
        body, name="ple_bwd",
        grid=(S // tm,),
        in_specs=[
            tile, tile,
            pl.BlockSpec((1, D_MODEL), lambda t: (0, 0)),
            pl.BlockSpec((tm, PLE_DIM), lambda t: (t, 0)),
            pl.BlockSpec((None, D_MODEL, D_MODEL), lambda t: (layer, 0, 0)),
            pl.BlockSpec((None, N_CHIPS, PLE_DIM, PLE_DIM), lambda t: (layer, 0, 0, 0)),
        ],
        out_specs=[tile, tile, tile, tile, pl.BlockSpec((SUBLANES, D_MODEL), lambda t: (0, 0))],
        out_shape=[jax.ShapeDtypeStruct((S, D_MODEL), F32), jax.ShapeDtypeStruct((S, D_MODEL), BF16),
                   jax.ShapeDtypeStruct((S, D_MODEL), BF16), jax.ShapeDtypeStruct((S, D_MODEL), BF16),
                   jax.ShapeDtypeStruct((SUBLANES, D_MODEL), F32)],
        compiler_params=_cparams(("arbitrary",)),
    )(dout, h, gamma, p_l, wgate, wproj)


def _final(h, gamma, target):
    S = h.shape[0]
    tm = _tile(S, 512)

    def body(h_ref, g_ref, tgt_ref, dh_ref, dgam_ref, loss_ref):
        xhat, rstd, y = _rms(h_ref[...], g_ref[...])
        err = y - tgt_ref[...]
        dy = err * (1.0 / D_MODEL)
        dhn, dgam = _rms_bwd(xhat, rstd, g_ref[...], dy)
        dh_ref[...] = dhn
        sq = _rowsum8(err * err)
        part = sq[:, :LANES]
        for j in range(1, D_MODEL // LANES):
            part = part + sq[:, j * LANES:(j + 1) * LANES]

        @pl.when(pl.program_id(0) == 0)
        def _():
            dgam_ref[...] = jnp.zeros_like(dgam_ref)
            loss_ref[...] = jnp.zeros_like(loss_ref)

        dgam_ref[...] += dgam
        loss_ref[...] += part * (0.5 / D_MODEL)

    tile = pl.BlockSpec((tm, D_MODEL), lambda t: (t, 0))
    return pl.pallas_call(
        body, name="final_loss",
        grid=(S // tm,),
        in_specs=[tile, pl.BlockSpec((1, D_MODEL), lambda t: (0, 0)), tile],
        out_specs=[tile, pl.BlockSpec((SUBLANES, D_MODEL), lambda t: (0, 0)),
                   pl.BlockSpec((SUBLANES, LANES), lambda t: (0, 0))],
        out_shape=[jax.ShapeDtypeStruct((S, D_MODEL), F32), jax.ShapeDtypeStruct((SUBLANES, D_MODEL), F32),
                   jax.ShapeDtypeStruct((SUBLANES, LANES), F32)],
        compiler_params=_cparams(("arbitrary",)),
    )(h, gamma, target)


def _mesh_pos():
    return lax.axis_index("x"), lax.axis_index("y"), lax.axis_index("c")


def _other_chip(x, y, j):
    fx, fy = CHIP_FLIPS[j]
    return (1 - x if fx else x), (1 - y if fy else y)


def _any_specs(n):
    return [pl.BlockSpec(memory_space=pl.ANY)] * n


def _gather_weights(arrs):
    n = len(arrs)
    nflip = len(CHIP_FLIPS)

    def body(*refs):
        ins, outs = refs[:n], refs[n:2 * n]
        send1, recv1, send2, recv2, local_sems = refs[2 * n:]
        x, y, c = _mesh_pos()
        k = 2 * x + y

        def half(a, cc):
            hrows = arrs[a].shape[1] // 2
            return pl.ds(cc * hrows, hrows)

        def chip_index(j):
            px, py = _other_chip(x, y, j)
            return 2 * px + py

        def ici_copy(a, j):
            px, py = _other_chip(x, y, j)
            return pltpu.make_async_remote_copy(
                src_ref=ins[a].at[:, half(a, c), :], dst_ref=outs[a].at[:, k, half(a, c), :],
                send_sem=send1.at[a * nflip + j], recv_sem=recv1.at[a * nflip + j],
                device_id=(px, py, c), device_id_type=MESH_ID)

        def ici_landed(a, j):
            px, py = _other_chip(x, y, j)
            return pltpu.make_async_remote_copy(
                src_ref=ins[a].at[:, half(a, c), :], dst_ref=outs[a].at[:, chip_index(j), half(a, c), :],
                send_sem=send1.at[a * nflip + j], recv_sem=recv1.at[a * nflip + j],
                device_id=(px, py, c), device_id_type=MESH_ID)

        def d2d_copy(a, j, cc):
            blk = outs[a].at[:, chip_index(j), half(a, cc), :]
            return pltpu.make_async_remote_copy(
                src_ref=blk, dst_ref=blk, send_sem=send2.at[a * nflip + j], recv_sem=recv2.at[a * nflip + j],
                device_id=(x, y, 1 - c), device_id_type=MESH_ID)

        locals_ = [pltpu.make_async_copy(ins[a], outs[a].at[:, k], local_sems.at[a]) for a in range(n)]
        for cp in locals_:
            cp.start()
        for a in range(n):
            for j in range(nflip):
                ici_copy(a, j).start()
        for j in range(nflip):
            for a in range(n):
                ici_landed(a, j).wait_recv()
                d2d_copy(a, j, c).start()
        for a in range(n):
            for j in range(nflip):
                ici_copy(a, j).wait_send()
                d2d_copy(a, j, c).wait_send()
                d2d_copy(a, j, 1 - c).wait_recv()
        for cp in locals_:
            cp.wait()

    outs = pl.pallas_call(
        body, name="gather_weights",
        in_specs=_any_specs(n), out_specs=_any_specs(n),
        out_shape=[jax.ShapeDtypeStruct((a.shape[0], N_CHIPS) + a.shape[1:], a.dtype) for a in arrs],
        scratch_shapes=[pltpu.SemaphoreType.DMA((n * nflip,))] * 4 + [pltpu.SemaphoreType.DMA((n,))],
    )(*arrs)
    return list(outs)


def _sibling_exchange_halves(arrs):
    n = len(arrs)

    def body(*refs):
        ins, outs = refs[:n], refs[n:2 * n]
        send, recv = refs[2 * n:]
        x, y, c = _mesh_pos()
        copies = []
        for a in range(n):
            hrows = arrs[a].shape[1] // 2
            copies.append(pltpu.make_async_remote_copy(
                src_ref=ins[a].at[:, pl.ds((1 - c) * hrows, hrows), :], dst_ref=outs[a],
                send_sem=send.at[a], recv_sem=recv.at[a], device_id=(x, y, 1 - c), device_id_type=MESH_ID))
        for cp in copies:
            cp.start()
        for cp in copies:
            cp.wait()

    outs = pl.pallas_call(
        body, name="grad_sibling_exchange",
        in_specs=_any_specs(n), out_specs=_any_specs(n),
        out_shape=[jax.ShapeDtypeStruct((a.shape[0], a.shape[1] // 2, a.shape[2]), a.dtype) for a in arrs],
        scratch_shapes=[pltpu.SemaphoreType.DMA((n,))] * 2,
    )(*arrs)
    return list(outs)


def _chip_exchange(arrs):
    n = len(arrs)
    nflip = len(CHIP_FLIPS)

    def body(*refs):
        ins, outs = refs[:n], refs[n:2 * n]
        send, recv = refs[2 * n:]
        x, y, c = _mesh_pos()
        copies = []
        for a in range(n):
            for j in range(nflip):
                px, py = _other_chip(x, y, j)
                copies.append(pltpu.make_async_remote_copy(
                    src_ref=ins[a].at[2 * px + py], dst_ref=outs[a].at[j],
                    send_sem=send.at[a * nflip + j], recv_sem=recv.at[a * nflip + j],
                    device_id=(px, py, c), device_id_type=MESH_ID))
        for cp in copies:
            cp.start()
        for cp in copies:
            cp.wait()

    outs = pl.pallas_call(
        body, name="grad_chip_exchange",
        in_specs=_any_specs(n), out_specs=_any_specs(n),
        out_shape=[jax.ShapeDtypeStruct((nflip,) + a.shape[1:], a.dtype) for a in arrs],
        scratch_shapes=[pltpu.SemaphoreType.DMA((n * nflip,))] * 2,
    )(*arrs)
    return list(outs)


def _sibling_join_halves(halves, groups):
    n = len(halves)
    ng = len(groups)
    where = {}
    for gi, (_, _, _, members) in enumerate(groups):
        for li, idx in enumerate(members):
            where[idx] = (gi, li)

    def body(*refs):
        ins, outs = refs[:n], refs[n:n + ng]
        send, recv, local_sems = refs[n + ng:]
        x, y, c = _mesh_pos()
        remote, local = [], []
        for a in range(n):
            gi, li = where[a]
            hrows = groups[gi][1] // 2
            mine = outs[gi].at[li, pl.ds(c * hrows, hrows), :]
            local.append(pltpu.make_async_copy(ins[a], mine, local_sems.at[a]))
            remote.append(pltpu.make_async_remote_copy(
                src_ref=ins[a], dst_ref=mine, send_sem=send.at[a], recv_sem=recv.at[a],
                device_id=(x, y, 1 - c), device_id_type=MESH_ID))
        for cp in local + remote:
            cp.start()
        for a in range(n):
            gi, li = where[a]
            hrows = groups[gi][1] // 2
            theirs = outs[gi].at[li, pl.ds((1 - c) * hrows, hrows), :]
            remote[a].wait_send()
            pltpu.make_async_remote_copy(
                src_ref=ins[a], dst_ref=theirs, send_sem=send.at[a], recv_sem=recv.at[a],
                device_id=(x, y, 1 - c), device_id_type=MESH_ID).wait_recv()
            local[a].wait()

    outs = pl.pallas_call(
        body, name="grad_sibling_join",
        in_specs=_any_specs(n), out_specs=_any_specs(ng),
        out_shape=[jax.ShapeDtypeStruct((nl, rows, cols), F32) for nl, rows, cols, _ in groups],
        scratch_shapes=[pltpu.SemaphoreType.DMA((n,))] * 3,
    )(*halves)
    return list(outs)


def _allreduce_small(buf):
    R = buf.shape[0]

    def body(in_ref, out_ref, land, send, recv):
        x, y, c = _mesh_pos()
        out_ref[...] = in_ref[...]
        for s, peer in enumerate(((x, y, 1 - c), (x, 1 - y, c), (1 - x, y, c))):
            cp = pltpu.make_async_remote_copy(
                src_ref=out_ref, dst_ref=land.at[s], send_sem=send.at[s], recv_sem=recv.at[s],
                device_id=peer, device_id_type=MESH_ID)
            cp.start()
            cp.wait()
            out_ref[...] = out_ref[...] + land[s]

    return pl.pallas_call(
        body, name="allreduce_small",
        in_specs=[pl.BlockSpec(memory_space=pltpu.VMEM)],
        out_specs=pl.BlockSpec(memory_space=pltpu.VMEM),
        out_shape=jax.ShapeDtypeStruct((R, LANES), F32),
        scratch_shapes=[pltpu.VMEM((3, R, LANES), F32), pltpu.SemaphoreType.DMA((3,)), pltpu.SemaphoreType.DMA((3,))],
        compiler_params=pltpu.CompilerParams(vmem_limit_bytes=VMEM_LIMIT_MB * 2 ** 20),
    )(buf)


def _presum_with_sibling(grad, landed, pos):
    nchunk, rows, cols = grad.shape
    hrows = rows // 2

    def body(pos_ref, g_ref, l_ref, all_ref, own_ref):
        s = g_ref[...] + l_ref[...]
        all_ref[...] = s.astype(BF16)

        @pl.when(pl.program_id(0) == pos_ref[0])
        def _():
            own_ref[...] = s

    return pl.pallas_call(
        body, name="grad_presum",
        grid_spec=pltpu.PrefetchScalarGridSpec(
            num_scalar_prefetch=1, grid=(nchunk,),
            in_specs=[pl.BlockSpec((None, hrows, cols), lambda k, pos: (k, pos[1], 0)),
                      pl.BlockSpec((None, hrows, cols), lambda k, pos: (k, 0, 0))],
            out_specs=[pl.BlockSpec((None, hrows, cols), lambda k, pos: (k, 0, 0)),
                       pl.BlockSpec((hrows, cols), lambda k, pos: (0, 0))]),
        out_shape=[jax.ShapeDtypeStruct((nchunk, hrows, cols), BF16), jax.ShapeDtypeStruct((hrows, cols), F32)],
        compiler_params=_cparams(("arbitrary",)),
    )(pos, grad, landed)


def _sum_chips(own, landed):
    hrows, cols = own.shape

    def body(o_ref, l_ref, out_ref):
        s = o_ref[...]
        for j in range(len(CHIP_FLIPS)):
            s = s + l_ref[j].astype(F32)
        out_ref[...] = s

    return pl.pallas_call(
        body, name="grad_sum_chips",
        in_specs=[pl.BlockSpec(memory_space=pltpu.VMEM)] * 2,
        out_specs=pl.BlockSpec(memory_space=pltpu.VMEM),
        out_shape=jax.ShapeDtypeStruct((hrows, cols), F32),
        compiler_params=pltpu.CompilerParams(vmem_limit_bytes=VMEM_LIMIT_MB * 2 ** 20),
    )(own, landed)


def _adamw(w, g, m, v):
    R, C = w.shape
    rb = R
    for cand in (512, 352, 320, 256, 128, 64, 32, 16, 8):
        if R % cand == 0:
            rb = cand
            break
    c1 = 1.0 - ADAM_B1 ** ADAM_STEP
    c2 = 1.0 - ADAM_B2 ** ADAM_STEP

    def body(w_ref, g_ref, m_ref, v_ref, d_ref, mo_ref, vo_ref):
        gv = g_ref[...]
        m2 = ADAM_B1 * m_ref[...] + (1.0 - ADAM_B1) * gv
        v2 = ADAM_B2 * v_ref[...] + (1.0 - ADAM_B2) * (gv * gv)
        mo_ref[...] = m2
        vo_ref[...] = v2
        d_ref[...] = -ADAM_LR * ((m2 / c1) / (jnp.sqrt(v2 / c2) + ADAM_EPS) + ADAM_WD * w_ref[...])

    spec = pl.BlockSpec((rb, C), lambda i: (i, 0))
    return pl.pallas_call(
        body, name="adamw",
        grid=(R // rb,),
        in_specs=[spec] * 4, out_specs=[spec] * 3,
        out_shape=[jax.ShapeDtypeStruct((R, C), F32)] * 3,
        compiler_params=_cparams(("arbitrary",)),
    )(w, g, m, v)


def _pack(parts, align=SUBLANES * LANES):
    flat = jnp.concatenate([p.reshape(-1).astype(F32) for p in parts])
    pad = (-flat.shape[0]) % align
    return jnp.pad(flat, (0, pad)).reshape(-1, LANES)


def _unpack(buf, shapes):
    flat = buf.reshape(-1)
    out, off = [], 0
    for shp in shapes:
        size = 1
        for d in shp:
            size *= d
        out.append(flat[off:off + size].reshape(shp))
        off += size
    return out


def _block_diag_gates(w_a, w_x):
    nq = D_RNN // GATE_CHUNK
    hpc = LRU_HEADS // nq
    eye = jnp.eye(hpc, dtype=F32)

    def bd(w):
        wq = w.reshape(nq, hpc, LRU_HEAD_DIM, LRU_HEAD_DIM)
        return (wq[:, :, :, None, :] * eye[None, :, None, :, None]).reshape(nq, GATE_CHUNK, GATE_CHUNK)

    return jnp.concatenate([bd(w_a), bd(w_x)], axis=2).astype(BF16)


def _block_diag_extract(dwbd):
    nq = D_RNN // GATE_CHUNK
    hpc = LRU_HEADS // nq
    eye = jnp.eye(hpc, dtype=F32)

    def ex(d):
        d5 = d.reshape(nq, hpc, LRU_HEAD_DIM, hpc, LRU_HEAD_DIM)
        return jnp.sum(d5 * eye[None, :, None, :, None], axis=3).reshape(LRU_HEADS, LRU_HEAD_DIM, LRU_HEAD_DIM)

    return ex(dwbd[:, :, :GATE_CHUNK]), ex(dwbd[:, :, GATE_CHUNK:])


BIG = ("ffn1_w_gate", "ffn1_w_up", "ffn1_w_down", "lru_w_in", "lru_w_out", "pool_w",
       "ffn2_w_gate", "ffn2_w_up", "ffn2_w_down", "ple_w_gate", "ple_w_proj")
TINY_SHARDED = ("lru_conv_w", "pool_b", "pool_scale")
REPLICATED = ("ffn1_norm", "mix_norm", "lru_conv_b", "lru_w_a", "lru_b_a", "lru_w_x", "lru_b_x", "lru_a_param",
              "ffn2_norm", "ple_norm", "final_norm")
WEIGHT_ORDER = ("ffn1_norm", "ffn1_w_gate", "ffn1_w_up", "ffn1_w_down", "mix_norm", "lru_w_in", "lru_conv_w",
                "lru_conv_b", "lru_w_a", "lru_b_a", "lru_w_x", "lru_b_x", "lru_a_param", "lru_w_out", "pool_w",
                "pool_b", "pool_scale", "ffn2_norm", "ffn2_w_gate", "ffn2_w_up", "ffn2_w_down", "ple_norm",
                "ple_w_gate", "ple_w_proj", "final_norm")


def _as3(a):
    return a.reshape(a.shape[0], -1, a.shape[-1])


def kernel(x, p, ffn1_norm, ffn1_w_gate, ffn1_w_up, ffn1_w_down, mix_norm, lru_w_in, lru_conv_w, lru_conv_b, lru_w_a, lru_b_a, lru_w_x, lru_b_x, lru_a_param, lru_w_out, pool_w, pool_b, pool_scale, ffn2_norm, ffn2_w_gate, ffn2_w_up, ffn2_w_down, ple_norm, ple_w_gate, ple_w_proj, final_norm, loss_target, m_ffn1_norm, m_ffn1_w_gate, m_ffn1_w_up, m_ffn1_w_down, m_mix_norm, m_lru_w_in, m_lru_conv_w, m_lru_conv_b, m_lru_w_a, m_lru_b_a, m_lru_w_x, m_lru_b_x, m_lru_a_param, m_lru_w_out, m_pool_w, m_pool_b, m_pool_scale, m_ffn2_norm, m_ffn2_w_gate, m_ffn2_w_up, m_ffn2_w_down, m_ple_norm, m_ple_w_gate, m_ple_w_proj, m_final_norm, v_ffn1_norm, v_ffn1_w_gate, v_ffn1_w_up, v_ffn1_w_down, v_mix_norm, v_lru_w_in, v_lru_conv_w, v_lru_conv_b, v_lru_w_a, v_lru_b_a, v_lru_w_x, v_lru_b_x, v_lru_a_param, v_lru_w_out, v_pool_w, v_pool_b, v_pool_scale, v_ffn2_norm, v_ffn2_w_gate, v_ffn2_w_up, v_ffn2_w_down, v_ple_norm, v_ple_w_gate, v_ple_w_proj, v_final_norm):
    W = dict(ffn1_norm=ffn1_norm, ffn1_w_gate=ffn1_w_gate, ffn1_w_up=ffn1_w_up, ffn1_w_down=ffn1_w_down,
             mix_norm=mix_norm, lru_w_in=lru_w_in, lru_conv_w=lru_conv_w, lru_conv_b=lru_conv_b, lru_w_a=lru_w_a,
             lru_b_a=lru_b_a, lru_w_x=lru_w_x, lru_b_x=lru_b_x, lru_a_param=lru_a_param, lru_w_out=lru_w_out,
             pool_w=pool_w, pool_b=pool_b, pool_scale=pool_scale, ffn2_norm=ffn2_norm, ffn2_w_gate=ffn2_w_gate,
             ffn2_w_up=ffn2_w_up, ffn2_w_down=ffn2_w_down, ple_norm=ple_norm, ple_w_gate=ple_w_gate,
             ple_w_proj=ple_w_proj, final_norm=final_norm)
    M = dict(ffn1_norm=m_ffn1_norm, ffn1_w_gate=m_ffn1_w_gate, ffn1_w_up=m_ffn1_w_up, ffn1_w_down=m_ffn1_w_down,
             mix_norm=m_mix_norm, lru_w_in=m_lru_w_in, lru_conv_w=m_lru_conv_w, lru_conv_b=m_lru_conv_b,
             lru_w_a=m_lru_w_a, lru_b_a=m_lru_b_a, lru_w_x=m_lru_w_x, lru_b_x=m_lru_b_x, lru_a_param=m_lru_a_param,
             lru_w_out=m_lru_w_out, pool_w=m_pool_w, pool_b=m_pool_b, pool_scale=m_pool_scale, ffn2_norm=m_ffn2_norm,
             ffn2_w_gate=m_ffn2_w_gate, ffn2_w_up=m_ffn2_w_up, ffn2_w_down=m_ffn2_w_down, ple_norm=m_ple_norm,
             ple_w_gate=m_ple_w_gate, ple_w_proj=m_ple_w_proj, final_norm=m_final_norm)
    V = dict(ffn1_norm=v_ffn1_norm, ffn1_w_gate=v_ffn1_w_gate, ffn1_w_up=v_ffn1_w_up, ffn1_w_down=v_ffn1_w_down,
             mix_norm=v_mix_norm, lru_w_in=v_lru_w_in, lru_conv_w=v_lru_conv_w, lru_conv_b=v_lru_conv_b,
             lru_w_a=v_lru_w_a, lru_b_a=v_lru_b_a, lru_w_x=v_lru_w_x, lru_b_x=v_lru_b_x, lru_a_param=v_lru_a_param,
             lru_w_out=v_lru_w_out, pool_w=v_pool_w, pool_b=v_pool_b, pool_scale=v_pool_scale, ffn2_norm=v_ffn2_norm,
             ffn2_w_gate=v_ffn2_w_gate, ffn2_w_up=v_ffn2_w_up, ffn2_w_down=v_ffn2_w_down, ple_norm=v_ple_norm,
             ple_w_gate=v_ple_w_gate, ple_w_proj=v_ple_w_proj, final_norm=v_final_norm)

    S = x.shape[1]
    my_x, my_y, my_c = _mesh_pos()
    my_chip = 2 * my_x + my_y
    pos = jnp.stack([my_chip, my_c]).astype(jnp.int32)
    n_lru, n_pool = lru_w_in.shape[0], pool_w.shape[0]

    tiny_shapes = [W[n].shape for n in TINY_SHARDED]
    tiny_local = _pack([W[n] for n in TINY_SHARDED], align=2 * 16 * LANES)[None]
    gathered = _gather_weights([_as3(W[n].astype(BF16)) for n in BIG] + [tiny_local])
    G = {n: g for n, g in zip(BIG, gathered[:-1])}
    tiny_all = gathered[-1][0]
    tiny_by_chip = [_unpack(tiny_all[k], tiny_shapes) for k in range(N_CHIPS)]
    conv_w_full = jnp.concatenate([tiny_by_chip[k][0] for k in range(N_CHIPS)], axis=-1)
    pool_b_full = jnp.concatenate([tiny_by_chip[k][1] for k in range(N_CHIPS)], axis=-1)
    pool_s_full = jnp.concatenate([tiny_by_chip[k][2] for k in range(N_CHIPS)], axis=-1)

    wg1, wu1 = G["ffn1_w_gate"], G["ffn1_w_up"]
    wg2, wu2 = G["ffn2_w_gate"], G["ffn2_w_up"]
    wd1 = G["ffn1_w_down"].reshape(DEPTH, D_FF, D_MODEL)
    wd2 = G["ffn2_w_down"].reshape(DEPTH, D_FF, D_MODEL)
    win = G["lru_w_in"]
    wout = G["lru_w_out"].reshape(n_lru, D_RNN, D_MODEL)
    wpg = G["ple_w_gate"].reshape(DEPTH, D_MODEL, D_MODEL)
    wpp = G["ple_w_proj"]
    ngroup = len(POOL_WINDOWS)
    pw = G["pool_w"].reshape(n_pool, N_CHIPS, ngroup, POOL_GROUP_DIM // N_CHIPS, POOL_GROUP_DIM)
    pw = pw.transpose(0, 2, 1, 3, 4).reshape(n_pool, ngroup, POOL_GROUP_DIM, POOL_GROUP_DIM)
    wbd = [_block_diag_gates(lru_w_a[j], lru_w_x[j]) for j in range(n_lru)]

    row = lambda a: a.reshape(1, -1)

    h = x.reshape(S, D_MODEL)
    saved = []
    for i in range(DEPTH):
        j = i // 2
        sv = {"h0": h}
        h, sv["xn1"], sv["g1"], sv["u1"] = _ffn_fwd(h, row(ffn1_norm[i]), wg1, wu1, wd1, i)
        sv["h1"] = h
        if i % 2 == 0:
            h, sv["xn_mix"], sv["z"], sv["hs"] = _lru_fwd(
                h, row(mix_norm[i]), win, j, conv_w_full[j], row(lru_conv_b[j]), wbd[j], row(lru_b_a[j]),
                row(lru_b_x[j]), row(lru_a_param[j]), wout)
        else:
            h, sv["u"] = _pool_fwd(h, row(mix_norm[i]), pw, row(pool_b_full[j]), row(pool_s_full[j]), j)
        sv["h2"] = h
        h, sv["xn2"], sv["g2"], sv["u2"] = _ffn_fwd(h, row(ffn2_norm[i]), wg2, wu2, wd2, i)
        sv["h3"] = h
        sv["p"] = p[i, 0]
        h = _ple_fwd(h, row(ple_norm[i]), sv["p"], wpg, wpp, i)
        saved.append(sv)

    dh, dgam_final, loss_part = _final(h, row(final_norm), loss_target.reshape(S, D_MODEL))

    big_grads = {n: [None] * W[n].shape[0] for n in BIG}
    norm_grads = {n: [None] * DEPTH for n in ("ffn1_norm", "mix_norm", "ffn2_norm", "ple_norm")}
    lru_vec = [None] * n_lru
    pool_vec = [None] * n_pool
    sum8 = lambda a: jnp.sum(a, axis=-2)

    for i in reversed(range(DEPTH)):
        j = i // 2
        sv = saved[i]
        dh, xn_p, dpre_p, dproj_p, dgam = _ple_bwd(dh, sv["h3"], row(ple_norm[i]), sv["p"], wpg, wpp, i)
        norm_grads["ple_norm"][i] = sum8(dgam)
        big_grads["ple_w_gate"][i] = _xt_dy("ple_dw_gate", xn_p, dpre_p, 1, D_MODEL, D_MODEL, False, False).reshape(
            N_CHIPS, D_MODEL // N_CHIPS, D_MODEL)
        big_grads["ple_w_proj"][i] = _xt_dy("ple_dw_proj", sv["p"], dproj_p, N_CHIPS, PLE_DIM, PLE_DIM, False, True)

        dwg, dwu, dwd, slabs = _ffn_bwd(sv["xn2"], dh, sv["g2"], sv["u2"], wg2, wu2, wd2, i)
        big_grads["ffn2_w_gate"][i], big_grads["ffn2_w_up"][i], big_grads["ffn2_w_down"][i] = dwg, dwu, dwd
        dh, dgam = _norm_bwd_combine(dh, slabs, sv["h2"], row(ffn2_norm[i]))
        norm_grads["ffn2_norm"][i] = sum8(dgam)

        if i % 2 == 0:
            dz, dpre, xc_b, y_b, dcw, vec = _lru_bwd_seq(
                dh, sv["z"], sv["hs"], conv_w_full[j], row(lru_conv_b[j]), wbd[j], row(lru_b_a[j]),
                row(lru_b_x[j]), row(lru_a_param[j]), wout, j)
            big_grads["lru_w_out"][j] = _xt_dy("lru_dw_out", y_b, dh, 1, D_RNN, D_MODEL, False, False).reshape(
                N_CHIPS, D_RNN // N_CHIPS, D_MODEL)
            big_grads["lru_w_in"][j] = _xt_dy("lru_dw_in", sv["xn_mix"], dz, N_CHIPS, D_MODEL, RNN_IN_CHUNK,
                                              False, True)
            dwbd = _xt_dy("lru_dw_gates", xc_b, dpre, D_RNN // GATE_CHUNK, GATE_CHUNK, 2 * GATE_CHUNK, True, True)
            dw_a, dw_x = _block_diag_extract(dwbd)
            vsum = sum8(vec)
            lru_vec[j] = (sum8(dcw), vsum[0], vsum[1], vsum[2], vsum[3], dw_a, dw_x)
            dh, dgam = _lru_bwd_in(dz, sv["h1"], row(mix_norm[i]), dh, win, j)
            norm_grads["mix_norm"][i] = sum8(dgam)
        else:
            dh_new, dpre_b, vec = _pool_bwd(dh, sv["h1"], sv["u"], row(mix_norm[i]), pw, row(pool_b_full[j]),
                                            row(pool_s_full[j]), j)
            dpw = _xt_dy("pool_dw", sv["u"], dpre_b, ngroup, POOL_GROUP_DIM, POOL_GROUP_DIM, True, True)
            dpw = dpw.reshape(ngroup, N_CHIPS, POOL_GROUP_DIM // N_CHIPS, POOL_GROUP_DIM).transpose(1, 0, 2, 3)
            big_grads["pool_w"][j] = dpw.reshape(N_CHIPS, POOL_GROUP_DIM, POOL_GROUP_DIM)
            vsum = sum8(vec)
            norm_grads["mix_norm"][i] = vsum[0]
            pool_vec[j] = (vsum[1], vsum[2])
            dh = dh_new

        dwg, dwu, dwd, slabs = _ffn_bwd(sv["xn1"], dh, sv["g1"], sv["u1"], wg1, wu1, wd1, i)
        big_grads["ffn1_w_gate"][i], big_grads["ffn1_w_up"][i], big_grads["ffn1_w_down"][i] = dwg, dwu, dwd
        dh, dgam = _norm_bwd_combine(dh, slabs, sv["h0"], row(ffn1_norm[i]))
        norm_grads["ffn1_norm"][i] = sum8(dgam)

    grad_x = dh.reshape(1, S, D_MODEL)

    flat, groups = [], []
    for n in BIG:
        members = []
        for gl in big_grads[n]:
            members.append(len(flat))
            flat.append(gl)
        shard3 = _as3(W[n]).shape
        groups.append((shard3[0], shard3[1], shard3[2], members))
    landed = _sibling_exchange_halves(flat)
    presum = [_presum_with_sibling(g, l, pos) for g, l in zip(flat, landed)]
    from_chips = _chip_exchange([ps[0] for ps in presum])
    halves = [_sum_chips(ps[1], fc) for ps, fc in zip(presum, from_chips)]
    big_final = {n: g for n, g in zip(BIG, _sibling_join_halves(halves, groups))}

    small_parts = [
        jnp.stack(norm_grads["ffn1_norm"]), jnp.stack(norm_grads["mix_norm"]),
        jnp.stack(norm_grads["ffn2_norm"]), jnp.stack(norm_grads["ple_norm"]), sum8(dgam_final),
        jnp.stack([lv[0] for lv in lru_vec]), jnp.stack([lv[1] for lv in lru_vec]),
        jnp.stack([lv[2] for lv in lru_vec]), jnp.stack([lv[3] for lv in lru_vec]),
        jnp.stack([lv[4] for lv in lru_vec]), jnp.stack([lv[5] for lv in lru_vec]),
        jnp.stack([lv[6] for lv in lru_vec]),
        jnp.stack([pv[0] for pv in pool_vec]), jnp.stack([pv[1] for pv in pool_vec]),
        jnp.sum(loss_part).reshape(1),
    ]
    small_names = ("ffn1_norm", "mix_norm", "ffn2_norm", "ple_norm", "final_norm", "lru_conv_w", "lru_conv_b",
                   "lru_b_a", "lru_b_x", "lru_a_param", "lru_w_a", "lru_w_x", "pool_b", "pool_scale", "loss")
    reduced = _unpack(_allreduce_small(_pack(small_parts)), [sp.shape for sp in small_parts])
    small_grad = dict(zip(small_names, reduced))
    loss = small_grad.pop("loss").reshape(())
    for n in TINY_SHARDED:
        width = W[n].shape[-1]
        small_grad[n] = lax.dynamic_slice_in_dim(small_grad[n], my_chip * width, width, axis=-1)

    grads, deltas, new_m, new_v = {}, {}, {}, {}
    for n in BIG:
        shp = W[n].shape
        to2 = lambda a: a.reshape(-1, shp[-1])
        g2 = to2(big_final[n])
        d, m2, v2 = _adamw(to2(W[n]), g2, to2(M[n]), to2(V[n]))
        grads[n], deltas[n], new_m[n], new_v[n] = (a.reshape(shp) for a in (g2, d, m2, v2))
    small_order = TINY_SHARDED + REPLICATED
    small_shapes = [W[n].shape for n in small_order]
    pack_rows = functools.partial(_pack, align=512 * LANES)
    sd, sm, sv_ = _adamw(pack_rows([W[n] for n in small_order]), pack_rows([small_grad[n] for n in small_order]),
                         pack_rows([M[n] for n in small_order]), pack_rows([V[n] for n in small_order]))
    for n, d, m2, v2 in zip(small_order, _unpack(sd, small_shapes), _unpack(sm, small_shapes),
                            _unpack(sv_, small_shapes)):
        grads[n], deltas[n], new_m[n], new_v[n] = small_grad[n].reshape(W[n].shape), d, m2, v2

    return (loss, grad_x, *[grads[n] for n in WEIGHT_ORDER], *[deltas[n] for n in WEIGHT_ORDER],
            *[new_m[n] for n in WEIGHT_ORDER], *[new_v[n] for n in WEIGHT_ORDER])
```

```python
import functools

import jax
import jax.numpy as jnp
from jax import lax
from jax.experimental import pallas as pl
from jax.experimental.pallas import tpu as pltpu

F32 = jnp.float32
BF16 = jnp.bfloat16

D_MODEL = 1024
D_FF = 2816
D_RNN = 1280
DEPTH = 4
N_CHIPS = 4
FF_CHUNK = D_FF // N_CHIPS
RNN_IN_CHUNK = 2 * D_RNN // N_CHIPS
GATE_CHUNK = 640
LRU_HEADS = 16
LRU_HEAD_DIM = 80
CONV_WIDTH = 4
LRU_C = 8.0
POOL_WINDOWS = (2, 4, 8, 16)
POOL_GROUP_DIM = 256
PLE_DIM = 256
RMS_EPS = 1e-6
POOL_HALO = 16
SUBLANES = 8
LANES = 128

ADAM_LR = 0.001
ADAM_B1 = 0.9
ADAM_B2 = 0.999
ADAM_EPS = 1e-08
ADAM_WD = 0.01
ADAM_STEP = 10

VMEM_LIMIT_MB = 56
MESH_ID = pl.DeviceIdType.MESH
CHIP_FLIPS = ((1, 0), (0, 1), (1, 1))


def _cparams(semantics):
    return pltpu.CompilerParams(dimension_semantics=semantics, vmem_limit_bytes=VMEM_LIMIT_MB * 2 ** 20)


def _dot(a, b):
    return lax.dot_general(a, b, (((1,), (0,)), ((), ())), preferred_element_type=F32)


def _dot_nt(a, b):
    return lax.dot_general(a, b, (((1,), (1,)), ((), ())), preferred_element_type=F32)


def _dot_tn(a, b):
    return lax.dot_general(a, b, (((0,), (0,)), ((), ())), preferred_element_type=F32)


def _sigmoid(x):
    return 1.0 / (1.0 + jnp.exp(-x))


def _rms(hf, gamma):
    rstd = lax.rsqrt(jnp.mean(hf * hf, axis=-1, keepdims=True) + RMS_EPS)
    xhat = hf * rstd
    return xhat, rstd, xhat * gamma


def _rms_bwd(xhat, rstd, gamma, dxn):
    dxhat = dxn * gamma
    m = jnp.mean(dxhat * xhat, axis=-1, keepdims=True)
    return rstd * (dxhat - xhat * m), _rowsum8(dxn * xhat)


def _rowsum8(v):
    tm, n = v.shape
    return jnp.sum(v.reshape(tm // SUBLANES, SUBLANES, n), axis=0)


def _gelu(x):
    u = 0.7978845608028654 * (x + 0.044715 * x * x * x)
    return 0.5 * x * (1.0 + jnp.tanh(u))


def _gelu_and_grad(x):
    c = 0.7978845608028654
    u = c * (x + 0.044715 * x * x * x)
    th = jnp.tanh(u)
    g = 0.5 * x * (1.0 + th)
    dg = 0.5 * (1.0 + th) + 0.5 * x * (1.0 - th * th) * c * (1.0 + 3.0 * 0.044715 * x * x)
    return g, dg


def _softplus(z):
    e = jnp.exp(-jnp.abs(z))
    u = 1.0 + e
    log1p = jnp.where(u == 1.0, e, jnp.log(u) * e / jnp.where(u == 1.0, 1.0, u - 1.0))
    return jnp.maximum(z, 0.0) + log1p


def _neg_expm1(x):
    series = -(x + 0.5 * x * x + (1.0 / 6.0) * x * x * x)
    return jnp.where(x > -1e-2, series, 1.0 - jnp.exp(x))


def _shift_down(ext, j, halo):
    return pltpu.roll(ext, j, 0)[halo:]


def _shift_up(ext, j, tm):
    n = ext.shape[0]
    return pltpu.roll(ext, n - j, 0)[:tm]


def _scan_causal(a, b):
    tm = a.shape[0]
    rows = lax.broadcasted_iota(jnp.int32, a.shape, 0)
    s = 1
    while s < tm:
        keep = rows >= s
        a_sh = jnp.where(keep, pltpu.roll(a, s, 0), 1.0)
        b_sh = jnp.where(keep, pltpu.roll(b, s, 0), 0.0)
        b = a * b_sh + b
        a = a * a_sh
        s *= 2
    return a, b


def _scan_anticausal(c, d):
    tm = c.shape[0]
    rows = lax.broadcasted_iota(jnp.int32, c.shape, 0)
    s = 1
    while s < tm:
        keep = rows < tm - s
        c_sh = jnp.where(keep, pltpu.roll(c, tm - s, 0), 1.0)
        d_sh = jnp.where(keep, pltpu.roll(d, tm - s, 0), 0.0)
        d = d + c * d_sh
        c = c * c_sh
        s *= 2
    return c, d


def _tile(n, want):
    t = min(n, want)
    assert n % t == 0, (n, t)
    return t


def _ffn_fwd(h, gamma, wg, wu, wd, layer, cargo=None):
    S = h.shape[0]
    tm = _tile(S, 1024)
    nt = S // tm
    cargo = cargo or _Cargo()
    n_in, n_out = 5, 4
    nc_in, nc_out = len(cargo.operands), len(cargo.out_shapes)

    def body(*refs):
        h_ref, g_ref, wg_ref, wu_ref, wd_ref = refs[:n_in]
        c_ins = refs[n_in:n_in + nc_in]
        ho_ref, xn_ref, gg_ref, uu_ref = refs[n_in + nc_in:n_in + nc_in + n_out]
        c_outs = refs[n_in + nc_in + n_out:n_in + nc_in + n_out + nc_out]
        xn_s, acc_s = refs[n_in + nc_in + n_out + nc_out:n_in + nc_in + n_out + nc_out + 2]
        sems = refs[n_in + nc_in + n_out + nc_out + 2:]
        t, k = pl.program_id(0), pl.program_id(1)

        @pl.when((t == 0) & (k == 0))
        def _():
            cargo.start(c_ins, c_outs, sems)

        @pl.when(k == 0)
        def _():
            _, _, xn = _rms(h_ref[...], g_ref[...])
            xnb = xn.astype(BF16)
            xn_s[...] = xnb
            xn_ref[...] = xnb
            acc_s[...] = jnp.zeros_like(acc_s)

        xnb = xn_s[...]
        g = _dot(xnb, wg_ref[...])
        u = _dot(xnb, wu_ref[...])
        gg_ref[...] = g.astype(BF16)
        uu_ref[...] = u.astype(BF16)
        hid = (g * _sigmoid(g)) * u
        acc_s[...] += _dot(hid.astype(BF16), wd_ref[...])

        @pl.when(k == N_CHIPS - 1)
        def _():
            ho_ref[...] = h_ref[...] + 0.5 * acc_s[...]

        @pl.when((t == nt - 1) & (k == N_CHIPS - 1))
        def _():
            cargo.finish(c_ins, c_outs, sems)

    outs = pl.pallas_call(
        body, name="ffn_fwd",
        grid=(nt, N_CHIPS),
        in_specs=[
            pl.BlockSpec((tm, D_MODEL), lambda t, k: (t, 0)),
            pl.BlockSpec((1, D_MODEL), lambda t, k: (0, 0)),
            pl.BlockSpec((None, None, D_MODEL, FF_CHUNK), lambda t, k: (layer, k, 0, 0)),
            pl.BlockSpec((None, None, D_MODEL, FF_CHUNK), lambda t, k: (layer, k, 0, 0)),
            pl.BlockSpec((None, None, FF_CHUNK, D_MODEL), lambda t, k: (layer, k, 0, 0)),
        ] + _any_specs(nc_in),
        out_specs=[
            pl.BlockSpec((tm, D_MODEL), lambda t, k: (t, 0)),
            pl.BlockSpec((tm, D_MODEL), lambda t, k: (t, 0)),
            pl.BlockSpec((None, tm, FF_CHUNK), lambda t, k: (k, t, 0)),
            pl.BlockSpec((None, tm, FF_CHUNK), lambda t, k: (k, t, 0)),
        ] + _any_specs(nc_out),
        out_shape=[
            jax.ShapeDtypeStruct((S, D_MODEL), F32),
            jax.ShapeDtypeStruct((S, D_MODEL), BF16),
            jax.ShapeDtypeStruct((N_CHIPS, S, FF_CHUNK), BF16),
            jax.ShapeDtypeStruct((N_CHIPS, S, FF_CHUNK), BF16),
        ] + cargo.out_shapes,
        input_output_aliases={n_in + i: n_out + o for i, o in cargo.aliases.items()},
        scratch_shapes=[pltpu.VMEM((tm, D_MODEL), BF16), pltpu.VMEM((tm, D_MODEL), F32)] + cargo.sem_shapes,
        compiler_params=_cparams(("arbitrary", "arbitrary")),
    )(h, gamma, wg, wu, wd, *cargo.operands)
    return outs[:n_out], list(outs[n_out:])


def _ffn_bwd(xn, dout, gg, uu, wg, wu, wd, layer, cargo=None):
    S = xn.shape[0]
    tm = _tile(S, 512)
    nt = S // tm
    cargo = cargo or _Cargo()
    n_in, n_out = 7, 4
    nc_in, nc_out = len(cargo.operands), len(cargo.out_shapes)

    def body(*refs):
        xn_ref, do_ref, gg_ref, uu_ref, wg_ref, wu_ref, wd_ref = refs[:n_in]
        c_ins = refs[n_in:n_in + nc_in]
        dwg_ref, dwu_ref, dwd_ref, slab_ref = refs[n_in + nc_in:n_in + nc_in + n_out]
        c_outs = refs[n_in + nc_in + n_out:n_in + nc_in + n_out + nc_out]
        sems = refs[n_in + nc_in + n_out + nc_out:]
        k, t = pl.program_id(0), pl.program_id(1)

        @pl.when((k == 0) & (t == 0))
        def _():
            cargo.start(c_ins, c_outs, sems)

        @pl.when(t == 0)
        def _():
            dwg_ref[...] = jnp.zeros_like(dwg_ref)
            dwu_ref[...] = jnp.zeros_like(dwu_ref)
            dwd_ref[...] = jnp.zeros_like(dwd_ref)

        xnb = xn_ref[...]
        dob = (0.5 * do_ref[...]).astype(BF16)
        g = gg_ref[...].astype(F32)
        u = uu_ref[...].astype(F32)
        s = _sigmoid(g)
        sil = g * s
        dhid = _dot_nt(dob, wd_ref[...])
        dwd_ref[...] += _dot_tn((sil * u).astype(BF16), dob)
        du = (dhid * sil).astype(BF16)
        dg = (dhid * u * (s * (1.0 + g * (1.0 - s)))).astype(BF16)
        dwg_ref[...] += _dot_tn(xnb, dg)
        dwu_ref[...] += _dot_tn(xnb, du)
        slab_ref[...] = (_dot_nt(dg, wg_ref[...]) + _dot_nt(du, wu_ref[...])).astype(BF16)

        @pl.when((k == N_CHIPS - 1) & (t == nt - 1))
        def _():
            cargo.finish(c_ins, c_outs, sems)

    outs = pl.pallas_call(
        body, name="ffn_bwd",
        grid=(N_CHIPS, nt),
        in_specs=[
            pl.BlockSpec((tm, D_MODEL), lambda k, t: (t, 0)),
            pl.BlockSpec((tm, D_MODEL), lambda k, t: (t, 0)),
            pl.BlockSpec((None, tm, FF_CHUNK), lambda k, t: (k, t, 0)),
            pl.BlockSpec((None, tm, FF_CHUNK), lambda k, t: (k, t, 0)),
            pl.BlockSpec((None, None, D_MODEL, FF_CHUNK), lambda k, t: (layer, k, 0, 0)),
            pl.BlockSpec((None, None, D_MODEL, FF_CHUNK), lambda k, t: (layer, k, 0, 0)),
            pl.BlockSpec((None, None, FF_CHUNK, D_MODEL), lambda k, t: (layer, k, 0, 0)),
        ] + _any_specs(nc_in),
        out_specs=[
            pl.BlockSpec((None, D_MODEL, FF_CHUNK), lambda k, t: (k, 0, 0)),
            pl.BlockSpec((None, D_MODEL, FF_CHUNK), lambda k, t: (k, 0, 0)),
            pl.BlockSpec((None, FF_CHUNK, D_MODEL), lambda k, t: (k, 0, 0)),
            pl.BlockSpec((None, tm, D_MODEL), lambda k, t: (k, t, 0)),
        ] + _any_specs(nc_out),
        out_shape=[
            jax.ShapeDtypeStruct((N_CHIPS, D_MODEL, FF_CHUNK), F32),
            jax.ShapeDtypeStruct((N_CHIPS, D_MODEL, FF_CHUNK), F32),
            jax.ShapeDtypeStruct((N_CHIPS, FF_CHUNK, D_MODEL), F32),
            jax.ShapeDtypeStruct((N_CHIPS, S, D_MODEL), BF16),
        ] + cargo.out_shapes,
        input_output_aliases={n_in + i: n_out + o for i, o in cargo.aliases.items()},
        scratch_shapes=list(cargo.sem_shapes),
        compiler_params=_cparams(("arbitrary", "arbitrary")),
    )(xn, dout, gg, uu, wg, wu, wd, *cargo.operands)
    return outs[:n_out], list(outs[n_out:])


def _norm_bwd_combine(dres, slabs, h, gamma):
    S = h.shape[0]
    tm = _tile(S, 512)
    nslab = slabs.shape[0]

    def body(dres_ref, slab_ref, h_ref, g_ref, dh_ref, dgam_ref):
        t = pl.program_id(0)
        dxn = slab_ref[0].astype(F32)
        for i in range(1, nslab):
            dxn = dxn + slab_ref[i].astype(F32)
        xhat, rstd, _ = _rms(h_ref[...], g_ref[...])
        dhn, dgam = _rms_bwd(xhat, rstd, g_ref[...], dxn)
        dh_ref[...] = dres_ref[...] + dhn

        @pl.when(t == 0)
        def _():
            dgam_ref[...] = jnp.zeros_like(dgam_ref)

        dgam_ref[...] += dgam

    return pl.pallas_call(
        body, name="norm_bwd_combine",
        grid=(S // tm,),
        in_specs=[
            pl.BlockSpec((tm, D_MODEL), lambda t: (t, 0)),
            pl.BlockSpec((nslab, tm, D_MODEL), lambda t: (0, t, 0)),
            pl.BlockSpec((tm, D_MODEL), lambda t: (t, 0)),
            pl.BlockSpec((1, D_MODEL), lambda t: (0, 0)),
        ],
        out_specs=[
            pl.BlockSpec((tm, D_MODEL), lambda t: (t, 0)),
            pl.BlockSpec((SUBLANES, D_MODEL), lambda t: (0, 0)),
        ],
        out_shape=[jax.ShapeDtypeStruct((S, D_MODEL), F32), jax.ShapeDtypeStruct((SUBLANES, D_MODEL), F32)],
        compiler_params=_cparams(("arbitrary",)),
    )(dres, slabs, h, gamma)


def _xt_dy(name, x, dy, nchunk, kb, nb, x_by_chunk, y_by_chunk):
    S = x.shape[0]
    tm = _tile(S, 512)

    def body(x_ref, dy_ref, o_ref):
        @pl.when(pl.program_id(1) == 0)
        def _():
            o_ref[...] = jnp.zeros_like(o_ref)

        o_ref[...] += _dot_tn(x_ref[...].astype(BF16), dy_ref[...].astype(BF16))

    return pl.pallas_call(
        body, name=name,
        grid=(nchunk, S // tm),
        in_specs=[
            pl.BlockSpec((tm, kb), (lambda c, t: (t, c)) if x_by_chunk else (lambda c, t: (t, 0))),
            pl.BlockSpec((tm, nb), (lambda c, t: (t, c)) if y_by_chunk else (lambda c, t: (t, 0))),
        ],
        out_specs=pl.BlockSpec((None, kb, nb), lambda c, t: (c, 0, 0)),
        out_shape=jax.ShapeDtypeStruct((nchunk, kb, nb), F32),
        compiler_params=_cparams(("arbitrary", "arbitrary")),
    )(x, dy)


def _lru_gates(xc, wbd_ref, ba, bx, apar):
    xcb = xc.astype(BF16)
    r_parts, ig_parts = [], []
    for q in range(D_RNN // GATE_CHUNK):
        lo, hi = q * GATE_CHUNK, (q + 1) * GATE_CHUNK
        pre = _dot(xcb[:, lo:hi], wbd_ref[q])
        r_parts.append(_sigmoid(pre[:, :GATE_CHUNK] + ba[:, lo:hi]))
        ig_parts.append(_sigmoid(pre[:, GATE_CHUNK:] + bx[:, lo:hi]))
    r = jnp.concatenate(r_parts, axis=1)
    ig = jnp.concatenate(ig_parts, axis=1)
    sp = LRU_C * _softplus(-apar)
    log_a = -(r * sp)
    a = jnp.exp(log_a)
    mult = jnp.sqrt(_neg_expm1(2.0 * log_a))
    return r, ig, a, mult, sp


def _conv_causal(xb, tail, cw_ref, cb):
    ext = jnp.concatenate([tail, xb], axis=0)
    xc = cb + cw_ref[CONV_WIDTH - 1:CONV_WIDTH, :] * xb
    for j in range(1, CONV_WIDTH):
        xc = xc + cw_ref[CONV_WIDTH - 1 - j:CONV_WIDTH - j, :] * _shift_down(ext, j, SUBLANES)
    return xc, ext


def _lru_fwd(h, gamma, win, layer, convw, convb, wbd, ba, bx, apar, wout):
    S = h.shape[0]
    tm = _tile(S, 256)

    def body(h_ref, g_ref, win_ref, cw_ref, cb_ref, wbd_ref, ba_ref, bx_ref, ap_ref, wout_ref,
             ho_ref, xn_ref, z_ref, hs_ref, tail_s, carry_s):
        @pl.when(pl.program_id(0) == 0)
        def _():
            tail_s[...] = jnp.zeros_like(tail_s)
            carry_s[...] = jnp.zeros_like(carry_s)

        hf = h_ref[...]
        _, _, xn = _rms(hf, g_ref[...])
        xnb = xn.astype(BF16)
        xn_ref[...] = xnb
        for k in range(N_CHIPS):
            z_ref[:, k * RNN_IN_CHUNK:(k + 1) * RNN_IN_CHUNK] = _dot(xnb, win_ref[k])
        gate = z_ref[:, :D_RNN]
        xb = z_ref[:, D_RNN:]
        xc, _ = _conv_causal(xb, tail_s[...], cw_ref, cb_ref[...])
        tail_s[...] = xb[tm - SUBLANES:, :]
        _, ig, a, mult, _ = _lru_gates(xc, wbd_ref, ba_ref[...], bx_ref[...], ap_ref[...])
        big_a, big_b = _scan_causal(a, mult * (ig * xc))
        hs = big_a * carry_s[SUBLANES - 1:SUBLANES, :] + big_b
        hs_ref[...] = hs
        carry_s[...] = hs[tm - SUBLANES:, :]
        y = hs * _gelu(gate)
        ho_ref[...] = hf + _dot(y.astype(BF16), wout_ref[...])

    row = lambda n: pl.BlockSpec((1, n), lambda t: (0, 0))
    return pl.pallas_call(
        body, name="lru_fwd",
        grid=(S // tm,),
        in_specs=[
            pl.BlockSpec((tm, D_MODEL), lambda t: (t, 0)),
            row(D_MODEL),
            pl.BlockSpec((None, N_CHIPS, D_MODEL, RNN_IN_CHUNK), lambda t: (layer, 0, 0, 0)),
            pl.BlockSpec((CONV_WIDTH, D_RNN), lambda t: (0, 0)),
            row(D_RNN),
            pl.BlockSpec((D_RNN // GATE_CHUNK, GATE_CHUNK, 2 * GATE_CHUNK), lambda t: (0, 0, 0)),
            row(D_RNN), row(D_RNN), row(D_RNN),
            pl.BlockSpec((None, D_RNN, D_MODEL), lambda t: (layer, 0, 0)),
        ],
        out_specs=[
            pl.BlockSpec((tm, D_MODEL), lambda t: (t, 0)),
            pl.BlockSpec((tm, D_MODEL), lambda t: (t, 0)),
            pl.BlockSpec((tm, 2 * D_RNN), lambda t: (t, 0)),
            pl.BlockSpec((tm, D_RNN), lambda t: (t, 0)),
        ],
        out_shape=[
            jax.ShapeDtypeStruct((S, D_MODEL), F32),
            jax.ShapeDtypeStruct((S, D_MODEL), BF16),
            jax.ShapeDtypeStruct((S, 2 * D_RNN), F32),
            jax.ShapeDtypeStruct((S, D_RNN), F32),
        ],
        scratch_shapes=[pltpu.VMEM((SUBLANES, D_RNN), F32), pltpu.VMEM((SUBLANES, D_RNN), F32)],
        compiler_params=_cparams(("arbitrary",)),
    )(h, gamma, win, convw, convb, wbd, ba, bx, apar, wout)


def _lru_bwd_seq(dout, z, hs, convw, convb, wbd, ba, bx, apar, wout, layer):
    S = dout.shape[0]
    tm = _tile(S, 256)
    nt = S // tm
    per8 = tm // SUBLANES
    rev = lambda i: nt - 1 - i
    prev8 = lambda i: jnp.maximum(rev(i) * per8 - 1, 0)

    def body(do_ref, z_ref, hs_ref, ztail_ref, hstail_ref, cw_ref, cb_ref, wbd_ref, ba_ref, bx_ref, ap_ref, wout_ref,
             dz_ref, dpre_ref, xc_ref, y_ref, dcw_ref, vec_ref, a_first_s, g_first_s, dxc_head_s):
        i = pl.program_id(0)
        first_in_time = rev(i) == 0

        @pl.when(i == 0)
        def _():
            a_first_s[...] = jnp.zeros_like(a_first_s)
            g_first_s[...] = jnp.zeros_like(g_first_s)
            dxc_head_s[...] = jnp.zeros_like(dxc_head_s)
            dcw_ref[...] = jnp.zeros_like(dcw_ref)
            vec_ref[...] = jnp.zeros_like(vec_ref)

        gate = z_ref[:, :D_RNN]
        xb = z_ref[:, D_RNN:]
        hist = jnp.where(first_in_time, 0.0, 1.0)
        xc, xext = _conv_causal(xb, ztail_ref[:, D_RNN:] * hist, cw_ref, cb_ref[...])
        r, ig, a, mult, sp = _lru_gates(xc, wbd_ref, ba_ref[...], bx_ref[...], ap_ref[...])
        hs = hs_ref[...]
        gel, dgel = _gelu_and_grad(gate)
        y = hs * gel
        y_ref[...] = y.astype(BF16)
        xc_ref[...] = xc.astype(BF16)

        dy = _dot_nt(do_ref[...].astype(BF16), wout_ref[...])
        dhs = dy * gel
        dgate = dy * hs * dgel

        coef = _shift_up(jnp.concatenate([a, a_first_s[...]], axis=0), 1, tm)
        big_c, big_d = _scan_anticausal(coef, dhs)
        g = big_d + big_c * g_first_s[0:1, :]
        g_first_s[...] = g[:SUBLANES, :]
        a_first_s[...] = a[:SUBLANES, :]

        hs_prev = _shift_down(jnp.concatenate([hstail_ref[...] * hist, hs], axis=0), 1, SUBLANES)
        da = g * hs_prev
        dmult = g * ig * xc
        dig = g * mult * xc
        dxc = g * mult * ig
        dlog_a = da * a - dmult * (a * a) / mult
        dr = -(dlog_a * sp)
        dpre_a = dr * r * (1.0 - r)
        dpre_x = dig * ig * (1.0 - ig)
        d_apar = dlog_a * r * (LRU_C * _sigmoid(-ap_ref[...]))

        for q in range(D_RNN // GATE_CHUNK):
            lo, hi = q * GATE_CHUNK, (q + 1) * GATE_CHUNK
            dpre_q = jnp.concatenate([dpre_a[:, lo:hi], dpre_x[:, lo:hi]], axis=1).astype(BF16)
            dpre_ref[:, 2 * lo:2 * hi] = dpre_q
            dxc_q = _dot_nt(dpre_q, wbd_ref[q])
            if q == 0:
                dxc_parts = [dxc_q]
            else:
                dxc_parts.append(dxc_q)
        dxc = dxc + jnp.concatenate(dxc_parts, axis=1)

        dext = jnp.concatenate([dxc, dxc_head_s[...]], axis=0)
        dxb = cw_ref[CONV_WIDTH - 1:CONV_WIDTH, :] * dxc
        for j in range(1, CONV_WIDTH):
            dxb = dxb + cw_ref[CONV_WIDTH - 1 - j:CONV_WIDTH - j, :] * _shift_up(dext, j, tm)
        dxc_head_s[...] = dxc[:SUBLANES, :]
        dz_ref[:, :D_RNN] = dgate.astype(BF16)
        dz_ref[:, D_RNN:] = dxb.astype(BF16)

        dcw_ref[CONV_WIDTH - 1] += _rowsum8(dxc * xb)
        for j in range(1, CONV_WIDTH):
            dcw_ref[CONV_WIDTH - 1 - j] += _rowsum8(dxc * _shift_down(xext, j, SUBLANES))
        vec_ref[0] += _rowsum8(dxc)
        vec_ref[1] += _rowsum8(dpre_a)
        vec_ref[2] += _rowsum8(dpre_x)
        vec_ref[3] += _rowsum8(d_apar)

    row = lambda n: pl.BlockSpec((1, n), lambda i: (0, 0))
    return pl.pallas_call(
        body, name="lru_bwd_seq",
        grid=(nt,),
        in_specs=[
            pl.BlockSpec((tm, D_MODEL), lambda i: (rev(i), 0)),
            pl.BlockSpec((tm, 2 * D_RNN), lambda i: (rev(i), 0)),
            pl.BlockSpec((tm, D_RNN), lambda i: (rev(i), 0)),
            pl.BlockSpec((SUBLANES, 2 * D_RNN), lambda i: (prev8(i), 0)),
            pl.BlockSpec((SUBLANES, D_RNN), lambda i: (prev8(i), 0)),
            pl.BlockSpec((CONV_WIDTH, D_RNN), lambda i: (0, 0)),
            row(D_RNN),
            pl.BlockSpec((D_RNN // GATE_CHUNK, GATE_CHUNK, 2 * GATE_CHUNK), lambda i: (0, 0, 0)),
            row(D_RNN), row(D_RNN), row(D_RNN),
            pl.BlockSpec((None, D_RNN, D_MODEL), lambda i: (layer, 0, 0)),
        ],
        out_specs=[
            pl.BlockSpec((tm, 2 * D_RNN), lambda i: (rev(i), 0)),
            pl.BlockSpec((tm, 2 * D_RNN), lambda i: (rev(i), 0)),
            pl.BlockSpec((tm, D_RNN), lambda i: (rev(i), 0)),
            pl.BlockSpec((tm, D_RNN), lambda i: (rev(i), 0)),
            pl.BlockSpec((CONV_WIDTH, SUBLANES, D_RNN), lambda i: (0, 0, 0)),
            pl.BlockSpec((4, SUBLANES, D_RNN), lambda i: (0, 0, 0)),
        ],
        out_shape=[
            jax.ShapeDtypeStruct((S, 2 * D_RNN), BF16),
            jax.ShapeDtypeStruct((S, 2 * D_RNN), BF16),
            jax.ShapeDtypeStruct((S, D_RNN), BF16),
            jax.ShapeDtypeStruct((S, D_RNN), BF16),
            jax.ShapeDtypeStruct((CONV_WIDTH, SUBLANES, D_RNN), F32),
            jax.ShapeDtypeStruct((4, SUBLANES, D_RNN), F32),
        ],
        scratch_shapes=[pltpu.VMEM((SUBLANES, D_RNN), F32)] * 3,
        compiler_params=_cparams(("arbitrary",)),
    )(dout, z, hs, z, hs, convw, convb, wbd, ba, bx, apar, wout)


def _lru_bwd_in(dz, h, gamma, dres, win, layer):
    S = h.shape[0]
    tm = _tile(S, 512)

    def body(dz_ref, h_ref, g_ref, dres_ref, win_ref, dh_ref, dgam_ref):
        dxn = _dot_nt(dz_ref[:, :RNN_IN_CHUNK], win_ref[0])
        for k in range(1, N_CHIPS):
            dxn = dxn + _dot_nt(dz_ref[:, k * RNN_IN_CHUNK:(k + 1) * RNN_IN_CHUNK], win_ref[k])
        xhat, rstd, _ = _rms(h_ref[...], g_ref[...])
        dhn, dgam = _rms_bwd(xhat, rstd, g_ref[...], dxn)
        dh_ref[...] = dres_ref[...] + dhn

        @pl.when(pl.program_id(0) == 0)
        def _():
            dgam_ref[...] = jnp.zeros_like(dgam_ref)

        dgam_ref[...] += dgam

    return pl.pallas_call(
        body, name="lru_bwd_in",
        grid=(S // tm,),
        in_specs=[
            pl.BlockSpec((tm, 2 * D_RNN), lambda t: (t, 0)),
            pl.BlockSpec((tm, D_MODEL), lambda t: (t, 0)),
            pl.BlockSpec((1, D_MODEL), lambda t: (0, 0)),
            pl.BlockSpec((tm, D_MODEL), lambda t: (t, 0)),
            pl.BlockSpec((None, N_CHIPS, D_MODEL, RNN_IN_CHUNK), lambda t: (layer, 0, 0, 0)),
        ],
        out_specs=[
            pl.BlockSpec((tm, D_MODEL), lambda t: (t, 0)),
            pl.BlockSpec((SUBLANES, D_MODEL), lambda t: (0, 0)),
        ],
        out_shape=[jax.ShapeDtypeStruct((S, D_MODEL), F32), jax.ShapeDtypeStruct((SUBLANES, D_MODEL), F32)],
        compiler_params=_cparams(("arbitrary",)),
    )(dz, h, gamma, dres, win)


def _pool_inv_count(t_index, tm):
    rows = (lax.broadcasted_iota(jnp.int32, (tm, D_MODEL), 0) + t_index * tm + 1).astype(F32)
    col = lax.broadcasted_iota(jnp.int32, (tm, D_MODEL), 1)
    win = jnp.where(col < POOL_GROUP_DIM, float(POOL_WINDOWS[0]),
                    jnp.where(col < 2 * POOL_GROUP_DIM, float(POOL_WINDOWS[1]),
                              jnp.where(col < 3 * POOL_GROUP_DIM, float(POOL_WINDOWS[2]), float(POOL_WINDOWS[3]))))
    return 1.0 / jnp.minimum(rows, win)


def _window_sums(ext, shift, take):
    gd = POOL_GROUP_DIM
    s2 = ext + shift(ext, 1)
    s4 = s2[:, gd:] + shift(s2[:, gd:], 2)
    s8 = s4[:, gd:] + shift(s4[:, gd:], 4)
    s16 = s8[:, gd:] + shift(s8[:, gd:], 8)
    return jnp.concatenate([take(s2[:, :gd]), take(s4[:, :gd]), take(s8[:, :gd]), take(s16)], axis=1)


def _pool_fwd(h, gamma, pw, pb, pscale, layer):
    S = h.shape[0]
    tm = _tile(S, 512)

    def body(h_ref, g_ref, pw_ref, pb_ref, ps_ref, ho_ref, u_ref, tail_s):
        t = pl.program_id(0)

        @pl.when(t == 0)
        def _():
            tail_s[...] = jnp.zeros_like(tail_s)

        hf = h_ref[...]
        _, _, hn = _rms(hf, g_ref[...])
        ext = jnp.concatenate([tail_s[...], hn], axis=0)
        tail_s[...] = hn[tm - POOL_HALO:, :]
        sums = _window_sums(ext, lambda v, j: pltpu.roll(v, j, 0), lambda v: v[POOL_HALO:])
        ub = (sums * _pool_inv_count(t, tm) - hn).astype(BF16)
        u_ref[...] = ub
        ys = [_dot(ub[:, g * POOL_GROUP_DIM:(g + 1) * POOL_GROUP_DIM], pw_ref[g]) for g in range(len(POOL_WINDOWS))]
        y = jnp.concatenate(ys, axis=1)
        ho_ref[...] = hf + (y + pb_ref[...]) * ps_ref[...]

    row = pl.BlockSpec((1, D_MODEL), lambda t: (0, 0))
    return pl.pallas_call(
        body, name="pool_fwd",
        grid=(S // tm,),
        in_specs=[
            pl.BlockSpec((tm, D_MODEL), lambda t: (t, 0)), row,
            pl.BlockSpec((None, len(POOL_WINDOWS), POOL_GROUP_DIM, POOL_GROUP_DIM), lambda t: (layer, 0, 0, 0)),
            row, row,
        ],
        out_specs=[pl.BlockSpec((tm, D_MODEL), lambda t: (t, 0)), pl.BlockSpec((tm, D_MODEL), lambda t: (t, 0))],
        out_shape=[jax.ShapeDtypeStruct((S, D_MODEL), F32), jax.ShapeDtypeStruct((S, D_MODEL), BF16)],
        scratch_shapes=[pltpu.VMEM((POOL_HALO, D_MODEL), F32)],
        compiler_params=_cparams(("arbitrary",)),
    )(h, gamma, pw, pb, pscale)


def _pool_bwd(dout, h, u, gamma, pw, pb, pscale, layer):
    S = h.shape[0]
    tm = _tile(S, 512)
    nt = S // tm
    rev = lambda i: nt - 1 - i
    ngroup = len(POOL_WINDOWS)

    def body(do_ref, h_ref, u_ref, g_ref, pw_ref, pb_ref, ps_ref, dh_ref, dpre_ref, vec_ref, head_s):
        i = pl.program_id(0)

        @pl.when(i == 0)
        def _():
            head_s[...] = jnp.zeros_like(head_s)
            vec_ref[...] = jnp.zeros_like(vec_ref)

        do = do_ref[...]
        ub = u_ref[...]
        gsl = lambda v, g: v[:, g * POOL_GROUP_DIM:(g + 1) * POOL_GROUP_DIM]
        y = jnp.concatenate([_dot(gsl(ub, g), pw_ref[g]) for g in range(ngroup)], axis=1)
        dpre = do * ps_ref[...]
        dpb = dpre.astype(BF16)
        dpre_ref[...] = dpb
        du = jnp.concatenate([_dot_nt(gsl(dpb, g), pw_ref[g]) for g in range(ngroup)], axis=1)
        v = du * _pool_inv_count(rev(i), tm)
        ext = jnp.concatenate([v, head_s[...]], axis=0)
        head_s[...] = v[:POOL_HALO, :]
        n = tm + POOL_HALO
        dhn = _window_sums(ext, lambda w, j: pltpu.roll(w, n - j, 0), lambda w: w[:tm]) - du
        xhat, rstd, _ = _rms(h_ref[...], g_ref[...])
        dh_in, dgam = _rms_bwd(xhat, rstd, g_ref[...], dhn)
        dh_ref[...] = do + dh_in
        vec_ref[0] += dgam
        vec_ref[1] += _rowsum8(dpre)
        vec_ref[2] += _rowsum8(do * (y + pb_ref[...]))

    row = pl.BlockSpec((1, D_MODEL), lambda i: (0, 0))
    tile = pl.BlockSpec((tm, D_MODEL), lambda i: (rev(i), 0))
    return pl.pallas_call(
        body, name="pool_bwd",
        grid=(nt,),
        in_specs=[tile, tile, tile, row,
                  pl.BlockSpec((None, ngroup, POOL_GROUP_DIM, POOL_GROUP_DIM), lambda i: (layer, 0, 0, 0)), row, row],
        out_specs=[tile, tile, pl.BlockSpec((3, SUBLANES, D_MODEL), lambda i: (0, 0, 0))],
        out_shape=[jax.ShapeDtypeStruct((S, D_MODEL), F32), jax.ShapeDtypeStruct((S, D_MODEL), BF16),
                   jax.ShapeDtypeStruct((3, SUBLANES, D_MODEL), F32)],
        scratch_shapes=[pltpu.VMEM((POOL_HALO, D_MODEL), F32)],
        compiler_params=_cparams(("arbitrary",)),
    )(dout, h, u, gamma, pw, pb, pscale)


def _ple_parts(hf, gamma, p_tile, wgate_ref, wproj_ref):
    xhat, rstd, xn = _rms(hf, gamma)
    xnb = xn.astype(BF16)
    gate = _sigmoid(_dot(xnb, wgate_ref[...]))
    pb = p_tile.astype(BF16)
    proj = jnp.concatenate([_dot(pb, wproj_ref[k]) for k in range(N_CHIPS)], axis=1)
    return xhat, rstd, xnb, gate, proj


def _ple_fwd(h, gamma, p_l, wgate, wproj, layer):
    S = h.shape[0]
    tm = _tile(S, 512)

    def body(h_ref, g_ref, p_ref, wgate_ref, wproj_ref, ho_ref):
        hf = h_ref[...]
        _, _, _, gate, proj = _ple_parts(hf, g_ref[...], p_ref[...], wgate_ref, wproj_ref)
        ho_ref[...] = hf + gate * proj

    return pl.pallas_call(
        body, name="ple_fwd",
        grid=(S // tm,),
        in_specs=[
            pl.BlockSpec((tm, D_MODEL), lambda t: (t, 0)),
            pl.BlockSpec((1, D_MODEL), lambda t: (0, 0)),
            pl.BlockSpec((tm, PLE_DIM), lambda t: (t, 0)),
            pl.BlockSpec((None, D_MODEL, D_MODEL), lambda t: (layer, 0, 0)),
            pl.BlockSpec((None, N_CHIPS, PLE_DIM, PLE_DIM), lambda t: (layer, 0, 0, 0)),
        ],
        out_specs=pl.BlockSpec((tm, D_MODEL), lambda t: (t, 0)),
        out_shape=jax.ShapeDtypeStruct((S, D_MODEL), F32),
        compiler_params=_cparams(("arbitrary",)),
    )(h, gamma, p_l, wgate, wproj)


def _ple_bwd(dout, h, gamma, p_l, wgate, wproj, layer):
    S = h.shape[0]
    tm = _tile(S, 512)

    def body(do_ref, h_ref, g_ref, p_ref, wgate_ref, wproj_ref, dh_ref, xn_ref, dpre_ref, dproj_ref, dgam_ref):
        do = do_ref[...]
        xhat, rstd, xnb, gate, proj = _ple_parts(h_ref[...], g_ref[...], p_ref[...], wgate_ref, wproj_ref)
        xn_ref[...] = xnb
        dproj_ref[...] = (do * gate).astype(BF16)
        dpre = (do * proj * gate * (1.0 - gate)).astype(BF16)
        dpre_ref[...] = dpre
        dhn, dgam = _rms_bwd(xhat, rstd, g_ref[...], _dot_nt(dpre, wgate_ref[...]))
        dh_ref[...] = do + dhn

        @pl.when(pl.program_id(0) == 0)
        def _():
            dgam_ref[...] = jnp.zeros_like(dgam_ref)

        dgam_ref[...] += dgam

    tile = pl.BlockSpec((tm, D_MODEL), lambda t: (t, 0))
    return pl.pallas_call(
        body, name="ple_bwd",
        grid=(S // tm,),
        in_specs=[
            tile, tile,
            pl.BlockSpec((1, D_MODEL), lambda t: (0, 0)),
            pl.BlockSpec((tm, PLE_DIM), lambda t: (t, 0)),
            pl.BlockSpec((None, D_MODEL, D_MODEL), lambda t: (layer, 0, 0)),
            pl.BlockSpec((None, N_CHIPS, PLE_DIM, PLE_DIM), lambda t: (layer, 0, 0, 0)),
        ],
        out_specs=[tile, tile, tile, tile, pl.BlockSpec((SUBLANES, D_MODEL), lambda t: (0, 0))],
        out_shape=[jax.ShapeDtypeStruct((S, D_MODEL), F32), jax.ShapeDtypeStruct((S, D_MODEL), BF16),
                   jax.ShapeDtypeStruct((S, D_MODEL), BF16), jax.ShapeDtypeStruct((S, D_MODEL), BF16),
                   jax.ShapeDtypeStruct((SUBLANES, D_MODEL), F32)],
        compiler_params=_cparams(("arbitrary",)),
    )(dout, h, gamma, p_l, wgate, wproj)


def _final(h, gamma, target):
    S = h.shape[0]
    tm = _tile(S, 512)

    def body(h_ref, g_ref, tgt_ref, dh_ref, dgam_ref, loss_ref):
        xhat, rstd, y = _rms(h_ref[...], g_ref[...])
        err = y - tgt_ref[...]
        dy = err * (1.0 / D_MODEL)
        dhn, dgam = _rms_bwd(xhat, rstd, g_ref[...], dy)
        dh_ref[...] = dhn
        sq = _rowsum8(err * err)
        part = sq[:, :LANES]
        for j in range(1, D_MODEL // LANES):
            part = part + sq[:, j * LANES:(j + 1) * LANES]

        @pl.when(pl.program_id(0) == 0)
        def _():
            dgam_ref[...] = jnp.zeros_like(dgam_ref)
            loss_ref[...] = jnp.zeros_like(loss_ref)

        dgam_ref[...] += dgam
        loss_ref[...] += part * (0.5 / D_MODEL)

    tile = pl.BlockSpec((tm, D_MODEL), lambda t: (t, 0))
    return pl.pallas_call(
        body, name="final_loss",
        grid=(S // tm,),
        in_specs=[tile, pl.BlockSpec((1, D_MODEL), lambda t: (0, 0)), tile],
        out_specs=[tile, pl.BlockSpec((SUBLANES, D_MODEL), lambda t: (0, 0)),
                   pl.BlockSpec((SUBLANES, LANES), lambda t: (0, 0))],
        out_shape=[jax.ShapeDtypeStruct((S, D_MODEL), F32), jax.ShapeDtypeStruct((SUBLANES, D_MODEL), F32),
                   jax.ShapeDtypeStruct((SUBLANES, LANES), F32)],
        compiler_params=_cparams(("arbitrary",)),
    )(h, gamma, target)


def _mesh_pos():
    return lax.axis_index("x"), lax.axis_index("y"), lax.axis_index("c")


def _other_chip(x, y, j):
    fx, fy = CHIP_FLIPS[j]
    return (1 - x if fx else x), (1 - y if fy else y)


def _any_specs(n):
    return [pl.BlockSpec(memory_space=pl.ANY)] * n


class _Cargo:
    def __init__(self):
        self.operands, self.out_shapes, self.aliases, self.sem_shapes, self.names = [], [], {}, [], []
        self.start = lambda ins, outs, sems: None
        self.finish = lambda ins, outs, sems: None


def _remote(src, dst, send, recv, device):
    return pltpu.make_async_remote_copy(src_ref=src, dst_ref=dst, send_sem=send, recv_sem=recv,
                                        device_id=device, device_id_type=MESH_ID)


def _gather_cargo(bufs, pieces):
    cargo = _Cargo()
    if not pieces:
        return cargo
    plist = []
    for name, layer in pieces:
        if name not in cargo.names:
            cargo.names.append(name)
            cargo.operands.append(bufs[name])
        plist.append((cargo.names.index(name), layer, bufs[name].shape[2] // 2))
    nflip = len(CHIP_FLIPS)
    cargo.out_shapes = [jax.ShapeDtypeStruct(b.shape, b.dtype) for b in cargo.operands]
    cargo.aliases = {i: i for i in range(len(cargo.operands))}
    cargo.sem_shapes = [pltpu.SemaphoreType.DMA((len(plist) * nflip,))] * 4

    def copies(outs, sems):
        send1, recv1, send2, recv2 = sems
        x, y, c = _mesh_pos()
        k = 2 * x + y

        def blk(p, chip, cc):
            b, layer, hrows = plist[p]
            return outs[b].at[layer, chip, pl.ds(cc * hrows, hrows), :]

        def chip_of(j):
            px, py = _other_chip(x, y, j)
            return 2 * px + py

        def ici(p, j):
            px, py = _other_chip(x, y, j)
            return _remote(blk(p, k, c), blk(p, k, c), send1.at[p * nflip + j], recv1.at[p * nflip + j], (px, py, c))

        def landed(p, j):
            px, py = _other_chip(x, y, j)
            return _remote(blk(p, k, c), blk(p, chip_of(j), c), send1.at[p * nflip + j], recv1.at[p * nflip + j],
                           (px, py, c))

        def d2d(p, j, cc):
            return _remote(blk(p, chip_of(j), cc), blk(p, chip_of(j), cc), send2.at[p * nflip + j],
                           recv2.at[p * nflip + j], (x, y, 1 - c))

        return c, ici, landed, d2d

    def start(ins, outs, sems):
        _, ici, _, _ = copies(outs, sems)
        for p in range(len(plist)):
            for j in range(nflip):
                ici(p, j).start()

    def finish(ins, outs, sems):
        c, ici, landed, d2d = copies(outs, sems)
        for j in range(nflip):
            for p in range(len(plist)):
                landed(p, j).wait_recv()
                d2d(p, j, c).start()
        for p in range(len(plist)):
            for j in range(nflip):
                ici(p, j).wait_send()
                d2d(p, j, c).wait_send()
                d2d(p, j, 1 - c).wait_recv()

    cargo.start, cargo.finish = start, finish
    return cargo


def _reduce_cargo(grads, presums):
    cargo = _Cargo()
    na, nb = len(grads), len(presums)
    nflip = len(CHIP_FLIPS)
    cargo.operands = list(grads) + list(presums)
    cargo.out_shapes = ([jax.ShapeDtypeStruct((g.shape[0], g.shape[1] // 2, g.shape[2]), g.dtype) for g in grads]
                        + [jax.ShapeDtypeStruct((nflip,) + ps.shape[1:], ps.dtype) for ps in presums])
    cargo.sem_shapes = ([pltpu.SemaphoreType.DMA((na,))] * 2 if na else []) + (
        [pltpu.SemaphoreType.DMA((nb * nflip,))] * 2 if nb else [])

    def copies(ins, outs, sems):
        x, y, c = _mesh_pos()
        out = []
        if na:
            send, recv = sems[0], sems[1]
            for a in range(na):
                hrows = grads[a].shape[1] // 2
                out.append(_remote(ins[a].at[:, pl.ds((1 - c) * hrows, hrows), :], outs[a], send.at[a], recv.at[a],
                                   (x, y, 1 - c)))
        if nb:
            send, recv = sems[-2], sems[-1]
            for b in range(nb):
                for j in range(nflip):
                    px, py = _other_chip(x, y, j)
                    out.append(_remote(ins[na + b].at[2 * px + py], outs[na + b].at[j], send.at[b * nflip + j],
                                       recv.at[b * nflip + j], (px, py, c)))
        return out

    def start(ins, outs, sems):
        for cp in copies(ins, outs, sems):
            cp.start()

    def finish(ins, outs, sems):
        for cp in copies(ins, outs, sems):
            cp.wait()

    cargo.start, cargo.finish = start, finish
    return cargo


def _run_cargo(name, cargo):
    nin, nout = len(cargo.operands), len(cargo.out_shapes)

    def body(*refs):
        ins, outs, sems = refs[:nin], refs[nin:nin + nout], refs[nin + nout:]
        cargo.start(ins, outs, sems)
        cargo.finish(ins, outs, sems)

    return list(pl.pallas_call(
        body, name=name,
        in_specs=_any_specs(nin), out_specs=_any_specs(nout), out_shape=cargo.out_shapes,
        input_output_aliases=dict(cargo.aliases), scratch_shapes=list(cargo.sem_shapes),
    )(*cargo.operands))


def _join_siblings(bufs):
    nb = len(bufs)
    items = [(b, layer) for b, buf in enumerate(bufs) for layer in range(buf.shape[0])]

    def body(*refs):
        outs = refs[nb:2 * nb]
        send, recv = refs[2 * nb:]
        x, y, c = _mesh_pos()

        def half(i, cc):
            b, layer = items[i]
            hrows = bufs[b].shape[1] // 2
            blk = outs[b].at[layer, pl.ds(cc * hrows, hrows), :]
            return _remote(blk, blk, send.at[i], recv.at[i], (x, y, 1 - c))

        for i in range(len(items)):
            half(i, c).start()
        for i in range(len(items)):
            half(i, c).wait_send()
            half(i, 1 - c).wait_recv()

    return list(pl.pallas_call(
        body, name="grad_sibling_join",
        in_specs=_any_specs(nb), out_specs=_any_specs(nb),
        out_shape=[jax.ShapeDtypeStruct(b.shape, b.dtype) for b in bufs],
        input_output_aliases={i: i for i in range(nb)},
        scratch_shapes=[pltpu.SemaphoreType.DMA((len(items),))] * 2,
    )(*bufs))


def _cast_place(w3, pos, dtype):
    L, rows, cols = w3.shape

    def body(pos_ref, w_ref, o_ref):
        o_ref[...] = w_ref[...].astype(dtype)

    return pl.pallas_call(
        body, name="cast_place",
        grid_spec=pltpu.PrefetchScalarGridSpec(
            num_scalar_prefetch=1, grid=(L,),
            in_specs=[pl.BlockSpec((None, rows, cols), lambda l, pos: (l, 0, 0))],
            out_specs=pl.BlockSpec((None, None, rows, cols), lambda l, pos: (l, pos[0], 0, 0))),
        out_shape=jax.ShapeDtypeStruct((L, N_CHIPS, rows, cols), dtype),
        compiler_params=_cparams(("arbitrary",)),
    )(pos, w3)


def _allreduce_small(buf):
    R = buf.shape[0]

    def body(in_ref, out_ref, land, send, recv):
        x, y, c = _mesh_pos()
        out_ref[...] = in_ref[...]
        for s, peer in enumerate(((x, y, 1 - c), (x, 1 - y, c), (1 - x, y, c))):
            cp = pltpu.make_async_remote_copy(
                src_ref=out_ref, dst_ref=land.at[s], send_sem=send.at[s], recv_sem=recv.at[s],
                device_id=peer, device_id_type=MESH_ID)
            cp.start()
            cp.wait()
            out_ref[...] = out_ref[...] + land[s]

    return pl.pallas_call(
        body, name="allreduce_small",
        in_specs=[pl.BlockSpec(memory_space=pltpu.VMEM)],
        out_specs=pl.BlockSpec(memory_space=pltpu.VMEM),
        out_shape=jax.ShapeDtypeStruct((R, LANES), F32),
        scratch_shapes=[pltpu.VMEM((3, R, LANES), F32), pltpu.SemaphoreType.DMA((3,)), pltpu.SemaphoreType.DMA((3,))],
        compiler_params=pltpu.CompilerParams(vmem_limit_bytes=VMEM_LIMIT_MB * 2 ** 20),
    )(buf)


def _presum_with_sibling(grad, landed, pos):
    nchunk, rows, cols = grad.shape
    hrows = rows // 2

    def body(pos_ref, g_ref, l_ref, all_ref, own_ref):
        s = g_ref[...] + l_ref[...]
        all_ref[...] = s.astype(BF16)

        @pl.when(pl.program_id(0) == pos_ref[0])
        def _():
            own_ref[...] = s

    return pl.pallas_call(
        body, name="grad_presum",
        grid_spec=pltpu.PrefetchScalarGridSpec(
            num_scalar_prefetch=1, grid=(nchunk,),
            in_specs=[pl.BlockSpec((None, hrows, cols), lambda k, pos: (k, pos[1], 0)),
                      pl.BlockSpec((None, hrows, cols), lambda k, pos: (k, 0, 0))],
            out_specs=[pl.BlockSpec((None, hrows, cols), lambda k, pos: (k, 0, 0)),
                       pl.BlockSpec((hrows, cols), lambda k, pos: (0, 0))]),
        out_shape=[jax.ShapeDtypeStruct((nchunk, hrows, cols), BF16), jax.ShapeDtypeStruct((hrows, cols), F32)],
        compiler_params=_cparams(("arbitrary",)),
    )(pos, grad, landed)


def _sum_chips(own, landed, stacked, layer, shape3, pos):
    hrows, cols = own.shape

    def body(pos_ref, o_ref, l_ref, *rest):
        s = o_ref[...]
        for j in range(len(CHIP_FLIPS)):
            s = s + l_ref[j].astype(F32)
        rest[-1][...] = s

    in_specs = [pl.BlockSpec((hrows, cols), lambda i, pos: (0, 0)),
                pl.BlockSpec((len(CHIP_FLIPS), hrows, cols), lambda i, pos: (0, 0, 0))]
    args = [pos, own, landed]
    aliases = {}
    if stacked is not None:
        in_specs.append(pl.BlockSpec(memory_space=pl.ANY))
        args.append(stacked)
        aliases = {3: 0}
    return pl.pallas_call(
        body, name="grad_sum_chips",
        grid_spec=pltpu.PrefetchScalarGridSpec(
            num_scalar_prefetch=1, grid=(1,), in_specs=in_specs,
            out_specs=pl.BlockSpec((None, hrows, cols), lambda i, pos: (layer, pos[1], 0))),
        out_shape=jax.ShapeDtypeStruct(shape3, F32),
        input_output_aliases=aliases,
        compiler_params=_cparams(("arbitrary",)),
    )(*args)


def _adamw(w, g, m, v):
    R, C = w.shape
    rb = R
    for cand in (512, 352, 320, 256, 128, 64, 32, 16, 8):
        if R % cand == 0:
            rb = cand
            break
    c1 = 1.0 - ADAM_B1 ** ADAM_STEP
    c2 = 1.0 - ADAM_B2 ** ADAM_STEP

    def body(w_ref, g_ref, m_ref, v_ref, d_ref, mo_ref, vo_ref):
        gv = g_ref[...]
        m2 = ADAM_B1 * m_ref[...] + (1.0 - ADAM_B1) * gv
        v2 = ADAM_B2 * v_ref[...] + (1.0 - ADAM_B2) * (gv * gv)
        mo_ref[...] = m2
        vo_ref[...] = v2
        d_ref[...] = -ADAM_LR * ((m2 / c1) / (jnp.sqrt(v2 / c2) + ADAM_EPS) + ADAM_WD * w_ref[...])

    spec = pl.BlockSpec((rb, C), lambda i: (i, 0))
    return pl.pallas_call(
        body, name="adamw",
        grid=(R // rb,),
        in_specs=[spec] * 4, out_specs=[spec] * 3,
        out_shape=[jax.ShapeDtypeStruct((R, C), F32)] * 3,
        compiler_params=_cparams(("arbitrary",)),
    )(w, g, m, v)


def _pack(parts, align=SUBLANES * LANES):
    flat = jnp.concatenate([p.reshape(-1).astype(F32) for p in parts])
    pad = (-flat.shape[0]) % align
    return jnp.pad(flat, (0, pad)).reshape(-1, LANES)


def _unpack(buf, shapes):
    flat = buf.reshape(-1)
    out, off = [], 0
    for shp in shapes:
        size = 1
        for d in shp:
            size *= d
        out.append(flat[off:off + size].reshape(shp))
        off += size
    return out


def _block_diag_gates(w_a, w_x):
    nq = D_RNN // GATE_CHUNK
    hpc = LRU_HEADS // nq
    eye = jnp.eye(hpc, dtype=F32)

    def bd(w):
        wq = w.reshape(nq, hpc, LRU_HEAD_DIM, LRU_HEAD_DIM)
        return (wq[:, :, :, None, :] * eye[None, :, None, :, None]).reshape(nq, GATE_CHUNK, GATE_CHUNK)

    return jnp.concatenate([bd(w_a), bd(w_x)], axis=2).astype(BF16)


def _block_diag_extract(dwbd):
    nq = D_RNN // GATE_CHUNK
    hpc = LRU_HEADS // nq
    eye = jnp.eye(hpc, dtype=F32)

    def ex(d):
        d5 = d.reshape(nq, hpc, LRU_HEAD_DIM, hpc, LRU_HEAD_DIM)
        return jnp.sum(d5 * eye[None, :, None, :, None], axis=3).reshape(LRU_HEADS, LRU_HEAD_DIM, LRU_HEAD_DIM)

    return ex(dwbd[:, :, :GATE_CHUNK]), ex(dwbd[:, :, GATE_CHUNK:])


BIG = ("ffn1_w_gate", "ffn1_w_up", "ffn1_w_down", "lru_w_in", "lru_w_out", "pool_w",
       "ffn2_w_gate", "ffn2_w_up", "ffn2_w_down", "ple_w_gate", "ple_w_proj")
TINY_SHARDED = ("lru_conv_w", "pool_b", "pool_scale")
REPLICATED = ("ffn1_norm", "mix_norm", "lru_conv_b", "lru_w_a", "lru_b_a", "lru_w_x", "lru_b_x", "lru_a_param",
              "ffn2_norm", "ple_norm", "final_norm")
WEIGHT_ORDER = ("ffn1_norm", "ffn1_w_gate", "ffn1_w_up", "ffn1_w_down", "mix_norm", "lru_w_in", "lru_conv_w",
                "lru_conv_b", "lru_w_a", "lru_b_a", "lru_w_x", "lru_b_x", "lru_a_param", "lru_w_out", "pool_w",
                "pool_b", "pool_scale", "ffn2_norm", "ffn2_w_gate", "ffn2_w_up", "ffn2_w_down", "ple_norm",
                "ple_w_gate", "ple_w_proj", "final_norm")


def _as3(a):
    return a.reshape(a.shape[0], -1, a.shape[-1])


def kernel(x, p, ffn1_norm, ffn1_w_gate, ffn1_w_up, ffn1_w_down, mix_norm, lru_w_in, lru_conv_w, lru_conv_b, lru_w_a, lru_b_a, lru_w_x, lru_b_x, lru_a_param, lru_w_out, pool_w, pool_b, pool_scale, ffn2_norm, ffn2_w_gate, ffn2_w_up, ffn2_w_down, ple_norm, ple_w_gate, ple_w_proj, final_norm, loss_target, m_ffn1_norm, m_ffn1_w_gate, m_ffn1_w_up, m_ffn1_w_down, m_mix_norm, m_lru_w_in, m_lru_conv_w, m_lru_conv_b, m_lru_w_a, m_lru_b_a, m_lru_w_x, m_lru_b_x, m_lru_a_param, m_lru_w_out, m_pool_w, m_pool_b, m_pool_scale, m_ffn2_norm, m_ffn2_w_gate, m_ffn2_w_up, m_ffn2_w_down, m_ple_norm, m_ple_w_gate, m_ple_w_proj, m_final_norm, v_ffn1_norm, v_ffn1_w_gate, v_ffn1_w_up, v_ffn1_w_down, v_mix_norm, v_lru_w_in, v_lru_conv_w, v_lru_conv_b, v_lru_w_a, v_lru_b_a, v_lru_w_x, v_lru_b_x, v_lru_a_param, v_lru_w_out, v_pool_w, v_pool_b, v_pool_scale, v_ffn2_norm, v_ffn2_w_gate, v_ffn2_w_up, v_ffn2_w_down, v_ple_norm, v_ple_w_gate, v_ple_w_proj, v_final_norm):
    W = dict(ffn1_norm=ffn1_norm, ffn1_w_gate=ffn1_w_gate, ffn1_w_up=ffn1_w_up, ffn1_w_down=ffn1_w_down,
             mix_norm=mix_norm, lru_w_in=lru_w_in, lru_conv_w=lru_conv_w, lru_conv_b=lru_conv_b, lru_w_a=lru_w_a,
             lru_b_a=lru_b_a, lru_w_x=lru_w_x, lru_b_x=lru_b_x, lru_a_param=lru_a_param, lru_w_out=lru_w_out,
             pool_w=pool_w, pool_b=pool_b, pool_scale=pool_scale, ffn2_norm=ffn2_norm, ffn2_w_gate=ffn2_w_gate,
             ffn2_w_up=ffn2_w_up, ffn2_w_down=ffn2_w_down, ple_norm=ple_norm, ple_w_gate=ple_w_gate,
             ple_w_proj=ple_w_proj, final_norm=final_norm)
    M = dict(ffn1_norm=m_ffn1_norm, ffn1_w_gate=m_ffn1_w_gate, ffn1_w_up=m_ffn1_w_up, ffn1_w_down=m_ffn1_w_down,
             mix_norm=m_mix_norm, lru_w_in=m_lru_w_in, lru_conv_w=m_lru_conv_w, lru_conv_b=m_lru_conv_b,
             lru_w_a=m_lru_w_a, lru_b_a=m_lru_b_a, lru_w_x=m_lru_w_x, lru_b_x=m_lru_b_x, lru_a_param=m_lru_a_param,
             lru_w_out=m_lru_w_out, pool_w=m_pool_w, pool_b=m_pool_b, pool_scale=m_pool_scale, ffn2_norm=m_ffn2_norm,
             ffn2_w_gate=m_ffn2_w_gate, ffn2_w_up=m_ffn2_w_up, ffn2_w_down=m_ffn2_w_down, ple_norm=m_ple_norm,
             ple_w_gate=m_ple_w_gate, ple_w_proj=m_ple_w_proj, final_norm=m_final_norm)
    V = dict(ffn1_norm=v_ffn1_norm, ffn1_w_gate=v_ffn1_w_gate, ffn1_w_up=v_ffn1_w_up, ffn1_w_down=v_ffn1_w_down,
             mix_norm=v_mix_norm, lru_w_in=v_lru_w_in, lru_conv_w=v_lru_conv_w, lru_conv_b=v_lru_conv_b,
             lru_w_a=v_lru_w_a, lru_b_a=v_lru_b_a, lru_w_x=v_lru_w_x, lru_b_x=v_lru_b_x, lru_a_param=v_lru_a_param,
             lru_w_out=v_lru_w_out, pool_w=v_pool_w, pool_b=v_pool_b, pool_scale=v_pool_scale, ffn2_norm=v_ffn2_norm,
             ffn2_w_gate=v_ffn2_w_gate, ffn2_w_up=v_ffn2_w_up, ffn2_w_down=v_ffn2_w_down, ple_norm=v_ple_norm,
             ple_w_gate=v_ple_w_gate, ple_w_proj=v_ple_w_proj, final_norm=v_final_norm)

    S = x.shape[1]
    my_x, my_y, my_c = _mesh_pos()
    my_chip = 2 * my_x + my_y
    pos = jnp.stack([my_chip, my_c]).astype(jnp.int32)
    n_lru, n_pool = lru_w_in.shape[0], pool_w.shape[0]

    tiny_shapes = [W[n].shape for n in TINY_SHARDED]
    tiny_local = _pack([W[n] for n in TINY_SHARDED], align=2 * 16 * LANES)[None]
    bufs = {n: _cast_place(_as3(W[n]), pos, BF16) for n in BIG}
    bufs["tiny"] = _cast_place(tiny_local, pos, F32)

    def gather_now(name, pieces):
        cargo = _gather_cargo(bufs, pieces)
        bufs.update(zip(cargo.names, _run_cargo(name, cargo)))

    def ffn_pieces(which, layer):
        return [("%s_w_gate" % which, layer), ("%s_w_up" % which, layer), ("%s_w_down" % which, layer)]

    def mixer_pieces(layer):
        if layer % 2 == 0:
            return [("lru_w_in", layer // 2), ("lru_w_out", layer // 2)]
        return [("pool_w", layer // 2)]

    gather_now("gather_first", [("tiny", 0)] + ffn_pieces("ffn1", 0) + mixer_pieces(0))
    tiny_by_chip = [_unpack(bufs["tiny"][0, k], tiny_shapes) for k in range(N_CHIPS)]
    conv_w_full = jnp.concatenate([tiny_by_chip[k][0] for k in range(N_CHIPS)], axis=-1)
    pool_b_full = jnp.concatenate([tiny_by_chip[k][1] for k in range(N_CHIPS)], axis=-1)
    pool_s_full = jnp.concatenate([tiny_by_chip[k][2] for k in range(N_CHIPS)], axis=-1)
    ngroup = len(POOL_WINDOWS)

    def pool_weights():
        pw5 = bufs["pool_w"].reshape(n_pool, N_CHIPS, ngroup, POOL_GROUP_DIM // N_CHIPS, POOL_GROUP_DIM)
        return pw5.transpose(0, 2, 1, 3, 4).reshape(n_pool, ngroup, POOL_GROUP_DIM, POOL_GROUP_DIM)

    lru_out = lambda: bufs["lru_w_out"].reshape(n_lru, D_RNN, D_MODEL)
    ple_gate = lambda: bufs["ple_w_gate"].reshape(DEPTH, D_MODEL, D_MODEL)
    wbd = [_block_diag_gates(lru_w_a[j], lru_w_x[j]) for j in range(n_lru)]
    row = lambda a: a.reshape(1, -1)

    def ffn_forward(which, h, gamma, layer, pieces):
        cargo = _gather_cargo(bufs, pieces)
        outs, updated = _ffn_fwd(h, gamma, bufs[which + "_w_gate"], bufs[which + "_w_up"], bufs[which + "_w_down"],
                                 layer, cargo)
        bufs.update(zip(cargo.names, updated))
        return outs

    h = x.reshape(S, D_MODEL)
    saved = []
    for i in range(DEPTH):
        j = i // 2
        sv = {"h0": h}
        h, sv["xn1"], sv["g1"], sv["u1"] = ffn_forward(
            "ffn1", h, row(ffn1_norm[i]), i, ffn_pieces("ffn2", i) + [("ple_w_gate", i), ("ple_w_proj", i)])
        sv["h1"] = h
        if i % 2 == 0:
            h, sv["xn_mix"], sv["z"], sv["hs"] = _lru_fwd(
                h, row(mix_norm[i]), bufs["lru_w_in"], j, conv_w_full[j], row(lru_conv_b[j]), wbd[j],
                row(lru_b_a[j]), row(lru_b_x[j]), row(lru_a_param[j]), lru_out())
        else:
            h, sv["u"] = _pool_fwd(h, row(mix_norm[i]), pool_weights(), row(pool_b_full[j]), row(pool_s_full[j]), j)
        sv["h2"] = h
        nxt = ffn_pieces("ffn1", i + 1) + mixer_pieces(i + 1) if i + 1 < DEPTH else []
        h, sv["xn2"], sv["g2"], sv["u2"] = ffn_forward("ffn2", h, row(ffn2_norm[i]), i, nxt)
        sv["h3"] = h
        sv["p"] = p[i, 0]
        h = _ple_fwd(h, row(ple_norm[i]), sv["p"], ple_gate(), bufs["ple_w_proj"], i)
        saved.append(sv)

    dh, dgam_final, loss_part = _final(h, row(final_norm), loss_target.reshape(S, D_MODEL))
    win, wout, wpg, wpp, pw = bufs["lru_w_in"], lru_out(), ple_gate(), bufs["ple_w_proj"], pool_weights()

    norm_grads = {n: [None] * DEPTH for n in ("ffn1_norm", "mix_norm", "ffn2_norm", "ple_norm")}
    lru_vec = [None] * n_lru
    pool_vec = [None] * n_pool
    sum8 = lambda a: jnp.sum(a, axis=-2)

    to_siblings, to_chips = [], []
    stacked = {n: None for n in BIG}

    def take_cargo():
        a_items, b_items = list(to_siblings), list(to_chips)
        del to_siblings[:], to_chips[:]
        return _reduce_cargo([it[2] for it in a_items], [it[2] for it in b_items]), a_items, b_items

    def absorb(a_items, b_items, outs):
        for (n, layer, g), landed in zip(a_items, outs[:len(a_items)]):
            all_chunks, own = _presum_with_sibling(g, landed, pos)
            to_chips.append((n, layer, all_chunks, own))
        for (n, layer, _, own), from_chips in zip(b_items, outs[len(a_items):]):
            stacked[n] = _sum_chips(own, from_chips, stacked[n], layer, _as3(W[n]).shape, pos)

    def ffn_backward(which, xn, dout, gg, uu, layer):
        cargo, a_items, b_items = take_cargo()
        (dwg, dwu, dwd, slabs), c_outs = _ffn_bwd(xn, dout, gg, uu, bufs[which + "_w_gate"], bufs[which + "_w_up"],
                                                  bufs[which + "_w_down"], layer, cargo)
        absorb(a_items, b_items, c_outs)
        to_siblings.extend([(which + "_w_gate", layer, dwg), (which + "_w_up", layer, dwu),
                            (which + "_w_down", layer, dwd)])
        return slabs

    for i in reversed(range(DEPTH)):
        j = i // 2
        sv = saved[i]
        dh, xn_p, dpre_p, dproj_p, dgam = _ple_bwd(dh, sv["h3"], row(ple_norm[i]), sv["p"], wpg, wpp, i)
        norm_grads["ple_norm"][i] = sum8(dgam)
        to_siblings.append(("ple_w_gate", i, _xt_dy("ple_dw_gate", xn_p, dpre_p, 1, D_MODEL, D_MODEL, False, False)
                            .reshape(N_CHIPS, D_MODEL // N_CHIPS, D_MODEL)))
        to_siblings.append(("ple_w_proj", i,
                            _xt_dy("ple_dw_proj", sv["p"], dproj_p, N_CHIPS, PLE_DIM, PLE_DIM, False, True)))

        slabs = ffn_backward("ffn2", sv["xn2"], dh, sv["g2"], sv["u2"], i)
        dh, dgam = _norm_bwd_combine(dh, slabs, sv["h2"], row(ffn2_norm[i]))
        norm_grads["ffn2_norm"][i] = sum8(dgam)

        if i % 2 == 0:
            dz, dpre, xc_b, y_b, dcw, vec = _lru_bwd_seq(
                dh, sv["z"], sv["hs"], conv_w_full[j], row(lru_conv_b[j]), wbd[j], row(lru_b_a[j]),
                row(lru_b_x[j]), row(lru_a_param[j]), wout, j)
            to_siblings.append(("lru_w_out", j, _xt_dy("lru_dw_out", y_b, dh, 1, D_RNN, D_MODEL, False, False)
                                .reshape(N_CHIPS, D_RNN // N_CHIPS, D_MODEL)))
            to_siblings.append(("lru_w_in", j, _xt_dy("lru_dw_in", sv["xn_mix"], dz, N_CHIPS, D_MODEL, RNN_IN_CHUNK,
                                                      False, True)))
            dwbd = _xt_dy("lru_dw_gates", xc_b, dpre, D_RNN // GATE_CHUNK, GATE_CHUNK, 2 * GATE_CHUNK, True, True)
            dw_a, dw_x = _block_diag_extract(dwbd)
            vsum = sum8(vec)
            lru_vec[j] = (sum8(dcw), vsum[0], vsum[1], vsum[2], vsum[3], dw_a, dw_x)
            dh, dgam = _lru_bwd_in(dz, sv["h1"], row(mix_norm[i]), dh, win, j)
            norm_grads["mix_norm"][i] = sum8(dgam)
        else:
            dh_new, dpre_b, vec = _pool_bwd(dh, sv["h1"], sv["u"], row(mix_norm[i]), pw, row(pool_b_full[j]),
                                            row(pool_s_full[j]), j)
            dpw = _xt_dy("pool_dw", sv["u"], dpre_b, ngroup, POOL_GROUP_DIM, POOL_GROUP_DIM, True, True)
            dpw = dpw.reshape(ngroup, N_CHIPS, POOL_GROUP_DIM // N_CHIPS, POOL_GROUP_DIM).transpose(1, 0, 2, 3)
            to_siblings.append(("pool_w", j, dpw.reshape(N_CHIPS, POOL_GROUP_DIM, POOL_GROUP_DIM)))
            vsum = sum8(vec)
            norm_grads["mix_norm"][i] = vsum[0]
            pool_vec[j] = (vsum[1], vsum[2])
            dh = dh_new

        slabs = ffn_backward("ffn1", sv["xn1"], dh, sv["g1"], sv["u1"], i)
        dh, dgam = _norm_bwd_combine(dh, slabs, sv["h0"], row(ffn1_norm[i]))
        norm_grads["ffn1_norm"][i] = sum8(dgam)

    grad_x = dh.reshape(1, S, D_MODEL)

    tail = 0
    while to_siblings or to_chips:
        cargo, a_items, b_items = take_cargo()
        absorb(a_items, b_items, _run_cargo("grad_exchange_tail%d" % tail, cargo))
        tail += 1
    big_final = dict(zip(BIG, _join_siblings([stacked[n] for n in BIG])))

    small_parts = [
        jnp.stack(norm_grads["ffn1_norm"]), jnp.stack(norm_grads["mix_norm"]),
        jnp.stack(norm_grads["ffn2_norm"]), jnp.stack(norm_grads["ple_norm"]), sum8(dgam_final),
        jnp.stack([lv[0] for lv in lru_vec]), jnp.stack([lv[1] for lv in lru_vec]),
        jnp.stack([lv[2] for lv in lru_vec]), jnp.stack([lv[3] for lv in lru_vec]),
        jnp.stack([lv[4] for lv in lru_vec]), jnp.stack([lv[5] for lv in lru_vec]),
        jnp.stack([lv[6] for lv in lru_vec]),
        jnp.stack([pv[0] for pv in pool_vec]), jnp.stack([pv[1] for pv in pool_vec]),
        jnp.sum(loss_part).reshape(1),
    ]
    small_names = ("ffn1_norm", "mix_norm", "ffn2_norm", "ple_norm", "final_norm", "lru_conv_w", "lru_conv_b",
                   "lru_b_a", "lru_b_x", "lru_a_param", "lru_w_a", "lru_w_x", "pool_b", "pool_scale", "loss")
    reduced = _unpack(_allreduce_small(_pack(small_parts)), [sp.shape for sp in small_parts])
    small_grad = dict(zip(small_names, reduced))
    loss = small_grad.pop("loss").reshape(())
    for n in TINY_SHARDED:
        width = W[n].shape[-1]
        small_grad[n] = lax.dynamic_slice_in_dim(small_grad[n], my_chip * width, width, axis=-1)

    grads, deltas, new_m, new_v = {}, {}, {}, {}
    for n in BIG:
        shp = W[n].shape
        to2 = lambda a: a.reshape(-1, shp[-1])
        g2 = to2(big_final[n])
        d, m2, v2 = _adamw(to2(W[n]), g2, to2(M[n]), to2(V[n]))
        grads[n], deltas[n], new_m[n], new_v[n] = (a.reshape(shp) for a in (g2, d, m2, v2))
    small_order = TINY_SHARDED + REPLICATED
    small_shapes = [W[n].shape for n in small_order]
    pack_rows = functools.partial(_pack, align=512 * LANES)
    sd, sm, sv_ = _adamw(pack_rows([W[n] for n in small_order]), pack_rows([small_grad[n] for n in small_order]),
                         pack_rows([M[n] for n in small_order]), pack_rows([V[n] for n in small_order]))
    for n, d, m2, v2 in zip(small_order, _unpack(sd, small_shapes), _unpack(sm, small_shapes),
                            _unpack(sv_, small_shapes)):
        grads[n], deltas[n], new_m[n], new_v[n] = small_grad[n].reshape(W[n].shape), d, m2, v2

    return (loss, grad_x, *[grads[n] for n in WEIGHT_ORDER], *[deltas[n] for n in WEIGHT_ORDER],
            *[new_m[n] for n in WEIGHT_ORDER], *[new_v[n] for n in WEIGHT_ORDER])
```

```python
import functools

import jax
import jax.numpy as jnp
from jax import lax
from jax.experimental import pallas as pl
from jax.experimental.pallas import tpu as pltpu

F32 = jnp.float32
BF16 = jnp.bfloat16

D_MODEL = 1024
D_FF = 2816
D_RNN = 1280
DEPTH = 4
N_CHIPS = 4
FF_CHUNK = D_FF // N_CHIPS
RNN_IN_CHUNK = 2 * D_RNN // N_CHIPS
GATE_CHUNK = 640
LRU_HEADS = 16
LRU_HEAD_DIM = 80
CONV_WIDTH = 4
LRU_C = 8.0
POOL_WINDOWS = (2, 4, 8, 16)
POOL_GROUP_DIM = 256
PLE_DIM = 256
RMS_EPS = 1e-6
POOL_HALO = 16
SUBLANES = 8
LANES = 128

ADAM_LR = 0.001
ADAM_B1 = 0.9
ADAM_B2 = 0.999
ADAM_EPS = 1e-08
ADAM_WD = 0.01
ADAM_STEP = 10

VMEM_LIMIT_MB = 56
MESH_ID = pl.DeviceIdType.MESH
CHIP_FLIPS = ((1, 0), (0, 1), (1, 1))


def _cparams(semantics):
    return pltpu.CompilerParams(dimension_semantics=semantics, vmem_limit_bytes=VMEM_LIMIT_MB * 2 ** 20)


def _dot(a, b):
    return lax.dot_general(a, b, (((1,), (0,)), ((), ())), preferred_element_type=F32)


def _dot_nt(a, b):
    return lax.dot_general(a, b, (((1,), (1,)), ((), ())), preferred_element_type=F32)


def _dot_tn(a, b):
    return lax.dot_general(a, b, (((0,), (0,)), ((), ())), preferred_element_type=F32)


def _sigmoid(x):
    return 1.0 / (1.0 + jnp.exp(-x))


def _rms(hf, gamma):
    rstd = lax.rsqrt(jnp.mean(hf * hf, axis=-1, keepdims=True) + RMS_EPS)
    xhat = hf * rstd
    return xhat, rstd, xhat * gamma


def _rms_bwd(xhat, rstd, gamma, dxn):
    dxhat = dxn * gamma
    m = jnp.mean(dxhat * xhat, axis=-1, keepdims=True)
    return rstd * (dxhat - xhat * m), _rowsum8(dxn * xhat)


def _rowsum8(v):
    tm, n = v.shape
    return jnp.sum(v.reshape(tm // SUBLANES, SUBLANES, n), axis=0)


def _gelu(x):
    u = 0.7978845608028654 * (x + 0.044715 * x * x * x)
    return 0.5 * x * (1.0 + jnp.tanh(u))


def _gelu_and_grad(x):
    c = 0.7978845608028654
    u = c * (x + 0.044715 * x * x * x)
    th = jnp.tanh(u)
    g = 0.5 * x * (1.0 + th)
    dg = 0.5 * (1.0 + th) + 0.5 * x * (1.0 - th * th) * c * (1.0 + 3.0 * 0.044715 * x * x)
    return g, dg


def _softplus(z):
    e = jnp.exp(-jnp.abs(z))
    u = 1.0 + e
    log1p = jnp.where(u == 1.0, e, jnp.log(u) * e / jnp.where(u == 1.0, 1.0, u - 1.0))
    return jnp.maximum(z, 0.0) + log1p


def _neg_expm1(x):
    series = -(x + 0.5 * x * x + (1.0 / 6.0) * x * x * x)
    return jnp.where(x > -1e-2, series, 1.0 - jnp.exp(x))


def _shift_down(ext, j, halo):
    return pltpu.roll(ext, j, 0)[halo:]


def _shift_up(ext, j, tm):
    n = ext.shape[0]
    return pltpu.roll(ext, n - j, 0)[:tm]


def _scan_causal(a, b):
    tm, n = a.shape
    head_rows = lax.broadcasted_iota(jnp.int32, (SUBLANES, n), 0)
    s = 1
    while s < min(SUBLANES, tm):
        keep = head_rows >= s
        a_r, b_r = pltpu.roll(a, s, 0), pltpu.roll(b, s, 0)
        a_sh = jnp.concatenate([jnp.where(keep, a_r[:SUBLANES], 1.0), a_r[SUBLANES:]], axis=0)
        b_sh = jnp.concatenate([jnp.where(keep, b_r[:SUBLANES], 0.0), b_r[SUBLANES:]], axis=0)
        b = a * b_sh + b
        a = a * a_sh
        s *= 2
    while s < tm:
        b = jnp.concatenate([b[:s], a[s:] * b[:tm - s] + b[s:]], axis=0)
        a = jnp.concatenate([a[:s], a[s:] * a[:tm - s]], axis=0)
        s *= 2
    return a, b


def _scan_anticausal(c, d):
    tm, n = c.shape
    body = tm - SUBLANES
    tail_rows = lax.broadcasted_iota(jnp.int32, (SUBLANES, n), 0) + body
    s = 1
    while s < min(SUBLANES, tm):
        keep = tail_rows < tm - s
        c_r, d_r = pltpu.roll(c, tm - s, 0), pltpu.roll(d, tm - s, 0)
        c_sh = jnp.concatenate([c_r[:body], jnp.where(keep, c_r[body:], 1.0)], axis=0)
        d_sh = jnp.concatenate([d_r[:body], jnp.where(keep, d_r[body:], 0.0)], axis=0)
        d = d + c * d_sh
        c = c * c_sh
        s *= 2
    while s < tm:
        d = jnp.concatenate([d[:tm - s] + c[:tm - s] * d[s:], d[tm - s:]], axis=0)
        c = jnp.concatenate([c[:tm - s] * c[s:], c[tm - s:]], axis=0)
        s *= 2
    return c, d


def _tile(n, want):
    t = min(n, want)
    assert n % t == 0, (n, t)
    return t


def _ffn_fwd(h, gamma, wg, wu, wd, layer, cargo=None):
    S = h.shape[0]
    tm = _tile(S, 1024)
    nt = S // tm
    cargo = cargo or _Cargo()
    n_in, n_out = 5, 4
    nc_in, nc_out = len(cargo.operands), len(cargo.out_shapes)

    def body(*refs):
        h_ref, g_ref, wg_ref, wu_ref, wd_ref = refs[:n_in]
        c_ins = refs[n_in:n_in + nc_in]
        ho_ref, xn_ref, gg_ref, uu_ref = refs[n_in + nc_in:n_in + nc_in + n_out]
        c_outs = refs[n_in + nc_in + n_out:n_in + nc_in + n_out + nc_out]
        xn_s, acc_s = refs[n_in + nc_in + n_out + nc_out:n_in + nc_in + n_out + nc_out + 2]
        sems = refs[n_in + nc_in + n_out + nc_out + 2:]
        t, k = pl.program_id(0), pl.program_id(1)

        @pl.when((t == 0) & (k == 0))
        def _():
            cargo.start(c_ins, c_outs, sems)

        @pl.when(k == 0)
        def _():
            _, _, xn = _rms(h_ref[...], g_ref[...])
            xnb = xn.astype(BF16)
            xn_s[...] = xnb
            xn_ref[...] = xnb
            acc_s[...] = jnp.zeros_like(acc_s)

        xnb = xn_s[...]
        g = _dot(xnb, wg_ref[...])
        u = _dot(xnb, wu_ref[...])
        gg_ref[...] = g.astype(BF16)
        uu_ref[...] = u.astype(BF16)
        hid = (g * _sigmoid(g)) * u
        acc_s[...] += _dot(hid.astype(BF16), wd_ref[...])

        @pl.when(k == N_CHIPS - 1)
        def _():
            ho_ref[...] = h_ref[...] + 0.5 * acc_s[...]

        @pl.when((t == nt - 1) & (k == N_CHIPS - 1))
        def _():
            cargo.finish(c_ins, c_outs, sems)

    outs = pl.pallas_call(
        body, name="ffn_fwd",
        grid=(nt, N_CHIPS),
        in_specs=[
            pl.BlockSpec((tm, D_MODEL), lambda t, k: (t, 0)),
            pl.BlockSpec((1, D_MODEL), lambda t, k: (0, 0)),
            pl.BlockSpec((None, None, D_MODEL, FF_CHUNK), lambda t, k: (layer, k, 0, 0)),
            pl.BlockSpec((None, None, D_MODEL, FF_CHUNK), lambda t, k: (layer, k, 0, 0)),
            pl.BlockSpec((None, None, FF_CHUNK, D_MODEL), lambda t, k: (layer, k, 0, 0)),
        ] + _any_specs(nc_in),
        out_specs=[
            pl.BlockSpec((tm, D_MODEL), lambda t, k: (t, 0)),
            pl.BlockSpec((tm, D_MODEL), lambda t, k: (t, 0)),
            pl.BlockSpec((None, tm, FF_CHUNK), lambda t, k: (k, t, 0)),
            pl.BlockSpec((None, tm, FF_CHUNK), lambda t, k: (k, t, 0)),
        ] + _any_specs(nc_out),
        out_shape=[
            jax.ShapeDtypeStruct((S, D_MODEL), F32),
            jax.ShapeDtypeStruct((S, D_MODEL), BF16),
            jax.ShapeDtypeStruct((N_CHIPS, S, FF_CHUNK), BF16),
            jax.ShapeDtypeStruct((N_CHIPS, S, FF_CHUNK), BF16),
        ] + cargo.out_shapes,
        input_output_aliases={n_in + i: n_out + o for i, o in cargo.aliases.items()},
        scratch_shapes=[pltpu.VMEM((tm, D_MODEL), BF16), pltpu.VMEM((tm, D_MODEL), F32)] + cargo.sem_shapes,
        compiler_params=_cparams(("arbitrary", "arbitrary")),
    )(h, gamma, wg, wu, wd, *cargo.operands)
    return outs[:n_out], list(outs[n_out:])


def _ffn_bwd(xn, dout, gg, uu, wg, wu, wd, layer, cargo=None):
    S = xn.shape[0]
    tm = _tile(S, 512)
    nt = S // tm
    nchunk = N_CHIPS - 1
    cargo = cargo or _Cargo()
    n_in, n_out = 7, 4
    nc_in, nc_out = len(cargo.operands), len(cargo.out_shapes)

    def body(*refs):
        xn_ref, do_ref, gg_ref, uu_ref, wg_ref, wu_ref, wd_ref = refs[:n_in]
        c_ins = refs[n_in:n_in + nc_in]
        dwg_ref, dwu_ref, dwd_ref, slab_ref = refs[n_in + nc_in:n_in + nc_in + n_out]
        c_outs = refs[n_in + nc_in + n_out:n_in + nc_in + n_out + nc_out]
        sems = refs[n_in + nc_in + n_out + nc_out:]
        k, t = pl.program_id(0), pl.program_id(1)

        @pl.when((k == 0) & (t == 0))
        def _():
            cargo.start(c_ins, c_outs, sems)

        @pl.when(t == 0)
        def _():
            dwg_ref[...] = jnp.zeros_like(dwg_ref)
            dwu_ref[...] = jnp.zeros_like(dwu_ref)
            dwd_ref[...] = jnp.zeros_like(dwd_ref)

        xnb = xn_ref[...]
        dob = (0.5 * do_ref[...]).astype(BF16)
        g = gg_ref[...].astype(F32)
        u = uu_ref[...].astype(F32)
        s = _sigmoid(g)
        sil = g * s
        dhid = _dot_nt(dob, wd_ref[...])
        dwd_ref[...] += _dot_tn((sil * u).astype(BF16), dob)
        du = (dhid * sil).astype(BF16)
        dg = (dhid * u * (s * (1.0 + g * (1.0 - s)))).astype(BF16)
        dwg_ref[...] += _dot_tn(xnb, dg)
        dwu_ref[...] += _dot_tn(xnb, du)
        slab_ref[...] = (_dot_nt(dg, wg_ref[...]) + _dot_nt(du, wu_ref[...])).astype(BF16)

        @pl.when((k == nchunk - 1) & (t == nt - 1))
        def _():
            cargo.finish(c_ins, c_outs, sems)

    outs = pl.pallas_call(
        body, name="ffn_bwd",
        grid=(nchunk, nt),
        in_specs=[
            pl.BlockSpec((tm, D_MODEL), lambda k, t: (t, 0)),
            pl.BlockSpec((tm, D_MODEL), lambda k, t: (t, 0)),
            pl.BlockSpec((None, tm, FF_CHUNK), lambda k, t: (k, t, 0)),
            pl.BlockSpec((None, tm, FF_CHUNK), lambda k, t: (k, t, 0)),
            pl.BlockSpec((None, None, D_MODEL, FF_CHUNK), lambda k, t: (layer, k, 0, 0)),
            pl.BlockSpec((None, None, D_MODEL, FF_CHUNK), lambda k, t: (layer, k, 0, 0)),
            pl.BlockSpec((None, None, FF_CHUNK, D_MODEL), lambda k, t: (layer, k, 0, 0)),
        ] + _any_specs(nc_in),
        out_specs=[
            pl.BlockSpec((None, D_MODEL, FF_CHUNK), lambda k, t: (k, 0, 0)),
            pl.BlockSpec((None, D_MODEL, FF_CHUNK), lambda k, t: (k, 0, 0)),
            pl.BlockSpec((None, FF_CHUNK, D_MODEL), lambda k, t: (k, 0, 0)),
            pl.BlockSpec((None, tm, D_MODEL), lambda k, t: (k, t, 0)),
        ] + _any_specs(nc_out),
        out_shape=[
            jax.ShapeDtypeStruct((N_CHIPS, D_MODEL, FF_CHUNK), F32),
            jax.ShapeDtypeStruct((N_CHIPS, D_MODEL, FF_CHUNK), F32),
            jax.ShapeDtypeStruct((N_CHIPS, FF_CHUNK, D_MODEL), F32),
            jax.ShapeDtypeStruct((nchunk, S, D_MODEL), BF16),
        ] + cargo.out_shapes,
        input_output_aliases={n_in + i: n_out + o for i, o in cargo.aliases.items()},
        scratch_shapes=list(cargo.sem_shapes),
        compiler_params=_cparams(("arbitrary", "arbitrary")),
    )(xn, dout, gg, uu, wg, wu, wd, *cargo.operands)
    return outs[:n_out], list(outs[n_out:])


def _ffn_bwd_last(xn, dout, gg, uu, wg, wu, wd, layer, slabs, h, gamma, dwg, dwu, dwd):
    S = xn.shape[0]
    tm = _tile(S, 512)
    k = N_CHIPS - 1
    nprev = slabs.shape[0]

    def body(xn_ref, do_ref, gg_ref, uu_ref, wg_ref, wu_ref, wd_ref, slab_ref, h_ref, g_ref, _dwg, _dwu, _dwd,
             dh_ref, dgam_ref, dwg_ref, dwu_ref, dwd_ref):
        @pl.when(pl.program_id(0) == 0)
        def _():
            dgam_ref[...] = jnp.zeros_like(dgam_ref)
            dwg_ref[...] = jnp.zeros_like(dwg_ref)
            dwu_ref[...] = jnp.zeros_like(dwu_ref)
            dwd_ref[...] = jnp.zeros_like(dwd_ref)

        xnb = xn_ref[...]
        do = do_ref[...]
        dob = (0.5 * do).astype(BF16)
        g = gg_ref[...].astype(F32)
        u = uu_ref[...].astype(F32)
        s = _sigmoid(g)
        sil = g * s
        dhid = _dot_nt(dob, wd_ref[...])
        dwd_ref[...] += _dot_tn((sil * u).astype(BF16), dob)
        du = (dhid * sil).astype(BF16)
        dg = (dhid * u * (s * (1.0 + g * (1.0 - s)))).astype(BF16)
        dwg_ref[...] += _dot_tn(xnb, dg)
        dwu_ref[...] += _dot_tn(xnb, du)
        dxn = _dot_nt(dg, wg_ref[...]) + _dot_nt(du, wu_ref[...])
        for i in range(nprev):
            dxn = dxn + slab_ref[i].astype(F32)
        xhat, rstd, _ = _rms(h_ref[...], g_ref[...])
        dhn, dgam = _rms_bwd(xhat, rstd, g_ref[...], dxn)
        dh_ref[...] = do + dhn
        dgam_ref[...] += dgam

    tile = pl.BlockSpec((tm, D_MODEL), lambda t: (t, 0))
    hidden = pl.BlockSpec((None, tm, FF_CHUNK), lambda t: (k, t, 0))
    w_in = pl.BlockSpec((None, None, D_MODEL, FF_CHUNK), lambda t: (layer, k, 0, 0))
    dw_in = pl.BlockSpec((None, D_MODEL, FF_CHUNK), lambda t: (k, 0, 0))
    return pl.pallas_call(
        body, name="ffn_bwd_last",
        grid=(S // tm,),
        in_specs=[tile, tile, hidden, hidden, w_in, w_in,
                  pl.BlockSpec((None, None, FF_CHUNK, D_MODEL), lambda t: (layer, k, 0, 0)),
                  pl.BlockSpec((nprev, tm, D_MODEL), lambda t: (0, t, 0)), tile,
                  pl.BlockSpec((1, D_MODEL), lambda t: (0, 0))] + _any_specs(3),
        out_specs=[tile, pl.BlockSpec((SUBLANES, D_MODEL), lambda t: (0, 0)), dw_in, dw_in,
                   pl.BlockSpec((None, FF_CHUNK, D_MODEL), lambda t: (k, 0, 0))],
        out_shape=[jax.ShapeDtypeStruct((S, D_MODEL), F32), jax.ShapeDtypeStruct((SUBLANES, D_MODEL), F32),
                   jax.ShapeDtypeStruct(dwg.shape, F32), jax.ShapeDtypeStruct(dwu.shape, F32),
                   jax.ShapeDtypeStruct(dwd.shape, F32)],
        input_output_aliases={10: 2, 11: 3, 12: 4},
        compiler_params=_cparams(("arbitrary",)),
    )(xn, dout, gg, uu, wg, wu, wd, slabs, h, gamma, dwg, dwu, dwd)


def _xt_dy(name, x, dy, nchunk, kb, nb, x_by_chunk, y_by_chunk):
    S = x.shape[0]
    tm = _tile(S, 512)

    def body(x_ref, dy_ref, o_ref):
        @pl.when(pl.program_id(1) == 0)
        def _():
            o_ref[...] = jnp.zeros_like(o_ref)

        o_ref[...] += _dot_tn(x_ref[...].astype(BF16), dy_ref[...].astype(BF16))

    return pl.pallas_call(
        body, name=name,
        grid=(nchunk, S // tm),
        in_specs=[
            pl.BlockSpec((tm, kb), (lambda c, t: (t, c)) if x_by_chunk else (lambda c, t: (t, 0))),
            pl.BlockSpec((tm, nb), (lambda c, t: (t, c)) if y_by_chunk else (lambda c, t: (t, 0))),
        ],
        out_specs=pl.BlockSpec((None, kb, nb), lambda c, t: (c, 0, 0)),
        out_shape=jax.ShapeDtypeStruct((nchunk, kb, nb), F32),
        compiler_params=_cparams(("arbitrary", "arbitrary")),
    )(x, dy)


def _lru_gates(xc, wbd_ref, ba, bx, apar):
    xcb = xc.astype(BF16)
    r_parts, ig_parts = [], []
    for q in range(D_RNN // GATE_CHUNK):
        lo, hi = q * GATE_CHUNK, (q + 1) * GATE_CHUNK
        pre = _dot(xcb[:, lo:hi], wbd_ref[q])
        r_parts.append(_sigmoid(pre[:, :GATE_CHUNK] + ba[:, lo:hi]))
        ig_parts.append(_sigmoid(pre[:, GATE_CHUNK:] + bx[:, lo:hi]))
    r = jnp.concatenate(r_parts, axis=1)
    ig = jnp.concatenate(ig_parts, axis=1)
    sp = LRU_C * _softplus(-apar)
    log_a = -(r * sp)
    a = jnp.exp(log_a)
    mult = jnp.sqrt(_neg_expm1(2.0 * log_a))
    return r, ig, a, mult, sp


def _conv_causal(xb, tail, cw_ref, cb):
    ext = jnp.concatenate([tail, xb], axis=0)
    xc = cb + cw_ref[CONV_WIDTH - 1:CONV_WIDTH, :] * xb
    for j in range(1, CONV_WIDTH):
        xc = xc + cw_ref[CONV_WIDTH - 1 - j:CONV_WIDTH - j, :] * _shift_down(ext, j, SUBLANES)
    return xc, ext


def _lru_fwd(h, gamma, win, layer, convw, convb, wbd, ba, bx, apar, wout):
    S = h.shape[0]
    tm = _tile(S, 256)

    def body(h_ref, g_ref, win_ref, cw_ref, cb_ref, wbd_ref, ba_ref, bx_ref, ap_ref, wout_ref,
             ho_ref, xn_ref, z_ref, hs_ref, tail_s, carry_s):
        @pl.when(pl.program_id(0) == 0)
        def _():
            tail_s[...] = jnp.zeros_like(tail_s)
            carry_s[...] = jnp.zeros_like(carry_s)

        hf = h_ref[...]
        _, _, xn = _rms(hf, g_ref[...])
        xnb = xn.astype(BF16)
        xn_ref[...] = xnb
        for k in range(N_CHIPS):
            z_ref[:, k * RNN_IN_CHUNK:(k + 1) * RNN_IN_CHUNK] = _dot(xnb, win_ref[k])
        gate = z_ref[:, :D_RNN]
        xb = z_ref[:, D_RNN:]
        xc, _ = _conv_causal(xb, tail_s[...], cw_ref, cb_ref[...])
        tail_s[...] = xb[tm - SUBLANES:, :]
        _, ig, a, mult, _ = _lru_gates(xc, wbd_ref, ba_ref[...], bx_ref[...], ap_ref[...])
        big_a, big_b = _scan_causal(a, mult * (ig * xc))
        hs = big_a * carry_s[SUBLANES - 1:SUBLANES, :] + big_b
        hs_ref[...] = hs
        carry_s[...] = hs[tm - SUBLANES:, :]
        y = hs * _gelu(gate)
        ho_ref[...] = hf + _dot(y.astype(BF16), wout_ref[...])

    row = lambda n: pl.BlockSpec((1, n), lambda t: (0, 0))
    return pl.pallas_call(
        body, name="lru_fwd",
        grid=(S // tm,),
        in_specs=[
            pl.BlockSpec((tm, D_MODEL), lambda t: (t, 0)),
            row(D_MODEL),
            pl.BlockSpec((None, N_CHIPS, D_MODEL, RNN_IN_CHUNK), lambda t: (layer, 0, 0, 0)),
            pl.BlockSpec((CONV_WIDTH, D_RNN), lambda t: (0, 0)),
            row(D_RNN),
            pl.BlockSpec((D_RNN // GATE_CHUNK, GATE_CHUNK, 2 * GATE_CHUNK), lambda t: (0, 0, 0)),
            row(D_RNN), row(D_RNN), row(D_RNN),
            pl.BlockSpec((None, D_RNN, D_MODEL), lambda t: (layer, 0, 0)),
        ],
        out_specs=[
            pl.BlockSpec((tm, D_MODEL), lambda t: (t, 0)),
            pl.BlockSpec((tm, D_MODEL), lambda t: (t, 0)),
            pl.BlockSpec((tm, 2 * D_RNN), lambda t: (t, 0)),
            pl.BlockSpec((tm, D_RNN), lambda t: (t, 0)),
        ],
        out_shape=[
            jax.ShapeDtypeStruct((S, D_MODEL), F32),
            jax.ShapeDtypeStruct((S, D_MODEL), BF16),
            jax.ShapeDtypeStruct((S, 2 * D_RNN), F32),
            jax.ShapeDtypeStruct((S, D_RNN), F32),
        ],
        scratch_shapes=[pltpu.VMEM((SUBLANES, D_RNN), F32), pltpu.VMEM((SUBLANES, D_RNN), F32)],
        compiler_params=_cparams(("arbitrary",)),
    )(h, gamma, win, convw, convb, wbd, ba, bx, apar, wout)


def _lru_bwd_seq(dout, z, hs, convw, convb, wbd, ba, bx, apar, wout, layer):
    S = dout.shape[0]
    tm = _tile(S, 256)
    nt = S // tm
    per8 = tm // SUBLANES
    rev = lambda i: nt - 1 - i
    prev8 = lambda i: jnp.maximum(rev(i) * per8 - 1, 0)

    def body(do_ref, z_ref, hs_ref, ztail_ref, hstail_ref, cw_ref, cb_ref, wbd_ref, ba_ref, bx_ref, ap_ref, wout_ref,
             dz_ref, dpre_ref, xc_ref, y_ref, dcw_ref, vec_ref, a_first_s, g_first_s, dxc_head_s):
        i = pl.program_id(0)
        first_in_time = rev(i) == 0

        @pl.when(i == 0)
        def _():
            a_first_s[...] = jnp.zeros_like(a_first_s)
            g_first_s[...] = jnp.zeros_like(g_first_s)
            dxc_head_s[...] = jnp.zeros_like(dxc_head_s)
            dcw_ref[...] = jnp.zeros_like(dcw_ref)
            vec_ref[...] = jnp.zeros_like(vec_ref)

        gate = z_ref[:, :D_RNN]
        xb = z_ref[:, D_RNN:]
        hist = jnp.where(first_in_time, 0.0, 1.0)
        xc, xext = _conv_causal(xb, ztail_ref[:, D_RNN:] * hist, cw_ref, cb_ref[...])
        r, ig, a, mult, sp = _lru_gates(xc, wbd_ref, ba_ref[...], bx_ref[...], ap_ref[...])
        hs = hs_ref[...]
        gel, dgel = _gelu_and_grad(gate)
        y = hs * gel
        y_ref[...] = y.astype(BF16)
        xc_ref[...] = xc.astype(BF16)

        dy = _dot_nt(do_ref[...].astype(BF16), wout_ref[...])
        dhs = dy * gel
        dgate = dy * hs * dgel

        coef = _shift_up(jnp.concatenate([a, a_first_s[...]], axis=0), 1, tm)
        big_c, big_d = _scan_anticausal(coef, dhs)
        g = big_d + big_c * g_first_s[0:1, :]
        g_first_s[...] = g[:SUBLANES, :]
        a_first_s[...] = a[:SUBLANES, :]

        hs_prev = _shift_down(jnp.concatenate([hstail_ref[...] * hist, hs], axis=0), 1, SUBLANES)
        da = g * hs_prev
        dmult = g * ig * xc
        dig = g * mult * xc
        dxc = g * mult * ig
        dlog_a = da * a - dmult * (a * a) / mult
        dr = -(dlog_a * sp)
        dpre_a = dr * r * (1.0 - r)
        dpre_x = dig * ig * (1.0 - ig)
        d_apar = dlog_a * r * (LRU_C * _sigmoid(-ap_ref[...]))

        for q in range(D_RNN // GATE_CHUNK):
            lo, hi = q * GATE_CHUNK, (q + 1) * GATE_CHUNK
            dpre_q = jnp.concatenate([dpre_a[:, lo:hi], dpre_x[:, lo:hi]], axis=1).astype(BF16)
            dpre_ref[:, 2 * lo:2 * hi] = dpre_q
            dxc_q = _dot_nt(dpre_q, wbd_ref[q])
            if q == 0:
                dxc_parts = [dxc_q]
            else:
                dxc_parts.append(dxc_q)
        dxc = dxc + jnp.concatenate(dxc_parts, axis=1)

        dext = jnp.concatenate([dxc, dxc_head_s[...]], axis=0)
        dxb = cw_ref[CONV_WIDTH - 1:CONV_WIDTH, :] * dxc
        for j in range(1, CONV_WIDTH):
            dxb = dxb + cw_ref[CONV_WIDTH - 1 - j:CONV_WIDTH - j, :] * _shift_up(dext, j, tm)
        dxc_head_s[...] = dxc[:SUBLANES, :]
        dz_ref[:, :D_RNN] = dgate.astype(BF16)
        dz_ref[:, D_RNN:] = dxb.astype(BF16)

        dcw_ref[CONV_WIDTH - 1] += _rowsum8(dxc * xb)
        for j in range(1, CONV_WIDTH):
            dcw_ref[CONV_WIDTH - 1 - j] += _rowsum8(dxc * _shift_down(xext, j, SUBLANES))
        vec_ref[0] += _rowsum8(dxc)
        vec_ref[1] += _rowsum8(dpre_a)
        vec_ref[2] += _rowsum8(dpre_x)
        vec_ref[3] += _rowsum8(d_apar)

    row = lambda n: pl.BlockSpec((1, n), lambda i: (0, 0))
    return pl.pallas_call(
        body, name="lru_bwd_seq",
        grid=(nt,),
        in_specs=[
            pl.BlockSpec((tm, D_MODEL), lambda i: (rev(i), 0)),
            pl.BlockSpec((tm, 2 * D_RNN), lambda i: (rev(i), 0)),
            pl.BlockSpec((tm, D_RNN), lambda i: (rev(i), 0)),
            pl.BlockSpec((SUBLANES, 2 * D_RNN), lambda i: (prev8(i), 0)),
            pl.BlockSpec((SUBLANES, D_RNN), lambda i: (prev8(i), 0)),
            pl.BlockSpec((CONV_WIDTH, D_RNN), lambda i: (0, 0)),
            row(D_RNN),
            pl.BlockSpec((D_RNN // GATE_CHUNK, GATE_CHUNK, 2 * GATE_CHUNK), lambda i: (0, 0, 0)),
            row(D_RNN), row(D_RNN), row(D_RNN),
            pl.BlockSpec((None, D_RNN, D_MODEL), lambda i: (layer, 0, 0)),
        ],
        out_specs=[
            pl.BlockSpec((tm, 2 * D_RNN), lambda i: (rev(i), 0)),
            pl.BlockSpec((tm, 2 * D_RNN), lambda i: (rev(i), 0)),
            pl.BlockSpec((tm, D_RNN), lambda i: (rev(i), 0)),
            pl.BlockSpec((tm, D_RNN), lambda i: (rev(i), 0)),
            pl.BlockSpec((CONV_WIDTH, SUBLANES, D_RNN), lambda i: (0, 0, 0)),
            pl.BlockSpec((4, SUBLANES, D_RNN), lambda i: (0, 0, 0)),
        ],
        out_shape=[
            jax.ShapeDtypeStruct((S, 2 * D_RNN), BF16),
            jax.ShapeDtypeStruct((S, 2 * D_RNN), BF16),
            jax.ShapeDtypeStruct((S, D_RNN), BF16),
            jax.ShapeDtypeStruct((S, D_RNN), BF16),
            jax.ShapeDtypeStruct((CONV_WIDTH, SUBLANES, D_RNN), F32),
            jax.ShapeDtypeStruct((4, SUBLANES, D_RNN), F32),
        ],
        scratch_shapes=[pltpu.VMEM((SUBLANES, D_RNN), F32)] * 3,
        compiler_params=_cparams(("arbitrary",)),
    )(dout, z, hs, z, hs, convw, convb, wbd, ba, bx, apar, wout)


def _lru_bwd_in(dz, h, gamma, dres, win, layer):
    S = h.shape[0]
    tm = _tile(S, 512)

    def body(dz_ref, h_ref, g_ref, dres_ref, win_ref, dh_ref, dgam_ref):
        dxn = _dot_nt(dz_ref[:, :RNN_IN_CHUNK], win_ref[0])
        for k in range(1, N_CHIPS):
            dxn = dxn + _dot_nt(dz_ref[:, k * RNN_IN_CHUNK:(k + 1) * RNN_IN_CHUNK], win_ref[k])
        xhat, rstd, _ = _rms(h_ref[...], g_ref[...])
        dhn, dgam = _rms_bwd(xhat, rstd, g_ref[...], dxn)
        dh_ref[...] = dres_ref[...] + dhn

        @pl.when(pl.program_id(0) == 0)
        def _():
            dgam_ref[...] = jnp.zeros_like(dgam_ref)

        dgam_ref[...] += dgam

    return pl.pallas_call(
        body, name="lru_bwd_in",
        grid=(S // tm,),
        in_specs=[
            pl.BlockSpec((tm, 2 * D_RNN), lambda t: (t, 0)),
            pl.BlockSpec((tm, D_MODEL), lambda t: (t, 0)),
            pl.BlockSpec((1, D_MODEL), lambda t: (0, 0)),
            pl.BlockSpec((tm, D_MODEL), lambda t: (t, 0)),
            pl.BlockSpec((None, N_CHIPS, D_MODEL, RNN_IN_CHUNK), lambda t: (layer, 0, 0, 0)),
        ],
        out_specs=[
            pl.BlockSpec((tm, D_MODEL), lambda t: (t, 0)),
            pl.BlockSpec((SUBLANES, D_MODEL), lambda t: (0, 0)),
        ],
        out_shape=[jax.ShapeDtypeStruct((S, D_MODEL), F32), jax.ShapeDtypeStruct((SUBLANES, D_MODEL), F32)],
        compiler_params=_cparams(("arbitrary",)),
    )(dz, h, gamma, dres, win)


def _pool_inv_count(t_index, tm):
    rows = (lax.broadcasted_iota(jnp.int32, (tm, D_MODEL), 0) + t_index * tm + 1).astype(F32)
    col = lax.broadcasted_iota(jnp.int32, (tm, D_MODEL), 1)
    win = jnp.where(col < POOL_GROUP_DIM, float(POOL_WINDOWS[0]),
                    jnp.where(col < 2 * POOL_GROUP_DIM, float(POOL_WINDOWS[1]),
                              jnp.where(col < 3 * POOL_GROUP_DIM, float(POOL_WINDOWS[2]), float(POOL_WINDOWS[3]))))
    return 1.0 / jnp.minimum(rows, win)


def _window_sums(ext, shift, take):
    gd = POOL_GROUP_DIM
    s2 = ext + shift(ext, 1)
    s4 = s2[:, gd:] + shift(s2[:, gd:], 2)
    s8 = s4[:, gd:] + shift(s4[:, gd:], 4)
    s16 = s8[:, gd:] + shift(s8[:, gd:], 8)
    return jnp.concatenate([take(s2[:, :gd]), take(s4[:, :gd]), take(s8[:, :gd]), take(s16)], axis=1)


def _pool_fwd(h, gamma, pw, pb, pscale, layer):
    S = h.shape[0]
    tm = _tile(S, 512)

    def body(h_ref, g_ref, pw_ref, pb_ref, ps_ref, ho_ref, u_ref, tail_s):
        t = pl.program_id(0)

        @pl.when(t == 0)
        def _():
            tail_s[...] = jnp.zeros_like(tail_s)

        hf = h_ref[...]
        _, _, hn = _rms(hf, g_ref[...])
        ext = jnp.concatenate([tail_s[...], hn], axis=0)
        tail_s[...] = hn[tm - POOL_HALO:, :]
        sums = _window_sums(ext, lambda v, j: pltpu.roll(v, j, 0), lambda v: v[POOL_HALO:])
        ub = (sums * _pool_inv_count(t, tm) - hn).astype(BF16)
        u_ref[...] = ub
        ys = [_dot(ub[:, g * POOL_GROUP_DIM:(g + 1) * POOL_GROUP_DIM], pw_ref[g]) for g in range(len(POOL_WINDOWS))]
        y = jnp.concatenate(ys, axis=1)
        ho_ref[...] = hf + (y + pb_ref[...]) * ps_ref[...]

    row = pl.BlockSpec((1, D_MODEL), lambda t: (0, 0))
    return pl.pallas_call(
        body, name="pool_fwd",
        grid=(S // tm,),
        in_specs=[
            pl.BlockSpec((tm, D_MODEL), lambda t: (t, 0)), row,
            pl.BlockSpec((None, len(POOL_WINDOWS), POOL_GROUP_DIM, POOL_GROUP_DIM), lambda t: (layer, 0, 0, 0)),
            row, row,
        ],
        out_specs=[pl.BlockSpec((tm, D_MODEL), lambda t: (t, 0)), pl.BlockSpec((tm, D_MODEL), lambda t: (t, 0))],
        out_shape=[jax.ShapeDtypeStruct((S, D_MODEL), F32), jax.ShapeDtypeStruct((S, D_MODEL), BF16)],
        scratch_shapes=[pltpu.VMEM((POOL_HALO, D_MODEL), F32)],
        compiler_params=_cparams(("arbitrary",)),
    )(h, gamma, pw, pb, pscale)


def _pool_bwd(dout, h, u, gamma, pw, pb, pscale, layer):
    S = h.shape[0]
    tm = _tile(S, 512)
    nt = S // tm
    rev = lambda i: nt - 1 - i
    ngroup = len(POOL_WINDOWS)

    def body(do_ref, h_ref, u_ref, g_ref, pw_ref, pb_ref, ps_ref, dh_ref, dpre_ref, vec_ref, head_s):
        i = pl.program_id(0)

        @pl.when(i == 0)
        def _():
            head_s[...] = jnp.zeros_like(head_s)
            vec_ref[...] = jnp.zeros_like(vec_ref)

        do = do_ref[...]
        ub = u_ref[...]
        gsl = lambda v, g: v[:, g * POOL_GROUP_DIM:(g + 1) * POOL_GROUP_DIM]
        y = jnp.concatenate([_dot(gsl(ub, g), pw_ref[g]) for g in range(ngroup)], axis=1)
        dpre = do * ps_ref[...]
        dpb = dpre.astype(BF16)
        dpre_ref[...] = dpb
        du = jnp.concatenate([_dot_nt(gsl(dpb, g), pw_ref[g]) for g in range(ngroup)], axis=1)
        v = du * _pool_inv_count(rev(i), tm)
        ext = jnp.concatenate([v, head_s[...]], axis=0)
        head_s[...] = v[:POOL_HALO, :]
        n = tm + POOL_HALO
        dhn = _window_sums(ext, lambda w, j: pltpu.roll(w, n - j, 0), lambda w: w[:tm]) - du
        xhat, rstd, _ = _rms(h_ref[...], g_ref[...])
        dh_in, dgam = _rms_bwd(xhat, rstd, g_ref[...], dhn)
        dh_ref[...] = do + dh_in
        vec_ref[0] += dgam
        vec_ref[1] += _rowsum8(dpre)
        vec_ref[2] += _rowsum8(do * (y + pb_ref[...]))

    row = pl.BlockSpec((1, D_MODEL), lambda i: (0, 0))
    tile = pl.BlockSpec((tm, D_MODEL), lambda i: (rev(i), 0))
    return pl.pallas_call(
        body, name="pool_bwd",
        grid=(nt,),
        in_specs=[tile, tile, tile, row,
                  pl.BlockSpec((None, ngroup, POOL_GROUP_DIM, POOL_GROUP_DIM), lambda i: (layer, 0, 0, 0)), row, row],
        out_specs=[tile, tile, pl.BlockSpec((3, SUBLANES, D_MODEL), lambda i: (0, 0, 0))],
        out_shape=[jax.ShapeDtypeStruct((S, D_MODEL), F32), jax.ShapeDtypeStruct((S, D_MODEL), BF16),
                   jax.ShapeDtypeStruct((3, SUBLANES, D_MODEL), F32)],
        scratch_shapes=[pltpu.VMEM((POOL_HALO, D_MODEL), F32)],
        compiler_params=_cparams(("arbitrary",)),
    )(dout, h, u, gamma, pw, pb, pscale)


def _ple_parts(hf, gamma, p_tile, wgate_ref, wproj_ref):
    xhat, rstd, xn = _rms(hf, gamma)
    xnb = xn.astype(BF16)
    gate = _sigmoid(_dot(xnb, wgate_ref[...]))
    pb = p_tile.astype(BF16)
    proj = jnp.concatenate([_dot(pb, wproj_ref[k]) for k in range(N_CHIPS)], axis=1)
    return xhat, rstd, xnb, pb, gate, proj


def _ple_fwd(h, gamma, p_l, wgate, wproj, layer):
    S = h.shape[0]
    tm = _tile(S, 512)

    def body(h_ref, g_ref, p_ref, wgate_ref, wproj_ref, ho_ref):
        hf = h_ref[...]
        _, _, _, _, gate, proj = _ple_parts(hf, g_ref[...], p_ref[...], wgate_ref, wproj_ref)
        ho_ref[...] = hf + gate * proj

    return pl.pallas_call(
        body, name="ple_fwd",
        grid=(S // tm,),
        in_specs=[
            pl.BlockSpec((tm, D_MODEL), lambda t: (t, 0)),
            pl.BlockSpec((1, D_MODEL), lambda t: (0, 0)),
            pl.BlockSpec((tm, PLE_DIM), lambda t: (t, 0)),
            pl.BlockSpec((None, D_MODEL, D_MODEL), lambda t: (layer, 0, 0)),
            pl.BlockSpec((None, N_CHIPS, PLE_DIM, PLE_DIM), lambda t: (layer, 0, 0, 0)),
        ],
        out_specs=pl.BlockSpec((tm, D_MODEL), lambda t: (t, 0)),
        out_shape=jax.ShapeDtypeStruct((S, D_MODEL), F32),
        compiler_params=_cparams(("arbitrary",)),
    )(h, gamma, p_l, wgate, wproj)


def _ple_bwd(dout, h, gamma, p_l, wgate, wproj, layer):
    S = h.shape[0]
    tm = _tile(S, 512)

    def body(do_ref, h_ref, g_ref, p_ref, wgate_ref, wproj_ref, dh_ref, dwg_ref, dwp_ref, dgam_ref):
        @pl.when(pl.program_id(0) == 0)
        def _():
            dgam_ref[...] = jnp.zeros_like(dgam_ref)
            dwg_ref[...] = jnp.zeros_like(dwg_ref)
            dwp_ref[...] = jnp.zeros_like(dwp_ref)

        do = do_ref[...]
        xhat, rstd, xnb, pb, gate, proj = _ple_parts(h_ref[...], g_ref[...], p_ref[...], wgate_ref, wproj_ref)
        dproj = (do * gate).astype(BF16)
        dpre = (do * proj * gate * (1.0 - gate)).astype(BF16)
        dwg_ref[...] += _dot_tn(xnb, dpre)
        for k in range(N_CHIPS):
            dwp_ref[k] += _dot_tn(pb, dproj[:, k * PLE_DIM:(k + 1) * PLE_DIM])
        dhn, dgam = _rms_bwd(xhat, rstd, g_ref[...], _dot_nt(dpre, wgate_ref[...]))
        dh_ref[...] = do + dhn
        dgam_ref[...] += dgam

    tile = pl.BlockSpec((tm, D_MODEL), lambda t: (t, 0))
    return pl.pallas_call(
        body, name="ple_bwd",
        grid=(S // tm,),
        in_specs=[
            tile, tile,
            pl.BlockSpec((1, D_MODEL), lambda t: (0, 0)),
            pl.BlockSpec((tm, PLE_DIM), lambda t: (t, 0)),
            pl.BlockSpec((None, D_MODEL, D_MODEL), lambda t: (layer, 0, 0)),
            pl.BlockSpec((None, N_CHIPS, PLE_DIM, PLE_DIM), lambda t: (layer, 0, 0, 0)),
        ],
        out_specs=[tile, pl.BlockSpec((D_MODEL, D_MODEL), lambda t: (0, 0)),
                   pl.BlockSpec((N_CHIPS, PLE_DIM, PLE_DIM), lambda t: (0, 0, 0)),
                   pl.BlockSpec((SUBLANES, D_MODEL), lambda t: (0, 0))],
        out_shape=[jax.ShapeDtypeStruct((S, D_MODEL), F32), jax.ShapeDtypeStruct((D_MODEL, D_MODEL), F32),
                   jax.ShapeDtypeStruct((N_CHIPS, PLE_DIM, PLE_DIM), F32),
                   jax.ShapeDtypeStruct((SUBLANES, D_MODEL), F32)],
        compiler_params=_cparams(("arbitrary",)),
    )(dout, h, gamma, p_l, wgate, wproj)


def _final(h, gamma, target):
    S = h.shape[0]
    tm = _tile(S, 512)

    def body(h_ref, g_ref, tgt_ref, dh_ref, dgam_ref, loss_ref):
        xhat, rstd, y = _rms(h_ref[...], g_ref[...])
        err = y - tgt_ref[...]
        dy = err * (1.0 / D_MODEL)
        dhn, dgam = _rms_bwd(xhat, rstd, g_ref[...], dy)
        dh_ref[...] = dhn
        sq = _rowsum8(err * err)
        part = sq[:, :LANES]
        for j in range(1, D_MODEL // LANES):
            part = part + sq[:, j * LANES:(j + 1) * LANES]

        @pl.when(pl.program_id(0) == 0)
        def _():
            dgam_ref[...] = jnp.zeros_like(dgam_ref)
            loss_ref[...] = jnp.zeros_like(loss_ref)

        dgam_ref[...] += dgam
        loss_ref[...] += part * (0.5 / D_MODEL)

    tile = pl.BlockSpec((tm, D_MODEL), lambda t: (t, 0))
    return pl.pallas_call(
        body, name="final_loss",
        grid=(S // tm,),
        in_specs=[tile, pl.BlockSpec((1, D_MODEL), lambda t: (0, 0)), tile],
        out_specs=[tile, pl.BlockSpec((SUBLANES, D_MODEL), lambda t: (0, 0)),
                   pl.BlockSpec((SUBLANES, LANES), lambda t: (0, 0))],
        out_shape=[jax.ShapeDtypeStruct((S, D_MODEL), F32), jax.ShapeDtypeStruct((SUBLANES, D_MODEL), F32),
                   jax.ShapeDtypeStruct((SUBLANES, LANES), F32)],
        compiler_params=_cparams(("arbitrary",)),
    )(h, gamma, target)


def _mesh_pos():
    return lax.axis_index("x"), lax.axis_index("y"), lax.axis_index("c")


def _other_chip(x, y, j):
    fx, fy = CHIP_FLIPS[j]
    return (1 - x if fx else x), (1 - y if fy else y)


def _any_specs(n):
    return [pl.BlockSpec(memory_space=pl.ANY)] * n


class _Cargo:
    def __init__(self):
        self.operands, self.out_shapes, self.aliases, self.sem_shapes, self.names = [], [], {}, [], []
        self.start = lambda ins, outs, sems: None
        self.finish = lambda ins, outs, sems: None


def _remote(src, dst, send, recv, device):
    return pltpu.make_async_remote_copy(src_ref=src, dst_ref=dst, send_sem=send, recv_sem=recv,
                                        device_id=device, device_id_type=MESH_ID)


def _gather_cargo(bufs, pieces):
    cargo = _Cargo()
    if not pieces:
        return cargo
    plist = []
    for name, layer in pieces:
        if name not in cargo.names:
            cargo.names.append(name)
            cargo.operands.append(bufs[name])
        plist.append((cargo.names.index(name), layer, bufs[name].shape[2] // 2))
    nflip = len(CHIP_FLIPS)
    cargo.out_shapes = [jax.ShapeDtypeStruct(b.shape, b.dtype) for b in cargo.operands]
    cargo.aliases = {i: i for i in range(len(cargo.operands))}
    cargo.sem_shapes = [pltpu.SemaphoreType.DMA((len(plist) * nflip,))] * 4

    def copies(outs, sems):
        send1, recv1, send2, recv2 = sems
        x, y, c = _mesh_pos()
        k = 2 * x + y

        def blk(p, chip, cc):
            b, layer, hrows = plist[p]
            return outs[b].at[layer, chip, pl.ds(cc * hrows, hrows), :]

        def chip_of(j):
            px, py = _other_chip(x, y, j)
            return 2 * px + py

        def ici(p, j):
            px, py = _other_chip(x, y, j)
            return _remote(blk(p, k, c), blk(p, k, c), send1.at[p * nflip + j], recv1.at[p * nflip + j], (px, py, c))

        def landed(p, j):
            px, py = _other_chip(x, y, j)
            return _remote(blk(p, k, c), blk(p, chip_of(j), c), send1.at[p * nflip + j], recv1.at[p * nflip + j],
                           (px, py, c))

        def d2d(p, j, cc):
            return _remote(blk(p, chip_of(j), cc), blk(p, chip_of(j), cc), send2.at[p * nflip + j],
                           recv2.at[p * nflip + j], (x, y, 1 - c))

        return c, ici, landed, d2d

    def start(ins, outs, sems):
        _, ici, _, _ = copies(outs, sems)
        for p in range(len(plist)):
            for j in range(nflip):
                ici(p, j).start()

    def finish(ins, outs, sems):
        c, ici, landed, d2d = copies(outs, sems)
        for j in range(nflip):
            for p in range(len(plist)):
                landed(p, j).wait_recv()
                d2d(p, j, c).start()
        for p in range(len(plist)):
            for j in range(nflip):
                ici(p, j).wait_send()
                d2d(p, j, c).wait_send()
                d2d(p, j, 1 - c).wait_recv()

    cargo.start, cargo.finish = start, finish
    return cargo


def _reduce_cargo(grads, presums):
    cargo = _Cargo()
    na, nb = len(grads), len(presums)
    nflip = len(CHIP_FLIPS)
    cargo.operands = list(grads) + list(presums)
    cargo.out_shapes = ([jax.ShapeDtypeStruct((g.shape[0], g.shape[1] // 2, g.shape[2]), g.dtype) for g in grads]
                        + [jax.ShapeDtypeStruct((nflip,) + ps.shape[1:], ps.dtype) for ps in presums])
    cargo.sem_shapes = ([pltpu.SemaphoreType.DMA((na,))] * 2 if na else []) + (
        [pltpu.SemaphoreType.DMA((nb * nflip,))] * 2 if nb else [])

    def copies(ins, outs, sems):
        x, y, c = _mesh_pos()
        out = []
        if na:
            send, recv = sems[0], sems[1]
            for a in range(na):
                hrows = grads[a].shape[1] // 2
                out.append(_remote(ins[a].at[:, pl.ds((1 - c) * hrows, hrows), :], outs[a], send.at[a], recv.at[a],
                                   (x, y, 1 - c)))
        if nb:
            send, recv = sems[-2], sems[-1]
            for b in range(nb):
                for j in range(nflip):
                    px, py = _other_chip(x, y, j)
                    out.append(_remote(ins[na + b].at[2 * px + py], outs[na + b].at[j], send.at[b * nflip + j],
                                       recv.at[b * nflip + j], (px, py, c)))
        return out

    def start(ins, outs, sems):
        for cp in copies(ins, outs, sems):
            cp.start()

    def finish(ins, outs, sems):
        for cp in copies(ins, outs, sems):
            cp.wait()

    cargo.start, cargo.finish = start, finish
    return cargo


def _run_cargo(name, cargo):
    nin, nout = len(cargo.operands), len(cargo.out_shapes)

    def body(*refs):
        ins, outs, sems = refs[:nin], refs[nin:nin + nout], refs[nin + nout:]
        cargo.start(ins, outs, sems)
        cargo.finish(ins, outs, sems)

    return list(pl.pallas_call(
        body, name=name,
        in_specs=_any_specs(nin), out_specs=_any_specs(nout), out_shape=cargo.out_shapes,
        input_output_aliases=dict(cargo.aliases), scratch_shapes=list(cargo.sem_shapes),
    )(*cargo.operands))


def _join_siblings(bufs):
    nb = len(bufs)
    items = [(b, layer) for b, buf in enumerate(bufs) for layer in range(buf.shape[0])]

    def body(*refs):
        outs = refs[nb:2 * nb]
        send, recv = refs[2 * nb:]
        x, y, c = _mesh_pos()

        def half(i, cc):
            b, layer = items[i]
            hrows = bufs[b].shape[1] // 2
            blk = outs[b].at[layer, pl.ds(cc * hrows, hrows), :]
            return _remote(blk, blk, send.at[i], recv.at[i], (x, y, 1 - c))

        for i in range(len(items)):
            half(i, c).start()
        for i in range(len(items)):
            half(i, c).wait_send()
            half(i, 1 - c).wait_recv()

    return list(pl.pallas_call(
        body, name="grad_sibling_join",
        in_specs=_any_specs(nb), out_specs=_any_specs(nb),
        out_shape=[jax.ShapeDtypeStruct(b.shape, b.dtype) for b in bufs],
        input_output_aliases={i: i for i in range(nb)},
        scratch_shapes=[pltpu.SemaphoreType.DMA((len(items),))] * 2,
    )(*bufs))


def _cast_place(w3, pos, dtype):
    L, rows, cols = w3.shape

    def body(pos_ref, w_ref, o_ref):
        o_ref[...] = w_ref[...].astype(dtype)

    return pl.pallas_call(
        body, name="cast_place",
        grid_spec=pltpu.PrefetchScalarGridSpec(
            num_scalar_prefetch=1, grid=(L,),
            in_specs=[pl.BlockSpec((None, rows, cols), lambda l, pos: (l, 0, 0))],
            out_specs=pl.BlockSpec((None, None, rows, cols), lambda l, pos: (l, pos[0], 0, 0))),
        out_shape=jax.ShapeDtypeStruct((L, N_CHIPS, rows, cols), dtype),
        compiler_params=_cparams(("arbitrary",)),
    )(pos, w3)


def _allreduce_small(buf):
    R = buf.shape[0]

    def body(in_ref, out_ref, land, send, recv):
        x, y, c = _mesh_pos()
        out_ref[...] = in_ref[...]
        for s, peer in enumerate(((x, y, 1 - c), (x, 1 - y, c), (1 - x, y, c))):
            cp = pltpu.make_async_remote_copy(
                src_ref=out_ref, dst_ref=land.at[s], send_sem=send.at[s], recv_sem=recv.at[s],
                device_id=peer, device_id_type=MESH_ID)
            cp.start()
            cp.wait()
            out_ref[...] = out_ref[...] + land[s]

    return pl.pallas_call(
        body, name="allreduce_small",
        in_specs=[pl.BlockSpec(memory_space=pltpu.VMEM)],
        out_specs=pl.BlockSpec(memory_space=pltpu.VMEM),
        out_shape=jax.ShapeDtypeStruct((R, LANES), F32),
        scratch_shapes=[pltpu.VMEM((3, R, LANES), F32), pltpu.SemaphoreType.DMA((3,)), pltpu.SemaphoreType.DMA((3,))],
        compiler_params=pltpu.CompilerParams(vmem_limit_bytes=VMEM_LIMIT_MB * 2 ** 20),
    )(buf)


def _presum_with_sibling(grad, landed, pos):
    nchunk, rows, cols = grad.shape
    hrows = rows // 2
    nsub = 1
    for cand in (4, 2):
        if hrows % (cand * 16) == 0:
            nsub = cand
            break
    rb = hrows // nsub

    def body(pos_ref, g_ref, l_ref, all_ref, own_ref):
        s = g_ref[...] + l_ref[...]
        all_ref[...] = s.astype(BF16)

        @pl.when(pl.program_id(1) == pos_ref[0])
        def _():
            own_ref[...] = s

    return pl.pallas_call(
        body, name="grad_presum",
        grid_spec=pltpu.PrefetchScalarGridSpec(
            num_scalar_prefetch=1, grid=(nsub, nchunk),
            in_specs=[pl.BlockSpec((None, rb, cols), lambda i, k, pos: (k, pos[1] * nsub + i, 0)),
                      pl.BlockSpec((None, rb, cols), lambda i, k, pos: (k, i, 0))],
            out_specs=[pl.BlockSpec((None, rb, cols), lambda i, k, pos: (k, i, 0)),
                       pl.BlockSpec((rb, cols), lambda i, k, pos: (i, 0))]),
        out_shape=[jax.ShapeDtypeStruct((nchunk, hrows, cols), BF16), jax.ShapeDtypeStruct((hrows, cols), F32)],
        compiler_params=_cparams(("arbitrary", "arbitrary")),
    )(pos, grad, landed)


def _sum_chips(own, landed, stacked, layer, shape3, pos):
    hrows, cols = own.shape

    def body(pos_ref, o_ref, l_ref, *rest):
        s = o_ref[...]
        for j in range(len(CHIP_FLIPS)):
            s = s + l_ref[j].astype(F32)
        rest[-1][...] = s

    in_specs = [pl.BlockSpec((hrows, cols), lambda i, pos: (0, 0)),
                pl.BlockSpec((len(CHIP_FLIPS), hrows, cols), lambda i, pos: (0, 0, 0))]
    args = [pos, own, landed]
    aliases = {}
    if stacked is not None:
        in_specs.append(pl.BlockSpec(memory_space=pl.ANY))
        args.append(stacked)
        aliases = {3: 0}
    return pl.pallas_call(
        body, name="grad_sum_chips",
        grid_spec=pltpu.PrefetchScalarGridSpec(
            num_scalar_prefetch=1, grid=(1,), in_specs=in_specs,
            out_specs=pl.BlockSpec((None, hrows, cols), lambda i, pos: (layer, pos[1], 0))),
        out_shape=jax.ShapeDtypeStruct(shape3, F32),
        input_output_aliases=aliases,
        compiler_params=_cparams(("arbitrary",)),
    )(*args)


def _adamw(w, g, m, v):
    R, C = w.shape
    rb = R
    for cand in (512, 352, 320, 256, 128, 64, 32, 16, 8):
        if R % cand == 0:
            rb = cand
            break
    c1 = 1.0 - ADAM_B1 ** ADAM_STEP
    c2 = 1.0 - ADAM_B2 ** ADAM_STEP

    def body(w_ref, g_ref, m_ref, v_ref, go_ref, d_ref, mo_ref, vo_ref):
        gv = g_ref[...]
        go_ref[...] = gv
        m2 = ADAM_B1 * m_ref[...] + (1.0 - ADAM_B1) * gv
        v2 = ADAM_B2 * v_ref[...] + (1.0 - ADAM_B2) * (gv * gv)
        mo_ref[...] = m2
        vo_ref[...] = v2
        d_ref[...] = -ADAM_LR * ((m2 / c1) / (jnp.sqrt(v2 / c2) + ADAM_EPS) + ADAM_WD * w_ref[...])

    spec = pl.BlockSpec((rb, C), lambda i: (i, 0))
    return pl.pallas_call(
        body, name="adamw",
        grid=(R // rb,),
        in_specs=[spec] * 4, out_specs=[spec] * 4,
        out_shape=[jax.ShapeDtypeStruct((R, C), F32)] * 4,
        compiler_params=_cparams(("arbitrary",)),
    )(w, g, m, v)


def _pack(parts, align=SUBLANES * LANES):
    flat = jnp.concatenate([p.reshape(-1).astype(F32) for p in parts])
    pad = (-flat.shape[0]) % align
    return jnp.pad(flat, (0, pad)).reshape(-1, LANES)


def _unpack(buf, shapes):
    flat = buf.reshape(-1)
    out, off = [], 0
    for shp in shapes:
        size = 1
        for d in shp:
            size *= d
        out.append(flat[off:off + size].reshape(shp))
        off += size
    return out


def _block_diag_gates(w_a, w_x):
    nq = D_RNN // GATE_CHUNK
    hpc = LRU_HEADS // nq
    eye = jnp.eye(hpc, dtype=F32)

    def bd(w):
        wq = w.reshape(nq, hpc, LRU_HEAD_DIM, LRU_HEAD_DIM)
        return (wq[:, :, :, None, :] * eye[None, :, None, :, None]).reshape(nq, GATE_CHUNK, GATE_CHUNK)

    return jnp.concatenate([bd(w_a), bd(w_x)], axis=2).astype(BF16)


def _block_diag_extract(dwbd):
    nq = D_RNN // GATE_CHUNK
    hpc = LRU_HEADS // nq
    eye = jnp.eye(hpc, dtype=F32)

    def ex(d):
        d5 = d.reshape(nq, hpc, LRU_HEAD_DIM, hpc, LRU_HEAD_DIM)
        return jnp.sum(d5 * eye[None, :, None, :, None], axis=3).reshape(LRU_HEADS, LRU_HEAD_DIM, LRU_HEAD_DIM)

    return ex(dwbd[:, :, :GATE_CHUNK]), ex(dwbd[:, :, GATE_CHUNK:])


BIG = ("ffn1_w_gate", "ffn1_w_up", "ffn1_w_down", "lru_w_in", "lru_w_out", "pool_w",
       "ffn2_w_gate", "ffn2_w_up", "ffn2_w_down", "ple_w_gate", "ple_w_proj")
TINY_SHARDED = ("lru_conv_w", "pool_b", "pool_scale")
REPLICATED = ("ffn1_norm", "mix_norm", "lru_conv_b", "lru_w_a", "lru_b_a", "lru_w_x", "lru_b_x", "lru_a_param",
              "ffn2_norm", "ple_norm", "final_norm")
WEIGHT_ORDER = ("ffn1_norm", "ffn1_w_gate", "ffn1_w_up", "ffn1_w_down", "mix_norm", "lru_w_in", "lru_conv_w",
                "lru_conv_b", "lru_w_a", "lru_b_a", "lru_w_x", "lru_b_x", "lru_a_param", "lru_w_out", "pool_w",
                "pool_b", "pool_scale", "ffn2_norm", "ffn2_w_gate", "ffn2_w_up", "ffn2_w_down", "ple_norm",
                "ple_w_gate", "ple_w_proj", "final_norm")


def _as3(a):
    return a.reshape(a.shape[0], -1, a.shape[-1])


def kernel(x, p, ffn1_norm, ffn1_w_gate, ffn1_w_up, ffn1_w_down, mix_norm, lru_w_in, lru_conv_w, lru_conv_b, lru_w_a, lru_b_a, lru_w_x, lru_b_x, lru_a_param, lru_w_out, pool_w, pool_b, pool_scale, ffn2_norm, ffn2_w_gate, ffn2_w_up, ffn2_w_down, ple_norm, ple_w_gate, ple_w_proj, final_norm, loss_target, m_ffn1_norm, m_ffn1_w_gate, m_ffn1_w_up, m_ffn1_w_down, m_mix_norm, m_lru_w_in, m_lru_conv_w, m_lru_conv_b, m_lru_w_a, m_lru_b_a, m_lru_w_x, m_lru_b_x, m_lru_a_param, m_lru_w_out, m_pool_w, m_pool_b, m_pool_scale, m_ffn2_norm, m_ffn2_w_gate, m_ffn2_w_up, m_ffn2_w_down, m_ple_norm, m_ple_w_gate, m_ple_w_proj, m_final_norm, v_ffn1_norm, v_ffn1_w_gate, v_ffn1_w_up, v_ffn1_w_down, v_mix_norm, v_lru_w_in, v_lru_conv_w, v_lru_conv_b, v_lru_w_a, v_lru_b_a, v_lru_w_x, v_lru_b_x, v_lru_a_param, v_lru_w_out, v_pool_w, v_pool_b, v_pool_scale, v_ffn2_norm, v_ffn2_w_gate, v_ffn2_w_up, v_ffn2_w_down, v_ple_norm, v_ple_w_gate, v_ple_w_proj, v_final_norm):
    W = dict(ffn1_norm=ffn1_norm, ffn1_w_gate=ffn1_w_gate, ffn1_w_up=ffn1_w_up, ffn1_w_down=ffn1_w_down,
             mix_norm=mix_norm, lru_w_in=lru_w_in, lru_conv_w=lru_conv_w, lru_conv_b=lru_conv_b, lru_w_a=lru_w_a,
             lru_b_a=lru_b_a, lru_w_x=lru_w_x, lru_b_x=lru_b_x, lru_a_param=lru_a_param, lru_w_out=lru_w_out,
             pool_w=pool_w, pool_b=pool_b, pool_scale=pool_scale, ffn2_norm=ffn2_norm, ffn2_w_gate=ffn2_w_gate,
             ffn2_w_up=ffn2_w_up, ffn2_w_down=ffn2_w_down, ple_norm=ple_norm, ple_w_gate=ple_w_gate,
             ple_w_proj=ple_w_proj, final_norm=final_norm)
    M = dict(ffn1_norm=m_ffn1_norm, ffn1_w_gate=m_ffn1_w_gate, ffn1_w_up=m_ffn1_w_up, ffn1_w_down=m_ffn1_w_down,
             mix_norm=m_mix_norm, lru_w_in=m_lru_w_in, lru_conv_w=m_lru_conv_w, lru_conv_b=m_lru_conv_b,
             lru_w_a=m_lru_w_a, lru_b_a=m_lru_b_a, lru_w_x=m_lru_w_x, lru_b_x=m_lru_b_x, lru_a_param=m_lru_a_param,
             lru_w_out=m_lru_w_out, pool_w=m_pool_w, pool_b=m_pool_b, pool_scale=m_pool_scale, ffn2_norm=m_ffn2_norm,
             ffn2_w_gate=m_ffn2_w_gate, ffn2_w_up=m_ffn2_w_up, ffn2_w_down=m_ffn2_w_down, ple_norm=m_ple_norm,
             ple_w_gate=m_ple_w_gate, ple_w_proj=m_ple_w_proj, final_norm=m_final_norm)
    V = dict(ffn1_norm=v_ffn1_norm, ffn1_w_gate=v_ffn1_w_gate, ffn1_w_up=v_ffn1_w_up, ffn1_w_down=v_ffn1_w_down,
             mix_norm=v_mix_norm, lru_w_in=v_lru_w_in, lru_conv_w=v_lru_conv_w, lru_conv_b=v_lru_conv_b,
             lru_w_a=v_lru_w_a, lru_b_a=v_lru_b_a, lru_w_x=v_lru_w_x, lru_b_x=v_lru_b_x, lru_a_param=v_lru_a_param,
             lru_w_out=v_lru_w_out, pool_w=v_pool_w, pool_b=v_pool_b, pool_scale=v_pool_scale, ffn2_norm=v_ffn2_norm,
             ffn2_w_gate=v_ffn2_w_gate, ffn2_w_up=v_ffn2_w_up, ffn2_w_down=v_ffn2_w_down, ple_norm=v_ple_norm,
             ple_w_gate=v_ple_w_gate, ple_w_proj=v_ple_w_proj, final_norm=v_final_norm)

    S = x.shape[1]
    my_x, my_y, my_c = _mesh_pos()
    my_chip = 2 * my_x + my_y
    pos = jnp.stack([my_chip, my_c]).astype(jnp.int32)
    n_lru, n_pool = lru_w_in.shape[0], pool_w.shape[0]

    tiny_shapes = [W[n].shape for n in TINY_SHARDED]
    tiny_local = _pack([W[n] for n in TINY_SHARDED], align=2 * 16 * LANES)[None]
    bufs = {n: _cast_place(_as3(W[n]), pos, BF16) for n in BIG}
    bufs["tiny"] = _cast_place(tiny_local, pos, F32)

    def gather_now(name, pieces):
        cargo = _gather_cargo(bufs, pieces)
        bufs.update(zip(cargo.names, _run_cargo(name, cargo)))

    def ffn_pieces(which, layer):
        return [("%s_w_gate" % which, layer), ("%s_w_up" % which, layer), ("%s_w_down" % which, layer)]

    def mixer_pieces(layer):
        if layer % 2 == 0:
            return [("lru_w_in", layer // 2), ("lru_w_out", layer // 2)]
        return [("pool_w", layer // 2)]

    gather_now("gather_first", [("tiny", 0)] + ffn_pieces("ffn1", 0))
    tiny_by_chip = [_unpack(bufs["tiny"][0, k], tiny_shapes) for k in range(N_CHIPS)]
    conv_w_full = jnp.concatenate([tiny_by_chip[k][0] for k in range(N_CHIPS)], axis=-1)
    pool_b_full = jnp.concatenate([tiny_by_chip[k][1] for k in range(N_CHIPS)], axis=-1)
    pool_s_full = jnp.concatenate([tiny_by_chip[k][2] for k in range(N_CHIPS)], axis=-1)
    ngroup = len(POOL_WINDOWS)

    def pool_weights():
        pw5 = bufs["pool_w"].reshape(n_pool, N_CHIPS, ngroup, POOL_GROUP_DIM // N_CHIPS, POOL_GROUP_DIM)
        return pw5.transpose(0, 2, 1, 3, 4).reshape(n_pool, ngroup, POOL_GROUP_DIM, POOL_GROUP_DIM)

    lru_out = lambda: bufs["lru_w_out"].reshape(n_lru, D_RNN, D_MODEL)
    ple_gate = lambda: bufs["ple_w_gate"].reshape(DEPTH, D_MODEL, D_MODEL)
    wbd = [_block_diag_gates(lru_w_a[j], lru_w_x[j]) for j in range(n_lru)]
    row = lambda a: a.reshape(1, -1)

    def ffn_forward(which, h, gamma, layer, pieces):
        cargo = _gather_cargo(bufs, pieces)
        outs, updated = _ffn_fwd(h, gamma, bufs[which + "_w_gate"], bufs[which + "_w_up"], bufs[which + "_w_down"],
                                 layer, cargo)
        bufs.update(zip(cargo.names, updated))
        return outs

    h = x.reshape(S, D_MODEL)
    saved = []
    for i in range(DEPTH):
        j = i // 2
        sv = {"h0": h}
        first_mixer = mixer_pieces(0) if i == 0 else []
        h, sv["xn1"], sv["g1"], sv["u1"] = ffn_forward(
            "ffn1", h, row(ffn1_norm[i]), i,
            first_mixer + ffn_pieces("ffn2", i) + [("ple_w_gate", i), ("ple_w_proj", i)])
        sv["h1"] = h
        if i % 2 == 0:
            h, sv["xn_mix"], sv["z"], sv["hs"] = _lru_fwd(
                h, row(mix_norm[i]), bufs["lru_w_in"], j, conv_w_full[j], row(lru_conv_b[j]), wbd[j],
                row(lru_b_a[j]), row(lru_b_x[j]), row(lru_a_param[j]), lru_out())
        else:
            h, sv["u"] = _pool_fwd(h, row(mix_norm[i]), pool_weights(), row(pool_b_full[j]), row(pool_s_full[j]), j)
        sv["h2"] = h
        nxt = ffn_pieces("ffn1", i + 1) + mixer_pieces(i + 1) if i + 1 < DEPTH else []
        h, sv["xn2"], sv["g2"], sv["u2"] = ffn_forward("ffn2", h, row(ffn2_norm[i]), i, nxt)
        sv["h3"] = h
        sv["p"] = p[i, 0]
        h = _ple_fwd(h, row(ple_norm[i]), sv["p"], ple_gate(), bufs["ple_w_proj"], i)
        saved.append(sv)

    dh, dgam_final, loss_part = _final(h, row(final_norm), loss_target.reshape(S, D_MODEL))
    win, wout, wpg, wpp, pw = bufs["lru_w_in"], lru_out(), ple_gate(), bufs["ple_w_proj"], pool_weights()

    norm_grads = {n: [None] * DEPTH for n in ("ffn1_norm", "mix_norm", "ffn2_norm", "ple_norm")}
    lru_vec = [None] * n_lru
    pool_vec = [None] * n_pool
    sum8 = lambda a: jnp.sum(a, axis=-2)

    to_siblings, to_chips = [], []
    stacked = {n: None for n in BIG}

    def take_cargo():
        a_items, b_items = list(to_siblings), list(to_chips)
        del to_siblings[:], to_chips[:]
        return _reduce_cargo([it[2] for it in a_items], [it[2] for it in b_items]), a_items, b_items

    def absorb(a_items, b_items, outs):
        for (n, layer, g), landed in zip(a_items, outs[:len(a_items)]):
            all_chunks, own = _presum_with_sibling(g, landed, pos)
            to_chips.append((n, layer, all_chunks, own))
        for (n, layer, _, own), from_chips in zip(b_items, outs[len(a_items):]):
            stacked[n] = _sum_chips(own, from_chips, stacked[n], layer, _as3(W[n]).shape, pos)

    def ffn_backward(which, xn, dout, gg, uu, layer, h_in, gamma):
        cargo, a_items, b_items = take_cargo()
        weights = (bufs[which + "_w_gate"], bufs[which + "_w_up"], bufs[which + "_w_down"])
        (dwg, dwu, dwd, slabs), c_outs = _ffn_bwd(xn, dout, gg, uu, *weights, layer, cargo)
        absorb(a_items, b_items, c_outs)
        dh_in, dgam, dwg, dwu, dwd = _ffn_bwd_last(xn, dout, gg, uu, *weights, layer, slabs, h_in, gamma,
                                                   dwg, dwu, dwd)
        to_siblings.extend([(which + "_w_gate", layer, dwg), (which + "_w_up", layer, dwu),
                            (which + "_w_down", layer, dwd)])
        return dh_in, sum8(dgam)

    for i in reversed(range(DEPTH)):
        j = i // 2
        sv = saved[i]
        dh, dw_pg, dw_pp, dgam = _ple_bwd(dh, sv["h3"], row(ple_norm[i]), sv["p"], wpg, wpp, i)
        norm_grads["ple_norm"][i] = sum8(dgam)
        to_siblings.append(("ple_w_gate", i, dw_pg.reshape(N_CHIPS, D_MODEL // N_CHIPS, D_MODEL)))
        to_siblings.append(("ple_w_proj", i, dw_pp))

        dh, norm_grads["ffn2_norm"][i] = ffn_backward("ffn2", sv["xn2"], dh, sv["g2"], sv["u2"], i, sv["h2"],
                                                      row(ffn2_norm[i]))

        if i % 2 == 0:
            dz, dpre, xc_b, y_b, dcw, vec = _lru_bwd_seq(
                dh, sv["z"], sv["hs"], conv_w_full[j], row(lru_conv_b[j]), wbd[j], row(lru_b_a[j]),
                row(lru_b_x[j]), row(lru_a_param[j]), wout, j)
            to_siblings.append(("lru_w_out", j, _xt_dy("lru_dw_out", y_b, dh, 1, D_RNN, D_MODEL, False, False)
                                .reshape(N_CHIPS, D_RNN // N_CHIPS, D_MODEL)))
            to_siblings.append(("lru_w_in", j, _xt_dy("lru_dw_in", sv["xn_mix"], dz, N_CHIPS, D_MODEL, RNN_IN_CHUNK,
                                                      False, True)))
            dwbd = _xt_dy("lru_dw_gates", xc_b, dpre, D_RNN // GATE_CHUNK, GATE_CHUNK, 2 * GATE_CHUNK, True, True)
            dw_a, dw_x = _block_diag_extract(dwbd)
            vsum = sum8(vec)
            lru_vec[j] = (sum8(dcw), vsum[0], vsum[1], vsum[2], vsum[3], dw_a, dw_x)
            dh, dgam = _lru_bwd_in(dz, sv["h1"], row(mix_norm[i]), dh, win, j)
            norm_grads["mix_norm"][i] = sum8(dgam)
        else:
            dh_new, dpre_b, vec = _pool_bwd(dh, sv["h1"], sv["u"], row(mix_norm[i]), pw, row(pool_b_full[j]),
                                            row(pool_s_full[j]), j)
            dpw = _xt_dy("pool_dw", sv["u"], dpre_b, ngroup, POOL_GROUP_DIM, POOL_GROUP_DIM, True, True)
            dpw = dpw.reshape(ngroup, N_CHIPS, POOL_GROUP_DIM // N_CHIPS, POOL_GROUP_DIM).transpose(1, 0, 2, 3)
            to_siblings.append(("pool_w", j, dpw.reshape(N_CHIPS, POOL_GROUP_DIM, POOL_GROUP_DIM)))
            vsum = sum8(vec)
            norm_grads["mix_norm"][i] = vsum[0]
            pool_vec[j] = (vsum[1], vsum[2])
            dh = dh_new

        dh, norm_grads["ffn1_norm"][i] = ffn_backward("ffn1", sv["xn1"], dh, sv["g1"], sv["u1"], i, sv["h0"],
                                                      row(ffn1_norm[i]))

    grad_x = dh.reshape(1, S, D_MODEL)

    tail = 0
    while to_siblings or to_chips:
        cargo, a_items, b_items = take_cargo()
        absorb(a_items, b_items, _run_cargo("grad_exchange_tail%d" % tail, cargo))
        tail += 1
    big_final = dict(zip(BIG, _join_siblings([stacked[n] for n in BIG])))

    small_parts = [
        jnp.stack(norm_grads["ffn1_norm"]), jnp.stack(norm_grads["mix_norm"]),
        jnp.stack(norm_grads["ffn2_norm"]), jnp.stack(norm_grads["ple_norm"]), sum8(dgam_final),
        jnp.stack([lv[0] for lv in lru_vec]), jnp.stack([lv[1] for lv in lru_vec]),
        jnp.stack([lv[2] for lv in lru_vec]), jnp.stack([lv[3] for lv in lru_vec]),
        jnp.stack([lv[4] for lv in lru_vec]), jnp.stack([lv[5] for lv in lru_vec]),
        jnp.stack([lv[6] for lv in lru_vec]),
        jnp.stack([pv[0] for pv in pool_vec]), jnp.stack([pv[1] for pv in pool_vec]),
        jnp.sum(loss_part).reshape(1),
    ]
    small_names = ("ffn1_norm", "mix_norm", "ffn2_norm", "ple_norm", "final_norm", "lru_conv_w", "lru_conv_b",
                   "lru_b_a", "lru_b_x", "lru_a_param", "lru_w_a", "lru_w_x", "pool_b", "pool_scale", "loss")
    reduced = _unpack(_allreduce_small(_pack(small_parts)), [sp.shape for sp in small_parts])
    small_grad = dict(zip(small_names, reduced))
    loss = small_grad.pop("loss").reshape(())
    for n in TINY_SHARDED:
        width = W[n].shape[-1]
        small_grad[n] = lax.dynamic_slice_in_dim(small_grad[n], my_chip * width, width, axis=-1)

    grads, deltas, new_m, new_v = {}, {}, {}, {}
    for n in BIG:
        shp = W[n].shape
        to2 = lambda a: a.reshape(-1, shp[-1])
        g2, d, m2, v2 = _adamw(to2(W[n]), to2(big_final[n]), to2(M[n]), to2(V[n]))
        grads[n], deltas[n], new_m[n], new_v[n] = (a.reshape(shp) for a in (g2, d, m2, v2))
    small_order = TINY_SHARDED + REPLICATED
    small_shapes = [W[n].shape for n in small_order]
    pack_rows = functools.partial(_pack, align=512 * LANES)
    _, sd, sm, sv_ = _adamw(pack_rows([W[n] for n in small_order]), pack_rows([small_grad[n] for n in small_order]),
                            pack_rows([M[n] for n in small_order]), pack_rows([V[n] for n in small_order]))
    for n, d, m2, v2 in zip(small_order, _unpack(sd, small_shapes), _unpack(sm, small_shapes),
                            _unpack(sv_, small_shapes)):
        grads[n], deltas[n], new_m[n], new_v[n] = small_grad[n].reshape(W[n].shape), d, m2, v2

    return (loss, grad_x, *[grads[n] for n in WEIGHT_ORDER], *[deltas[n] for n in WEIGHT_ORDER],
            *[new_m[n] for n in WEIGHT_ORDER], *[new_v[n] for n in WEIGHT_ORDER])
```

```python
import functools

import jax
import jax.numpy as jnp
from jax import lax
from jax.experimental import pallas as pl
from jax.experimental.pallas import tpu as pltpu

F32 = jnp.float32
BF16 = jnp.bfloat16

D_MODEL = 1024
D_FF = 2816
D_RNN = 1280
DEPTH = 4
N_CHIPS = 4
FF_CHUNK = D_FF // N_CHIPS
RNN_IN_CHUNK = 2 * D_RNN // N_CHIPS
GATE_CHUNK = 640
LRU_HEADS = 16
LRU_HEAD_DIM = 80
CONV_WIDTH = 4
LRU_C = 8.0
POOL_WINDOWS = (2, 4, 8, 16)
POOL_GROUP_DIM = 256
PLE_DIM = 256
RMS_EPS = 1e-6
POOL_HALO = 16
SUBLANES = 8
LANES = 128

ADAM_LR = 0.001
ADAM_B1 = 0.9
ADAM_B2 = 0.999
ADAM_EPS = 1e-08
ADAM_WD = 0.01
ADAM_STEP = 10

VMEM_LIMIT_MB = 56
MESH_ID = pl.DeviceIdType.MESH
CHIP_FLIPS = ((1, 0), (0, 1), (1, 1))


def _cparams(semantics):
    return pltpu.CompilerParams(dimension_semantics=semantics, vmem_limit_bytes=VMEM_LIMIT_MB * 2 ** 20)


def _dot(a, b):
    return lax.dot_general(a, b, (((1,), (0,)), ((), ())), preferred_element_type=F32)


def _dot_nt(a, b):
    return lax.dot_general(a, b, (((1,), (1,)), ((), ())), preferred_element_type=F32)


def _dot_tn(a, b):
    return lax.dot_general(a, b, (((0,), (0,)), ((), ())), preferred_element_type=F32)


def _sigmoid(x):
    return 1.0 / (1.0 + jnp.exp(-x))


def _rms(hf, gamma):
    rstd = lax.rsqrt(jnp.mean(hf * hf, axis=-1, keepdims=True) + RMS_EPS)
    xhat = hf * rstd
    return xhat, rstd, xhat * gamma


def _rms_bwd(xhat, rstd, gamma, dxn):
    dxhat = dxn * gamma
    m = jnp.mean(dxhat * xhat, axis=-1, keepdims=True)
    return rstd * (dxhat - xhat * m), _rowsum8(dxn * xhat)


def _rowsum8(v):
    tm, n = v.shape
    return jnp.sum(v.reshape(tm // SUBLANES, SUBLANES, n), axis=0)


def _gelu(x):
    u = 0.7978845608028654 * (x + 0.044715 * x * x * x)
    return 0.5 * x * (1.0 + jnp.tanh(u))


def _gelu_and_grad(x):
    c = 0.7978845608028654
    u = c * (x + 0.044715 * x * x * x)
    th = jnp.tanh(u)
    g = 0.5 * x * (1.0 + th)
    dg = 0.5 * (1.0 + th) + 0.5 * x * (1.0 - th * th) * c * (1.0 + 3.0 * 0.044715 * x * x)
    return g, dg


def _softplus(z):
    e = jnp.exp(-jnp.abs(z))
    u = 1.0 + e
    log1p = jnp.where(u == 1.0, e, jnp.log(u) * e / jnp.where(u == 1.0, 1.0, u - 1.0))
    return jnp.maximum(z, 0.0) + log1p


def _neg_expm1(x):
    series = -(x + 0.5 * x * x + (1.0 / 6.0) * x * x * x)
    return jnp.where(x > -1e-2, series, 1.0 - jnp.exp(x))


def _shift_down(ext, j, halo):
    return pltpu.roll(ext, j, 0)[halo:]


def _shift_up(ext, j, tm):
    n = ext.shape[0]
    return pltpu.roll(ext, n - j, 0)[:tm]


def _scan_causal(a, b):
    tm, n = a.shape
    head_rows = lax.broadcasted_iota(jnp.int32, (SUBLANES, n), 0)
    s = 1
    while s < min(SUBLANES, tm):
        keep = head_rows >= s
        a_r, b_r = pltpu.roll(a, s, 0), pltpu.roll(b, s, 0)
        a_sh = jnp.concatenate([jnp.where(keep, a_r[:SUBLANES], 1.0), a_r[SUBLANES:]], axis=0)
        b_sh = jnp.concatenate([jnp.where(keep, b_r[:SUBLANES], 0.0), b_r[SUBLANES:]], axis=0)
        b = a * b_sh + b
        a = a * a_sh
        s *= 2
    while s < tm:
        b = jnp.concatenate([b[:s], a[s:] * b[:tm - s] + b[s:]], axis=0)
        a = jnp.concatenate([a[:s], a[s:] * a[:tm - s]], axis=0)
        s *= 2
    return a, b


def _scan_anticausal(c, d):
    tm, n = c.shape
    body = tm - SUBLANES
    tail_rows = lax.broadcasted_iota(jnp.int32, (SUBLANES, n), 0) + body
    s = 1
    while s < min(SUBLANES, tm):
        keep = tail_rows < tm - s
        c_r, d_r = pltpu.roll(c, tm - s, 0), pltpu.roll(d, tm - s, 0)
        c_sh = jnp.concatenate([c_r[:body], jnp.where(keep, c_r[body:], 1.0)], axis=0)
        d_sh = jnp.concatenate([d_r[:body], jnp.where(keep, d_r[body:], 0.0)], axis=0)
        d = d + c * d_sh
        c = c * c_sh
        s *= 2
    while s < tm:
        d = jnp.concatenate([d[:tm - s] + c[:tm - s] * d[s:], d[tm - s:]], axis=0)
        c = jnp.concatenate([c[:tm - s] * c[s:], c[tm - s:]], axis=0)
        s *= 2
    return c, d


def _tile(n, want):
    t = min(n, want)
    assert n % t == 0, (n, t)
    return t


def _ffn_fwd(h, gamma, wg, wu, wd, layer, cargo=None):
    S = h.shape[0]
    tm = _tile(S, 1024)
    nt = S // tm
    cargo = cargo or _Cargo()
    n_in, n_out = 5, 4
    nc_in, nc_out = len(cargo.operands), len(cargo.out_shapes)

    def body(*refs):
        h_ref, g_ref, wg_ref, wu_ref, wd_ref = refs[:n_in]
        c_ins = refs[n_in:n_in + nc_in]
        ho_ref, xn_ref, gg_ref, uu_ref = refs[n_in + nc_in:n_in + nc_in + n_out]
        c_outs = refs[n_in + nc_in + n_out:n_in + nc_in + n_out + nc_out]
        xn_s, acc_s = refs[n_in + nc_in + n_out + nc_out:n_in + nc_in + n_out + nc_out + 2]
        sems = refs[n_in + nc_in + n_out + nc_out + 2:]
        t, k = pl.program_id(0), pl.program_id(1)

        @pl.when((t == 0) & (k == 0))
        def _():
            cargo.start(c_ins, c_outs, sems)

        @pl.when((t == nt - 1) & (k == 0))
        def _():
            cargo.forward(c_ins, c_outs, sems)

        @pl.when(k == 0)
        def _():
            _, _, xn = _rms(h_ref[...], g_ref[...])
            xnb = xn.astype(BF16)
            xn_s[...] = xnb
            xn_ref[...] = xnb
            acc_s[...] = jnp.zeros_like(acc_s)

        xnb = xn_s[...]
        g = _dot_nt(xnb, wg_ref[...])
        u = _dot_nt(xnb, wu_ref[...])
        gg_ref[...] = g.astype(BF16)
        uu_ref[...] = u.astype(BF16)
        hid = (g * _sigmoid(g)) * u
        acc_s[...] += _dot(hid.astype(BF16), wd_ref[...])

        @pl.when(k == N_CHIPS - 1)
        def _():
            ho_ref[...] = h_ref[...] + 0.5 * acc_s[...]

        @pl.when((t == nt - 1) & (k == N_CHIPS - 1))
        def _():
            cargo.finish(c_ins, c_outs, sems)

    outs = pl.pallas_call(
        body, name="ffn_fwd",
        grid=(nt, N_CHIPS),
        in_specs=[
            pl.BlockSpec((tm, D_MODEL), lambda t, k: (t, 0)),
            pl.BlockSpec((1, D_MODEL), lambda t, k: (0, 0)),
            pl.BlockSpec((None, None, FF_CHUNK, D_MODEL), lambda t, k: (layer, k, 0, 0)),
            pl.BlockSpec((None, None, FF_CHUNK, D_MODEL), lambda t, k: (layer, k, 0, 0)),
            pl.BlockSpec((None, None, FF_CHUNK, D_MODEL), lambda t, k: (layer, k, 0, 0)),
        ] + _any_specs(nc_in),
        out_specs=[
            pl.BlockSpec((tm, D_MODEL), lambda t, k: (t, 0)),
            pl.BlockSpec((tm, D_MODEL), lambda t, k: (t, 0)),
            pl.BlockSpec((None, tm, FF_CHUNK), lambda t, k: (k, t, 0)),
            pl.BlockSpec((None, tm, FF_CHUNK), lambda t, k: (k, t, 0)),
        ] + _any_specs(nc_out),
        out_shape=[
            jax.ShapeDtypeStruct((S, D_MODEL), F32),
            jax.ShapeDtypeStruct((S, D_MODEL), BF16),
            jax.ShapeDtypeStruct((N_CHIPS, S, FF_CHUNK), BF16),
            jax.ShapeDtypeStruct((N_CHIPS, S, FF_CHUNK), BF16),
        ] + cargo.out_shapes,
        input_output_aliases={n_in + i: n_out + o for i, o in cargo.aliases.items()},
        scratch_shapes=[pltpu.VMEM((tm, D_MODEL), BF16), pltpu.VMEM((tm, D_MODEL), F32)] + cargo.sem_shapes,
        compiler_params=_cparams(("arbitrary", "arbitrary")),
    )(h, gamma, wg, wu, wd, *cargo.operands)
    return outs[:n_out], list(outs[n_out:])


def _ffn_bwd(xn, dout, gg, uu, wg, wu, wd, layer, cargo=None):
    S = xn.shape[0]
    tm = _tile(S, 512)
    nt = S // tm
    nchunk = N_CHIPS - 1
    cargo = cargo or _Cargo()
    n_in, n_out = 7, 4
    nc_in, nc_out = len(cargo.operands), len(cargo.out_shapes)

    def body(*refs):
        xn_ref, do_ref, gg_ref, uu_ref, wg_ref, wu_ref, wd_ref = refs[:n_in]
        c_ins = refs[n_in:n_in + nc_in]
        dwg_ref, dwu_ref, dwd_ref, slab_ref = refs[n_in + nc_in:n_in + nc_in + n_out]
        c_outs = refs[n_in + nc_in + n_out:n_in + nc_in + n_out + nc_out]
        sems = refs[n_in + nc_in + n_out + nc_out:]
        k, t = pl.program_id(0), pl.program_id(1)

        @pl.when((k == 0) & (t == 0))
        def _():
            cargo.start(c_ins, c_outs, sems)

        @pl.when(t == 0)
        def _():
            dwg_ref[...] = jnp.zeros_like(dwg_ref)
            dwu_ref[...] = jnp.zeros_like(dwu_ref)
            dwd_ref[...] = jnp.zeros_like(dwd_ref)

        xnb = xn_ref[...]
        dob = (0.5 * do_ref[...]).astype(BF16)
        g = gg_ref[...].astype(F32)
        u = uu_ref[...].astype(F32)
        s = _sigmoid(g)
        sil = g * s
        dhid = _dot_nt(dob, wd_ref[...])
        dwd_ref[...] += _dot_tn((sil * u).astype(BF16), dob)
        du = (dhid * sil).astype(BF16)
        dg = (dhid * u * (s * (1.0 + g * (1.0 - s)))).astype(BF16)
        dwg_ref[...] += _dot_tn(dg, xnb)
        dwu_ref[...] += _dot_tn(du, xnb)
        slab_ref[...] = (_dot(dg, wg_ref[...]) + _dot(du, wu_ref[...])).astype(BF16)

        @pl.when((k == nchunk - 1) & (t == nt - 1))
        def _():
            cargo.finish(c_ins, c_outs, sems)

    outs = pl.pallas_call(
        body, name="ffn_bwd",
        grid=(nchunk, nt),
        in_specs=[
            pl.BlockSpec((tm, D_MODEL), lambda k, t: (t, 0)),
            pl.BlockSpec((tm, D_MODEL), lambda k, t: (t, 0)),
            pl.BlockSpec((None, tm, FF_CHUNK), lambda k, t: (k, t, 0)),
            pl.BlockSpec((None, tm, FF_CHUNK), lambda k, t: (k, t, 0)),
            pl.BlockSpec((None, None, FF_CHUNK, D_MODEL), lambda k, t: (layer, k, 0, 0)),
            pl.BlockSpec((None, None, FF_CHUNK, D_MODEL), lambda k, t: (layer, k, 0, 0)),
            pl.BlockSpec((None, None, FF_CHUNK, D_MODEL), lambda k, t: (layer, k, 0, 0)),
        ] + _any_specs(nc_in),
        out_specs=[
            pl.BlockSpec((None, FF_CHUNK, D_MODEL), lambda k, t: (k, 0, 0)),
            pl.BlockSpec((None, FF_CHUNK, D_MODEL), lambda k, t: (k, 0, 0)),
            pl.BlockSpec((None, FF_CHUNK, D_MODEL), lambda k, t: (k, 0, 0)),
            pl.BlockSpec((None, tm, D_MODEL), lambda k, t: (k, t, 0)),
        ] + _any_specs(nc_out),
        out_shape=[
            jax.ShapeDtypeStruct((N_CHIPS, FF_CHUNK, D_MODEL), F32),
            jax.ShapeDtypeStruct((N_CHIPS, FF_CHUNK, D_MODEL), F32),
            jax.ShapeDtypeStruct((N_CHIPS, FF_CHUNK, D_MODEL), F32),
            jax.ShapeDtypeStruct((nchunk, S, D_MODEL), BF16),
        ] + cargo.out_shapes,
        input_output_aliases={n_in + i: n_out + o for i, o in cargo.aliases.items()},
        scratch_shapes=list(cargo.sem_shapes),
        compiler_params=_cparams(("arbitrary", "arbitrary")),
    )(xn, dout, gg, uu, wg, wu, wd, *cargo.operands)
    return outs[:n_out], list(outs[n_out:])


def _ffn_bwd_last(xn, dout, gg, uu, wg, wu, wd, layer, slabs, h, gamma, dwg, dwu, dwd):
    S = xn.shape[0]
    tm = _tile(S, 512)
    k = N_CHIPS - 1
    nprev = slabs.shape[0]

    def body(xn_ref, do_ref, gg_ref, uu_ref, wg_ref, wu_ref, wd_ref, slab_ref, h_ref, g_ref, _dwg, _dwu, _dwd,
             dh_ref, dgam_ref, dwg_ref, dwu_ref, dwd_ref):
        @pl.when(pl.program_id(0) == 0)
        def _():
            dgam_ref[...] = jnp.zeros_like(dgam_ref)
            dwg_ref[...] = jnp.zeros_like(dwg_ref)
            dwu_ref[...] = jnp.zeros_like(dwu_ref)
            dwd_ref[...] = jnp.zeros_like(dwd_ref)

        xnb = xn_ref[...]
        do = do_ref[...]
        dob = (0.5 * do).astype(BF16)
        g = gg_ref[...].astype(F32)
        u = uu_ref[...].astype(F32)
        s = _sigmoid(g)
        sil = g * s
        dhid = _dot_nt(dob, wd_ref[...])
        dwd_ref[...] += _dot_tn((sil * u).astype(BF16), dob)
        du = (dhid * sil).astype(BF16)
        dg = (dhid * u * (s * (1.0 + g * (1.0 - s)))).astype(BF16)
        dwg_ref[...] += _dot_tn(dg, xnb)
        dwu_ref[...] += _dot_tn(du, xnb)
        dxn = _dot(dg, wg_ref[...]) + _dot(du, wu_ref[...])
        for i in range(nprev):
            dxn = dxn + slab_ref[i].astype(F32)
        xhat, rstd, _ = _rms(h_ref[...], g_ref[...])
        dhn, dgam = _rms_bwd(xhat, rstd, g_ref[...], dxn)
        dh_ref[...] = do + dhn
        dgam_ref[...] += dgam

    tile = pl.BlockSpec((tm, D_MODEL), lambda t: (t, 0))
    hidden = pl.BlockSpec((None, tm, FF_CHUNK), lambda t: (k, t, 0))
    w_in = pl.BlockSpec((None, None, FF_CHUNK, D_MODEL), lambda t: (layer, k, 0, 0))
    dw_in = pl.BlockSpec((None, FF_CHUNK, D_MODEL), lambda t: (k, 0, 0))
    return pl.pallas_call(
        body, name="ffn_bwd_last",
        grid=(S // tm,),
        in_specs=[tile, tile, hidden, hidden, w_in, w_in,
                  pl.BlockSpec((None, None, FF_CHUNK, D_MODEL), lambda t: (layer, k, 0, 0)),
                  pl.BlockSpec((nprev, tm, D_MODEL), lambda t: (0, t, 0)), tile,
                  pl.BlockSpec((1, D_MODEL), lambda t: (0, 0))] + _any_specs(3),
        out_specs=[tile, pl.BlockSpec((SUBLANES, D_MODEL), lambda t: (0, 0)), dw_in, dw_in,
                   pl.BlockSpec((None, FF_CHUNK, D_MODEL), lambda t: (k, 0, 0))],
        out_shape=[jax.ShapeDtypeStruct((S, D_MODEL), F32), jax.ShapeDtypeStruct((SUBLANES, D_MODEL), F32),
                   jax.ShapeDtypeStruct(dwg.shape, F32), jax.ShapeDtypeStruct(dwu.shape, F32),
                   jax.ShapeDtypeStruct(dwd.shape, F32)],
        input_output_aliases={10: 2, 11: 3, 12: 4},
        compiler_params=_cparams(("arbitrary",)),
    )(xn, dout, gg, uu, wg, wu, wd, slabs, h, gamma, dwg, dwu, dwd)


def _xt_dy(name, x, dy, nchunk, kb, nb, x_by_chunk, y_by_chunk):
    S = x.shape[0]
    tm = _tile(S, 2048)

    def body(x_ref, dy_ref, o_ref):
        @pl.when(pl.program_id(1) == 0)
        def _():
            o_ref[...] = jnp.zeros_like(o_ref)

        o_ref[...] += _dot_tn(x_ref[...].astype(BF16), dy_ref[...].astype(BF16))

    return pl.pallas_call(
        body, name=name,
        grid=(nchunk, S // tm),
        in_specs=[
            pl.BlockSpec((tm, kb), (lambda c, t: (t, c)) if x_by_chunk else (lambda c, t: (t, 0))),
            pl.BlockSpec((tm, nb), (lambda c, t: (t, c)) if y_by_chunk else (lambda c, t: (t, 0))),
        ],
        out_specs=pl.BlockSpec((None, kb, nb), lambda c, t: (c, 0, 0)),
        out_shape=jax.ShapeDtypeStruct((nchunk, kb, nb), F32),
        compiler_params=_cparams(("arbitrary", "arbitrary")),
    )(x, dy)


def _lru_gates(xc, wbd_ref, ba, bx, apar):
    xcb = xc.astype(BF16)
    r_parts, ig_parts = [], []
    for q in range(D_RNN // GATE_CHUNK):
        lo, hi = q * GATE_CHUNK, (q + 1) * GATE_CHUNK
        pre = _dot(xcb[:, lo:hi], wbd_ref[q])
        r_parts.append(_sigmoid(pre[:, :GATE_CHUNK] + ba[:, lo:hi]))
        ig_parts.append(_sigmoid(pre[:, GATE_CHUNK:] + bx[:, lo:hi]))
    r = jnp.concatenate(r_parts, axis=1)
    ig = jnp.concatenate(ig_parts, axis=1)
    sp = LRU_C * _softplus(-apar)
    log_a = -(r * sp)
    a = jnp.exp(log_a)
    mult = jnp.sqrt(_neg_expm1(2.0 * log_a))
    return r, ig, a, mult, sp


def _conv_causal(xb, tail, cw_ref, cb):
    ext = jnp.concatenate([tail, xb], axis=0)
    xc = cb + cw_ref[CONV_WIDTH - 1:CONV_WIDTH, :] * xb
    for j in range(1, CONV_WIDTH):
        xc = xc + cw_ref[CONV_WIDTH - 1 - j:CONV_WIDTH - j, :] * _shift_down(ext, j, SUBLANES)
    return xc, ext


def _lru_fwd(h, gamma, win, layer, convw, convb, wbd, ba, bx, apar, wout):
    S = h.shape[0]
    tm = _tile(S, 256)

    def body(h_ref, g_ref, win_ref, cw_ref, cb_ref, wbd_ref, ba_ref, bx_ref, ap_ref, wout_ref,
             ho_ref, xn_ref, z_ref, hs_ref, tail_s, carry_s):
        @pl.when(pl.program_id(0) == 0)
        def _():
            tail_s[...] = jnp.zeros_like(tail_s)
            carry_s[...] = jnp.zeros_like(carry_s)

        hf = h_ref[...]
        _, _, xn = _rms(hf, g_ref[...])
        xnb = xn.astype(BF16)
        xn_ref[...] = xnb
        for k in range(N_CHIPS):
            z_ref[:, k * RNN_IN_CHUNK:(k + 1) * RNN_IN_CHUNK] = _dot(xnb, win_ref[k])
        gate = z_ref[:, :D_RNN]
        xb = z_ref[:, D_RNN:]
        xc, _ = _conv_causal(xb, tail_s[...], cw_ref, cb_ref[...])
        tail_s[...] = xb[tm - SUBLANES:, :]
        _, ig, a, mult, _ = _lru_gates(xc, wbd_ref, ba_ref[...], bx_ref[...], ap_ref[...])
        big_a, big_b = _scan_causal(a, mult * (ig * xc))
        hs = big_a * carry_s[SUBLANES - 1:SUBLANES, :] + big_b
        hs_ref[...] = hs
        carry_s[...] = hs[tm - SUBLANES:, :]
        y = hs * _gelu(gate)
        ho_ref[...] = hf + _dot(y.astype(BF16), wout_ref[...])

    row = lambda n: pl.BlockSpec((1, n), lambda t: (0, 0))
    return pl.pallas_call(
        body, name="lru_fwd",
        grid=(S // tm,),
        in_specs=[
            pl.BlockSpec((tm, D_MODEL), lambda t: (t, 0)),
            row(D_MODEL),
            pl.BlockSpec((None, N_CHIPS, D_MODEL, RNN_IN_CHUNK), lambda t: (layer, 0, 0, 0)),
            pl.BlockSpec((CONV_WIDTH, D_RNN), lambda t: (0, 0)),
            row(D_RNN),
            pl.BlockSpec((D_RNN // GATE_CHUNK, GATE_CHUNK, 2 * GATE_CHUNK), lambda t: (0, 0, 0)),
            row(D_RNN), row(D_RNN), row(D_RNN),
            pl.BlockSpec((None, D_RNN, D_MODEL), lambda t: (layer, 0, 0)),
        ],
        out_specs=[
            pl.BlockSpec((tm, D_MODEL), lambda t: (t, 0)),
            pl.BlockSpec((tm, D_MODEL), lambda t: (t, 0)),
            pl.BlockSpec((tm, 2 * D_RNN), lambda t: (t, 0)),
            pl.BlockSpec((tm, D_RNN), lambda t: (t, 0)),
        ],
        out_shape=[
            jax.ShapeDtypeStruct((S, D_MODEL), F32),
            jax.ShapeDtypeStruct((S, D_MODEL), BF16),
            jax.ShapeDtypeStruct((S, 2 * D_RNN), F32),
            jax.ShapeDtypeStruct((S, D_RNN), F32),
        ],
        scratch_shapes=[pltpu.VMEM((SUBLANES, D_RNN), F32), pltpu.VMEM((SUBLANES, D_RNN), F32)],
        compiler_params=_cparams(("arbitrary",)),
    )(h, gamma, win, convw, convb, wbd, ba, bx, apar, wout)


def _lru_bwd_seq(dout, z, hs, convw, convb, wbd, ba, bx, apar, wout, layer):
    S = dout.shape[0]
    tm = _tile(S, 256)
    nt = S // tm
    per8 = tm // SUBLANES
    rev = lambda i: nt - 1 - i
    prev8 = lambda i: jnp.maximum(rev(i) * per8 - 1, 0)

    def body(do_ref, z_ref, hs_ref, ztail_ref, hstail_ref, cw_ref, cb_ref, wbd_ref, ba_ref, bx_ref, ap_ref, wout_ref,
             dz_ref, dpre_ref, xc_ref, y_ref, dcw_ref, vec_ref, a_first_s, g_first_s, dxc_head_s):
        i = pl.program_id(0)
        first_in_time = rev(i) == 0

        @pl.when(i == 0)
        def _():
            a_first_s[...] = jnp.zeros_like(a_first_s)
            g_first_s[...] = jnp.zeros_like(g_first_s)
            dxc_head_s[...] = jnp.zeros_like(dxc_head_s)
            dcw_ref[...] = jnp.zeros_like(dcw_ref)
            vec_ref[...] = jnp.zeros_like(vec_ref)

        gate = z_ref[:, :D_RNN]
        xb = z_ref[:, D_RNN:]
        hist = jnp.where(first_in_time, 0.0, 1.0)
        xc, xext = _conv_causal(xb, ztail_ref[:, D_RNN:] * hist, cw_ref, cb_ref[...])
        r, ig, a, mult, sp = _lru_gates(xc, wbd_ref, ba_ref[...], bx_ref[...], ap_ref[...])
        hs = hs_ref[...]
        gel, dgel = _gelu_and_grad(gate)
        y = hs * gel
        y_ref[...] = y.astype(BF16)
        xc_ref[...] = xc.astype(BF16)

        dy = _dot_nt(do_ref[...].astype(BF16), wout_ref[...])
        dhs = dy * gel
        dgate = dy * hs * dgel

        coef = _shift_up(jnp.concatenate([a, a_first_s[...]], axis=0), 1, tm)
        big_c, big_d = _scan_anticausal(coef, dhs)
        g = big_d + big_c * g_first_s[0:1, :]
        g_first_s[...] = g[:SUBLANES, :]
        a_first_s[...] = a[:SUBLANES, :]

        hs_prev = _shift_down(jnp.concatenate([hstail_ref[...] * hist, hs], axis=0), 1, SUBLANES)
        da = g * hs_prev
        dmult = g * ig * xc
        dig = g * mult * xc
        dxc = g * mult * ig
        dlog_a = da * a - dmult * (a * a) / mult
        dr = -(dlog_a * sp)
        dpre_a = dr * r * (1.0 - r)
        dpre_x = dig * ig * (1.0 - ig)
        d_apar = dlog_a * r * (LRU_C * _sigmoid(-ap_ref[...]))

        for q in range(D_RNN // GATE_CHUNK):
            lo, hi = q * GATE_CHUNK, (q + 1) * GATE_CHUNK
            dpre_q = jnp.concatenate([dpre_a[:, lo:hi], dpre_x[:, lo:hi]], axis=1).astype(BF16)
            dpre_ref[:, 2 * lo:2 * hi] = dpre_q
            dxc_q = _dot_nt(dpre_q, wbd_ref[q])
            if q == 0:
                dxc_parts = [dxc_q]
            else:
                dxc_parts.append(dxc_q)
        dxc = dxc + jnp.concatenate(dxc_parts, axis=1)

        dext = jnp.concatenate([dxc, dxc_head_s[...]], axis=0)
        dxb = cw_ref[CONV_WIDTH - 1:CONV_WIDTH, :] * dxc
        for j in range(1, CONV_WIDTH):
            dxb = dxb + cw_ref[CONV_WIDTH - 1 - j:CONV_WIDTH - j, :] * _shift_up(dext, j, tm)
        dxc_head_s[...] = dxc[:SUBLANES, :]
        dz_ref[:, :D_RNN] = dgate.astype(BF16)
        dz_ref[:, D_RNN:] = dxb.astype(BF16)

        dcw_ref[CONV_WIDTH - 1] += _rowsum8(dxc * xb)
        for j in range(1, CONV_WIDTH):
            dcw_ref[CONV_WIDTH - 1 - j] += _rowsum8(dxc * _shift_down(xext, j, SUBLANES))
        vec_ref[0] += _rowsum8(dxc)
        vec_ref[1] += _rowsum8(dpre_a)
        vec_ref[2] += _rowsum8(dpre_x)
        vec_ref[3] += _rowsum8(d_apar)

    row = lambda n: pl.BlockSpec((1, n), lambda i: (0, 0))
    return pl.pallas_call(
        body, name="lru_bwd_seq",
        grid=(nt,),
        in_specs=[
            pl.BlockSpec((tm, D_MODEL), lambda i: (rev(i), 0)),
            pl.BlockSpec((tm, 2 * D_RNN), lambda i: (rev(i), 0)),
            pl.BlockSpec((tm, D_RNN), lambda i: (rev(i), 0)),
            pl.BlockSpec((SUBLANES, 2 * D_RNN), lambda i: (prev8(i), 0)),
            pl.BlockSpec((SUBLANES, D_RNN), lambda i: (prev8(i), 0)),
            pl.BlockSpec((CONV_WIDTH, D_RNN), lambda i: (0, 0)),
            row(D_RNN),
            pl.BlockSpec((D_RNN // GATE_CHUNK, GATE_CHUNK, 2 * GATE_CHUNK), lambda i: (0, 0, 0)),
            row(D_RNN), row(D_RNN), row(D_RNN),
            pl.BlockSpec((None, D_RNN, D_MODEL), lambda i: (layer, 0, 0)),
        ],
        out_specs=[
            pl.BlockSpec((tm, 2 * D_RNN), lambda i: (rev(i), 0)),
            pl.BlockSpec((tm, 2 * D_RNN), lambda i: (rev(i), 0)),
            pl.BlockSpec((tm, D_RNN), lambda i: (rev(i), 0)),
            pl.BlockSpec((tm, D_RNN), lambda i: (rev(i), 0)),
            pl.BlockSpec((CONV_WIDTH, SUBLANES, D_RNN), lambda i: (0, 0, 0)),
            pl.BlockSpec((4, SUBLANES, D_RNN), lambda i: (0, 0, 0)),
        ],
        out_shape=[
            jax.ShapeDtypeStruct((S, 2 * D_RNN), BF16),
            jax.ShapeDtypeStruct((S, 2 * D_RNN), BF16),
            jax.ShapeDtypeStruct((S, D_RNN), BF16),
            jax.ShapeDtypeStruct((S, D_RNN), BF16),
            jax.ShapeDtypeStruct((CONV_WIDTH, SUBLANES, D_RNN), F32),
            jax.ShapeDtypeStruct((4, SUBLANES, D_RNN), F32),
        ],
        scratch_shapes=[pltpu.VMEM((SUBLANES, D_RNN), F32)] * 3,
        compiler_params=_cparams(("arbitrary",)),
    )(dout, z, hs, z, hs, convw, convb, wbd, ba, bx, apar, wout)


def _lru_bwd_in(dz, h, gamma, dres, win, layer):
    S = h.shape[0]
    tm = _tile(S, 512)

    def body(dz_ref, h_ref, g_ref, dres_ref, win_ref, dh_ref, dgam_ref):
        dxn = _dot_nt(dz_ref[:, :RNN_IN_CHUNK], win_ref[0])
        for k in range(1, N_CHIPS):
            dxn = dxn + _dot_nt(dz_ref[:, k * RNN_IN_CHUNK:(k + 1) * RNN_IN_CHUNK], win_ref[k])
        xhat, rstd, _ = _rms(h_ref[...], g_ref[...])
        dhn, dgam = _rms_bwd(xhat, rstd, g_ref[...], dxn)
        dh_ref[...] = dres_ref[...] + dhn

        @pl.when(pl.program_id(0) == 0)
        def _():
            dgam_ref[...] = jnp.zeros_like(dgam_ref)

        dgam_ref[...] += dgam

    return pl.pallas_call(
        body, name="lru_bwd_in",
        grid=(S // tm,),
        in_specs=[
            pl.BlockSpec((tm, 2 * D_RNN), lambda t: (t, 0)),
            pl.BlockSpec((tm, D_MODEL), lambda t: (t, 0)),
            pl.BlockSpec((1, D_MODEL), lambda t: (0, 0)),
            pl.BlockSpec((tm, D_MODEL), lambda t: (t, 0)),
            pl.BlockSpec((None, N_CHIPS, D_MODEL, RNN_IN_CHUNK), lambda t: (layer, 0, 0, 0)),
        ],
        out_specs=[
            pl.BlockSpec((tm, D_MODEL), lambda t: (t, 0)),
            pl.BlockSpec((SUBLANES, D_MODEL), lambda t: (0, 0)),
        ],
        out_shape=[jax.ShapeDtypeStruct((S, D_MODEL), F32), jax.ShapeDtypeStruct((SUBLANES, D_MODEL), F32)],
        compiler_params=_cparams(("arbitrary",)),
    )(dz, h, gamma, dres, win)


def _pool_inv_count(t_index, tm):
    rows = (lax.broadcasted_iota(jnp.int32, (tm, D_MODEL), 0) + t_index * tm + 1).astype(F32)
    col = lax.broadcasted_iota(jnp.int32, (tm, D_MODEL), 1)
    win = jnp.where(col < POOL_GROUP_DIM, float(POOL_WINDOWS[0]),
                    jnp.where(col < 2 * POOL_GROUP_DIM, float(POOL_WINDOWS[1]),
                              jnp.where(col < 3 * POOL_GROUP_DIM, float(POOL_WINDOWS[2]), float(POOL_WINDOWS[3]))))
    return 1.0 / jnp.minimum(rows, win)


def _window_sums(ext, shift, take):
    gd = POOL_GROUP_DIM
    s2 = ext + shift(ext, 1)
    s4 = s2[:, gd:] + shift(s2[:, gd:], 2)
    s8 = s4[:, gd:] + shift(s4[:, gd:], 4)
    s16 = s8[:, gd:] + shift(s8[:, gd:], 8)
    return jnp.concatenate([take(s2[:, :gd]), take(s4[:, :gd]), take(s8[:, :gd]), take(s16)], axis=1)


def _pool_fwd(h, gamma, pw, pb, pscale, layer):
    S = h.shape[0]
    tm = _tile(S, 512)

    def body(h_ref, g_ref, pw_ref, pb_ref, ps_ref, ho_ref, u_ref, tail_s):
        t = pl.program_id(0)

        @pl.when(t == 0)
        def _():
            tail_s[...] = jnp.zeros_like(tail_s)

        hf = h_ref[...]
        _, _, hn = _rms(hf, g_ref[...])
        ext = jnp.concatenate([tail_s[...], hn], axis=0)
        tail_s[...] = hn[tm - POOL_HALO:, :]
        sums = _window_sums(ext, lambda v, j: pltpu.roll(v, j, 0), lambda v: v[POOL_HALO:])
        ub = (sums * _pool_inv_count(t, tm) - hn).astype(BF16)
        u_ref[...] = ub
        ys = [_dot(ub[:, g * POOL_GROUP_DIM:(g + 1) * POOL_GROUP_DIM], pw_ref[g]) for g in range(len(POOL_WINDOWS))]
        y = jnp.concatenate(ys, axis=1)
        ho_ref[...] = hf + (y + pb_ref[...]) * ps_ref[...]

    row = pl.BlockSpec((1, D_MODEL), lambda t: (0, 0))
    return pl.pallas_call(
        body, name="pool_fwd",
        grid=(S // tm,),
        in_specs=[
            pl.BlockSpec((tm, D_MODEL), lambda t: (t, 0)), row,
            pl.BlockSpec((None, len(POOL_WINDOWS), POOL_GROUP_DIM, POOL_GROUP_DIM), lambda t: (layer, 0, 0, 0)),
            row, row,
        ],
        out_specs=[pl.BlockSpec((tm, D_MODEL), lambda t: (t, 0)), pl.BlockSpec((tm, D_MODEL), lambda t: (t, 0))],
        out_shape=[jax.ShapeDtypeStruct((S, D_MODEL), F32), jax.ShapeDtypeStruct((S, D_MODEL), BF16)],
        scratch_shapes=[pltpu.VMEM((POOL_HALO, D_MODEL), F32)],
        compiler_params=_cparams(("arbitrary",)),
    )(h, gamma, pw, pb, pscale)


def _pool_bwd(dout, h, u, gamma, pw, pb, pscale, layer):
    S = h.shape[0]
    tm = _tile(S, 512)
    nt = S // tm
    rev = lambda i: nt - 1 - i
    ngroup = len(POOL_WINDOWS)

    def body(do_ref, h_ref, u_ref, g_ref, pw_ref, pb_ref, ps_ref, dh_ref, dpre_ref, vec_ref, head_s):
        i = pl.program_id(0)

        @pl.when(i == 0)
        def _():
            head_s[...] = jnp.zeros_like(head_s)
            vec_ref[...] = jnp.zeros_like(vec_ref)

        do = do_ref[...]
        ub = u_ref[...]
        gsl = lambda v, g: v[:, g * POOL_GROUP_DIM:(g + 1) * POOL_GROUP_DIM]
        y = jnp.concatenate([_dot(gsl(ub, g), pw_ref[g]) for g in range(ngroup)], axis=1)
        dpre = do * ps_ref[...]
        dpb = dpre.astype(BF16)
        dpre_ref[...] = dpb
        du = jnp.concatenate([_dot_nt(gsl(dpb, g), pw_ref[g]) for g in range(ngroup)], axis=1)
        v = du * _pool_inv_count(rev(i), tm)
        ext = jnp.concatenate([v, head_s[...]], axis=0)
        head_s[...] = v[:POOL_HALO, :]
        n = tm + POOL_HALO
        dhn = _window_sums(ext, lambda w, j: pltpu.roll(w, n - j, 0), lambda w: w[:tm]) - du
        xhat, rstd, _ = _rms(h_ref[...], g_ref[...])
        dh_in, dgam = _rms_bwd(xhat, rstd, g_ref[...], dhn)
        dh_ref[...] = do + dh_in
        vec_ref[0] += dgam
        vec_ref[1] += _rowsum8(dpre)
        vec_ref[2] += _rowsum8(do * (y + pb_ref[...]))

    row = pl.BlockSpec((1, D_MODEL), lambda i: (0, 0))
    tile = pl.BlockSpec((tm, D_MODEL), lambda i: (rev(i), 0))
    return pl.pallas_call(
        body, name="pool_bwd",
        grid=(nt,),
        in_specs=[tile, tile, tile, row,
                  pl.BlockSpec((None, ngroup, POOL_GROUP_DIM, POOL_GROUP_DIM), lambda i: (layer, 0, 0, 0)), row, row],
        out_specs=[tile, tile, pl.BlockSpec((3, SUBLANES, D_MODEL), lambda i: (0, 0, 0))],
        out_shape=[jax.ShapeDtypeStruct((S, D_MODEL), F32), jax.ShapeDtypeStruct((S, D_MODEL), BF16),
                   jax.ShapeDtypeStruct((3, SUBLANES, D_MODEL), F32)],
        scratch_shapes=[pltpu.VMEM((POOL_HALO, D_MODEL), F32)],
        compiler_params=_cparams(("arbitrary",)),
    )(dout, h, u, gamma, pw, pb, pscale)


def _ple_parts(hf, gamma, p_tile, wgate_ref, wproj_ref):
    xhat, rstd, xn = _rms(hf, gamma)
    xnb = xn.astype(BF16)
    gate = _sigmoid(_dot(xnb, wgate_ref[...]))
    pb = p_tile.astype(BF16)
    proj = jnp.concatenate([_dot(pb, wproj_ref[k]) for k in range(N_CHIPS)], axis=1)
    return xhat, rstd, xnb, pb, gate, proj


def _ple_fwd(h, gamma, p_l, wgate, wproj, layer):
    S = h.shape[0]
    tm = _tile(S, 512)

    def body(h_ref, g_ref, p_ref, wgate_ref, wproj_ref, ho_ref):
        hf = h_ref[...]
        _, _, _, _, gate, proj = _ple_parts(hf, g_ref[...], p_ref[...], wgate_ref, wproj_ref)
        ho_ref[...] = hf + gate * proj

    return pl.pallas_call(
        body, name="ple_fwd",
        grid=(S // tm,),
        in_specs=[
            pl.BlockSpec((tm, D_MODEL), lambda t: (t, 0)),
            pl.BlockSpec((1, D_MODEL), lambda t: (0, 0)),
            pl.BlockSpec((tm, PLE_DIM), lambda t: (t, 0)),
            pl.BlockSpec((None, D_MODEL, D_MODEL), lambda t: (layer, 0, 0)),
            pl.BlockSpec((None, N_CHIPS, PLE_DIM, PLE_DIM), lambda t: (layer, 0, 0, 0)),
        ],
        out_specs=pl.BlockSpec((tm, D_MODEL), lambda t: (t, 0)),
        out_shape=jax.ShapeDtypeStruct((S, D_MODEL), F32),
        compiler_params=_cparams(("arbitrary",)),
    )(h, gamma, p_l, wgate, wproj)


def _ple_bwd(dout, h, gamma, p_l, wgate, wproj, layer):
    S = h.shape[0]
    tm = _tile(S, 512)

    def body(do_ref, h_ref, g_ref, p_ref, wgate_ref, wproj_ref, dh_ref, dwg_ref, dwp_ref, dgam_ref):
        @pl.when(pl.program_id(0) == 0)
        def _():
            dgam_ref[...] = jnp.zeros_like(dgam_ref)
            dwg_ref[...] = jnp.zeros_like(dwg_ref)
            dwp_ref[...] = jnp.zeros_like(dwp_ref)

        do = do_ref[...]
        xhat, rstd, xnb, pb, gate, proj = _ple_parts(h_ref[...], g_ref[...], p_ref[...], wgate_ref, wproj_ref)
        dproj = (do * gate).astype(BF16)
        dpre = (do * proj * gate * (1.0 - gate)).astype(BF16)
        dwg_ref[...] += _dot_tn(xnb, dpre)
        for k in range(N_CHIPS):
            dwp_ref[k] += _dot_tn(pb, dproj[:, k * PLE_DIM:(k + 1) * PLE_DIM])
        dhn, dgam = _rms_bwd(xhat, rstd, g_ref[...], _dot_nt(dpre, wgate_ref[...]))
        dh_ref[...] = do + dhn
        dgam_ref[...] += dgam

    tile = pl.BlockSpec((tm, D_MODEL), lambda t: (t, 0))
    return pl.pallas_call(
        body, name="ple_bwd",
        grid=(S // tm,),
        in_specs=[
            tile, tile,
            pl.BlockSpec((1, D_MODEL), lambda t: (0, 0)),
            pl.BlockSpec((tm, PLE_DIM), lambda t: (t, 0)),
            pl.BlockSpec((None, D_MODEL, D_MODEL), lambda t: (layer, 0, 0)),
            pl.BlockSpec((None, N_CHIPS, PLE_DIM, PLE_DIM), lambda t: (layer, 0, 0, 0)),
        ],
        out_specs=[tile, pl.BlockSpec((D_MODEL, D_MODEL), lambda t: (0, 0)),
                   pl.BlockSpec((N_CHIPS, PLE_DIM, PLE_DIM), lambda t: (0, 0, 0)),
                   pl.BlockSpec((SUBLANES, D_MODEL), lambda t: (0, 0))],
        out_shape=[jax.ShapeDtypeStruct((S, D_MODEL), F32), jax.ShapeDtypeStruct((D_MODEL, D_MODEL), F32),
                   jax.ShapeDtypeStruct((N_CHIPS, PLE_DIM, PLE_DIM), F32),
                   jax.ShapeDtypeStruct((SUBLANES, D_MODEL), F32)],
        compiler_params=_cparams(("arbitrary",)),
    )(dout, h, gamma, p_l, wgate, wproj)


def _final(h, gamma, target):
    S = h.shape[0]
    tm = _tile(S, 512)

    def body(h_ref, g_ref, tgt_ref, dh_ref, dgam_ref, loss_ref):
        xhat, rstd, y = _rms(h_ref[...], g_ref[...])
        err = y - tgt_ref[...]
        dy = err * (1.0 / D_MODEL)
        dhn, dgam = _rms_bwd(xhat, rstd, g_ref[...], dy)
        dh_ref[...] = dhn
        sq = _rowsum8(err * err)
        part = sq[:, :LANES]
        for j in range(1, D_MODEL // LANES):
            part = part + sq[:, j * LANES:(j + 1) * LANES]

        @pl.when(pl.program_id(0) == 0)
        def _():
            dgam_ref[...] = jnp.zeros_like(dgam_ref)
            loss_ref[...] = jnp.zeros_like(loss_ref)

        dgam_ref[...] += dgam
        loss_ref[...] += part * (0.5 / D_MODEL)

    tile = pl.BlockSpec((tm, D_MODEL), lambda t: (t, 0))
    return pl.pallas_call(
        body, name="final_loss",
        grid=(S // tm,),
        in_specs=[tile, pl.BlockSpec((1, D_MODEL), lambda t: (0, 0)), tile],
        out_specs=[tile, pl.BlockSpec((SUBLANES, D_MODEL), lambda t: (0, 0)),
                   pl.BlockSpec((SUBLANES, LANES), lambda t: (0, 0))],
        out_shape=[jax.ShapeDtypeStruct((S, D_MODEL), F32), jax.ShapeDtypeStruct((SUBLANES, D_MODEL), F32),
                   jax.ShapeDtypeStruct((SUBLANES, LANES), F32)],
        compiler_params=_cparams(("arbitrary",)),
    )(h, gamma, target)


def _mesh_pos():
    return lax.axis_index("x"), lax.axis_index("y"), lax.axis_index("c")


def _other_chip(x, y, j):
    fx, fy = CHIP_FLIPS[j]
    return (1 - x if fx else x), (1 - y if fy else y)


def _any_specs(n):
    return [pl.BlockSpec(memory_space=pl.ANY)] * n


class _Cargo:
    def __init__(self):
        self.operands, self.out_shapes, self.aliases, self.sem_shapes, self.names = [], [], {}, [], []
        self.start = lambda ins, outs, sems: None
        self.forward = lambda ins, outs, sems: None
        self.finish = lambda ins, outs, sems: None


def _remote(src, dst, send, recv, device):
    return pltpu.make_async_remote_copy(src_ref=src, dst_ref=dst, send_sem=send, recv_sem=recv,
                                        device_id=device, device_id_type=MESH_ID)


def _gather_cargo(bufs, pieces):
    cargo = _Cargo()
    if not pieces:
        return cargo
    plist = []
    for name, layer in pieces:
        if name not in cargo.names:
            cargo.names.append(name)
            cargo.operands.append(bufs[name])
        plist.append((cargo.names.index(name), layer, bufs[name].shape[2] // 2))
    nflip = len(CHIP_FLIPS)
    cargo.out_shapes = [jax.ShapeDtypeStruct(b.shape, b.dtype) for b in cargo.operands]
    cargo.aliases = {i: i for i in range(len(cargo.operands))}
    cargo.sem_shapes = [pltpu.SemaphoreType.DMA((len(plist) * nflip,))] * 4

    def copies(outs, sems):
        send1, recv1, send2, recv2 = sems
        x, y, c = _mesh_pos()
        k = 2 * x + y

        def blk(p, chip, cc):
            b, layer, hrows = plist[p]
            return outs[b].at[layer, chip, pl.ds(cc * hrows, hrows), :]

        def chip_of(j):
            px, py = _other_chip(x, y, j)
            return 2 * px + py

        def ici(p, j):
            px, py = _other_chip(x, y, j)
            return _remote(blk(p, k, c), blk(p, k, c), send1.at[p * nflip + j], recv1.at[p * nflip + j], (px, py, c))

        def landed(p, j):
            px, py = _other_chip(x, y, j)
            return _remote(blk(p, k, c), blk(p, chip_of(j), c), send1.at[p * nflip + j], recv1.at[p * nflip + j],
                           (px, py, c))

        def d2d(p, j, cc):
            return _remote(blk(p, chip_of(j), cc), blk(p, chip_of(j), cc), send2.at[p * nflip + j],
                           recv2.at[p * nflip + j], (x, y, 1 - c))

        return c, ici, landed, d2d

    def start(ins, outs, sems):
        _, ici, _, _ = copies(outs, sems)
        for p in range(len(plist)):
            for j in range(nflip):
                ici(p, j).start()

    def forward(ins, outs, sems):
        c, _, landed, d2d = copies(outs, sems)
        for j in range(nflip):
            for p in range(len(plist)):
                landed(p, j).wait_recv()
                d2d(p, j, c).start()

    def finish(ins, outs, sems):
        c, ici, _, d2d = copies(outs, sems)
        for p in range(len(plist)):
            for j in range(nflip):
                ici(p, j).wait_send()
                d2d(p, j, c).wait_send()
                d2d(p, j, 1 - c).wait_recv()

    cargo.start, cargo.forward, cargo.finish = start, forward, finish
    return cargo


def _reduce_cargo(grads, presums):
    cargo = _Cargo()
    na, nb = len(grads), len(presums)
    nflip = len(CHIP_FLIPS)
    cargo.operands = list(grads) + list(presums)
    cargo.out_shapes = ([jax.ShapeDtypeStruct((g.shape[0], g.shape[1] // 2, g.shape[2]), g.dtype) for g in grads]
                        + [jax.ShapeDtypeStruct((nflip,) + ps.shape[1:], ps.dtype) for ps in presums])
    cargo.sem_shapes = ([pltpu.SemaphoreType.DMA((na,))] * 2 if na else []) + (
        [pltpu.SemaphoreType.DMA((nb * nflip,))] * 2 if nb else [])

    def copies(ins, outs, sems):
        x, y, c = _mesh_pos()
        out = []
        if na:
            send, recv = sems[0], sems[1]
            for a in range(na):
                hrows = grads[a].shape[1] // 2
                out.append(_remote(ins[a].at[:, pl.ds((1 - c) * hrows, hrows), :], outs[a], send.at[a], recv.at[a],
                                   (x, y, 1 - c)))
        if nb:
            send, recv = sems[-2], sems[-1]
            for b in range(nb):
                for j in range(nflip):
                    px, py = _other_chip(x, y, j)
                    out.append(_remote(ins[na + b].at[2 * px + py], outs[na + b].at[j], send.at[b * nflip + j],
                                       recv.at[b * nflip + j], (px, py, c)))
        return out

    def start(ins, outs, sems):
        for cp in copies(ins, outs, sems):
            cp.start()

    def finish(ins, outs, sems):
        for cp in copies(ins, outs, sems):
            cp.wait()

    cargo.start, cargo.finish = start, finish
    return cargo


def _run_cargo(name, cargo):
    nin, nout = len(cargo.operands), len(cargo.out_shapes)

    def body(*refs):
        ins, outs, sems = refs[:nin], refs[nin:nin + nout], refs[nin + nout:]
        cargo.start(ins, outs, sems)
        cargo.forward(ins, outs, sems)
        cargo.finish(ins, outs, sems)

    return list(pl.pallas_call(
        body, name=name,
        in_specs=_any_specs(nin), out_specs=_any_specs(nout), out_shape=cargo.out_shapes,
        input_output_aliases=dict(cargo.aliases), scratch_shapes=list(cargo.sem_shapes),
    )(*cargo.operands))


def _join_siblings(bufs):
    nb = len(bufs)
    items = [(b, layer) for b, buf in enumerate(bufs) for layer in range(buf.shape[0])]

    def body(*refs):
        outs = refs[nb:2 * nb]
        send, recv = refs[2 * nb:]
        x, y, c = _mesh_pos()

        def half(i, cc):
            b, layer = items[i]
            hrows = bufs[b].shape[1] // 2
            blk = outs[b].at[layer, pl.ds(cc * hrows, hrows), :]
            return _remote(blk, blk, send.at[i], recv.at[i], (x, y, 1 - c))

        for i in range(len(items)):
            half(i, c).start()
        for i in range(len(items)):
            half(i, c).wait_send()
            half(i, 1 - c).wait_recv()

    return list(pl.pallas_call(
        body, name="grad_sibling_join",
        in_specs=_any_specs(nb), out_specs=_any_specs(nb),
        out_shape=[jax.ShapeDtypeStruct(b.shape, b.dtype) for b in bufs],
        input_output_aliases={i: i for i in range(nb)},
        scratch_shapes=[pltpu.SemaphoreType.DMA((len(items),))] * 2,
    )(*bufs))


def _cast_place(w3, pos, dtype):
    L, rows, cols = w3.shape

    def body(pos_ref, w_ref, o_ref):
        o_ref[...] = w_ref[...].astype(dtype)

    return pl.pallas_call(
        body, name="cast_place",
        grid_spec=pltpu.PrefetchScalarGridSpec(
            num_scalar_prefetch=1, grid=(L,),
            in_specs=[pl.BlockSpec((None, rows, cols), lambda l, pos: (l, 0, 0))],
            out_specs=pl.BlockSpec((None, None, rows, cols), lambda l, pos: (l, pos[0], 0, 0))),
        out_shape=jax.ShapeDtypeStruct((L, N_CHIPS, rows, cols), dtype),
        compiler_params=_cparams(("arbitrary",)),
    )(pos, w3)


def _allreduce_small(buf):
    R = buf.shape[0]

    def body(in_ref, out_ref, land, send, recv):
        x, y, c = _mesh_pos()
        out_ref[...] = in_ref[...]
        for s, peer in enumerate(((x, y, 1 - c), (x, 1 - y, c), (1 - x, y, c))):
            cp = pltpu.make_async_remote_copy(
                src_ref=out_ref, dst_ref=land.at[s], send_sem=send.at[s], recv_sem=recv.at[s],
                device_id=peer, device_id_type=MESH_ID)
            cp.start()
            cp.wait()
            out_ref[...] = out_ref[...] + land[s]

    return pl.pallas_call(
        body, name="allreduce_small",
        in_specs=[pl.BlockSpec(memory_space=pltpu.VMEM)],
        out_specs=pl.BlockSpec(memory_space=pltpu.VMEM),
        out_shape=jax.ShapeDtypeStruct((R, LANES), F32),
        scratch_shapes=[pltpu.VMEM((3, R, LANES), F32), pltpu.SemaphoreType.DMA((3,)), pltpu.SemaphoreType.DMA((3,))],
        compiler_params=pltpu.CompilerParams(vmem_limit_bytes=VMEM_LIMIT_MB * 2 ** 20),
    )(buf)


def _presum_with_sibling(grad, landed, pos):
    nchunk, rows, cols = grad.shape
    hrows = rows // 2

    def body(pos_ref, g_ref, l_ref, all_ref, own_ref):
        s = g_ref[...] + l_ref[...]
        all_ref[...] = s.astype(BF16)

        @pl.when(pl.program_id(0) == pos_ref[0])
        def _():
            own_ref[...] = s

    return pl.pallas_call(
        body, name="grad_presum",
        grid_spec=pltpu.PrefetchScalarGridSpec(
            num_scalar_prefetch=1, grid=(nchunk,),
            in_specs=[pl.BlockSpec((None, hrows, cols), lambda k, pos: (k, pos[1], 0)),
                      pl.BlockSpec((None, hrows, cols), lambda k, pos: (k, 0, 0))],
            out_specs=[pl.BlockSpec((None, hrows, cols), lambda k, pos: (k, 0, 0)),
                       pl.BlockSpec((hrows, cols), lambda k, pos: (0, 0))]),
        out_shape=[jax.ShapeDtypeStruct((nchunk, hrows, cols), BF16), jax.ShapeDtypeStruct((hrows, cols), F32)],
        compiler_params=_cparams(("arbitrary",)),
    )(pos, grad, landed)


def _sum_chips(own, landed, stacked, layer, shape3, pos):
    hrows, cols = own.shape

    def body(pos_ref, o_ref, l_ref, *rest):
        s = o_ref[...]
        for j in range(len(CHIP_FLIPS)):
            s = s + l_ref[j].astype(F32)
        rest[-1][...] = s

    in_specs = [pl.BlockSpec((hrows, cols), lambda i, pos: (0, 0)),
                pl.BlockSpec((len(CHIP_FLIPS), hrows, cols), lambda i, pos: (0, 0, 0))]
    args = [pos, own, landed]
    aliases = {}
    if stacked is not None:
        in_specs.append(pl.BlockSpec(memory_space=pl.ANY))
        args.append(stacked)
        aliases = {3: 0}
    return pl.pallas_call(
        body, name="grad_sum_chips",
        grid_spec=pltpu.PrefetchScalarGridSpec(
            num_scalar_prefetch=1, grid=(1,), in_specs=in_specs,
            out_specs=pl.BlockSpec((None, hrows, cols), lambda i, pos: (layer, pos[1], 0))),
        out_shape=jax.ShapeDtypeStruct(shape3, F32),
        input_output_aliases=aliases,
        compiler_params=_cparams(("arbitrary",)),
    )(*args)


def _adamw(w, g, m, v):
    R, C = w.shape
    rb = R
    for cand in (512, 352, 320, 256, 128, 64, 32, 16, 8):
        if R % cand == 0:
            rb = cand
            break
    c1 = 1.0 - ADAM_B1 ** ADAM_STEP
    c2 = 1.0 - ADAM_B2 ** ADAM_STEP

    def body(w_ref, g_ref, m_ref, v_ref, go_ref, d_ref, mo_ref, vo_ref):
        gv = g_ref[...]
        go_ref[...] = gv
        m2 = ADAM_B1 * m_ref[...] + (1.0 - ADAM_B1) * gv
        v2 = ADAM_B2 * v_ref[...] + (1.0 - ADAM_B2) * (gv * gv)
        mo_ref[...] = m2
        vo_ref[...] = v2
        d_ref[...] = -ADAM_LR * ((m2 / c1) / (jnp.sqrt(v2 / c2) + ADAM_EPS) + ADAM_WD * w_ref[...])

    spec = pl.BlockSpec((rb, C), lambda i: (i, 0))
    return pl.pallas_call(
        body, name="adamw",
        grid=(R // rb,),
        in_specs=[spec] * 4, out_specs=[spec] * 4,
        out_shape=[jax.ShapeDtypeStruct((R, C), F32)] * 4,
        compiler_params=_cparams(("arbitrary",)),
    )(w, g, m, v)


def _pack(parts, align=SUBLANES * LANES):
    flat = jnp.concatenate([p.reshape(-1).astype(F32) for p in parts])
    pad = (-flat.shape[0]) % align
    return jnp.pad(flat, (0, pad)).reshape(-1, LANES)


def _unpack(buf, shapes):
    flat = buf.reshape(-1)
    out, off = [], 0
    for shp in shapes:
        size = 1
        for d in shp:
            size *= d
        out.append(flat[off:off + size].reshape(shp))
        off += size
    return out


def _block_diag_gates(w_a, w_x):
    nq = D_RNN // GATE_CHUNK
    hpc = LRU_HEADS // nq
    eye = jnp.eye(hpc, dtype=F32)

    def bd(w):
        wq = w.reshape(nq, hpc, LRU_HEAD_DIM, LRU_HEAD_DIM)
        return (wq[:, :, :, None, :] * eye[None, :, None, :, None]).reshape(nq, GATE_CHUNK, GATE_CHUNK)

    return jnp.concatenate([bd(w_a), bd(w_x)], axis=2).astype(BF16)


def _block_diag_extract(dwbd):
    nq = D_RNN // GATE_CHUNK
    hpc = LRU_HEADS // nq
    eye = jnp.eye(hpc, dtype=F32)

    def ex(d):
        d5 = d.reshape(nq, hpc, LRU_HEAD_DIM, hpc, LRU_HEAD_DIM)
        return jnp.sum(d5 * eye[None, :, None, :, None], axis=3).reshape(LRU_HEADS, LRU_HEAD_DIM, LRU_HEAD_DIM)

    return ex(dwbd[:, :, :GATE_CHUNK]), ex(dwbd[:, :, GATE_CHUNK:])


BIG = ("ffn1_w_gate", "ffn1_w_up", "ffn1_w_down", "lru_w_in", "lru_w_out", "pool_w",
       "ffn2_w_gate", "ffn2_w_up", "ffn2_w_down", "ple_w_gate", "ple_w_proj")
TINY_SHARDED = ("lru_conv_w", "pool_b", "pool_scale")
REPLICATED = ("ffn1_norm", "mix_norm", "lru_conv_b", "lru_w_a", "lru_b_a", "lru_w_x", "lru_b_x", "lru_a_param",
              "ffn2_norm", "ple_norm", "final_norm")
WEIGHT_ORDER = ("ffn1_norm", "ffn1_w_gate", "ffn1_w_up", "ffn1_w_down", "mix_norm", "lru_w_in", "lru_conv_w",
                "lru_conv_b", "lru_w_a", "lru_b_a", "lru_w_x", "lru_b_x", "lru_a_param", "lru_w_out", "pool_w",
                "pool_b", "pool_scale", "ffn2_norm", "ffn2_w_gate", "ffn2_w_up", "ffn2_w_down", "ple_norm",
                "ple_w_gate", "ple_w_proj", "final_norm")


TRANSPOSED = ("ffn1_w_gate", "ffn1_w_up", "ffn2_w_gate", "ffn2_w_up")


def _stored(name, a):
    return jnp.swapaxes(a, 1, 2) if name in TRANSPOSED else a


def _as3(a):
    return a.reshape(a.shape[0], -1, a.shape[-1])


def kernel(x, p, ffn1_norm, ffn1_w_gate, ffn1_w_up, ffn1_w_down, mix_norm, lru_w_in, lru_conv_w, lru_conv_b, lru_w_a, lru_b_a, lru_w_x, lru_b_x, lru_a_param, lru_w_out, pool_w, pool_b, pool_scale, ffn2_norm, ffn2_w_gate, ffn2_w_up, ffn2_w_down, ple_norm, ple_w_gate, ple_w_proj, final_norm, loss_target, m_ffn1_norm, m_ffn1_w_gate, m_ffn1_w_up, m_ffn1_w_down, m_mix_norm, m_lru_w_in, m_lru_conv_w, m_lru_conv_b, m_lru_w_a, m_lru_b_a, m_lru_w_x, m_lru_b_x, m_lru_a_param, m_lru_w_out, m_pool_w, m_pool_b, m_pool_scale, m_ffn2_norm, m_ffn2_w_gate, m_ffn2_w_up, m_ffn2_w_down, m_ple_norm, m_ple_w_gate, m_ple_w_proj, m_final_norm, v_ffn1_norm, v_ffn1_w_gate, v_ffn1_w_up, v_ffn1_w_down, v_mix_norm, v_lru_w_in, v_lru_conv_w, v_lru_conv_b, v_lru_w_a, v_lru_b_a, v_lru_w_x, v_lru_b_x, v_lru_a_param, v_lru_w_out, v_pool_w, v_pool_b, v_pool_scale, v_ffn2_norm, v_ffn2_w_gate, v_ffn2_w_up, v_ffn2_w_down, v_ple_norm, v_ple_w_gate, v_ple_w_proj, v_final_norm):
    W = dict(ffn1_norm=ffn1_norm, ffn1_w_gate=ffn1_w_gate, ffn1_w_up=ffn1_w_up, ffn1_w_down=ffn1_w_down,
             mix_norm=mix_norm, lru_w_in=lru_w_in, lru_conv_w=lru_conv_w, lru_conv_b=lru_conv_b, lru_w_a=lru_w_a,
             lru_b_a=lru_b_a, lru_w_x=lru_w_x, lru_b_x=lru_b_x, lru_a_param=lru_a_param, lru_w_out=lru_w_out,
             pool_w=pool_w, pool_b=pool_b, pool_scale=pool_scale, ffn2_norm=ffn2_norm, ffn2_w_gate=ffn2_w_gate,
             ffn2_w_up=ffn2_w_up, ffn2_w_down=ffn2_w_down, ple_norm=ple_norm, ple_w_gate=ple_w_gate,
             ple_w_proj=ple_w_proj, final_norm=final_norm)
    M = dict(ffn1_norm=m_ffn1_norm, ffn1_w_gate=m_ffn1_w_gate, ffn1_w_up=m_ffn1_w_up, ffn1_w_down=m_ffn1_w_down,
             mix_norm=m_mix_norm, lru_w_in=m_lru_w_in, lru_conv_w=m_lru_conv_w, lru_conv_b=m_lru_conv_b,
             lru_w_a=m_lru_w_a, lru_b_a=m_lru_b_a, lru_w_x=m_lru_w_x, lru_b_x=m_lru_b_x, lru_a_param=m_lru_a_param,
             lru_w_out=m_lru_w_out, pool_w=m_pool_w, pool_b=m_pool_b, pool_scale=m_pool_scale, ffn2_norm=m_ffn2_norm,
             ffn2_w_gate=m_ffn2_w_gate, ffn2_w_up=m_ffn2_w_up, ffn2_w_down=m_ffn2_w_down, ple_norm=m_ple_norm,
             ple_w_gate=m_ple_w_gate, ple_w_proj=m_ple_w_proj, final_norm=m_final_norm)
    V = dict(ffn1_norm=v_ffn1_norm, ffn1_w_gate=v_ffn1_w_gate, ffn1_w_up=v_ffn1_w_up, ffn1_w_down=v_ffn1_w_down,
             mix_norm=v_mix_norm, lru_w_in=v_lru_w_in, lru_conv_w=v_lru_conv_w, lru_conv_b=v_lru_conv_b,
             lru_w_a=v_lru_w_a, lru_b_a=v_lru_b_a, lru_w_x=v_lru_w_x, lru_b_x=v_lru_b_x, lru_a_param=v_lru_a_param,
             lru_w_out=v_lru_w_out, pool_w=v_pool_w, pool_b=v_pool_b, pool_scale=v_pool_scale, ffn2_norm=v_ffn2_norm,
             ffn2_w_gate=v_ffn2_w_gate, ffn2_w_up=v_ffn2_w_up, ffn2_w_down=v_ffn2_w_down, ple_norm=v_ple_norm,
             ple_w_gate=v_ple_w_gate, ple_w_proj=v_ple_w_proj, final_norm=v_final_norm)

    S = x.shape[1]
    my_x, my_y, my_c = _mesh_pos()
    my_chip = 2 * my_x + my_y
    pos = jnp.stack([my_chip, my_c]).astype(jnp.int32)
    n_lru, n_pool = lru_w_in.shape[0], pool_w.shape[0]

    tiny_shapes = [W[n].shape for n in TINY_SHARDED]
    tiny_local = _pack([W[n] for n in TINY_SHARDED], align=2 * 16 * LANES)[None]
    Ws, Ms, Vs = ({n: _stored(n, d[n]) for n in BIG} for d in (W, M, V))
    bufs = {n: _cast_place(_as3(Ws[n]), pos, BF16) for n in BIG}
    bufs["tiny"] = _cast_place(tiny_local, pos, F32)

    def gather_now(name, pieces):
        cargo = _gather_cargo(bufs, pieces)
        bufs.update(zip(cargo.names, _run_cargo(name, cargo)))

    def ffn_pieces(which, layer):
        return [("%s_w_gate" % which, layer), ("%s_w_up" % which, layer), ("%s_w_down" % which, layer)]

    def mixer_pieces(layer):
        if layer % 2 == 0:
            return [("lru_w_in", layer // 2), ("lru_w_out", layer // 2)]
        return [("pool_w", layer // 2)]

    gather_now("gather_first", [("tiny", 0)] + ffn_pieces("ffn1", 0))
    tiny_by_chip = [_unpack(bufs["tiny"][0, k], tiny_shapes) for k in range(N_CHIPS)]
    conv_w_full = jnp.concatenate([tiny_by_chip[k][0] for k in range(N_CHIPS)], axis=-1)
    pool_b_full = jnp.concatenate([tiny_by_chip[k][1] for k in range(N_CHIPS)], axis=-1)
    pool_s_full = jnp.concatenate([tiny_by_chip[k][2] for k in range(N_CHIPS)], axis=-1)
    ngroup = len(POOL_WINDOWS)

    def pool_weights():
        pw5 = bufs["pool_w"].reshape(n_pool, N_CHIPS, ngroup, POOL_GROUP_DIM // N_CHIPS, POOL_GROUP_DIM)
        return pw5.transpose(0, 2, 1, 3, 4).reshape(n_pool, ngroup, POOL_GROUP_DIM, POOL_GROUP_DIM)

    lru_out = lambda: bufs["lru_w_out"].reshape(n_lru, D_RNN, D_MODEL)
    ple_gate = lambda: bufs["ple_w_gate"].reshape(DEPTH, D_MODEL, D_MODEL)
    wbd = [_block_diag_gates(lru_w_a[j], lru_w_x[j]) for j in range(n_lru)]
    row = lambda a: a.reshape(1, -1)

    def ffn_forward(which, h, gamma, layer, pieces):
        cargo = _gather_cargo(bufs, pieces)
        outs, updated = _ffn_fwd(h, gamma, bufs[which + "_w_gate"], bufs[which + "_w_up"], bufs[which + "_w_down"],
                                 layer, cargo)
        bufs.update(zip(cargo.names, updated))
        return outs

    h = x.reshape(S, D_MODEL)
    saved = []
    for i in range(DEPTH):
        j = i // 2
        sv = {"h0": h}
        first_mixer = mixer_pieces(0) if i == 0 else []
        h, sv["xn1"], sv["g1"], sv["u1"] = ffn_forward(
            "ffn1", h, row(ffn1_norm[i]), i,
            first_mixer + ffn_pieces("ffn2", i) + [("ple_w_gate", i), ("ple_w_proj", i)])
        sv["h1"] = h
        if i % 2 == 0:
            h, sv["xn_mix"], sv["z"], sv["hs"] = _lru_fwd(
                h, row(mix_norm[i]), bufs["lru_w_in"], j, conv_w_full[j], row(lru_conv_b[j]), wbd[j],
                row(lru_b_a[j]), row(lru_b_x[j]), row(lru_a_param[j]), lru_out())
        else:
            h, sv["u"] = _pool_fwd(h, row(mix_norm[i]), pool_weights(), row(pool_b_full[j]), row(pool_s_full[j]), j)
        sv["h2"] = h
        nxt = ffn_pieces("ffn1", i + 1) + mixer_pieces(i + 1) if i + 1 < DEPTH else []
        h, sv["xn2"], sv["g2"], sv["u2"] = ffn_forward("ffn2", h, row(ffn2_norm[i]), i, nxt)
        sv["h3"] = h
        sv["p"] = p[i, 0]
        h = _ple_fwd(h, row(ple_norm[i]), sv["p"], ple_gate(), bufs["ple_w_proj"], i)
        saved.append(sv)

    dh, dgam_final, loss_part = _final(h, row(final_norm), loss_target.reshape(S, D_MODEL))
    win, wout, wpg, wpp, pw = bufs["lru_w_in"], lru_out(), ple_gate(), bufs["ple_w_proj"], pool_weights()

    norm_grads = {n: [None] * DEPTH for n in ("ffn1_norm", "mix_norm", "ffn2_norm", "ple_norm")}
    lru_vec = [None] * n_lru
    pool_vec = [None] * n_pool
    sum8 = lambda a: jnp.sum(a, axis=-2)

    to_siblings, to_chips = [], []
    stacked = {n: None for n in BIG}

    def take_cargo():
        a_items, b_items = list(to_siblings), list(to_chips)
        del to_siblings[:], to_chips[:]
        return _reduce_cargo([it[2] for it in a_items], [it[2] for it in b_items]), a_items, b_items

    def absorb(a_items, b_items, outs):
        for (n, layer, g), landed in zip(a_items, outs[:len(a_items)]):
            all_chunks, own = _presum_with_sibling(g, landed, pos)
            to_chips.append((n, layer, all_chunks, own))
        for (n, layer, _, own), from_chips in zip(b_items, outs[len(a_items):]):
            stacked[n] = _sum_chips(own, from_chips, stacked[n], layer, _as3(Ws[n]).shape, pos)

    def ffn_backward(which, xn, dout, gg, uu, layer, h_in, gamma):
        cargo, a_items, b_items = take_cargo()
        weights = (bufs[which + "_w_gate"], bufs[which + "_w_up"], bufs[which + "_w_down"])
        (dwg, dwu, dwd, slabs), c_outs = _ffn_bwd(xn, dout, gg, uu, *weights, layer, cargo)
        absorb(a_items, b_items, c_outs)
        dh_in, dgam, dwg, dwu, dwd = _ffn_bwd_last(xn, dout, gg, uu, *weights, layer, slabs, h_in, gamma,
                                                   dwg, dwu, dwd)
        to_siblings.extend([(which + "_w_gate", layer, dwg), (which + "_w_up", layer, dwu),
                            (which + "_w_down", layer, dwd)])
        return dh_in, sum8(dgam)

    for i in reversed(range(DEPTH)):
        j = i // 2
        sv = saved[i]
        dh, dw_pg, dw_pp, dgam = _ple_bwd(dh, sv["h3"], row(ple_norm[i]), sv["p"], wpg, wpp, i)
        norm_grads["ple_norm"][i] = sum8(dgam)
        to_siblings.append(("ple_w_gate", i, dw_pg.reshape(N_CHIPS, D_MODEL // N_CHIPS, D_MODEL)))
        to_siblings.append(("ple_w_proj", i, dw_pp))

        dh, norm_grads["ffn2_norm"][i] = ffn_backward("ffn2", sv["xn2"], dh, sv["g2"], sv["u2"], i, sv["h2"],
                                                      row(ffn2_norm[i]))

        if i % 2 == 0:
            dz, dpre, xc_b, y_b, dcw, vec = _lru_bwd_seq(
                dh, sv["z"], sv["hs"], conv_w_full[j], row(lru_conv_b[j]), wbd[j], row(lru_b_a[j]),
                row(lru_b_x[j]), row(lru_a_param[j]), wout, j)
            to_siblings.append(("lru_w_out", j, _xt_dy("lru_dw_out", y_b, dh, 1, D_RNN, D_MODEL, False, False)
                                .reshape(N_CHIPS, D_RNN // N_CHIPS, D_MODEL)))
            to_siblings.append(("lru_w_in", j, _xt_dy("lru_dw_in", sv["xn_mix"], dz, N_CHIPS, D_MODEL, RNN_IN_CHUNK,
                                                      False, True)))
            dwbd = _xt_dy("lru_dw_gates", xc_b, dpre, D_RNN // GATE_CHUNK, GATE_CHUNK, 2 * GATE_CHUNK, True, True)
            dw_a, dw_x = _block_diag_extract(dwbd)
            vsum = sum8(vec)
            lru_vec[j] = (sum8(dcw), vsum[0], vsum[1], vsum[2], vsum[3], dw_a, dw_x)
            dh, dgam = _lru_bwd_in(dz, sv["h1"], row(mix_norm[i]), dh, win, j)
            norm_grads["mix_norm"][i] = sum8(dgam)
        else:
            dh_new, dpre_b, vec = _pool_bwd(dh, sv["h1"], sv["u"], row(mix_norm[i]), pw, row(pool_b_full[j]),
                                            row(pool_s_full[j]), j)
            dpw = _xt_dy("pool_dw", sv["u"], dpre_b, ngroup, POOL_GROUP_DIM, POOL_GROUP_DIM, True, True)
            dpw = dpw.reshape(ngroup, N_CHIPS, POOL_GROUP_DIM // N_CHIPS, POOL_GROUP_DIM).transpose(1, 0, 2, 3)
            to_siblings.append(("pool_w", j, dpw.reshape(N_CHIPS, POOL_GROUP_DIM, POOL_GROUP_DIM)))
            vsum = sum8(vec)
            norm_grads["mix_norm"][i] = vsum[0]
            pool_vec[j] = (vsum[1], vsum[2])
            dh = dh_new

        dh, norm_grads["ffn1_norm"][i] = ffn_backward("ffn1", sv["xn1"], dh, sv["g1"], sv["u1"], i, sv["h0"],
                                                      row(ffn1_norm[i]))

    grad_x = dh.reshape(1, S, D_MODEL)

    tail = 0
    while to_siblings or to_chips:
        cargo, a_items, b_items = take_cargo()
        absorb(a_items, b_items, _run_cargo("grad_exchange_tail%d" % tail, cargo))
        tail += 1
    big_final = dict(zip(BIG, _join_siblings([stacked[n] for n in BIG])))

    small_parts = [
        jnp.stack(norm_grads["ffn1_norm"]), jnp.stack(norm_grads["mix_norm"]),
        jnp.stack(norm_grads["ffn2_norm"]), jnp.stack(norm_grads["ple_norm"]), sum8(dgam_final),
        jnp.stack([lv[0] for lv in lru_vec]), jnp.stack([lv[1] for lv in lru_vec]),
        jnp.stack([lv[2] for lv in lru_vec]), jnp.stack([lv[3] for lv in lru_vec]),
        jnp.stack([lv[4] for lv in lru_vec]), jnp.stack([lv[5] for lv in lru_vec]),
        jnp.stack([lv[6] for lv in lru_vec]),
        jnp.stack([pv[0] for pv in pool_vec]), jnp.stack([pv[1] for pv in pool_vec]),
        jnp.sum(loss_part).reshape(1),
    ]
    small_names = ("ffn1_norm", "mix_norm", "ffn2_norm", "ple_norm", "final_norm", "lru_conv_w", "lru_conv_b",
                   "lru_b_a", "lru_b_x", "lru_a_param", "lru_w_a", "lru_w_x", "pool_b", "pool_scale", "loss")
    reduced = _unpack(_allreduce_small(_pack(small_parts)), [sp.shape for sp in small_parts])
    small_grad = dict(zip(small_names, reduced))
    loss = small_grad.pop("loss").reshape(())
    for n in TINY_SHARDED:
        width = W[n].shape[-1]
        small_grad[n] = lax.dynamic_slice_in_dim(small_grad[n], my_chip * width, width, axis=-1)

    grads, deltas, new_m, new_v = {}, {}, {}, {}
    for n in BIG:
        shp = Ws[n].shape
        to2 = lambda a: a.reshape(-1, shp[-1])
        g2, d, m2, v2 = _adamw(to2(Ws[n]), to2(big_final[n]), to2(Ms[n]), to2(Vs[n]))
        grads[n], deltas[n], new_m[n], new_v[n] = (_stored(n, a.reshape(shp)) for a in (g2, d, m2, v2))
    small_order = TINY_SHARDED + REPLICATED
    small_shapes = [W[n].shape for n in small_order]
    pack_rows = functools.partial(_pack, align=512 * LANES)
    _, sd, sm, sv_ = _adamw(pack_rows([W[n] for n in small_order]), pack_rows([small_grad[n] for n in small_order]),
                            pack_rows([M[n] for n in small_order]), pack_rows([V[n] for n in small_order]))
    for n, d, m2, v2 in zip(small_order, _unpack(sd, small_shapes), _unpack(sm, small_shapes),
                            _unpack(sv_, small_shapes)):
        grads[n], deltas[n], new_m[n], new_v[n] = small_grad[n].reshape(W[n].shape), d, m2, v2

    return (loss, grad_x, *[grads[n] for n in WEIGHT_ORDER], *[deltas[n] for n in WEIGHT_ORDER],
            *[new_m[n] for n in WEIGHT_ORDER], *[new_v[n] for n in WEIGHT_ORDER])
```

```python
import functools

import jax
import jax.numpy as jnp
from jax import lax
from jax.experimental import pallas as pl
from jax.experimental.pallas import tpu as pltpu

F32 = jnp.float32
BF16 = jnp.bfloat16

D_MODEL = 1024
D_FF = 2816
D_RNN = 1280
DEPTH = 4
N_CHIPS = 4
FF_CHUNK = D_FF // N_CHIPS
RNN_IN_CHUNK = 2 * D_RNN // N_CHIPS
GATE_CHUNK = 640
LRU_HEADS = 16
LRU_HEAD_DIM = 80
CONV_WIDTH = 4
LRU_C = 8.0
POOL_WINDOWS = (2, 4, 8, 16)
POOL_GROUP_DIM = 256
PLE_DIM = 256
RMS_EPS = 1e-6
POOL_HALO = 16
SUBLANES = 8
LANES = 128

ADAM_LR = 0.001
ADAM_B1 = 0.9
ADAM_B2 = 0.999
ADAM_EPS = 1e-08
ADAM_WD = 0.01
ADAM_STEP = 10

VMEM_LIMIT_MB = 56
MESH_ID = pl.DeviceIdType.MESH
CHIP_FLIPS = ((1, 0), (0, 1), (1, 1))


def _cparams(semantics):
    return pltpu.CompilerParams(dimension_semantics=semantics, vmem_limit_bytes=VMEM_LIMIT_MB * 2 ** 20)


def _dot(a, b):
    return lax.dot_general(a, b, (((1,), (0,)), ((), ())), preferred_element_type=F32)


def _dot_nt(a, b):
    return lax.dot_general(a, b, (((1,), (1,)), ((), ())), preferred_element_type=F32)


def _dot_tn(a, b):
    return lax.dot_general(a, b, (((0,), (0,)), ((), ())), preferred_element_type=F32)


def _sigmoid(x):
    return 1.0 / (1.0 + jnp.exp(-x))


def _rms(hf, gamma):
    rstd = lax.rsqrt(jnp.mean(hf * hf, axis=-1, keepdims=True) + RMS_EPS)
    xhat = hf * rstd
    return xhat, rstd, xhat * gamma


def _rms_bwd(xhat, rstd, gamma, dxn):
    dxhat = dxn * gamma
    m = jnp.mean(dxhat * xhat, axis=-1, keepdims=True)
    return rstd * (dxhat - xhat * m), _rowsum8(dxn * xhat)


def _rowsum8(v):
    tm, n = v.shape
    return jnp.sum(v.reshape(tm // SUBLANES, SUBLANES, n), axis=0)


def _gelu(x):
    u = 0.7978845608028654 * (x + 0.044715 * x * x * x)
    return 0.5 * x * (1.0 + jnp.tanh(u))


def _gelu_and_grad(x):
    c = 0.7978845608028654
    u = c * (x + 0.044715 * x * x * x)
    th = jnp.tanh(u)
    g = 0.5 * x * (1.0 + th)
    dg = 0.5 * (1.0 + th) + 0.5 * x * (1.0 - th * th) * c * (1.0 + 3.0 * 0.044715 * x * x)
    return g, dg


def _softplus(z):
    e = jnp.exp(-jnp.abs(z))
    u = 1.0 + e
    log1p = jnp.where(u == 1.0, e, jnp.log(u) * e / jnp.where(u == 1.0, 1.0, u - 1.0))
    return jnp.maximum(z, 0.0) + log1p


def _neg_expm1(x):
    series = -(x + 0.5 * x * x + (1.0 / 6.0) * x * x * x)
    return jnp.where(x > -1e-2, series, 1.0 - jnp.exp(x))


def _shift_down(ext, j, halo):
    return pltpu.roll(ext, j, 0)[halo:]


def _shift_up(ext, j, tm):
    n = ext.shape[0]
    return pltpu.roll(ext, n - j, 0)[:tm]


def _scan_causal(a, b):
    tm, n = a.shape
    head_rows = lax.broadcasted_iota(jnp.int32, (SUBLANES, n), 0)
    s = 1
    while s < min(SUBLANES, tm):
        keep = head_rows >= s
        a_r, b_r = pltpu.roll(a, s, 0), pltpu.roll(b, s, 0)
        a_sh = jnp.concatenate([jnp.where(keep, a_r[:SUBLANES], 1.0), a_r[SUBLANES:]], axis=0)
        b_sh = jnp.concatenate([jnp.where(keep, b_r[:SUBLANES], 0.0), b_r[SUBLANES:]], axis=0)
        b = a * b_sh + b
        a = a * a_sh
        s *= 2
    while s < tm:
        b = jnp.concatenate([b[:s], a[s:] * b[:tm - s] + b[s:]], axis=0)
        a = jnp.concatenate([a[:s], a[s:] * a[:tm - s]], axis=0)
        s *= 2
    return a, b


def _scan_anticausal(c, d):
    tm, n = c.shape
    body = tm - SUBLANES
    tail_rows = lax.broadcasted_iota(jnp.int32, (SUBLANES, n), 0) + body
    s = 1
    while s < min(SUBLANES, tm):
        keep = tail_rows < tm - s
        c_r, d_r = pltpu.roll(c, tm - s, 0), pltpu.roll(d, tm - s, 0)
        c_sh = jnp.concatenate([c_r[:body], jnp.where(keep, c_r[body:], 1.0)], axis=0)
        d_sh = jnp.concatenate([d_r[:body], jnp.where(keep, d_r[body:], 0.0)], axis=0)
        d = d + c * d_sh
        c = c * c_sh
        s *= 2
    while s < tm:
        d = jnp.concatenate([d[:tm - s] + c[:tm - s] * d[s:], d[tm - s:]], axis=0)
        c = jnp.concatenate([c[:tm - s] * c[s:], c[tm - s:]], axis=0)
        s *= 2
    return c, d


def _tile(n, want):
    t = min(n, want)
    assert n % t == 0, (n, t)
    return t


def _ffn_fwd(h, gamma, wg, wu, wd, layer, cargo=None):
    S = h.shape[0]
    tm = _tile(S, 1024)
    nt = S // tm
    cargo = cargo or _Cargo()
    n_in, n_out = 5, 4
    nc_in, nc_out = len(cargo.operands), len(cargo.out_shapes)

    def body(*refs):
        h_ref, g_ref, wg_ref, wu_ref, wd_ref = refs[:n_in]
        c_ins = refs[n_in:n_in + nc_in]
        ho_ref, xn_ref, gg_ref, uu_ref = refs[n_in + nc_in:n_in + nc_in + n_out]
        c_outs = refs[n_in + nc_in + n_out:n_in + nc_in + n_out + nc_out]
        xn_s, acc_s = refs[n_in + nc_in + n_out + nc_out:n_in + nc_in + n_out + nc_out + 2]
        sems = refs[n_in + nc_in + n_out + nc_out + 2:]
        t, k = pl.program_id(0), pl.program_id(1)

        @pl.when((t == 0) & (k == 0))
        def _():
            cargo.start(c_ins, c_outs, sems)

        @pl.when((t == nt - 1) & (k == 0))
        def _():
            cargo.forward(c_ins, c_outs, sems)

        @pl.when(k == 0)
        def _():
            _, _, xn = _rms(h_ref[...], g_ref[...])
            xnb = xn.astype(BF16)
            xn_s[...] = xnb
            xn_ref[...] = xnb
            acc_s[...] = jnp.zeros_like(acc_s)

        xnb = xn_s[...]
        g = _dot_nt(xnb, wg_ref[...])
        u = _dot_nt(xnb, wu_ref[...])
        gg_ref[...] = g.astype(BF16)
        uu_ref[...] = u.astype(BF16)
        hid = (g * _sigmoid(g)) * u
        acc_s[...] += _dot(hid.astype(BF16), wd_ref[...])

        @pl.when(k == N_CHIPS - 1)
        def _():
            ho_ref[...] = h_ref[...] + 0.5 * acc_s[...]

        @pl.when((t == nt - 1) & (k == N_CHIPS - 1))
        def _():
            cargo.finish(c_ins, c_outs, sems)

    outs = pl.pallas_call(
        body, name="ffn_fwd",
        grid=(nt, N_CHIPS),
        in_specs=[
            pl.BlockSpec((tm, D_MODEL), lambda t, k: (t, 0)),
            pl.BlockSpec((1, D_MODEL), lambda t, k: (0, 0)),
            pl.BlockSpec((None, None, FF_CHUNK, D_MODEL), lambda t, k: (layer, k, 0, 0)),
            pl.BlockSpec((None, None, FF_CHUNK, D_MODEL), lambda t, k: (layer, k, 0, 0)),
            pl.BlockSpec((None, None, FF_CHUNK, D_MODEL), lambda t, k: (layer, k, 0, 0)),
        ] + _any_specs(nc_in),
        out_specs=[
            pl.BlockSpec((tm, D_MODEL), lambda t, k: (t, 0)),
            pl.BlockSpec((tm, D_MODEL), lambda t, k: (t, 0)),
            pl.BlockSpec((None, tm, FF_CHUNK), lambda t, k: (k, t, 0)),
            pl.BlockSpec((None, tm, FF_CHUNK), lambda t, k: (k, t, 0)),
        ] + _any_specs(nc_out),
        out_shape=[
            jax.ShapeDtypeStruct((S, D_MODEL), F32),
            jax.ShapeDtypeStruct((S, D_MODEL), BF16),
            jax.ShapeDtypeStruct((N_CHIPS, S, FF_CHUNK), BF16),
            jax.ShapeDtypeStruct((N_CHIPS, S, FF_CHUNK), BF16),
        ] + cargo.out_shapes,
        input_output_aliases={n_in + i: n_out + o for i, o in cargo.aliases.items()},
        scratch_shapes=[pltpu.VMEM((tm, D_MODEL), BF16), pltpu.VMEM((tm, D_MODEL), F32)] + cargo.sem_shapes,
        compiler_params=_cparams(("arbitrary", "arbitrary")),
    )(h, gamma, wg, wu, wd, *cargo.operands)
    return outs[:n_out], list(outs[n_out:])


def _ffn_bwd(xn, dout, gg, uu, wg, wu, wd, layer, cargo=None):
    S = xn.shape[0]
    tm = _tile(S, 512)
    nt = S // tm
    nchunk = N_CHIPS - 1
    cargo = cargo or _Cargo()
    n_in, n_out = 7, 4
    nc_in, nc_out = len(cargo.operands), len(cargo.out_shapes)

    def body(*refs):
        xn_ref, do_ref, gg_ref, uu_ref, wg_ref, wu_ref, wd_ref = refs[:n_in]
        c_ins = refs[n_in:n_in + nc_in]
        dwg_ref, dwu_ref, dwd_ref, slab_ref = refs[n_in + nc_in:n_in + nc_in + n_out]
        c_outs = refs[n_in + nc_in + n_out:n_in + nc_in + n_out + nc_out]
        sems = refs[n_in + nc_in + n_out + nc_out:]
        k, t = pl.program_id(0), pl.program_id(1)

        @pl.when((k == 0) & (t == 0))
        def _():
            cargo.start(c_ins, c_outs, sems)

        @pl.when(t == 0)
        def _():
            dwg_ref[...] = jnp.zeros_like(dwg_ref)
            dwu_ref[...] = jnp.zeros_like(dwu_ref)
            dwd_ref[...] = jnp.zeros_like(dwd_ref)

        xnb = xn_ref[...]
        dob = (0.5 * do_ref[...]).astype(BF16)
        g = gg_ref[...].astype(F32)
        u = uu_ref[...].astype(F32)
        s = _sigmoid(g)
        sil = g * s
        dhid = _dot_nt(dob, wd_ref[...])
        dwd_ref[...] += _dot_tn((sil * u).astype(BF16), dob)
        du = (dhid * sil).astype(BF16)
        dg = (dhid * u * (s * (1.0 + g * (1.0 - s)))).astype(BF16)
        dwg_ref[...] += _dot_tn(dg, xnb)
        dwu_ref[...] += _dot_tn(du, xnb)
        slab_ref[...] = (_dot(dg, wg_ref[...]) + _dot(du, wu_ref[...])).astype(BF16)

        @pl.when((k == nchunk - 1) & (t == nt - 1))
        def _():
            cargo.finish(c_ins, c_outs, sems)

    outs = pl.pallas_call(
        body, name="ffn_bwd",
        grid=(nchunk, nt),
        in_specs=[
            pl.BlockSpec((tm, D_MODEL), lambda k, t: (t, 0)),
            pl.BlockSpec((tm, D_MODEL), lambda k, t: (t, 0)),
            pl.BlockSpec((None, tm, FF_CHUNK), lambda k, t: (k, t, 0)),
            pl.BlockSpec((None, tm, FF_CHUNK), lambda k, t: (k, t, 0)),
            pl.BlockSpec((None, None, FF_CHUNK, D_MODEL), lambda k, t: (layer, k, 0, 0)),
            pl.BlockSpec((None, None, FF_CHUNK, D_MODEL), lambda k, t: (layer, k, 0, 0)),
            pl.BlockSpec((None, None, FF_CHUNK, D_MODEL), lambda k, t: (layer, k, 0, 0)),
        ] + _any_specs(nc_in),
        out_specs=[
            pl.BlockSpec((None, FF_CHUNK, D_MODEL), lambda k, t: (k, 0, 0)),
            pl.BlockSpec((None, FF_CHUNK, D_MODEL), lambda k, t: (k, 0, 0)),
            pl.BlockSpec((None, FF_CHUNK, D_MODEL), lambda k, t: (k, 0, 0)),
            pl.BlockSpec((None, tm, D_MODEL), lambda k, t: (k, t, 0)),
        ] + _any_specs(nc_out),
        out_shape=[
            jax.ShapeDtypeStruct((N_CHIPS, FF_CHUNK, D_MODEL), F32),
            jax.ShapeDtypeStruct((N_CHIPS, FF_CHUNK, D_MODEL), F32),
            jax.ShapeDtypeStruct((N_CHIPS, FF_CHUNK, D_MODEL), F32),
            jax.ShapeDtypeStruct((nchunk, S, D_MODEL), BF16),
        ] + cargo.out_shapes,
        input_output_aliases={n_in + i: n_out + o for i, o in cargo.aliases.items()},
        scratch_shapes=list(cargo.sem_shapes),
        compiler_params=_cparams(("arbitrary", "arbitrary")),
    )(xn, dout, gg, uu, wg, wu, wd, *cargo.operands)
    return outs[:n_out], list(outs[n_out:])


def _ffn_bwd_last(xn, dout, gg, uu, wg, wu, wd, layer, slabs, h, gamma, dwg, dwu, dwd, cargo=None):
    S = xn.shape[0]
    tm = _tile(S, 512)
    nt = S // tm
    k = N_CHIPS - 1
    nprev = slabs.shape[0]
    cargo = cargo or _Cargo()
    n_in, n_out = 13, 5
    nc_in, nc_out = len(cargo.operands), len(cargo.out_shapes)

    def body(*refs):
        xn_ref, do_ref, gg_ref, uu_ref, wg_ref, wu_ref, wd_ref, slab_ref, h_ref, g_ref = refs[:10]
        c_ins = refs[n_in:n_in + nc_in]
        dh_ref, dgam_ref, dwg_ref, dwu_ref, dwd_ref = refs[n_in + nc_in:n_in + nc_in + n_out]
        c_outs = refs[n_in + nc_in + n_out:n_in + nc_in + n_out + nc_out]
        sems = refs[n_in + nc_in + n_out + nc_out:]

        @pl.when(pl.program_id(0) == 0)
        def _():
            cargo.start(c_ins, c_outs, sems)
            dgam_ref[...] = jnp.zeros_like(dgam_ref)
            dwg_ref[...] = jnp.zeros_like(dwg_ref)
            dwu_ref[...] = jnp.zeros_like(dwu_ref)
            dwd_ref[...] = jnp.zeros_like(dwd_ref)

        xnb = xn_ref[...]
        do = do_ref[...]
        dob = (0.5 * do).astype(BF16)
        g = gg_ref[...].astype(F32)
        u = uu_ref[...].astype(F32)
        s = _sigmoid(g)
        sil = g * s
        dhid = _dot_nt(dob, wd_ref[...])
        dwd_ref[...] += _dot_tn((sil * u).astype(BF16), dob)
        du = (dhid * sil).astype(BF16)
        dg = (dhid * u * (s * (1.0 + g * (1.0 - s)))).astype(BF16)
        dwg_ref[...] += _dot_tn(dg, xnb)
        dwu_ref[...] += _dot_tn(du, xnb)
        dxn = _dot(dg, wg_ref[...]) + _dot(du, wu_ref[...])
        for i in range(nprev):
            dxn = dxn + slab_ref[i].astype(F32)
        xhat, rstd, _ = _rms(h_ref[...], g_ref[...])
        dhn, dgam = _rms_bwd(xhat, rstd, g_ref[...], dxn)
        dh_ref[...] = do + dhn
        dgam_ref[...] += dgam

        @pl.when(pl.program_id(0) == nt - 1)
        def _():
            cargo.finish(c_ins, c_outs, sems)

    tile = pl.BlockSpec((tm, D_MODEL), lambda t: (t, 0))
    hidden = pl.BlockSpec((None, tm, FF_CHUNK), lambda t: (k, t, 0))
    w_in = pl.BlockSpec((None, None, FF_CHUNK, D_MODEL), lambda t: (layer, k, 0, 0))
    dw_in = pl.BlockSpec((None, FF_CHUNK, D_MODEL), lambda t: (k, 0, 0))
    outs = pl.pallas_call(
        body, name="ffn_bwd_last",
        grid=(nt,),
        in_specs=[tile, tile, hidden, hidden, w_in, w_in,
                  pl.BlockSpec((None, None, FF_CHUNK, D_MODEL), lambda t: (layer, k, 0, 0)),
                  pl.BlockSpec((nprev, tm, D_MODEL), lambda t: (0, t, 0)), tile,
                  pl.BlockSpec((1, D_MODEL), lambda t: (0, 0))] + _any_specs(3 + nc_in),
        out_specs=[tile, pl.BlockSpec((SUBLANES, D_MODEL), lambda t: (0, 0)), dw_in, dw_in,
                   pl.BlockSpec((None, FF_CHUNK, D_MODEL), lambda t: (k, 0, 0))] + _any_specs(nc_out),
        out_shape=[jax.ShapeDtypeStruct((S, D_MODEL), F32), jax.ShapeDtypeStruct((SUBLANES, D_MODEL), F32),
                   jax.ShapeDtypeStruct(dwg.shape, F32), jax.ShapeDtypeStruct(dwu.shape, F32),
                   jax.ShapeDtypeStruct(dwd.shape, F32)] + cargo.out_shapes,
        input_output_aliases={10: 2, 11: 3, 12: 4, **{n_in + i: n_out + o for i, o in cargo.aliases.items()}},
        scratch_shapes=list(cargo.sem_shapes),
        compiler_params=_cparams(("arbitrary",)),
    )(xn, dout, gg, uu, wg, wu, wd, slabs, h, gamma, dwg, dwu, dwd, *cargo.operands)
    return outs[:n_out], list(outs[n_out:])


def _xt_dy(name, x, dy, nchunk, kb, nb, x_by_chunk, y_by_chunk):
    S = x.shape[0]
    tm = _tile(S, 2048)

    def body(x_ref, dy_ref, o_ref):
        @pl.when(pl.program_id(1) == 0)
        def _():
            o_ref[...] = jnp.zeros_like(o_ref)

        o_ref[...] += _dot_tn(x_ref[...].astype(BF16), dy_ref[...].astype(BF16))

    return pl.pallas_call(
        body, name=name,
        grid=(nchunk, S // tm),
        in_specs=[
            pl.BlockSpec((tm, kb), (lambda c, t: (t, c)) if x_by_chunk else (lambda c, t: (t, 0))),
            pl.BlockSpec((tm, nb), (lambda c, t: (t, c)) if y_by_chunk else (lambda c, t: (t, 0))),
        ],
        out_specs=pl.BlockSpec((None, kb, nb), lambda c, t: (c, 0, 0)),
        out_shape=jax.ShapeDtypeStruct((nchunk, kb, nb), F32),
        compiler_params=_cparams(("arbitrary", "arbitrary")),
    )(x, dy)


def _lru_gates(xc, wbd_ref, ba, bx, apar):
    xcb = xc.astype(BF16)
    r_parts, ig_parts = [], []
    for q in range(D_RNN // GATE_CHUNK):
        lo, hi = q * GATE_CHUNK, (q + 1) * GATE_CHUNK
        pre = _dot(xcb[:, lo:hi], wbd_ref[q])
        r_parts.append(_sigmoid(pre[:, :GATE_CHUNK] + ba[:, lo:hi]))
        ig_parts.append(_sigmoid(pre[:, GATE_CHUNK:] + bx[:, lo:hi]))
    r = jnp.concatenate(r_parts, axis=1)
    ig = jnp.concatenate(ig_parts, axis=1)
    sp = LRU_C * _softplus(-apar)
    log_a = -(r * sp)
    a = jnp.exp(log_a)
    mult = jnp.sqrt(_neg_expm1(2.0 * log_a))
    return r, ig, a, mult, sp


def _conv_causal(xb, tail, cw_ref, cb):
    ext = jnp.concatenate([tail, xb], axis=0)
    xc = cb + cw_ref[CONV_WIDTH - 1:CONV_WIDTH, :] * xb
    for j in range(1, CONV_WIDTH):
        xc = xc + cw_ref[CONV_WIDTH - 1 - j:CONV_WIDTH - j, :] * _shift_down(ext, j, SUBLANES)
    return xc, ext


def _lru_fwd(h, gamma, win, layer, convw, convb, wbd, ba, bx, apar, wout):
    S = h.shape[0]
    tm = _tile(S, 256)

    def body(h_ref, g_ref, win_ref, cw_ref, cb_ref, wbd_ref, ba_ref, bx_ref, ap_ref, wout_ref,
             ho_ref, xn_ref, z_ref, hs_ref, tail_s, carry_s):
        @pl.when(pl.program_id(0) == 0)
        def _():
            tail_s[...] = jnp.zeros_like(tail_s)
            carry_s[...] = jnp.zeros_like(carry_s)

        hf = h_ref[...]
        _, _, xn = _rms(hf, g_ref[...])
        xnb = xn.astype(BF16)
        xn_ref[...] = xnb
        for k in range(N_CHIPS):
            z_ref[:, k * RNN_IN_CHUNK:(k + 1) * RNN_IN_CHUNK] = _dot(xnb, win_ref[k])
        gate = z_ref[:, :D_RNN]
        xb = z_ref[:, D_RNN:]
        xc, _ = _conv_causal(xb, tail_s[...], cw_ref, cb_ref[...])
        tail_s[...] = xb[tm - SUBLANES:, :]
        _, ig, a, mult, _ = _lru_gates(xc, wbd_ref, ba_ref[...], bx_ref[...], ap_ref[...])
        big_a, big_b = _scan_causal(a, mult * (ig * xc))
        hs = big_a * carry_s[SUBLANES - 1:SUBLANES, :] + big_b
        hs_ref[...] = hs
        carry_s[...] = hs[tm - SUBLANES:, :]
        y = hs * _gelu(gate)
        ho_ref[...] = hf + _dot(y.astype(BF16), wout_ref[...])

    row = lambda n: pl.BlockSpec((1, n), lambda t: (0, 0))
    return pl.pallas_call(
        body, name="lru_fwd",
        grid=(S // tm,),
        in_specs=[
            pl.BlockSpec((tm, D_MODEL), lambda t: (t, 0)),
            row(D_MODEL),
            pl.BlockSpec((None, N_CHIPS, D_MODEL, RNN_IN_CHUNK), lambda t: (layer, 0, 0, 0)),
            pl.BlockSpec((CONV_WIDTH, D_RNN), lambda t: (0, 0)),
            row(D_RNN),
            pl.BlockSpec((D_RNN // GATE_CHUNK, GATE_CHUNK, 2 * GATE_CHUNK), lambda t: (0, 0, 0)),
            row(D_RNN), row(D_RNN), row(D_RNN),
            pl.BlockSpec((None, D_RNN, D_MODEL), lambda t: (layer, 0, 0)),
        ],
        out_specs=[
            pl.BlockSpec((tm, D_MODEL), lambda t: (t, 0)),
            pl.BlockSpec((tm, D_MODEL), lambda t: (t, 0)),
            pl.BlockSpec((tm, 2 * D_RNN), lambda t: (t, 0)),
            pl.BlockSpec((tm, D_RNN), lambda t: (t, 0)),
        ],
        out_shape=[
            jax.ShapeDtypeStruct((S, D_MODEL), F32),
            jax.ShapeDtypeStruct((S, D_MODEL), BF16),
            jax.ShapeDtypeStruct((S, 2 * D_RNN), F32),
            jax.ShapeDtypeStruct((S, D_RNN), F32),
        ],
        scratch_shapes=[pltpu.VMEM((SUBLANES, D_RNN), F32), pltpu.VMEM((SUBLANES, D_RNN), F32)],
        compiler_params=_cparams(("arbitrary",)),
    )(h, gamma, win, convw, convb, wbd, ba, bx, apar, wout)


def _lru_bwd_seq(dout, z, hs, convw, convb, wbd, ba, bx, apar, wout, layer):
    S = dout.shape[0]
    tm = _tile(S, 256)
    nt = S // tm
    per8 = tm // SUBLANES
    rev = lambda i: nt - 1 - i
    prev8 = lambda i: jnp.maximum(rev(i) * per8 - 1, 0)

    def body(do_ref, z_ref, hs_ref, ztail_ref, hstail_ref, cw_ref, cb_ref, wbd_ref, ba_ref, bx_ref, ap_ref, wout_ref,
             dz_ref, dpre_ref, xc_ref, y_ref, dcw_ref, vec_ref, a_first_s, g_first_s, dxc_head_s):
        i = pl.program_id(0)
        first_in_time = rev(i) == 0

        @pl.when(i == 0)
        def _():
            a_first_s[...] = jnp.zeros_like(a_first_s)
            g_first_s[...] = jnp.zeros_like(g_first_s)
            dxc_head_s[...] = jnp.zeros_like(dxc_head_s)
            dcw_ref[...] = jnp.zeros_like(dcw_ref)
            vec_ref[...] = jnp.zeros_like(vec_ref)

        gate = z_ref[:, :D_RNN]
        xb = z_ref[:, D_RNN:]
        hist = jnp.where(first_in_time, 0.0, 1.0)
        xc, xext = _conv_causal(xb, ztail_ref[:, D_RNN:] * hist, cw_ref, cb_ref[...])
        r, ig, a, mult, sp = _lru_gates(xc, wbd_ref, ba_ref[...], bx_ref[...], ap_ref[...])
        hs = hs_ref[...]
        gel, dgel = _gelu_and_grad(gate)
        y = hs * gel
        y_ref[...] = y.astype(BF16)
        xc_ref[...] = xc.astype(BF16)

        dy = _dot_nt(do_ref[...].astype(BF16), wout_ref[...])
        dhs = dy * gel
        dgate = dy * hs * dgel

        coef = _shift_up(jnp.concatenate([a, a_first_s[...]], axis=0), 1, tm)
        big_c, big_d = _scan_anticausal(coef, dhs)
        g = big_d + big_c * g_first_s[0:1, :]
        g_first_s[...] = g[:SUBLANES, :]
        a_first_s[...] = a[:SUBLANES, :]

        hs_prev = _shift_down(jnp.concatenate([hstail_ref[...] * hist, hs], axis=0), 1, SUBLANES)
        da = g * hs_prev
        dmult = g * ig * xc
        dig = g * mult * xc
        dxc = g * mult * ig
        dlog_a = da * a - dmult * (a * a) / mult
        dr = -(dlog_a * sp)
        dpre_a = dr * r * (1.0 - r)
        dpre_x = dig * ig * (1.0 - ig)
        d_apar = dlog_a * r * (LRU_C * _sigmoid(-ap_ref[...]))

        for q in range(D_RNN // GATE_CHUNK):
            lo, hi = q * GATE_CHUNK, (q + 1) * GATE_CHUNK
            dpre_q = jnp.concatenate([dpre_a[:, lo:hi], dpre_x[:, lo:hi]], axis=1).astype(BF16)
            dpre_ref[:, 2 * lo:2 * hi] = dpre_q
            dxc_q = _dot_nt(dpre_q, wbd_ref[q])
            if q == 0:
                dxc_parts = [dxc_q]
            else:
                dxc_parts.append(dxc_q)
        dxc = dxc + jnp.concatenate(dxc_parts, axis=1)

        dext = jnp.concatenate([dxc, dxc_head_s[...]], axis=0)
        dxb = cw_ref[CONV_WIDTH - 1:CONV_WIDTH, :] * dxc
        for j in range(1, CONV_WIDTH):
            dxb = dxb + cw_ref[CONV_WIDTH - 1 - j:CONV_WIDTH - j, :] * _shift_up(dext, j, tm)
        dxc_head_s[...] = dxc[:SUBLANES, :]
        dz_ref[:, :D_RNN] = dgate.astype(BF16)
        dz_ref[:, D_RNN:] = dxb.astype(BF16)

        dcw_ref[CONV_WIDTH - 1] += _rowsum8(dxc * xb)
        for j in range(1, CONV_WIDTH):
            dcw_ref[CONV_WIDTH - 1 - j] += _rowsum8(dxc * _shift_down(xext, j, SUBLANES))
        vec_ref[0] += _rowsum8(dxc)
        vec_ref[1] += _rowsum8(dpre_a)
        vec_ref[2] += _rowsum8(dpre_x)
        vec_ref[3] += _rowsum8(d_apar)

    row = lambda n: pl.BlockSpec((1, n), lambda i: (0, 0))
    return pl.pallas_call(
        body, name="lru_bwd_seq",
        grid=(nt,),
        in_specs=[
            pl.BlockSpec((tm, D_MODEL), lambda i: (rev(i), 0)),
            pl.BlockSpec((tm, 2 * D_RNN), lambda i: (rev(i), 0)),
            pl.BlockSpec((tm, D_RNN), lambda i: (rev(i), 0)),
            pl.BlockSpec((SUBLANES, 2 * D_RNN), lambda i: (prev8(i), 0)),
            pl.BlockSpec((SUBLANES, D_RNN), lambda i: (prev8(i), 0)),
            pl.BlockSpec((CONV_WIDTH, D_RNN), lambda i: (0, 0)),
            row(D_RNN),
            pl.BlockSpec((D_RNN // GATE_CHUNK, GATE_CHUNK, 2 * GATE_CHUNK), lambda i: (0, 0, 0)),
            row(D_RNN), row(D_RNN), row(D_RNN),
            pl.BlockSpec((None, D_RNN, D_MODEL), lambda i: (layer, 0, 0)),
        ],
        out_specs=[
            pl.BlockSpec((tm, 2 * D_RNN), lambda i: (rev(i), 0)),
            pl.BlockSpec((tm, 2 * D_RNN), lambda i: (rev(i), 0)),
            pl.BlockSpec((tm, D_RNN), lambda i: (rev(i), 0)),
            pl.BlockSpec((tm, D_RNN), lambda i: (rev(i), 0)),
            pl.BlockSpec((CONV_WIDTH, SUBLANES, D_RNN), lambda i: (0, 0, 0)),
            pl.BlockSpec((4, SUBLANES, D_RNN), lambda i: (0, 0, 0)),
        ],
        out_shape=[
            jax.ShapeDtypeStruct((S, 2 * D_RNN), BF16),
            jax.ShapeDtypeStruct((S, 2 * D_RNN), BF16),
            jax.ShapeDtypeStruct((S, D_RNN), BF16),
            jax.ShapeDtypeStruct((S, D_RNN), BF16),
            jax.ShapeDtypeStruct((CONV_WIDTH, SUBLANES, D_RNN), F32),
            jax.ShapeDtypeStruct((4, SUBLANES, D_RNN), F32),
        ],
        scratch_shapes=[pltpu.VMEM((SUBLANES, D_RNN), F32)] * 3,
        compiler_params=_cparams(("arbitrary",)),
    )(dout, z, hs, z, hs, convw, convb, wbd, ba, bx, apar, wout)


def _lru_bwd_in(dz, h, gamma, dres, win, layer):
    S = h.shape[0]
    tm = _tile(S, 512)

    def body(dz_ref, h_ref, g_ref, dres_ref, win_ref, dh_ref, dgam_ref):
        dxn = _dot_nt(dz_ref[:, :RNN_IN_CHUNK], win_ref[0])
        for k in range(1, N_CHIPS):
            dxn = dxn + _dot_nt(dz_ref[:, k * RNN_IN_CHUNK:(k + 1) * RNN_IN_CHUNK], win_ref[k])
        xhat, rstd, _ = _rms(h_ref[...], g_ref[...])
        dhn, dgam = _rms_bwd(xhat, rstd, g_ref[...], dxn)
        dh_ref[...] = dres_ref[...] + dhn

        @pl.when(pl.program_id(0) == 0)
        def _():
            dgam_ref[...] = jnp.zeros_like(dgam_ref)

        dgam_ref[...] += dgam

    return pl.pallas_call(
        body, name="lru_bwd_in",
        grid=(S // tm,),
        in_specs=[
            pl.BlockSpec((tm, 2 * D_RNN), lambda t: (t, 0)),
            pl.BlockSpec((tm, D_MODEL), lambda t: (t, 0)),
            pl.BlockSpec((1, D_MODEL), lambda t: (0, 0)),
            pl.BlockSpec((tm, D_MODEL), lambda t: (t, 0)),
            pl.BlockSpec((None, N_CHIPS, D_MODEL, RNN_IN_CHUNK), lambda t: (layer, 0, 0, 0)),
        ],
        out_specs=[
            pl.BlockSpec((tm, D_MODEL), lambda t: (t, 0)),
            pl.BlockSpec((SUBLANES, D_MODEL), lambda t: (0, 0)),
        ],
        out_shape=[jax.ShapeDtypeStruct((S, D_MODEL), F32), jax.ShapeDtypeStruct((SUBLANES, D_MODEL), F32)],
        compiler_params=_cparams(("arbitrary",)),
    )(dz, h, gamma, dres, win)


def _pool_inv_count(t_index, tm):
    rows = (lax.broadcasted_iota(jnp.int32, (tm, D_MODEL), 0) + t_index * tm + 1).astype(F32)
    col = lax.broadcasted_iota(jnp.int32, (tm, D_MODEL), 1)
    win = jnp.where(col < POOL_GROUP_DIM, float(POOL_WINDOWS[0]),
                    jnp.where(col < 2 * POOL_GROUP_DIM, float(POOL_WINDOWS[1]),
                              jnp.where(col < 3 * POOL_GROUP_DIM, float(POOL_WINDOWS[2]), float(POOL_WINDOWS[3]))))
    return 1.0 / jnp.minimum(rows, win)


def _window_sums(ext, shift, take):
    gd = POOL_GROUP_DIM
    s2 = ext + shift(ext, 1)
    s4 = s2[:, gd:] + shift(s2[:, gd:], 2)
    s8 = s4[:, gd:] + shift(s4[:, gd:], 4)
    s16 = s8[:, gd:] + shift(s8[:, gd:], 8)
    return jnp.concatenate([take(s2[:, :gd]), take(s4[:, :gd]), take(s8[:, :gd]), take(s16)], axis=1)


def _pool_fwd(h, gamma, pw, pb, pscale, layer):
    S = h.shape[0]
    tm = _tile(S, 512)

    def body(h_ref, g_ref, pw_ref, pb_ref, ps_ref, ho_ref, u_ref, tail_s):
        t = pl.program_id(0)

        @pl.when(t == 0)
        def _():
            tail_s[...] = jnp.zeros_like(tail_s)

        hf = h_ref[...]
        _, _, hn = _rms(hf, g_ref[...])
        ext = jnp.concatenate([tail_s[...], hn], axis=0)
        tail_s[...] = hn[tm - POOL_HALO:, :]
        sums = _window_sums(ext, lambda v, j: pltpu.roll(v, j, 0), lambda v: v[POOL_HALO:])
        ub = (sums * _pool_inv_count(t, tm) - hn).astype(BF16)
        u_ref[...] = ub
        ys = [_dot(ub[:, g * POOL_GROUP_DIM:(g + 1) * POOL_GROUP_DIM], pw_ref[g]) for g in range(len(POOL_WINDOWS))]
        y = jnp.concatenate(ys, axis=1)
        ho_ref[...] = hf + (y + pb_ref[...]) * ps_ref[...]

    row = pl.BlockSpec((1, D_MODEL), lambda t: (0, 0))
    return pl.pallas_call(
        body, name="pool_fwd",
        grid=(S // tm,),
        in_specs=[
            pl.BlockSpec((tm, D_MODEL), lambda t: (t, 0)), row,
            pl.BlockSpec((None, len(POOL_WINDOWS), POOL_GROUP_DIM, POOL_GROUP_DIM), lambda t: (layer, 0, 0, 0)),
            row, row,
        ],
        out_specs=[pl.BlockSpec((tm, D_MODEL), lambda t: (t, 0)), pl.BlockSpec((tm, D_MODEL), lambda t: (t, 0))],
        out_shape=[jax.ShapeDtypeStruct((S, D_MODEL), F32), jax.ShapeDtypeStruct((S, D_MODEL), BF16)],
        scratch_shapes=[pltpu.VMEM((POOL_HALO, D_MODEL), F32)],
        compiler_params=_cparams(("arbitrary",)),
    )(h, gamma, pw, pb, pscale)


def _pool_bwd(dout, h, u, gamma, pw, pb, pscale, layer):
    S = h.shape[0]
    tm = _tile(S, 512)
    nt = S // tm
    rev = lambda i: nt - 1 - i
    ngroup = len(POOL_WINDOWS)

    def body(do_ref, h_ref, u_ref, g_ref, pw_ref, pb_ref, ps_ref, dh_ref, dpre_ref, vec_ref, head_s):
        i = pl.program_id(0)

        @pl.when(i == 0)
        def _():
            head_s[...] = jnp.zeros_like(head_s)
            vec_ref[...] = jnp.zeros_like(vec_ref)

        do = do_ref[...]
        ub = u_ref[...]
        gsl = lambda v, g: v[:, g * POOL_GROUP_DIM:(g + 1) * POOL_GROUP_DIM]
        y = jnp.concatenate([_dot(gsl(ub, g), pw_ref[g]) for g in range(ngroup)], axis=1)
        dpre = do * ps_ref[...]
        dpb = dpre.astype(BF16)
        dpre_ref[...] = dpb
        du = jnp.concatenate([_dot_nt(gsl(dpb, g), pw_ref[g]) for g in range(ngroup)], axis=1)
        v = du * _pool_inv_count(rev(i), tm)
        ext = jnp.concatenate([v, head_s[...]], axis=0)
        head_s[...] = v[:POOL_HALO, :]
        n = tm + POOL_HALO
        dhn = _window_sums(ext, lambda w, j: pltpu.roll(w, n - j, 0), lambda w: w[:tm]) - du
        xhat, rstd, _ = _rms(h_ref[...], g_ref[...])
        dh_in, dgam = _rms_bwd(xhat, rstd, g_ref[...], dhn)
        dh_ref[...] = do + dh_in
        vec_ref[0] += dgam
        vec_ref[1] += _rowsum8(dpre)
        vec_ref[2] += _rowsum8(do * (y + pb_ref[...]))

    row = pl.BlockSpec((1, D_MODEL), lambda i: (0, 0))
    tile = pl.BlockSpec((tm, D_MODEL), lambda i: (rev(i), 0))
    return pl.pallas_call(
        body, name="pool_bwd",
        grid=(nt,),
        in_specs=[tile, tile, tile, row,
                  pl.BlockSpec((None, ngroup, POOL_GROUP_DIM, POOL_GROUP_DIM), lambda i: (layer, 0, 0, 0)), row, row],
        out_specs=[tile, tile, pl.BlockSpec((3, SUBLANES, D_MODEL), lambda i: (0, 0, 0))],
        out_shape=[jax.ShapeDtypeStruct((S, D_MODEL), F32), jax.ShapeDtypeStruct((S, D_MODEL), BF16),
                   jax.ShapeDtypeStruct((3, SUBLANES, D_MODEL), F32)],
        scratch_shapes=[pltpu.VMEM((POOL_HALO, D_MODEL), F32)],
        compiler_params=_cparams(("arbitrary",)),
    )(dout, h, u, gamma, pw, pb, pscale)


def _ple_parts(hf, gamma, p_tile, wgate_ref, wproj_ref):
    xhat, rstd, xn = _rms(hf, gamma)
    xnb = xn.astype(BF16)
    gate = _sigmoid(_dot(xnb, wgate_ref[...]))
    pb = p_tile.astype(BF16)
    proj = jnp.concatenate([_dot(pb, wproj_ref[k]) for k in range(N_CHIPS)], axis=1)
    return xhat, rstd, xnb, pb, gate, proj


def _ple_fwd(h, gamma, p_l, wgate, wproj, layer):
    S = h.shape[0]
    tm = _tile(S, 512)

    def body(h_ref, g_ref, p_ref, wgate_ref, wproj_ref, ho_ref):
        hf = h_ref[...]
        _, _, _, _, gate, proj = _ple_parts(hf, g_ref[...], p_ref[...], wgate_ref, wproj_ref)
        ho_ref[...] = hf + gate * proj

    return pl.pallas_call(
        body, name="ple_fwd",
        grid=(S // tm,),
        in_specs=[
            pl.BlockSpec((tm, D_MODEL), lambda t: (t, 0)),
            pl.BlockSpec((1, D_MODEL), lambda t: (0, 0)),
            pl.BlockSpec((tm, PLE_DIM), lambda t: (t, 0)),
            pl.BlockSpec((None, D_MODEL, D_MODEL), lambda t: (layer, 0, 0)),
            pl.BlockSpec((None, N_CHIPS, PLE_DIM, PLE_DIM), lambda t: (layer, 0, 0, 0)),
        ],
        out_specs=pl.BlockSpec((tm, D_MODEL), lambda t: (t, 0)),
        out_shape=jax.ShapeDtypeStruct((S, D_MODEL), F32),
        compiler_params=_cparams(("arbitrary",)),
    )(h, gamma, p_l, wgate, wproj)


def _ple_bwd(dout, h, gamma, p_l, wgate, wproj, layer):
    S = h.shape[0]
    tm = _tile(S, 512)

    def body(do_ref, h_ref, g_ref, p_ref, wgate_ref, wproj_ref, dh_ref, dwg_ref, dwp_ref, dgam_ref):
        @pl.when(pl.program_id(0) == 0)
        def _():
            dgam_ref[...] = jnp.zeros_like(dgam_ref)
            dwg_ref[...] = jnp.zeros_like(dwg_ref)
            dwp_ref[...] = jnp.zeros_like(dwp_ref)

        do = do_ref[...]
        xhat, rstd, xnb, pb, gate, proj = _ple_parts(h_ref[...], g_ref[...], p_ref[...], wgate_ref, wproj_ref)
        dproj = (do * gate).astype(BF16)
        dpre = (do * proj * gate * (1.0 - gate)).astype(BF16)
        dwg_ref[...] += _dot_tn(xnb, dpre)
        for k in range(N_CHIPS):
            dwp_ref[k] += _dot_tn(pb, dproj[:, k * PLE_DIM:(k + 1) * PLE_DIM])
        dhn, dgam = _rms_bwd(xhat, rstd, g_ref[...], _dot_nt(dpre, wgate_ref[...]))
        dh_ref[...] = do + dhn
        dgam_ref[...] += dgam

    tile = pl.BlockSpec((tm, D_MODEL), lambda t: (t, 0))
    return pl.pallas_call(
        body, name="ple_bwd",
        grid=(S // tm,),
        in_specs=[
            tile, tile,
            pl.BlockSpec((1, D_MODEL), lambda t: (0, 0)),
            pl.BlockSpec((tm, PLE_DIM), lambda t: (t, 0)),
            pl.BlockSpec((None, D_MODEL, D_MODEL), lambda t: (layer, 0, 0)),
            pl.BlockSpec((None, N_CHIPS, PLE_DIM, PLE_DIM), lambda t: (layer, 0, 0, 0)),
        ],
        out_specs=[tile, pl.BlockSpec((D_MODEL, D_MODEL), lambda t: (0, 0)),
                   pl.BlockSpec((N_CHIPS, PLE_DIM, PLE_DIM), lambda t: (0, 0, 0)),
                   pl.BlockSpec((SUBLANES, D_MODEL), lambda t: (0, 0))],
        out_shape=[jax.ShapeDtypeStruct((S, D_MODEL), F32), jax.ShapeDtypeStruct((D_MODEL, D_MODEL), F32),
                   jax.ShapeDtypeStruct((N_CHIPS, PLE_DIM, PLE_DIM), F32),
                   jax.ShapeDtypeStruct((SUBLANES, D_MODEL), F32)],
        compiler_params=_cparams(("arbitrary",)),
    )(dout, h, gamma, p_l, wgate, wproj)


def _final(h, gamma, target):
    S = h.shape[0]
    tm = _tile(S, 512)

    def body(h_ref, g_ref, tgt_ref, dh_ref, dgam_ref, loss_ref):
        xhat, rstd, y = _rms(h_ref[...], g_ref[...])
        err = y - tgt_ref[...]
        dy = err * (1.0 / D_MODEL)
        dhn, dgam = _rms_bwd(xhat, rstd, g_ref[...], dy)
        dh_ref[...] = dhn
        sq = _rowsum8(err * err)
        part = sq[:, :LANES]
        for j in range(1, D_MODEL // LANES):
            part = part + sq[:, j * LANES:(j + 1) * LANES]

        @pl.when(pl.program_id(0) == 0)
        def _():
            dgam_ref[...] = jnp.zeros_like(dgam_ref)
            loss_ref[...] = jnp.zeros_like(loss_ref)

        dgam_ref[...] += dgam
        loss_ref[...] += part * (0.5 / D_MODEL)

    tile = pl.BlockSpec((tm, D_MODEL), lambda t: (t, 0))
    return pl.pallas_call(
        body, name="final_loss",
        grid=(S // tm,),
        in_specs=[tile, pl.BlockSpec((1, D_MODEL), lambda t: (0, 0)), tile],
        out_specs=[tile, pl.BlockSpec((SUBLANES, D_MODEL), lambda t: (0, 0)),
                   pl.BlockSpec((SUBLANES, LANES), lambda t: (0, 0))],
        out_shape=[jax.ShapeDtypeStruct((S, D_MODEL), F32), jax.ShapeDtypeStruct((SUBLANES, D_MODEL), F32),
                   jax.ShapeDtypeStruct((SUBLANES, LANES), F32)],
        compiler_params=_cparams(("arbitrary",)),
    )(h, gamma, target)


def _mesh_pos():
    return lax.axis_index("x"), lax.axis_index("y"), lax.axis_index("c")


def _other_chip(x, y, j):
    fx, fy = CHIP_FLIPS[j]
    return (1 - x if fx else x), (1 - y if fy else y)


def _any_specs(n):
    return [pl.BlockSpec(memory_space=pl.ANY)] * n


class _Cargo:
    def __init__(self):
        self.operands, self.out_shapes, self.aliases, self.sem_shapes, self.names = [], [], {}, [], []
        self.start = lambda ins, outs, sems: None
        self.forward = lambda ins, outs, sems: None
        self.finish = lambda ins, outs, sems: None


def _remote(src, dst, send, recv, device):
    return pltpu.make_async_remote_copy(src_ref=src, dst_ref=dst, send_sem=send, recv_sem=recv,
                                        device_id=device, device_id_type=MESH_ID)


def _gather_cargo(bufs, pieces):
    cargo = _Cargo()
    if not pieces:
        return cargo
    plist = []
    for name, layer in pieces:
        if name not in cargo.names:
            cargo.names.append(name)
            cargo.operands.append(bufs[name])
        plist.append((cargo.names.index(name), layer, bufs[name].shape[2] // 2))
    nflip = len(CHIP_FLIPS)
    cargo.out_shapes = [jax.ShapeDtypeStruct(b.shape, b.dtype) for b in cargo.operands]
    cargo.aliases = {i: i for i in range(len(cargo.operands))}
    cargo.sem_shapes = [pltpu.SemaphoreType.DMA((len(plist) * nflip,))] * 4

    def copies(outs, sems):
        send1, recv1, send2, recv2 = sems
        x, y, c = _mesh_pos()
        k = 2 * x + y

        def blk(p, chip, cc):
            b, layer, hrows = plist[p]
            return outs[b].at[layer, chip, pl.ds(cc * hrows, hrows), :]

        def chip_of(j):
            px, py = _other_chip(x, y, j)
            return 2 * px + py

        def ici(p, j):
            px, py = _other_chip(x, y, j)
            return _remote(blk(p, k, c), blk(p, k, c), send1.at[p * nflip + j], recv1.at[p * nflip + j], (px, py, c))

        def landed(p, j):
            px, py = _other_chip(x, y, j)
            return _remote(blk(p, k, c), blk(p, chip_of(j), c), send1.at[p * nflip + j], recv1.at[p * nflip + j],
                           (px, py, c))

        def d2d(p, j, cc):
            return _remote(blk(p, chip_of(j), cc), blk(p, chip_of(j), cc), send2.at[p * nflip + j],
                           recv2.at[p * nflip + j], (x, y, 1 - c))

        return c, ici, landed, d2d

    def start(ins, outs, sems):
        _, ici, _, _ = copies(outs, sems)
        for p in range(len(plist)):
            for j in range(nflip):
                ici(p, j).start()

    def forward(ins, outs, sems):
        c, _, landed, d2d = copies(outs, sems)
        for j in range(nflip):
            for p in range(len(plist)):
                landed(p, j).wait_recv()
                d2d(p, j, c).start()

    def finish(ins, outs, sems):
        c, ici, _, d2d = copies(outs, sems)
        for p in range(len(plist)):
            for j in range(nflip):
                ici(p, j).wait_send()
                d2d(p, j, c).wait_send()
                d2d(p, j, 1 - c).wait_recv()

    cargo.start, cargo.forward, cargo.finish = start, forward, finish
    return cargo


def _reduce_cargo(grads, presums):
    cargo = _Cargo()
    na, nb = len(grads), len(presums)
    nflip = len(CHIP_FLIPS)
    cargo.operands = list(grads) + list(presums)
    cargo.out_shapes = ([jax.ShapeDtypeStruct((g.shape[0], g.shape[1] // 2, g.shape[2]), g.dtype) for g in grads]
                        + [jax.ShapeDtypeStruct((nflip,) + ps.shape[1:], ps.dtype) for ps in presums])
    cargo.sem_shapes = ([pltpu.SemaphoreType.DMA((na,))] * 2 if na else []) + (
        [pltpu.SemaphoreType.DMA((nb * nflip,))] * 2 if nb else [])

    def copies(ins, outs, sems):
        x, y, c = _mesh_pos()
        out = []
        if na:
            send, recv = sems[0], sems[1]
            for a in range(na):
                hrows = grads[a].shape[1] // 2
                out.append(_remote(ins[a].at[:, pl.ds((1 - c) * hrows, hrows), :], outs[a], send.at[a], recv.at[a],
                                   (x, y, 1 - c)))
        if nb:
            send, recv = sems[-2], sems[-1]
            for b in range(nb):
                for j in range(nflip):
                    px, py = _other_chip(x, y, j)
                    out.append(_remote(ins[na + b].at[2 * px + py], outs[na + b].at[j], send.at[b * nflip + j],
                                       recv.at[b * nflip + j], (px, py, c)))
        return out

    def start(ins, outs, sems):
        for cp in copies(ins, outs, sems):
            cp.start()

    def finish(ins, outs, sems):
        for cp in copies(ins, outs, sems):
            cp.wait()

    cargo.start, cargo.finish = start, finish
    return cargo


def _run_cargo(name, cargo):
    nin, nout = len(cargo.operands), len(cargo.out_shapes)

    def body(*refs):
        ins, outs, sems = refs[:nin], refs[nin:nin + nout], refs[nin + nout:]
        cargo.start(ins, outs, sems)
        cargo.forward(ins, outs, sems)
        cargo.finish(ins, outs, sems)

    return list(pl.pallas_call(
        body, name=name,
        in_specs=_any_specs(nin), out_specs=_any_specs(nout), out_shape=cargo.out_shapes,
        input_output_aliases=dict(cargo.aliases), scratch_shapes=list(cargo.sem_shapes),
    )(*cargo.operands))


def _join_siblings(bufs):
    nb = len(bufs)
    items = [(b, layer) for b, buf in enumerate(bufs) for layer in range(buf.shape[0])]

    def body(*refs):
        outs = refs[nb:2 * nb]
        send, recv = refs[2 * nb:]
        x, y, c = _mesh_pos()

        def half(i, cc):
            b, layer = items[i]
            hrows = bufs[b].shape[1] // 2
            blk = outs[b].at[layer, pl.ds(cc * hrows, hrows), :]
            return _remote(blk, blk, send.at[i], recv.at[i], (x, y, 1 - c))

        for i in range(len(items)):
            half(i, c).start()
        for i in range(len(items)):
            half(i, c).wait_send()
            half(i, 1 - c).wait_recv()

    return list(pl.pallas_call(
        body, name="grad_sibling_join",
        in_specs=_any_specs(nb), out_specs=_any_specs(nb),
        out_shape=[jax.ShapeDtypeStruct(b.shape, b.dtype) for b in bufs],
        input_output_aliases={i: i for i in range(nb)},
        scratch_shapes=[pltpu.SemaphoreType.DMA((len(items),))] * 2,
    )(*bufs))


def _cast_place(w3, pos, dtype):
    L, rows, cols = w3.shape

    def body(pos_ref, w_ref, o_ref):
        o_ref[...] = w_ref[...].astype(dtype)

    return pl.pallas_call(
        body, name="cast_place",
        grid_spec=pltpu.PrefetchScalarGridSpec(
            num_scalar_prefetch=1, grid=(L,),
            in_specs=[pl.BlockSpec((None, rows, cols), lambda l, pos: (l, 0, 0))],
            out_specs=pl.BlockSpec((None, None, rows, cols), lambda l, pos: (l, pos[0], 0, 0))),
        out_shape=jax.ShapeDtypeStruct((L, N_CHIPS, rows, cols), dtype),
        compiler_params=_cparams(("arbitrary",)),
    )(pos, w3)


def _allreduce_small(buf):
    R = buf.shape[0]

    def body(in_ref, out_ref, land, send, recv):
        x, y, c = _mesh_pos()
        out_ref[...] = in_ref[...]
        for s, peer in enumerate(((x, y, 1 - c), (x, 1 - y, c), (1 - x, y, c))):
            cp = pltpu.make_async_remote_copy(
                src_ref=out_ref, dst_ref=land.at[s], send_sem=send.at[s], recv_sem=recv.at[s],
                device_id=peer, device_id_type=MESH_ID)
            cp.start()
            cp.wait()
            out_ref[...] = out_ref[...] + land[s]

    return pl.pallas_call(
        body, name="allreduce_small",
        in_specs=[pl.BlockSpec(memory_space=pltpu.VMEM)],
        out_specs=pl.BlockSpec(memory_space=pltpu.VMEM),
        out_shape=jax.ShapeDtypeStruct((R, LANES), F32),
        scratch_shapes=[pltpu.VMEM((3, R, LANES), F32), pltpu.SemaphoreType.DMA((3,)), pltpu.SemaphoreType.DMA((3,))],
        compiler_params=pltpu.CompilerParams(vmem_limit_bytes=VMEM_LIMIT_MB * 2 ** 20),
    )(buf)


def _presum_with_sibling(grad, landed, pos):
    nchunk, rows, cols = grad.shape
    hrows = rows // 2

    def body(pos_ref, g_ref, l_ref, all_ref, own_ref):
        s = g_ref[...] + l_ref[...]
        all_ref[...] = s.astype(BF16)

        @pl.when(pl.program_id(0) == pos_ref[0])
        def _():
            own_ref[...] = s

    return pl.pallas_call(
        body, name="grad_presum",
        grid_spec=pltpu.PrefetchScalarGridSpec(
            num_scalar_prefetch=1, grid=(nchunk,),
            in_specs=[pl.BlockSpec((None, hrows, cols), lambda k, pos: (k, pos[1], 0)),
                      pl.BlockSpec((None, hrows, cols), lambda k, pos: (k, 0, 0))],
            out_specs=[pl.BlockSpec((None, hrows, cols), lambda k, pos: (k, 0, 0)),
                       pl.BlockSpec((hrows, cols), lambda k, pos: (0, 0))]),
        out_shape=[jax.ShapeDtypeStruct((nchunk, hrows, cols), BF16), jax.ShapeDtypeStruct((hrows, cols), F32)],
        compiler_params=_cparams(("arbitrary",)),
    )(pos, grad, landed)


def _sum_chips(own, landed, stacked, layer, shape3, pos):
    hrows, cols = own.shape

    def body(pos_ref, o_ref, l_ref, *rest):
        s = o_ref[...]
        for j in range(len(CHIP_FLIPS)):
            s = s + l_ref[j].astype(F32)
        rest[-1][...] = s

    in_specs = [pl.BlockSpec((hrows, cols), lambda i, pos: (0, 0)),
                pl.BlockSpec((len(CHIP_FLIPS), hrows, cols), lambda i, pos: (0, 0, 0))]
    args = [pos, own, landed]
    aliases = {}
    if stacked is not None:
        in_specs.append(pl.BlockSpec(memory_space=pl.ANY))
        args.append(stacked)
        aliases = {3: 0}
    return pl.pallas_call(
        body, name="grad_sum_chips",
        grid_spec=pltpu.PrefetchScalarGridSpec(
            num_scalar_prefetch=1, grid=(1,), in_specs=in_specs,
            out_specs=pl.BlockSpec((None, hrows, cols), lambda i, pos: (layer, pos[1], 0))),
        out_shape=jax.ShapeDtypeStruct(shape3, F32),
        input_output_aliases=aliases,
        compiler_params=_cparams(("arbitrary",)),
    )(*args)


def _adamw(w, g, m, v, cargo=None):
    R, C = w.shape
    rb = R
    for cand in (512, 352, 320, 256, 128, 64, 32, 16, 8):
        if R % cand == 0:
            rb = cand
            break
    nsteps = R // rb
    c1 = 1.0 - ADAM_B1 ** ADAM_STEP
    c2 = 1.0 - ADAM_B2 ** ADAM_STEP
    cargo = cargo or _Cargo()
    n_in, n_out = 4, 4
    nc_in, nc_out = len(cargo.operands), len(cargo.out_shapes)

    def body(*refs):
        w_ref, g_ref, m_ref, v_ref = refs[:n_in]
        c_ins = refs[n_in:n_in + nc_in]
        go_ref, d_ref, mo_ref, vo_ref = refs[n_in + nc_in:n_in + nc_in + n_out]
        c_outs = refs[n_in + nc_in + n_out:n_in + nc_in + n_out + nc_out]
        sems = refs[n_in + nc_in + n_out + nc_out:]

        @pl.when(pl.program_id(0) == 0)
        def _():
            cargo.start(c_ins, c_outs, sems)

        gv = g_ref[...]
        go_ref[...] = gv
        m2 = ADAM_B1 * m_ref[...] + (1.0 - ADAM_B1) * gv
        v2 = ADAM_B2 * v_ref[...] + (1.0 - ADAM_B2) * (gv * gv)
        mo_ref[...] = m2
        vo_ref[...] = v2
        d_ref[...] = -ADAM_LR * ((m2 / c1) / (jnp.sqrt(v2 / c2) + ADAM_EPS) + ADAM_WD * w_ref[...])

        @pl.when(pl.program_id(0) == nsteps - 1)
        def _():
            cargo.finish(c_ins, c_outs, sems)

    spec = pl.BlockSpec((rb, C), lambda i: (i, 0))
    outs = pl.pallas_call(
        body, name="adamw",
        grid=(nsteps,),
        in_specs=[spec] * 4 + _any_specs(nc_in), out_specs=[spec] * 4 + _any_specs(nc_out),
        out_shape=[jax.ShapeDtypeStruct((R, C), F32)] * 4 + cargo.out_shapes,
        input_output_aliases={n_in + i: n_out + o for i, o in cargo.aliases.items()},
        scratch_shapes=list(cargo.sem_shapes),
        compiler_params=_cparams(("arbitrary",)),
    )(w, g, m, v, *cargo.operands)
    return outs[:n_out], list(outs[n_out:])


def _pack(parts, align=SUBLANES * LANES):
    flat = jnp.concatenate([p.reshape(-1).astype(F32) for p in parts])
    pad = (-flat.shape[0]) % align
    return jnp.pad(flat, (0, pad)).reshape(-1, LANES)


def _unpack(buf, shapes):
    flat = buf.reshape(-1)
    out, off = [], 0
    for shp in shapes:
        size = 1
        for d in shp:
            size *= d
        out.append(flat[off:off + size].reshape(shp))
        off += size
    return out


def _block_diag_gates(w_a, w_x):
    nq = D_RNN // GATE_CHUNK
    hpc = LRU_HEADS // nq
    eye = jnp.eye(hpc, dtype=F32)

    def bd(w):
        wq = w.reshape(nq, hpc, LRU_HEAD_DIM, LRU_HEAD_DIM)
        return (wq[:, :, :, None, :] * eye[None, :, None, :, None]).reshape(nq, GATE_CHUNK, GATE_CHUNK)

    return jnp.concatenate([bd(w_a), bd(w_x)], axis=2).astype(BF16)


def _block_diag_extract(dwbd):
    nq = D_RNN // GATE_CHUNK
    hpc = LRU_HEADS // nq
    eye = jnp.eye(hpc, dtype=F32)

    def ex(d):
        d5 = d.reshape(nq, hpc, LRU_HEAD_DIM, hpc, LRU_HEAD_DIM)
        return jnp.sum(d5 * eye[None, :, None, :, None], axis=3).reshape(LRU_HEADS, LRU_HEAD_DIM, LRU_HEAD_DIM)

    return ex(dwbd[:, :, :GATE_CHUNK]), ex(dwbd[:, :, GATE_CHUNK:])


BIG = ("ffn1_w_gate", "ffn1_w_up", "ffn1_w_down", "lru_w_in", "lru_w_out", "pool_w",
       "ffn2_w_gate", "ffn2_w_up", "ffn2_w_down", "ple_w_gate", "ple_w_proj")
TINY_SHARDED = ("lru_conv_w", "pool_b", "pool_scale")
REPLICATED = ("ffn1_norm", "mix_norm", "lru_conv_b", "lru_w_a", "lru_b_a", "lru_w_x", "lru_b_x", "lru_a_param",
              "ffn2_norm", "ple_norm", "final_norm")
WEIGHT_ORDER = ("ffn1_norm", "ffn1_w_gate", "ffn1_w_up", "ffn1_w_down", "mix_norm", "lru_w_in", "lru_conv_w",
                "lru_conv_b", "lru_w_a", "lru_b_a", "lru_w_x", "lru_b_x", "lru_a_param", "lru_w_out", "pool_w",
                "pool_b", "pool_scale", "ffn2_norm", "ffn2_w_gate", "ffn2_w_up", "ffn2_w_down", "ple_norm",
                "ple_w_gate", "ple_w_proj", "final_norm")


TRANSPOSED = ("ffn1_w_gate", "ffn1_w_up", "ffn2_w_gate", "ffn2_w_up")


def _stored(name, a):
    return jnp.swapaxes(a, 1, 2) if name in TRANSPOSED else a


def _as3(a):
    return a.reshape(a.shape[0], -1, a.shape[-1])


def kernel(x, p, ffn1_norm, ffn1_w_gate, ffn1_w_up, ffn1_w_down, mix_norm, lru_w_in, lru_conv_w, lru_conv_b, lru_w_a, lru_b_a, lru_w_x, lru_b_x, lru_a_param, lru_w_out, pool_w, pool_b, pool_scale, ffn2_norm, ffn2_w_gate, ffn2_w_up, ffn2_w_down, ple_norm, ple_w_gate, ple_w_proj, final_norm, loss_target, m_ffn1_norm, m_ffn1_w_gate, m_ffn1_w_up, m_ffn1_w_down, m_mix_norm, m_lru_w_in, m_lru_conv_w, m_lru_conv_b, m_lru_w_a, m_lru_b_a, m_lru_w_x, m_lru_b_x, m_lru_a_param, m_lru_w_out, m_pool_w, m_pool_b, m_pool_scale, m_ffn2_norm, m_ffn2_w_gate, m_ffn2_w_up, m_ffn2_w_down, m_ple_norm, m_ple_w_gate, m_ple_w_proj, m_final_norm, v_ffn1_norm, v_ffn1_w_gate, v_ffn1_w_up, v_ffn1_w_down, v_mix_norm, v_lru_w_in, v_lru_conv_w, v_lru_conv_b, v_lru_w_a, v_lru_b_a, v_lru_w_x, v_lru_b_x, v_lru_a_param, v_lru_w_out, v_pool_w, v_pool_b, v_pool_scale, v_ffn2_norm, v_ffn2_w_gate, v_ffn2_w_up, v_ffn2_w_down, v_ple_norm, v_ple_w_gate, v_ple_w_proj, v_final_norm):
    W = dict(ffn1_norm=ffn1_norm, ffn1_w_gate=ffn1_w_gate, ffn1_w_up=ffn1_w_up, ffn1_w_down=ffn1_w_down,
             mix_norm=mix_norm, lru_w_in=lru_w_in, lru_conv_w=lru_conv_w, lru_conv_b=lru_conv_b, lru_w_a=lru_w_a,
             lru_b_a=lru_b_a, lru_w_x=lru_w_x, lru_b_x=lru_b_x, lru_a_param=lru_a_param, lru_w_out=lru_w_out,
             pool_w=pool_w, pool_b=pool_b, pool_scale=pool_scale, ffn2_norm=ffn2_norm, ffn2_w_gate=ffn2_w_gate,
             ffn2_w_up=ffn2_w_up, ffn2_w_down=ffn2_w_down, ple_norm=ple_norm, ple_w_gate=ple_w_gate,
             ple_w_proj=ple_w_proj, final_norm=final_norm)
    M = dict(ffn1_norm=m_ffn1_norm, ffn1_w_gate=m_ffn1_w_gate, ffn1_w_up=m_ffn1_w_up, ffn1_w_down=m_ffn1_w_down,
             mix_norm=m_mix_norm, lru_w_in=m_lru_w_in, lru_conv_w=m_lru_conv_w, lru_conv_b=m_lru_conv_b,
             lru_w_a=m_lru_w_a, lru_b_a=m_lru_b_a, lru_w_x=m_lru_w_x, lru_b_x=m_lru_b_x, lru_a_param=m_lru_a_param,
             lru_w_out=m_lru_w_out, pool_w=m_pool_w, pool_b=m_pool_b, pool_scale=m_pool_scale, ffn2_norm=m_ffn2_norm,
             ffn2_w_gate=m_ffn2_w_gate, ffn2_w_up=m_ffn2_w_up, ffn2_w_down=m_ffn2_w_down, ple_norm=m_ple_norm,
             ple_w_gate=m_ple_w_gate, ple_w_proj=m_ple_w_proj, final_norm=m_final_norm)
    V = dict(ffn1_norm=v_ffn1_norm, ffn1_w_gate=v_ffn1_w_gate, ffn1_w_up=v_ffn1_w_up, ffn1_w_down=v_ffn1_w_down,
             mix_norm=v_mix_norm, lru_w_in=v_lru_w_in, lru_conv_w=v_lru_conv_w, lru_conv_b=v_lru_conv_b,
             lru_w_a=v_lru_w_a, lru_b_a=v_lru_b_a, lru_w_x=v_lru_w_x, lru_b_x=v_lru_b_x, lru_a_param=v_lru_a_param,
             lru_w_out=v_lru_w_out, pool_w=v_pool_w, pool_b=v_pool_b, pool_scale=v_pool_scale, ffn2_norm=v_ffn2_norm,
             ffn2_w_gate=v_ffn2_w_gate, ffn2_w_up=v_ffn2_w_up, ffn2_w_down=v_ffn2_w_down, ple_norm=v_ple_norm,
             ple_w_gate=v_ple_w_gate, ple_w_proj=v_ple_w_proj, final_norm=v_final_norm)

    S = x.shape[1]
    my_x, my_y, my_c = _mesh_pos()
    my_chip = 2 * my_x + my_y
    pos = jnp.stack([my_chip, my_c]).astype(jnp.int32)
    n_lru, n_pool = lru_w_in.shape[0], pool_w.shape[0]

    tiny_shapes = [W[n].shape for n in TINY_SHARDED]
    tiny_local = _pack([W[n] for n in TINY_SHARDED], align=2 * 16 * LANES)[None]
    Ws, Ms, Vs = ({n: _stored(n, d[n]) for n in BIG} for d in (W, M, V))
    bufs = {n: _cast_place(_as3(Ws[n]), pos, BF16) for n in BIG}
    bufs["tiny"] = _cast_place(tiny_local, pos, F32)

    def gather_now(name, pieces):
        cargo = _gather_cargo(bufs, pieces)
        bufs.update(zip(cargo.names, _run_cargo(name, cargo)))

    def ffn_pieces(which, layer):
        return [("%s_w_gate" % which, layer), ("%s_w_up" % which, layer), ("%s_w_down" % which, layer)]

    def mixer_pieces(layer):
        if layer % 2 == 0:
            return [("lru_w_in", layer // 2), ("lru_w_out", layer // 2)]
        return [("pool_w", layer // 2)]

    gather_now("gather_first", [("tiny", 0)] + ffn_pieces("ffn1", 0))
    tiny_by_chip = [_unpack(bufs["tiny"][0, k], tiny_shapes) for k in range(N_CHIPS)]
    conv_w_full = jnp.concatenate([tiny_by_chip[k][0] for k in range(N_CHIPS)], axis=-1)
    pool_b_full = jnp.concatenate([tiny_by_chip[k][1] for k in range(N_CHIPS)], axis=-1)
    pool_s_full = jnp.concatenate([tiny_by_chip[k][2] for k in range(N_CHIPS)], axis=-1)
    ngroup = len(POOL_WINDOWS)

    def pool_weights():
        pw5 = bufs["pool_w"].reshape(n_pool, N_CHIPS, ngroup, POOL_GROUP_DIM // N_CHIPS, POOL_GROUP_DIM)
        return pw5.transpose(0, 2, 1, 3, 4).reshape(n_pool, ngroup, POOL_GROUP_DIM, POOL_GROUP_DIM)

    lru_out = lambda: bufs["lru_w_out"].reshape(n_lru, D_RNN, D_MODEL)
    ple_gate = lambda: bufs["ple_w_gate"].reshape(DEPTH, D_MODEL, D_MODEL)
    wbd = [_block_diag_gates(lru_w_a[j], lru_w_x[j]) for j in range(n_lru)]
    row = lambda a: a.reshape(1, -1)

    def ffn_forward(which, h, gamma, layer, pieces):
        cargo = _gather_cargo(bufs, pieces)
        outs, updated = _ffn_fwd(h, gamma, bufs[which + "_w_gate"], bufs[which + "_w_up"], bufs[which + "_w_down"],
                                 layer, cargo)
        bufs.update(zip(cargo.names, updated))
        return outs

    h = x.reshape(S, D_MODEL)
    saved = []
    for i in range(DEPTH):
        j = i // 2
        sv = {"h0": h}
        first_mixer = mixer_pieces(0) if i == 0 else []
        h, sv["xn1"], sv["g1"], sv["u1"] = ffn_forward(
            "ffn1", h, row(ffn1_norm[i]), i,
            first_mixer + ffn_pieces("ffn2", i) + [("ple_w_gate", i), ("ple_w_proj", i)])
        sv["h1"] = h
        if i % 2 == 0:
            h, sv["xn_mix"], sv["z"], sv["hs"] = _lru_fwd(
                h, row(mix_norm[i]), bufs["lru_w_in"], j, conv_w_full[j], row(lru_conv_b[j]), wbd[j],
                row(lru_b_a[j]), row(lru_b_x[j]), row(lru_a_param[j]), lru_out())
        else:
            h, sv["u"] = _pool_fwd(h, row(mix_norm[i]), pool_weights(), row(pool_b_full[j]), row(pool_s_full[j]), j)
        sv["h2"] = h
        nxt = ffn_pieces("ffn1", i + 1) + mixer_pieces(i + 1) if i + 1 < DEPTH else []
        h, sv["xn2"], sv["g2"], sv["u2"] = ffn_forward("ffn2", h, row(ffn2_norm[i]), i, nxt)
        sv["h3"] = h
        sv["p"] = p[i, 0]
        h = _ple_fwd(h, row(ple_norm[i]), sv["p"], ple_gate(), bufs["ple_w_proj"], i)
        saved.append(sv)

    dh, dgam_final, loss_part = _final(h, row(final_norm), loss_target.reshape(S, D_MODEL))
    win, wout, wpg, wpp, pw = bufs["lru_w_in"], lru_out(), ple_gate(), bufs["ple_w_proj"], pool_weights()

    norm_grads = {n: [None] * DEPTH for n in ("ffn1_norm", "mix_norm", "ffn2_norm", "ple_norm")}
    lru_vec = [None] * n_lru
    pool_vec = [None] * n_pool
    sum8 = lambda a: jnp.sum(a, axis=-2)

    to_siblings, to_chips = [], []
    stacked = {n: None for n in BIG}

    def take_cargo(max_chip_items=None):
        nb = len(to_chips) if max_chip_items is None else min(max_chip_items, len(to_chips))
        a_items, b_items = list(to_siblings), to_chips[:nb]
        del to_siblings[:], to_chips[:nb]
        return _reduce_cargo([it[2] for it in a_items], [it[2] for it in b_items]), a_items, b_items

    def absorb(a_items, b_items, outs):
        for (n, layer, g), landed in zip(a_items, outs[:len(a_items)]):
            all_chunks, own = _presum_with_sibling(g, landed, pos)
            to_chips.append((n, layer, all_chunks, own))
        for (n, layer, _, own), from_chips in zip(b_items, outs[len(a_items):]):
            stacked[n] = _sum_chips(own, from_chips, stacked[n], layer, _as3(Ws[n]).shape, pos)

    def ffn_backward(which, xn, dout, gg, uu, layer, h_in, gamma):
        cargo, a_items, b_items = take_cargo()
        weights = (bufs[which + "_w_gate"], bufs[which + "_w_up"], bufs[which + "_w_down"])
        (dwg, dwu, dwd, slabs), c_outs = _ffn_bwd(xn, dout, gg, uu, *weights, layer, cargo)
        absorb(a_items, b_items, c_outs)
        cargo, a_items, b_items = take_cargo()
        (dh_in, dgam, dwg, dwu, dwd), c_outs = _ffn_bwd_last(xn, dout, gg, uu, *weights, layer, slabs, h_in, gamma,
                                                             dwg, dwu, dwd, cargo)
        absorb(a_items, b_items, c_outs)
        to_siblings.extend([(which + "_w_gate", layer, dwg), (which + "_w_up", layer, dwu),
                            (which + "_w_down", layer, dwd)])
        return dh_in, sum8(dgam)

    for i in reversed(range(DEPTH)):
        j = i // 2
        sv = saved[i]
        dh, dw_pg, dw_pp, dgam = _ple_bwd(dh, sv["h3"], row(ple_norm[i]), sv["p"], wpg, wpp, i)
        norm_grads["ple_norm"][i] = sum8(dgam)
        to_siblings.append(("ple_w_gate", i, dw_pg.reshape(N_CHIPS, D_MODEL // N_CHIPS, D_MODEL)))
        to_siblings.append(("ple_w_proj", i, dw_pp))

        dh, norm_grads["ffn2_norm"][i] = ffn_backward("ffn2", sv["xn2"], dh, sv["g2"], sv["u2"], i, sv["h2"],
                                                      row(ffn2_norm[i]))

        if i % 2 == 0:
            dz, dpre, xc_b, y_b, dcw, vec = _lru_bwd_seq(
                dh, sv["z"], sv["hs"], conv_w_full[j], row(lru_conv_b[j]), wbd[j], row(lru_b_a[j]),
                row(lru_b_x[j]), row(lru_a_param[j]), wout, j)
            to_siblings.append(("lru_w_out", j, _xt_dy("lru_dw_out", y_b, dh, 1, D_RNN, D_MODEL, False, False)
                                .reshape(N_CHIPS, D_RNN // N_CHIPS, D_MODEL)))
            to_siblings.append(("lru_w_in", j, _xt_dy("lru_dw_in", sv["xn_mix"], dz, N_CHIPS, D_MODEL, RNN_IN_CHUNK,
                                                      False, True)))
            dwbd = _xt_dy("lru_dw_gates", xc_b, dpre, D_RNN // GATE_CHUNK, GATE_CHUNK, 2 * GATE_CHUNK, True, True)
            dw_a, dw_x = _block_diag_extract(dwbd)
            vsum = sum8(vec)
            lru_vec[j] = (sum8(dcw), vsum[0], vsum[1], vsum[2], vsum[3], dw_a, dw_x)
            dh, dgam = _lru_bwd_in(dz, sv["h1"], row(mix_norm[i]), dh, win, j)
            norm_grads["mix_norm"][i] = sum8(dgam)
        else:
            dh_new, dpre_b, vec = _pool_bwd(dh, sv["h1"], sv["u"], row(mix_norm[i]), pw, row(pool_b_full[j]),
                                            row(pool_s_full[j]), j)
            dpw = _xt_dy("pool_dw", sv["u"], dpre_b, ngroup, POOL_GROUP_DIM, POOL_GROUP_DIM, True, True)
            dpw = dpw.reshape(ngroup, N_CHIPS, POOL_GROUP_DIM // N_CHIPS, POOL_GROUP_DIM).transpose(1, 0, 2, 3)
            to_siblings.append(("pool_w", j, dpw.reshape(N_CHIPS, POOL_GROUP_DIM, POOL_GROUP_DIM)))
            vsum = sum8(vec)
            norm_grads["mix_norm"][i] = vsum[0]
            pool_vec[j] = (vsum[1], vsum[2])
            dh = dh_new

        dh, norm_grads["ffn1_norm"][i] = ffn_backward("ffn1", sv["xn1"], dh, sv["g1"], sv["u1"], i, sv["h0"],
                                                      row(ffn1_norm[i]))

    grad_x = dh.reshape(1, S, D_MODEL)

    small_parts = [
        jnp.stack(norm_grads["ffn1_norm"]), jnp.stack(norm_grads["mix_norm"]),
        jnp.stack(norm_grads["ffn2_norm"]), jnp.stack(norm_grads["ple_norm"]), sum8(dgam_final),
        jnp.stack([lv[0] for lv in lru_vec]), jnp.stack([lv[1] for lv in lru_vec]),
        jnp.stack([lv[2] for lv in lru_vec]), jnp.stack([lv[3] for lv in lru_vec]),
        jnp.stack([lv[4] for lv in lru_vec]), jnp.stack([lv[5] for lv in lru_vec]),
        jnp.stack([lv[6] for lv in lru_vec]),
        jnp.stack([pv[0] for pv in pool_vec]), jnp.stack([pv[1] for pv in pool_vec]),
        jnp.sum(loss_part).reshape(1),
    ]
    small_names = ("ffn1_norm", "mix_norm", "ffn2_norm", "ple_norm", "final_norm", "lru_conv_w", "lru_conv_b",
                   "lru_b_a", "lru_b_x", "lru_a_param", "lru_w_a", "lru_w_x", "pool_b", "pool_scale", "loss")
    reduced = _unpack(_allreduce_small(_pack(small_parts)), [sp.shape for sp in small_parts])
    small_grad = dict(zip(small_names, reduced))
    loss = small_grad.pop("loss").reshape(())
    for n in TINY_SHARDED:
        width = W[n].shape[-1]
        small_grad[n] = lax.dynamic_slice_in_dim(small_grad[n], my_chip * width, width, axis=-1)

    grads, deltas, new_m, new_v = {}, {}, {}, {}

    def adam_big(n, grad3, cargo=None):
        shp = Ws[n].shape
        to2 = lambda a: a.reshape(-1, shp[-1])
        (g2, d, m2, v2), c_outs = _adamw(to2(Ws[n]), to2(grad3), to2(Ms[n]), to2(Vs[n]), cargo)
        grads[n], deltas[n], new_m[n], new_v[n] = (_stored(n, a.reshape(shp)) for a in (g2, d, m2, v2))
        return c_outs

    late = [n for n in BIG if n in {it[0] for it in to_siblings + to_chips}]
    ready = sorted((n for n in BIG if n not in late), key=lambda n: -W[n].size)
    for n, joined in zip(ready, _join_siblings([stacked[n] for n in ready])):
        cargo, a_items, b_items = take_cargo(max_chip_items=1)
        absorb(a_items, b_items, adam_big(n, joined, cargo))
    tail = 0
    while to_siblings or to_chips:
        cargo, a_items, b_items = take_cargo()
        absorb(a_items, b_items, _run_cargo("grad_exchange_tail%d" % tail, cargo))
        tail += 1
    if late:
        for n, joined in zip(late, _join_siblings([stacked[n] for n in late])):
            adam_big(n, joined)
    small_order = TINY_SHARDED + REPLICATED
    small_shapes = [W[n].shape for n in small_order]
    pack_rows = functools.partial(_pack, align=512 * LANES)
    (_, sd, sm, sv_), _ = _adamw(pack_rows([W[n] for n in small_order]),
                                 pack_rows([small_grad[n] for n in small_order]),
                                 pack_rows([M[n] for n in small_order]), pack_rows([V[n] for n in small_order]))
    for n, d, m2, v2 in zip(small_order, _unpack(sd, small_shapes), _unpack(sm, small_shapes),
                            _unpack(sv_, small_shapes)):
        grads[n], deltas[n], new_m[n], new_v[n] = small_grad[n].reshape(W[n].shape), d, m2, v2

    return (loss, grad_x, *[grads[n] for n in WEIGHT_ORDER], *[deltas[n] for n in WEIGHT_ORDER],
            *[new_m[n] for n in WEIGHT_ORDER], *[new_v[n] for n in WEIGHT_ORDER])
```

```python
import functools

import jax
import jax.numpy as jnp
from jax import lax
from jax.experimental import pallas as pl
from jax.experimental.pallas import tpu as pltpu

F32 = jnp.float32
BF16 = jnp.bfloat16

D_MODEL = 1024
D_FF = 2816
D_RNN = 1280
DEPTH = 4
N_CHIPS = 4
FF_CHUNK = D_FF // N_CHIPS
RNN_IN_CHUNK = 2 * D_RNN // N_CHIPS
GATE_CHUNK = 640
N_GATE_PLANES = 5
LRU_HEADS = 16
LRU_HEAD_DIM = 80
CONV_WIDTH = 4
LRU_C = 8.0
POOL_WINDOWS = (2, 4, 8, 16)
POOL_GROUP_DIM = 256
PLE_DIM = 256
RMS_EPS = 1e-6
POOL_HALO = 16
SUBLANES = 8
LANES = 128

ADAM_LR = 0.001
ADAM_B1 = 0.9
ADAM_B2 = 0.999
ADAM_EPS = 1e-08
ADAM_WD = 0.01
ADAM_STEP = 10

VMEM_LIMIT_MB = 56
MESH_ID = pl.DeviceIdType.MESH
CHIP_FLIPS = ((1, 0), (0, 1), (1, 1))


def _cparams(semantics):
    return pltpu.CompilerParams(dimension_semantics=semantics, vmem_limit_bytes=VMEM_LIMIT_MB * 2 ** 20)


def _dot(a, b):
    return lax.dot_general(a, b, (((1,), (0,)), ((), ())), preferred_element_type=F32)


def _dot_nt(a, b):
    return lax.dot_general(a, b, (((1,), (1,)), ((), ())), preferred_element_type=F32)


def _dot_tn(a, b):
    return lax.dot_general(a, b, (((0,), (0,)), ((), ())), preferred_element_type=F32)


def _sigmoid(x):
    return 1.0 / (1.0 + jnp.exp(-x))


def _rms(hf, gamma):
    rstd = lax.rsqrt(jnp.mean(hf * hf, axis=-1, keepdims=True) + RMS_EPS)
    xhat = hf * rstd
    return xhat, rstd, xhat * gamma


def _rms_bwd(xhat, rstd, gamma, dxn):
    dxhat = dxn * gamma
    m = jnp.mean(dxhat * xhat, axis=-1, keepdims=True)
    return rstd * (dxhat - xhat * m), _rowsum8(dxn * xhat)


def _rowsum8(v):
    tm, n = v.shape
    return jnp.sum(v.reshape(tm // SUBLANES, SUBLANES, n), axis=0)


def _gelu(x):
    u = 0.7978845608028654 * (x + 0.044715 * x * x * x)
    return 0.5 * x * (1.0 + jnp.tanh(u))


def _gelu_and_grad(x):
    c = 0.7978845608028654
    u = c * (x + 0.044715 * x * x * x)
    th = jnp.tanh(u)
    g = 0.5 * x * (1.0 + th)
    dg = 0.5 * (1.0 + th) + 0.5 * x * (1.0 - th * th) * c * (1.0 + 3.0 * 0.044715 * x * x)
    return g, dg


def _softplus(z):
    e = jnp.exp(-jnp.abs(z))
    u = 1.0 + e
    log1p = jnp.where(u == 1.0, e, jnp.log(u) * e / jnp.where(u == 1.0, 1.0, u - 1.0))
    return jnp.maximum(z, 0.0) + log1p


def _neg_expm1(x):
    series = -x * (1.0 + x * (0.5 + x * (1.0 / 6.0)))
    return jnp.where(x > -1e-2, series, 1.0 - jnp.exp(x))


def _shift_down(ext, j, halo):
    return pltpu.roll(ext, j, 0)[halo:]


def _shift_up(ext, j, tm):
    n = ext.shape[0]
    return pltpu.roll(ext, n - j, 0)[:tm]


def _scan_causal(a, b):
    tm, n = a.shape
    head_rows = lax.broadcasted_iota(jnp.int32, (SUBLANES, n), 0)
    s = 1
    while s < min(SUBLANES, tm):
        keep = head_rows >= s
        a_r, b_r = pltpu.roll(a, s, 0), pltpu.roll(b, s, 0)
        a_sh = jnp.concatenate([jnp.where(keep, a_r[:SUBLANES], 1.0), a_r[SUBLANES:]], axis=0)
        b_sh = jnp.concatenate([jnp.where(keep, b_r[:SUBLANES], 0.0), b_r[SUBLANES:]], axis=0)
        b = a * b_sh + b
        a = a * a_sh
        s *= 2
    while s < tm:
        b = jnp.concatenate([b[:s], a[s:] * b[:tm - s] + b[s:]], axis=0)
        a = jnp.concatenate([a[:s], a[s:] * a[:tm - s]], axis=0)
        s *= 2
    return a, b


def _scan_anticausal(c, d):
    tm, n = c.shape
    body = tm - SUBLANES
    tail_rows = lax.broadcasted_iota(jnp.int32, (SUBLANES, n), 0) + body
    s = 1
    while s < min(SUBLANES, tm):
        keep = tail_rows < tm - s
        c_r, d_r = pltpu.roll(c, tm - s, 0), pltpu.roll(d, tm - s, 0)
        c_sh = jnp.concatenate([c_r[:body], jnp.where(keep, c_r[body:], 1.0)], axis=0)
        d_sh = jnp.concatenate([d_r[:body], jnp.where(keep, d_r[body:], 0.0)], axis=0)
        d = d + c * d_sh
        c = c * c_sh
        s *= 2
    while s < tm:
        d = jnp.concatenate([d[:tm - s] + c[:tm - s] * d[s:], d[tm - s:]], axis=0)
        c = jnp.concatenate([c[:tm - s] * c[s:], c[tm - s:]], axis=0)
        s *= 2
    return c, d


def _tile(n, want):
    t = min(n, want)
    assert n % t == 0, (n, t)
    return t


def _ffn_fwd(h, gamma, wg, wu, wd, layer, cargo=None):
    S = h.shape[0]
    tm = _tile(S, 1024)
    nt = S // tm
    cargo = cargo or _Cargo()
    n_in, n_out = 5, 4
    nc_in, nc_out = len(cargo.operands), len(cargo.out_shapes)

    def body(*refs):
        h_ref, g_ref, wg_ref, wu_ref, wd_ref = refs[:n_in]
        c_ins = refs[n_in:n_in + nc_in]
        ho_ref, xn_ref, gg_ref, uu_ref = refs[n_in + nc_in:n_in + nc_in + n_out]
        c_outs = refs[n_in + nc_in + n_out:n_in + nc_in + n_out + nc_out]
        xn_s, acc_s = refs[n_in + nc_in + n_out + nc_out:n_in + nc_in + n_out + nc_out + 2]
        sems = refs[n_in + nc_in + n_out + nc_out + 2:]
        t, k = pl.program_id(0), pl.program_id(1)

        @pl.when((t == 0) & (k == 0))
        def _():
            cargo.start(c_ins, c_outs, sems)

        @pl.when((t == nt - 1) & (k == 0))
        def _():
            cargo.forward(c_ins, c_outs, sems)

        @pl.when(k == 0)
        def _():
            _, _, xn = _rms(h_ref[...], g_ref[...])
            xnb = xn.astype(BF16)
            xn_s[...] = xnb
            xn_ref[...] = xnb
            acc_s[...] = jnp.zeros_like(acc_s)

        xnb = xn_s[...]
        g = _dot_nt(xnb, wg_ref[...])
        u = _dot_nt(xnb, wu_ref[...])
        gg_ref[...] = g.astype(BF16)
        uu_ref[...] = u.astype(BF16)
        hid = (g * _sigmoid(g)) * u
        acc_s[...] += _dot(hid.astype(BF16), wd_ref[...])

        @pl.when(k == N_CHIPS - 1)
        def _():
            ho_ref[...] = h_ref[...] + 0.5 * acc_s[...]

        @pl.when((t == nt - 1) & (k == N_CHIPS - 1))
        def _():
            cargo.finish(c_ins, c_outs, sems)

    outs = pl.pallas_call(
        body, name="ffn_fwd",
        grid=(nt, N_CHIPS),
        in_specs=[
            pl.BlockSpec((tm, D_MODEL), lambda t, k: (t, 0)),
            pl.BlockSpec((1, D_MODEL), lambda t, k: (0, 0)),
            pl.BlockSpec((None, None, FF_CHUNK, D_MODEL), lambda t, k: (layer, k, 0, 0)),
            pl.BlockSpec((None, None, FF_CHUNK, D_MODEL), lambda t, k: (layer, k, 0, 0)),
            pl.BlockSpec((None, None, FF_CHUNK, D_MODEL), lambda t, k: (layer, k, 0, 0)),
        ] + _any_specs(nc_in),
        out_specs=[
            pl.BlockSpec((tm, D_MODEL), lambda t, k: (t, 0)),
            pl.BlockSpec((tm, D_MODEL), lambda t, k: (t, 0)),
            pl.BlockSpec((None, tm, FF_CHUNK), lambda t, k: (k, t, 0)),
            pl.BlockSpec((None, tm, FF_CHUNK), lambda t, k: (k, t, 0)),
        ] + _any_specs(nc_out),
        out_shape=[
            jax.ShapeDtypeStruct((S, D_MODEL), F32),
            jax.ShapeDtypeStruct((S, D_MODEL), BF16),
            jax.ShapeDtypeStruct((N_CHIPS, S, FF_CHUNK), BF16),
            jax.ShapeDtypeStruct((N_CHIPS, S, FF_CHUNK), BF16),
        ] + cargo.out_shapes,
        input_output_aliases={n_in + i: n_out + o for i, o in cargo.aliases.items()},
        scratch_shapes=[pltpu.VMEM((tm, D_MODEL), BF16), pltpu.VMEM((tm, D_MODEL), F32)] + cargo.sem_shapes,
        compiler_params=_cparams(("arbitrary", "arbitrary")),
    )(h, gamma, wg, wu, wd, *cargo.operands)
    return outs[:n_out], list(outs[n_out:])


def _ffn_bwd(xn, dout, gg, uu, wg, wu, wd, layer, cargo=None):
    S = xn.shape[0]
    tm = _tile(S, 512)
    nt = S // tm
    nchunk = N_CHIPS - 1
    cargo = cargo or _Cargo()
    n_in, n_out = 7, 4
    nc_in, nc_out = len(cargo.operands), len(cargo.out_shapes)

    def body(*refs):
        xn_ref, do_ref, gg_ref, uu_ref, wg_ref, wu_ref, wd_ref = refs[:n_in]
        c_ins = refs[n_in:n_in + nc_in]
        dwg_ref, dwu_ref, dwd_ref, slab_ref = refs[n_in + nc_in:n_in + nc_in + n_out]
        c_outs = refs[n_in + nc_in + n_out:n_in + nc_in + n_out + nc_out]
        sems = refs[n_in + nc_in + n_out + nc_out:]
        k, t = pl.program_id(0), pl.program_id(1)

        @pl.when((k == 0) & (t == 0))
        def _():
            cargo.start(c_ins, c_outs, sems)

        @pl.when(t == 0)
        def _():
            dwg_ref[...] = jnp.zeros_like(dwg_ref)
            dwu_ref[...] = jnp.zeros_like(dwu_ref)
            dwd_ref[...] = jnp.zeros_like(dwd_ref)

        xnb = xn_ref[...]
        dob = (0.5 * do_ref[...]).astype(BF16)
        g = gg_ref[...].astype(F32)
        u = uu_ref[...].astype(F32)
        s = _sigmoid(g)
        sil = g * s
        dhid = _dot_nt(dob, wd_ref[...])
        dwd_ref[...] += _dot_tn((sil * u).astype(BF16), dob)
        du = (dhid * sil).astype(BF16)
        dg = (dhid * u * (s * (1.0 + g * (1.0 - s)))).astype(BF16)
        dwg_ref[...] += _dot_tn(dg, xnb)
        dwu_ref[...] += _dot_tn(du, xnb)
        slab_ref[...] = (_dot(dg, wg_ref[...]) + _dot(du, wu_ref[...])).astype(BF16)

        @pl.when((k == nchunk - 1) & (t == nt - 1))
        def _():
            cargo.finish(c_ins, c_outs, sems)

    outs = pl.pallas_call(
        body, name="ffn_bwd",
        grid=(nchunk, nt),
        in_specs=[
            pl.BlockSpec((tm, D_MODEL), lambda k, t: (t, 0)),
            pl.BlockSpec((tm, D_MODEL), lambda k, t: (t, 0)),
            pl.BlockSpec((None, tm, FF_CHUNK), lambda k, t: (k, t, 0)),
            pl.BlockSpec((None, tm, FF_CHUNK), lambda k, t: (k, t, 0)),
            pl.BlockSpec((None, None, FF_CHUNK, D_MODEL), lambda k, t: (layer, k, 0, 0)),
            pl.BlockSpec((None, None, FF_CHUNK, D_MODEL), lambda k, t: (layer, k, 0, 0)),
            pl.BlockSpec((None, None, FF_CHUNK, D_MODEL), lambda k, t: (layer, k, 0, 0)),
        ] + _any_specs(nc_in),
        out_specs=[
            pl.BlockSpec((None, FF_CHUNK, D_MODEL), lambda k, t: (k, 0, 0)),
            pl.BlockSpec((None, FF_CHUNK, D_MODEL), lambda k, t: (k, 0, 0)),
            pl.BlockSpec((None, FF_CHUNK, D_MODEL), lambda k, t: (k, 0, 0)),
            pl.BlockSpec((None, tm, D_MODEL), lambda k, t: (k, t, 0)),
        ] + _any_specs(nc_out),
        out_shape=[
            jax.ShapeDtypeStruct((N_CHIPS, FF_CHUNK, D_MODEL), F32),
            jax.ShapeDtypeStruct((N_CHIPS, FF_CHUNK, D_MODEL), F32),
            jax.ShapeDtypeStruct((N_CHIPS, FF_CHUNK, D_MODEL), F32),
            jax.ShapeDtypeStruct((nchunk, S, D_MODEL), BF16),
        ] + cargo.out_shapes,
        input_output_aliases={n_in + i: n_out + o for i, o in cargo.aliases.items()},
        scratch_shapes=list(cargo.sem_shapes),
        compiler_params=_cparams(("arbitrary", "arbitrary")),
    )(xn, dout, gg, uu, wg, wu, wd, *cargo.operands)
    return outs[:n_out], list(outs[n_out:])


def _ffn_bwd_last(xn, dout, gg, uu, wg, wu, wd, layer, slabs, h, gamma, dwg, dwu, dwd):
    S = xn.shape[0]
    tm = _tile(S, 512)
    k = N_CHIPS - 1
    nprev = slabs.shape[0]

    def body(xn_ref, do_ref, gg_ref, uu_ref, wg_ref, wu_ref, wd_ref, slab_ref, h_ref, g_ref, _dwg, _dwu, _dwd,
             dh_ref, dgam_ref, dwg_ref, dwu_ref, dwd_ref):
        @pl.when(pl.program_id(0) == 0)
        def _():
            dgam_ref[...] = jnp.zeros_like(dgam_ref)
            dwg_ref[...] = jnp.zeros_like(dwg_ref)
            dwu_ref[...] = jnp.zeros_like(dwu_ref)
            dwd_ref[...] = jnp.zeros_like(dwd_ref)

        xnb = xn_ref[...]
        do = do_ref[...]
        dob = (0.5 * do).astype(BF16)
        g = gg_ref[...].astype(F32)
        u = uu_ref[...].astype(F32)
        s = _sigmoid(g)
        sil = g * s
        dhid = _dot_nt(dob, wd_ref[...])
        dwd_ref[...] += _dot_tn((sil * u).astype(BF16), dob)
        du = (dhid * sil).astype(BF16)
        dg = (dhid * u * (s * (1.0 + g * (1.0 - s)))).astype(BF16)
        dwg_ref[...] += _dot_tn(dg, xnb)
        dwu_ref[...] += _dot_tn(du, xnb)
        dxn = _dot(dg, wg_ref[...]) + _dot(du, wu_ref[...])
        for i in range(nprev):
            dxn = dxn + slab_ref[i].astype(F32)
        xhat, rstd, _ = _rms(h_ref[...], g_ref[...])
        dhn, dgam = _rms_bwd(xhat, rstd, g_ref[...], dxn)
        dh_ref[...] = do + dhn
        dgam_ref[...] += dgam

    tile = pl.BlockSpec((tm, D_MODEL), lambda t: (t, 0))
    hidden = pl.BlockSpec((None, tm, FF_CHUNK), lambda t: (k, t, 0))
    w_in = pl.BlockSpec((None, None, FF_CHUNK, D_MODEL), lambda t: (layer, k, 0, 0))
    dw_in = pl.BlockSpec((None, FF_CHUNK, D_MODEL), lambda t: (k, 0, 0))
    return pl.pallas_call(
        body, name="ffn_bwd_last",
        grid=(S // tm,),
        in_specs=[tile, tile, hidden, hidden, w_in, w_in,
                  pl.BlockSpec((None, None, FF_CHUNK, D_MODEL), lambda t: (layer, k, 0, 0)),
                  pl.BlockSpec((nprev, tm, D_MODEL), lambda t: (0, t, 0)), tile,
                  pl.BlockSpec((1, D_MODEL), lambda t: (0, 0))] + _any_specs(3),
        out_specs=[tile, pl.BlockSpec((SUBLANES, D_MODEL), lambda t: (0, 0)), dw_in, dw_in,
                   pl.BlockSpec((None, FF_CHUNK, D_MODEL), lambda t: (k, 0, 0))],
        out_shape=[jax.ShapeDtypeStruct((S, D_MODEL), F32), jax.ShapeDtypeStruct((SUBLANES, D_MODEL), F32),
                   jax.ShapeDtypeStruct(dwg.shape, F32), jax.ShapeDtypeStruct(dwu.shape, F32),
                   jax.ShapeDtypeStruct(dwd.shape, F32)],
        input_output_aliases={10: 2, 11: 3, 12: 4},
        compiler_params=_cparams(("arbitrary",)),
    )(xn, dout, gg, uu, wg, wu, wd, slabs, h, gamma, dwg, dwu, dwd)


def _xt_dy(name, x, dy, nchunk, kb, nb, x_by_chunk, y_by_chunk):
    S = x.shape[0]
    tm = _tile(S, 2048)

    def body(x_ref, dy_ref, o_ref):
        @pl.when(pl.program_id(1) == 0)
        def _():
            o_ref[...] = jnp.zeros_like(o_ref)

        o_ref[...] += _dot_tn(x_ref[...].astype(BF16), dy_ref[...].astype(BF16))

    return pl.pallas_call(
        body, name=name,
        grid=(nchunk, S // tm),
        in_specs=[
            pl.BlockSpec((tm, kb), (lambda c, t: (t, c)) if x_by_chunk else (lambda c, t: (t, 0))),
            pl.BlockSpec((tm, nb), (lambda c, t: (t, c)) if y_by_chunk else (lambda c, t: (t, 0))),
        ],
        out_specs=pl.BlockSpec((None, kb, nb), lambda c, t: (c, 0, 0)),
        out_shape=jax.ShapeDtypeStruct((nchunk, kb, nb), F32),
        compiler_params=_cparams(("arbitrary", "arbitrary")),
    )(x, dy)


def _lru_gates(xc, wbd_ref, ba, bx, apar):
    xcb = xc.astype(BF16)
    r_parts, ig_parts = [], []
    for q in range(D_RNN // GATE_CHUNK):
        lo, hi = q * GATE_CHUNK, (q + 1) * GATE_CHUNK
        pre = _dot(xcb[:, lo:hi], wbd_ref[q])
        r_parts.append(_sigmoid(pre[:, :GATE_CHUNK] + ba[:, lo:hi]))
        ig_parts.append(_sigmoid(pre[:, GATE_CHUNK:] + bx[:, lo:hi]))
    r = jnp.concatenate(r_parts, axis=1)
    ig = jnp.concatenate(ig_parts, axis=1)
    sp = LRU_C * _softplus(-apar)
    log_a = -(r * sp)
    a = jnp.exp(log_a)
    mult = jnp.sqrt(_neg_expm1(2.0 * log_a))
    return r, ig, a, mult, sp


def _conv_causal(xb, tail, cw_ref, cb):
    ext = jnp.concatenate([tail, xb], axis=0)
    xc = cb + cw_ref[CONV_WIDTH - 1:CONV_WIDTH, :] * xb
    for j in range(1, CONV_WIDTH):
        xc = xc + cw_ref[CONV_WIDTH - 1 - j:CONV_WIDTH - j, :] * _shift_down(ext, j, SUBLANES)
    return xc, ext


def _lru_fwd(h, gamma, win, layer, convw, convb, wbd, ba, bx, apar, wout):
    S = h.shape[0]
    tm = _tile(S, 256)

    def body(h_ref, g_ref, win_ref, cw_ref, cb_ref, wbd_ref, ba_ref, bx_ref, ap_ref, wout_ref,
             ho_ref, xn_ref, z_ref, hs_ref, gates_ref, tail_s, carry_s):
        @pl.when(pl.program_id(0) == 0)
        def _():
            tail_s[...] = jnp.zeros_like(tail_s)
            carry_s[...] = jnp.zeros_like(carry_s)

        hf = h_ref[...]
        _, _, xn = _rms(hf, g_ref[...])
        xnb = xn.astype(BF16)
        xn_ref[...] = xnb
        for k in range(N_CHIPS):
            z_ref[:, k * RNN_IN_CHUNK:(k + 1) * RNN_IN_CHUNK] = _dot(xnb, win_ref[k])
        gate = z_ref[:, :D_RNN]
        xb = z_ref[:, D_RNN:]
        xc, _ = _conv_causal(xb, tail_s[...], cw_ref, cb_ref[...])
        tail_s[...] = xb[tm - SUBLANES:, :]
        r, ig, a, mult, _ = _lru_gates(xc, wbd_ref, ba_ref[...], bx_ref[...], ap_ref[...])
        for plane, val in enumerate((xc, r, ig, a, mult)):
            gates_ref[plane] = val
        big_a, big_b = _scan_causal(a, mult * (ig * xc))
        hs = big_a * carry_s[SUBLANES - 1:SUBLANES, :] + big_b
        hs_ref[...] = hs
        carry_s[...] = hs[tm - SUBLANES:, :]
        y = hs * _gelu(gate)
        ho_ref[...] = hf + _dot(y.astype(BF16), wout_ref[...])

    row = lambda n: pl.BlockSpec((1, n), lambda t: (0, 0))
    return pl.pallas_call(
        body, name="lru_fwd",
        grid=(S // tm,),
        in_specs=[
            pl.BlockSpec((tm, D_MODEL), lambda t: (t, 0)),
            row(D_MODEL),
            pl.BlockSpec((None, N_CHIPS, D_MODEL, RNN_IN_CHUNK), lambda t: (layer, 0, 0, 0)),
            pl.BlockSpec((CONV_WIDTH, D_RNN), lambda t: (0, 0)),
            row(D_RNN),
            pl.BlockSpec((D_RNN // GATE_CHUNK, GATE_CHUNK, 2 * GATE_CHUNK), lambda t: (0, 0, 0)),
            row(D_RNN), row(D_RNN), row(D_RNN),
            pl.BlockSpec((None, D_RNN, D_MODEL), lambda t: (layer, 0, 0)),
        ],
        out_specs=[
            pl.BlockSpec((tm, D_MODEL), lambda t: (t, 0)),
            pl.BlockSpec((tm, D_MODEL), lambda t: (t, 0)),
            pl.BlockSpec((tm, 2 * D_RNN), lambda t: (t, 0)),
            pl.BlockSpec((tm, D_RNN), lambda t: (t, 0)),
            pl.BlockSpec((N_GATE_PLANES, tm, D_RNN), lambda t: (0, t, 0)),
        ],
        out_shape=[
            jax.ShapeDtypeStruct((S, D_MODEL), F32),
            jax.ShapeDtypeStruct((S, D_MODEL), BF16),
            jax.ShapeDtypeStruct((S, 2 * D_RNN), F32),
            jax.ShapeDtypeStruct((S, D_RNN), F32),
            jax.ShapeDtypeStruct((N_GATE_PLANES, S, D_RNN), F32),
        ],
        scratch_shapes=[pltpu.VMEM((SUBLANES, D_RNN), F32), pltpu.VMEM((SUBLANES, D_RNN), F32)],
        compiler_params=_cparams(("arbitrary",)),
    )(h, gamma, win, convw, convb, wbd, ba, bx, apar, wout)


def _lru_bwd_seq(dout, z, hs, gates, convw, wbd, apar, wout, layer):
    S = dout.shape[0]
    tm = _tile(S, 256)
    nt = S // tm
    per8 = tm // SUBLANES
    rev = lambda i: nt - 1 - i
    prev8 = lambda i: jnp.maximum(rev(i) * per8 - 1, 0)

    def body(do_ref, z_ref, hs_ref, gates_ref, ztail_ref, hstail_ref, cw_ref, wbd_ref, ap_ref, wout_ref,
             dz_ref, dpre_ref, xc_ref, y_ref, dcw_ref, vec_ref, a_first_s, g_first_s, dxc_head_s):
        i = pl.program_id(0)
        first_in_time = rev(i) == 0

        @pl.when(i == 0)
        def _():
            a_first_s[...] = jnp.zeros_like(a_first_s)
            g_first_s[...] = jnp.zeros_like(g_first_s)
            dxc_head_s[...] = jnp.zeros_like(dxc_head_s)
            dcw_ref[...] = jnp.zeros_like(dcw_ref)
            vec_ref[...] = jnp.zeros_like(vec_ref)

        gate = z_ref[:, :D_RNN]
        xb = z_ref[:, D_RNN:]
        hist = jnp.where(first_in_time, 0.0, 1.0)
        xext = jnp.concatenate([ztail_ref[:, D_RNN:] * hist, xb], axis=0)
        xc, r, ig, a, mult = (gates_ref[plane] for plane in range(N_GATE_PLANES))
        sp = LRU_C * _softplus(-ap_ref[...])
        hs = hs_ref[...]
        gel, dgel = _gelu_and_grad(gate)
        y = hs * gel
        y_ref[...] = y.astype(BF16)
        xc_ref[...] = xc.astype(BF16)

        dy = _dot_nt(do_ref[...].astype(BF16), wout_ref[...])
        dhs = dy * gel
        dgate = dy * hs * dgel

        coef = _shift_up(jnp.concatenate([a, a_first_s[...]], axis=0), 1, tm)
        big_c, big_d = _scan_anticausal(coef, dhs)
        g = big_d + big_c * g_first_s[0:1, :]
        g_first_s[...] = g[:SUBLANES, :]
        a_first_s[...] = a[:SUBLANES, :]

        hs_prev = _shift_down(jnp.concatenate([hstail_ref[...] * hist, hs], axis=0), 1, SUBLANES)
        da = g * hs_prev
        dmult = g * ig * xc
        dig = g * mult * xc
        dxc = g * mult * ig
        dlog_a = da * a - dmult * (a * a) / mult
        dr = -(dlog_a * sp)
        dpre_a = dr * r * (1.0 - r)
        dpre_x = dig * ig * (1.0 - ig)
        d_apar = dlog_a * r * (LRU_C * _sigmoid(-ap_ref[...]))

        for q in range(D_RNN // GATE_CHUNK):
            lo, hi = q * GATE_CHUNK, (q + 1) * GATE_CHUNK
            dpre_q = jnp.concatenate([dpre_a[:, lo:hi], dpre_x[:, lo:hi]], axis=1).astype(BF16)
            dpre_ref[:, 2 * lo:2 * hi] = dpre_q
            dxc_q = _dot_nt(dpre_q, wbd_ref[q])
            if q == 0:
                dxc_parts = [dxc_q]
            else:
                dxc_parts.append(dxc_q)
        dxc = dxc + jnp.concatenate(dxc_parts, axis=1)

        dext = jnp.concatenate([dxc, dxc_head_s[...]], axis=0)
        dxb = cw_ref[CONV_WIDTH - 1:CONV_WIDTH, :] * dxc
        for j in range(1, CONV_WIDTH):
            dxb = dxb + cw_ref[CONV_WIDTH - 1 - j:CONV_WIDTH - j, :] * _shift_up(dext, j, tm)
        dxc_head_s[...] = dxc[:SUBLANES, :]
        dz_ref[:, :D_RNN] = dgate.astype(BF16)
        dz_ref[:, D_RNN:] = dxb.astype(BF16)

        dcw_ref[CONV_WIDTH - 1] += _rowsum8(dxc * xb)
        for j in range(1, CONV_WIDTH):
            dcw_ref[CONV_WIDTH - 1 - j] += _rowsum8(dxc * _shift_down(xext, j, SUBLANES))
        vec_ref[0] += _rowsum8(dxc)
        vec_ref[1] += _rowsum8(dpre_a)
        vec_ref[2] += _rowsum8(dpre_x)
        vec_ref[3] += _rowsum8(d_apar)

    row = lambda n: pl.BlockSpec((1, n), lambda i: (0, 0))
    return pl.pallas_call(
        body, name="lru_bwd_seq",
        grid=(nt,),
        in_specs=[
            pl.BlockSpec((tm, D_MODEL), lambda i: (rev(i), 0)),
            pl.BlockSpec((tm, 2 * D_RNN), lambda i: (rev(i), 0)),
            pl.BlockSpec((tm, D_RNN), lambda i: (rev(i), 0)),
            pl.BlockSpec((N_GATE_PLANES, tm, D_RNN), lambda i: (0, rev(i), 0)),
            pl.BlockSpec((SUBLANES, 2 * D_RNN), lambda i: (prev8(i), 0)),
            pl.BlockSpec((SUBLANES, D_RNN), lambda i: (prev8(i), 0)),
            pl.BlockSpec((CONV_WIDTH, D_RNN), lambda i: (0, 0)),
            pl.BlockSpec((D_RNN // GATE_CHUNK, GATE_CHUNK, 2 * GATE_CHUNK), lambda i: (0, 0, 0)),
            row(D_RNN),
            pl.BlockSpec((None, D_RNN, D_MODEL), lambda i: (layer, 0, 0)),
        ],
        out_specs=[
            pl.BlockSpec((tm, 2 * D_RNN), lambda i: (rev(i), 0)),
            pl.BlockSpec((tm, 2 * D_RNN), lambda i: (rev(i), 0)),
            pl.BlockSpec((tm, D_RNN), lambda i: (rev(i), 0)),
            pl.BlockSpec((tm, D_RNN), lambda i: (rev(i), 0)),
            pl.BlockSpec((CONV_WIDTH, SUBLANES, D_RNN), lambda i: (0, 0, 0)),
            pl.BlockSpec((4, SUBLANES, D_RNN), lambda i: (0, 0, 0)),
        ],
        out_shape=[
            jax.ShapeDtypeStruct((S, 2 * D_RNN), BF16),
            jax.ShapeDtypeStruct((S, 2 * D_RNN), BF16),
            jax.ShapeDtypeStruct((S, D_RNN), BF16),
            jax.ShapeDtypeStruct((S, D_RNN), BF16),
            jax.ShapeDtypeStruct((CONV_WIDTH, SUBLANES, D_RNN), F32),
            jax.ShapeDtypeStruct((4, SUBLANES, D_RNN), F32),
        ],
        scratch_shapes=[pltpu.VMEM((SUBLANES, D_RNN), F32)] * 3,
        compiler_params=_cparams(("arbitrary",)),
    )(dout, z, hs, gates, z, hs, convw, wbd, apar, wout)


def _lru_bwd_in(dz, h, gamma, dres, win, layer):
    S = h.shape[0]
    tm = _tile(S, 512)

    def body(dz_ref, h_ref, g_ref, dres_ref, win_ref, dh_ref, dgam_ref):
        dxn = _dot_nt(dz_ref[:, :RNN_IN_CHUNK], win_ref[0])
        for k in range(1, N_CHIPS):
            dxn = dxn + _dot_nt(dz_ref[:, k * RNN_IN_CHUNK:(k + 1) * RNN_IN_CHUNK], win_ref[k])
        xhat, rstd, _ = _rms(h_ref[...], g_ref[...])
        dhn, dgam = _rms_bwd(xhat, rstd, g_ref[...], dxn)
        dh_ref[...] = dres_ref[...] + dhn

        @pl.when(pl.program_id(0) == 0)
        def _():
            dgam_ref[...] = jnp.zeros_like(dgam_ref)

        dgam_ref[...] += dgam

    return pl.pallas_call(
        body, name="lru_bwd_in",
        grid=(S // tm,),
        in_specs=[
            pl.BlockSpec((tm, 2 * D_RNN), lambda t: (t, 0)),
            pl.BlockSpec((tm, D_MODEL), lambda t: (t, 0)),
            pl.BlockSpec((1, D_MODEL), lambda t: (0, 0)),
            pl.BlockSpec((tm, D_MODEL), lambda t: (t, 0)),
            pl.BlockSpec((None, N_CHIPS, D_MODEL, RNN_IN_CHUNK), lambda t: (layer, 0, 0, 0)),
        ],
        out_specs=[
            pl.BlockSpec((tm, D_MODEL), lambda t: (t, 0)),
            pl.BlockSpec((SUBLANES, D_MODEL), lambda t: (0, 0)),
        ],
        out_shape=[jax.ShapeDtypeStruct((S, D_MODEL), F32), jax.ShapeDtypeStruct((SUBLANES, D_MODEL), F32)],
        compiler_params=_cparams(("arbitrary",)),
    )(dz, h, gamma, dres, win)


def _pool_inv_count(t_index, tm):
    rows = (lax.broadcasted_iota(jnp.int32, (tm, D_MODEL), 0) + t_index * tm + 1).astype(F32)
    col = lax.broadcasted_iota(jnp.int32, (tm, D_MODEL), 1)
    win = jnp.where(col < POOL_GROUP_DIM, float(POOL_WINDOWS[0]),
                    jnp.where(col < 2 * POOL_GROUP_DIM, float(POOL_WINDOWS[1]),
                              jnp.where(col < 3 * POOL_GROUP_DIM, float(POOL_WINDOWS[2]), float(POOL_WINDOWS[3]))))
    return 1.0 / jnp.minimum(rows, win)


def _window_sums(ext, shift, take):
    gd = POOL_GROUP_DIM
    s2 = ext + shift(ext, 1)
    s4 = s2[:, gd:] + shift(s2[:, gd:], 2)
    s8 = s4[:, gd:] + shift(s4[:, gd:], 4)
    s16 = s8[:, gd:] + shift(s8[:, gd:], 8)
    return jnp.concatenate([take(s2[:, :gd]), take(s4[:, :gd]), take(s8[:, :gd]), take(s16)], axis=1)


def _pool_fwd(h, gamma, pw, pb, pscale, layer):
    S = h.shape[0]
    tm = _tile(S, 512)

    def body(h_ref, g_ref, pw_ref, pb_ref, ps_ref, ho_ref, u_ref, tail_s):
        t = pl.program_id(0)

        @pl.when(t == 0)
        def _():
            tail_s[...] = jnp.zeros_like(tail_s)

        hf = h_ref[...]
        _, _, hn = _rms(hf, g_ref[...])
        ext = jnp.concatenate([tail_s[...], hn], axis=0)
        tail_s[...] = hn[tm - POOL_HALO:, :]
        sums = _window_sums(ext, lambda v, j: pltpu.roll(v, j, 0), lambda v: v[POOL_HALO:])
        ub = (sums * _pool_inv_count(t, tm) - hn).astype(BF16)
        u_ref[...] = ub
        ys = [_dot(ub[:, g * POOL_GROUP_DIM:(g + 1) * POOL_GROUP_DIM], pw_ref[g]) for g in range(len(POOL_WINDOWS))]
        y = jnp.concatenate(ys, axis=1)
        ho_ref[...] = hf + (y + pb_ref[...]) * ps_ref[...]

    row = pl.BlockSpec((1, D_MODEL), lambda t: (0, 0))
    return pl.pallas_call(
        body, name="pool_fwd",
        grid=(S // tm,),
        in_specs=[
            pl.BlockSpec((tm, D_MODEL), lambda t: (t, 0)), row,
            pl.BlockSpec((None, len(POOL_WINDOWS), POOL_GROUP_DIM, POOL_GROUP_DIM), lambda t: (layer, 0, 0, 0)),
            row, row,
        ],
        out_specs=[pl.BlockSpec((tm, D_MODEL), lambda t: (t, 0)), pl.BlockSpec((tm, D_MODEL), lambda t: (t, 0))],
        out_shape=[jax.ShapeDtypeStruct((S, D_MODEL), F32), jax.ShapeDtypeStruct((S, D_MODEL), BF16)],
        scratch_shapes=[pltpu.VMEM((POOL_HALO, D_MODEL), F32)],
        compiler_params=_cparams(("arbitrary",)),
    )(h, gamma, pw, pb, pscale)


def _pool_bwd(dout, h, u, gamma, pw, pb, pscale, layer):
    S = h.shape[0]
    tm = _tile(S, 512)
    nt = S // tm
    rev = lambda i: nt - 1 - i
    ngroup = len(POOL_WINDOWS)

    def body(do_ref, h_ref, u_ref, g_ref, pw_ref, pb_ref, ps_ref, dh_ref, dpre_ref, vec_ref, head_s):
        i = pl.program_id(0)

        @pl.when(i == 0)
        def _():
            head_s[...] = jnp.zeros_like(head_s)
            vec_ref[...] = jnp.zeros_like(vec_ref)

        do = do_ref[...]
        ub = u_ref[...]
        gsl = lambda v, g: v[:, g * POOL_GROUP_DIM:(g + 1) * POOL_GROUP_DIM]
        y = jnp.concatenate([_dot(gsl(ub, g), pw_ref[g]) for g in range(ngroup)], axis=1)
        dpre = do * ps_ref[...]
        dpb = dpre.astype(BF16)
        dpre_ref[...] = dpb
        du = jnp.concatenate([_dot_nt(gsl(dpb, g), pw_ref[g]) for g in range(ngroup)], axis=1)
        v = du * _pool_inv_count(rev(i), tm)
        ext = jnp.concatenate([v, head_s[...]], axis=0)
        head_s[...] = v[:POOL_HALO, :]
        n = tm + POOL_HALO
        dhn = _window_sums(ext, lambda w, j: pltpu.roll(w, n - j, 0), lambda w: w[:tm]) - du
        xhat, rstd, _ = _rms(h_ref[...], g_ref[...])
        dh_in, dgam = _rms_bwd(xhat, rstd, g_ref[...], dhn)
        dh_ref[...] = do + dh_in
        vec_ref[0] += dgam
        vec_ref[1] += _rowsum8(dpre)
        vec_ref[2] += _rowsum8(do * (y + pb_ref[...]))

    row = pl.BlockSpec((1, D_MODEL), lambda i: (0, 0))
    tile = pl.BlockSpec((tm, D_MODEL), lambda i: (rev(i), 0))
    return pl.pallas_call(
        body, name="pool_bwd",
        grid=(nt,),
        in_specs=[tile, tile, tile, row,
                  pl.BlockSpec((None, ngroup, POOL_GROUP_DIM, POOL_GROUP_DIM), lambda i: (layer, 0, 0, 0)), row, row],
        out_specs=[tile, tile, pl.BlockSpec((3, SUBLANES, D_MODEL), lambda i: (0, 0, 0))],
        out_shape=[jax.ShapeDtypeStruct((S, D_MODEL), F32), jax.ShapeDtypeStruct((S, D_MODEL), BF16),
                   jax.ShapeDtypeStruct((3, SUBLANES, D_MODEL), F32)],
        scratch_shapes=[pltpu.VMEM((POOL_HALO, D_MODEL), F32)],
        compiler_params=_cparams(("arbitrary",)),
    )(dout, h, u, gamma, pw, pb, pscale)


def _ple_parts(hf, gamma, p_tile, wgate_ref, wproj_ref):
    xhat, rstd, xn = _rms(hf, gamma)
    xnb = xn.astype(BF16)
    gate = _sigmoid(_dot(xnb, wgate_ref[...]))
    pb = p_tile.astype(BF16)
    proj = jnp.concatenate([_dot(pb, wproj_ref[k]) for k in range(N_CHIPS)], axis=1)
    return xhat, rstd, xnb, pb, gate, proj


def _ple_fwd(h, gamma, p_l, wgate, wproj, layer):
    S = h.shape[0]
    tm = _tile(S, 512)

    def body(h_ref, g_ref, p_ref, wgate_ref, wproj_ref, ho_ref):
        hf = h_ref[...]
        _, _, _, _, gate, proj = _ple_parts(hf, g_ref[...], p_ref[...], wgate_ref, wproj_ref)
        ho_ref[...] = hf + gate * proj

    return pl.pallas_call(
        body, name="ple_fwd",
        grid=(S // tm,),
        in_specs=[
            pl.BlockSpec((tm, D_MODEL), lambda t: (t, 0)),
            pl.BlockSpec((1, D_MODEL), lambda t: (0, 0)),
            pl.BlockSpec((tm, PLE_DIM), lambda t: (t, 0)),
            pl.BlockSpec((None, D_MODEL, D_MODEL), lambda t: (layer, 0, 0)),
            pl.BlockSpec((None, N_CHIPS, PLE_DIM, PLE_DIM), lambda t: (layer, 0, 0, 0)),
        ],
        out_specs=pl.BlockSpec((tm, D_MODEL), lambda t: (t, 0)),
        out_shape=jax.ShapeDtypeStruct((S, D_MODEL), F32),
        compiler_params=_cparams(("arbitrary",)),
    )(h, gamma, p_l, wgate, wproj)


def _ple_bwd(dout, h, gamma, p_l, wgate, wproj, layer):
    S = h.shape[0]
    tm = _tile(S, 512)

    def body(do_ref, h_ref, g_ref, p_ref, wgate_ref, wproj_ref, dh_ref, dwg_ref, dwp_ref, dgam_ref):
        @pl.when(pl.program_id(0) == 0)
        def _():
            dgam_ref[...] = jnp.zeros_like(dgam_ref)
            dwg_ref[...] = jnp.zeros_like(dwg_ref)
            dwp_ref[...] = jnp.zeros_like(dwp_ref)

        do = do_ref[...]
        xhat, rstd, xnb, pb, gate, proj = _ple_parts(h_ref[...], g_ref[...], p_ref[...], wgate_ref, wproj_ref)
        dproj = (do * gate).astype(BF16)
        dpre = (do * proj * gate * (1.0 - gate)).astype(BF16)
        dwg_ref[...] += _dot_tn(xnb, dpre)
        for k in range(N_CHIPS):
            dwp_ref[k] += _dot_tn(pb, dproj[:, k * PLE_DIM:(k + 1) * PLE_DIM])
        dhn, dgam = _rms_bwd(xhat, rstd, g_ref[...], _dot_nt(dpre, wgate_ref[...]))
        dh_ref[...] = do + dhn
        dgam_ref[...] += dgam

    tile = pl.BlockSpec((tm, D_MODEL), lambda t: (t, 0))
    return pl.pallas_call(
        body, name="ple_bwd",
        grid=(S // tm,),
        in_specs=[
            tile, tile,
            pl.BlockSpec((1, D_MODEL), lambda t: (0, 0)),
            pl.BlockSpec((tm, PLE_DIM), lambda t: (t, 0)),
            pl.BlockSpec((None, D_MODEL, D_MODEL), lambda t: (layer, 0, 0)),
            pl.BlockSpec((None, N_CHIPS, PLE_DIM, PLE_DIM), lambda t: (layer, 0, 0, 0)),
        ],
        out_specs=[tile, pl.BlockSpec((D_MODEL, D_MODEL), lambda t: (0, 0)),
                   pl.BlockSpec((N_CHIPS, PLE_DIM, PLE_DIM), lambda t: (0, 0, 0)),
                   pl.BlockSpec((SUBLANES, D_MODEL), lambda t: (0, 0))],
        out_shape=[jax.ShapeDtypeStruct((S, D_MODEL), F32), jax.ShapeDtypeStruct((D_MODEL, D_MODEL), F32),
                   jax.ShapeDtypeStruct((N_CHIPS, PLE_DIM, PLE_DIM), F32),
                   jax.ShapeDtypeStruct((SUBLANES, D_MODEL), F32)],
        compiler_params=_cparams(("arbitrary",)),
    )(dout, h, gamma, p_l, wgate, wproj)


def _final(h, gamma, target):
    S = h.shape[0]
    tm = _tile(S, 512)

    def body(h_ref, g_ref, tgt_ref, dh_ref, dgam_ref, loss_ref):
        xhat, rstd, y = _rms(h_ref[...], g_ref[...])
        err = y - tgt_ref[...]
        dy = err * (1.0 / D_MODEL)
        dhn, dgam = _rms_bwd(xhat, rstd, g_ref[...], dy)
        dh_ref[...] = dhn
        sq = _rowsum8(err * err)
        part = sq[:, :LANES]
        for j in range(1, D_MODEL // LANES):
            part = part + sq[:, j * LANES:(j + 1) * LANES]

        @pl.when(pl.program_id(0) == 0)
        def _():
            dgam_ref[...] = jnp.zeros_like(dgam_ref)
            loss_ref[...] = jnp.zeros_like(loss_ref)

        dgam_ref[...] += dgam
        loss_ref[...] += part * (0.5 / D_MODEL)

    tile = pl.BlockSpec((tm, D_MODEL), lambda t: (t, 0))
    return pl.pallas_call(
        body, name="final_loss",
        grid=(S // tm,),
        in_specs=[tile, pl.BlockSpec((1, D_MODEL), lambda t: (0, 0)), tile],
        out_specs=[tile, pl.BlockSpec((SUBLANES, D_MODEL), lambda t: (0, 0)),
                   pl.BlockSpec((SUBLANES, LANES), lambda t: (0, 0))],
        out_shape=[jax.ShapeDtypeStruct((S, D_MODEL), F32), jax.ShapeDtypeStruct((SUBLANES, D_MODEL), F32),
                   jax.ShapeDtypeStruct((SUBLANES, LANES), F32)],
        compiler_params=_cparams(("arbitrary",)),
    )(h, gamma, target)


def _mesh_pos():
    return lax.axis_index("x"), lax.axis_index("y"), lax.axis_index("c")


def _other_chip(x, y, j):
    fx, fy = CHIP_FLIPS[j]
    return (1 - x if fx else x), (1 - y if fy else y)


def _any_specs(n):
    return [pl.BlockSpec(memory_space=pl.ANY)] * n


class _Cargo:
    def __init__(self):
        self.operands, self.out_shapes, self.aliases, self.sem_shapes, self.names = [], [], {}, [], []
        self.start = lambda ins, outs, sems: None
        self.forward = lambda ins, outs, sems: None
        self.finish = lambda ins, outs, sems: None


def _remote(src, dst, send, recv, device):
    return pltpu.make_async_remote_copy(src_ref=src, dst_ref=dst, send_sem=send, recv_sem=recv,
                                        device_id=device, device_id_type=MESH_ID)


def _gather_cargo(bufs, pieces):
    cargo = _Cargo()
    if not pieces:
        return cargo
    plist = []
    for name, layer in pieces:
        if name not in cargo.names:
            cargo.names.append(name)
            cargo.operands.append(bufs[name])
        plist.append((cargo.names.index(name), layer, bufs[name].shape[2] // 2))
    nflip = len(CHIP_FLIPS)
    cargo.out_shapes = [jax.ShapeDtypeStruct(b.shape, b.dtype) for b in cargo.operands]
    cargo.aliases = {i: i for i in range(len(cargo.operands))}
    cargo.sem_shapes = [pltpu.SemaphoreType.DMA((len(plist) * nflip,))] * 4

    def copies(outs, sems):
        send1, recv1, send2, recv2 = sems
        x, y, c = _mesh_pos()
        k = 2 * x + y

        def blk(p, chip, cc):
            b, layer, hrows = plist[p]
            return outs[b].at[layer, chip, pl.ds(cc * hrows, hrows), :]

        def chip_of(j):
            px, py = _other_chip(x, y, j)
            return 2 * px + py

        def ici(p, j):
            px, py = _other_chip(x, y, j)
            return _remote(blk(p, k, c), blk(p, k, c), send1.at[p * nflip + j], recv1.at[p * nflip + j], (px, py, c))

        def landed(p, j):
            px, py = _other_chip(x, y, j)
            return _remote(blk(p, k, c), blk(p, chip_of(j), c), send1.at[p * nflip + j], recv1.at[p * nflip + j],
                           (px, py, c))

        def d2d(p, j, cc):
            return _remote(blk(p, chip_of(j), cc), blk(p, chip_of(j), cc), send2.at[p * nflip + j],
                           recv2.at[p * nflip + j], (x, y, 1 - c))

        return c, ici, landed, d2d

    def start(ins, outs, sems):
        _, ici, _, _ = copies(outs, sems)
        for p in range(len(plist)):
            for j in range(nflip):
                ici(p, j).start()

    def forward(ins, outs, sems):
        c, _, landed, d2d = copies(outs, sems)
        for j in range(nflip):
            for p in range(len(plist)):
                landed(p, j).wait_recv()
                d2d(p, j, c).start()

    def finish(ins, outs, sems):
        c, ici, _, d2d = copies(outs, sems)
        for p in range(len(plist)):
            for j in range(nflip):
                ici(p, j).wait_send()
                d2d(p, j, c).wait_send()
                d2d(p, j, 1 - c).wait_recv()

    cargo.start, cargo.forward, cargo.finish = start, forward, finish
    return cargo


def _reduce_cargo(grads, presums):
    cargo = _Cargo()
    na, nb = len(grads), len(presums)
    nflip = len(CHIP_FLIPS)
    cargo.operands = list(grads) + list(presums)
    cargo.out_shapes = ([jax.ShapeDtypeStruct((g.shape[0], g.shape[1] // 2, g.shape[2]), g.dtype) for g in grads]
                        + [jax.ShapeDtypeStruct((nflip,) + ps.shape[1:], ps.dtype) for ps in presums])
    cargo.sem_shapes = ([pltpu.SemaphoreType.DMA((na,))] * 2 if na else []) + (
        [pltpu.SemaphoreType.DMA((nb * nflip,))] * 2 if nb else [])

    def copies(ins, outs, sems):
        x, y, c = _mesh_pos()
        out = []
        if na:
            send, recv = sems[0], sems[1]
            for a in range(na):
                hrows = grads[a].shape[1] // 2
                out.append(_remote(ins[a].at[:, pl.ds((1 - c) * hrows, hrows), :], outs[a], send.at[a], recv.at[a],
                                   (x, y, 1 - c)))
        if nb:
            send, recv = sems[-2], sems[-1]
            for b in range(nb):
                for j in range(nflip):
                    px, py = _other_chip(x, y, j)
                    out.append(_remote(ins[na + b].at[2 * px + py], outs[na + b].at[j], send.at[b * nflip + j],
                                       recv.at[b * nflip + j], (px, py, c)))
        return out

    def start(ins, outs, sems):
        for cp in copies(ins, outs, sems):
            cp.start()

    def finish(ins, outs, sems):
        for cp in copies(ins, outs, sems):
            cp.wait()

    cargo.start, cargo.finish = start, finish
    return cargo


def _run_cargo(name, cargo):
    nin, nout = len(cargo.operands), len(cargo.out_shapes)

    def body(*refs):
        ins, outs, sems = refs[:nin], refs[nin:nin + nout], refs[nin + nout:]
        cargo.start(ins, outs, sems)
        cargo.forward(ins, outs, sems)
        cargo.finish(ins, outs, sems)

    return list(pl.pallas_call(
        body, name=name,
        in_specs=_any_specs(nin), out_specs=_any_specs(nout), out_shape=cargo.out_shapes,
        input_output_aliases=dict(cargo.aliases), scratch_shapes=list(cargo.sem_shapes),
    )(*cargo.operands))


def _join_siblings(bufs):
    nb = len(bufs)
    items = [(b, layer) for b, buf in enumerate(bufs) for layer in range(buf.shape[0])]

    def body(*refs):
        outs = refs[nb:2 * nb]
        send, recv = refs[2 * nb:]
        x, y, c = _mesh_pos()

        def half(i, cc):
            b, layer = items[i]
            hrows = bufs[b].shape[1] // 2
            blk = outs[b].at[layer, pl.ds(cc * hrows, hrows), :]
            return _remote(blk, blk, send.at[i], recv.at[i], (x, y, 1 - c))

        for i in range(len(items)):
            half(i, c).start()
        for i in range(len(items)):
            half(i, c).wait_send()
            half(i, 1 - c).wait_recv()

    return list(pl.pallas_call(
        body, name="grad_sibling_join",
        in_specs=_any_specs(nb), out_specs=_any_specs(nb),
        out_shape=[jax.ShapeDtypeStruct(b.shape, b.dtype) for b in bufs],
        input_output_aliases={i: i for i in range(nb)},
        scratch_shapes=[pltpu.SemaphoreType.DMA((len(items),))] * 2,
    )(*bufs))


def _cast_place(w3, pos, dtype):
    L, rows, cols = w3.shape

    def body(pos_ref, w_ref, o_ref):
        o_ref[...] = w_ref[...].astype(dtype)

    return pl.pallas_call(
        body, name="cast_place",
        grid_spec=pltpu.PrefetchScalarGridSpec(
            num_scalar_prefetch=1, grid=(L,),
            in_specs=[pl.BlockSpec((None, rows, cols), lambda l, pos: (l, 0, 0))],
            out_specs=pl.BlockSpec((None, None, rows, cols), lambda l, pos: (l, pos[0], 0, 0))),
        out_shape=jax.ShapeDtypeStruct((L, N_CHIPS, rows, cols), dtype),
        compiler_params=_cparams(("arbitrary",)),
    )(pos, w3)


def _allreduce_small(buf):
    R = buf.shape[0]
    half = R // 2
    assert half % SUBLANES == 0, R

    def body(in_ref, out_ref, land, send, recv):
        x, y, c = _mesh_pos()
        out_ref[...] = in_ref[...]
        cp = _remote(out_ref, land.at[0], send.at[0], recv.at[0], (x, y, 1 - c))
        cp.start()
        cp.wait()
        out_ref[...] = out_ref[...] + land[0]
        along_y, along_x = (x, 1 - y, c), (1 - x, y, c)
        lo, hi = pl.ds(0, half), pl.ds(half, half)
        for stage, (peer_lo, peer_hi) in enumerate(((along_y, along_x), (along_x, along_y))):
            slot = 1 + stage
            cps = [_remote(out_ref.at[lo], land.at[slot, lo], send.at[1 + 2 * stage], recv.at[1 + 2 * stage], peer_lo),
                   _remote(out_ref.at[hi], land.at[slot, hi], send.at[2 + 2 * stage], recv.at[2 + 2 * stage], peer_hi)]
            for cp in cps:
                cp.start()
            for cp in cps:
                cp.wait()
            out_ref[...] = out_ref[...] + land[slot]

    return pl.pallas_call(
        body, name="allreduce_small",
        in_specs=[pl.BlockSpec(memory_space=pltpu.VMEM)],
        out_specs=pl.BlockSpec(memory_space=pltpu.VMEM),
        out_shape=jax.ShapeDtypeStruct((R, LANES), F32),
        scratch_shapes=[pltpu.VMEM((3, R, LANES), F32), pltpu.SemaphoreType.DMA((5,)), pltpu.SemaphoreType.DMA((5,))],
        compiler_params=pltpu.CompilerParams(vmem_limit_bytes=VMEM_LIMIT_MB * 2 ** 20),
    )(buf)


def _presum_with_sibling(grad, landed, pos):
    nchunk, rows, cols = grad.shape
    hrows = rows // 2

    def body(pos_ref, g_ref, l_ref, all_ref, own_ref):
        s = g_ref[...] + l_ref[...]
        all_ref[...] = s.astype(BF16)

        @pl.when(pl.program_id(0) == pos_ref[0])
        def _():
            own_ref[...] = s

    return pl.pallas_call(
        body, name="grad_presum",
        grid_spec=pltpu.PrefetchScalarGridSpec(
            num_scalar_prefetch=1, grid=(nchunk,),
            in_specs=[pl.BlockSpec((None, hrows, cols), lambda k, pos: (k, pos[1], 0)),
                      pl.BlockSpec((None, hrows, cols), lambda k, pos: (k, 0, 0))],
            out_specs=[pl.BlockSpec((None, hrows, cols), lambda k, pos: (k, 0, 0)),
                       pl.BlockSpec((hrows, cols), lambda k, pos: (0, 0))]),
        out_shape=[jax.ShapeDtypeStruct((nchunk, hrows, cols), BF16), jax.ShapeDtypeStruct((hrows, cols), F32)],
        compiler_params=_cparams(("arbitrary",)),
    )(pos, grad, landed)


def _sum_chips(own, landed, stacked, layer, shape3, pos):
    hrows, cols = own.shape

    def body(pos_ref, o_ref, l_ref, *rest):
        s = o_ref[...]
        for j in range(len(CHIP_FLIPS)):
            s = s + l_ref[j].astype(F32)
        rest[-1][...] = s

    in_specs = [pl.BlockSpec((hrows, cols), lambda i, pos: (0, 0)),
                pl.BlockSpec((len(CHIP_FLIPS), hrows, cols), lambda i, pos: (0, 0, 0))]
    args = [pos, own, landed]
    aliases = {}
    if stacked is not None:
        in_specs.append(pl.BlockSpec(memory_space=pl.ANY))
        args.append(stacked)
        aliases = {3: 0}
    return pl.pallas_call(
        body, name="grad_sum_chips",
        grid_spec=pltpu.PrefetchScalarGridSpec(
            num_scalar_prefetch=1, grid=(1,), in_specs=in_specs,
            out_specs=pl.BlockSpec((None, hrows, cols), lambda i, pos: (layer, pos[1], 0))),
        out_shape=jax.ShapeDtypeStruct(shape3, F32),
        input_output_aliases=aliases,
        compiler_params=_cparams(("arbitrary",)),
    )(*args)


def _adamw(w, g, m, v):
    R, C = w.shape
    rb = R
    for cand in (512, 352, 320, 256, 128, 64, 32, 16, 8):
        if R % cand == 0:
            rb = cand
            break
    c1 = 1.0 - ADAM_B1 ** ADAM_STEP
    c2 = 1.0 - ADAM_B2 ** ADAM_STEP

    def body(w_ref, g_ref, m_ref, v_ref, go_ref, d_ref, mo_ref, vo_ref):
        gv = g_ref[...]
        go_ref[...] = gv
        m2 = ADAM_B1 * m_ref[...] + (1.0 - ADAM_B1) * gv
        v2 = ADAM_B2 * v_ref[...] + (1.0 - ADAM_B2) * (gv * gv)
        mo_ref[...] = m2
        vo_ref[...] = v2
        d_ref[...] = -ADAM_LR * ((m2 / c1) / (jnp.sqrt(v2 / c2) + ADAM_EPS) + ADAM_WD * w_ref[...])

    spec = pl.BlockSpec((rb, C), lambda i: (i, 0))
    return pl.pallas_call(
        body, name="adamw",
        grid=(R // rb,),
        in_specs=[spec] * 4, out_specs=[spec] * 4,
        out_shape=[jax.ShapeDtypeStruct((R, C), F32)] * 4,
        compiler_params=_cparams(("arbitrary",)),
    )(w, g, m, v)


def _pack(parts, align=SUBLANES * LANES):
    flat = jnp.concatenate([p.reshape(-1).astype(F32) for p in parts])
    pad = (-flat.shape[0]) % align
    return jnp.pad(flat, (0, pad)).reshape(-1, LANES)


def _unpack(buf, shapes):
    flat = buf.reshape(-1)
    out, off = [], 0
    for shp in shapes:
        size = 1
        for d in shp:
            size *= d
        out.append(flat[off:off + size].reshape(shp))
        off += size
    return out


def _block_diag_gates(w_a, w_x):
    nq = D_RNN // GATE_CHUNK
    hpc = LRU_HEADS // nq
    eye = jnp.eye(hpc, dtype=F32)

    def bd(w):
        wq = w.reshape(nq, hpc, LRU_HEAD_DIM, LRU_HEAD_DIM)
        return (wq[:, :, :, None, :] * eye[None, :, None, :, None]).reshape(nq, GATE_CHUNK, GATE_CHUNK)

    return jnp.concatenate([bd(w_a), bd(w_x)], axis=2).astype(BF16)


def _block_diag_extract(dwbd):
    nq = D_RNN // GATE_CHUNK
    hpc = LRU_HEADS // nq
    eye = jnp.eye(hpc, dtype=F32)

    def ex(d):
        d5 = d.reshape(nq, hpc, LRU_HEAD_DIM, hpc, LRU_HEAD_DIM)
        return jnp.sum(d5 * eye[None, :, None, :, None], axis=3).reshape(LRU_HEADS, LRU_HEAD_DIM, LRU_HEAD_DIM)

    return ex(dwbd[:, :, :GATE_CHUNK]), ex(dwbd[:, :, GATE_CHUNK:])


BIG = ("ffn1_w_gate", "ffn1_w_up", "ffn1_w_down", "lru_w_in", "lru_w_out", "pool_w",
       "ffn2_w_gate", "ffn2_w_up", "ffn2_w_down", "ple_w_gate", "ple_w_proj")
TINY_SHARDED = ("lru_conv_w", "pool_b", "pool_scale")
REPLICATED = ("ffn1_norm", "mix_norm", "lru_conv_b", "lru_w_a", "lru_b_a", "lru_w_x", "lru_b_x", "lru_a_param",
              "ffn2_norm", "ple_norm", "final_norm")
WEIGHT_ORDER = ("ffn1_norm", "ffn1_w_gate", "ffn1_w_up", "ffn1_w_down", "mix_norm", "lru_w_in", "lru_conv_w",
                "lru_conv_b", "lru_w_a", "lru_b_a", "lru_w_x", "lru_b_x", "lru_a_param", "lru_w_out", "pool_w",
                "pool_b", "pool_scale", "ffn2_norm", "ffn2_w_gate", "ffn2_w_up", "ffn2_w_down", "ple_norm",
                "ple_w_gate", "ple_w_proj", "final_norm")


TRANSPOSED = ("ffn1_w_gate", "ffn1_w_up", "ffn2_w_gate", "ffn2_w_up")


def _stored(name, a):
    return jnp.swapaxes(a, 1, 2) if name in TRANSPOSED else a


def _as3(a):
    return a.reshape(a.shape[0], -1, a.shape[-1])


def kernel(x, p, ffn1_norm, ffn1_w_gate, ffn1_w_up, ffn1_w_down, mix_norm, lru_w_in, lru_conv_w, lru_conv_b, lru_w_a, lru_b_a, lru_w_x, lru_b_x, lru_a_param, lru_w_out, pool_w, pool_b, pool_scale, ffn2_norm, ffn2_w_gate, ffn2_w_up, ffn2_w_down, ple_norm, ple_w_gate, ple_w_proj, final_norm, loss_target, m_ffn1_norm, m_ffn1_w_gate, m_ffn1_w_up, m_ffn1_w_down, m_mix_norm, m_lru_w_in, m_lru_conv_w, m_lru_conv_b, m_lru_w_a, m_lru_b_a, m_lru_w_x, m_lru_b_x, m_lru_a_param, m_lru_w_out, m_pool_w, m_pool_b, m_pool_scale, m_ffn2_norm, m_ffn2_w_gate, m_ffn2_w_up, m_ffn2_w_down, m_ple_norm, m_ple_w_gate, m_ple_w_proj, m_final_norm, v_ffn1_norm, v_ffn1_w_gate, v_ffn1_w_up, v_ffn1_w_down, v_mix_norm, v_lru_w_in, v_lru_conv_w, v_lru_conv_b, v_lru_w_a, v_lru_b_a, v_lru_w_x, v_lru_b_x, v_lru_a_param, v_lru_w_out, v_pool_w, v_pool_b, v_pool_scale, v_ffn2_norm, v_ffn2_w_gate, v_ffn2_w_up, v_ffn2_w_down, v_ple_norm, v_ple_w_gate, v_ple_w_proj, v_final_norm):
    W = dict(ffn1_norm=ffn1_norm, ffn1_w_gate=ffn1_w_gate, ffn1_w_up=ffn1_w_up, ffn1_w_down=ffn1_w_down,
             mix_norm=mix_norm, lru_w_in=lru_w_in, lru_conv_w=lru_conv_w, lru_conv_b=lru_conv_b, lru_w_a=lru_w_a,
             lru_b_a=lru_b_a, lru_w_x=lru_w_x, lru_b_x=lru_b_x, lru_a_param=lru_a_param, lru_w_out=lru_w_out,
             pool_w=pool_w, pool_b=pool_b, pool_scale=pool_scale, ffn2_norm=ffn2_norm, ffn2_w_gate=ffn2_w_gate,
             ffn2_w_up=ffn2_w_up, ffn2_w_down=ffn2_w_down, ple_norm=ple_norm, ple_w_gate=ple_w_gate,
             ple_w_proj=ple_w_proj, final_norm=final_norm)
    M = dict(ffn1_norm=m_ffn1_norm, ffn1_w_gate=m_ffn1_w_gate, ffn1_w_up=m_ffn1_w_up, ffn1_w_down=m_ffn1_w_down,
             mix_norm=m_mix_norm, lru_w_in=m_lru_w_in, lru_conv_w=m_lru_conv_w, lru_conv_b=m_lru_conv_b,
             lru_w_a=m_lru_w_a, lru_b_a=m_lru_b_a, lru_w_x=m_lru_w_x, lru_b_x=m_lru_b_x, lru_a_param=m_lru_a_param,
             lru_w_out=m_lru_w_out, pool_w=m_pool_w, pool_b=m_pool_b, pool_scale=m_pool_scale, ffn2_norm=m_ffn2_norm,
             ffn2_w_gate=m_ffn2_w_gate, ffn2_w_up=m_ffn2_w_up, ffn2_w_down=m_ffn2_w_down, ple_norm=m_ple_norm,
             ple_w_gate=m_ple_w_gate, ple_w_proj=m_ple_w_proj, final_norm=m_final_norm)
    V = dict(ffn1_norm=v_ffn1_norm, ffn1_w_gate=v_ffn1_w_gate, ffn1_w_up=v_ffn1_w_up, ffn1_w_down=v_ffn1_w_down,
             mix_norm=v_mix_norm, lru_w_in=v_lru_w_in, lru_conv_w=v_lru_conv_w, lru_conv_b=v_lru_conv_b,
             lru_w_a=v_lru_w_a, lru_b_a=v_lru_b_a, lru_w_x=v_lru_w_x, lru_b_x=v_lru_b_x, lru_a_param=v_lru_a_param,
             lru_w_out=v_lru_w_out, pool_w=v_pool_w, pool_b=v_pool_b, pool_scale=v_pool_scale, ffn2_norm=v_ffn2_norm,
             ffn2_w_gate=v_ffn2_w_gate, ffn2_w_up=v_ffn2_w_up, ffn2_w_down=v_ffn2_w_down, ple_norm=v_ple_norm,
             ple_w_gate=v_ple_w_gate, ple_w_proj=v_ple_w_proj, final_norm=v_final_norm)

    S = x.shape[1]
    my_x, my_y, my_c = _mesh_pos()
    my_chip = 2 * my_x + my_y
    pos = jnp.stack([my_chip, my_c]).astype(jnp.int32)
    n_lru, n_pool = lru_w_in.shape[0], pool_w.shape[0]

    tiny_shapes = [W[n].shape for n in TINY_SHARDED]
    tiny_local = _pack([W[n] for n in TINY_SHARDED], align=2 * 16 * LANES)[None]
    Ws, Ms, Vs = ({n: _stored(n, d[n]) for n in BIG} for d in (W, M, V))
    bufs = {n: _cast_place(_as3(Ws[n]), pos, BF16) for n in BIG}
    bufs["tiny"] = _cast_place(tiny_local, pos, F32)

    def gather_now(name, pieces):
        cargo = _gather_cargo(bufs, pieces)
        bufs.update(zip(cargo.names, _run_cargo(name, cargo)))

    def ffn_pieces(which, layer):
        return [("%s_w_gate" % which, layer), ("%s_w_up" % which, layer), ("%s_w_down" % which, layer)]

    def mixer_pieces(layer):
        if layer % 2 == 0:
            return [("lru_w_in", layer // 2), ("lru_w_out", layer // 2)]
        return [("pool_w", layer // 2)]

    gather_now("gather_first", [("tiny", 0)] + ffn_pieces("ffn1", 0))
    tiny_by_chip = [_unpack(bufs["tiny"][0, k], tiny_shapes) for k in range(N_CHIPS)]
    conv_w_full = jnp.concatenate([tiny_by_chip[k][0] for k in range(N_CHIPS)], axis=-1)
    pool_b_full = jnp.concatenate([tiny_by_chip[k][1] for k in range(N_CHIPS)], axis=-1)
    pool_s_full = jnp.concatenate([tiny_by_chip[k][2] for k in range(N_CHIPS)], axis=-1)
    ngroup = len(POOL_WINDOWS)

    def pool_weights():
        pw5 = bufs["pool_w"].reshape(n_pool, N_CHIPS, ngroup, POOL_GROUP_DIM // N_CHIPS, POOL_GROUP_DIM)
        return pw5.transpose(0, 2, 1, 3, 4).reshape(n_pool, ngroup, POOL_GROUP_DIM, POOL_GROUP_DIM)

    lru_out = lambda: bufs["lru_w_out"].reshape(n_lru, D_RNN, D_MODEL)
    ple_gate = lambda: bufs["ple_w_gate"].reshape(DEPTH, D_MODEL, D_MODEL)
    wbd = [_block_diag_gates(lru_w_a[j], lru_w_x[j]) for j in range(n_lru)]
    row = lambda a: a.reshape(1, -1)

    def ffn_forward(which, h, gamma, layer, pieces):
        cargo = _gather_cargo(bufs, pieces)
        outs, updated = _ffn_fwd(h, gamma, bufs[which + "_w_gate"], bufs[which + "_w_up"], bufs[which + "_w_down"],
                                 layer, cargo)
        bufs.update(zip(cargo.names, updated))
        return outs

    h = x.reshape(S, D_MODEL)
    saved = []
    for i in range(DEPTH):
        j = i // 2
        sv = {"h0": h}
        first_mixer = mixer_pieces(0) if i == 0 else []
        h, sv["xn1"], sv["g1"], sv["u1"] = ffn_forward(
            "ffn1", h, row(ffn1_norm[i]), i,
            first_mixer + ffn_pieces("ffn2", i) + [("ple_w_gate", i), ("ple_w_proj", i)])
        sv["h1"] = h
        if i % 2 == 0:
            h, sv["xn_mix"], sv["z"], sv["hs"], sv["gates"] = _lru_fwd(
                h, row(mix_norm[i]), bufs["lru_w_in"], j, conv_w_full[j], row(lru_conv_b[j]), wbd[j],
                row(lru_b_a[j]), row(lru_b_x[j]), row(lru_a_param[j]), lru_out())
        else:
            h, sv["u"] = _pool_fwd(h, row(mix_norm[i]), pool_weights(), row(pool_b_full[j]), row(pool_s_full[j]), j)
        sv["h2"] = h
        nxt = ffn_pieces("ffn1", i + 1) + mixer_pieces(i + 1) if i + 1 < DEPTH else []
        h, sv["xn2"], sv["g2"], sv["u2"] = ffn_forward("ffn2", h, row(ffn2_norm[i]), i, nxt)
        sv["h3"] = h
        sv["p"] = p[i, 0]
        h = _ple_fwd(h, row(ple_norm[i]), sv["p"], ple_gate(), bufs["ple_w_proj"], i)
        saved.append(sv)

    dh, dgam_final, loss_part = _final(h, row(final_norm), loss_target.reshape(S, D_MODEL))
    win, wout, wpg, wpp, pw = bufs["lru_w_in"], lru_out(), ple_gate(), bufs["ple_w_proj"], pool_weights()

    norm_grads = {n: [None] * DEPTH for n in ("ffn1_norm", "mix_norm", "ffn2_norm", "ple_norm")}
    lru_vec = [None] * n_lru
    pool_vec = [None] * n_pool
    sum8 = lambda a: jnp.sum(a, axis=-2)

    to_siblings, to_chips = [], []
    stacked = {n: None for n in BIG}

    def take_cargo():
        a_items, b_items = list(to_siblings), list(to_chips)
        del to_siblings[:], to_chips[:]
        return _reduce_cargo([it[2] for it in a_items], [it[2] for it in b_items]), a_items, b_items

    def absorb(a_items, b_items, outs):
        for (n, layer, g), landed in zip(a_items, outs[:len(a_items)]):
            all_chunks, own = _presum_with_sibling(g, landed, pos)
            to_chips.append((n, layer, all_chunks, own))
        for (n, layer, _, own), from_chips in zip(b_items, outs[len(a_items):]):
            stacked[n] = _sum_chips(own, from_chips, stacked[n], layer, _as3(Ws[n]).shape, pos)

    def ffn_backward(which, xn, dout, gg, uu, layer, h_in, gamma):
        cargo, a_items, b_items = take_cargo()
        weights = (bufs[which + "_w_gate"], bufs[which + "_w_up"], bufs[which + "_w_down"])
        (dwg, dwu, dwd, slabs), c_outs = _ffn_bwd(xn, dout, gg, uu, *weights, layer, cargo)
        absorb(a_items, b_items, c_outs)
        dh_in, dgam, dwg, dwu, dwd = _ffn_bwd_last(xn, dout, gg, uu, *weights, layer, slabs, h_in, gamma,
                                                   dwg, dwu, dwd)
        to_siblings.extend([(which + "_w_gate", layer, dwg), (which + "_w_up", layer, dwu),
                            (which + "_w_down", layer, dwd)])
        return dh_in, sum8(dgam)

    for i in reversed(range(DEPTH)):
        j = i // 2
        sv = saved[i]
        dh, dw_pg, dw_pp, dgam = _ple_bwd(dh, sv["h3"], row(ple_norm[i]), sv["p"], wpg, wpp, i)
        norm_grads["ple_norm"][i] = sum8(dgam)
        to_siblings.append(("ple_w_gate", i, dw_pg.reshape(N_CHIPS, D_MODEL // N_CHIPS, D_MODEL)))
        to_siblings.append(("ple_w_proj", i, dw_pp))

        dh, norm_grads["ffn2_norm"][i] = ffn_backward("ffn2", sv["xn2"], dh, sv["g2"], sv["u2"], i, sv["h2"],
                                                      row(ffn2_norm[i]))

        if i % 2 == 0:
            dz, dpre, xc_b, y_b, dcw, vec = _lru_bwd_seq(
                dh, sv["z"], sv["hs"], sv["gates"], conv_w_full[j], wbd[j], row(lru_a_param[j]), wout, j)
            to_siblings.append(("lru_w_out", j, _xt_dy("lru_dw_out", y_b, dh, 1, D_RNN, D_MODEL, False, False)
                                .reshape(N_CHIPS, D_RNN // N_CHIPS, D_MODEL)))
            to_siblings.append(("lru_w_in", j, _xt_dy("lru_dw_in", sv["xn_mix"], dz, N_CHIPS, D_MODEL, RNN_IN_CHUNK,
                                                      False, True)))
            dwbd = _xt_dy("lru_dw_gates", xc_b, dpre, D_RNN // GATE_CHUNK, GATE_CHUNK, 2 * GATE_CHUNK, True, True)
            dw_a, dw_x = _block_diag_extract(dwbd)
            vsum = sum8(vec)
            lru_vec[j] = (sum8(dcw), vsum[0], vsum[1], vsum[2], vsum[3], dw_a, dw_x)
            dh, dgam = _lru_bwd_in(dz, sv["h1"], row(mix_norm[i]), dh, win, j)
            norm_grads["mix_norm"][i] = sum8(dgam)
        else:
            dh_new, dpre_b, vec = _pool_bwd(dh, sv["h1"], sv["u"], row(mix_norm[i]), pw, row(pool_b_full[j]),
                                            row(pool_s_full[j]), j)
            dpw = _xt_dy("pool_dw", sv["u"], dpre_b, ngroup, POOL_GROUP_DIM, POOL_GROUP_DIM, True, True)
            dpw = dpw.reshape(ngroup, N_CHIPS, POOL_GROUP_DIM // N_CHIPS, POOL_GROUP_DIM).transpose(1, 0, 2, 3)
            to_siblings.append(("pool_w", j, dpw.reshape(N_CHIPS, POOL_GROUP_DIM, POOL_GROUP_DIM)))
            vsum = sum8(vec)
            norm_grads["mix_norm"][i] = vsum[0]
            pool_vec[j] = (vsum[1], vsum[2])
            dh = dh_new

        dh, norm_grads["ffn1_norm"][i] = ffn_backward("ffn1", sv["xn1"], dh, sv["g1"], sv["u1"], i, sv["h0"],
                                                      row(ffn1_norm[i]))

    grad_x = dh.reshape(1, S, D_MODEL)

    small_parts = [
        jnp.stack(norm_grads["ffn1_norm"]), jnp.stack(norm_grads["mix_norm"]),
        jnp.stack(norm_grads["ffn2_norm"]), jnp.stack(norm_grads["ple_norm"]), sum8(dgam_final),
        jnp.stack([lv[0] for lv in lru_vec]), jnp.stack([lv[1] for lv in lru_vec]),
        jnp.stack([lv[2] for lv in lru_vec]), jnp.stack([lv[3] for lv in lru_vec]),
        jnp.stack([lv[4] for lv in lru_vec]), jnp.stack([lv[5] for lv in lru_vec]),
        jnp.stack([lv[6] for lv in lru_vec]),
        jnp.stack([pv[0] for pv in pool_vec]), jnp.stack([pv[1] for pv in pool_vec]),
        jnp.sum(loss_part).reshape(1),
    ]
    small_names = ("ffn1_norm", "mix_norm", "ffn2_norm", "ple_norm", "final_norm", "lru_conv_w", "lru_conv_b",
                   "lru_b_a", "lru_b_x", "lru_a_param", "lru_w_a", "lru_w_x", "pool_b", "pool_scale", "loss")
    reduced = _unpack(_allreduce_small(_pack(small_parts, align=2 * SUBLANES * LANES)),
                      [sp.shape for sp in small_parts])
    small_grad = dict(zip(small_names, reduced))
    loss = small_grad.pop("loss").reshape(())
    for n in TINY_SHARDED:
        width = W[n].shape[-1]
        small_grad[n] = lax.dynamic_slice_in_dim(small_grad[n], my_chip * width, width, axis=-1)

    tail = 0
    while to_siblings or to_chips:
        cargo, a_items, b_items = take_cargo()
        absorb(a_items, b_items, _run_cargo("grad_exchange_tail%d" % tail, cargo))
        tail += 1
    big_final = dict(zip(BIG, _join_siblings([stacked[n] for n in BIG])))

    grads, deltas, new_m, new_v = {}, {}, {}, {}
    for n in BIG:
        shp = Ws[n].shape
        to2 = lambda a: a.reshape(-1, shp[-1])
        g2, d, m2, v2 = _adamw(to2(Ws[n]), to2(big_final[n]), to2(Ms[n]), to2(Vs[n]))
        grads[n], deltas[n], new_m[n], new_v[n] = (_stored(n, a.reshape(shp)) for a in (g2, d, m2, v2))
    small_order = TINY_SHARDED + REPLICATED
    small_shapes = [W[n].shape for n in small_order]
    pack_rows = functools.partial(_pack, align=512 * LANES)
    _, sd, sm, sv_ = _adamw(pack_rows([W[n] for n in small_order]), pack_rows([small_grad[n] for n in small_order]),
                            pack_rows([M[n] for n in small_order]), pack_rows([V[n] for n in small_order]))
    for n, d, m2, v2 in zip(small_order, _unpack(sd, small_shapes), _unpack(sm, small_shapes),
                            _unpack(sv_, small_shapes)):
        grads[n], deltas[n], new_m[n], new_v[n] = small_grad[n].reshape(W[n].shape), d, m2, v2

    return (loss, grad_x, *[grads[n] for n in WEIGHT_ORDER], *[deltas[n] for n in WEIGHT_ORDER],
            *[new_m[n] for n in WEIGHT_ORDER], *[new_v[n] for n in WEIGHT_ORDER])
```

```python
import functools

import jax
import jax.numpy as jnp
from jax import lax
from jax.experimental import pallas as pl
from jax.experimental.pallas import tpu as pltpu

F32 = jnp.float32
BF16 = jnp.bfloat16

D_MODEL = 1024
D_FF = 2816
D_RNN = 1280
DEPTH = 4
N_CHIPS = 4
FF_CHUNK = D_FF // N_CHIPS
RNN_IN_CHUNK = 2 * D_RNN // N_CHIPS
GATE_CHUNK = 640
N_GATE_PLANES = 5
LRU_HEADS = 16
LRU_HEAD_DIM = 80
CONV_WIDTH = 4
LRU_C = 8.0
POOL_WINDOWS = (2, 4, 8, 16)
POOL_GROUP_DIM = 256
PLE_DIM = 256
RMS_EPS = 1e-6
POOL_HALO = 16
SUBLANES = 8
LANES = 128

ADAM_LR = 0.001
ADAM_B1 = 0.9
ADAM_B2 = 0.999
ADAM_EPS = 1e-08
ADAM_WD = 0.01
ADAM_STEP = 10

VMEM_LIMIT_MB = 56
MESH_ID = pl.DeviceIdType.MESH
CHIP_FLIPS = ((1, 0), (0, 1), (1, 1))


def _cparams(semantics):
    return pltpu.CompilerParams(dimension_semantics=semantics, vmem_limit_bytes=VMEM_LIMIT_MB * 2 ** 20)


def _dot(a, b):
    return lax.dot_general(a, b, (((1,), (0,)), ((), ())), preferred_element_type=F32)


def _dot_nt(a, b):
    return lax.dot_general(a, b, (((1,), (1,)), ((), ())), preferred_element_type=F32)


def _dot_tn(a, b):
    return lax.dot_general(a, b, (((0,), (0,)), ((), ())), preferred_element_type=F32)


def _sigmoid(x):
    return 1.0 / (1.0 + jnp.exp(-x))


def _rms(hf, gamma):
    rstd = lax.rsqrt(jnp.mean(hf * hf, axis=-1, keepdims=True) + RMS_EPS)
    xhat = hf * rstd
    return xhat, rstd, xhat * gamma


def _rms_bwd(xhat, rstd, gamma, dxn):
    dxhat = dxn * gamma
    m = jnp.mean(dxhat * xhat, axis=-1, keepdims=True)
    return rstd * (dxhat - xhat * m), _rowsum8(dxn * xhat)


def _rowsum8(v):
    tm, n = v.shape
    return jnp.sum(v.reshape(tm // SUBLANES, SUBLANES, n), axis=0)


def _gelu(x):
    u = 0.7978845608028654 * (x + 0.044715 * x * x * x)
    return 0.5 * x * (1.0 + jnp.tanh(u))


def _gelu_and_grad(x):
    c = 0.7978845608028654
    u = c * (x + 0.044715 * x * x * x)
    th = jnp.tanh(u)
    g = 0.5 * x * (1.0 + th)
    dg = 0.5 * (1.0 + th) + 0.5 * x * (1.0 - th * th) * c * (1.0 + 3.0 * 0.044715 * x * x)
    return g, dg


def _softplus(z):
    e = jnp.exp(-jnp.abs(z))
    u = 1.0 + e
    log1p = jnp.where(u == 1.0, e, jnp.log(u) * e / jnp.where(u == 1.0, 1.0, u - 1.0))
    return jnp.maximum(z, 0.0) + log1p


def _neg_expm1(x):
    series = -x * (1.0 + x * (0.5 + x * (1.0 / 6.0)))
    return jnp.where(x > -1e-2, series, 1.0 - jnp.exp(x))


def _shift_down(ext, j, halo):
    return pltpu.roll(ext, j, 0)[halo:]


def _shift_up(ext, j, tm):
    n = ext.shape[0]
    return pltpu.roll(ext, n - j, 0)[:tm]


def _scan_causal(a, b):
    tm, n = a.shape
    head_rows = lax.broadcasted_iota(jnp.int32, (SUBLANES, n), 0)
    s = 1
    while s < min(SUBLANES, tm):
        keep = head_rows >= s
        a_r, b_r = pltpu.roll(a, s, 0), pltpu.roll(b, s, 0)
        a_sh = jnp.concatenate([jnp.where(keep, a_r[:SUBLANES], 1.0), a_r[SUBLANES:]], axis=0)
        b_sh = jnp.concatenate([jnp.where(keep, b_r[:SUBLANES], 0.0), b_r[SUBLANES:]], axis=0)
        b = a * b_sh + b
        a = a * a_sh
        s *= 2
    while s < tm:
        b = jnp.concatenate([b[:s], a[s:] * b[:tm - s] + b[s:]], axis=0)
        a = jnp.concatenate([a[:s], a[s:] * a[:tm - s]], axis=0)
        s *= 2
    return a, b


def _scan_anticausal(c, d):
    tm, n = c.shape
    body = tm - SUBLANES
    tail_rows = lax.broadcasted_iota(jnp.int32, (SUBLANES, n), 0) + body
    s = 1
    while s < min(SUBLANES, tm):
        keep = tail_rows < tm - s
        c_r, d_r = pltpu.roll(c, tm - s, 0), pltpu.roll(d, tm - s, 0)
        c_sh = jnp.concatenate([c_r[:body], jnp.where(keep, c_r[body:], 1.0)], axis=0)
        d_sh = jnp.concatenate([d_r[:body], jnp.where(keep, d_r[body:], 0.0)], axis=0)
        d = d + c * d_sh
        c = c * c_sh
        s *= 2
    while s < tm:
        d = jnp.concatenate([d[:tm - s] + c[:tm - s] * d[s:], d[tm - s:]], axis=0)
        c = jnp.concatenate([c[:tm - s] * c[s:], c[tm - s:]], axis=0)
        s *= 2
    return c, d


def _tile(n, want):
    t = min(n, want)
    assert n % t == 0, (n, t)
    return t


def _ffn_fwd(h, gamma, wg, wu, wd, layer, cargo=None):
    S = h.shape[0]
    tm = _tile(S, 1024)
    nt = S // tm
    cargo = cargo or _Cargo()
    n_in, n_out = 5, 4
    nc_in, nc_out = len(cargo.operands), len(cargo.out_shapes)

    def body(*refs):
        h_ref, g_ref, wg_ref, wu_ref, wd_ref = refs[:n_in]
        c_ins = refs[n_in:n_in + nc_in]
        ho_ref, xn_ref, gg_ref, uu_ref = refs[n_in + nc_in:n_in + nc_in + n_out]
        c_outs = refs[n_in + nc_in + n_out:n_in + nc_in + n_out + nc_out]
        xn_s, acc_s = refs[n_in + nc_in + n_out + nc_out:n_in + nc_in + n_out + nc_out + 2]
        sems = refs[n_in + nc_in + n_out + nc_out + 2:]
        t, k = pl.program_id(0), pl.program_id(1)

        @pl.when((t == 0) & (k == 0))
        def _():
            cargo.start(c_ins, c_outs, sems)

        @pl.when((t == nt - 1) & (k == 0))
        def _():
            cargo.forward(c_ins, c_outs, sems)

        @pl.when(k == 0)
        def _():
            _, _, xn = _rms(h_ref[...], g_ref[...])
            xnb = xn.astype(BF16)
            xn_s[...] = xnb
            xn_ref[...] = xnb
            acc_s[...] = jnp.zeros_like(acc_s)

        xnb = xn_s[...]
        g = _dot_nt(xnb, wg_ref[...])
        u = _dot_nt(xnb, wu_ref[...])
        gg_ref[...] = g.astype(BF16)
        uu_ref[...] = u.astype(BF16)
        hid = (g * _sigmoid(g)) * u
        acc_s[...] += _dot(hid.astype(BF16), wd_ref[...])

        @pl.when(k == N_CHIPS - 1)
        def _():
            ho_ref[...] = h_ref[...] + 0.5 * acc_s[...]

        @pl.when((t == nt - 1) & (k == N_CHIPS - 1))
        def _():
            cargo.finish(c_ins, c_outs, sems)

    outs = pl.pallas_call(
        body, name="ffn_fwd",
        grid=(nt, N_CHIPS),
        in_specs=[
            pl.BlockSpec((tm, D_MODEL), lambda t, k: (t, 0)),
            pl.BlockSpec((1, D_MODEL), lambda t, k: (0, 0)),
            pl.BlockSpec((None, None, FF_CHUNK, D_MODEL), lambda t, k: (layer, k, 0, 0)),
            pl.BlockSpec((None, None, FF_CHUNK, D_MODEL), lambda t, k: (layer, k, 0, 0)),
            pl.BlockSpec((None, None, FF_CHUNK, D_MODEL), lambda t, k: (layer, k, 0, 0)),
        ] + _any_specs(nc_in),
        out_specs=[
            pl.BlockSpec((tm, D_MODEL), lambda t, k: (t, 0)),
            pl.BlockSpec((tm, D_MODEL), lambda t, k: (t, 0)),
            pl.BlockSpec((None, tm, FF_CHUNK), lambda t, k: (k, t, 0)),
            pl.BlockSpec((None, tm, FF_CHUNK), lambda t, k: (k, t, 0)),
        ] + _any_specs(nc_out),
        out_shape=[
            jax.ShapeDtypeStruct((S, D_MODEL), F32),
            jax.ShapeDtypeStruct((S, D_MODEL), BF16),
            jax.ShapeDtypeStruct((N_CHIPS, S, FF_CHUNK), BF16),
            jax.ShapeDtypeStruct((N_CHIPS, S, FF_CHUNK), BF16),
        ] + cargo.out_shapes,
        input_output_aliases={n_in + i: n_out + o for i, o in cargo.aliases.items()},
        scratch_shapes=[pltpu.VMEM((tm, D_MODEL), BF16), pltpu.VMEM((tm, D_MODEL), F32)] + cargo.sem_shapes,
        compiler_params=_cparams(("arbitrary", "arbitrary")),
    )(h, gamma, wg, wu, wd, *cargo.operands)
    return outs[:n_out], list(outs[n_out:])


def _ffn_bwd(xn, dout, gg, uu, wg, wu, wd, layer, cargo=None):
    S = xn.shape[0]
    tm = _tile(S, 512)
    nt = S // tm
    nchunk = N_CHIPS - 1
    cargo = cargo or _Cargo()
    n_in, n_out = 7, 4
    nc_in, nc_out = len(cargo.operands), len(cargo.out_shapes)

    def body(*refs):
        xn_ref, do_ref, gg_ref, uu_ref, wg_ref, wu_ref, wd_ref = refs[:n_in]
        c_ins = refs[n_in:n_in + nc_in]
        dwg_ref, dwu_ref, dwd_ref, slab_ref = refs[n_in + nc_in:n_in + nc_in + n_out]
        c_outs = refs[n_in + nc_in + n_out:n_in + nc_in + n_out + nc_out]
        sems = refs[n_in + nc_in + n_out + nc_out:]
        k, t = pl.program_id(0), pl.program_id(1)

        @pl.when((k == 0) & (t == 0))
        def _():
            cargo.start(c_ins, c_outs, sems)

        @pl.when(t == 0)
        def _():
            dwg_ref[...] = jnp.zeros_like(dwg_ref)
            dwu_ref[...] = jnp.zeros_like(dwu_ref)
            dwd_ref[...] = jnp.zeros_like(dwd_ref)

        xnb = xn_ref[...]
        dob = (0.5 * do_ref[...]).astype(BF16)
        g = gg_ref[...].astype(F32)
        u = uu_ref[...].astype(F32)
        s = _sigmoid(g)
        sil = g * s
        dhid = _dot_nt(dob, wd_ref[...])
        dwd_ref[...] += _dot_tn((sil * u).astype(BF16), dob)
        du = (dhid * sil).astype(BF16)
        dg = (dhid * u * (s * (1.0 + g * (1.0 - s)))).astype(BF16)
        dwg_ref[...] += _dot_tn(dg, xnb)
        dwu_ref[...] += _dot_tn(du, xnb)
        slab_ref[...] = (_dot(dg, wg_ref[...]) + _dot(du, wu_ref[...])).astype(BF16)

        @pl.when((k == nchunk - 1) & (t == nt - 1))
        def _():
            cargo.finish(c_ins, c_outs, sems)

    outs = pl.pallas_call(
        body, name="ffn_bwd",
        grid=(nchunk, nt),
        in_specs=[
            pl.BlockSpec((tm, D_MODEL), lambda k, t: (t, 0)),
            pl.BlockSpec((tm, D_MODEL), lambda k, t: (t, 0)),
            pl.BlockSpec((None, tm, FF_CHUNK), lambda k, t: (k, t, 0)),
            pl.BlockSpec((None, tm, FF_CHUNK), lambda k, t: (k, t, 0)),
            pl.BlockSpec((None, None, FF_CHUNK, D_MODEL), lambda k, t: (layer, k, 0, 0)),
            pl.BlockSpec((None, None, FF_CHUNK, D_MODEL), lambda k, t: (layer, k, 0, 0)),
            pl.BlockSpec((None, None, FF_CHUNK, D_MODEL), lambda k, t: (layer, k, 0, 0)),
        ] + _any_specs(nc_in),
        out_specs=[
            pl.BlockSpec((None, FF_CHUNK, D_MODEL), lambda k, t: (k, 0, 0)),
            pl.BlockSpec((None, FF_CHUNK, D_MODEL), lambda k, t: (k, 0, 0)),
            pl.BlockSpec((None, FF_CHUNK, D_MODEL), lambda k, t: (k, 0, 0)),
            pl.BlockSpec((None, tm, D_MODEL), lambda k, t: (k, t, 0)),
        ] + _any_specs(nc_out),
        out_shape=[
            jax.ShapeDtypeStruct((N_CHIPS, FF_CHUNK, D_MODEL), F32),
            jax.ShapeDtypeStruct((N_CHIPS, FF_CHUNK, D_MODEL), F32),
            jax.ShapeDtypeStruct((N_CHIPS, FF_CHUNK, D_MODEL), F32),
            jax.ShapeDtypeStruct((nchunk, S, D_MODEL), BF16),
        ] + cargo.out_shapes,
        input_output_aliases={n_in + i: n_out + o for i, o in cargo.aliases.items()},
        scratch_shapes=list(cargo.sem_shapes),
        compiler_params=_cparams(("arbitrary", "arbitrary")),
    )(xn, dout, gg, uu, wg, wu, wd, *cargo.operands)
    return outs[:n_out], list(outs[n_out:])


def _ffn_bwd_last(xn, dout, gg, uu, wg, wu, wd, layer, slabs, h, gamma, dwg, dwu, dwd):
    S = xn.shape[0]
    tm = _tile(S, 512)
    k = N_CHIPS - 1
    nprev = slabs.shape[0]

    def body(xn_ref, do_ref, gg_ref, uu_ref, wg_ref, wu_ref, wd_ref, slab_ref, h_ref, g_ref, _dwg, _dwu, _dwd,
             dh_ref, dgam_ref, dwg_ref, dwu_ref, dwd_ref):
        @pl.when(pl.program_id(0) == 0)
        def _():
            dgam_ref[...] = jnp.zeros_like(dgam_ref)
            dwg_ref[...] = jnp.zeros_like(dwg_ref)
            dwu_ref[...] = jnp.zeros_like(dwu_ref)
            dwd_ref[...] = jnp.zeros_like(dwd_ref)

        xnb = xn_ref[...]
        do = do_ref[...]
        dob = (0.5 * do).astype(BF16)
        g = gg_ref[...].astype(F32)
        u = uu_ref[...].astype(F32)
        s = _sigmoid(g)
        sil = g * s
        dhid = _dot_nt(dob, wd_ref[...])
        dwd_ref[...] += _dot_tn((sil * u).astype(BF16), dob)
        du = (dhid * sil).astype(BF16)
        dg = (dhid * u * (s * (1.0 + g * (1.0 - s)))).astype(BF16)
        dwg_ref[...] += _dot_tn(dg, xnb)
        dwu_ref[...] += _dot_tn(du, xnb)
        dxn = _dot(dg, wg_ref[...]) + _dot(du, wu_ref[...])
        for i in range(nprev):
            dxn = dxn + slab_ref[i].astype(F32)
        xhat, rstd, _ = _rms(h_ref[...], g_ref[...])
        dhn, dgam = _rms_bwd(xhat, rstd, g_ref[...], dxn)
        dh_ref[...] = do + dhn
        dgam_ref[...] += dgam

    tile = pl.BlockSpec((tm, D_MODEL), lambda t: (t, 0))
    hidden = pl.BlockSpec((None, tm, FF_CHUNK), lambda t: (k, t, 0))
    w_in = pl.BlockSpec((None, None, FF_CHUNK, D_MODEL), lambda t: (layer, k, 0, 0))
    dw_in = pl.BlockSpec((None, FF_CHUNK, D_MODEL), lambda t: (k, 0, 0))
    return pl.pallas_call(
        body, name="ffn_bwd_last",
        grid=(S // tm,),
        in_specs=[tile, tile, hidden, hidden, w_in, w_in,
                  pl.BlockSpec((None, None, FF_CHUNK, D_MODEL), lambda t: (layer, k, 0, 0)),
                  pl.BlockSpec((nprev, tm, D_MODEL), lambda t: (0, t, 0)), tile,
                  pl.BlockSpec((1, D_MODEL), lambda t: (0, 0))] + _any_specs(3),
        out_specs=[tile, pl.BlockSpec((SUBLANES, D_MODEL), lambda t: (0, 0)), dw_in, dw_in,
                   pl.BlockSpec((None, FF_CHUNK, D_MODEL), lambda t: (k, 0, 0))],
        out_shape=[jax.ShapeDtypeStruct((S, D_MODEL), F32), jax.ShapeDtypeStruct((SUBLANES, D_MODEL), F32),
                   jax.ShapeDtypeStruct(dwg.shape, F32), jax.ShapeDtypeStruct(dwu.shape, F32),
                   jax.ShapeDtypeStruct(dwd.shape, F32)],
        input_output_aliases={10: 2, 11: 3, 12: 4},
        compiler_params=_cparams(("arbitrary",)),
    )(xn, dout, gg, uu, wg, wu, wd, slabs, h, gamma, dwg, dwu, dwd)


def _xt_dy(name, x, dy, nchunk, kb, nb, x_by_chunk, y_by_chunk):
    S = x.shape[0]
    tm = _tile(S, 2048)

    def body(x_ref, dy_ref, o_ref):
        @pl.when(pl.program_id(1) == 0)
        def _():
            o_ref[...] = jnp.zeros_like(o_ref)

        o_ref[...] += _dot_tn(x_ref[...].astype(BF16), dy_ref[...].astype(BF16))

    return pl.pallas_call(
        body, name=name,
        grid=(nchunk, S // tm),
        in_specs=[
            pl.BlockSpec((tm, kb), (lambda c, t: (t, c)) if x_by_chunk else (lambda c, t: (t, 0))),
            pl.BlockSpec((tm, nb), (lambda c, t: (t, c)) if y_by_chunk else (lambda c, t: (t, 0))),
        ],
        out_specs=pl.BlockSpec((None, kb, nb), lambda c, t: (c, 0, 0)),
        out_shape=jax.ShapeDtypeStruct((nchunk, kb, nb), F32),
        compiler_params=_cparams(("arbitrary", "arbitrary")),
    )(x, dy)


def _lru_gates(xc, wbd_ref, ba, bx, apar):
    xcb = xc.astype(BF16)
    r_parts, ig_parts = [], []
    for q in range(D_RNN // GATE_CHUNK):
        lo, hi = q * GATE_CHUNK, (q + 1) * GATE_CHUNK
        pre = _dot(xcb[:, lo:hi], wbd_ref[q])
        r_parts.append(_sigmoid(pre[:, :GATE_CHUNK] + ba[:, lo:hi]))
        ig_parts.append(_sigmoid(pre[:, GATE_CHUNK:] + bx[:, lo:hi]))
    r = jnp.concatenate(r_parts, axis=1)
    ig = jnp.concatenate(ig_parts, axis=1)
    sp = LRU_C * _softplus(-apar)
    log_a = -(r * sp)
    a = jnp.exp(log_a)
    mult = jnp.sqrt(_neg_expm1(2.0 * log_a))
    return r, ig, a, mult, sp


def _conv_causal(xb, tail, cw_ref, cb):
    ext = jnp.concatenate([tail, xb], axis=0)
    xc = cb + cw_ref[CONV_WIDTH - 1:CONV_WIDTH, :] * xb
    for j in range(1, CONV_WIDTH):
        xc = xc + cw_ref[CONV_WIDTH - 1 - j:CONV_WIDTH - j, :] * _shift_down(ext, j, SUBLANES)
    return xc, ext


def _lru_fwd(h, gamma, win, layer, convw, convb, wbd, ba, bx, apar, wout):
    S = h.shape[0]
    tm = _tile(S, 256)

    def body(h_ref, g_ref, win_ref, cw_ref, cb_ref, wbd_ref, ba_ref, bx_ref, ap_ref, wout_ref,
             ho_ref, xn_ref, z_ref, hs_ref, gates_ref, tail_s, carry_s):
        @pl.when(pl.program_id(0) == 0)
        def _():
            tail_s[...] = jnp.zeros_like(tail_s)
            carry_s[...] = jnp.zeros_like(carry_s)

        hf = h_ref[...]
        _, _, xn = _rms(hf, g_ref[...])
        xnb = xn.astype(BF16)
        xn_ref[...] = xnb
        for k in range(N_CHIPS):
            z_ref[:, k * RNN_IN_CHUNK:(k + 1) * RNN_IN_CHUNK] = _dot(xnb, win_ref[k])
        gate = z_ref[:, :D_RNN]
        xb = z_ref[:, D_RNN:]
        xc, _ = _conv_causal(xb, tail_s[...], cw_ref, cb_ref[...])
        tail_s[...] = xb[tm - SUBLANES:, :]
        r, ig, a, mult, _ = _lru_gates(xc, wbd_ref, ba_ref[...], bx_ref[...], ap_ref[...])
        for plane, val in enumerate((xc, r, ig, a, mult)):
            gates_ref[plane] = val
        big_a, big_b = _scan_causal(a, mult * (ig * xc))
        hs = big_a * carry_s[SUBLANES - 1:SUBLANES, :] + big_b
        hs_ref[...] = hs
        carry_s[...] = hs[tm - SUBLANES:, :]
        y = hs * _gelu(gate)
        ho_ref[...] = hf + _dot(y.astype(BF16), wout_ref[...])

    row = lambda n: pl.BlockSpec((1, n), lambda t: (0, 0))
    return pl.pallas_call(
        body, name="lru_fwd",
        grid=(S // tm,),
        in_specs=[
            pl.BlockSpec((tm, D_MODEL), lambda t: (t, 0)),
            row(D_MODEL),
            pl.BlockSpec((None, N_CHIPS, D_MODEL, RNN_IN_CHUNK), lambda t: (layer, 0, 0, 0)),
            pl.BlockSpec((CONV_WIDTH, D_RNN), lambda t: (0, 0)),
            row(D_RNN),
            pl.BlockSpec((D_RNN // GATE_CHUNK, GATE_CHUNK, 2 * GATE_CHUNK), lambda t: (0, 0, 0)),
            row(D_RNN), row(D_RNN), row(D_RNN),
            pl.BlockSpec((None, D_RNN, D_MODEL), lambda t: (layer, 0, 0)),
        ],
        out_specs=[
            pl.BlockSpec((tm, D_MODEL), lambda t: (t, 0)),
            pl.BlockSpec((tm, D_MODEL), lambda t: (t, 0)),
            pl.BlockSpec((tm, 2 * D_RNN), lambda t: (t, 0)),
            pl.BlockSpec((tm, D_RNN), lambda t: (t, 0)),
            pl.BlockSpec((N_GATE_PLANES, tm, D_RNN), lambda t: (0, t, 0)),
        ],
        out_shape=[
            jax.ShapeDtypeStruct((S, D_MODEL), F32),
            jax.ShapeDtypeStruct((S, D_MODEL), BF16),
            jax.ShapeDtypeStruct((S, 2 * D_RNN), F32),
            jax.ShapeDtypeStruct((S, D_RNN), F32),
            jax.ShapeDtypeStruct((N_GATE_PLANES, S, D_RNN), F32),
        ],
        scratch_shapes=[pltpu.VMEM((SUBLANES, D_RNN), F32), pltpu.VMEM((SUBLANES, D_RNN), F32)],
        compiler_params=_cparams(("arbitrary",)),
    )(h, gamma, win, convw, convb, wbd, ba, bx, apar, wout)


def _lru_bwd_seq(dout, z, hs, gates, convw, wbd, apar, wout, layer, cargo=None):
    S = dout.shape[0]
    tm = _tile(S, 256)
    nt = S // tm
    per8 = tm // SUBLANES
    rev = lambda i: nt - 1 - i
    prev8 = lambda i: jnp.maximum(rev(i) * per8 - 1, 0)
    cargo = cargo or _Cargo()
    n_in, n_out = 10, 6

    def body(*refs):
        ins, c_ins, outs, c_outs, scratch, sems = _cargo_refs(refs, cargo, n_in, n_out, n_scratch=3)
        do_ref, z_ref, hs_ref, gates_ref, ztail_ref, hstail_ref, cw_ref, wbd_ref, ap_ref, wout_ref = ins
        dz_ref, dpre_ref, xc_ref, y_ref, dcw_ref, vec_ref = outs
        a_first_s, g_first_s, dxc_head_s = scratch
        i = pl.program_id(0)
        first_in_time = rev(i) == 0

        @pl.when(i == 0)
        def _():
            cargo.start(c_ins, c_outs, sems)
            a_first_s[...] = jnp.zeros_like(a_first_s)
            g_first_s[...] = jnp.zeros_like(g_first_s)
            dxc_head_s[...] = jnp.zeros_like(dxc_head_s)
            dcw_ref[...] = jnp.zeros_like(dcw_ref)
            vec_ref[...] = jnp.zeros_like(vec_ref)

        gate = z_ref[:, :D_RNN]
        xb = z_ref[:, D_RNN:]
        hist = jnp.where(first_in_time, 0.0, 1.0)
        xext = jnp.concatenate([ztail_ref[:, D_RNN:] * hist, xb], axis=0)
        xc, r, ig, a, mult = (gates_ref[plane] for plane in range(N_GATE_PLANES))
        sp = LRU_C * _softplus(-ap_ref[...])
        hs = hs_ref[...]
        gel, dgel = _gelu_and_grad(gate)
        y = hs * gel
        y_ref[...] = y.astype(BF16)
        xc_ref[...] = xc.astype(BF16)

        dy = _dot_nt(do_ref[...].astype(BF16), wout_ref[...])
        dhs = dy * gel
        dgate = dy * hs * dgel

        coef = _shift_up(jnp.concatenate([a, a_first_s[...]], axis=0), 1, tm)
        big_c, big_d = _scan_anticausal(coef, dhs)
        g = big_d + big_c * g_first_s[0:1, :]
        g_first_s[...] = g[:SUBLANES, :]
        a_first_s[...] = a[:SUBLANES, :]

        hs_prev = _shift_down(jnp.concatenate([hstail_ref[...] * hist, hs], axis=0), 1, SUBLANES)
        da = g * hs_prev
        dmult = g * ig * xc
        dig = g * mult * xc
        dxc = g * mult * ig
        dlog_a = da * a - dmult * (a * a) / mult
        dr = -(dlog_a * sp)
        dpre_a = dr * r * (1.0 - r)
        dpre_x = dig * ig * (1.0 - ig)
        d_apar = dlog_a * r * (LRU_C * _sigmoid(-ap_ref[...]))

        for q in range(D_RNN // GATE_CHUNK):
            lo, hi = q * GATE_CHUNK, (q + 1) * GATE_CHUNK
            dpre_q = jnp.concatenate([dpre_a[:, lo:hi], dpre_x[:, lo:hi]], axis=1).astype(BF16)
            dpre_ref[:, 2 * lo:2 * hi] = dpre_q
            dxc_q = _dot_nt(dpre_q, wbd_ref[q])
            if q == 0:
                dxc_parts = [dxc_q]
            else:
                dxc_parts.append(dxc_q)
        dxc = dxc + jnp.concatenate(dxc_parts, axis=1)

        dext = jnp.concatenate([dxc, dxc_head_s[...]], axis=0)
        dxb = cw_ref[CONV_WIDTH - 1:CONV_WIDTH, :] * dxc
        for j in range(1, CONV_WIDTH):
            dxb = dxb + cw_ref[CONV_WIDTH - 1 - j:CONV_WIDTH - j, :] * _shift_up(dext, j, tm)
        dxc_head_s[...] = dxc[:SUBLANES, :]
        dz_ref[:, :D_RNN] = dgate.astype(BF16)
        dz_ref[:, D_RNN:] = dxb.astype(BF16)

        dcw_ref[CONV_WIDTH - 1] += _rowsum8(dxc * xb)
        for j in range(1, CONV_WIDTH):
            dcw_ref[CONV_WIDTH - 1 - j] += _rowsum8(dxc * _shift_down(xext, j, SUBLANES))
        vec_ref[0] += _rowsum8(dxc)
        vec_ref[1] += _rowsum8(dpre_a)
        vec_ref[2] += _rowsum8(dpre_x)
        vec_ref[3] += _rowsum8(d_apar)

        @pl.when(i == nt - 1)
        def _():
            cargo.finish(c_ins, c_outs, sems)

    row = lambda n: pl.BlockSpec((1, n), lambda i: (0, 0))
    outs = pl.pallas_call(
        body, name="lru_bwd_seq",
        grid=(nt,),
        in_specs=[
            pl.BlockSpec((tm, D_MODEL), lambda i: (rev(i), 0)),
            pl.BlockSpec((tm, 2 * D_RNN), lambda i: (rev(i), 0)),
            pl.BlockSpec((tm, D_RNN), lambda i: (rev(i), 0)),
            pl.BlockSpec((N_GATE_PLANES, tm, D_RNN), lambda i: (0, rev(i), 0)),
            pl.BlockSpec((SUBLANES, 2 * D_RNN), lambda i: (prev8(i), 0)),
            pl.BlockSpec((SUBLANES, D_RNN), lambda i: (prev8(i), 0)),
            pl.BlockSpec((CONV_WIDTH, D_RNN), lambda i: (0, 0)),
            pl.BlockSpec((D_RNN // GATE_CHUNK, GATE_CHUNK, 2 * GATE_CHUNK), lambda i: (0, 0, 0)),
            row(D_RNN),
            pl.BlockSpec((None, D_RNN, D_MODEL), lambda i: (layer, 0, 0)),
        ] + _any_specs(len(cargo.operands)),
        out_specs=[
            pl.BlockSpec((tm, 2 * D_RNN), lambda i: (rev(i), 0)),
            pl.BlockSpec((tm, 2 * D_RNN), lambda i: (rev(i), 0)),
            pl.BlockSpec((tm, D_RNN), lambda i: (rev(i), 0)),
            pl.BlockSpec((tm, D_RNN), lambda i: (rev(i), 0)),
            pl.BlockSpec((CONV_WIDTH, SUBLANES, D_RNN), lambda i: (0, 0, 0)),
            pl.BlockSpec((4, SUBLANES, D_RNN), lambda i: (0, 0, 0)),
        ] + _any_specs(len(cargo.out_shapes)),
        out_shape=[
            jax.ShapeDtypeStruct((S, 2 * D_RNN), BF16),
            jax.ShapeDtypeStruct((S, 2 * D_RNN), BF16),
            jax.ShapeDtypeStruct((S, D_RNN), BF16),
            jax.ShapeDtypeStruct((S, D_RNN), BF16),
            jax.ShapeDtypeStruct((CONV_WIDTH, SUBLANES, D_RNN), F32),
            jax.ShapeDtypeStruct((4, SUBLANES, D_RNN), F32),
        ] + cargo.out_shapes,
        input_output_aliases={n_in + i: n_out + o for i, o in cargo.aliases.items()},
        scratch_shapes=[pltpu.VMEM((SUBLANES, D_RNN), F32)] * 3 + cargo.sem_shapes,
        compiler_params=_cparams(("arbitrary",)),
    )(dout, z, hs, gates, z, hs, convw, wbd, apar, wout, *cargo.operands)
    return outs[:n_out], list(outs[n_out:])


def _lru_bwd_in(dz, h, gamma, dres, win, layer, cargo=None):
    S = h.shape[0]
    tm = _tile(S, 512)
    nt = S // tm
    cargo = cargo or _Cargo()
    n_in, n_out = 5, 2

    def body(*refs):
        (dz_ref, h_ref, g_ref, dres_ref, win_ref), c_ins, (dh_ref, dgam_ref), c_outs, _, sems = _cargo_refs(
            refs, cargo, n_in, n_out)

        @pl.when(pl.program_id(0) == 0)
        def _():
            cargo.start(c_ins, c_outs, sems)

        dxn = _dot_nt(dz_ref[:, :RNN_IN_CHUNK], win_ref[0])
        for k in range(1, N_CHIPS):
            dxn = dxn + _dot_nt(dz_ref[:, k * RNN_IN_CHUNK:(k + 1) * RNN_IN_CHUNK], win_ref[k])
        xhat, rstd, _ = _rms(h_ref[...], g_ref[...])
        dhn, dgam = _rms_bwd(xhat, rstd, g_ref[...], dxn)
        dh_ref[...] = dres_ref[...] + dhn

        @pl.when(pl.program_id(0) == 0)
        def _():
            dgam_ref[...] = jnp.zeros_like(dgam_ref)

        dgam_ref[...] += dgam

        @pl.when(pl.program_id(0) == nt - 1)
        def _():
            cargo.finish(c_ins, c_outs, sems)

    outs = pl.pallas_call(
        body, name="lru_bwd_in",
        grid=(nt,),
        in_specs=[
            pl.BlockSpec((tm, 2 * D_RNN), lambda t: (t, 0)),
            pl.BlockSpec((tm, D_MODEL), lambda t: (t, 0)),
            pl.BlockSpec((1, D_MODEL), lambda t: (0, 0)),
            pl.BlockSpec((tm, D_MODEL), lambda t: (t, 0)),
            pl.BlockSpec((None, N_CHIPS, D_MODEL, RNN_IN_CHUNK), lambda t: (layer, 0, 0, 0)),
        ] + _any_specs(len(cargo.operands)),
        out_specs=[
            pl.BlockSpec((tm, D_MODEL), lambda t: (t, 0)),
            pl.BlockSpec((SUBLANES, D_MODEL), lambda t: (0, 0)),
        ] + _any_specs(len(cargo.out_shapes)),
        out_shape=[jax.ShapeDtypeStruct((S, D_MODEL), F32),
                   jax.ShapeDtypeStruct((SUBLANES, D_MODEL), F32)] + cargo.out_shapes,
        input_output_aliases={n_in + i: n_out + o for i, o in cargo.aliases.items()},
        scratch_shapes=list(cargo.sem_shapes),
        compiler_params=_cparams(("arbitrary",)),
    )(dz, h, gamma, dres, win, *cargo.operands)
    return outs[:n_out], list(outs[n_out:])


def _pool_inv_count(t_index, tm):
    rows = (lax.broadcasted_iota(jnp.int32, (tm, D_MODEL), 0) + t_index * tm + 1).astype(F32)
    col = lax.broadcasted_iota(jnp.int32, (tm, D_MODEL), 1)
    win = jnp.where(col < POOL_GROUP_DIM, float(POOL_WINDOWS[0]),
                    jnp.where(col < 2 * POOL_GROUP_DIM, float(POOL_WINDOWS[1]),
                              jnp.where(col < 3 * POOL_GROUP_DIM, float(POOL_WINDOWS[2]), float(POOL_WINDOWS[3]))))
    return 1.0 / jnp.minimum(rows, win)


def _window_sums(ext, shift, take):
    gd = POOL_GROUP_DIM
    s2 = ext + shift(ext, 1)
    s4 = s2[:, gd:] + shift(s2[:, gd:], 2)
    s8 = s4[:, gd:] + shift(s4[:, gd:], 4)
    s16 = s8[:, gd:] + shift(s8[:, gd:], 8)
    return jnp.concatenate([take(s2[:, :gd]), take(s4[:, :gd]), take(s8[:, :gd]), take(s16)], axis=1)


def _pool_fwd(h, gamma, pw, pb, pscale, layer):
    S = h.shape[0]
    tm = _tile(S, 512)

    def body(h_ref, g_ref, pw_ref, pb_ref, ps_ref, ho_ref, u_ref, tail_s):
        t = pl.program_id(0)

        @pl.when(t == 0)
        def _():
            tail_s[...] = jnp.zeros_like(tail_s)

        hf = h_ref[...]
        _, _, hn = _rms(hf, g_ref[...])
        ext = jnp.concatenate([tail_s[...], hn], axis=0)
        tail_s[...] = hn[tm - POOL_HALO:, :]
        sums = _window_sums(ext, lambda v, j: pltpu.roll(v, j, 0), lambda v: v[POOL_HALO:])
        ub = (sums * _pool_inv_count(t, tm) - hn).astype(BF16)
        u_ref[...] = ub
        ys = [_dot(ub[:, g * POOL_GROUP_DIM:(g + 1) * POOL_GROUP_DIM], pw_ref[g]) for g in range(len(POOL_WINDOWS))]
        y = jnp.concatenate(ys, axis=1)
        ho_ref[...] = hf + (y + pb_ref[...]) * ps_ref[...]

    row = pl.BlockSpec((1, D_MODEL), lambda t: (0, 0))
    return pl.pallas_call(
        body, name="pool_fwd",
        grid=(S // tm,),
        in_specs=[
            pl.BlockSpec((tm, D_MODEL), lambda t: (t, 0)), row,
            pl.BlockSpec((None, len(POOL_WINDOWS), POOL_GROUP_DIM, POOL_GROUP_DIM), lambda t: (layer, 0, 0, 0)),
            row, row,
        ],
        out_specs=[pl.BlockSpec((tm, D_MODEL), lambda t: (t, 0)), pl.BlockSpec((tm, D_MODEL), lambda t: (t, 0))],
        out_shape=[jax.ShapeDtypeStruct((S, D_MODEL), F32), jax.ShapeDtypeStruct((S, D_MODEL), BF16)],
        scratch_shapes=[pltpu.VMEM((POOL_HALO, D_MODEL), F32)],
        compiler_params=_cparams(("arbitrary",)),
    )(h, gamma, pw, pb, pscale)


def _pool_bwd(dout, h, u, gamma, pw, pb, pscale, layer):
    S = h.shape[0]
    tm = _tile(S, 512)
    nt = S // tm
    rev = lambda i: nt - 1 - i
    ngroup = len(POOL_WINDOWS)

    def body(do_ref, h_ref, u_ref, g_ref, pw_ref, pb_ref, ps_ref, dh_ref, dpre_ref, vec_ref, head_s):
        i = pl.program_id(0)

        @pl.when(i == 0)
        def _():
            head_s[...] = jnp.zeros_like(head_s)
            vec_ref[...] = jnp.zeros_like(vec_ref)

        do = do_ref[...]
        ub = u_ref[...]
        gsl = lambda v, g: v[:, g * POOL_GROUP_DIM:(g + 1) * POOL_GROUP_DIM]
        y = jnp.concatenate([_dot(gsl(ub, g), pw_ref[g]) for g in range(ngroup)], axis=1)
        dpre = do * ps_ref[...]
        dpb = dpre.astype(BF16)
        dpre_ref[...] = dpb
        du = jnp.concatenate([_dot_nt(gsl(dpb, g), pw_ref[g]) for g in range(ngroup)], axis=1)
        v = du * _pool_inv_count(rev(i), tm)
        ext = jnp.concatenate([v, head_s[...]], axis=0)
        head_s[...] = v[:POOL_HALO, :]
        n = tm + POOL_HALO
        dhn = _window_sums(ext, lambda w, j: pltpu.roll(w, n - j, 0), lambda w: w[:tm]) - du
        xhat, rstd, _ = _rms(h_ref[...], g_ref[...])
        dh_in, dgam = _rms_bwd(xhat, rstd, g_ref[...], dhn)
        dh_ref[...] = do + dh_in
        vec_ref[0] += dgam
        vec_ref[1] += _rowsum8(dpre)
        vec_ref[2] += _rowsum8(do * (y + pb_ref[...]))

    row = pl.BlockSpec((1, D_MODEL), lambda i: (0, 0))
    tile = pl.BlockSpec((tm, D_MODEL), lambda i: (rev(i), 0))
    return pl.pallas_call(
        body, name="pool_bwd",
        grid=(nt,),
        in_specs=[tile, tile, tile, row,
                  pl.BlockSpec((None, ngroup, POOL_GROUP_DIM, POOL_GROUP_DIM), lambda i: (layer, 0, 0, 0)), row, row],
        out_specs=[tile, tile, pl.BlockSpec((3, SUBLANES, D_MODEL), lambda i: (0, 0, 0))],
        out_shape=[jax.ShapeDtypeStruct((S, D_MODEL), F32), jax.ShapeDtypeStruct((S, D_MODEL), BF16),
                   jax.ShapeDtypeStruct((3, SUBLANES, D_MODEL), F32)],
        scratch_shapes=[pltpu.VMEM((POOL_HALO, D_MODEL), F32)],
        compiler_params=_cparams(("arbitrary",)),
    )(dout, h, u, gamma, pw, pb, pscale)


def _ple_parts(hf, gamma, p_tile, wgate_ref, wproj_ref):
    xhat, rstd, xn = _rms(hf, gamma)
    xnb = xn.astype(BF16)
    gate = _sigmoid(_dot(xnb, wgate_ref[...]))
    pb = p_tile.astype(BF16)
    proj = jnp.concatenate([_dot(pb, wproj_ref[k]) for k in range(N_CHIPS)], axis=1)
    return xhat, rstd, xnb, pb, gate, proj


def _ple_fwd(h, gamma, p_l, wgate, wproj, layer):
    S = h.shape[0]
    tm = _tile(S, 512)

    def body(h_ref, g_ref, p_ref, wgate_ref, wproj_ref, ho_ref):
        hf = h_ref[...]
        _, _, _, _, gate, proj = _ple_parts(hf, g_ref[...], p_ref[...], wgate_ref, wproj_ref)
        ho_ref[...] = hf + gate * proj

    return pl.pallas_call(
        body, name="ple_fwd",
        grid=(S // tm,),
        in_specs=[
            pl.BlockSpec((tm, D_MODEL), lambda t: (t, 0)),
            pl.BlockSpec((1, D_MODEL), lambda t: (0, 0)),
            pl.BlockSpec((None, None, tm, PLE_DIM), lambda t: (layer, 0, t, 0)),
            pl.BlockSpec((None, D_MODEL, D_MODEL), lambda t: (layer, 0, 0)),
            pl.BlockSpec((None, N_CHIPS, PLE_DIM, PLE_DIM), lambda t: (layer, 0, 0, 0)),
        ],
        out_specs=pl.BlockSpec((tm, D_MODEL), lambda t: (t, 0)),
        out_shape=jax.ShapeDtypeStruct((S, D_MODEL), F32),
        compiler_params=_cparams(("arbitrary",)),
    )(h, gamma, p_l, wgate, wproj)


def _ple_bwd(dout, h, gamma, p_l, wgate, wproj, layer):
    S = h.shape[0]
    tm = _tile(S, 512)

    def body(do_ref, h_ref, g_ref, p_ref, wgate_ref, wproj_ref, dh_ref, dwg_ref, dwp_ref, dgam_ref):
        @pl.when(pl.program_id(0) == 0)
        def _():
            dgam_ref[...] = jnp.zeros_like(dgam_ref)
            dwg_ref[...] = jnp.zeros_like(dwg_ref)
            dwp_ref[...] = jnp.zeros_like(dwp_ref)

        do = do_ref[...]
        xhat, rstd, xnb, pb, gate, proj = _ple_parts(h_ref[...], g_ref[...], p_ref[...], wgate_ref, wproj_ref)
        dproj = (do * gate).astype(BF16)
        dpre = (do * proj * gate * (1.0 - gate)).astype(BF16)
        dwg_ref[...] += _dot_tn(xnb, dpre)
        for k in range(N_CHIPS):
            dwp_ref[k] += _dot_tn(pb, dproj[:, k * PLE_DIM:(k + 1) * PLE_DIM])
        dhn, dgam = _rms_bwd(xhat, rstd, g_ref[...], _dot_nt(dpre, wgate_ref[...]))
        dh_ref[...] = do + dhn
        dgam_ref[...] += dgam

    tile = pl.BlockSpec((tm, D_MODEL), lambda t: (t, 0))
    return pl.pallas_call(
        body, name="ple_bwd",
        grid=(S // tm,),
        in_specs=[
            tile, tile,
            pl.BlockSpec((1, D_MODEL), lambda t: (0, 0)),
            pl.BlockSpec((None, None, tm, PLE_DIM), lambda t: (layer, 0, t, 0)),
            pl.BlockSpec((None, D_MODEL, D_MODEL), lambda t: (layer, 0, 0)),
            pl.BlockSpec((None, N_CHIPS, PLE_DIM, PLE_DIM), lambda t: (layer, 0, 0, 0)),
        ],
        out_specs=[tile, pl.BlockSpec((D_MODEL, D_MODEL), lambda t: (0, 0)),
                   pl.BlockSpec((N_CHIPS, PLE_DIM, PLE_DIM), lambda t: (0, 0, 0)),
                   pl.BlockSpec((SUBLANES, D_MODEL), lambda t: (0, 0))],
        out_shape=[jax.ShapeDtypeStruct((S, D_MODEL), F32), jax.ShapeDtypeStruct((D_MODEL, D_MODEL), F32),
                   jax.ShapeDtypeStruct((N_CHIPS, PLE_DIM, PLE_DIM), F32),
                   jax.ShapeDtypeStruct((SUBLANES, D_MODEL), F32)],
        compiler_params=_cparams(("arbitrary",)),
    )(dout, h, gamma, p_l, wgate, wproj)


def _final(h, gamma, target):
    S = h.shape[0]
    tm = _tile(S, 512)

    def body(h_ref, g_ref, tgt_ref, dh_ref, dgam_ref, loss_ref):
        xhat, rstd, y = _rms(h_ref[...], g_ref[...])
        err = y - tgt_ref[...]
        dy = err * (1.0 / D_MODEL)
        dhn, dgam = _rms_bwd(xhat, rstd, g_ref[...], dy)
        dh_ref[...] = dhn
        sq = _rowsum8(err * err)
        part = sq[:, :LANES]
        for j in range(1, D_MODEL // LANES):
            part = part + sq[:, j * LANES:(j + 1) * LANES]

        @pl.when(pl.program_id(0) == 0)
        def _():
            dgam_ref[...] = jnp.zeros_like(dgam_ref)
            loss_ref[...] = jnp.zeros_like(loss_ref)

        dgam_ref[...] += dgam
        loss_ref[...] += part * (0.5 / D_MODEL)

    tile = pl.BlockSpec((tm, D_MODEL), lambda t: (t, 0))
    return pl.pallas_call(
        body, name="final_loss",
        grid=(S // tm,),
        in_specs=[tile, pl.BlockSpec((1, D_MODEL), lambda t: (0, 0)), tile],
        out_specs=[tile, pl.BlockSpec((SUBLANES, D_MODEL), lambda t: (0, 0)),
                   pl.BlockSpec((SUBLANES, LANES), lambda t: (0, 0))],
        out_shape=[jax.ShapeDtypeStruct((S, D_MODEL), F32), jax.ShapeDtypeStruct((SUBLANES, D_MODEL), F32),
                   jax.ShapeDtypeStruct((SUBLANES, LANES), F32)],
        compiler_params=_cparams(("arbitrary",)),
    )(h, gamma, target)


def _mesh_pos():
    return lax.axis_index("x"), lax.axis_index("y"), lax.axis_index("c")


def _other_chip(x, y, j):
    fx, fy = CHIP_FLIPS[j]
    return (1 - x if fx else x), (1 - y if fy else y)


def _any_specs(n):
    return [pl.BlockSpec(memory_space=pl.ANY)] * n


class _Cargo:
    def __init__(self):
        self.operands, self.out_shapes, self.aliases, self.sem_shapes, self.names = [], [], {}, [], []
        self.start = lambda ins, outs, sems: None
        self.forward = lambda ins, outs, sems: None
        self.finish = lambda ins, outs, sems: None


def _cargo_refs(refs, cargo, n_in, n_out, n_scratch=0):
    a = n_in
    b = a + len(cargo.operands)
    c = b + n_out
    d = c + len(cargo.out_shapes)
    e = d + n_scratch
    return refs[:a], refs[a:b], refs[b:c], refs[c:d], refs[d:e], refs[e:]


def _remote(src, dst, send, recv, device):
    return pltpu.make_async_remote_copy(src_ref=src, dst_ref=dst, send_sem=send, recv_sem=recv,
                                        device_id=device, device_id_type=MESH_ID)


def _gather_cargo(bufs, pieces):
    cargo = _Cargo()
    if not pieces:
        return cargo
    plist = []
    for name, layer in pieces:
        if name not in cargo.names:
            cargo.names.append(name)
            cargo.operands.append(bufs[name])
        plist.append((cargo.names.index(name), layer, bufs[name].shape[2] // 2))
    nflip = len(CHIP_FLIPS)
    cargo.out_shapes = [jax.ShapeDtypeStruct(b.shape, b.dtype) for b in cargo.operands]
    cargo.aliases = {i: i for i in range(len(cargo.operands))}
    cargo.sem_shapes = [pltpu.SemaphoreType.DMA((len(plist) * nflip,))] * 4

    def copies(outs, sems):
        send1, recv1, send2, recv2 = sems
        x, y, c = _mesh_pos()
        k = 2 * x + y

        def blk(p, chip, cc):
            b, layer, hrows = plist[p]
            return outs[b].at[layer, chip, pl.ds(cc * hrows, hrows), :]

        def chip_of(j):
            px, py = _other_chip(x, y, j)
            return 2 * px + py

        def ici(p, j):
            px, py = _other_chip(x, y, j)
            return _remote(blk(p, k, c), blk(p, k, c), send1.at[p * nflip + j], recv1.at[p * nflip + j], (px, py, c))

        def landed(p, j):
            px, py = _other_chip(x, y, j)
            return _remote(blk(p, k, c), blk(p, chip_of(j), c), send1.at[p * nflip + j], recv1.at[p * nflip + j],
                           (px, py, c))

        def d2d(p, j, cc):
            return _remote(blk(p, chip_of(j), cc), blk(p, chip_of(j), cc), send2.at[p * nflip + j],
                           recv2.at[p * nflip + j], (x, y, 1 - c))

        return c, ici, landed, d2d

    def start(ins, outs, sems):
        _, ici, _, _ = copies(outs, sems)
        for p in range(len(plist)):
            for j in range(nflip):
                ici(p, j).start()

    def forward(ins, outs, sems):
        c, _, landed, d2d = copies(outs, sems)
        for j in range(nflip):
            for p in range(len(plist)):
                landed(p, j).wait_recv()
                d2d(p, j, c).start()

    def finish(ins, outs, sems):
        c, ici, _, d2d = copies(outs, sems)
        for p in range(len(plist)):
            for j in range(nflip):
                ici(p, j).wait_send()
                d2d(p, j, c).wait_send()
                d2d(p, j, 1 - c).wait_recv()

    cargo.start, cargo.forward, cargo.finish = start, forward, finish
    return cargo


def _reduce_cargo(grads, presums):
    cargo = _Cargo()
    na, nb = len(grads), len(presums)
    nflip = len(CHIP_FLIPS)
    cargo.operands = list(grads) + list(presums)
    cargo.out_shapes = ([jax.ShapeDtypeStruct((g.shape[0], g.shape[1] // 2, g.shape[2]), g.dtype) for g in grads]
                        + [jax.ShapeDtypeStruct((nflip,) + ps.shape[1:], ps.dtype) for ps in presums])
    cargo.sem_shapes = ([pltpu.SemaphoreType.DMA((na,))] * 2 if na else []) + (
        [pltpu.SemaphoreType.DMA((nb * nflip,))] * 2 if nb else [])

    def copies(ins, outs, sems):
        x, y, c = _mesh_pos()
        out = []
        if na:
            send, recv = sems[0], sems[1]
            for a in range(na):
                hrows = grads[a].shape[1] // 2
                out.append(_remote(ins[a].at[:, pl.ds((1 - c) * hrows, hrows), :], outs[a], send.at[a], recv.at[a],
                                   (x, y, 1 - c)))
        if nb:
            send, recv = sems[-2], sems[-1]
            for b in range(nb):
                for j in range(nflip):
                    px, py = _other_chip(x, y, j)
                    out.append(_remote(ins[na + b].at[2 * px + py], outs[na + b].at[j], send.at[b * nflip + j],
                                       recv.at[b * nflip + j], (px, py, c)))
        return out

    def start(ins, outs, sems):
        for cp in copies(ins, outs, sems):
            cp.start()

    def finish(ins, outs, sems):
        for cp in copies(ins, outs, sems):
            cp.wait()

    cargo.start, cargo.finish = start, finish
    return cargo


def _run_cargo(name, cargo):
    nin, nout = len(cargo.operands), len(cargo.out_shapes)

    def body(*refs):
        ins, outs, sems = refs[:nin], refs[nin:nin + nout], refs[nin + nout:]
        cargo.start(ins, outs, sems)
        cargo.forward(ins, outs, sems)
        cargo.finish(ins, outs, sems)

    return list(pl.pallas_call(
        body, name=name,
        in_specs=_any_specs(nin), out_specs=_any_specs(nout), out_shape=cargo.out_shapes,
        input_output_aliases=dict(cargo.aliases), scratch_shapes=list(cargo.sem_shapes),
    )(*cargo.operands))


def _join_siblings(bufs):
    nb = len(bufs)
    items = [(b, layer) for b, buf in enumerate(bufs) for layer in range(buf.shape[0])]

    def body(*refs):
        outs = refs[nb:2 * nb]
        send, recv = refs[2 * nb:]
        x, y, c = _mesh_pos()

        def half(i, cc):
            b, layer = items[i]
            hrows = bufs[b].shape[1] // 2
            blk = outs[b].at[layer, pl.ds(cc * hrows, hrows), :]
            return _remote(blk, blk, send.at[i], recv.at[i], (x, y, 1 - c))

        for i in range(len(items)):
            half(i, c).start()
        for i in range(len(items)):
            half(i, c).wait_send()
            half(i, 1 - c).wait_recv()

    return list(pl.pallas_call(
        body, name="grad_sibling_join",
        in_specs=_any_specs(nb), out_specs=_any_specs(nb),
        out_shape=[jax.ShapeDtypeStruct(b.shape, b.dtype) for b in bufs],
        input_output_aliases={i: i for i in range(nb)},
        scratch_shapes=[pltpu.SemaphoreType.DMA((len(items),))] * 2,
    )(*bufs))


def _cast_place(w3, pos, dtype):
    L, rows, cols = w3.shape

    def body(pos_ref, w_ref, o_ref):
        o_ref[...] = w_ref[...].astype(dtype)

    return pl.pallas_call(
        body, name="cast_place",
        grid_spec=pltpu.PrefetchScalarGridSpec(
            num_scalar_prefetch=1, grid=(L,),
            in_specs=[pl.BlockSpec((None, rows, cols), lambda l, pos: (l, 0, 0))],
            out_specs=pl.BlockSpec((None, None, rows, cols), lambda l, pos: (l, pos[0], 0, 0))),
        out_shape=jax.ShapeDtypeStruct((L, N_CHIPS, rows, cols), dtype),
        compiler_params=_cparams(("arbitrary",)),
    )(pos, w3)


def _allreduce_small(buf):
    R = buf.shape[0]
    half = R // 2
    assert half % SUBLANES == 0, R

    def body(in_ref, out_ref, land, send, recv):
        x, y, c = _mesh_pos()
        out_ref[...] = in_ref[...]
        cp = _remote(out_ref, land.at[0], send.at[0], recv.at[0], (x, y, 1 - c))
        cp.start()
        cp.wait()
        out_ref[...] = out_ref[...] + land[0]
        along_y, along_x = (x, 1 - y, c), (1 - x, y, c)
        lo, hi = pl.ds(0, half), pl.ds(half, half)
        for stage, (peer_lo, peer_hi) in enumerate(((along_y, along_x), (along_x, along_y))):
            slot = 1 + stage
            cps = [_remote(out_ref.at[lo], land.at[slot, lo], send.at[1 + 2 * stage], recv.at[1 + 2 * stage], peer_lo),
                   _remote(out_ref.at[hi], land.at[slot, hi], send.at[2 + 2 * stage], recv.at[2 + 2 * stage], peer_hi)]
            for cp in cps:
                cp.start()
            for cp in cps:
                cp.wait()
            out_ref[...] = out_ref[...] + land[slot]

    return pl.pallas_call(
        body, name="allreduce_small",
        in_specs=[pl.BlockSpec(memory_space=pltpu.VMEM)],
        out_specs=pl.BlockSpec(memory_space=pltpu.VMEM),
        out_shape=jax.ShapeDtypeStruct((R, LANES), F32),
        scratch_shapes=[pltpu.VMEM((3, R, LANES), F32), pltpu.SemaphoreType.DMA((5,)), pltpu.SemaphoreType.DMA((5,))],
        compiler_params=pltpu.CompilerParams(vmem_limit_bytes=VMEM_LIMIT_MB * 2 ** 20),
    )(buf)


def _presum_with_sibling(grad, landed, pos):
    nchunk, rows, cols = grad.shape
    hrows = rows // 2

    def body(pos_ref, g_ref, l_ref, all_ref, own_ref):
        s = g_ref[...] + l_ref[...]
        all_ref[...] = s.astype(BF16)

        @pl.when(pl.program_id(0) == pos_ref[0])
        def _():
            own_ref[...] = s

    return pl.pallas_call(
        body, name="grad_presum",
        grid_spec=pltpu.PrefetchScalarGridSpec(
            num_scalar_prefetch=1, grid=(nchunk,),
            in_specs=[pl.BlockSpec((None, hrows, cols), lambda k, pos: (k, pos[1], 0)),
                      pl.BlockSpec((None, hrows, cols), lambda k, pos: (k, 0, 0))],
            out_specs=[pl.BlockSpec((None, hrows, cols), lambda k, pos: (k, 0, 0)),
                       pl.BlockSpec((hrows, cols), lambda k, pos: (0, 0))]),
        out_shape=[jax.ShapeDtypeStruct((nchunk, hrows, cols), BF16), jax.ShapeDtypeStruct((hrows, cols), F32)],
        compiler_params=_cparams(("arbitrary",)),
    )(pos, grad, landed)


def _sum_chips(own, landed, stacked, layer, shape3, pos):
    hrows, cols = own.shape

    def body(pos_ref, o_ref, l_ref, *rest):
        s = o_ref[...]
        for j in range(len(CHIP_FLIPS)):
            s = s + l_ref[j].astype(F32)
        rest[-1][...] = s

    in_specs = [pl.BlockSpec((hrows, cols), lambda i, pos: (0, 0)),
                pl.BlockSpec((len(CHIP_FLIPS), hrows, cols), lambda i, pos: (0, 0, 0))]
    args = [pos, own, landed]
    aliases = {}
    if stacked is not None:
        in_specs.append(pl.BlockSpec(memory_space=pl.ANY))
        args.append(stacked)
        aliases = {3: 0}
    return pl.pallas_call(
        body, name="grad_sum_chips",
        grid_spec=pltpu.PrefetchScalarGridSpec(
            num_scalar_prefetch=1, grid=(1,), in_specs=in_specs,
            out_specs=pl.BlockSpec((None, hrows, cols), lambda i, pos: (layer, pos[1], 0))),
        out_shape=jax.ShapeDtypeStruct(shape3, F32),
        input_output_aliases=aliases,
        compiler_params=_cparams(("arbitrary",)),
    )(*args)


def _adamw(w, g, m, v):
    R, C = w.shape
    rb = R
    for cand in (512, 352, 320, 256, 128, 64, 32, 16, 8):
        if R % cand == 0:
            rb = cand
            break
    c1 = 1.0 - ADAM_B1 ** ADAM_STEP
    c2 = 1.0 - ADAM_B2 ** ADAM_STEP

    def body(w_ref, g_ref, m_ref, v_ref, go_ref, d_ref, mo_ref, vo_ref):
        gv = g_ref[...]
        go_ref[...] = gv
        m2 = ADAM_B1 * m_ref[...] + (1.0 - ADAM_B1) * gv
        v2 = ADAM_B2 * v_ref[...] + (1.0 - ADAM_B2) * (gv * gv)
        mo_ref[...] = m2
        vo_ref[...] = v2
        d_ref[...] = -ADAM_LR * ((m2 / c1) / (jnp.sqrt(v2 / c2) + ADAM_EPS) + ADAM_WD * w_ref[...])

    spec = pl.BlockSpec((rb, C), lambda i: (i, 0))
    return pl.pallas_call(
        body, name="adamw",
        grid=(R // rb,),
        in_specs=[spec] * 4, out_specs=[spec] * 4,
        out_shape=[jax.ShapeDtypeStruct((R, C), F32)] * 4,
        compiler_params=_cparams(("arbitrary",)),
    )(w, g, m, v)


def _pack(parts, align=SUBLANES * LANES):
    flat = jnp.concatenate([p.reshape(-1).astype(F32) for p in parts])
    pad = (-flat.shape[0]) % align
    return jnp.pad(flat, (0, pad)).reshape(-1, LANES)


def _unpack(buf, shapes):
    flat = buf.reshape(-1)
    out, off = [], 0
    for shp in shapes:
        size = 1
        for d in shp:
            size *= d
        out.append(flat[off:off + size].reshape(shp))
        off += size
    return out


def _block_diag_gates(w_a, w_x):
    nq = D_RNN // GATE_CHUNK
    hpc = LRU_HEADS // nq
    eye = jnp.eye(hpc, dtype=F32)

    def bd(w):
        wq = w.reshape(nq, hpc, LRU_HEAD_DIM, LRU_HEAD_DIM)
        return (wq[:, :, :, None, :] * eye[None, :, None, :, None]).reshape(nq, GATE_CHUNK, GATE_CHUNK)

    return jnp.concatenate([bd(w_a), bd(w_x)], axis=2).astype(BF16)


def _block_diag_extract(dwbd):
    nq = D_RNN // GATE_CHUNK
    hpc = LRU_HEADS // nq
    eye = jnp.eye(hpc, dtype=F32)

    def ex(d):
        d5 = d.reshape(nq, hpc, LRU_HEAD_DIM, hpc, LRU_HEAD_DIM)
        return jnp.sum(d5 * eye[None, :, None, :, None], axis=3).reshape(LRU_HEADS, LRU_HEAD_DIM, LRU_HEAD_DIM)

    return ex(dwbd[:, :, :GATE_CHUNK]), ex(dwbd[:, :, GATE_CHUNK:])


BIG = ("ffn1_w_gate", "ffn1_w_up", "ffn1_w_down", "lru_w_in", "lru_w_out", "pool_w",
       "ffn2_w_gate", "ffn2_w_up", "ffn2_w_down", "ple_w_gate", "ple_w_proj")
TINY_SHARDED = ("lru_conv_w", "pool_b", "pool_scale")
REPLICATED = ("ffn1_norm", "mix_norm", "lru_conv_b", "lru_w_a", "lru_b_a", "lru_w_x", "lru_b_x", "lru_a_param",
              "ffn2_norm", "ple_norm", "final_norm")
WEIGHT_ORDER = ("ffn1_norm", "ffn1_w_gate", "ffn1_w_up", "ffn1_w_down", "mix_norm", "lru_w_in", "lru_conv_w",
                "lru_conv_b", "lru_w_a", "lru_b_a", "lru_w_x", "lru_b_x", "lru_a_param", "lru_w_out", "pool_w",
                "pool_b", "pool_scale", "ffn2_norm", "ffn2_w_gate", "ffn2_w_up", "ffn2_w_down", "ple_norm",
                "ple_w_gate", "ple_w_proj", "final_norm")


TRANSPOSED = ("ffn1_w_gate", "ffn1_w_up", "ffn2_w_gate", "ffn2_w_up")


def _stored(name, a):
    return jnp.swapaxes(a, 1, 2) if name in TRANSPOSED else a


def _as3(a):
    return a.reshape(a.shape[0], -1, a.shape[-1])


def kernel(x, p, ffn1_norm, ffn1_w_gate, ffn1_w_up, ffn1_w_down, mix_norm, lru_w_in, lru_conv_w, lru_conv_b, lru_w_a, lru_b_a, lru_w_x, lru_b_x, lru_a_param, lru_w_out, pool_w, pool_b, pool_scale, ffn2_norm, ffn2_w_gate, ffn2_w_up, ffn2_w_down, ple_norm, ple_w_gate, ple_w_proj, final_norm, loss_target, m_ffn1_norm, m_ffn1_w_gate, m_ffn1_w_up, m_ffn1_w_down, m_mix_norm, m_lru_w_in, m_lru_conv_w, m_lru_conv_b, m_lru_w_a, m_lru_b_a, m_lru_w_x, m_lru_b_x, m_lru_a_param, m_lru_w_out, m_pool_w, m_pool_b, m_pool_scale, m_ffn2_norm, m_ffn2_w_gate, m_ffn2_w_up, m_ffn2_w_down, m_ple_norm, m_ple_w_gate, m_ple_w_proj, m_final_norm, v_ffn1_norm, v_ffn1_w_gate, v_ffn1_w_up, v_ffn1_w_down, v_mix_norm, v_lru_w_in, v_lru_conv_w, v_lru_conv_b, v_lru_w_a, v_lru_b_a, v_lru_w_x, v_lru_b_x, v_lru_a_param, v_lru_w_out, v_pool_w, v_pool_b, v_pool_scale, v_ffn2_norm, v_ffn2_w_gate, v_ffn2_w_up, v_ffn2_w_down, v_ple_norm, v_ple_w_gate, v_ple_w_proj, v_final_norm):
    W = dict(ffn1_norm=ffn1_norm, ffn1_w_gate=ffn1_w_gate, ffn1_w_up=ffn1_w_up, ffn1_w_down=ffn1_w_down,
             mix_norm=mix_norm, lru_w_in=lru_w_in, lru_conv_w=lru_conv_w, lru_conv_b=lru_conv_b, lru_w_a=lru_w_a,
             lru_b_a=lru_b_a, lru_w_x=lru_w_x, lru_b_x=lru_b_x, lru_a_param=lru_a_param, lru_w_out=lru_w_out,
             pool_w=pool_w, pool_b=pool_b, pool_scale=pool_scale, ffn2_norm=ffn2_norm, ffn2_w_gate=ffn2_w_gate,
             ffn2_w_up=ffn2_w_up, ffn2_w_down=ffn2_w_down, ple_norm=ple_norm, ple_w_gate=ple_w_gate,
             ple_w_proj=ple_w_proj, final_norm=final_norm)
    M = dict(ffn1_norm=m_ffn1_norm, ffn1_w_gate=m_ffn1_w_gate, ffn1_w_up=m_ffn1_w_up, ffn1_w_down=m_ffn1_w_down,
             mix_norm=m_mix_norm, lru_w_in=m_lru_w_in, lru_conv_w=m_lru_conv_w, lru_conv_b=m_lru_conv_b,
             lru_w_a=m_lru_w_a, lru_b_a=m_lru_b_a, lru_w_x=m_lru_w_x, lru_b_x=m_lru_b_x, lru_a_param=m_lru_a_param,
             lru_w_out=m_lru_w_out, pool_w=m_pool_w, pool_b=m_pool_b, pool_scale=m_pool_scale, ffn2_norm=m_ffn2_norm,
             ffn2_w_gate=m_ffn2_w_gate, ffn2_w_up=m_ffn2_w_up, ffn2_w_down=m_ffn2_w_down, ple_norm=m_ple_norm,
             ple_w_gate=m_ple_w_gate, ple_w_proj=m_ple_w_proj, final_norm=m_final_norm)
    V = dict(ffn1_norm=v_ffn1_norm, ffn1_w_gate=v_ffn1_w_gate, ffn1_w_up=v_ffn1_w_up, ffn1_w_down=v_ffn1_w_down,
             mix_norm=v_mix_norm, lru_w_in=v_lru_w_in, lru_conv_w=v_lru_conv_w, lru_conv_b=v_lru_conv_b,
             lru_w_a=v_lru_w_a, lru_b_a=v_lru_b_a, lru_w_x=v_lru_w_x, lru_b_x=v_lru_b_x, lru_a_param=v_lru_a_param,
             lru_w_out=v_lru_w_out, pool_w=v_pool_w, pool_b=v_pool_b, pool_scale=v_pool_scale, ffn2_norm=v_ffn2_norm,
             ffn2_w_gate=v_ffn2_w_gate, ffn2_w_up=v_ffn2_w_up, ffn2_w_down=v_ffn2_w_down, ple_norm=v_ple_norm,
             ple_w_gate=v_ple_w_gate, ple_w_proj=v_ple_w_proj, final_norm=v_final_norm)

    S = x.shape[1]
    my_x, my_y, my_c = _mesh_pos()
    my_chip = 2 * my_x + my_y
    pos = jnp.stack([my_chip, my_c]).astype(jnp.int32)
    n_lru, n_pool = lru_w_in.shape[0], pool_w.shape[0]

    tiny_shapes = [W[n].shape for n in TINY_SHARDED]
    tiny_local = _pack([W[n] for n in TINY_SHARDED], align=2 * 16 * LANES)[None]
    Ws, Ms, Vs = ({n: _stored(n, d[n]) for n in BIG} for d in (W, M, V))
    bufs = {n: _cast_place(_as3(Ws[n]), pos, BF16) for n in BIG}
    bufs["tiny"] = _cast_place(tiny_local, pos, F32)

    def gather_now(name, pieces):
        cargo = _gather_cargo(bufs, pieces)
        bufs.update(zip(cargo.names, _run_cargo(name, cargo)))

    def ffn_pieces(which, layer):
        return [("%s_w_gate" % which, layer), ("%s_w_up" % which, layer), ("%s_w_down" % which, layer)]

    def mixer_pieces(layer):
        if layer % 2 == 0:
            return [("lru_w_in", layer // 2), ("lru_w_out", layer // 2)]
        return [("pool_w", layer // 2)]

    gather_now("gather_first", [("tiny", 0)] + ffn_pieces("ffn1", 0))
    tiny_by_chip = [_unpack(bufs["tiny"][0, k], tiny_shapes) for k in range(N_CHIPS)]
    conv_w_full = jnp.concatenate([tiny_by_chip[k][0] for k in range(N_CHIPS)], axis=-1)
    pool_b_full = jnp.concatenate([tiny_by_chip[k][1] for k in range(N_CHIPS)], axis=-1)
    pool_s_full = jnp.concatenate([tiny_by_chip[k][2] for k in range(N_CHIPS)], axis=-1)
    ngroup = len(POOL_WINDOWS)

    def pool_weights():
        pw5 = bufs["pool_w"].reshape(n_pool, N_CHIPS, ngroup, POOL_GROUP_DIM // N_CHIPS, POOL_GROUP_DIM)
        return pw5.transpose(0, 2, 1, 3, 4).reshape(n_pool, ngroup, POOL_GROUP_DIM, POOL_GROUP_DIM)

    lru_out = lambda: bufs["lru_w_out"].reshape(n_lru, D_RNN, D_MODEL)
    ple_gate = lambda: bufs["ple_w_gate"].reshape(DEPTH, D_MODEL, D_MODEL)
    wbd = [_block_diag_gates(lru_w_a[j], lru_w_x[j]) for j in range(n_lru)]
    row = lambda a: a.reshape(1, -1)

    def ffn_forward(which, h, gamma, layer, pieces):
        cargo = _gather_cargo(bufs, pieces)
        outs, updated = _ffn_fwd(h, gamma, bufs[which + "_w_gate"], bufs[which + "_w_up"], bufs[which + "_w_down"],
                                 layer, cargo)
        bufs.update(zip(cargo.names, updated))
        return outs

    h = x.reshape(S, D_MODEL)
    saved = []
    for i in range(DEPTH):
        j = i // 2
        sv = {"h0": h}
        first_mixer = mixer_pieces(0) if i == 0 else []
        h, sv["xn1"], sv["g1"], sv["u1"] = ffn_forward(
            "ffn1", h, row(ffn1_norm[i]), i,
            first_mixer + ffn_pieces("ffn2", i) + [("ple_w_gate", i), ("ple_w_proj", i)])
        sv["h1"] = h
        if i % 2 == 0:
            h, sv["xn_mix"], sv["z"], sv["hs"], sv["gates"] = _lru_fwd(
                h, row(mix_norm[i]), bufs["lru_w_in"], j, conv_w_full[j], row(lru_conv_b[j]), wbd[j],
                row(lru_b_a[j]), row(lru_b_x[j]), row(lru_a_param[j]), lru_out())
        else:
            h, sv["u"] = _pool_fwd(h, row(mix_norm[i]), pool_weights(), row(pool_b_full[j]), row(pool_s_full[j]), j)
        sv["h2"] = h
        nxt = ffn_pieces("ffn1", i + 1) + mixer_pieces(i + 1) if i + 1 < DEPTH else []
        h, sv["xn2"], sv["g2"], sv["u2"] = ffn_forward("ffn2", h, row(ffn2_norm[i]), i, nxt)
        sv["h3"] = h
        h = _ple_fwd(h, row(ple_norm[i]), p, ple_gate(), bufs["ple_w_proj"], i)
        saved.append(sv)

    dh, dgam_final, loss_part = _final(h, row(final_norm), loss_target.reshape(S, D_MODEL))
    win, wout, wpg, wpp, pw = bufs["lru_w_in"], lru_out(), ple_gate(), bufs["ple_w_proj"], pool_weights()

    norm_grads = {n: [None] * DEPTH for n in ("ffn1_norm", "mix_norm", "ffn2_norm", "ple_norm")}
    lru_vec = [None] * n_lru
    pool_vec = [None] * n_pool
    sum8 = lambda a: jnp.sum(a, axis=-2)

    to_siblings, to_chips = [], []
    stacked = {n: None for n in BIG}

    def take_cargo(with_chips=True):
        a_items, b_items = list(to_siblings), list(to_chips) if with_chips else []
        del to_siblings[:], to_chips[:len(b_items)]
        return _reduce_cargo([it[2] for it in a_items], [it[2] for it in b_items]), a_items, b_items

    def absorb(a_items, b_items, outs):
        for (n, layer, g), landed in zip(a_items, outs[:len(a_items)]):
            all_chunks, own = _presum_with_sibling(g, landed, pos)
            to_chips.append((n, layer, all_chunks, own))
        for (n, layer, _, own), from_chips in zip(b_items, outs[len(a_items):]):
            stacked[n] = _sum_chips(own, from_chips, stacked[n], layer, _as3(Ws[n]).shape, pos)

    def ffn_backward(which, xn, dout, gg, uu, layer, h_in, gamma):
        cargo, a_items, b_items = take_cargo()
        weights = (bufs[which + "_w_gate"], bufs[which + "_w_up"], bufs[which + "_w_down"])
        (dwg, dwu, dwd, slabs), c_outs = _ffn_bwd(xn, dout, gg, uu, *weights, layer, cargo)
        absorb(a_items, b_items, c_outs)
        dh_in, dgam, dwg, dwu, dwd = _ffn_bwd_last(xn, dout, gg, uu, *weights, layer, slabs, h_in, gamma,
                                                   dwg, dwu, dwd)
        to_siblings.extend([(which + "_w_gate", layer, dwg), (which + "_w_up", layer, dwu),
                            (which + "_w_down", layer, dwd)])
        return dh_in, sum8(dgam)

    for i in reversed(range(DEPTH)):
        j = i // 2
        sv = saved[i]
        dh, dw_pg, dw_pp, dgam = _ple_bwd(dh, sv["h3"], row(ple_norm[i]), p, wpg, wpp, i)
        norm_grads["ple_norm"][i] = sum8(dgam)
        to_siblings.append(("ple_w_gate", i, dw_pg.reshape(N_CHIPS, D_MODEL // N_CHIPS, D_MODEL)))
        to_siblings.append(("ple_w_proj", i, dw_pp))

        dh, norm_grads["ffn2_norm"][i] = ffn_backward("ffn2", sv["xn2"], dh, sv["g2"], sv["u2"], i, sv["h2"],
                                                      row(ffn2_norm[i]))

        if i % 2 == 0:
            cargo, a_items, b_items = take_cargo()
            (dz, dpre, xc_b, y_b, dcw, vec), c_outs = _lru_bwd_seq(
                dh, sv["z"], sv["hs"], sv["gates"], conv_w_full[j], wbd[j], row(lru_a_param[j]), wout, j, cargo)
            absorb(a_items, b_items, c_outs)
            to_siblings.append(("lru_w_out", j, _xt_dy("lru_dw_out", y_b, dh, 1, D_RNN, D_MODEL, False, False)
                                .reshape(N_CHIPS, D_RNN // N_CHIPS, D_MODEL)))
            to_siblings.append(("lru_w_in", j, _xt_dy("lru_dw_in", sv["xn_mix"], dz, N_CHIPS, D_MODEL, RNN_IN_CHUNK,
                                                      False, True)))
            dwbd = _xt_dy("lru_dw_gates", xc_b, dpre, D_RNN // GATE_CHUNK, GATE_CHUNK, 2 * GATE_CHUNK, True, True)
            dw_a, dw_x = _block_diag_extract(dwbd)
            vsum = sum8(vec)
            lru_vec[j] = (sum8(dcw), vsum[0], vsum[1], vsum[2], vsum[3], dw_a, dw_x)
            cargo, a_items, b_items = take_cargo(with_chips=False)
            (dh, dgam), c_outs = _lru_bwd_in(dz, sv["h1"], row(mix_norm[i]), dh, win, j, cargo)
            absorb(a_items, b_items, c_outs)
            norm_grads["mix_norm"][i] = sum8(dgam)
        else:
            dh_new, dpre_b, vec = _pool_bwd(dh, sv["h1"], sv["u"], row(mix_norm[i]), pw, row(pool_b_full[j]),
                                            row(pool_s_full[j]), j)
            dpw = _xt_dy("pool_dw", sv["u"], dpre_b, ngroup, POOL_GROUP_DIM, POOL_GROUP_DIM, True, True)
            dpw = dpw.reshape(ngroup, N_CHIPS, POOL_GROUP_DIM // N_CHIPS, POOL_GROUP_DIM).transpose(1, 0, 2, 3)
            to_siblings.append(("pool_w", j, dpw.reshape(N_CHIPS, POOL_GROUP_DIM, POOL_GROUP_DIM)))
            vsum = sum8(vec)
            norm_grads["mix_norm"][i] = vsum[0]
            pool_vec[j] = (vsum[1], vsum[2])
            dh = dh_new

        dh, norm_grads["ffn1_norm"][i] = ffn_backward("ffn1", sv["xn1"], dh, sv["g1"], sv["u1"], i, sv["h0"],
                                                      row(ffn1_norm[i]))

    grad_x = dh.reshape(1, S, D_MODEL)

    small_parts = [
        jnp.stack(norm_grads["ffn1_norm"]), jnp.stack(norm_grads["mix_norm"]),
        jnp.stack(norm_grads["ffn2_norm"]), jnp.stack(norm_grads["ple_norm"]), sum8(dgam_final),
        jnp.stack([lv[0] for lv in lru_vec]), jnp.stack([lv[1] for lv in lru_vec]),
        jnp.stack([lv[2] for lv in lru_vec]), jnp.stack([lv[3] for lv in lru_vec]),
        jnp.stack([lv[4] for lv in lru_vec]), jnp.stack([lv[5] for lv in lru_vec]),
        jnp.stack([lv[6] for lv in lru_vec]),
        jnp.stack([pv[0] for pv in pool_vec]), jnp.stack([pv[1] for pv in pool_vec]),
        jnp.sum(loss_part).reshape(1),
    ]
    small_names = ("ffn1_norm", "mix_norm", "ffn2_norm", "ple_norm", "final_norm", "lru_conv_w", "lru_conv_b",
                   "lru_b_a", "lru_b_x", "lru_a_param", "lru_w_a", "lru_w_x", "pool_b", "pool_scale", "loss")
    reduced = _unpack(_allreduce_small(_pack(small_parts, align=2 * SUBLANES * LANES)),
                      [sp.shape for sp in small_parts])
    small_grad = dict(zip(small_names, reduced))
    loss = small_grad.pop("loss").reshape(())
    for n in TINY_SHARDED:
        width = W[n].shape[-1]
        small_grad[n] = lax.dynamic_slice_in_dim(small_grad[n], my_chip * width, width, axis=-1)

    tail = 0
    while to_siblings or to_chips:
        cargo, a_items, b_items = take_cargo()
        absorb(a_items, b_items, _run_cargo("grad_exchange_tail%d" % tail, cargo))
        tail += 1
    big_final = dict(zip(BIG, _join_siblings([stacked[n] for n in BIG])))

    grads, deltas, new_m, new_v = {}, {}, {}, {}
    for n in BIG:
        shp = Ws[n].shape
        to2 = lambda a: a.reshape(-1, shp[-1])
        g2, d, m2, v2 = _adamw(to2(Ws[n]), to2(big_final[n]), to2(Ms[n]), to2(Vs[n]))
        grads[n], deltas[n], new_m[n], new_v[n] = (_stored(n, a.reshape(shp)) for a in (g2, d, m2, v2))
    small_order = TINY_SHARDED + REPLICATED
    small_shapes = [W[n].shape for n in small_order]
    pack_rows = functools.partial(_pack, align=512 * LANES)
    _, sd, sm, sv_ = _adamw(pack_rows([W[n] for n in small_order]), pack_rows([small_grad[n] for n in small_order]),
                            pack_rows([M[n] for n in small_order]), pack_rows([V[n] for n in small_order]))
    for n, d, m2, v2 in zip(small_order, _unpack(sd, small_shapes), _unpack(sm, small_shapes),
                            _unpack(sv_, small_shapes)):
        grads[n], deltas[n], new_m[n], new_v[n] = small_grad[n].reshape(W[n].shape), d, m2, v2

    return (loss, grad_x, *[grads[n] for n in WEIGHT_ORDER], *[deltas[n] for n in WEIGHT_ORDER],
            *[new_m[n] for n in WEIGHT_ORDER], *[new_v[n] for n in WEIGHT_ORDER])
```

```python
import functools

import jax
import jax.numpy as jnp
from jax import lax
from jax.experimental import pallas as pl
from jax.experimental.pallas import tpu as pltpu

F32 = jnp.float32
BF16 = jnp.bfloat16

D_MODEL = 1024
D_FF = 2816
D_RNN = 1280
DEPTH = 4
N_CHIPS = 4
FF_CHUNK = D_FF // N_CHIPS
RNN_IN_CHUNK = 2 * D_RNN // N_CHIPS
GATE_CHUNK = 640
N_GATE_PLANES = 5
LRU_HEADS = 16
LRU_HEAD_DIM = 80
CONV_WIDTH = 4
LRU_C = 8.0
POOL_WINDOWS = (2, 4, 8, 16)
POOL_GROUP_DIM = 256
PLE_DIM = 256
RMS_EPS = 1e-6
POOL_HALO = 16
SUBLANES = 8
LANES = 128

ADAM_LR = 0.001
ADAM_B1 = 0.9
ADAM_B2 = 0.999
ADAM_EPS = 1e-08
ADAM_WD = 0.01
ADAM_STEP = 10

VMEM_LIMIT_MB = 56
MESH_ID = pl.DeviceIdType.MESH
CHIP_FLIPS = ((1, 0), (0, 1), (1, 1))


def _cparams(semantics):
    return pltpu.CompilerParams(dimension_semantics=semantics, vmem_limit_bytes=VMEM_LIMIT_MB * 2 ** 20)


def _dot(a, b):
    return lax.dot_general(a, b, (((1,), (0,)), ((), ())), preferred_element_type=F32)


def _dot_nt(a, b):
    return lax.dot_general(a, b, (((1,), (1,)), ((), ())), preferred_element_type=F32)


def _dot_tn(a, b):
    return lax.dot_general(a, b, (((0,), (0,)), ((), ())), preferred_element_type=F32)


def _sigmoid(x):
    return 1.0 / (1.0 + jnp.exp(-x))


def _rms(hf, gamma):
    rstd = lax.rsqrt(jnp.mean(hf * hf, axis=-1, keepdims=True) + RMS_EPS)
    xhat = hf * rstd
    return xhat, rstd, xhat * gamma


def _rms_bwd(xhat, rstd, gamma, dxn):
    dxhat = dxn * gamma
    m = jnp.mean(dxhat * xhat, axis=-1, keepdims=True)
    return rstd * (dxhat - xhat * m), _rowsum8(dxn * xhat)


def _rowsum8(v):
    tm, n = v.shape
    return jnp.sum(v.reshape(tm // SUBLANES, SUBLANES, n), axis=0)


def _gelu(x):
    u = 0.7978845608028654 * (x + 0.044715 * x * x * x)
    return 0.5 * x * (1.0 + jnp.tanh(u))


def _gelu_and_grad(x):
    c = 0.7978845608028654
    u = c * (x + 0.044715 * x * x * x)
    th = jnp.tanh(u)
    g = 0.5 * x * (1.0 + th)
    dg = 0.5 * (1.0 + th) + 0.5 * x * (1.0 - th * th) * c * (1.0 + 3.0 * 0.044715 * x * x)
    return g, dg


def _softplus(z):
    e = jnp.exp(-jnp.abs(z))
    u = 1.0 + e
    log1p = jnp.where(u == 1.0, e, jnp.log(u) * e / jnp.where(u == 1.0, 1.0, u - 1.0))
    return jnp.maximum(z, 0.0) + log1p


def _neg_expm1(x):
    series = -x * (1.0 + x * (0.5 + x * (1.0 / 6.0)))
    return jnp.where(x > -1e-2, series, 1.0 - jnp.exp(x))


def _shift_down(ext, j, halo):
    return pltpu.roll(ext, j, 0)[halo:]


def _shift_up(ext, j, tm):
    n = ext.shape[0]
    return pltpu.roll(ext, n - j, 0)[:tm]


def _scan_causal(a, b):
    tm, n = a.shape
    head_rows = lax.broadcasted_iota(jnp.int32, (SUBLANES, n), 0)
    s = 1
    while s < min(SUBLANES, tm):
        keep = head_rows >= s
        a_r, b_r = pltpu.roll(a, s, 0), pltpu.roll(b, s, 0)
        a_sh = jnp.concatenate([jnp.where(keep, a_r[:SUBLANES], 1.0), a_r[SUBLANES:]], axis=0)
        b_sh = jnp.concatenate([jnp.where(keep, b_r[:SUBLANES], 0.0), b_r[SUBLANES:]], axis=0)
        b = a * b_sh + b
        a = a * a_sh
        s *= 2
    while s < tm:
        b = jnp.concatenate([b[:s], a[s:] * b[:tm - s] + b[s:]], axis=0)
        a = jnp.concatenate([a[:s], a[s:] * a[:tm - s]], axis=0)
        s *= 2
    return a, b


def _scan_anticausal(c, d):
    tm, n = c.shape
    body = tm - SUBLANES
    tail_rows = lax.broadcasted_iota(jnp.int32, (SUBLANES, n), 0) + body
    s = 1
    while s < min(SUBLANES, tm):
        keep = tail_rows < tm - s
        c_r, d_r = pltpu.roll(c, tm - s, 0), pltpu.roll(d, tm - s, 0)
        c_sh = jnp.concatenate([c_r[:body], jnp.where(keep, c_r[body:], 1.0)], axis=0)
        d_sh = jnp.concatenate([d_r[:body], jnp.where(keep, d_r[body:], 0.0)], axis=0)
        d = d + c * d_sh
        c = c * c_sh
        s *= 2
    while s < tm:
        d = jnp.concatenate([d[:tm - s] + c[:tm - s] * d[s:], d[tm - s:]], axis=0)
        c = jnp.concatenate([c[:tm - s] * c[s:], c[tm - s:]], axis=0)
        s *= 2
    return c, d


class _Rows:
    def __init__(self, stacked, index):
        self.array = stacked if stacked.ndim == 3 else stacked.reshape(stacked.shape[0], 1, stacked.shape[-1])
        self.index = index

    def spec(self):
        index = self.index
        return pl.BlockSpec((None,) + self.array.shape[1:], lambda *_: (index, 0, 0))


def _tile(n, want):
    t = min(n, want)
    assert n % t == 0, (n, t)
    return t


def _ffn_fwd(h, gamma, wg, wu, wd, layer, cargo=None):
    S = h.shape[0]
    tm = _tile(S, 1024)
    nt = S // tm
    cargo = cargo or _Cargo()
    n_in, n_out = 5, 4
    nc_in, nc_out = len(cargo.operands), len(cargo.out_shapes)

    def body(*refs):
        h_ref, g_ref, wg_ref, wu_ref, wd_ref = refs[:n_in]
        c_ins = refs[n_in:n_in + nc_in]
        ho_ref, xn_ref, gg_ref, uu_ref = refs[n_in + nc_in:n_in + nc_in + n_out]
        c_outs = refs[n_in + nc_in + n_out:n_in + nc_in + n_out + nc_out]
        xn_s, acc_s = refs[n_in + nc_in + n_out + nc_out:n_in + nc_in + n_out + nc_out + 2]
        sems = refs[n_in + nc_in + n_out + nc_out + 2:]
        t, k = pl.program_id(0), pl.program_id(1)

        @pl.when((t == 0) & (k == 0))
        def _():
            cargo.start(c_ins, c_outs, sems)

        @pl.when((t == nt - 1) & (k == 0))
        def _():
            cargo.forward(c_ins, c_outs, sems)

        @pl.when(k == 0)
        def _():
            _, _, xn = _rms(h_ref[...], g_ref[...])
            xnb = xn.astype(BF16)
            xn_s[...] = xnb
            xn_ref[...] = xnb
            acc_s[...] = jnp.zeros_like(acc_s)

        xnb = xn_s[...]
        g = _dot_nt(xnb, wg_ref[...])
        u = _dot_nt(xnb, wu_ref[...])
        gg_ref[...] = g.astype(BF16)
        uu_ref[...] = u.astype(BF16)
        hid = (g * _sigmoid(g)) * u
        acc_s[...] += _dot(hid.astype(BF16), wd_ref[...])

        @pl.when(k == N_CHIPS - 1)
        def _():
            ho_ref[...] = h_ref[...] + 0.5 * acc_s[...]

        @pl.when((t == nt - 1) & (k == N_CHIPS - 1))
        def _():
            cargo.finish(c_ins, c_outs, sems)

    outs = pl.pallas_call(
        body, name="ffn_fwd",
        grid=(nt, N_CHIPS),
        in_specs=[
            pl.BlockSpec((tm, D_MODEL), lambda t, k: (t, 0)),
            gamma.spec(),
            pl.BlockSpec((None, None, FF_CHUNK, D_MODEL), lambda t, k: (layer, k, 0, 0)),
            pl.BlockSpec((None, None, FF_CHUNK, D_MODEL), lambda t, k: (layer, k, 0, 0)),
            pl.BlockSpec((None, None, FF_CHUNK, D_MODEL), lambda t, k: (layer, k, 0, 0)),
        ] + _any_specs(nc_in),
        out_specs=[
            pl.BlockSpec((tm, D_MODEL), lambda t, k: (t, 0)),
            pl.BlockSpec((tm, D_MODEL), lambda t, k: (t, 0)),
            pl.BlockSpec((None, tm, FF_CHUNK), lambda t, k: (k, t, 0)),
            pl.BlockSpec((None, tm, FF_CHUNK), lambda t, k: (k, t, 0)),
        ] + _any_specs(nc_out),
        out_shape=[
            jax.ShapeDtypeStruct((S, D_MODEL), F32),
            jax.ShapeDtypeStruct((S, D_MODEL), BF16),
            jax.ShapeDtypeStruct((N_CHIPS, S, FF_CHUNK), BF16),
            jax.ShapeDtypeStruct((N_CHIPS, S, FF_CHUNK), BF16),
        ] + cargo.out_shapes,
        input_output_aliases={n_in + i: n_out + o for i, o in cargo.aliases.items()},
        scratch_shapes=[pltpu.VMEM((tm, D_MODEL), BF16), pltpu.VMEM((tm, D_MODEL), F32)] + cargo.sem_shapes,
        compiler_params=_cparams(("arbitrary", "arbitrary")),
    )(h, gamma.array, wg, wu, wd, *cargo.operands)
    return outs[:n_out], list(outs[n_out:])


def _ffn_bwd(xn, dout, gg, uu, wg, wu, wd, layer, cargo=None):
    S = xn.shape[0]
    tm = _tile(S, 512)
    nt = S // tm
    nchunk = N_CHIPS - 1
    cargo = cargo or _Cargo()
    n_in, n_out = 7, 4
    nc_in, nc_out = len(cargo.operands), len(cargo.out_shapes)

    def body(*refs):
        xn_ref, do_ref, gg_ref, uu_ref, wg_ref, wu_ref, wd_ref = refs[:n_in]
        c_ins = refs[n_in:n_in + nc_in]
        dwg_ref, dwu_ref, dwd_ref, slab_ref = refs[n_in + nc_in:n_in + nc_in + n_out]
        c_outs = refs[n_in + nc_in + n_out:n_in + nc_in + n_out + nc_out]
        sems = refs[n_in + nc_in + n_out + nc_out:]
        k, t = pl.program_id(0), pl.program_id(1)

        @pl.when((k == 0) & (t == 0))
        def _():
            cargo.start(c_ins, c_outs, sems)

        @pl.when(t == 0)
        def _():
            dwg_ref[...] = jnp.zeros_like(dwg_ref)
            dwu_ref[...] = jnp.zeros_like(dwu_ref)
            dwd_ref[...] = jnp.zeros_like(dwd_ref)

        xnb = xn_ref[...]
        dob = (0.5 * do_ref[...]).astype(BF16)
        g = gg_ref[...].astype(F32)
        u = uu_ref[...].astype(F32)
        s = _sigmoid(g)
        sil = g * s
        dhid = _dot_nt(dob, wd_ref[...])
        dwd_ref[...] += _dot_tn((sil * u).astype(BF16), dob)
        du = (dhid * sil).astype(BF16)
        dg = (dhid * u * (s * (1.0 + g * (1.0 - s)))).astype(BF16)
        dwg_ref[...] += _dot_tn(dg, xnb)
        dwu_ref[...] += _dot_tn(du, xnb)
        slab_ref[...] = (_dot(dg, wg_ref[...]) + _dot(du, wu_ref[...])).astype(BF16)

        @pl.when((k == nchunk - 1) & (t == nt - 1))
        def _():
            cargo.finish(c_ins, c_outs, sems)

    outs = pl.pallas_call(
        body, name="ffn_bwd",
        grid=(nchunk, nt),
        in_specs=[
            pl.BlockSpec((tm, D_MODEL), lambda k, t: (t, 0)),
            pl.BlockSpec((tm, D_MODEL), lambda k, t: (t, 0)),
            pl.BlockSpec((None, tm, FF_CHUNK), lambda k, t: (k, t, 0)),
            pl.BlockSpec((None, tm, FF_CHUNK), lambda k, t: (k, t, 0)),
            pl.BlockSpec((None, None, FF_CHUNK, D_MODEL), lambda k, t: (layer, k, 0, 0)),
            pl.BlockSpec((None, None, FF_CHUNK, D_MODEL), lambda k, t: (layer, k, 0, 0)),
            pl.BlockSpec((None, None, FF_CHUNK, D_MODEL), lambda k, t: (layer, k, 0, 0)),
        ] + _any_specs(nc_in),
        out_specs=[
            pl.BlockSpec((None, FF_CHUNK, D_MODEL), lambda k, t: (k, 0, 0)),
            pl.BlockSpec((None, FF_CHUNK, D_MODEL), lambda k, t: (k, 0, 0)),
            pl.BlockSpec((None, FF_CHUNK, D_MODEL), lambda k, t: (k, 0, 0)),
            pl.BlockSpec((None, tm, D_MODEL), lambda k, t: (k, t, 0)),
        ] + _any_specs(nc_out),
        out_shape=[
            jax.ShapeDtypeStruct((N_CHIPS, FF_CHUNK, D_MODEL), F32),
            jax.ShapeDtypeStruct((N_CHIPS, FF_CHUNK, D_MODEL), F32),
            jax.ShapeDtypeStruct((N_CHIPS, FF_CHUNK, D_MODEL), F32),
            jax.ShapeDtypeStruct((nchunk, S, D_MODEL), BF16),
        ] + cargo.out_shapes,
        input_output_aliases={n_in + i: n_out + o for i, o in cargo.aliases.items()},
        scratch_shapes=list(cargo.sem_shapes),
        compiler_params=_cparams(("arbitrary", "arbitrary")),
    )(xn, dout, gg, uu, wg, wu, wd, *cargo.operands)
    return outs[:n_out], list(outs[n_out:])


def _ffn_bwd_last(xn, dout, gg, uu, wg, wu, wd, layer, slabs, h, gamma, dwg, dwu, dwd):
    S = xn.shape[0]
    tm = _tile(S, 512)
    k = N_CHIPS - 1
    nprev = slabs.shape[0]

    def body(xn_ref, do_ref, gg_ref, uu_ref, wg_ref, wu_ref, wd_ref, slab_ref, h_ref, g_ref, _dwg, _dwu, _dwd,
             dh_ref, dgam_ref, dwg_ref, dwu_ref, dwd_ref):
        @pl.when(pl.program_id(0) == 0)
        def _():
            dgam_ref[...] = jnp.zeros_like(dgam_ref)
            dwg_ref[...] = jnp.zeros_like(dwg_ref)
            dwu_ref[...] = jnp.zeros_like(dwu_ref)
            dwd_ref[...] = jnp.zeros_like(dwd_ref)

        xnb = xn_ref[...]
        do = do_ref[...]
        dob = (0.5 * do).astype(BF16)
        g = gg_ref[...].astype(F32)
        u = uu_ref[...].astype(F32)
        s = _sigmoid(g)
        sil = g * s
        dhid = _dot_nt(dob, wd_ref[...])
        dwd_ref[...] += _dot_tn((sil * u).astype(BF16), dob)
        du = (dhid * sil).astype(BF16)
        dg = (dhid * u * (s * (1.0 + g * (1.0 - s)))).astype(BF16)
        dwg_ref[...] += _dot_tn(dg, xnb)
        dwu_ref[...] += _dot_tn(du, xnb)
        dxn = _dot(dg, wg_ref[...]) + _dot(du, wu_ref[...])
        for i in range(nprev):
            dxn = dxn + slab_ref[i].astype(F32)
        xhat, rstd, _ = _rms(h_ref[...], g_ref[...])
        dhn, dgam = _rms_bwd(xhat, rstd, g_ref[...], dxn)
        dh_ref[...] = do + dhn
        dgam_ref[...] += dgam

    tile = pl.BlockSpec((tm, D_MODEL), lambda t: (t, 0))
    hidden = pl.BlockSpec((None, tm, FF_CHUNK), lambda t: (k, t, 0))
    w_in = pl.BlockSpec((None, None, FF_CHUNK, D_MODEL), lambda t: (layer, k, 0, 0))
    dw_in = pl.BlockSpec((None, FF_CHUNK, D_MODEL), lambda t: (k, 0, 0))
    return pl.pallas_call(
        body, name="ffn_bwd_last",
        grid=(S // tm,),
        in_specs=[tile, tile, hidden, hidden, w_in, w_in,
                  pl.BlockSpec((None, None, FF_CHUNK, D_MODEL), lambda t: (layer, k, 0, 0)),
                  pl.BlockSpec((nprev, tm, D_MODEL), lambda t: (0, t, 0)), tile,
                  gamma.spec()] + _any_specs(3),
        out_specs=[tile, pl.BlockSpec((SUBLANES, D_MODEL), lambda t: (0, 0)), dw_in, dw_in,
                   pl.BlockSpec((None, FF_CHUNK, D_MODEL), lambda t: (k, 0, 0))],
        out_shape=[jax.ShapeDtypeStruct((S, D_MODEL), F32), jax.ShapeDtypeStruct((SUBLANES, D_MODEL), F32),
                   jax.ShapeDtypeStruct(dwg.shape, F32), jax.ShapeDtypeStruct(dwu.shape, F32),
                   jax.ShapeDtypeStruct(dwd.shape, F32)],
        input_output_aliases={10: 2, 11: 3, 12: 4},
        compiler_params=_cparams(("arbitrary",)),
    )(xn, dout, gg, uu, wg, wu, wd, slabs, h, gamma.array, dwg, dwu, dwd)


def _xt_dy(name, x, dy, nchunk, kb, nb, x_by_chunk, y_by_chunk):
    S = x.shape[0]
    tm = _tile(S, 2048)

    def body(x_ref, dy_ref, o_ref):
        @pl.when(pl.program_id(1) == 0)
        def _():
            o_ref[...] = jnp.zeros_like(o_ref)

        o_ref[...] += _dot_tn(x_ref[...].astype(BF16), dy_ref[...].astype(BF16))

    return pl.pallas_call(
        body, name=name,
        grid=(nchunk, S // tm),
        in_specs=[
            pl.BlockSpec((tm, kb), (lambda c, t: (t, c)) if x_by_chunk else (lambda c, t: (t, 0))),
            pl.BlockSpec((tm, nb), (lambda c, t: (t, c)) if y_by_chunk else (lambda c, t: (t, 0))),
        ],
        out_specs=pl.BlockSpec((None, kb, nb), lambda c, t: (c, 0, 0)),
        out_shape=jax.ShapeDtypeStruct((nchunk, kb, nb), F32),
        compiler_params=_cparams(("arbitrary", "arbitrary")),
    )(x, dy)


def _lru_gates(xc, wbd_ref, ba, bx, apar):
    xcb = xc.astype(BF16)
    r_parts, ig_parts = [], []
    for q in range(D_RNN // GATE_CHUNK):
        lo, hi = q * GATE_CHUNK, (q + 1) * GATE_CHUNK
        pre = _dot(xcb[:, lo:hi], wbd_ref[q])
        r_parts.append(_sigmoid(pre[:, :GATE_CHUNK] + ba[:, lo:hi]))
        ig_parts.append(_sigmoid(pre[:, GATE_CHUNK:] + bx[:, lo:hi]))
    r = jnp.concatenate(r_parts, axis=1)
    ig = jnp.concatenate(ig_parts, axis=1)
    sp = LRU_C * _softplus(-apar)
    log_a = -(r * sp)
    a = jnp.exp(log_a)
    mult = jnp.sqrt(_neg_expm1(2.0 * log_a))
    return r, ig, a, mult, sp


def _conv_causal(xb, tail, cw_ref, cb):
    ext = jnp.concatenate([tail, xb], axis=0)
    xc = cb + cw_ref[CONV_WIDTH - 1:CONV_WIDTH, :] * xb
    for j in range(1, CONV_WIDTH):
        xc = xc + cw_ref[CONV_WIDTH - 1 - j:CONV_WIDTH - j, :] * _shift_down(ext, j, SUBLANES)
    return xc, ext


def _lru_fwd(h, gamma, win, layer, convw, convb, wbd, ba, bx, apar, wout):
    S = h.shape[0]
    tm = _tile(S, 256)

    def body(h_ref, g_ref, win_ref, cw_ref, cb_ref, wbd_ref, ba_ref, bx_ref, ap_ref, wout_ref,
             ho_ref, xn_ref, z_ref, hs_ref, gates_ref, tail_s, carry_s):
        @pl.when(pl.program_id(0) == 0)
        def _():
            tail_s[...] = jnp.zeros_like(tail_s)
            carry_s[...] = jnp.zeros_like(carry_s)

        hf = h_ref[...]
        _, _, xn = _rms(hf, g_ref[...])
        xnb = xn.astype(BF16)
        xn_ref[...] = xnb
        for k in range(N_CHIPS):
            z_ref[:, k * RNN_IN_CHUNK:(k + 1) * RNN_IN_CHUNK] = _dot(xnb, win_ref[k])
        gate = z_ref[:, :D_RNN]
        xb = z_ref[:, D_RNN:]
        xc, _ = _conv_causal(xb, tail_s[...], cw_ref, cb_ref[...])
        tail_s[...] = xb[tm - SUBLANES:, :]
        r, ig, a, mult, _ = _lru_gates(xc, wbd_ref, ba_ref[...], bx_ref[...], ap_ref[...])
        for plane, val in enumerate((xc, r, ig, a, mult)):
            gates_ref[plane] = val
        big_a, big_b = _scan_causal(a, mult * (ig * xc))
        hs = big_a * carry_s[SUBLANES - 1:SUBLANES, :] + big_b
        hs_ref[...] = hs
        carry_s[...] = hs[tm - SUBLANES:, :]
        y = hs * _gelu(gate)
        ho_ref[...] = hf + _dot(y.astype(BF16), wout_ref[...])

    return pl.pallas_call(
        body, name="lru_fwd",
        grid=(S // tm,),
        in_specs=[
            pl.BlockSpec((tm, D_MODEL), lambda t: (t, 0)),
            gamma.spec(),
            pl.BlockSpec((None, N_CHIPS, D_MODEL, RNN_IN_CHUNK), lambda t: (layer, 0, 0, 0)),
            convw.spec(),
            convb.spec(),
            pl.BlockSpec((D_RNN // GATE_CHUNK, GATE_CHUNK, 2 * GATE_CHUNK), lambda t: (0, 0, 0)),
            ba.spec(), bx.spec(), apar.spec(),
            pl.BlockSpec((None, D_RNN, D_MODEL), lambda t: (layer, 0, 0)),
        ],
        out_specs=[
            pl.BlockSpec((tm, D_MODEL), lambda t: (t, 0)),
            pl.BlockSpec((tm, D_MODEL), lambda t: (t, 0)),
            pl.BlockSpec((tm, 2 * D_RNN), lambda t: (t, 0)),
            pl.BlockSpec((tm, D_RNN), lambda t: (t, 0)),
            pl.BlockSpec((N_GATE_PLANES, tm, D_RNN), lambda t: (0, t, 0)),
        ],
        out_shape=[
            jax.ShapeDtypeStruct((S, D_MODEL), F32),
            jax.ShapeDtypeStruct((S, D_MODEL), BF16),
            jax.ShapeDtypeStruct((S, 2 * D_RNN), F32),
            jax.ShapeDtypeStruct((S, D_RNN), F32),
            jax.ShapeDtypeStruct((N_GATE_PLANES, S, D_RNN), F32),
        ],
        scratch_shapes=[pltpu.VMEM((SUBLANES, D_RNN), F32), pltpu.VMEM((SUBLANES, D_RNN), F32)],
        compiler_params=_cparams(("arbitrary",)),
    )(h, gamma.array, win, convw.array, convb.array, wbd, ba.array, bx.array, apar.array, wout)


def _lru_bwd_seq(dout, z, hs, gates, convw, wbd, apar, wout, layer, cargo=None):
    S = dout.shape[0]
    tm = _tile(S, 256)
    nt = S // tm
    per8 = tm // SUBLANES
    rev = lambda i: nt - 1 - i
    prev8 = lambda i: jnp.maximum(rev(i) * per8 - 1, 0)
    cargo = cargo or _Cargo()
    n_in, n_out = 10, 6

    def body(*refs):
        ins, c_ins, outs, c_outs, scratch, sems = _cargo_refs(refs, cargo, n_in, n_out, n_scratch=3)
        do_ref, z_ref, hs_ref, gates_ref, ztail_ref, hstail_ref, cw_ref, wbd_ref, ap_ref, wout_ref = ins
        dz_ref, dpre_ref, xc_ref, y_ref, dcw_ref, vec_ref = outs
        a_first_s, g_first_s, dxc_head_s = scratch
        i = pl.program_id(0)
        first_in_time = rev(i) == 0

        @pl.when(i == 0)
        def _():
            cargo.start(c_ins, c_outs, sems)
            a_first_s[...] = jnp.zeros_like(a_first_s)
            g_first_s[...] = jnp.zeros_like(g_first_s)
            dxc_head_s[...] = jnp.zeros_like(dxc_head_s)
            dcw_ref[...] = jnp.zeros_like(dcw_ref)
            vec_ref[...] = jnp.zeros_like(vec_ref)

        gate = z_ref[:, :D_RNN]
        xb = z_ref[:, D_RNN:]
        hist = jnp.where(first_in_time, 0.0, 1.0)
        xext = jnp.concatenate([ztail_ref[:, D_RNN:] * hist, xb], axis=0)
        xc, r, ig, a, mult = (gates_ref[plane] for plane in range(N_GATE_PLANES))
        sp = LRU_C * _softplus(-ap_ref[...])
        hs = hs_ref[...]
        gel, dgel = _gelu_and_grad(gate)
        y = hs * gel
        y_ref[...] = y.astype(BF16)
        xc_ref[...] = xc.astype(BF16)

        dy = _dot_nt(do_ref[...].astype(BF16), wout_ref[...])
        dhs = dy * gel
        dgate = dy * hs * dgel

        coef = _shift_up(jnp.concatenate([a, a_first_s[...]], axis=0), 1, tm)
        big_c, big_d = _scan_anticausal(coef, dhs)
        g = big_d + big_c * g_first_s[0:1, :]
        g_first_s[...] = g[:SUBLANES, :]
        a_first_s[...] = a[:SUBLANES, :]

        hs_prev = _shift_down(jnp.concatenate([hstail_ref[...] * hist, hs], axis=0), 1, SUBLANES)
        da = g * hs_prev
        dmult = g * ig * xc
        dig = g * mult * xc
        dxc = g * mult * ig
        dlog_a = da * a - dmult * (a * a) / mult
        dr = -(dlog_a * sp)
        dpre_a = dr * r * (1.0 - r)
        dpre_x = dig * ig * (1.0 - ig)
        d_apar = dlog_a * r * (LRU_C * _sigmoid(-ap_ref[...]))

        for q in range(D_RNN // GATE_CHUNK):
            lo, hi = q * GATE_CHUNK, (q + 1) * GATE_CHUNK
            dpre_q = jnp.concatenate([dpre_a[:, lo:hi], dpre_x[:, lo:hi]], axis=1).astype(BF16)
            dpre_ref[:, 2 * lo:2 * hi] = dpre_q
            dxc_q = _dot_nt(dpre_q, wbd_ref[q])
            if q == 0:
                dxc_parts = [dxc_q]
            else:
                dxc_parts.append(dxc_q)
        dxc = dxc + jnp.concatenate(dxc_parts, axis=1)

        dext = jnp.concatenate([dxc, dxc_head_s[...]], axis=0)
        dxb = cw_ref[CONV_WIDTH - 1:CONV_WIDTH, :] * dxc
        for j in range(1, CONV_WIDTH):
            dxb = dxb + cw_ref[CONV_WIDTH - 1 - j:CONV_WIDTH - j, :] * _shift_up(dext, j, tm)
        dxc_head_s[...] = dxc[:SUBLANES, :]
        dz_ref[:, :D_RNN] = dgate.astype(BF16)
        dz_ref[:, D_RNN:] = dxb.astype(BF16)

        dcw_ref[CONV_WIDTH - 1] += _rowsum8(dxc * xb)
        for j in range(1, CONV_WIDTH):
            dcw_ref[CONV_WIDTH - 1 - j] += _rowsum8(dxc * _shift_down(xext, j, SUBLANES))
        vec_ref[0] += _rowsum8(dxc)
        vec_ref[1] += _rowsum8(dpre_a)
        vec_ref[2] += _rowsum8(dpre_x)
        vec_ref[3] += _rowsum8(d_apar)

        @pl.when(i == nt - 1)
        def _():
            cargo.finish(c_ins, c_outs, sems)

    outs = pl.pallas_call(
        body, name="lru_bwd_seq",
        grid=(nt,),
        in_specs=[
            pl.BlockSpec((tm, D_MODEL), lambda i: (rev(i), 0)),
            pl.BlockSpec((tm, 2 * D_RNN), lambda i: (rev(i), 0)),
            pl.BlockSpec((tm, D_RNN), lambda i: (rev(i), 0)),
            pl.BlockSpec((N_GATE_PLANES, tm, D_RNN), lambda i: (0, rev(i), 0)),
            pl.BlockSpec((SUBLANES, 2 * D_RNN), lambda i: (prev8(i), 0)),
            pl.BlockSpec((SUBLANES, D_RNN), lambda i: (prev8(i), 0)),
            convw.spec(),
            pl.BlockSpec((D_RNN // GATE_CHUNK, GATE_CHUNK, 2 * GATE_CHUNK), lambda i: (0, 0, 0)),
            apar.spec(),
            pl.BlockSpec((None, D_RNN, D_MODEL), lambda i: (layer, 0, 0)),
        ] + _any_specs(len(cargo.operands)),
        out_specs=[
            pl.BlockSpec((tm, 2 * D_RNN), lambda i: (rev(i), 0)),
            pl.BlockSpec((tm, 2 * D_RNN), lambda i: (rev(i), 0)),
            pl.BlockSpec((tm, D_RNN), lambda i: (rev(i), 0)),
            pl.BlockSpec((tm, D_RNN), lambda i: (rev(i), 0)),
            pl.BlockSpec((CONV_WIDTH, SUBLANES, D_RNN), lambda i: (0, 0, 0)),
            pl.BlockSpec((4, SUBLANES, D_RNN), lambda i: (0, 0, 0)),
        ] + _any_specs(len(cargo.out_shapes)),
        out_shape=[
            jax.ShapeDtypeStruct((S, 2 * D_RNN), BF16),
            jax.ShapeDtypeStruct((S, 2 * D_RNN), BF16),
            jax.ShapeDtypeStruct((S, D_RNN), BF16),
            jax.ShapeDtypeStruct((S, D_RNN), BF16),
            jax.ShapeDtypeStruct((CONV_WIDTH, SUBLANES, D_RNN), F32),
            jax.ShapeDtypeStruct((4, SUBLANES, D_RNN), F32),
        ] + cargo.out_shapes,
        input_output_aliases={n_in + i: n_out + o for i, o in cargo.aliases.items()},
        scratch_shapes=[pltpu.VMEM((SUBLANES, D_RNN), F32)] * 3 + cargo.sem_shapes,
        compiler_params=_cparams(("arbitrary",)),
    )(dout, z, hs, gates, z, hs, convw.array, wbd, apar.array, wout, *cargo.operands)
    return outs[:n_out], list(outs[n_out:])


def _lru_bwd_in(dz, h, gamma, dres, win, layer, cargo=None):
    S = h.shape[0]
    tm = _tile(S, 512)
    nt = S // tm
    cargo = cargo or _Cargo()
    n_in, n_out = 5, 2

    def body(*refs):
        (dz_ref, h_ref, g_ref, dres_ref, win_ref), c_ins, (dh_ref, dgam_ref), c_outs, _, sems = _cargo_refs(
            refs, cargo, n_in, n_out)

        @pl.when(pl.program_id(0) == 0)
        def _():
            cargo.start(c_ins, c_outs, sems)

        dxn = _dot_nt(dz_ref[:, :RNN_IN_CHUNK], win_ref[0])
        for k in range(1, N_CHIPS):
            dxn = dxn + _dot_nt(dz_ref[:, k * RNN_IN_CHUNK:(k + 1) * RNN_IN_CHUNK], win_ref[k])
        xhat, rstd, _ = _rms(h_ref[...], g_ref[...])
        dhn, dgam = _rms_bwd(xhat, rstd, g_ref[...], dxn)
        dh_ref[...] = dres_ref[...] + dhn

        @pl.when(pl.program_id(0) == 0)
        def _():
            dgam_ref[...] = jnp.zeros_like(dgam_ref)

        dgam_ref[...] += dgam

        @pl.when(pl.program_id(0) == nt - 1)
        def _():
            cargo.finish(c_ins, c_outs, sems)

    outs = pl.pallas_call(
        body, name="lru_bwd_in",
        grid=(nt,),
        in_specs=[
            pl.BlockSpec((tm, 2 * D_RNN), lambda t: (t, 0)),
            pl.BlockSpec((tm, D_MODEL), lambda t: (t, 0)),
            gamma.spec(),
            pl.BlockSpec((tm, D_MODEL), lambda t: (t, 0)),
            pl.BlockSpec((None, N_CHIPS, D_MODEL, RNN_IN_CHUNK), lambda t: (layer, 0, 0, 0)),
        ] + _any_specs(len(cargo.operands)),
        out_specs=[
            pl.BlockSpec((tm, D_MODEL), lambda t: (t, 0)),
            pl.BlockSpec((SUBLANES, D_MODEL), lambda t: (0, 0)),
        ] + _any_specs(len(cargo.out_shapes)),
        out_shape=[jax.ShapeDtypeStruct((S, D_MODEL), F32),
                   jax.ShapeDtypeStruct((SUBLANES, D_MODEL), F32)] + cargo.out_shapes,
        input_output_aliases={n_in + i: n_out + o for i, o in cargo.aliases.items()},
        scratch_shapes=list(cargo.sem_shapes),
        compiler_params=_cparams(("arbitrary",)),
    )(dz, h, gamma.array, dres, win, *cargo.operands)
    return outs[:n_out], list(outs[n_out:])


def _pool_inv_count(t_index, tm):
    rows = (lax.broadcasted_iota(jnp.int32, (tm, D_MODEL), 0) + t_index * tm + 1).astype(F32)
    col = lax.broadcasted_iota(jnp.int32, (tm, D_MODEL), 1)
    win = jnp.where(col < POOL_GROUP_DIM, float(POOL_WINDOWS[0]),
                    jnp.where(col < 2 * POOL_GROUP_DIM, float(POOL_WINDOWS[1]),
                              jnp.where(col < 3 * POOL_GROUP_DIM, float(POOL_WINDOWS[2]), float(POOL_WINDOWS[3]))))
    return 1.0 / jnp.minimum(rows, win)


def _window_sums(ext, shift, take):
    gd = POOL_GROUP_DIM
    s2 = ext + shift(ext, 1)
    s4 = s2[:, gd:] + shift(s2[:, gd:], 2)
    s8 = s4[:, gd:] + shift(s4[:, gd:], 4)
    s16 = s8[:, gd:] + shift(s8[:, gd:], 8)
    return jnp.concatenate([take(s2[:, :gd]), take(s4[:, :gd]), take(s8[:, :gd]), take(s16)], axis=1)


def _pool_fwd(h, gamma, pw, pb, pscale, layer):
    S = h.shape[0]
    tm = _tile(S, 512)

    def body(h_ref, g_ref, pw_ref, pb_ref, ps_ref, ho_ref, u_ref, tail_s):
        t = pl.program_id(0)

        @pl.when(t == 0)
        def _():
            tail_s[...] = jnp.zeros_like(tail_s)

        hf = h_ref[...]
        _, _, hn = _rms(hf, g_ref[...])
        ext = jnp.concatenate([tail_s[...], hn], axis=0)
        tail_s[...] = hn[tm - POOL_HALO:, :]
        sums = _window_sums(ext, lambda v, j: pltpu.roll(v, j, 0), lambda v: v[POOL_HALO:])
        ub = (sums * _pool_inv_count(t, tm) - hn).astype(BF16)
        u_ref[...] = ub
        ys = [_dot(ub[:, g * POOL_GROUP_DIM:(g + 1) * POOL_GROUP_DIM], pw_ref[g]) for g in range(len(POOL_WINDOWS))]
        y = jnp.concatenate(ys, axis=1)
        ho_ref[...] = hf + (y + pb_ref[...]) * ps_ref[...]

    return pl.pallas_call(
        body, name="pool_fwd",
        grid=(S // tm,),
        in_specs=[
            pl.BlockSpec((tm, D_MODEL), lambda t: (t, 0)), gamma.spec(),
            pl.BlockSpec((None, len(POOL_WINDOWS), POOL_GROUP_DIM, POOL_GROUP_DIM), lambda t: (layer, 0, 0, 0)),
            pb.spec(), pscale.spec(),
        ],
        out_specs=[pl.BlockSpec((tm, D_MODEL), lambda t: (t, 0)), pl.BlockSpec((tm, D_MODEL), lambda t: (t, 0))],
        out_shape=[jax.ShapeDtypeStruct((S, D_MODEL), F32), jax.ShapeDtypeStruct((S, D_MODEL), BF16)],
        scratch_shapes=[pltpu.VMEM((POOL_HALO, D_MODEL), F32)],
        compiler_params=_cparams(("arbitrary",)),
    )(h, gamma.array, pw, pb.array, pscale.array)


def _pool_bwd(dout, h, u, gamma, pw, pb, pscale, layer):
    S = h.shape[0]
    tm = _tile(S, 512)
    nt = S // tm
    rev = lambda i: nt - 1 - i
    ngroup = len(POOL_WINDOWS)

    def body(do_ref, h_ref, u_ref, g_ref, pw_ref, pb_ref, ps_ref, dh_ref, dpre_ref, vec_ref, head_s):
        i = pl.program_id(0)

        @pl.when(i == 0)
        def _():
            head_s[...] = jnp.zeros_like(head_s)
            vec_ref[...] = jnp.zeros_like(vec_ref)

        do = do_ref[...]
        ub = u_ref[...]
        gsl = lambda v, g: v[:, g * POOL_GROUP_DIM:(g + 1) * POOL_GROUP_DIM]
        y = jnp.concatenate([_dot(gsl(ub, g), pw_ref[g]) for g in range(ngroup)], axis=1)
        dpre = do * ps_ref[...]
        dpb = dpre.astype(BF16)
        dpre_ref[...] = dpb
        du = jnp.concatenate([_dot_nt(gsl(dpb, g), pw_ref[g]) for g in range(ngroup)], axis=1)
        v = du * _pool_inv_count(rev(i), tm)
        ext = jnp.concatenate([v, head_s[...]], axis=0)
        head_s[...] = v[:POOL_HALO, :]
        n = tm + POOL_HALO
        dhn = _window_sums(ext, lambda w, j: pltpu.roll(w, n - j, 0), lambda w: w[:tm]) - du
        xhat, rstd, _ = _rms(h_ref[...], g_ref[...])
        dh_in, dgam = _rms_bwd(xhat, rstd, g_ref[...], dhn)
        dh_ref[...] = do + dh_in
        vec_ref[0] += dgam
        vec_ref[1] += _rowsum8(dpre)
        vec_ref[2] += _rowsum8(do * (y + pb_ref[...]))

    tile = pl.BlockSpec((tm, D_MODEL), lambda i: (rev(i), 0))
    return pl.pallas_call(
        body, name="pool_bwd",
        grid=(nt,),
        in_specs=[tile, tile, tile, gamma.spec(),
                  pl.BlockSpec((None, ngroup, POOL_GROUP_DIM, POOL_GROUP_DIM), lambda i: (layer, 0, 0, 0)),
                  pb.spec(), pscale.spec()],
        out_specs=[tile, tile, pl.BlockSpec((3, SUBLANES, D_MODEL), lambda i: (0, 0, 0))],
        out_shape=[jax.ShapeDtypeStruct((S, D_MODEL), F32), jax.ShapeDtypeStruct((S, D_MODEL), BF16),
                   jax.ShapeDtypeStruct((3, SUBLANES, D_MODEL), F32)],
        scratch_shapes=[pltpu.VMEM((POOL_HALO, D_MODEL), F32)],
        compiler_params=_cparams(("arbitrary",)),
    )(dout, h, u, gamma.array, pw, pb.array, pscale.array)


def _ple_parts(hf, gamma, p_tile, wgate_ref, wproj_ref):
    xhat, rstd, xn = _rms(hf, gamma)
    xnb = xn.astype(BF16)
    gate = _sigmoid(_dot(xnb, wgate_ref[...]))
    pb = p_tile.astype(BF16)
    proj = jnp.concatenate([_dot(pb, wproj_ref[k]) for k in range(N_CHIPS)], axis=1)
    return xhat, rstd, xnb, pb, gate, proj


def _ple_fwd(h, gamma, p_l, wgate, wproj, layer):
    S = h.shape[0]
    tm = _tile(S, 1024)

    def body(h_ref, g_ref, p_ref, wgate_ref, wproj_ref, ho_ref):
        hf = h_ref[...]
        _, _, _, _, gate, proj = _ple_parts(hf, g_ref[...], p_ref[...], wgate_ref, wproj_ref)
        ho_ref[...] = hf + gate * proj

    return pl.pallas_call(
        body, name="ple_fwd",
        grid=(S // tm,),
        in_specs=[
            pl.BlockSpec((tm, D_MODEL), lambda t: (t, 0)),
            gamma.spec(),
            pl.BlockSpec((None, None, tm, PLE_DIM), lambda t: (layer, 0, t, 0)),
            pl.BlockSpec((None, D_MODEL, D_MODEL), lambda t: (layer, 0, 0)),
            pl.BlockSpec((None, N_CHIPS, PLE_DIM, PLE_DIM), lambda t: (layer, 0, 0, 0)),
        ],
        out_specs=pl.BlockSpec((tm, D_MODEL), lambda t: (t, 0)),
        out_shape=jax.ShapeDtypeStruct((S, D_MODEL), F32),
        compiler_params=_cparams(("arbitrary",)),
    )(h, gamma.array, p_l, wgate, wproj)


def _ple_bwd(dout, h, gamma, p_l, wgate, wproj, layer):
    S = h.shape[0]
    tm = _tile(S, 512)

    def body(do_ref, h_ref, g_ref, p_ref, wgate_ref, wproj_ref, dh_ref, dwg_ref, dwp_ref, dgam_ref):
        @pl.when(pl.program_id(0) == 0)
        def _():
            dgam_ref[...] = jnp.zeros_like(dgam_ref)
            dwg_ref[...] = jnp.zeros_like(dwg_ref)
            dwp_ref[...] = jnp.zeros_like(dwp_ref)

        do = do_ref[...]
        xhat, rstd, xnb, pb, gate, proj = _ple_parts(h_ref[...], g_ref[...], p_ref[...], wgate_ref, wproj_ref)
        dproj = (do * gate).astype(BF16)
        dpre = (do * proj * gate * (1.0 - gate)).astype(BF16)
        dwg_ref[...] += _dot_tn(xnb, dpre)
        for k in range(N_CHIPS):
            dwp_ref[k] += _dot_tn(pb, dproj[:, k * PLE_DIM:(k + 1) * PLE_DIM])
        dhn, dgam = _rms_bwd(xhat, rstd, g_ref[...], _dot_nt(dpre, wgate_ref[...]))
        dh_ref[...] = do + dhn
        dgam_ref[...] += dgam

    tile = pl.BlockSpec((tm, D_MODEL), lambda t: (t, 0))
    return pl.pallas_call(
        body, name="ple_bwd",
        grid=(S // tm,),
        in_specs=[
            tile, tile,
            gamma.spec(),
            pl.BlockSpec((None, None, tm, PLE_DIM), lambda t: (layer, 0, t, 0)),
            pl.BlockSpec((None, D_MODEL, D_MODEL), lambda t: (layer, 0, 0)),
            pl.BlockSpec((None, N_CHIPS, PLE_DIM, PLE_DIM), lambda t: (layer, 0, 0, 0)),
        ],
        out_specs=[tile, pl.BlockSpec((D_MODEL, D_MODEL), lambda t: (0, 0)),
                   pl.BlockSpec((N_CHIPS, PLE_DIM, PLE_DIM), lambda t: (0, 0, 0)),
                   pl.BlockSpec((SUBLANES, D_MODEL), lambda t: (0, 0))],
        out_shape=[jax.ShapeDtypeStruct((S, D_MODEL), F32), jax.ShapeDtypeStruct((D_MODEL, D_MODEL), F32),
                   jax.ShapeDtypeStruct((N_CHIPS, PLE_DIM, PLE_DIM), F32),
                   jax.ShapeDtypeStruct((SUBLANES, D_MODEL), F32)],
        compiler_params=_cparams(("arbitrary",)),
    )(dout, h, gamma.array, p_l, wgate, wproj)


def _final(h, gamma, target):
    S = h.shape[0]
    tm = _tile(S, 1024)

    def body(h_ref, g_ref, tgt_ref, dh_ref, dgam_ref, loss_ref):
        xhat, rstd, y = _rms(h_ref[...], g_ref[...])
        err = y - tgt_ref[...]
        dy = err * (1.0 / D_MODEL)
        dhn, dgam = _rms_bwd(xhat, rstd, g_ref[...], dy)
        dh_ref[...] = dhn
        sq = _rowsum8(err * err)
        part = sq[:, :LANES]
        for j in range(1, D_MODEL // LANES):
            part = part + sq[:, j * LANES:(j + 1) * LANES]

        @pl.when(pl.program_id(0) == 0)
        def _():
            dgam_ref[...] = jnp.zeros_like(dgam_ref)
            loss_ref[...] = jnp.zeros_like(loss_ref)

        dgam_ref[...] += dgam
        loss_ref[...] += part * (0.5 / D_MODEL)

    tile = pl.BlockSpec((tm, D_MODEL), lambda t: (t, 0))
    return pl.pallas_call(
        body, name="final_loss",
        grid=(S // tm,),
        in_specs=[tile, gamma.spec(), tile],
        out_specs=[tile, pl.BlockSpec((SUBLANES, D_MODEL), lambda t: (0, 0)),
                   pl.BlockSpec((SUBLANES, LANES), lambda t: (0, 0))],
        out_shape=[jax.ShapeDtypeStruct((S, D_MODEL), F32), jax.ShapeDtypeStruct((SUBLANES, D_MODEL), F32),
                   jax.ShapeDtypeStruct((SUBLANES, LANES), F32)],
        compiler_params=_cparams(("arbitrary",)),
    )(h, gamma.array, target)


def _mesh_pos():
    return lax.axis_index("x"), lax.axis_index("y"), lax.axis_index("c")


def _other_chip(x, y, j):
    fx, fy = CHIP_FLIPS[j]
    return (1 - x if fx else x), (1 - y if fy else y)


def _any_specs(n):
    return [pl.BlockSpec(memory_space=pl.ANY)] * n


class _Cargo:
    def __init__(self):
        self.operands, self.out_shapes, self.aliases, self.sem_shapes, self.names = [], [], {}, [], []
        self.start = lambda ins, outs, sems: None
        self.forward = lambda ins, outs, sems: None
        self.finish = lambda ins, outs, sems: None


def _cargo_refs(refs, cargo, n_in, n_out, n_scratch=0):
    a = n_in
    b = a + len(cargo.operands)
    c = b + n_out
    d = c + len(cargo.out_shapes)
    e = d + n_scratch
    return refs[:a], refs[a:b], refs[b:c], refs[c:d], refs[d:e], refs[e:]


def _remote(src, dst, send, recv, device):
    return pltpu.make_async_remote_copy(src_ref=src, dst_ref=dst, send_sem=send, recv_sem=recv,
                                        device_id=device, device_id_type=MESH_ID)


def _gather_cargo(bufs, pieces):
    cargo = _Cargo()
    if not pieces:
        return cargo
    plist = []
    for name, layer in pieces:
        if name not in cargo.names:
            cargo.names.append(name)
            cargo.operands.append(bufs[name])
        plist.append((cargo.names.index(name), layer, bufs[name].shape[2] // 2))
    nflip = len(CHIP_FLIPS)
    cargo.out_shapes = [jax.ShapeDtypeStruct(b.shape, b.dtype) for b in cargo.operands]
    cargo.aliases = {i: i for i in range(len(cargo.operands))}
    cargo.sem_shapes = [pltpu.SemaphoreType.DMA((len(plist) * nflip,))] * 4

    def copies(outs, sems):
        send1, recv1, send2, recv2 = sems
        x, y, c = _mesh_pos()
        k = 2 * x + y

        def blk(p, chip, cc):
            b, layer, hrows = plist[p]
            return outs[b].at[layer, chip, pl.ds(cc * hrows, hrows), :]

        def chip_of(j):
            px, py = _other_chip(x, y, j)
            return 2 * px + py

        def ici(p, j):
            px, py = _other_chip(x, y, j)
            return _remote(blk(p, k, c), blk(p, k, c), send1.at[p * nflip + j], recv1.at[p * nflip + j], (px, py, c))

        def landed(p, j):
            px, py = _other_chip(x, y, j)
            return _remote(blk(p, k, c), blk(p, chip_of(j), c), send1.at[p * nflip + j], recv1.at[p * nflip + j],
                           (px, py, c))

        def d2d(p, j, cc):
            return _remote(blk(p, chip_of(j), cc), blk(p, chip_of(j), cc), send2.at[p * nflip + j],
                           recv2.at[p * nflip + j], (x, y, 1 - c))

        return c, ici, landed, d2d

    def start(ins, outs, sems):
        _, ici, _, _ = copies(outs, sems)
        for p in range(len(plist)):
            for j in range(nflip):
                ici(p, j).start()

    def forward(ins, outs, sems):
        c, _, landed, d2d = copies(outs, sems)
        for j in range(nflip):
            for p in range(len(plist)):
                landed(p, j).wait_recv()
                d2d(p, j, c).start()

    def finish(ins, outs, sems):
        c, ici, _, d2d = copies(outs, sems)
        for p in range(len(plist)):
            for j in range(nflip):
                ici(p, j).wait_send()
                d2d(p, j, c).wait_send()
                d2d(p, j, 1 - c).wait_recv()

    cargo.start, cargo.forward, cargo.finish = start, forward, finish
    return cargo


def _reduce_cargo(grads, presums):
    cargo = _Cargo()
    na, nb = len(grads), len(presums)
    nflip = len(CHIP_FLIPS)
    cargo.operands = list(grads) + list(presums)
    cargo.out_shapes = ([jax.ShapeDtypeStruct((g.shape[0], g.shape[1] // 2, g.shape[2]), g.dtype) for g in grads]
                        + [jax.ShapeDtypeStruct((nflip,) + ps.shape[1:], ps.dtype) for ps in presums])
    cargo.sem_shapes = ([pltpu.SemaphoreType.DMA((na,))] * 2 if na else []) + (
        [pltpu.SemaphoreType.DMA((nb * nflip,))] * 2 if nb else [])

    def copies(ins, outs, sems):
        x, y, c = _mesh_pos()
        out = []
        if na:
            send, recv = sems[0], sems[1]
            for a in range(na):
                hrows = grads[a].shape[1] // 2
                out.append(_remote(ins[a].at[:, pl.ds((1 - c) * hrows, hrows), :], outs[a], send.at[a], recv.at[a],
                                   (x, y, 1 - c)))
        if nb:
            send, recv = sems[-2], sems[-1]
            for b in range(nb):
                for j in range(nflip):
                    px, py = _other_chip(x, y, j)
                    out.append(_remote(ins[na + b].at[2 * px + py], outs[na + b].at[j], send.at[b * nflip + j],
                                       recv.at[b * nflip + j], (px, py, c)))
        return out

    def start(ins, outs, sems):
        for cp in copies(ins, outs, sems):
            cp.start()

    def finish(ins, outs, sems):
        for cp in copies(ins, outs, sems):
            cp.wait()

    cargo.start, cargo.finish = start, finish
    return cargo


def _run_cargo(name, cargo):
    nin, nout = len(cargo.operands), len(cargo.out_shapes)

    def body(*refs):
        ins, outs, sems = refs[:nin], refs[nin:nin + nout], refs[nin + nout:]
        cargo.start(ins, outs, sems)
        cargo.forward(ins, outs, sems)
        cargo.finish(ins, outs, sems)

    return list(pl.pallas_call(
        body, name=name,
        in_specs=_any_specs(nin), out_specs=_any_specs(nout), out_shape=cargo.out_shapes,
        input_output_aliases=dict(cargo.aliases), scratch_shapes=list(cargo.sem_shapes),
    )(*cargo.operands))


def _join_siblings(bufs):
    nb = len(bufs)
    items = [(b, layer) for b, buf in enumerate(bufs) for layer in range(buf.shape[0])]

    def body(*refs):
        outs = refs[nb:2 * nb]
        send, recv = refs[2 * nb:]
        x, y, c = _mesh_pos()

        def half(i, cc):
            b, layer = items[i]
            hrows = bufs[b].shape[1] // 2
            blk = outs[b].at[layer, pl.ds(cc * hrows, hrows), :]
            return _remote(blk, blk, send.at[i], recv.at[i], (x, y, 1 - c))

        for i in range(len(items)):
            half(i, c).start()
        for i in range(len(items)):
            half(i, c).wait_send()
            half(i, 1 - c).wait_recv()

    return list(pl.pallas_call(
        body, name="grad_sibling_join",
        in_specs=_any_specs(nb), out_specs=_any_specs(nb),
        out_shape=[jax.ShapeDtypeStruct(b.shape, b.dtype) for b in bufs],
        input_output_aliases={i: i for i in range(nb)},
        scratch_shapes=[pltpu.SemaphoreType.DMA((len(items),))] * 2,
    )(*bufs))


def _cast_place(w3, pos, dtype):
    L, rows, cols = w3.shape

    def body(pos_ref, w_ref, o_ref):
        o_ref[...] = w_ref[...].astype(dtype)

    return pl.pallas_call(
        body, name="cast_place",
        grid_spec=pltpu.PrefetchScalarGridSpec(
            num_scalar_prefetch=1, grid=(L,),
            in_specs=[pl.BlockSpec((None, rows, cols), lambda l, pos: (l, 0, 0))],
            out_specs=pl.BlockSpec((None, None, rows, cols), lambda l, pos: (l, pos[0], 0, 0))),
        out_shape=jax.ShapeDtypeStruct((L, N_CHIPS, rows, cols), dtype),
        compiler_params=_cparams(("arbitrary",)),
    )(pos, w3)


def _allreduce_small(buf):
    R = buf.shape[0]
    half = R // 2
    assert half % SUBLANES == 0, R

    def body(in_ref, out_ref, land, send, recv):
        x, y, c = _mesh_pos()
        out_ref[...] = in_ref[...]
        cp = _remote(out_ref, land.at[0], send.at[0], recv.at[0], (x, y, 1 - c))
        cp.start()
        cp.wait()
        out_ref[...] = out_ref[...] + land[0]
        along_y, along_x = (x, 1 - y, c), (1 - x, y, c)
        lo, hi = pl.ds(0, half), pl.ds(half, half)
        for stage, (peer_lo, peer_hi) in enumerate(((along_y, along_x), (along_x, along_y))):
            slot = 1 + stage
            cps = [_remote(out_ref.at[lo], land.at[slot, lo], send.at[1 + 2 * stage], recv.at[1 + 2 * stage], peer_lo),
                   _remote(out_ref.at[hi], land.at[slot, hi], send.at[2 + 2 * stage], recv.at[2 + 2 * stage], peer_hi)]
            for cp in cps:
                cp.start()
            for cp in cps:
                cp.wait()
            out_ref[...] = out_ref[...] + land[slot]

    return pl.pallas_call(
        body, name="allreduce_small",
        in_specs=[pl.BlockSpec(memory_space=pltpu.VMEM)],
        out_specs=pl.BlockSpec(memory_space=pltpu.VMEM),
        out_shape=jax.ShapeDtypeStruct((R, LANES), F32),
        scratch_shapes=[pltpu.VMEM((3, R, LANES), F32), pltpu.SemaphoreType.DMA((5,)), pltpu.SemaphoreType.DMA((5,))],
        compiler_params=pltpu.CompilerParams(vmem_limit_bytes=VMEM_LIMIT_MB * 2 ** 20),
    )(buf)


def _presum_with_sibling(grad, landed, pos):
    nchunk, rows, cols = grad.shape
    hrows = rows // 2

    def body(pos_ref, g_ref, l_ref, all_ref, own_ref):
        s = g_ref[...] + l_ref[...]
        all_ref[...] = s.astype(BF16)

        @pl.when(pl.program_id(0) == pos_ref[0])
        def _():
            own_ref[...] = s

    return pl.pallas_call(
        body, name="grad_presum",
        grid_spec=pltpu.PrefetchScalarGridSpec(
            num_scalar_prefetch=1, grid=(nchunk,),
            in_specs=[pl.BlockSpec((None, hrows, cols), lambda k, pos: (k, pos[1], 0)),
                      pl.BlockSpec((None, hrows, cols), lambda k, pos: (k, 0, 0))],
            out_specs=[pl.BlockSpec((None, hrows, cols), lambda k, pos: (k, 0, 0)),
                       pl.BlockSpec((hrows, cols), lambda k, pos: (0, 0))]),
        out_shape=[jax.ShapeDtypeStruct((nchunk, hrows, cols), BF16), jax.ShapeDtypeStruct((hrows, cols), F32)],
        compiler_params=_cparams(("arbitrary",)),
    )(pos, grad, landed)


def _sum_chips(own, landed, stacked, layer, shape3, pos):
    hrows, cols = own.shape

    def body(pos_ref, o_ref, l_ref, *rest):
        s = o_ref[...]
        for j in range(len(CHIP_FLIPS)):
            s = s + l_ref[j].astype(F32)
        rest[-1][...] = s

    in_specs = [pl.BlockSpec((hrows, cols), lambda i, pos: (0, 0)),
                pl.BlockSpec((len(CHIP_FLIPS), hrows, cols), lambda i, pos: (0, 0, 0))]
    args = [pos, own, landed]
    aliases = {}
    if stacked is not None:
        in_specs.append(pl.BlockSpec(memory_space=pl.ANY))
        args.append(stacked)
        aliases = {3: 0}
    return pl.pallas_call(
        body, name="grad_sum_chips",
        grid_spec=pltpu.PrefetchScalarGridSpec(
            num_scalar_prefetch=1, grid=(1,), in_specs=in_specs,
            out_specs=pl.BlockSpec((None, hrows, cols), lambda i, pos: (layer, pos[1], 0))),
        out_shape=jax.ShapeDtypeStruct(shape3, F32),
        input_output_aliases=aliases,
        compiler_params=_cparams(("arbitrary",)),
    )(*args)


def _adamw(w, g, m, v):
    R, C = w.shape
    rb = R
    for cand in (512, 352, 320, 256, 128, 64, 32, 16, 8):
        if R % cand == 0:
            rb = cand
            break
    c1 = 1.0 - ADAM_B1 ** ADAM_STEP
    c2 = 1.0 - ADAM_B2 ** ADAM_STEP

    def body(w_ref, g_ref, m_ref, v_ref, go_ref, d_ref, mo_ref, vo_ref):
        gv = g_ref[...]
        go_ref[...] = gv
        m2 = ADAM_B1 * m_ref[...] + (1.0 - ADAM_B1) * gv
        v2 = ADAM_B2 * v_ref[...] + (1.0 - ADAM_B2) * (gv * gv)
        mo_ref[...] = m2
        vo_ref[...] = v2
        d_ref[...] = -ADAM_LR * ((m2 / c1) / (jnp.sqrt(v2 / c2) + ADAM_EPS) + ADAM_WD * w_ref[...])

    spec = pl.BlockSpec((rb, C), lambda i: (i, 0))
    return pl.pallas_call(
        body, name="adamw",
        grid=(R // rb,),
        in_specs=[spec] * 4, out_specs=[spec] * 4,
        out_shape=[jax.ShapeDtypeStruct((R, C), F32)] * 4,
        compiler_params=_cparams(("arbitrary",)),
    )(w, g, m, v)


def _pack(parts, align=SUBLANES * LANES):
    flat = jnp.concatenate([p.reshape(-1).astype(F32) for p in parts])
    pad = (-flat.shape[0]) % align
    return jnp.pad(flat, (0, pad)).reshape(-1, LANES)


def _unpack(buf, shapes):
    flat = buf.reshape(-1)
    out, off = [], 0
    for shp in shapes:
        size = 1
        for d in shp:
            size *= d
        out.append(flat[off:off + size].reshape(shp))
        off += size
    return out


def _block_diag_gates(w_a, w_x):
    nq = D_RNN // GATE_CHUNK
    hpc = LRU_HEADS // nq
    eye = jnp.eye(hpc, dtype=F32)

    def bd(w):
        wq = w.reshape(nq, hpc, LRU_HEAD_DIM, LRU_HEAD_DIM)
        return (wq[:, :, :, None, :] * eye[None, :, None, :, None]).reshape(nq, GATE_CHUNK, GATE_CHUNK)

    return jnp.concatenate([bd(w_a), bd(w_x)], axis=2).astype(BF16)


def _block_diag_extract(dwbd):
    nq = D_RNN // GATE_CHUNK
    hpc = LRU_HEADS // nq
    eye = jnp.eye(hpc, dtype=F32)

    def ex(d):
        d5 = d.reshape(nq, hpc, LRU_HEAD_DIM, hpc, LRU_HEAD_DIM)
        return jnp.sum(d5 * eye[None, :, None, :, None], axis=3).reshape(LRU_HEADS, LRU_HEAD_DIM, LRU_HEAD_DIM)

    return ex(dwbd[:, :, :GATE_CHUNK]), ex(dwbd[:, :, GATE_CHUNK:])


BIG = ("ffn1_w_gate", "ffn1_w_up", "ffn1_w_down", "lru_w_in", "lru_w_out", "pool_w",
       "ffn2_w_gate", "ffn2_w_up", "ffn2_w_down", "ple_w_gate", "ple_w_proj")
TINY_SHARDED = ("lru_conv_w", "pool_b", "pool_scale")
REPLICATED = ("ffn1_norm", "mix_norm", "lru_conv_b", "lru_w_a", "lru_b_a", "lru_w_x", "lru_b_x", "lru_a_param",
              "ffn2_norm", "ple_norm", "final_norm")
WEIGHT_ORDER = ("ffn1_norm", "ffn1_w_gate", "ffn1_w_up", "ffn1_w_down", "mix_norm", "lru_w_in", "lru_conv_w",
                "lru_conv_b", "lru_w_a", "lru_b_a", "lru_w_x", "lru_b_x", "lru_a_param", "lru_w_out", "pool_w",
                "pool_b", "pool_scale", "ffn2_norm", "ffn2_w_gate", "ffn2_w_up", "ffn2_w_down", "ple_norm",
                "ple_w_gate", "ple_w_proj", "final_norm")


TRANSPOSED = ("ffn1_w_gate", "ffn1_w_up", "ffn2_w_gate", "ffn2_w_up")


def _stored(name, a):
    return jnp.swapaxes(a, 1, 2) if name in TRANSPOSED else a


def _as3(a):
    return a.reshape(a.shape[0], -1, a.shape[-1])


def kernel(x, p, ffn1_norm, ffn1_w_gate, ffn1_w_up, ffn1_w_down, mix_norm, lru_w_in, lru_conv_w, lru_conv_b, lru_w_a, lru_b_a, lru_w_x, lru_b_x, lru_a_param, lru_w_out, pool_w, pool_b, pool_scale, ffn2_norm, ffn2_w_gate, ffn2_w_up, ffn2_w_down, ple_norm, ple_w_gate, ple_w_proj, final_norm, loss_target, m_ffn1_norm, m_ffn1_w_gate, m_ffn1_w_up, m_ffn1_w_down, m_mix_norm, m_lru_w_in, m_lru_conv_w, m_lru_conv_b, m_lru_w_a, m_lru_b_a, m_lru_w_x, m_lru_b_x, m_lru_a_param, m_lru_w_out, m_pool_w, m_pool_b, m_pool_scale, m_ffn2_norm, m_ffn2_w_gate, m_ffn2_w_up, m_ffn2_w_down, m_ple_norm, m_ple_w_gate, m_ple_w_proj, m_final_norm, v_ffn1_norm, v_ffn1_w_gate, v_ffn1_w_up, v_ffn1_w_down, v_mix_norm, v_lru_w_in, v_lru_conv_w, v_lru_conv_b, v_lru_w_a, v_lru_b_a, v_lru_w_x, v_lru_b_x, v_lru_a_param, v_lru_w_out, v_pool_w, v_pool_b, v_pool_scale, v_ffn2_norm, v_ffn2_w_gate, v_ffn2_w_up, v_ffn2_w_down, v_ple_norm, v_ple_w_gate, v_ple_w_proj, v_final_norm):
    W = dict(ffn1_norm=ffn1_norm, ffn1_w_gate=ffn1_w_gate, ffn1_w_up=ffn1_w_up, ffn1_w_down=ffn1_w_down,
             mix_norm=mix_norm, lru_w_in=lru_w_in, lru_conv_w=lru_conv_w, lru_conv_b=lru_conv_b, lru_w_a=lru_w_a,
             lru_b_a=lru_b_a, lru_w_x=lru_w_x, lru_b_x=lru_b_x, lru_a_param=lru_a_param, lru_w_out=lru_w_out,
             pool_w=pool_w, pool_b=pool_b, pool_scale=pool_scale, ffn2_norm=ffn2_norm, ffn2_w_gate=ffn2_w_gate,
             ffn2_w_up=ffn2_w_up, ffn2_w_down=ffn2_w_down, ple_norm=ple_norm, ple_w_gate=ple_w_gate,
             ple_w_proj=ple_w_proj, final_norm=final_norm)
    M = dict(ffn1_norm=m_ffn1_norm, ffn1_w_gate=m_ffn1_w_gate, ffn1_w_up=m_ffn1_w_up, ffn1_w_down=m_ffn1_w_down,
             mix_norm=m_mix_norm, lru_w_in=m_lru_w_in, lru_conv_w=m_lru_conv_w, lru_conv_b=m_lru_conv_b,
             lru_w_a=m_lru_w_a, lru_b_a=m_lru_b_a, lru_w_x=m_lru_w_x, lru_b_x=m_lru_b_x, lru_a_param=m_lru_a_param,
             lru_w_out=m_lru_w_out, pool_w=m_pool_w, pool_b=m_pool_b, pool_scale=m_pool_scale, ffn2_norm=m_ffn2_norm,
             ffn2_w_gate=m_ffn2_w_gate, ffn2_w_up=m_ffn2_w_up, ffn2_w_down=m_ffn2_w_down, ple_norm=m_ple_norm,
             ple_w_gate=m_ple_w_gate, ple_w_proj=m_ple_w_proj, final_norm=m_final_norm)
    V = dict(ffn1_norm=v_ffn1_norm, ffn1_w_gate=v_ffn1_w_gate, ffn1_w_up=v_ffn1_w_up, ffn1_w_down=v_ffn1_w_down,
             mix_norm=v_mix_norm, lru_w_in=v_lru_w_in, lru_conv_w=v_lru_conv_w, lru_conv_b=v_lru_conv_b,
             lru_w_a=v_lru_w_a, lru_b_a=v_lru_b_a, lru_w_x=v_lru_w_x, lru_b_x=v_lru_b_x, lru_a_param=v_lru_a_param,
             lru_w_out=v_lru_w_out, pool_w=v_pool_w, pool_b=v_pool_b, pool_scale=v_pool_scale, ffn2_norm=v_ffn2_norm,
             ffn2_w_gate=v_ffn2_w_gate, ffn2_w_up=v_ffn2_w_up, ffn2_w_down=v_ffn2_w_down, ple_norm=v_ple_norm,
             ple_w_gate=v_ple_w_gate, ple_w_proj=v_ple_w_proj, final_norm=v_final_norm)

    S = x.shape[1]
    my_x, my_y, my_c = _mesh_pos()
    my_chip = 2 * my_x + my_y
    pos = jnp.stack([my_chip, my_c]).astype(jnp.int32)
    n_lru, n_pool = lru_w_in.shape[0], pool_w.shape[0]

    tiny_shapes = [W[n].shape for n in TINY_SHARDED]
    tiny_local = _pack([W[n] for n in TINY_SHARDED], align=2 * 16 * LANES)[None]
    Ws, Ms, Vs = ({n: _stored(n, d[n]) for n in BIG} for d in (W, M, V))
    bufs = {n: _cast_place(_as3(Ws[n]), pos, BF16) for n in BIG}
    bufs["tiny"] = _cast_place(tiny_local, pos, F32)

    def gather_now(name, pieces):
        cargo = _gather_cargo(bufs, pieces)
        bufs.update(zip(cargo.names, _run_cargo(name, cargo)))

    def ffn_pieces(which, layer):
        return [("%s_w_gate" % which, layer), ("%s_w_up" % which, layer), ("%s_w_down" % which, layer)]

    def mixer_pieces(layer):
        if layer % 2 == 0:
            return [("lru_w_in", layer // 2), ("lru_w_out", layer // 2)]
        return [("pool_w", layer // 2)]

    gather_now("gather_first", [("tiny", 0)] + ffn_pieces("ffn1", 0))
    tiny_by_chip = [_unpack(bufs["tiny"][0, k], tiny_shapes) for k in range(N_CHIPS)]
    conv_w_full = jnp.concatenate([tiny_by_chip[k][0] for k in range(N_CHIPS)], axis=-1)
    pool_b_full = jnp.concatenate([tiny_by_chip[k][1] for k in range(N_CHIPS)], axis=-1)
    pool_s_full = jnp.concatenate([tiny_by_chip[k][2] for k in range(N_CHIPS)], axis=-1)
    ngroup = len(POOL_WINDOWS)

    def pool_weights():
        pw5 = bufs["pool_w"].reshape(n_pool, N_CHIPS, ngroup, POOL_GROUP_DIM // N_CHIPS, POOL_GROUP_DIM)
        return pw5.transpose(0, 2, 1, 3, 4).reshape(n_pool, ngroup, POOL_GROUP_DIM, POOL_GROUP_DIM)

    lru_out = lambda: bufs["lru_w_out"].reshape(n_lru, D_RNN, D_MODEL)
    ple_gate = lambda: bufs["ple_w_gate"].reshape(DEPTH, D_MODEL, D_MODEL)
    wbd = [_block_diag_gates(lru_w_a[j], lru_w_x[j]) for j in range(n_lru)]

    def ffn_forward(which, h, gamma, layer, pieces):
        cargo = _gather_cargo(bufs, pieces)
        outs, updated = _ffn_fwd(h, gamma, bufs[which + "_w_gate"], bufs[which + "_w_up"], bufs[which + "_w_down"],
                                 layer, cargo)
        bufs.update(zip(cargo.names, updated))
        return outs

    h = x.reshape(S, D_MODEL)
    saved = []
    for i in range(DEPTH):
        j = i // 2
        sv = {"h0": h}
        first_mixer = mixer_pieces(0) if i == 0 else []
        h, sv["xn1"], sv["g1"], sv["u1"] = ffn_forward(
            "ffn1", h, _Rows(ffn1_norm, i), i,
            first_mixer + ffn_pieces("ffn2", i) + [("ple_w_gate", i), ("ple_w_proj", i)])
        sv["h1"] = h
        if i % 2 == 0:
            h, sv["xn_mix"], sv["z"], sv["hs"], sv["gates"] = _lru_fwd(
                h, _Rows(mix_norm, i), bufs["lru_w_in"], j, _Rows(conv_w_full, j), _Rows(lru_conv_b, j), wbd[j],
                _Rows(lru_b_a, j), _Rows(lru_b_x, j), _Rows(lru_a_param, j), lru_out())
        else:
            h, sv["u"] = _pool_fwd(h, _Rows(mix_norm, i), pool_weights(), _Rows(pool_b_full, j), _Rows(pool_s_full, j), j)
        sv["h2"] = h
        nxt = ffn_pieces("ffn1", i + 1) + mixer_pieces(i + 1) if i + 1 < DEPTH else []
        h, sv["xn2"], sv["g2"], sv["u2"] = ffn_forward("ffn2", h, _Rows(ffn2_norm, i), i, nxt)
        sv["h3"] = h
        h = _ple_fwd(h, _Rows(ple_norm, i), p, ple_gate(), bufs["ple_w_proj"], i)
        saved.append(sv)

    dh, dgam_final, loss_part = _final(h, _Rows(final_norm.reshape(1, -1), 0), loss_target.reshape(S, D_MODEL))
    win, wout, wpg, wpp, pw = bufs["lru_w_in"], lru_out(), ple_gate(), bufs["ple_w_proj"], pool_weights()

    norm_grads = {n: [None] * DEPTH for n in ("ffn1_norm", "mix_norm", "ffn2_norm", "ple_norm")}
    lru_vec = [None] * n_lru
    pool_vec = [None] * n_pool
    sum8 = lambda a: jnp.sum(a, axis=-2)

    to_siblings, to_chips = [], []
    stacked = {n: None for n in BIG}

    def take_cargo(with_chips=True):
        a_items, b_items = list(to_siblings), list(to_chips) if with_chips else []
        del to_siblings[:], to_chips[:len(b_items)]
        return _reduce_cargo([it[2] for it in a_items], [it[2] for it in b_items]), a_items, b_items

    def absorb(a_items, b_items, outs):
        for (n, layer, g), landed in zip(a_items, outs[:len(a_items)]):
            all_chunks, own = _presum_with_sibling(g, landed, pos)
            to_chips.append((n, layer, all_chunks, own))
        for (n, layer, _, own), from_chips in zip(b_items, outs[len(a_items):]):
            stacked[n] = _sum_chips(own, from_chips, stacked[n], layer, _as3(Ws[n]).shape, pos)

    def ffn_backward(which, xn, dout, gg, uu, layer, h_in, gamma):
        cargo, a_items, b_items = take_cargo()
        weights = (bufs[which + "_w_gate"], bufs[which + "_w_up"], bufs[which + "_w_down"])
        (dwg, dwu, dwd, slabs), c_outs = _ffn_bwd(xn, dout, gg, uu, *weights, layer, cargo)
        absorb(a_items, b_items, c_outs)
        dh_in, dgam, dwg, dwu, dwd = _ffn_bwd_last(xn, dout, gg, uu, *weights, layer, slabs, h_in, gamma,
                                                   dwg, dwu, dwd)
        to_siblings.extend([(which + "_w_gate", layer, dwg), (which + "_w_up", layer, dwu),
                            (which + "_w_down", layer, dwd)])
        return dh_in, sum8(dgam)

    for i in reversed(range(DEPTH)):
        j = i // 2
        sv = saved[i]
        dh, dw_pg, dw_pp, dgam = _ple_bwd(dh, sv["h3"], _Rows(ple_norm, i), p, wpg, wpp, i)
        norm_grads["ple_norm"][i] = sum8(dgam)
        to_siblings.append(("ple_w_gate", i, dw_pg.reshape(N_CHIPS, D_MODEL // N_CHIPS, D_MODEL)))
        to_siblings.append(("ple_w_proj", i, dw_pp))

        dh, norm_grads["ffn2_norm"][i] = ffn_backward("ffn2", sv["xn2"], dh, sv["g2"], sv["u2"], i, sv["h2"],
                                                      _Rows(ffn2_norm, i))

        if i % 2 == 0:
            cargo, a_items, b_items = take_cargo()
            (dz, dpre, xc_b, y_b, dcw, vec), c_outs = _lru_bwd_seq(
                dh, sv["z"], sv["hs"], sv["gates"], _Rows(conv_w_full, j), wbd[j], _Rows(lru_a_param, j), wout, j, cargo)
            absorb(a_items, b_items, c_outs)
            to_siblings.append(("lru_w_out", j, _xt_dy("lru_dw_out", y_b, dh, 1, D_RNN, D_MODEL, False, False)
                                .reshape(N_CHIPS, D_RNN // N_CHIPS, D_MODEL)))
            to_siblings.append(("lru_w_in", j, _xt_dy("lru_dw_in", sv["xn_mix"], dz, N_CHIPS, D_MODEL, RNN_IN_CHUNK,
                                                      False, True)))
            dwbd = _xt_dy("lru_dw_gates", xc_b, dpre, D_RNN // GATE_CHUNK, GATE_CHUNK, 2 * GATE_CHUNK, True, True)
            dw_a, dw_x = _block_diag_extract(dwbd)
            vsum = sum8(vec)
            lru_vec[j] = (sum8(dcw), vsum[0], vsum[1], vsum[2], vsum[3], dw_a, dw_x)
            cargo, a_items, b_items = take_cargo(with_chips=False)
            (dh, dgam), c_outs = _lru_bwd_in(dz, sv["h1"], _Rows(mix_norm, i), dh, win, j, cargo)
            absorb(a_items, b_items, c_outs)
            norm_grads["mix_norm"][i] = sum8(dgam)
        else:
            dh_new, dpre_b, vec = _pool_bwd(dh, sv["h1"], sv["u"], _Rows(mix_norm, i), pw, _Rows(pool_b_full, j),
                                            _Rows(pool_s_full, j), j)
            dpw = _xt_dy("pool_dw", sv["u"], dpre_b, ngroup, POOL_GROUP_DIM, POOL_GROUP_DIM, True, True)
            dpw = dpw.reshape(ngroup, N_CHIPS, POOL_GROUP_DIM // N_CHIPS, POOL_GROUP_DIM).transpose(1, 0, 2, 3)
            to_siblings.append(("pool_w", j, dpw.reshape(N_CHIPS, POOL_GROUP_DIM, POOL_GROUP_DIM)))
            vsum = sum8(vec)
            norm_grads["mix_norm"][i] = vsum[0]
            pool_vec[j] = (vsum[1], vsum[2])
            dh = dh_new

        dh, norm_grads["ffn1_norm"][i] = ffn_backward("ffn1", sv["xn1"], dh, sv["g1"], sv["u1"], i, sv["h0"],
                                                      _Rows(ffn1_norm, i))

    grad_x = dh.reshape(1, S, D_MODEL)

    small_parts = [
        jnp.stack(norm_grads["ffn1_norm"]), jnp.stack(norm_grads["mix_norm"]),
        jnp.stack(norm_grads["ffn2_norm"]), jnp.stack(norm_grads["ple_norm"]), sum8(dgam_final),
        jnp.stack([lv[0] for lv in lru_vec]), jnp.stack([lv[1] for lv in lru_vec]),
        jnp.stack([lv[2] for lv in lru_vec]), jnp.stack([lv[3] for lv in lru_vec]),
        jnp.stack([lv[4] for lv in lru_vec]), jnp.stack([lv[5] for lv in lru_vec]),
        jnp.stack([lv[6] for lv in lru_vec]),
        jnp.stack([pv[0] for pv in pool_vec]), jnp.stack([pv[1] for pv in pool_vec]),
        jnp.sum(loss_part).reshape(1),
    ]
    small_names = ("ffn1_norm", "mix_norm", "ffn2_norm", "ple_norm", "final_norm", "lru_conv_w", "lru_conv_b",
                   "lru_b_a", "lru_b_x", "lru_a_param", "lru_w_a", "lru_w_x", "pool_b", "pool_scale", "loss")
    reduced = _unpack(_allreduce_small(_pack(small_parts, align=2 * SUBLANES * LANES)),
                      [sp.shape for sp in small_parts])
    small_grad = dict(zip(small_names, reduced))
    loss = small_grad.pop("loss").reshape(())
    for n in TINY_SHARDED:
        width = W[n].shape[-1]
        small_grad[n] = lax.dynamic_slice_in_dim(small_grad[n], my_chip * width, width, axis=-1)

    tail = 0
    while to_siblings or to_chips:
        cargo, a_items, b_items = take_cargo()
        absorb(a_items, b_items, _run_cargo("grad_exchange_tail%d" % tail, cargo))
        tail += 1
    big_final = dict(zip(BIG, _join_siblings([stacked[n] for n in BIG])))

    grads, deltas, new_m, new_v = {}, {}, {}, {}
    for n in BIG:
        shp = Ws[n].shape
        to2 = lambda a: a.reshape(-1, shp[-1])
        g2, d, m2, v2 = _adamw(to2(Ws[n]), to2(big_final[n]), to2(Ms[n]), to2(Vs[n]))
        grads[n], deltas[n], new_m[n], new_v[n] = (_stored(n, a.reshape(shp)) for a in (g2, d, m2, v2))
    small_order = TINY_SHARDED + REPLICATED
    small_shapes = [W[n].shape for n in small_order]
    pack_rows = functools.partial(_pack, align=512 * LANES)
    _, sd, sm, sv_ = _adamw(pack_rows([W[n] for n in small_order]), pack_rows([small_grad[n] for n in small_order]),
                            pack_rows([M[n] for n in small_order]), pack_rows([V[n] for n in small_order]))
    for n, d, m2, v2 in zip(small_order, _unpack(sd, small_shapes), _unpack(sm, small_shapes),
                            _unpack(sv_, small_shapes)):
        grads[n], deltas[n], new_m[n], new_v[n] = small_grad[n].reshape(W[n].shape), d, m2, v2

    return (loss, grad_x, *[grads[n] for n in WEIGHT_ORDER], *[deltas[n] for n in WEIGHT_ORDER],
            *[new_m[n] for n in WEIGHT_ORDER], *[new_v[n] for n in WEIGHT_ORDER])
```

```python
import functools

import jax
import jax.numpy as jnp
from jax import lax
from jax.experimental import pallas as pl
from jax.experimental.pallas import tpu as pltpu

F32 = jnp.float32
BF16 = jnp.bfloat16

D_MODEL = 1024
D_FF = 2816
D_RNN = 1280
DEPTH = 4
N_CHIPS = 4
FF_CHUNK = D_FF // N_CHIPS
RNN_IN_CHUNK = 2 * D_RNN // N_CHIPS
GATE_CHUNK = 640
N_GATE_PLANES = 5
LRU_HEADS = 16
LRU_HEAD_DIM = 80
CONV_WIDTH = 4
LRU_C = 8.0
POOL_WINDOWS = (2, 4, 8, 16)
POOL_GROUP_DIM = 256
PLE_DIM = 256
RMS_EPS = 1e-6
POOL_HALO = 16
SUBLANES = 8
LANES = 128

ADAM_LR = 0.001
ADAM_B1 = 0.9
ADAM_B2 = 0.999
ADAM_EPS = 1e-08
ADAM_WD = 0.01
ADAM_STEP = 10

VMEM_LIMIT_MB = 56
MESH_ID = pl.DeviceIdType.MESH
CHIP_FLIPS = ((1, 0), (0, 1), (1, 1))


def _cparams(semantics):
    return pltpu.CompilerParams(dimension_semantics=semantics, vmem_limit_bytes=VMEM_LIMIT_MB * 2 ** 20)


def _dot(a, b):
    return lax.dot_general(a, b, (((1,), (0,)), ((), ())), preferred_element_type=F32)


def _dot_nt(a, b):
    return lax.dot_general(a, b, (((1,), (1,)), ((), ())), preferred_element_type=F32)


def _dot_tn(a, b):
    return lax.dot_general(a, b, (((0,), (0,)), ((), ())), preferred_element_type=F32)


def _sigmoid(x):
    return 1.0 / (1.0 + jnp.exp(-x))


def _rms(hf, gamma):
    rstd = lax.rsqrt(jnp.mean(hf * hf, axis=-1, keepdims=True) + RMS_EPS)
    xhat = hf * rstd
    return xhat, rstd, xhat * gamma


def _rms_bwd(xhat, rstd, gamma, dxn):
    dxhat = dxn * gamma
    m = jnp.mean(dxhat * xhat, axis=-1, keepdims=True)
    return rstd * (dxhat - xhat * m), _rowsum8(dxn * xhat)


def _rowsum8(v):
    tm, n = v.shape
    return jnp.sum(v.reshape(tm // SUBLANES, SUBLANES, n), axis=0)


def _gelu(x):
    u = 0.7978845608028654 * (x + 0.044715 * x * x * x)
    return 0.5 * x * (1.0 + jnp.tanh(u))


def _gelu_and_grad(x):
    c = 0.7978845608028654
    u = c * (x + 0.044715 * x * x * x)
    th = jnp.tanh(u)
    g = 0.5 * x * (1.0 + th)
    dg = 0.5 * (1.0 + th) + 0.5 * x * (1.0 - th * th) * c * (1.0 + 3.0 * 0.044715 * x * x)
    return g, dg


def _softplus(z):
    e = jnp.exp(-jnp.abs(z))
    u = 1.0 + e
    log1p = jnp.where(u == 1.0, e, jnp.log(u) * e / jnp.where(u == 1.0, 1.0, u - 1.0))
    return jnp.maximum(z, 0.0) + log1p


def _neg_expm1(x):
    series = -x * (1.0 + x * (0.5 + x * (1.0 / 6.0)))
    return jnp.where(x > -1e-2, series, 1.0 - jnp.exp(x))


def _shift_down(ext, j, halo):
    return pltpu.roll(ext, j, 0)[halo:]


def _shift_up(ext, j, tm):
    n = ext.shape[0]
    return pltpu.roll(ext, n - j, 0)[:tm]


def _scan_causal(a, b):
    tm, n = a.shape
    head_rows = lax.broadcasted_iota(jnp.int32, (SUBLANES, n), 0)
    s = 1
    while s < min(SUBLANES, tm):
        keep = head_rows >= s
        a_r, b_r = pltpu.roll(a, s, 0), pltpu.roll(b, s, 0)
        a_sh = jnp.concatenate([jnp.where(keep, a_r[:SUBLANES], 1.0), a_r[SUBLANES:]], axis=0)
        b_sh = jnp.concatenate([jnp.where(keep, b_r[:SUBLANES], 0.0), b_r[SUBLANES:]], axis=0)
        b = a * b_sh + b
        a = a * a_sh
        s *= 2
    while s < tm:
        b = jnp.concatenate([b[:s], a[s:] * b[:tm - s] + b[s:]], axis=0)
        a = jnp.concatenate([a[:s], a[s:] * a[:tm - s]], axis=0)
        s *= 2
    return a, b


def _scan_anticausal(c, d):
    tm, n = c.shape
    body = tm - SUBLANES
    tail_rows = lax.broadcasted_iota(jnp.int32, (SUBLANES, n), 0) + body
    s = 1
    while s < min(SUBLANES, tm):
        keep = tail_rows < tm - s
        c_r, d_r = pltpu.roll(c, tm - s, 0), pltpu.roll(d, tm - s, 0)
        c_sh = jnp.concatenate([c_r[:body], jnp.where(keep, c_r[body:], 1.0)], axis=0)
        d_sh = jnp.concatenate([d_r[:body], jnp.where(keep, d_r[body:], 0.0)], axis=0)
        d = d + c * d_sh
        c = c * c_sh
        s *= 2
    while s < tm:
        d = jnp.concatenate([d[:tm - s] + c[:tm - s] * d[s:], d[tm - s:]], axis=0)
        c = jnp.concatenate([c[:tm - s] * c[s:], c[tm - s:]], axis=0)
        s *= 2
    return c, d


def _rows3(stacked):
    return stacked.reshape(stacked.shape[0], 1, stacked.shape[-1])


class _Rows:
    def __init__(self, stacked3, index):
        assert stacked3.ndim == 3, stacked3.shape
        self.array = stacked3
        self.index = index

    def spec(self):
        index = self.index
        return pl.BlockSpec((None,) + self.array.shape[1:], lambda *_: (index, 0, 0))


def _tile(n, want):
    t = min(n, want)
    assert n % t == 0, (n, t)
    return t


def _ffn_fwd(h, gamma, wg, wu, wd, layer, cargo=None):
    S = h.shape[0]
    tm = _tile(S, 1024)
    nt = S // tm
    cargo = cargo or _Cargo()
    n_in, n_out = 5, 4
    nc_in, nc_out = len(cargo.operands), len(cargo.out_shapes)

    def body(*refs):
        h_ref, g_ref, wg_ref, wu_ref, wd_ref = refs[:n_in]
        c_ins = refs[n_in:n_in + nc_in]
        ho_ref, xn_ref, gg_ref, uu_ref = refs[n_in + nc_in:n_in + nc_in + n_out]
        c_outs = refs[n_in + nc_in + n_out:n_in + nc_in + n_out + nc_out]
        xn_s, acc_s = refs[n_in + nc_in + n_out + nc_out:n_in + nc_in + n_out + nc_out + 2]
        sems = refs[n_in + nc_in + n_out + nc_out + 2:]
        t, k = pl.program_id(0), pl.program_id(1)

        @pl.when((t == 0) & (k == 0))
        def _():
            cargo.start(c_ins, c_outs, sems)

        @pl.when((t == nt - 1) & (k == 0))
        def _():
            cargo.forward(c_ins, c_outs, sems)

        @pl.when(k == 0)
        def _():
            _, _, xn = _rms(h_ref[...], g_ref[...])
            xnb = xn.astype(BF16)
            xn_s[...] = xnb
            xn_ref[...] = xnb
            acc_s[...] = jnp.zeros_like(acc_s)

        xnb = xn_s[...]
        g = _dot_nt(xnb, wg_ref[...])
        u = _dot_nt(xnb, wu_ref[...])
        gg_ref[...] = g.astype(BF16)
        uu_ref[...] = u.astype(BF16)
        hid = (g * _sigmoid(g)) * u
        acc_s[...] += _dot(hid.astype(BF16), wd_ref[...])

        @pl.when(k == N_CHIPS - 1)
        def _():
            ho_ref[...] = h_ref[...] + 0.5 * acc_s[...]

        @pl.when((t == nt - 1) & (k == N_CHIPS - 1))
        def _():
            cargo.finish(c_ins, c_outs, sems)

    outs = pl.pallas_call(
        body, name="ffn_fwd",
        grid=(nt, N_CHIPS),
        in_specs=[
            pl.BlockSpec((tm, D_MODEL), lambda t, k: (t, 0)),
            gamma.spec(),
            pl.BlockSpec((None, None, FF_CHUNK, D_MODEL), lambda t, k: (layer, k, 0, 0)),
            pl.BlockSpec((None, None, FF_CHUNK, D_MODEL), lambda t, k: (layer, k, 0, 0)),
            pl.BlockSpec((None, None, FF_CHUNK, D_MODEL), lambda t, k: (layer, k, 0, 0)),
        ] + _any_specs(nc_in),
        out_specs=[
            pl.BlockSpec((tm, D_MODEL), lambda t, k: (t, 0)),
            pl.BlockSpec((tm, D_MODEL), lambda t, k: (t, 0)),
            pl.BlockSpec((None, tm, FF_CHUNK), lambda t, k: (k, t, 0)),
            pl.BlockSpec((None, tm, FF_CHUNK), lambda t, k: (k, t, 0)),
        ] + _any_specs(nc_out),
        out_shape=[
            jax.ShapeDtypeStruct((S, D_MODEL), F32),
            jax.ShapeDtypeStruct((S, D_MODEL), BF16),
            jax.ShapeDtypeStruct((N_CHIPS, S, FF_CHUNK), BF16),
            jax.ShapeDtypeStruct((N_CHIPS, S, FF_CHUNK), BF16),
        ] + cargo.out_shapes,
        input_output_aliases={n_in + i: n_out + o for i, o in cargo.aliases.items()},
        scratch_shapes=[pltpu.VMEM((tm, D_MODEL), BF16), pltpu.VMEM((tm, D_MODEL), F32)] + cargo.sem_shapes,
        compiler_params=_cparams(("arbitrary", "arbitrary")),
    )(h, gamma.array, wg, wu, wd, *cargo.operands)
    return outs[:n_out], list(outs[n_out:])


def _ffn_bwd(xn, dout, gg, uu, wg, wu, wd, layer, cargo=None):
    S = xn.shape[0]
    tm = _tile(S, 512)
    nt = S // tm
    nchunk = N_CHIPS - 1
    cargo = cargo or _Cargo()
    n_in, n_out = 7, 4
    nc_in, nc_out = len(cargo.operands), len(cargo.out_shapes)

    def body(*refs):
        xn_ref, do_ref, gg_ref, uu_ref, wg_ref, wu_ref, wd_ref = refs[:n_in]
        c_ins = refs[n_in:n_in + nc_in]
        dwg_ref, dwu_ref, dwd_ref, slab_ref = refs[n_in + nc_in:n_in + nc_in + n_out]
        c_outs = refs[n_in + nc_in + n_out:n_in + nc_in + n_out + nc_out]
        sems = refs[n_in + nc_in + n_out + nc_out:]
        k, t = pl.program_id(0), pl.program_id(1)

        @pl.when((k == 0) & (t == 0))
        def _():
            cargo.start(c_ins, c_outs, sems)

        @pl.when(t == 0)
        def _():
            dwg_ref[...] = jnp.zeros_like(dwg_ref)
            dwu_ref[...] = jnp.zeros_like(dwu_ref)
            dwd_ref[...] = jnp.zeros_like(dwd_ref)

        xnb = xn_ref[...]
        dob = (0.5 * do_ref[...]).astype(BF16)
        g = gg_ref[...].astype(F32)
        u = uu_ref[...].astype(F32)
        s = _sigmoid(g)
        sil = g * s
        dhid = _dot_nt(dob, wd_ref[...])
        dwd_ref[...] += _dot_tn((sil * u).astype(BF16), dob)
        du = (dhid * sil).astype(BF16)
        dg = (dhid * u * (s * (1.0 + g * (1.0 - s)))).astype(BF16)
        dwg_ref[...] += _dot_tn(dg, xnb)
        dwu_ref[...] += _dot_tn(du, xnb)
        slab_ref[...] = (_dot(dg, wg_ref[...]) + _dot(du, wu_ref[...])).astype(BF16)

        @pl.when((k == nchunk - 1) & (t == nt - 1))
        def _():
            cargo.finish(c_ins, c_outs, sems)

    outs = pl.pallas_call(
        body, name="ffn_bwd",
        grid=(nchunk, nt),
        in_specs=[
            pl.BlockSpec((tm, D_MODEL), lambda k, t: (t, 0)),
            pl.BlockSpec((tm, D_MODEL), lambda k, t: (t, 0)),
            pl.BlockSpec((None, tm, FF_CHUNK), lambda k, t: (k, t, 0)),
            pl.BlockSpec((None, tm, FF_CHUNK), lambda k, t: (k, t, 0)),
            pl.BlockSpec((None, None, FF_CHUNK, D_MODEL), lambda k, t: (layer, k, 0, 0)),
            pl.BlockSpec((None, None, FF_CHUNK, D_MODEL), lambda k, t: (layer, k, 0, 0)),
            pl.BlockSpec((None, None, FF_CHUNK, D_MODEL), lambda k, t: (layer, k, 0, 0)),
        ] + _any_specs(nc_in),
        out_specs=[
            pl.BlockSpec((None, FF_CHUNK, D_MODEL), lambda k, t: (k, 0, 0)),
            pl.BlockSpec((None, FF_CHUNK, D_MODEL), lambda k, t: (k, 0, 0)),
            pl.BlockSpec((None, FF_CHUNK, D_MODEL), lambda k, t: (k, 0, 0)),
            pl.BlockSpec((None, tm, D_MODEL), lambda k, t: (k, t, 0)),
        ] + _any_specs(nc_out),
        out_shape=[
            jax.ShapeDtypeStruct((N_CHIPS, FF_CHUNK, D_MODEL), F32),
            jax.ShapeDtypeStruct((N_CHIPS, FF_CHUNK, D_MODEL), F32),
            jax.ShapeDtypeStruct((N_CHIPS, FF_CHUNK, D_MODEL), F32),
            jax.ShapeDtypeStruct((nchunk, S, D_MODEL), BF16),
        ] + cargo.out_shapes,
        input_output_aliases={n_in + i: n_out + o for i, o in cargo.aliases.items()},
        scratch_shapes=list(cargo.sem_shapes),
        compiler_params=_cparams(("arbitrary", "arbitrary")),
    )(xn, dout, gg, uu, wg, wu, wd, *cargo.operands)
    return outs[:n_out], list(outs[n_out:])


def _ffn_bwd_last(xn, dout, gg, uu, wg, wu, wd, layer, slabs, h, gamma, dwg, dwu, dwd):
    S = xn.shape[0]
    tm = _tile(S, 512)
    k = N_CHIPS - 1
    nprev = slabs.shape[0]

    def body(xn_ref, do_ref, gg_ref, uu_ref, wg_ref, wu_ref, wd_ref, slab_ref, h_ref, g_ref, _dwg, _dwu, _dwd,
             dh_ref, dgam_ref, dwg_ref, dwu_ref, dwd_ref):
        @pl.when(pl.program_id(0) == 0)
        def _():
            dgam_ref[...] = jnp.zeros_like(dgam_ref)
            dwg_ref[...] = jnp.zeros_like(dwg_ref)
            dwu_ref[...] = jnp.zeros_like(dwu_ref)
            dwd_ref[...] = jnp.zeros_like(dwd_ref)

        xnb = xn_ref[...]
        do = do_ref[...]
        dob = (0.5 * do).astype(BF16)
        g = gg_ref[...].astype(F32)
        u = uu_ref[...].astype(F32)
        s = _sigmoid(g)
        sil = g * s
        dhid = _dot_nt(dob, wd_ref[...])
        dwd_ref[...] += _dot_tn((sil * u).astype(BF16), dob)
        du = (dhid * sil).astype(BF16)
        dg = (dhid * u * (s * (1.0 + g * (1.0 - s)))).astype(BF16)
        dwg_ref[...] += _dot_tn(dg, xnb)
        dwu_ref[...] += _dot_tn(du, xnb)
        dxn = _dot(dg, wg_ref[...]) + _dot(du, wu_ref[...])
        for i in range(nprev):
            dxn = dxn + slab_ref[i].astype(F32)
        xhat, rstd, _ = _rms(h_ref[...], g_ref[...])
        dhn, dgam = _rms_bwd(xhat, rstd, g_ref[...], dxn)
        dh_ref[...] = do + dhn
        dgam_ref[...] += dgam

    tile = pl.BlockSpec((tm, D_MODEL), lambda t: (t, 0))
    hidden = pl.BlockSpec((None, tm, FF_CHUNK), lambda t: (k, t, 0))
    w_in = pl.BlockSpec((None, None, FF_CHUNK, D_MODEL), lambda t: (layer, k, 0, 0))
    dw_in = pl.BlockSpec((None, FF_CHUNK, D_MODEL), lambda t: (k, 0, 0))
    return pl.pallas_call(
        body, name="ffn_bwd_last",
        grid=(S // tm,),
        in_specs=[tile, tile, hidden, hidden, w_in, w_in,
                  pl.BlockSpec((None, None, FF_CHUNK, D_MODEL), lambda t: (layer, k, 0, 0)),
                  pl.BlockSpec((nprev, tm, D_MODEL), lambda t: (0, t, 0)), tile,
                  gamma.spec()] + _any_specs(3),
        out_specs=[tile, pl.BlockSpec((SUBLANES, D_MODEL), lambda t: (0, 0)), dw_in, dw_in,
                   pl.BlockSpec((None, FF_CHUNK, D_MODEL), lambda t: (k, 0, 0))],
        out_shape=[jax.ShapeDtypeStruct((S, D_MODEL), F32), jax.ShapeDtypeStruct((SUBLANES, D_MODEL), F32),
                   jax.ShapeDtypeStruct(dwg.shape, F32), jax.ShapeDtypeStruct(dwu.shape, F32),
                   jax.ShapeDtypeStruct(dwd.shape, F32)],
        input_output_aliases={10: 2, 11: 3, 12: 4},
        compiler_params=_cparams(("arbitrary",)),
    )(xn, dout, gg, uu, wg, wu, wd, slabs, h, gamma.array, dwg, dwu, dwd)


def _xt_dy(name, x, dy, nchunk, kb, nb, x_by_chunk, y_by_chunk):
    S = x.shape[0]
    tm = _tile(S, 2048)

    def body(x_ref, dy_ref, o_ref):
        @pl.when(pl.program_id(1) == 0)
        def _():
            o_ref[...] = jnp.zeros_like(o_ref)

        o_ref[...] += _dot_tn(x_ref[...].astype(BF16), dy_ref[...].astype(BF16))

    return pl.pallas_call(
        body, name=name,
        grid=(nchunk, S // tm),
        in_specs=[
            pl.BlockSpec((tm, kb), (lambda c, t: (t, c)) if x_by_chunk else (lambda c, t: (t, 0))),
            pl.BlockSpec((tm, nb), (lambda c, t: (t, c)) if y_by_chunk else (lambda c, t: (t, 0))),
        ],
        out_specs=pl.BlockSpec((None, kb, nb), lambda c, t: (c, 0, 0)),
        out_shape=jax.ShapeDtypeStruct((nchunk, kb, nb), F32),
        compiler_params=_cparams(("arbitrary", "arbitrary")),
    )(x, dy)


def _lru_gates(xc, wbd_ref, ba, bx, apar):
    xcb = xc.astype(BF16)
    r_parts, ig_parts = [], []
    for q in range(D_RNN // GATE_CHUNK):
        lo, hi = q * GATE_CHUNK, (q + 1) * GATE_CHUNK
        pre = _dot(xcb[:, lo:hi], wbd_ref[q])
        r_parts.append(_sigmoid(pre[:, :GATE_CHUNK] + ba[:, lo:hi]))
        ig_parts.append(_sigmoid(pre[:, GATE_CHUNK:] + bx[:, lo:hi]))
    r = jnp.concatenate(r_parts, axis=1)
    ig = jnp.concatenate(ig_parts, axis=1)
    sp = LRU_C * _softplus(-apar)
    log_a = -(r * sp)
    a = jnp.exp(log_a)
    mult = jnp.sqrt(_neg_expm1(2.0 * log_a))
    return r, ig, a, mult, sp


def _conv_causal(xb, tail, cw_ref, cb):
    ext = jnp.concatenate([tail, xb], axis=0)
    xc = cb + cw_ref[CONV_WIDTH - 1:CONV_WIDTH, :] * xb
    for j in range(1, CONV_WIDTH):
        xc = xc + cw_ref[CONV_WIDTH - 1 - j:CONV_WIDTH - j, :] * _shift_down(ext, j, SUBLANES)
    return xc, ext


def _lru_fwd(h, gamma, win, layer, convw, convb, wbd, ba, bx, apar, wout):
    S = h.shape[0]
    tm = _tile(S, 256)

    def body(h_ref, g_ref, win_ref, cw_ref, cb_ref, wbd_ref, ba_ref, bx_ref, ap_ref, wout_ref,
             ho_ref, xn_ref, z_ref, hs_ref, gates_ref, tail_s, carry_s):
        @pl.when(pl.program_id(0) == 0)
        def _():
            tail_s[...] = jnp.zeros_like(tail_s)
            carry_s[...] = jnp.zeros_like(carry_s)

        hf = h_ref[...]
        _, _, xn = _rms(hf, g_ref[...])
        xnb = xn.astype(BF16)
        xn_ref[...] = xnb
        for k in range(N_CHIPS):
            z_ref[:, k * RNN_IN_CHUNK:(k + 1) * RNN_IN_CHUNK] = _dot(xnb, win_ref[k])
        gate = z_ref[:, :D_RNN]
        xb = z_ref[:, D_RNN:]
        xc, _ = _conv_causal(xb, tail_s[...], cw_ref, cb_ref[...])
        tail_s[...] = xb[tm - SUBLANES:, :]
        r, ig, a, mult, _ = _lru_gates(xc, wbd_ref, ba_ref[...], bx_ref[...], ap_ref[...])
        for plane, val in enumerate((xc, r, ig, a, mult)):
            gates_ref[plane] = val
        big_a, big_b = _scan_causal(a, mult * (ig * xc))
        hs = big_a * carry_s[SUBLANES - 1:SUBLANES, :] + big_b
        hs_ref[...] = hs
        carry_s[...] = hs[tm - SUBLANES:, :]
        y = hs * _gelu(gate)
        ho_ref[...] = hf + _dot(y.astype(BF16), wout_ref[...])

    return pl.pallas_call(
        body, name="lru_fwd",
        grid=(S // tm,),
        in_specs=[
            pl.BlockSpec((tm, D_MODEL), lambda t: (t, 0)),
            gamma.spec(),
            pl.BlockSpec((None, N_CHIPS, D_MODEL, RNN_IN_CHUNK), lambda t: (layer, 0, 0, 0)),
            convw.spec(),
            convb.spec(),
            pl.BlockSpec((D_RNN // GATE_CHUNK, GATE_CHUNK, 2 * GATE_CHUNK), lambda t: (0, 0, 0)),
            ba.spec(), bx.spec(), apar.spec(),
            pl.BlockSpec((None, D_RNN, D_MODEL), lambda t: (layer, 0, 0)),
        ],
        out_specs=[
            pl.BlockSpec((tm, D_MODEL), lambda t: (t, 0)),
            pl.BlockSpec((tm, D_MODEL), lambda t: (t, 0)),
            pl.BlockSpec((tm, 2 * D_RNN), lambda t: (t, 0)),
            pl.BlockSpec((tm, D_RNN), lambda t: (t, 0)),
            pl.BlockSpec((N_GATE_PLANES, tm, D_RNN), lambda t: (0, t, 0)),
        ],
        out_shape=[
            jax.ShapeDtypeStruct((S, D_MODEL), F32),
            jax.ShapeDtypeStruct((S, D_MODEL), BF16),
            jax.ShapeDtypeStruct((S, 2 * D_RNN), F32),
            jax.ShapeDtypeStruct((S, D_RNN), F32),
            jax.ShapeDtypeStruct((N_GATE_PLANES, S, D_RNN), F32),
        ],
        scratch_shapes=[pltpu.VMEM((SUBLANES, D_RNN), F32), pltpu.VMEM((SUBLANES, D_RNN), F32)],
        compiler_params=_cparams(("arbitrary",)),
    )(h, gamma.array, win, convw.array, convb.array, wbd, ba.array, bx.array, apar.array, wout)


def _lru_bwd_seq(dout, z, hs, gates, convw, wbd, apar, wout, layer, cargo=None):
    S = dout.shape[0]
    tm = _tile(S, 256)
    nt = S // tm
    per8 = tm // SUBLANES
    rev = lambda i: nt - 1 - i
    prev8 = lambda i: jnp.maximum(rev(i) * per8 - 1, 0)
    cargo = cargo or _Cargo()
    n_in, n_out = 10, 6

    def body(*refs):
        ins, c_ins, outs, c_outs, scratch, sems = _cargo_refs(refs, cargo, n_in, n_out, n_scratch=3)
        do_ref, z_ref, hs_ref, gates_ref, ztail_ref, hstail_ref, cw_ref, wbd_ref, ap_ref, wout_ref = ins
        dz_ref, dpre_ref, xc_ref, y_ref, dcw_ref, vec_ref = outs
        a_first_s, g_first_s, dxc_head_s = scratch
        i = pl.program_id(0)
        first_in_time = rev(i) == 0

        @pl.when(i == 0)
        def _():
            cargo.start(c_ins, c_outs, sems)
            a_first_s[...] = jnp.zeros_like(a_first_s)
            g_first_s[...] = jnp.zeros_like(g_first_s)
            dxc_head_s[...] = jnp.zeros_like(dxc_head_s)
            dcw_ref[...] = jnp.zeros_like(dcw_ref)
            vec_ref[...] = jnp.zeros_like(vec_ref)

        gate = z_ref[:, :D_RNN]
        xb = z_ref[:, D_RNN:]
        hist = jnp.where(first_in_time, 0.0, 1.0)
        xext = jnp.concatenate([ztail_ref[:, D_RNN:] * hist, xb], axis=0)
        xc, r, ig, a, mult = (gates_ref[plane] for plane in range(N_GATE_PLANES))
        sp = LRU_C * _softplus(-ap_ref[...])
        hs = hs_ref[...]
        gel, dgel = _gelu_and_grad(gate)
        y = hs * gel
        y_ref[...] = y.astype(BF16)
        xc_ref[...] = xc.astype(BF16)

        dy = _dot_nt(do_ref[...].astype(BF16), wout_ref[...])
        dhs = dy * gel
        dgate = dy * hs * dgel

        coef = _shift_up(jnp.concatenate([a, a_first_s[...]], axis=0), 1, tm)
        big_c, big_d = _scan_anticausal(coef, dhs)
        g = big_d + big_c * g_first_s[0:1, :]
        g_first_s[...] = g[:SUBLANES, :]
        a_first_s[...] = a[:SUBLANES, :]

        hs_prev = _shift_down(jnp.concatenate([hstail_ref[...] * hist, hs], axis=0), 1, SUBLANES)
        da = g * hs_prev
        dmult = g * ig * xc
        dig = g * mult * xc
        dxc = g * mult * ig
        dlog_a = da * a - dmult * (a * a) / mult
        dr = -(dlog_a * sp)
        dpre_a = dr * r * (1.0 - r)
        dpre_x = dig * ig * (1.0 - ig)
        d_apar = dlog_a * r * (LRU_C * _sigmoid(-ap_ref[...]))

        for q in range(D_RNN // GATE_CHUNK):
            lo, hi = q * GATE_CHUNK, (q + 1) * GATE_CHUNK
            dpre_q = jnp.concatenate([dpre_a[:, lo:hi], dpre_x[:, lo:hi]], axis=1).astype(BF16)
            dpre_ref[:, 2 * lo:2 * hi] = dpre_q
            dxc_q = _dot_nt(dpre_q, wbd_ref[q])
            if q == 0:
                dxc_parts = [dxc_q]
            else:
                dxc_parts.append(dxc_q)
        dxc = dxc + jnp.concatenate(dxc_parts, axis=1)

        dext = jnp.concatenate([dxc, dxc_head_s[...]], axis=0)
        dxb = cw_ref[CONV_WIDTH - 1:CONV_WIDTH, :] * dxc
        for j in range(1, CONV_WIDTH):
            dxb = dxb + cw_ref[CONV_WIDTH - 1 - j:CONV_WIDTH - j, :] * _shift_up(dext, j, tm)
        dxc_head_s[...] = dxc[:SUBLANES, :]
        dz_ref[:, :D_RNN] = dgate.astype(BF16)
        dz_ref[:, D_RNN:] = dxb.astype(BF16)

        dcw_ref[CONV_WIDTH - 1] += _rowsum8(dxc * xb)
        for j in range(1, CONV_WIDTH):
            dcw_ref[CONV_WIDTH - 1 - j] += _rowsum8(dxc * _shift_down(xext, j, SUBLANES))
        vec_ref[0] += _rowsum8(dxc)
        vec_ref[1] += _rowsum8(dpre_a)
        vec_ref[2] += _rowsum8(dpre_x)
        vec_ref[3] += _rowsum8(d_apar)

        @pl.when(i == nt - 1)
        def _():
            cargo.finish(c_ins, c_outs, sems)

    outs = pl.pallas_call(
        body, name="lru_bwd_seq",
        grid=(nt,),
        in_specs=[
            pl.BlockSpec((tm, D_MODEL), lambda i: (rev(i), 0)),
            pl.BlockSpec((tm, 2 * D_RNN), lambda i: (rev(i), 0)),
            pl.BlockSpec((tm, D_RNN), lambda i: (rev(i), 0)),
            pl.BlockSpec((N_GATE_PLANES, tm, D_RNN), lambda i: (0, rev(i), 0)),
            pl.BlockSpec((SUBLANES, 2 * D_RNN), lambda i: (prev8(i), 0)),
            pl.BlockSpec((SUBLANES, D_RNN), lambda i: (prev8(i), 0)),
            convw.spec(),
            pl.BlockSpec((D_RNN // GATE_CHUNK, GATE_CHUNK, 2 * GATE_CHUNK), lambda i: (0, 0, 0)),
            apar.spec(),
            pl.BlockSpec((None, D_RNN, D_MODEL), lambda i: (layer, 0, 0)),
        ] + _any_specs(len(cargo.operands)),
        out_specs=[
            pl.BlockSpec((tm, 2 * D_RNN), lambda i: (rev(i), 0)),
            pl.BlockSpec((tm, 2 * D_RNN), lambda i: (rev(i), 0)),
            pl.BlockSpec((tm, D_RNN), lambda i: (rev(i), 0)),
            pl.BlockSpec((tm, D_RNN), lambda i: (rev(i), 0)),
            pl.BlockSpec((CONV_WIDTH, SUBLANES, D_RNN), lambda i: (0, 0, 0)),
            pl.BlockSpec((4, SUBLANES, D_RNN), lambda i: (0, 0, 0)),
        ] + _any_specs(len(cargo.out_shapes)),
        out_shape=[
            jax.ShapeDtypeStruct((S, 2 * D_RNN), BF16),
            jax.ShapeDtypeStruct((S, 2 * D_RNN), BF16),
            jax.ShapeDtypeStruct((S, D_RNN), BF16),
            jax.ShapeDtypeStruct((S, D_RNN), BF16),
            jax.ShapeDtypeStruct((CONV_WIDTH, SUBLANES, D_RNN), F32),
            jax.ShapeDtypeStruct((4, SUBLANES, D_RNN), F32),
        ] + cargo.out_shapes,
        input_output_aliases={n_in + i: n_out + o for i, o in cargo.aliases.items()},
        scratch_shapes=[pltpu.VMEM((SUBLANES, D_RNN), F32)] * 3 + cargo.sem_shapes,
        compiler_params=_cparams(("arbitrary",)),
    )(dout, z, hs, gates, z, hs, convw.array, wbd, apar.array, wout, *cargo.operands)
    return outs[:n_out], list(outs[n_out:])


def _lru_bwd_in(dz, h, gamma, dres, win, layer, cargo=None):
    S = h.shape[0]
    tm = _tile(S, 512)
    nt = S // tm
    cargo = cargo or _Cargo()
    n_in, n_out = 5, 2

    def body(*refs):
        (dz_ref, h_ref, g_ref, dres_ref, win_ref), c_ins, (dh_ref, dgam_ref), c_outs, _, sems = _cargo_refs(
            refs, cargo, n_in, n_out)

        @pl.when(pl.program_id(0) == 0)
        def _():
            cargo.start(c_ins, c_outs, sems)

        dxn = _dot_nt(dz_ref[:, :RNN_IN_CHUNK], win_ref[0])
        for k in range(1, N_CHIPS):
            dxn = dxn + _dot_nt(dz_ref[:, k * RNN_IN_CHUNK:(k + 1) * RNN_IN_CHUNK], win_ref[k])
        xhat, rstd, _ = _rms(h_ref[...], g_ref[...])
        dhn, dgam = _rms_bwd(xhat, rstd, g_ref[...], dxn)
        dh_ref[...] = dres_ref[...] + dhn

        @pl.when(pl.program_id(0) == 0)
        def _():
            dgam_ref[...] = jnp.zeros_like(dgam_ref)

        dgam_ref[...] += dgam

        @pl.when(pl.program_id(0) == nt - 1)
        def _():
            cargo.finish(c_ins, c_outs, sems)

    outs = pl.pallas_call(
        body, name="lru_bwd_in",
        grid=(nt,),
        in_specs=[
            pl.BlockSpec((tm, 2 * D_RNN), lambda t: (t, 0)),
            pl.BlockSpec((tm, D_MODEL), lambda t: (t, 0)),
            gamma.spec(),
            pl.BlockSpec((tm, D_MODEL), lambda t: (t, 0)),
            pl.BlockSpec((None, N_CHIPS, D_MODEL, RNN_IN_CHUNK), lambda t: (layer, 0, 0, 0)),
        ] + _any_specs(len(cargo.operands)),
        out_specs=[
            pl.BlockSpec((tm, D_MODEL), lambda t: (t, 0)),
            pl.BlockSpec((SUBLANES, D_MODEL), lambda t: (0, 0)),
        ] + _any_specs(len(cargo.out_shapes)),
        out_shape=[jax.ShapeDtypeStruct((S, D_MODEL), F32),
                   jax.ShapeDtypeStruct((SUBLANES, D_MODEL), F32)] + cargo.out_shapes,
        input_output_aliases={n_in + i: n_out + o for i, o in cargo.aliases.items()},
        scratch_shapes=list(cargo.sem_shapes),
        compiler_params=_cparams(("arbitrary",)),
    )(dz, h, gamma.array, dres, win, *cargo.operands)
    return outs[:n_out], list(outs[n_out:])


def _pool_inv_count(t_index, tm):
    rows = (lax.broadcasted_iota(jnp.int32, (tm, D_MODEL), 0) + t_index * tm + 1).astype(F32)
    col = lax.broadcasted_iota(jnp.int32, (tm, D_MODEL), 1)
    win = jnp.where(col < POOL_GROUP_DIM, float(POOL_WINDOWS[0]),
                    jnp.where(col < 2 * POOL_GROUP_DIM, float(POOL_WINDOWS[1]),
                              jnp.where(col < 3 * POOL_GROUP_DIM, float(POOL_WINDOWS[2]), float(POOL_WINDOWS[3]))))
    return 1.0 / jnp.minimum(rows, win)


def _window_sums(ext, shift, take):
    gd = POOL_GROUP_DIM
    s2 = ext + shift(ext, 1)
    s4 = s2[:, gd:] + shift(s2[:, gd:], 2)
    s8 = s4[:, gd:] + shift(s4[:, gd:], 4)
    s16 = s8[:, gd:] + shift(s8[:, gd:], 8)
    return jnp.concatenate([take(s2[:, :gd]), take(s4[:, :gd]), take(s8[:, :gd]), take(s16)], axis=1)


def _pool_fwd(h, gamma, pw, pb, pscale, layer):
    S = h.shape[0]
    tm = _tile(S, 512)

    def body(h_ref, g_ref, pw_ref, pb_ref, ps_ref, ho_ref, u_ref, tail_s):
        t = pl.program_id(0)

        @pl.when(t == 0)
        def _():
            tail_s[...] = jnp.zeros_like(tail_s)

        hf = h_ref[...]
        _, _, hn = _rms(hf, g_ref[...])
        ext = jnp.concatenate([tail_s[...], hn], axis=0)
        tail_s[...] = hn[tm - POOL_HALO:, :]
        sums = _window_sums(ext, lambda v, j: pltpu.roll(v, j, 0), lambda v: v[POOL_HALO:])
        ub = (sums * _pool_inv_count(t, tm) - hn).astype(BF16)
        u_ref[...] = ub
        ys = [_dot(ub[:, g * POOL_GROUP_DIM:(g + 1) * POOL_GROUP_DIM], pw_ref[g]) for g in range(len(POOL_WINDOWS))]
        y = jnp.concatenate(ys, axis=1)
        ho_ref[...] = hf + (y + pb_ref[...]) * ps_ref[...]

    return pl.pallas_call(
        body, name="pool_fwd",
        grid=(S // tm,),
        in_specs=[
            pl.BlockSpec((tm, D_MODEL), lambda t: (t, 0)), gamma.spec(),
            pl.BlockSpec((None, len(POOL_WINDOWS), POOL_GROUP_DIM, POOL_GROUP_DIM), lambda t: (layer, 0, 0, 0)),
            pb.spec(), pscale.spec(),
        ],
        out_specs=[pl.BlockSpec((tm, D_MODEL), lambda t: (t, 0)), pl.BlockSpec((tm, D_MODEL), lambda t: (t, 0))],
        out_shape=[jax.ShapeDtypeStruct((S, D_MODEL), F32), jax.ShapeDtypeStruct((S, D_MODEL), BF16)],
        scratch_shapes=[pltpu.VMEM((POOL_HALO, D_MODEL), F32)],
        compiler_params=_cparams(("arbitrary",)),
    )(h, gamma.array, pw, pb.array, pscale.array)


def _pool_bwd(dout, h, u, gamma, pw, pb, pscale, layer):
    S = h.shape[0]
    tm = _tile(S, 512)
    nt = S // tm
    rev = lambda i: nt - 1 - i
    ngroup = len(POOL_WINDOWS)

    def body(do_ref, h_ref, u_ref, g_ref, pw_ref, pb_ref, ps_ref, dh_ref, dpre_ref, vec_ref, head_s):
        i = pl.program_id(0)

        @pl.when(i == 0)
        def _():
            head_s[...] = jnp.zeros_like(head_s)
            vec_ref[...] = jnp.zeros_like(vec_ref)

        do = do_ref[...]
        ub = u_ref[...]
        gsl = lambda v, g: v[:, g * POOL_GROUP_DIM:(g + 1) * POOL_GROUP_DIM]
        y = jnp.concatenate([_dot(gsl(ub, g), pw_ref[g]) for g in range(ngroup)], axis=1)
        dpre = do * ps_ref[...]
        dpb = dpre.astype(BF16)
        dpre_ref[...] = dpb
        du = jnp.concatenate([_dot_nt(gsl(dpb, g), pw_ref[g]) for g in range(ngroup)], axis=1)
        v = du * _pool_inv_count(rev(i), tm)
        ext = jnp.concatenate([v, head_s[...]], axis=0)
        head_s[...] = v[:POOL_HALO, :]
        n = tm + POOL_HALO
        dhn = _window_sums(ext, lambda w, j: pltpu.roll(w, n - j, 0), lambda w: w[:tm]) - du
        xhat, rstd, _ = _rms(h_ref[...], g_ref[...])
        dh_in, dgam = _rms_bwd(xhat, rstd, g_ref[...], dhn)
        dh_ref[...] = do + dh_in
        vec_ref[0] += dgam
        vec_ref[1] += _rowsum8(dpre)
        vec_ref[2] += _rowsum8(do * (y + pb_ref[...]))

    tile = pl.BlockSpec((tm, D_MODEL), lambda i: (rev(i), 0))
    return pl.pallas_call(
        body, name="pool_bwd",
        grid=(nt,),
        in_specs=[tile, tile, tile, gamma.spec(),
                  pl.BlockSpec((None, ngroup, POOL_GROUP_DIM, POOL_GROUP_DIM), lambda i: (layer, 0, 0, 0)),
                  pb.spec(), pscale.spec()],
        out_specs=[tile, tile, pl.BlockSpec((3, SUBLANES, D_MODEL), lambda i: (0, 0, 0))],
        out_shape=[jax.ShapeDtypeStruct((S, D_MODEL), F32), jax.ShapeDtypeStruct((S, D_MODEL), BF16),
                   jax.ShapeDtypeStruct((3, SUBLANES, D_MODEL), F32)],
        scratch_shapes=[pltpu.VMEM((POOL_HALO, D_MODEL), F32)],
        compiler_params=_cparams(("arbitrary",)),
    )(dout, h, u, gamma.array, pw, pb.array, pscale.array)


def _ple_parts(hf, gamma, p_tile, wgate_ref, wproj_ref):
    xhat, rstd, xn = _rms(hf, gamma)
    xnb = xn.astype(BF16)
    gate = _sigmoid(_dot(xnb, wgate_ref[...]))
    pb = p_tile.astype(BF16)
    proj = jnp.concatenate([_dot(pb, wproj_ref[k]) for k in range(N_CHIPS)], axis=1)
    return xhat, rstd, xnb, pb, gate, proj


def _ple_fwd(h, gamma, p_l, wgate, wproj, layer):
    S = h.shape[0]
    tm = _tile(S, 1024)

    def body(h_ref, g_ref, p_ref, wgate_ref, wproj_ref, ho_ref):
        hf = h_ref[...]
        _, _, _, _, gate, proj = _ple_parts(hf, g_ref[...], p_ref[...], wgate_ref, wproj_ref)
        ho_ref[...] = hf + gate * proj

    return pl.pallas_call(
        body, name="ple_fwd",
        grid=(S // tm,),
        in_specs=[
            pl.BlockSpec((tm, D_MODEL), lambda t: (t, 0)),
            gamma.spec(),
            pl.BlockSpec((None, None, tm, PLE_DIM), lambda t: (layer, 0, t, 0)),
            pl.BlockSpec((None, D_MODEL, D_MODEL), lambda t: (layer, 0, 0)),
            pl.BlockSpec((None, N_CHIPS, PLE_DIM, PLE_DIM), lambda t: (layer, 0, 0, 0)),
        ],
        out_specs=pl.BlockSpec((tm, D_MODEL), lambda t: (t, 0)),
        out_shape=jax.ShapeDtypeStruct((S, D_MODEL), F32),
        compiler_params=_cparams(("arbitrary",)),
    )(h, gamma.array, p_l, wgate, wproj)


def _ple_bwd(dout, h, gamma, p_l, wgate, wproj, layer):
    S = h.shape[0]
    tm = _tile(S, 512)

    def body(do_ref, h_ref, g_ref, p_ref, wgate_ref, wproj_ref, dh_ref, dwg_ref, dwp_ref, dgam_ref):
        @pl.when(pl.program_id(0) == 0)
        def _():
            dgam_ref[...] = jnp.zeros_like(dgam_ref)
            dwg_ref[...] = jnp.zeros_like(dwg_ref)
            dwp_ref[...] = jnp.zeros_like(dwp_ref)

        do = do_ref[...]
        xhat, rstd, xnb, pb, gate, proj = _ple_parts(h_ref[...], g_ref[...], p_ref[...], wgate_ref, wproj_ref)
        dproj = (do * gate).astype(BF16)
        dpre = (do * proj * gate * (1.0 - gate)).astype(BF16)
        dwg_ref[...] += _dot_tn(xnb, dpre)
        for k in range(N_CHIPS):
            dwp_ref[k] += _dot_tn(pb, dproj[:, k * PLE_DIM:(k + 1) * PLE_DIM])
        dhn, dgam = _rms_bwd(xhat, rstd, g_ref[...], _dot_nt(dpre, wgate_ref[...]))
        dh_ref[...] = do + dhn
        dgam_ref[...] += dgam

    tile = pl.BlockSpec((tm, D_MODEL), lambda t: (t, 0))
    return pl.pallas_call(
        body, name="ple_bwd",
        grid=(S // tm,),
        in_specs=[
            tile, tile,
            gamma.spec(),
            pl.BlockSpec((None, None, tm, PLE_DIM), lambda t: (layer, 0, t, 0)),
            pl.BlockSpec((None, D_MODEL, D_MODEL), lambda t: (layer, 0, 0)),
            pl.BlockSpec((None, N_CHIPS, PLE_DIM, PLE_DIM), lambda t: (layer, 0, 0, 0)),
        ],
        out_specs=[tile, pl.BlockSpec((D_MODEL, D_MODEL), lambda t: (0, 0)),
                   pl.BlockSpec((N_CHIPS, PLE_DIM, PLE_DIM), lambda t: (0, 0, 0)),
                   pl.BlockSpec((SUBLANES, D_MODEL), lambda t: (0, 0))],
        out_shape=[jax.ShapeDtypeStruct((S, D_MODEL), F32), jax.ShapeDtypeStruct((D_MODEL, D_MODEL), F32),
                   jax.ShapeDtypeStruct((N_CHIPS, PLE_DIM, PLE_DIM), F32),
                   jax.ShapeDtypeStruct((SUBLANES, D_MODEL), F32)],
        compiler_params=_cparams(("arbitrary",)),
    )(dout, h, gamma.array, p_l, wgate, wproj)


def _final(h, gamma, target):
    S = h.shape[0]
    tm = _tile(S, 1024)

    def body(h_ref, g_ref, tgt_ref, dh_ref, dgam_ref, loss_ref):
        xhat, rstd, y = _rms(h_ref[...], g_ref[...])
        err = y - tgt_ref[...]
        dy = err * (1.0 / D_MODEL)
        dhn, dgam = _rms_bwd(xhat, rstd, g_ref[...], dy)
        dh_ref[...] = dhn
        sq = _rowsum8(err * err)
        part = sq[:, :LANES]
        for j in range(1, D_MODEL // LANES):
            part = part + sq[:, j * LANES:(j + 1) * LANES]

        @pl.when(pl.program_id(0) == 0)
        def _():
            dgam_ref[...] = jnp.zeros_like(dgam_ref)
            loss_ref[...] = jnp.zeros_like(loss_ref)

        dgam_ref[...] += dgam
        loss_ref[...] += part * (0.5 / D_MODEL)

    tile = pl.BlockSpec((tm, D_MODEL), lambda t: (t, 0))
    return pl.pallas_call(
        body, name="final_loss",
        grid=(S // tm,),
        in_specs=[tile, gamma.spec(), tile],
        out_specs=[tile, pl.BlockSpec((SUBLANES, D_MODEL), lambda t: (0, 0)),
                   pl.BlockSpec((SUBLANES, LANES), lambda t: (0, 0))],
        out_shape=[jax.ShapeDtypeStruct((S, D_MODEL), F32), jax.ShapeDtypeStruct((SUBLANES, D_MODEL), F32),
                   jax.ShapeDtypeStruct((SUBLANES, LANES), F32)],
        compiler_params=_cparams(("arbitrary",)),
    )(h, gamma.array, target)


def _mesh_pos():
    return lax.axis_index("x"), lax.axis_index("y"), lax.axis_index("c")


def _other_chip(x, y, j):
    fx, fy = CHIP_FLIPS[j]
    return (1 - x if fx else x), (1 - y if fy else y)


def _any_specs(n):
    return [pl.BlockSpec(memory_space=pl.ANY)] * n


class _Cargo:
    def __init__(self):
        self.operands, self.out_shapes, self.aliases, self.sem_shapes, self.names = [], [], {}, [], []
        self.start = lambda ins, outs, sems: None
        self.forward = lambda ins, outs, sems: None
        self.finish = lambda ins, outs, sems: None


def _cargo_refs(refs, cargo, n_in, n_out, n_scratch=0):
    a = n_in
    b = a + len(cargo.operands)
    c = b + n_out
    d = c + len(cargo.out_shapes)
    e = d + n_scratch
    return refs[:a], refs[a:b], refs[b:c], refs[c:d], refs[d:e], refs[e:]


def _remote(src, dst, send, recv, device):
    return pltpu.make_async_remote_copy(src_ref=src, dst_ref=dst, send_sem=send, recv_sem=recv,
                                        device_id=device, device_id_type=MESH_ID)


def _gather_cargo(bufs, pieces):
    cargo = _Cargo()
    if not pieces:
        return cargo
    plist = []
    for name, layer in pieces:
        if name not in cargo.names:
            cargo.names.append(name)
            cargo.operands.append(bufs[name])
        plist.append((cargo.names.index(name), layer, bufs[name].shape[2] // 2))
    nflip = len(CHIP_FLIPS)
    cargo.out_shapes = [jax.ShapeDtypeStruct(b.shape, b.dtype) for b in cargo.operands]
    cargo.aliases = {i: i for i in range(len(cargo.operands))}
    cargo.sem_shapes = [pltpu.SemaphoreType.DMA((len(plist) * nflip,))] * 4

    def copies(outs, sems):
        send1, recv1, send2, recv2 = sems
        x, y, c = _mesh_pos()
        k = 2 * x + y

        def blk(p, chip, cc):
            b, layer, hrows = plist[p]
            return outs[b].at[layer, chip, pl.ds(cc * hrows, hrows), :]

        def chip_of(j):
            px, py = _other_chip(x, y, j)
            return 2 * px + py

        def ici(p, j):
            px, py = _other_chip(x, y, j)
            return _remote(blk(p, k, c), blk(p, k, c), send1.at[p * nflip + j], recv1.at[p * nflip + j], (px, py, c))

        def landed(p, j):
            px, py = _other_chip(x, y, j)
            return _remote(blk(p, k, c), blk(p, chip_of(j), c), send1.at[p * nflip + j], recv1.at[p * nflip + j],
                           (px, py, c))

        def d2d(p, j, cc):
            return _remote(blk(p, chip_of(j), cc), blk(p, chip_of(j), cc), send2.at[p * nflip + j],
                           recv2.at[p * nflip + j], (x, y, 1 - c))

        return c, ici, landed, d2d

    def start(ins, outs, sems):
        _, ici, _, _ = copies(outs, sems)
        for p in range(len(plist)):
            for j in range(nflip):
                ici(p, j).start()

    def forward(ins, outs, sems):
        c, _, landed, d2d = copies(outs, sems)
        for j in range(nflip):
            for p in range(len(plist)):
                landed(p, j).wait_recv()
                d2d(p, j, c).start()

    def finish(ins, outs, sems):
        c, ici, _, d2d = copies(outs, sems)
        for p in range(len(plist)):
            for j in range(nflip):
                ici(p, j).wait_send()
                d2d(p, j, c).wait_send()
                d2d(p, j, 1 - c).wait_recv()

    cargo.start, cargo.forward, cargo.finish = start, forward, finish
    return cargo


def _reduce_cargo(grads, presums):
    cargo = _Cargo()
    na, nb = len(grads), len(presums)
    nflip = len(CHIP_FLIPS)
    cargo.operands = list(grads) + list(presums)
    cargo.out_shapes = ([jax.ShapeDtypeStruct((g.shape[0], g.shape[1] // 2, g.shape[2]), g.dtype) for g in grads]
                        + [jax.ShapeDtypeStruct((nflip,) + ps.shape[1:], ps.dtype) for ps in presums])
    cargo.sem_shapes = ([pltpu.SemaphoreType.DMA((na,))] * 2 if na else []) + (
        [pltpu.SemaphoreType.DMA((nb * nflip,))] * 2 if nb else [])

    def copies(ins, outs, sems):
        x, y, c = _mesh_pos()
        out = []
        if na:
            send, recv = sems[0], sems[1]
            for a in range(na):
                hrows = grads[a].shape[1] // 2
                out.append(_remote(ins[a].at[:, pl.ds((1 - c) * hrows, hrows), :], outs[a], send.at[a], recv.at[a],
                                   (x, y, 1 - c)))
        if nb:
            send, recv = sems[-2], sems[-1]
            for b in range(nb):
                for j in range(nflip):
                    px, py = _other_chip(x, y, j)
                    out.append(_remote(ins[na + b].at[2 * px + py], outs[na + b].at[j], send.at[b * nflip + j],
                                       recv.at[b * nflip + j], (px, py, c)))
        return out

    def start(ins, outs, sems):
        for cp in copies(ins, outs, sems):
            cp.start()

    def finish(ins, outs, sems):
        for cp in copies(ins, outs, sems):
            cp.wait()

    cargo.start, cargo.finish = start, finish
    return cargo


def _run_cargo(name, cargo):
    nin, nout = len(cargo.operands), len(cargo.out_shapes)

    def body(*refs):
        ins, outs, sems = refs[:nin], refs[nin:nin + nout], refs[nin + nout:]
        cargo.start(ins, outs, sems)
        cargo.forward(ins, outs, sems)
        cargo.finish(ins, outs, sems)

    return list(pl.pallas_call(
        body, name=name,
        in_specs=_any_specs(nin), out_specs=_any_specs(nout), out_shape=cargo.out_shapes,
        input_output_aliases=dict(cargo.aliases), scratch_shapes=list(cargo.sem_shapes),
    )(*cargo.operands))


def _join_siblings(bufs):
    nb = len(bufs)
    items = [(b, layer) for b, buf in enumerate(bufs) for layer in range(buf.shape[0])]

    def body(*refs):
        outs = refs[nb:2 * nb]
        send, recv = refs[2 * nb:]
        x, y, c = _mesh_pos()

        def half(i, cc):
            b, layer = items[i]
            hrows = bufs[b].shape[1] // 2
            blk = outs[b].at[layer, pl.ds(cc * hrows, hrows), :]
            return _remote(blk, blk, send.at[i], recv.at[i], (x, y, 1 - c))

        for i in range(len(items)):
            half(i, c).start()
        for i in range(len(items)):
            half(i, c).wait_send()
            half(i, 1 - c).wait_recv()

    return list(pl.pallas_call(
        body, name="grad_sibling_join",
        in_specs=_any_specs(nb), out_specs=_any_specs(nb),
        out_shape=[jax.ShapeDtypeStruct(b.shape, b.dtype) for b in bufs],
        input_output_aliases={i: i for i in range(nb)},
        scratch_shapes=[pltpu.SemaphoreType.DMA((len(items),))] * 2,
    )(*bufs))


def _cast_place(w3, pos, dtype):
    L, rows, cols = w3.shape

    def body(pos_ref, w_ref, o_ref):
        o_ref[...] = w_ref[...].astype(dtype)

    return pl.pallas_call(
        body, name="cast_place",
        grid_spec=pltpu.PrefetchScalarGridSpec(
            num_scalar_prefetch=1, grid=(L,),
            in_specs=[pl.BlockSpec((None, rows, cols), lambda l, pos: (l, 0, 0))],
            out_specs=pl.BlockSpec((None, None, rows, cols), lambda l, pos: (l, pos[0], 0, 0))),
        out_shape=jax.ShapeDtypeStruct((L, N_CHIPS, rows, cols), dtype),
        compiler_params=_cparams(("arbitrary",)),
    )(pos, w3)


def _allreduce_small(buf):
    R = buf.shape[0]
    half = R // 2
    assert half % SUBLANES == 0, R

    def body(in_ref, out_ref, land, send, recv):
        x, y, c = _mesh_pos()
        out_ref[...] = in_ref[...]
        cp = _remote(out_ref, land.at[0], send.at[0], recv.at[0], (x, y, 1 - c))
        cp.start()
        cp.wait()
        out_ref[...] = out_ref[...] + land[0]
        along_y, along_x = (x, 1 - y, c), (1 - x, y, c)
        lo, hi = pl.ds(0, half), pl.ds(half, half)
        for stage, (peer_lo, peer_hi) in enumerate(((along_y, along_x), (along_x, along_y))):
            slot = 1 + stage
            cps = [_remote(out_ref.at[lo], land.at[slot, lo], send.at[1 + 2 * stage], recv.at[1 + 2 * stage], peer_lo),
                   _remote(out_ref.at[hi], land.at[slot, hi], send.at[2 + 2 * stage], recv.at[2 + 2 * stage], peer_hi)]
            for cp in cps:
                cp.start()
            for cp in cps:
                cp.wait()
            out_ref[...] = out_ref[...] + land[slot]

    return pl.pallas_call(
        body, name="allreduce_small",
        in_specs=[pl.BlockSpec(memory_space=pltpu.VMEM)],
        out_specs=pl.BlockSpec(memory_space=pltpu.VMEM),
        out_shape=jax.ShapeDtypeStruct((R, LANES), F32),
        scratch_shapes=[pltpu.VMEM((3, R, LANES), F32), pltpu.SemaphoreType.DMA((5,)), pltpu.SemaphoreType.DMA((5,))],
        compiler_params=pltpu.CompilerParams(vmem_limit_bytes=VMEM_LIMIT_MB * 2 ** 20),
    )(buf)


def _presum_with_sibling(grad, landed, pos):
    nchunk, rows, cols = grad.shape
    hrows = rows // 2

    def body(pos_ref, g_ref, l_ref, all_ref, own_ref):
        s = g_ref[...] + l_ref[...]
        all_ref[...] = s.astype(BF16)

        @pl.when(pl.program_id(0) == pos_ref[0])
        def _():
            own_ref[...] = s

    return pl.pallas_call(
        body, name="grad_presum",
        grid_spec=pltpu.PrefetchScalarGridSpec(
            num_scalar_prefetch=1, grid=(nchunk,),
            in_specs=[pl.BlockSpec((None, hrows, cols), lambda k, pos: (k, pos[1], 0)),
                      pl.BlockSpec((None, hrows, cols), lambda k, pos: (k, 0, 0))],
            out_specs=[pl.BlockSpec((None, hrows, cols), lambda k, pos: (k, 0, 0)),
                       pl.BlockSpec((hrows, cols), lambda k, pos: (0, 0))]),
        out_shape=[jax.ShapeDtypeStruct((nchunk, hrows, cols), BF16), jax.ShapeDtypeStruct((hrows, cols), F32)],
        compiler_params=_cparams(("arbitrary",)),
    )(pos, grad, landed)


def _sum_chips(own, landed, stacked, layer, shape3, pos):
    hrows, cols = own.shape

    def body(pos_ref, o_ref, l_ref, *rest):
        s = o_ref[...]
        for j in range(len(CHIP_FLIPS)):
            s = s + l_ref[j].astype(F32)
        rest[-1][...] = s

    in_specs = [pl.BlockSpec((hrows, cols), lambda i, pos: (0, 0)),
                pl.BlockSpec((len(CHIP_FLIPS), hrows, cols), lambda i, pos: (0, 0, 0))]
    args = [pos, own, landed]
    aliases = {}
    if stacked is not None:
        in_specs.append(pl.BlockSpec(memory_space=pl.ANY))
        args.append(stacked)
        aliases = {3: 0}
    return pl.pallas_call(
        body, name="grad_sum_chips",
        grid_spec=pltpu.PrefetchScalarGridSpec(
            num_scalar_prefetch=1, grid=(1,), in_specs=in_specs,
            out_specs=pl.BlockSpec((None, hrows, cols), lambda i, pos: (layer, pos[1], 0))),
        out_shape=jax.ShapeDtypeStruct(shape3, F32),
        input_output_aliases=aliases,
        compiler_params=_cparams(("arbitrary",)),
    )(*args)


def _adamw(w, g, m, v):
    R, C = w.shape
    rb = R
    for cand in (512, 352, 320, 256, 128, 64, 32, 16, 8):
        if R % cand == 0:
            rb = cand
            break
    c1 = 1.0 - ADAM_B1 ** ADAM_STEP
    c2 = 1.0 - ADAM_B2 ** ADAM_STEP

    def body(w_ref, g_ref, m_ref, v_ref, go_ref, d_ref, mo_ref, vo_ref):
        gv = g_ref[...]
        go_ref[...] = gv
        m2 = ADAM_B1 * m_ref[...] + (1.0 - ADAM_B1) * gv
        v2 = ADAM_B2 * v_ref[...] + (1.0 - ADAM_B2) * (gv * gv)
        mo_ref[...] = m2
        vo_ref[...] = v2
        d_ref[...] = -ADAM_LR * ((m2 / c1) / (jnp.sqrt(v2 / c2) + ADAM_EPS) + ADAM_WD * w_ref[...])

    spec = pl.BlockSpec((rb, C), lambda i: (i, 0))
    return pl.pallas_call(
        body, name="adamw",
        grid=(R // rb,),
        in_specs=[spec] * 4, out_specs=[spec] * 4,
        out_shape=[jax.ShapeDtypeStruct((R, C), F32)] * 4,
        compiler_params=_cparams(("arbitrary",)),
    )(w, g, m, v)


def _pack(parts, align=SUBLANES * LANES):
    flat = jnp.concatenate([p.reshape(-1).astype(F32) for p in parts])
    pad = (-flat.shape[0]) % align
    return jnp.pad(flat, (0, pad)).reshape(-1, LANES)


def _unpack(buf, shapes):
    flat = buf.reshape(-1)
    out, off = [], 0
    for shp in shapes:
        size = 1
        for d in shp:
            size *= d
        out.append(flat[off:off + size].reshape(shp))
        off += size
    return out


def _block_diag_gates(w_a, w_x):
    nq = D_RNN // GATE_CHUNK
    hpc = LRU_HEADS // nq
    eye = jnp.eye(hpc, dtype=F32)

    def bd(w):
        wq = w.reshape(nq, hpc, LRU_HEAD_DIM, LRU_HEAD_DIM)
        return (wq[:, :, :, None, :] * eye[None, :, None, :, None]).reshape(nq, GATE_CHUNK, GATE_CHUNK)

    return jnp.concatenate([bd(w_a), bd(w_x)], axis=2).astype(BF16)


def _block_diag_extract(dwbd):
    nq = D_RNN // GATE_CHUNK
    hpc = LRU_HEADS // nq
    eye = jnp.eye(hpc, dtype=F32)

    def ex(d):
        d5 = d.reshape(nq, hpc, LRU_HEAD_DIM, hpc, LRU_HEAD_DIM)
        return jnp.sum(d5 * eye[None, :, None, :, None], axis=3).reshape(LRU_HEADS, LRU_HEAD_DIM, LRU_HEAD_DIM)

    return ex(dwbd[:, :, :GATE_CHUNK]), ex(dwbd[:, :, GATE_CHUNK:])


BIG = ("ffn1_w_gate", "ffn1_w_up", "ffn1_w_down", "lru_w_in", "lru_w_out", "pool_w",
       "ffn2_w_gate", "ffn2_w_up", "ffn2_w_down", "ple_w_gate", "ple_w_proj")
TINY_SHARDED = ("lru_conv_w", "pool_b", "pool_scale")
REPLICATED = ("ffn1_norm", "mix_norm", "lru_conv_b", "lru_w_a", "lru_b_a", "lru_w_x", "lru_b_x", "lru_a_param",
              "ffn2_norm", "ple_norm", "final_norm")
WEIGHT_ORDER = ("ffn1_norm", "ffn1_w_gate", "ffn1_w_up", "ffn1_w_down", "mix_norm", "lru_w_in", "lru_conv_w",
                "lru_conv_b", "lru_w_a", "lru_b_a", "lru_w_x", "lru_b_x", "lru_a_param", "lru_w_out", "pool_w",
                "pool_b", "pool_scale", "ffn2_norm", "ffn2_w_gate", "ffn2_w_up", "ffn2_w_down", "ple_norm",
                "ple_w_gate", "ple_w_proj", "final_norm")


TRANSPOSED = ("ffn1_w_gate", "ffn1_w_up", "ffn2_w_gate", "ffn2_w_up")


def _stored(name, a):
    return jnp.swapaxes(a, 1, 2) if name in TRANSPOSED else a


def _as3(a):
    return a.reshape(a.shape[0], -1, a.shape[-1])


def kernel(x, p, ffn1_norm, ffn1_w_gate, ffn1_w_up, ffn1_w_down, mix_norm, lru_w_in, lru_conv_w, lru_conv_b, lru_w_a, lru_b_a, lru_w_x, lru_b_x, lru_a_param, lru_w_out, pool_w, pool_b, pool_scale, ffn2_norm, ffn2_w_gate, ffn2_w_up, ffn2_w_down, ple_norm, ple_w_gate, ple_w_proj, final_norm, loss_target, m_ffn1_norm, m_ffn1_w_gate, m_ffn1_w_up, m_ffn1_w_down, m_mix_norm, m_lru_w_in, m_lru_conv_w, m_lru_conv_b, m_lru_w_a, m_lru_b_a, m_lru_w_x, m_lru_b_x, m_lru_a_param, m_lru_w_out, m_pool_w, m_pool_b, m_pool_scale, m_ffn2_norm, m_ffn2_w_gate, m_ffn2_w_up, m_ffn2_w_down, m_ple_norm, m_ple_w_gate, m_ple_w_proj, m_final_norm, v_ffn1_norm, v_ffn1_w_gate, v_ffn1_w_up, v_ffn1_w_down, v_mix_norm, v_lru_w_in, v_lru_conv_w, v_lru_conv_b, v_lru_w_a, v_lru_b_a, v_lru_w_x, v_lru_b_x, v_lru_a_param, v_lru_w_out, v_pool_w, v_pool_b, v_pool_scale, v_ffn2_norm, v_ffn2_w_gate, v_ffn2_w_up, v_ffn2_w_down, v_ple_norm, v_ple_w_gate, v_ple_w_proj, v_final_norm):
    W = dict(ffn1_norm=ffn1_norm, ffn1_w_gate=ffn1_w_gate, ffn1_w_up=ffn1_w_up, ffn1_w_down=ffn1_w_down,
             mix_norm=mix_norm, lru_w_in=lru_w_in, lru_conv_w=lru_conv_w, lru_conv_b=lru_conv_b, lru_w_a=lru_w_a,
             lru_b_a=lru_b_a, lru_w_x=lru_w_x, lru_b_x=lru_b_x, lru_a_param=lru_a_param, lru_w_out=lru_w_out,
             pool_w=pool_w, pool_b=pool_b, pool_scale=pool_scale, ffn2_norm=ffn2_norm, ffn2_w_gate=ffn2_w_gate,
             ffn2_w_up=ffn2_w_up, ffn2_w_down=ffn2_w_down, ple_norm=ple_norm, ple_w_gate=ple_w_gate,
             ple_w_proj=ple_w_proj, final_norm=final_norm)
    M = dict(ffn1_norm=m_ffn1_norm, ffn1_w_gate=m_ffn1_w_gate, ffn1_w_up=m_ffn1_w_up, ffn1_w_down=m_ffn1_w_down,
             mix_norm=m_mix_norm, lru_w_in=m_lru_w_in, lru_conv_w=m_lru_conv_w, lru_conv_b=m_lru_conv_b,
             lru_w_a=m_lru_w_a, lru_b_a=m_lru_b_a, lru_w_x=m_lru_w_x, lru_b_x=m_lru_b_x, lru_a_param=m_lru_a_param,
             lru_w_out=m_lru_w_out, pool_w=m_pool_w, pool_b=m_pool_b, pool_scale=m_pool_scale, ffn2_norm=m_ffn2_norm,
             ffn2_w_gate=m_ffn2_w_gate, ffn2_w_up=m_ffn2_w_up, ffn2_w_down=m_ffn2_w_down, ple_norm=m_ple_norm,
             ple_w_gate=m_ple_w_gate, ple_w_proj=m_ple_w_proj, final_norm=m_final_norm)
    V = dict(ffn1_norm=v_ffn1_norm, ffn1_w_gate=v_ffn1_w_gate, ffn1_w_up=v_ffn1_w_up, ffn1_w_down=v_ffn1_w_down,
             mix_norm=v_mix_norm, lru_w_in=v_lru_w_in, lru_conv_w=v_lru_conv_w, lru_conv_b=v_lru_conv_b,
             lru_w_a=v_lru_w_a, lru_b_a=v_lru_b_a, lru_w_x=v_lru_w_x, lru_b_x=v_lru_b_x, lru_a_param=v_lru_a_param,
             lru_w_out=v_lru_w_out, pool_w=v_pool_w, pool_b=v_pool_b, pool_scale=v_pool_scale, ffn2_norm=v_ffn2_norm,
             ffn2_w_gate=v_ffn2_w_gate, ffn2_w_up=v_ffn2_w_up, ffn2_w_down=v_ffn2_w_down, ple_norm=v_ple_norm,
             ple_w_gate=v_ple_w_gate, ple_w_proj=v_ple_w_proj, final_norm=v_final_norm)

    S = x.shape[1]
    my_x, my_y, my_c = _mesh_pos()
    my_chip = 2 * my_x + my_y
    pos = jnp.stack([my_chip, my_c]).astype(jnp.int32)
    n_lru, n_pool = lru_w_in.shape[0], pool_w.shape[0]

    tiny_shapes = [W[n].shape for n in TINY_SHARDED]
    tiny_local = _pack([W[n] for n in TINY_SHARDED], align=2 * 16 * LANES)[None]
    Ws, Ms, Vs = ({n: _stored(n, d[n]) for n in BIG} for d in (W, M, V))
    bufs = {n: _cast_place(_as3(Ws[n]), pos, BF16) for n in BIG}
    bufs["tiny"] = _cast_place(tiny_local, pos, F32)

    def gather_now(name, pieces):
        cargo = _gather_cargo(bufs, pieces)
        bufs.update(zip(cargo.names, _run_cargo(name, cargo)))

    def ffn_pieces(which, layer):
        return [("%s_w_gate" % which, layer), ("%s_w_up" % which, layer), ("%s_w_down" % which, layer)]

    def mixer_pieces(layer):
        if layer % 2 == 0:
            return [("lru_w_in", layer // 2), ("lru_w_out", layer // 2)]
        return [("pool_w", layer // 2)]

    gather_now("gather_first", [("tiny", 0)] + ffn_pieces("ffn1", 0))
    tiny_by_chip = [_unpack(bufs["tiny"][0, k], tiny_shapes) for k in range(N_CHIPS)]
    conv_w_full = jnp.concatenate([tiny_by_chip[k][0] for k in range(N_CHIPS)], axis=-1)
    pool_b_full = jnp.concatenate([tiny_by_chip[k][1] for k in range(N_CHIPS)], axis=-1)
    pool_s_full = jnp.concatenate([tiny_by_chip[k][2] for k in range(N_CHIPS)], axis=-1)
    ngroup = len(POOL_WINDOWS)

    def pool_weights():
        pw5 = bufs["pool_w"].reshape(n_pool, N_CHIPS, ngroup, POOL_GROUP_DIM // N_CHIPS, POOL_GROUP_DIM)
        return pw5.transpose(0, 2, 1, 3, 4).reshape(n_pool, ngroup, POOL_GROUP_DIM, POOL_GROUP_DIM)

    lru_out = lambda: bufs["lru_w_out"].reshape(n_lru, D_RNN, D_MODEL)
    ple_gate = lambda: bufs["ple_w_gate"].reshape(DEPTH, D_MODEL, D_MODEL)
    wbd = [_block_diag_gates(lru_w_a[j], lru_w_x[j]) for j in range(n_lru)]
    ffn1_norm, mix_norm, ffn2_norm, ple_norm = (_rows3(a) for a in (ffn1_norm, mix_norm, ffn2_norm, ple_norm))
    lru_conv_b, lru_b_a, lru_b_x, lru_a_param = (_rows3(a) for a in (lru_conv_b, lru_b_a, lru_b_x, lru_a_param))
    pool_b_full, pool_s_full = _rows3(pool_b_full), _rows3(pool_s_full)

    def ffn_forward(which, h, gamma, layer, pieces):
        cargo = _gather_cargo(bufs, pieces)
        outs, updated = _ffn_fwd(h, gamma, bufs[which + "_w_gate"], bufs[which + "_w_up"], bufs[which + "_w_down"],
                                 layer, cargo)
        bufs.update(zip(cargo.names, updated))
        return outs

    h = x.reshape(S, D_MODEL)
    saved = []
    for i in range(DEPTH):
        j = i // 2
        sv = {"h0": h}
        first_mixer = mixer_pieces(0) if i == 0 else []
        h, sv["xn1"], sv["g1"], sv["u1"] = ffn_forward(
            "ffn1", h, _Rows(ffn1_norm, i), i,
            first_mixer + ffn_pieces("ffn2", i) + [("ple_w_gate", i), ("ple_w_proj", i)])
        sv["h1"] = h
        if i % 2 == 0:
            h, sv["xn_mix"], sv["z"], sv["hs"], sv["gates"] = _lru_fwd(
                h, _Rows(mix_norm, i), bufs["lru_w_in"], j, _Rows(conv_w_full, j), _Rows(lru_conv_b, j), wbd[j],
                _Rows(lru_b_a, j), _Rows(lru_b_x, j), _Rows(lru_a_param, j), lru_out())
        else:
            h, sv["u"] = _pool_fwd(h, _Rows(mix_norm, i), pool_weights(), _Rows(pool_b_full, j), _Rows(pool_s_full, j), j)
        sv["h2"] = h
        nxt = ffn_pieces("ffn1", i + 1) + mixer_pieces(i + 1) if i + 1 < DEPTH else []
        h, sv["xn2"], sv["g2"], sv["u2"] = ffn_forward("ffn2", h, _Rows(ffn2_norm, i), i, nxt)
        sv["h3"] = h
        h = _ple_fwd(h, _Rows(ple_norm, i), p, ple_gate(), bufs["ple_w_proj"], i)
        saved.append(sv)

    dh, dgam_final, loss_part = _final(h, _Rows(final_norm.reshape(1, 1, -1), 0), loss_target.reshape(S, D_MODEL))
    win, wout, wpg, wpp, pw = bufs["lru_w_in"], lru_out(), ple_gate(), bufs["ple_w_proj"], pool_weights()

    norm_grads = {n: [None] * DEPTH for n in ("ffn1_norm", "mix_norm", "ffn2_norm", "ple_norm")}
    lru_vec = [None] * n_lru
    pool_vec = [None] * n_pool
    sum8 = lambda a: jnp.sum(a, axis=-2)

    to_siblings, to_chips = [], []
    stacked = {n: None for n in BIG}

    def take_cargo(with_chips=True):
        a_items, b_items = list(to_siblings), list(to_chips) if with_chips else []
        del to_siblings[:], to_chips[:len(b_items)]
        return _reduce_cargo([it[2] for it in a_items], [it[2] for it in b_items]), a_items, b_items

    def absorb(a_items, b_items, outs):
        for (n, layer, g), landed in zip(a_items, outs[:len(a_items)]):
            all_chunks, own = _presum_with_sibling(g, landed, pos)
            to_chips.append((n, layer, all_chunks, own))
        for (n, layer, _, own), from_chips in zip(b_items, outs[len(a_items):]):
            stacked[n] = _sum_chips(own, from_chips, stacked[n], layer, _as3(Ws[n]).shape, pos)

    def ffn_backward(which, xn, dout, gg, uu, layer, h_in, gamma):
        cargo, a_items, b_items = take_cargo()
        weights = (bufs[which + "_w_gate"], bufs[which + "_w_up"], bufs[which + "_w_down"])
        (dwg, dwu, dwd, slabs), c_outs = _ffn_bwd(xn, dout, gg, uu, *weights, layer, cargo)
        absorb(a_items, b_items, c_outs)
        dh_in, dgam, dwg, dwu, dwd = _ffn_bwd_last(xn, dout, gg, uu, *weights, layer, slabs, h_in, gamma,
                                                   dwg, dwu, dwd)
        to_siblings.extend([(which + "_w_gate", layer, dwg), (which + "_w_up", layer, dwu),
                            (which + "_w_down", layer, dwd)])
        return dh_in, sum8(dgam)

    for i in reversed(range(DEPTH)):
        j = i // 2
        sv = saved[i]
        dh, dw_pg, dw_pp, dgam = _ple_bwd(dh, sv["h3"], _Rows(ple_norm, i), p, wpg, wpp, i)
        norm_grads["ple_norm"][i] = sum8(dgam)
        to_siblings.append(("ple_w_gate", i, dw_pg.reshape(N_CHIPS, D_MODEL // N_CHIPS, D_MODEL)))
        to_siblings.append(("ple_w_proj", i, dw_pp))

        dh, norm_grads["ffn2_norm"][i] = ffn_backward("ffn2", sv["xn2"], dh, sv["g2"], sv["u2"], i, sv["h2"],
                                                      _Rows(ffn2_norm, i))

        if i % 2 == 0:
            cargo, a_items, b_items = take_cargo()
            (dz, dpre, xc_b, y_b, dcw, vec), c_outs = _lru_bwd_seq(
                dh, sv["z"], sv["hs"], sv["gates"], _Rows(conv_w_full, j), wbd[j], _Rows(lru_a_param, j), wout, j, cargo)
            absorb(a_items, b_items, c_outs)
            to_siblings.append(("lru_w_out", j, _xt_dy("lru_dw_out", y_b, dh, 1, D_RNN, D_MODEL, False, False)
                                .reshape(N_CHIPS, D_RNN // N_CHIPS, D_MODEL)))
            to_siblings.append(("lru_w_in", j, _xt_dy("lru_dw_in", sv["xn_mix"], dz, N_CHIPS, D_MODEL, RNN_IN_CHUNK,
                                                      False, True)))
            dwbd = _xt_dy("lru_dw_gates", xc_b, dpre, D_RNN // GATE_CHUNK, GATE_CHUNK, 2 * GATE_CHUNK, True, True)
            dw_a, dw_x = _block_diag_extract(dwbd)
            vsum = sum8(vec)
            lru_vec[j] = (sum8(dcw), vsum[0], vsum[1], vsum[2], vsum[3], dw_a, dw_x)
            cargo, a_items, b_items = take_cargo(with_chips=False)
            (dh, dgam), c_outs = _lru_bwd_in(dz, sv["h1"], _Rows(mix_norm, i), dh, win, j, cargo)
            absorb(a_items, b_items, c_outs)
            norm_grads["mix_norm"][i] = sum8(dgam)
        else:
            dh_new, dpre_b, vec = _pool_bwd(dh, sv["h1"], sv["u"], _Rows(mix_norm, i), pw, _Rows(pool_b_full, j),
                                            _Rows(pool_s_full, j), j)
            dpw = _xt_dy("pool_dw", sv["u"], dpre_b, ngroup, POOL_GROUP_DIM, POOL_GROUP_DIM, True, True)
            dpw = dpw.reshape(ngroup, N_CHIPS, POOL_GROUP_DIM // N_CHIPS, POOL_GROUP_DIM).transpose(1, 0, 2, 3)
            to_siblings.append(("pool_w", j, dpw.reshape(N_CHIPS, POOL_GROUP_DIM, POOL_GROUP_DIM)))
            vsum = sum8(vec)
            norm_grads["mix_norm"][i] = vsum[0]
            pool_vec[j] = (vsum[1], vsum[2])
            dh = dh_new

        dh, norm_grads["ffn1_norm"][i] = ffn_backward("ffn1", sv["xn1"], dh, sv["g1"], sv["u1"], i, sv["h0"],
                                                      _Rows(ffn1_norm, i))

    grad_x = dh.reshape(1, S, D_MODEL)

    small_parts = [
        jnp.stack(norm_grads["ffn1_norm"]), jnp.stack(norm_grads["mix_norm"]),
        jnp.stack(norm_grads["ffn2_norm"]), jnp.stack(norm_grads["ple_norm"]), sum8(dgam_final),
        jnp.stack([lv[0] for lv in lru_vec]), jnp.stack([lv[1] for lv in lru_vec]),
        jnp.stack([lv[2] for lv in lru_vec]), jnp.stack([lv[3] for lv in lru_vec]),
        jnp.stack([lv[4] for lv in lru_vec]), jnp.stack([lv[5] for lv in lru_vec]),
        jnp.stack([lv[6] for lv in lru_vec]),
        jnp.stack([pv[0] for pv in pool_vec]), jnp.stack([pv[1] for pv in pool_vec]),
        jnp.sum(loss_part).reshape(1),
    ]
    small_names = ("ffn1_norm", "mix_norm", "ffn2_norm", "ple_norm", "final_norm", "lru_conv_w", "lru_conv_b",
                   "lru_b_a", "lru_b_x", "lru_a_param", "lru_w_a", "lru_w_x", "pool_b", "pool_scale", "loss")
    reduced = _unpack(_allreduce_small(_pack(small_parts, align=2 * SUBLANES * LANES)),
                      [sp.shape for sp in small_parts])
    small_grad = dict(zip(small_names, reduced))
    loss = small_grad.pop("loss").reshape(())
    for n in TINY_SHARDED:
        width = W[n].shape[-1]
        small_grad[n] = lax.dynamic_slice_in_dim(small_grad[n], my_chip * width, width, axis=-1)

    tail = 0
    while to_siblings or to_chips:
        cargo, a_items, b_items = take_cargo()
        absorb(a_items, b_items, _run_cargo("grad_exchange_tail%d" % tail, cargo))
        tail += 1
    big_final = dict(zip(BIG, _join_siblings([stacked[n] for n in BIG])))

    grads, deltas, new_m, new_v = {}, {}, {}, {}
    for n in BIG:
        shp = Ws[n].shape
        to2 = lambda a: a.reshape(-1, shp[-1])
        g2, d, m2, v2 = _adamw(to2(Ws[n]), to2(big_final[n]), to2(Ms[n]), to2(Vs[n]))
        grads[n], deltas[n], new_m[n], new_v[n] = (_stored(n, a.reshape(shp)) for a in (g2, d, m2, v2))
    small_order = TINY_SHARDED + REPLICATED
    small_shapes = [W[n].shape for n in small_order]
    pack_rows = functools.partial(_pack, align=512 * LANES)
    _, sd, sm, sv_ = _adamw(pack_rows([W[n] for n in small_order]), pack_rows([small_grad[n] for n in small_order]),
                            pack_rows([M[n] for n in small_order]), pack_rows([V[n] for n in small_order]))
    for n, d, m2, v2 in zip(small_order, _unpack(sd, small_shapes), _unpack(sm, small_shapes),
                            _unpack(sv_, small_shapes)):
        grads[n], deltas[n], new_m[n], new_v[n] = small_grad[n].reshape(W[n].shape), d, m2, v2

    return (loss, grad_x, *[grads[n] for n in WEIGHT_ORDER], *[deltas[n] for n in WEIGHT_ORDER],
            *[new_m[n] for n in WEIGHT_ORDER], *[new_v[n] for n in WEIGHT_ORDER])
```

```python
import functools

import jax
import jax.numpy as jnp
from jax import lax
from jax.experimental import pallas as pl
from jax.experimental.pallas import tpu as pltpu

F32 = jnp.float32
BF16 = jnp.bfloat16

D_MODEL = 1024
D_FF = 2816
D_RNN = 1280
DEPTH = 4
N_CHIPS = 4
FF_CHUNK = D_FF // N_CHIPS
RNN_IN_CHUNK = 2 * D_RNN // N_CHIPS
GATE_CHUNK = 640
N_GATE_PLANES = 5
LRU_HEADS = 16
LRU_HEAD_DIM = 80
CONV_WIDTH = 4
LRU_C = 8.0
POOL_WINDOWS = (2, 4, 8, 16)
POOL_GROUP_DIM = 256
PLE_DIM = 256
RMS_EPS = 1e-6
POOL_HALO = 16
SUBLANES = 8
LANES = 128

ADAM_LR = 0.001
ADAM_B1 = 0.9
ADAM_B2 = 0.999
ADAM_EPS = 1e-08
ADAM_WD = 0.01
ADAM_STEP = 10

VMEM_LIMIT_MB = 56
XT_DY_WINDOW_BYTES = 40 * 2 ** 20
MESH_ID = pl.DeviceIdType.MESH
CHIP_FLIPS = ((1, 0), (0, 1), (1, 1))


def _cparams(semantics):
    return pltpu.CompilerParams(dimension_semantics=semantics, vmem_limit_bytes=VMEM_LIMIT_MB * 2 ** 20)


def _dot(a, b):
    return lax.dot_general(a, b, (((1,), (0,)), ((), ())), preferred_element_type=F32)


def _dot_nt(a, b):
    return lax.dot_general(a, b, (((1,), (1,)), ((), ())), preferred_element_type=F32)


def _dot_tn(a, b):
    return lax.dot_general(a, b, (((0,), (0,)), ((), ())), preferred_element_type=F32)


def _sigmoid(x):
    return 1.0 / (1.0 + jnp.exp(-x))


def _rms(hf, gamma):
    rstd = lax.rsqrt(jnp.mean(hf * hf, axis=-1, keepdims=True) + RMS_EPS)
    xhat = hf * rstd
    return xhat, rstd, xhat * gamma


def _rms_bwd(xhat, rstd, gamma, dxn):
    dxhat = dxn * gamma
    m = jnp.mean(dxhat * xhat, axis=-1, keepdims=True)
    return rstd * (dxhat - xhat * m), _rowsum8(dxn * xhat)


def _rowsum8(v):
    tm, n = v.shape
    return jnp.sum(v.reshape(tm // SUBLANES, SUBLANES, n), axis=0)


def _gelu(x):
    u = 0.7978845608028654 * (x + 0.044715 * x * x * x)
    return 0.5 * x * (1.0 + jnp.tanh(u))


def _gelu_and_grad(x):
    c = 0.7978845608028654
    u = c * (x + 0.044715 * x * x * x)
    th = jnp.tanh(u)
    g = 0.5 * x * (1.0 + th)
    dg = 0.5 * (1.0 + th) + 0.5 * x * (1.0 - th * th) * c * (1.0 + 3.0 * 0.044715 * x * x)
    return g, dg


def _softplus(z):
    e = jnp.exp(-jnp.abs(z))
    u = 1.0 + e
    log1p = jnp.where(u == 1.0, e, jnp.log(u) * e / jnp.where(u == 1.0, 1.0, u - 1.0))
    return jnp.maximum(z, 0.0) + log1p


def _neg_expm1(x):
    series = -x * (1.0 + x * (0.5 + x * (1.0 / 6.0)))
    return jnp.where(x > -1e-2, series, 1.0 - jnp.exp(x))


def _shift_down(ext, j, halo):
    return pltpu.roll(ext, j, 0)[halo:]


def _shift_up(ext, j, tm):
    n = ext.shape[0]
    return pltpu.roll(ext, n - j, 0)[:tm]


def _scan_causal(a, b):
    tm, n = a.shape
    head_rows = lax.broadcasted_iota(jnp.int32, (SUBLANES, n), 0)
    s = 1
    while s < min(SUBLANES, tm):
        keep = head_rows >= s
        a_r, b_r = pltpu.roll(a, s, 0), pltpu.roll(b, s, 0)
        a_sh = jnp.concatenate([jnp.where(keep, a_r[:SUBLANES], 1.0), a_r[SUBLANES:]], axis=0)
        b_sh = jnp.concatenate([jnp.where(keep, b_r[:SUBLANES], 0.0), b_r[SUBLANES:]], axis=0)
        b = a * b_sh + b
        a = a * a_sh
        s *= 2
    while s < tm:
        b = jnp.concatenate([b[:s], a[s:] * b[:tm - s] + b[s:]], axis=0)
        a = jnp.concatenate([a[:s], a[s:] * a[:tm - s]], axis=0)
        s *= 2
    return a, b


def _scan_anticausal(c, d):
    tm, n = c.shape
    body = tm - SUBLANES
    tail_rows = lax.broadcasted_iota(jnp.int32, (SUBLANES, n), 0) + body
    s = 1
    while s < min(SUBLANES, tm):
        keep = tail_rows < tm - s
        c_r, d_r = pltpu.roll(c, tm - s, 0), pltpu.roll(d, tm - s, 0)
        c_sh = jnp.concatenate([c_r[:body], jnp.where(keep, c_r[body:], 1.0)], axis=0)
        d_sh = jnp.concatenate([d_r[:body], jnp.where(keep, d_r[body:], 0.0)], axis=0)
        d = d + c * d_sh
        c = c * c_sh
        s *= 2
    while s < tm:
        d = jnp.concatenate([d[:tm - s] + c[:tm - s] * d[s:], d[tm - s:]], axis=0)
        c = jnp.concatenate([c[:tm - s] * c[s:], c[tm - s:]], axis=0)
        s *= 2
    return c, d


def _rows3(stacked):
    return stacked.reshape(stacked.shape[0], 1, stacked.shape[-1])


class _Rows:
    def __init__(self, stacked3, index):
        assert stacked3.ndim == 3, stacked3.shape
        self.array = stacked3
        self.index = index

    def spec(self):
        index = self.index
        return pl.BlockSpec((None,) + self.array.shape[1:], lambda *_: (index, 0, 0))


def _tile(n, want):
    t = min(n, want)
    assert n % t == 0, (n, t)
    return t


def _ffn_fwd(h, gamma, wg, wu, wd, layer, cargo=None):
    S = h.shape[0]
    tm = _tile(S, 1024)
    nt = S // tm
    cargo = cargo or _Cargo()
    n_in, n_out = 5, 4
    nc_in, nc_out = len(cargo.operands), len(cargo.out_shapes)

    def body(*refs):
        h_ref, g_ref, wg_ref, wu_ref, wd_ref = refs[:n_in]
        c_ins = refs[n_in:n_in + nc_in]
        ho_ref, xn_ref, gg_ref, uu_ref = refs[n_in + nc_in:n_in + nc_in + n_out]
        c_outs = refs[n_in + nc_in + n_out:n_in + nc_in + n_out + nc_out]
        xn_s, acc_s = refs[n_in + nc_in + n_out + nc_out:n_in + nc_in + n_out + nc_out + 2]
        sems = refs[n_in + nc_in + n_out + nc_out + 2:]
        t, k = pl.program_id(0), pl.program_id(1)

        @pl.when((t == 0) & (k == 0))
        def _():
            cargo.start(c_ins, c_outs, sems)

        @pl.when((t == nt - 1) & (k == 0))
        def _():
            cargo.forward(c_ins, c_outs, sems)

        @pl.when(k == 0)
        def _():
            _, _, xn = _rms(h_ref[...], g_ref[...])
            xnb = xn.astype(BF16)
            xn_s[...] = xnb
            xn_ref[...] = xnb
            acc_s[...] = jnp.zeros_like(acc_s)

        xnb = xn_s[...]
        g = _dot_nt(xnb, wg_ref[...])
        u = _dot_nt(xnb, wu_ref[...])
        gg_ref[...] = g.astype(BF16)
        uu_ref[...] = u.astype(BF16)
        hid = (g * _sigmoid(g)) * u
        acc_s[...] += _dot(hid.astype(BF16), wd_ref[...])

        @pl.when(k == N_CHIPS - 1)
        def _():
            ho_ref[...] = h_ref[...] + 0.5 * acc_s[...]

        @pl.when((t == nt - 1) & (k == N_CHIPS - 1))
        def _():
            cargo.finish(c_ins, c_outs, sems)

    outs = pl.pallas_call(
        body, name="ffn_fwd",
        grid=(nt, N_CHIPS),
        in_specs=[
            pl.BlockSpec((tm, D_MODEL), lambda t, k: (t, 0)),
            gamma.spec(),
            pl.BlockSpec((None, None, FF_CHUNK, D_MODEL), lambda t, k: (layer, k, 0, 0)),
            pl.BlockSpec((None, None, FF_CHUNK, D_MODEL), lambda t, k: (layer, k, 0, 0)),
            pl.BlockSpec((None, None, FF_CHUNK, D_MODEL), lambda t, k: (layer, k, 0, 0)),
        ] + _any_specs(nc_in),
        out_specs=[
            pl.BlockSpec((tm, D_MODEL), lambda t, k: (t, 0)),
            pl.BlockSpec((tm, D_MODEL), lambda t, k: (t, 0)),
            pl.BlockSpec((None, tm, FF_CHUNK), lambda t, k: (k, t, 0)),
            pl.BlockSpec((None, tm, FF_CHUNK), lambda t, k: (k, t, 0)),
        ] + _any_specs(nc_out),
        out_shape=[
            jax.ShapeDtypeStruct((S, D_MODEL), F32),
            jax.ShapeDtypeStruct((S, D_MODEL), BF16),
            jax.ShapeDtypeStruct((N_CHIPS, S, FF_CHUNK), BF16),
            jax.ShapeDtypeStruct((N_CHIPS, S, FF_CHUNK), BF16),
        ] + cargo.out_shapes,
        input_output_aliases={n_in + i: n_out + o for i, o in cargo.aliases.items()},
        scratch_shapes=[pltpu.VMEM((tm, D_MODEL), BF16), pltpu.VMEM((tm, D_MODEL), F32)] + cargo.sem_shapes,
        compiler_params=_cparams(("arbitrary", "arbitrary")),
    )(h, gamma.array, wg, wu, wd, *cargo.operands)
    return outs[:n_out], list(outs[n_out:])


def _ffn_bwd(xn, dout, gg, uu, wg, wu, wd, layer, cargo=None):
    S = xn.shape[0]
    tm = _tile(S, 512)
    nt = S // tm
    nchunk = N_CHIPS - 1
    cargo = cargo or _Cargo()
    n_in, n_out = 7, 4
    nc_in, nc_out = len(cargo.operands), len(cargo.out_shapes)

    def body(*refs):
        xn_ref, do_ref, gg_ref, uu_ref, wg_ref, wu_ref, wd_ref = refs[:n_in]
        c_ins = refs[n_in:n_in + nc_in]
        dwg_ref, dwu_ref, dwd_ref, slab_ref = refs[n_in + nc_in:n_in + nc_in + n_out]
        c_outs = refs[n_in + nc_in + n_out:n_in + nc_in + n_out + nc_out]
        sems = refs[n_in + nc_in + n_out + nc_out:]
        k, t = pl.program_id(0), pl.program_id(1)

        @pl.when((k == 0) & (t == 0))
        def _():
            cargo.start(c_ins, c_outs, sems)

        @pl.when(t == 0)
        def _():
            dwg_ref[...] = jnp.zeros_like(dwg_ref)
            dwu_ref[...] = jnp.zeros_like(dwu_ref)
            dwd_ref[...] = jnp.zeros_like(dwd_ref)

        xnb = xn_ref[...]
        dob = (0.5 * do_ref[...]).astype(BF16)
        g = gg_ref[...].astype(F32)
        u = uu_ref[...].astype(F32)
        s = _sigmoid(g)
        sil = g * s
        dhid = _dot_nt(dob, wd_ref[...])
        dwd_ref[...] += _dot_tn((sil * u).astype(BF16), dob)
        du = (dhid * sil).astype(BF16)
        dg = (dhid * u * (s * (1.0 + g * (1.0 - s)))).astype(BF16)
        dwg_ref[...] += _dot_tn(dg, xnb)
        dwu_ref[...] += _dot_tn(du, xnb)
        slab_ref[...] = (_dot(dg, wg_ref[...]) + _dot(du, wu_ref[...])).astype(BF16)

        @pl.when((k == nchunk - 1) & (t == nt - 1))
        def _():
            cargo.finish(c_ins, c_outs, sems)

    outs = pl.pallas_call(
        body, name="ffn_bwd",
        grid=(nchunk, nt),
        in_specs=[
            pl.BlockSpec((tm, D_MODEL), lambda k, t: (t, 0)),
            pl.BlockSpec((tm, D_MODEL), lambda k, t: (t, 0)),
            pl.BlockSpec((None, tm, FF_CHUNK), lambda k, t: (k, t, 0)),
            pl.BlockSpec((None, tm, FF_CHUNK), lambda k, t: (k, t, 0)),
            pl.BlockSpec((None, None, FF_CHUNK, D_MODEL), lambda k, t: (layer, k, 0, 0)),
            pl.BlockSpec((None, None, FF_CHUNK, D_MODEL), lambda k, t: (layer, k, 0, 0)),
            pl.BlockSpec((None, None, FF_CHUNK, D_MODEL), lambda k, t: (layer, k, 0, 0)),
        ] + _any_specs(nc_in),
        out_specs=[
            pl.BlockSpec((None, FF_CHUNK, D_MODEL), lambda k, t: (k, 0, 0)),
            pl.BlockSpec((None, FF_CHUNK, D_MODEL), lambda k, t: (k, 0, 0)),
            pl.BlockSpec((None, FF_CHUNK, D_MODEL), lambda k, t: (k, 0, 0)),
            pl.BlockSpec((None, tm, D_MODEL), lambda k, t: (k, t, 0)),
        ] + _any_specs(nc_out),
        out_shape=[
            jax.ShapeDtypeStruct((N_CHIPS, FF_CHUNK, D_MODEL), F32),
            jax.ShapeDtypeStruct((N_CHIPS, FF_CHUNK, D_MODEL), F32),
            jax.ShapeDtypeStruct((N_CHIPS, FF_CHUNK, D_MODEL), F32),
            jax.ShapeDtypeStruct((nchunk, S, D_MODEL), BF16),
        ] + cargo.out_shapes,
        input_output_aliases={n_in + i: n_out + o for i, o in cargo.aliases.items()},
        scratch_shapes=list(cargo.sem_shapes),
        compiler_params=_cparams(("arbitrary", "arbitrary")),
    )(xn, dout, gg, uu, wg, wu, wd, *cargo.operands)
    return outs[:n_out], list(outs[n_out:])


def _ffn_bwd_last(xn, dout, gg, uu, wg, wu, wd, layer, slabs, h, gamma, dwg, dwu, dwd):
    S = xn.shape[0]
    tm = _tile(S, 512)
    k = N_CHIPS - 1
    nprev = slabs.shape[0]

    def body(xn_ref, do_ref, gg_ref, uu_ref, wg_ref, wu_ref, wd_ref, slab_ref, h_ref, g_ref, _dwg, _dwu, _dwd,
             dh_ref, dgam_ref, dwg_ref, dwu_ref, dwd_ref):
        @pl.when(pl.program_id(0) == 0)
        def _():
            dgam_ref[...] = jnp.zeros_like(dgam_ref)
            dwg_ref[...] = jnp.zeros_like(dwg_ref)
            dwu_ref[...] = jnp.zeros_like(dwu_ref)
            dwd_ref[...] = jnp.zeros_like(dwd_ref)

        xnb = xn_ref[...]
        do = do_ref[...]
        dob = (0.5 * do).astype(BF16)
        g = gg_ref[...].astype(F32)
        u = uu_ref[...].astype(F32)
        s = _sigmoid(g)
        sil = g * s
        dhid = _dot_nt(dob, wd_ref[...])
        dwd_ref[...] += _dot_tn((sil * u).astype(BF16), dob)
        du = (dhid * sil).astype(BF16)
        dg = (dhid * u * (s * (1.0 + g * (1.0 - s)))).astype(BF16)
        dwg_ref[...] += _dot_tn(dg, xnb)
        dwu_ref[...] += _dot_tn(du, xnb)
        dxn = _dot(dg, wg_ref[...]) + _dot(du, wu_ref[...])
        for i in range(nprev):
            dxn = dxn + slab_ref[i].astype(F32)
        xhat, rstd, _ = _rms(h_ref[...], g_ref[...])
        dhn, dgam = _rms_bwd(xhat, rstd, g_ref[...], dxn)
        dh_ref[...] = do + dhn
        dgam_ref[...] += dgam

    tile = pl.BlockSpec((tm, D_MODEL), lambda t: (t, 0))
    hidden = pl.BlockSpec((None, tm, FF_CHUNK), lambda t: (k, t, 0))
    w_in = pl.BlockSpec((None, None, FF_CHUNK, D_MODEL), lambda t: (layer, k, 0, 0))
    dw_in = pl.BlockSpec((None, FF_CHUNK, D_MODEL), lambda t: (k, 0, 0))
    return pl.pallas_call(
        body, name="ffn_bwd_last",
        grid=(S // tm,),
        in_specs=[tile, tile, hidden, hidden, w_in, w_in,
                  pl.BlockSpec((None, None, FF_CHUNK, D_MODEL), lambda t: (layer, k, 0, 0)),
                  pl.BlockSpec((nprev, tm, D_MODEL), lambda t: (0, t, 0)), tile,
                  gamma.spec()] + _any_specs(3),
        out_specs=[tile, pl.BlockSpec((SUBLANES, D_MODEL), lambda t: (0, 0)), dw_in, dw_in,
                   pl.BlockSpec((None, FF_CHUNK, D_MODEL), lambda t: (k, 0, 0))],
        out_shape=[jax.ShapeDtypeStruct((S, D_MODEL), F32), jax.ShapeDtypeStruct((SUBLANES, D_MODEL), F32),
                   jax.ShapeDtypeStruct(dwg.shape, F32), jax.ShapeDtypeStruct(dwu.shape, F32),
                   jax.ShapeDtypeStruct(dwd.shape, F32)],
        input_output_aliases={10: 2, 11: 3, 12: 4},
        compiler_params=_cparams(("arbitrary",)),
    )(xn, dout, gg, uu, wg, wu, wd, slabs, h, gamma.array, dwg, dwu, dwd)


def _xt_dy(name, x, dy, nchunk, kb, nb, x_by_chunk, y_by_chunk):
    S = x.shape[0]
    want = 512
    for cand in (4096, 2048, 1024):
        windows = 2 * cand * (kb * x.dtype.itemsize + nb * dy.dtype.itemsize) + 2 * kb * nb * 4
        if windows <= XT_DY_WINDOW_BYTES:
            want = cand
            break
    tm = _tile(S, want)

    def body(x_ref, dy_ref, o_ref):
        @pl.when(pl.program_id(1) == 0)
        def _():
            o_ref[...] = jnp.zeros_like(o_ref)

        o_ref[...] += _dot_tn(x_ref[...].astype(BF16), dy_ref[...].astype(BF16))

    return pl.pallas_call(
        body, name=name,
        grid=(nchunk, S // tm),
        in_specs=[
            pl.BlockSpec((tm, kb), (lambda c, t: (t, c)) if x_by_chunk else (lambda c, t: (t, 0))),
            pl.BlockSpec((tm, nb), (lambda c, t: (t, c)) if y_by_chunk else (lambda c, t: (t, 0))),
        ],
        out_specs=pl.BlockSpec((None, kb, nb), lambda c, t: (c, 0, 0)),
        out_shape=jax.ShapeDtypeStruct((nchunk, kb, nb), F32),
        compiler_params=_cparams(("arbitrary", "arbitrary")),
    )(x, dy)


def _lru_gates(xc, wbd_ref, ba, bx, apar):
    xcb = xc.astype(BF16)
    r_parts, ig_parts = [], []
    for q in range(D_RNN // GATE_CHUNK):
        lo, hi = q * GATE_CHUNK, (q + 1) * GATE_CHUNK
        pre = _dot(xcb[:, lo:hi], wbd_ref[q])
        r_parts.append(_sigmoid(pre[:, :GATE_CHUNK] + ba[:, lo:hi]))
        ig_parts.append(_sigmoid(pre[:, GATE_CHUNK:] + bx[:, lo:hi]))
    r = jnp.concatenate(r_parts, axis=1)
    ig = jnp.concatenate(ig_parts, axis=1)
    sp = LRU_C * _softplus(-apar)
    log_a = -(r * sp)
    a = jnp.exp(log_a)
    mult = jnp.sqrt(_neg_expm1(2.0 * log_a))
    return r, ig, a, mult, sp


def _conv_causal(xb, tail, cw_ref, cb):
    ext = jnp.concatenate([tail, xb], axis=0)
    xc = cb + cw_ref[CONV_WIDTH - 1:CONV_WIDTH, :] * xb
    for j in range(1, CONV_WIDTH):
        xc = xc + cw_ref[CONV_WIDTH - 1 - j:CONV_WIDTH - j, :] * _shift_down(ext, j, SUBLANES)
    return xc, ext


def _lru_fwd(h, gamma, win, layer, convw, convb, wbd, ba, bx, apar, wout):
    S = h.shape[0]
    tm = _tile(S, 256)

    def body(h_ref, g_ref, win_ref, cw_ref, cb_ref, wbd_ref, ba_ref, bx_ref, ap_ref, wout_ref,
             ho_ref, xn_ref, z_ref, hs_ref, gates_ref, tail_s, carry_s):
        @pl.when(pl.program_id(0) == 0)
        def _():
            tail_s[...] = jnp.zeros_like(tail_s)
            carry_s[...] = jnp.zeros_like(carry_s)

        hf = h_ref[...]
        _, _, xn = _rms(hf, g_ref[...])
        xnb = xn.astype(BF16)
        xn_ref[...] = xnb
        for k in range(N_CHIPS):
            z_ref[:, k * RNN_IN_CHUNK:(k + 1) * RNN_IN_CHUNK] = _dot(xnb, win_ref[k])
        gate = z_ref[:, :D_RNN]
        xb = z_ref[:, D_RNN:]
        xc, _ = _conv_causal(xb, tail_s[...], cw_ref, cb_ref[...])
        tail_s[...] = xb[tm - SUBLANES:, :]
        r, ig, a, mult, _ = _lru_gates(xc, wbd_ref, ba_ref[...], bx_ref[...], ap_ref[...])
        for plane, val in enumerate((xc, r, ig, a, mult)):
            gates_ref[plane] = val
        big_a, big_b = _scan_causal(a, mult * (ig * xc))
        hs = big_a * carry_s[SUBLANES - 1:SUBLANES, :] + big_b
        hs_ref[...] = hs
        carry_s[...] = hs[tm - SUBLANES:, :]
        y = hs * _gelu(gate)
        ho_ref[...] = hf + _dot(y.astype(BF16), wout_ref[...])

    return pl.pallas_call(
        body, name="lru_fwd",
        grid=(S // tm,),
        in_specs=[
            pl.BlockSpec((tm, D_MODEL), lambda t: (t, 0)),
            gamma.spec(),
            pl.BlockSpec((None, N_CHIPS, D_MODEL, RNN_IN_CHUNK), lambda t: (layer, 0, 0, 0)),
            convw.spec(),
            convb.spec(),
            pl.BlockSpec((D_RNN // GATE_CHUNK, GATE_CHUNK, 2 * GATE_CHUNK), lambda t: (0, 0, 0)),
            ba.spec(), bx.spec(), apar.spec(),
            pl.BlockSpec((None, D_RNN, D_MODEL), lambda t: (layer, 0, 0)),
        ],
        out_specs=[
            pl.BlockSpec((tm, D_MODEL), lambda t: (t, 0)),
            pl.BlockSpec((tm, D_MODEL), lambda t: (t, 0)),
            pl.BlockSpec((tm, 2 * D_RNN), lambda t: (t, 0)),
            pl.BlockSpec((tm, D_RNN), lambda t: (t, 0)),
            pl.BlockSpec((N_GATE_PLANES, tm, D_RNN), lambda t: (0, t, 0)),
        ],
        out_shape=[
            jax.ShapeDtypeStruct((S, D_MODEL), F32),
            jax.ShapeDtypeStruct((S, D_MODEL), BF16),
            jax.ShapeDtypeStruct((S, 2 * D_RNN), F32),
            jax.ShapeDtypeStruct((S, D_RNN), F32),
            jax.ShapeDtypeStruct((N_GATE_PLANES, S, D_RNN), F32),
        ],
        scratch_shapes=[pltpu.VMEM((SUBLANES, D_RNN), F32), pltpu.VMEM((SUBLANES, D_RNN), F32)],
        compiler_params=_cparams(("arbitrary",)),
    )(h, gamma.array, win, convw.array, convb.array, wbd, ba.array, bx.array, apar.array, wout)


def _lru_bwd_seq(dout, z, hs, gates, convw, wbd, apar, wout, layer, cargo=None):
    S = dout.shape[0]
    tm = _tile(S, 256)
    nt = S // tm
    per8 = tm // SUBLANES
    rev = lambda i: nt - 1 - i
    prev8 = lambda i: jnp.maximum(rev(i) * per8 - 1, 0)
    cargo = cargo or _Cargo()
    n_in, n_out = 10, 6

    def body(*refs):
        ins, c_ins, outs, c_outs, scratch, sems = _cargo_refs(refs, cargo, n_in, n_out, n_scratch=3)
        do_ref, z_ref, hs_ref, gates_ref, ztail_ref, hstail_ref, cw_ref, wbd_ref, ap_ref, wout_ref = ins
        dz_ref, dpre_ref, xc_ref, y_ref, dcw_ref, vec_ref = outs
        a_first_s, g_first_s, dxc_head_s = scratch
        i = pl.program_id(0)
        first_in_time = rev(i) == 0

        @pl.when(i == 0)
        def _():
            cargo.start(c_ins, c_outs, sems)
            a_first_s[...] = jnp.zeros_like(a_first_s)
            g_first_s[...] = jnp.zeros_like(g_first_s)
            dxc_head_s[...] = jnp.zeros_like(dxc_head_s)
            dcw_ref[...] = jnp.zeros_like(dcw_ref)
            vec_ref[...] = jnp.zeros_like(vec_ref)

        gate = z_ref[:, :D_RNN]
        xb = z_ref[:, D_RNN:]
        hist = jnp.where(first_in_time, 0.0, 1.0)
        xext = jnp.concatenate([ztail_ref[:, D_RNN:] * hist, xb], axis=0)
        xc, r, ig, a, mult = (gates_ref[plane] for plane in range(N_GATE_PLANES))
        sp = LRU_C * _softplus(-ap_ref[...])
        hs = hs_ref[...]
        gel, dgel = _gelu_and_grad(gate)
        y = hs * gel
        y_ref[...] = y.astype(BF16)
        xc_ref[...] = xc.astype(BF16)

        dy = _dot_nt(do_ref[...].astype(BF16), wout_ref[...])
        dhs = dy * gel
        dgate = dy * hs * dgel

        coef = _shift_up(jnp.concatenate([a, a_first_s[...]], axis=0), 1, tm)
        big_c, big_d = _scan_anticausal(coef, dhs)
        g = big_d + big_c * g_first_s[0:1, :]
        g_first_s[...] = g[:SUBLANES, :]
        a_first_s[...] = a[:SUBLANES, :]

        hs_prev = _shift_down(jnp.concatenate([hstail_ref[...] * hist, hs], axis=0), 1, SUBLANES)
        da = g * hs_prev
        dmult = g * ig * xc
        dig = g * mult * xc
        dxc = g * mult * ig
        dlog_a = da * a - dmult * (a * a) / mult
        dr = -(dlog_a * sp)
        dpre_a = dr * r * (1.0 - r)
        dpre_x = dig * ig * (1.0 - ig)
        d_apar = dlog_a * r * (LRU_C * _sigmoid(-ap_ref[...]))

        for q in range(D_RNN // GATE_CHUNK):
            lo, hi = q * GATE_CHUNK, (q + 1) * GATE_CHUNK
            dpre_q = jnp.concatenate([dpre_a[:, lo:hi], dpre_x[:, lo:hi]], axis=1).astype(BF16)
            dpre_ref[:, 2 * lo:2 * hi] = dpre_q
            dxc_q = _dot_nt(dpre_q, wbd_ref[q])
            if q == 0:
                dxc_parts = [dxc_q]
            else:
                dxc_parts.append(dxc_q)
        dxc = dxc + jnp.concatenate(dxc_parts, axis=1)

        dext = jnp.concatenate([dxc, dxc_head_s[...]], axis=0)
        dxb = cw_ref[CONV_WIDTH - 1:CONV_WIDTH, :] * dxc
        for j in range(1, CONV_WIDTH):
            dxb = dxb + cw_ref[CONV_WIDTH - 1 - j:CONV_WIDTH - j, :] * _shift_up(dext, j, tm)
        dxc_head_s[...] = dxc[:SUBLANES, :]
        dz_ref[:, :D_RNN] = dgate.astype(BF16)
        dz_ref[:, D_RNN:] = dxb.astype(BF16)

        dcw_ref[CONV_WIDTH - 1] += _rowsum8(dxc * xb)
        for j in range(1, CONV_WIDTH):
            dcw_ref[CONV_WIDTH - 1 - j] += _rowsum8(dxc * _shift_down(xext, j, SUBLANES))
        vec_ref[0] += _rowsum8(dxc)
        vec_ref[1] += _rowsum8(dpre_a)
        vec_ref[2] += _rowsum8(dpre_x)
        vec_ref[3] += _rowsum8(d_apar)

        @pl.when(i == nt - 1)
        def _():
            cargo.finish(c_ins, c_outs, sems)

    outs = pl.pallas_call(
        body, name="lru_bwd_seq",
        grid=(nt,),
        in_specs=[
            pl.BlockSpec((tm, D_MODEL), lambda i: (rev(i), 0)),
            pl.BlockSpec((tm, 2 * D_RNN), lambda i: (rev(i), 0)),
            pl.BlockSpec((tm, D_RNN), lambda i: (rev(i), 0)),
            pl.BlockSpec((N_GATE_PLANES, tm, D_RNN), lambda i: (0, rev(i), 0)),
            pl.BlockSpec((SUBLANES, 2 * D_RNN), lambda i: (prev8(i), 0)),
            pl.BlockSpec((SUBLANES, D_RNN), lambda i: (prev8(i), 0)),
            convw.spec(),
            pl.BlockSpec((D_RNN // GATE_CHUNK, GATE_CHUNK, 2 * GATE_CHUNK), lambda i: (0, 0, 0)),
            apar.spec(),
            pl.BlockSpec((None, D_RNN, D_MODEL), lambda i: (layer, 0, 0)),
        ] + _any_specs(len(cargo.operands)),
        out_specs=[
            pl.BlockSpec((tm, 2 * D_RNN), lambda i: (rev(i), 0)),
            pl.BlockSpec((tm, 2 * D_RNN), lambda i: (rev(i), 0)),
            pl.BlockSpec((tm, D_RNN), lambda i: (rev(i), 0)),
            pl.BlockSpec((tm, D_RNN), lambda i: (rev(i), 0)),
            pl.BlockSpec((CONV_WIDTH, SUBLANES, D_RNN), lambda i: (0, 0, 0)),
            pl.BlockSpec((4, SUBLANES, D_RNN), lambda i: (0, 0, 0)),
        ] + _any_specs(len(cargo.out_shapes)),
        out_shape=[
            jax.ShapeDtypeStruct((S, 2 * D_RNN), BF16),
            jax.ShapeDtypeStruct((S, 2 * D_RNN), BF16),
            jax.ShapeDtypeStruct((S, D_RNN), BF16),
            jax.ShapeDtypeStruct((S, D_RNN), BF16),
            jax.ShapeDtypeStruct((CONV_WIDTH, SUBLANES, D_RNN), F32),
            jax.ShapeDtypeStruct((4, SUBLANES, D_RNN), F32),
        ] + cargo.out_shapes,
        input_output_aliases={n_in + i: n_out + o for i, o in cargo.aliases.items()},
        scratch_shapes=[pltpu.VMEM((SUBLANES, D_RNN), F32)] * 3 + cargo.sem_shapes,
        compiler_params=_cparams(("arbitrary",)),
    )(dout, z, hs, gates, z, hs, convw.array, wbd, apar.array, wout, *cargo.operands)
    return outs[:n_out], list(outs[n_out:])


def _lru_bwd_in(dz, h, gamma, dres, win, layer, cargo=None):
    S = h.shape[0]
    tm = _tile(S, 512)
    nt = S // tm
    cargo = cargo or _Cargo()
    n_in, n_out = 5, 2

    def body(*refs):
        (dz_ref, h_ref, g_ref, dres_ref, win_ref), c_ins, (dh_ref, dgam_ref), c_outs, _, sems = _cargo_refs(
            refs, cargo, n_in, n_out)

        @pl.when(pl.program_id(0) == 0)
        def _():
            cargo.start(c_ins, c_outs, sems)

        dxn = _dot_nt(dz_ref[:, :RNN_IN_CHUNK], win_ref[0])
        for k in range(1, N_CHIPS):
            dxn = dxn + _dot_nt(dz_ref[:, k * RNN_IN_CHUNK:(k + 1) * RNN_IN_CHUNK], win_ref[k])
        xhat, rstd, _ = _rms(h_ref[...], g_ref[...])
        dhn, dgam = _rms_bwd(xhat, rstd, g_ref[...], dxn)
        dh_ref[...] = dres_ref[...] + dhn

        @pl.when(pl.program_id(0) == 0)
        def _():
            dgam_ref[...] = jnp.zeros_like(dgam_ref)

        dgam_ref[...] += dgam

        @pl.when(pl.program_id(0) == nt - 1)
        def _():
            cargo.finish(c_ins, c_outs, sems)

    outs = pl.pallas_call(
        body, name="lru_bwd_in",
        grid=(nt,),
        in_specs=[
            pl.BlockSpec((tm, 2 * D_RNN), lambda t: (t, 0)),
            pl.BlockSpec((tm, D_MODEL), lambda t: (t, 0)),
            gamma.spec(),
            pl.BlockSpec((tm, D_MODEL), lambda t: (t, 0)),
            pl.BlockSpec((None, N_CHIPS, D_MODEL, RNN_IN_CHUNK), lambda t: (layer, 0, 0, 0)),
        ] + _any_specs(len(cargo.operands)),
        out_specs=[
            pl.BlockSpec((tm, D_MODEL), lambda t: (t, 0)),
            pl.BlockSpec((SUBLANES, D_MODEL), lambda t: (0, 0)),
        ] + _any_specs(len(cargo.out_shapes)),
        out_shape=[jax.ShapeDtypeStruct((S, D_MODEL), F32),
                   jax.ShapeDtypeStruct((SUBLANES, D_MODEL), F32)] + cargo.out_shapes,
        input_output_aliases={n_in + i: n_out + o for i, o in cargo.aliases.items()},
        scratch_shapes=list(cargo.sem_shapes),
        compiler_params=_cparams(("arbitrary",)),
    )(dz, h, gamma.array, dres, win, *cargo.operands)
    return outs[:n_out], list(outs[n_out:])


def _pool_inv_count(t_index, tm):
    rows = (lax.broadcasted_iota(jnp.int32, (tm, D_MODEL), 0) + t_index * tm + 1).astype(F32)
    col = lax.broadcasted_iota(jnp.int32, (tm, D_MODEL), 1)
    win = jnp.where(col < POOL_GROUP_DIM, float(POOL_WINDOWS[0]),
                    jnp.where(col < 2 * POOL_GROUP_DIM, float(POOL_WINDOWS[1]),
                              jnp.where(col < 3 * POOL_GROUP_DIM, float(POOL_WINDOWS[2]), float(POOL_WINDOWS[3]))))
    return 1.0 / jnp.minimum(rows, win)


def _window_sums(ext, shift, take):
    gd = POOL_GROUP_DIM
    s2 = ext + shift(ext, 1)
    s4 = s2[:, gd:] + shift(s2[:, gd:], 2)
    s8 = s4[:, gd:] + shift(s4[:, gd:], 4)
    s16 = s8[:, gd:] + shift(s8[:, gd:], 8)
    return jnp.concatenate([take(s2[:, :gd]), take(s4[:, :gd]), take(s8[:, :gd]), take(s16)], axis=1)


def _pool_fwd(h, gamma, pw, pb, pscale, layer):
    S = h.shape[0]
    tm = _tile(S, 512)

    def body(h_ref, g_ref, pw_ref, pb_ref, ps_ref, ho_ref, u_ref, tail_s):
        t = pl.program_id(0)

        @pl.when(t == 0)
        def _():
            tail_s[...] = jnp.zeros_like(tail_s)

        hf = h_ref[...]
        _, _, hn = _rms(hf, g_ref[...])
        ext = jnp.concatenate([tail_s[...], hn], axis=0)
        tail_s[...] = hn[tm - POOL_HALO:, :]
        sums = _window_sums(ext, lambda v, j: pltpu.roll(v, j, 0), lambda v: v[POOL_HALO:])
        ub = (sums * _pool_inv_count(t, tm) - hn).astype(BF16)
        u_ref[...] = ub
        ys = [_dot(ub[:, g * POOL_GROUP_DIM:(g + 1) * POOL_GROUP_DIM], pw_ref[g]) for g in range(len(POOL_WINDOWS))]
        y = jnp.concatenate(ys, axis=1)
        ho_ref[...] = hf + (y + pb_ref[...]) * ps_ref[...]

    return pl.pallas_call(
        body, name="pool_fwd",
        grid=(S // tm,),
        in_specs=[
            pl.BlockSpec((tm, D_MODEL), lambda t: (t, 0)), gamma.spec(),
            pl.BlockSpec((None, len(POOL_WINDOWS), POOL_GROUP_DIM, POOL_GROUP_DIM), lambda t: (layer, 0, 0, 0)),
            pb.spec(), pscale.spec(),
        ],
        out_specs=[pl.BlockSpec((tm, D_MODEL), lambda t: (t, 0)), pl.BlockSpec((tm, D_MODEL), lambda t: (t, 0))],
        out_shape=[jax.ShapeDtypeStruct((S, D_MODEL), F32), jax.ShapeDtypeStruct((S, D_MODEL), BF16)],
        scratch_shapes=[pltpu.VMEM((POOL_HALO, D_MODEL), F32)],
        compiler_params=_cparams(("arbitrary",)),
    )(h, gamma.array, pw, pb.array, pscale.array)


def _pool_bwd(dout, h, u, gamma, pw, pb, pscale, layer):
    S = h.shape[0]
    tm = _tile(S, 512)
    nt = S // tm
    rev = lambda i: nt - 1 - i
    ngroup = len(POOL_WINDOWS)

    def body(do_ref, h_ref, u_ref, g_ref, pw_ref, pb_ref, ps_ref, dh_ref, dpre_ref, vec_ref, head_s):
        i = pl.program_id(0)

        @pl.when(i == 0)
        def _():
            head_s[...] = jnp.zeros_like(head_s)
            vec_ref[...] = jnp.zeros_like(vec_ref)

        do = do_ref[...]
        ub = u_ref[...]
        gsl = lambda v, g: v[:, g * POOL_GROUP_DIM:(g + 1) * POOL_GROUP_DIM]
        y = jnp.concatenate([_dot(gsl(ub, g), pw_ref[g]) for g in range(ngroup)], axis=1)
        dpre = do * ps_ref[...]
        dpb = dpre.astype(BF16)
        dpre_ref[...] = dpb
        du = jnp.concatenate([_dot_nt(gsl(dpb, g), pw_ref[g]) for g in range(ngroup)], axis=1)
        v = du * _pool_inv_count(rev(i), tm)
        ext = jnp.concatenate([v, head_s[...]], axis=0)
        head_s[...] = v[:POOL_HALO, :]
        n = tm + POOL_HALO
        dhn = _window_sums(ext, lambda w, j: pltpu.roll(w, n - j, 0), lambda w: w[:tm]) - du
        xhat, rstd, _ = _rms(h_ref[...], g_ref[...])
        dh_in, dgam = _rms_bwd(xhat, rstd, g_ref[...], dhn)
        dh_ref[...] = do + dh_in
        vec_ref[0] += dgam
        vec_ref[1] += _rowsum8(dpre)
        vec_ref[2] += _rowsum8(do * (y + pb_ref[...]))

    tile = pl.BlockSpec((tm, D_MODEL), lambda i: (rev(i), 0))
    return pl.pallas_call(
        body, name="pool_bwd",
        grid=(nt,),
        in_specs=[tile, tile, tile, gamma.spec(),
                  pl.BlockSpec((None, ngroup, POOL_GROUP_DIM, POOL_GROUP_DIM), lambda i: (layer, 0, 0, 0)),
                  pb.spec(), pscale.spec()],
        out_specs=[tile, tile, pl.BlockSpec((3, SUBLANES, D_MODEL), lambda i: (0, 0, 0))],
        out_shape=[jax.ShapeDtypeStruct((S, D_MODEL), F32), jax.ShapeDtypeStruct((S, D_MODEL), BF16),
                   jax.ShapeDtypeStruct((3, SUBLANES, D_MODEL), F32)],
        scratch_shapes=[pltpu.VMEM((POOL_HALO, D_MODEL), F32)],
        compiler_params=_cparams(("arbitrary",)),
    )(dout, h, u, gamma.array, pw, pb.array, pscale.array)


def _ple_parts(hf, gamma, p_tile, wgate_ref, wproj_ref):
    xhat, rstd, xn = _rms(hf, gamma)
    xnb = xn.astype(BF16)
    gate = _sigmoid(_dot(xnb, wgate_ref[...]))
    pb = p_tile.astype(BF16)
    proj = jnp.concatenate([_dot(pb, wproj_ref[k]) for k in range(N_CHIPS)], axis=1)
    return xhat, rstd, xnb, pb, gate, proj


def _ple_fwd(h, gamma, p_l, wgate, wproj, layer):
    S = h.shape[0]
    tm = _tile(S, 1024)

    def body(h_ref, g_ref, p_ref, wgate_ref, wproj_ref, ho_ref):
        hf = h_ref[...]
        _, _, _, _, gate, proj = _ple_parts(hf, g_ref[...], p_ref[...], wgate_ref, wproj_ref)
        ho_ref[...] = hf + gate * proj

    return pl.pallas_call(
        body, name="ple_fwd",
        grid=(S // tm,),
        in_specs=[
            pl.BlockSpec((tm, D_MODEL), lambda t: (t, 0)),
            gamma.spec(),
            pl.BlockSpec((None, None, tm, PLE_DIM), lambda t: (layer, 0, t, 0)),
            pl.BlockSpec((None, D_MODEL, D_MODEL), lambda t: (layer, 0, 0)),
            pl.BlockSpec((None, N_CHIPS, PLE_DIM, PLE_DIM), lambda t: (layer, 0, 0, 0)),
        ],
        out_specs=pl.BlockSpec((tm, D_MODEL), lambda t: (t, 0)),
        out_shape=jax.ShapeDtypeStruct((S, D_MODEL), F32),
        compiler_params=_cparams(("arbitrary",)),
    )(h, gamma.array, p_l, wgate, wproj)


def _ple_bwd(dout, h, gamma, p_l, wgate, wproj, layer):
    S = h.shape[0]
    tm = _tile(S, 512)

    def body(do_ref, h_ref, g_ref, p_ref, wgate_ref, wproj_ref, dh_ref, dwg_ref, dwp_ref, dgam_ref):
        @pl.when(pl.program_id(0) == 0)
        def _():
            dgam_ref[...] = jnp.zeros_like(dgam_ref)
            dwg_ref[...] = jnp.zeros_like(dwg_ref)
            dwp_ref[...] = jnp.zeros_like(dwp_ref)

        do = do_ref[...]
        xhat, rstd, xnb, pb, gate, proj = _ple_parts(h_ref[...], g_ref[...], p_ref[...], wgate_ref, wproj_ref)
        dproj = (do * gate).astype(BF16)
        dpre = (do * proj * gate * (1.0 - gate)).astype(BF16)
        dwg_ref[...] += _dot_tn(xnb, dpre)
        for k in range(N_CHIPS):
            dwp_ref[k] += _dot_tn(pb, dproj[:, k * PLE_DIM:(k + 1) * PLE_DIM])
        dhn, dgam = _rms_bwd(xhat, rstd, g_ref[...], _dot_nt(dpre, wgate_ref[...]))
        dh_ref[...] = do + dhn
        dgam_ref[...] += dgam

    tile = pl.BlockSpec((tm, D_MODEL), lambda t: (t, 0))
    return pl.pallas_call(
        body, name="ple_bwd",
        grid=(S // tm,),
        in_specs=[
            tile, tile,
            gamma.spec(),
            pl.BlockSpec((None, None, tm, PLE_DIM), lambda t: (layer, 0, t, 0)),
            pl.BlockSpec((None, D_MODEL, D_MODEL), lambda t: (layer, 0, 0)),
            pl.BlockSpec((None, N_CHIPS, PLE_DIM, PLE_DIM), lambda t: (layer, 0, 0, 0)),
        ],
        out_specs=[tile, pl.BlockSpec((D_MODEL, D_MODEL), lambda t: (0, 0)),
                   pl.BlockSpec((N_CHIPS, PLE_DIM, PLE_DIM), lambda t: (0, 0, 0)),
                   pl.BlockSpec((SUBLANES, D_MODEL), lambda t: (0, 0))],
        out_shape=[jax.ShapeDtypeStruct((S, D_MODEL), F32), jax.ShapeDtypeStruct((D_MODEL, D_MODEL), F32),
                   jax.ShapeDtypeStruct((N_CHIPS, PLE_DIM, PLE_DIM), F32),
                   jax.ShapeDtypeStruct((SUBLANES, D_MODEL), F32)],
        compiler_params=_cparams(("arbitrary",)),
    )(dout, h, gamma.array, p_l, wgate, wproj)


def _final(h, gamma, target):
    S = h.shape[0]
    tm = _tile(S, 1024)

    def body(h_ref, g_ref, tgt_ref, dh_ref, dgam_ref, loss_ref):
        xhat, rstd, y = _rms(h_ref[...], g_ref[...])
        err = y - tgt_ref[...]
        dy = err * (1.0 / D_MODEL)
        dhn, dgam = _rms_bwd(xhat, rstd, g_ref[...], dy)
        dh_ref[...] = dhn
        sq = _rowsum8(err * err)
        part = sq[:, :LANES]
        for j in range(1, D_MODEL // LANES):
            part = part + sq[:, j * LANES:(j + 1) * LANES]

        @pl.when(pl.program_id(0) == 0)
        def _():
            dgam_ref[...] = jnp.zeros_like(dgam_ref)
            loss_ref[...] = jnp.zeros_like(loss_ref)

        dgam_ref[...] += dgam
        loss_ref[...] += part * (0.5 / D_MODEL)

    tile = pl.BlockSpec((tm, D_MODEL), lambda t: (t, 0))
    return pl.pallas_call(
        body, name="final_loss",
        grid=(S // tm,),
        in_specs=[tile, gamma.spec(), tile],
        out_specs=[tile, pl.BlockSpec((SUBLANES, D_MODEL), lambda t: (0, 0)),
                   pl.BlockSpec((SUBLANES, LANES), lambda t: (0, 0))],
        out_shape=[jax.ShapeDtypeStruct((S, D_MODEL), F32), jax.ShapeDtypeStruct((SUBLANES, D_MODEL), F32),
                   jax.ShapeDtypeStruct((SUBLANES, LANES), F32)],
        compiler_params=_cparams(("arbitrary",)),
    )(h, gamma.array, target)


def _mesh_pos():
    return lax.axis_index("x"), lax.axis_index("y"), lax.axis_index("c")


def _other_chip(x, y, j):
    fx, fy = CHIP_FLIPS[j]
    return (1 - x if fx else x), (1 - y if fy else y)


def _any_specs(n):
    return [pl.BlockSpec(memory_space=pl.ANY)] * n


class _Cargo:
    def __init__(self):
        self.operands, self.out_shapes, self.aliases, self.sem_shapes, self.names = [], [], {}, [], []
        self.start = lambda ins, outs, sems: None
        self.forward = lambda ins, outs, sems: None
        self.finish = lambda ins, outs, sems: None


def _cargo_refs(refs, cargo, n_in, n_out, n_scratch=0):
    a = n_in
    b = a + len(cargo.operands)
    c = b + n_out
    d = c + len(cargo.out_shapes)
    e = d + n_scratch
    return refs[:a], refs[a:b], refs[b:c], refs[c:d], refs[d:e], refs[e:]


def _remote(src, dst, send, recv, device):
    return pltpu.make_async_remote_copy(src_ref=src, dst_ref=dst, send_sem=send, recv_sem=recv,
                                        device_id=device, device_id_type=MESH_ID)


def _gather_cargo(bufs, pieces):
    cargo = _Cargo()
    if not pieces:
        return cargo
    plist = []
    for name, layer in pieces:
        if name not in cargo.names:
            cargo.names.append(name)
            cargo.operands.append(bufs[name])
        plist.append((cargo.names.index(name), layer, bufs[name].shape[2] // 2))
    nflip = len(CHIP_FLIPS)
    cargo.out_shapes = [jax.ShapeDtypeStruct(b.shape, b.dtype) for b in cargo.operands]
    cargo.aliases = {i: i for i in range(len(cargo.operands))}
    cargo.sem_shapes = [pltpu.SemaphoreType.DMA((len(plist) * nflip,))] * 4

    def copies(outs, sems):
        send1, recv1, send2, recv2 = sems
        x, y, c = _mesh_pos()
        k = 2 * x + y

        def blk(p, chip, cc):
            b, layer, hrows = plist[p]
            return outs[b].at[layer, chip, pl.ds(cc * hrows, hrows), :]

        def chip_of(j):
            px, py = _other_chip(x, y, j)
            return 2 * px + py

        def ici(p, j):
            px, py = _other_chip(x, y, j)
            return _remote(blk(p, k, c), blk(p, k, c), send1.at[p * nflip + j], recv1.at[p * nflip + j], (px, py, c))

        def landed(p, j):
            px, py = _other_chip(x, y, j)
            return _remote(blk(p, k, c), blk(p, chip_of(j), c), send1.at[p * nflip + j], recv1.at[p * nflip + j],
                           (px, py, c))

        def d2d(p, j, cc):
            return _remote(blk(p, chip_of(j), cc), blk(p, chip_of(j), cc), send2.at[p * nflip + j],
                           recv2.at[p * nflip + j], (x, y, 1 - c))

        return c, ici, landed, d2d

    def start(ins, outs, sems):
        _, ici, _, _ = copies(outs, sems)
        for p in range(len(plist)):
            for j in range(nflip):
                ici(p, j).start()

    def forward(ins, outs, sems):
        c, _, landed, d2d = copies(outs, sems)
        for j in range(nflip):
            for p in range(len(plist)):
                landed(p, j).wait_recv()
                d2d(p, j, c).start()

    def finish(ins, outs, sems):
        c, ici, _, d2d = copies(outs, sems)
        for p in range(len(plist)):
            for j in range(nflip):
                ici(p, j).wait_send()
                d2d(p, j, c).wait_send()
                d2d(p, j, 1 - c).wait_recv()

    cargo.start, cargo.forward, cargo.finish = start, forward, finish
    return cargo


def _reduce_cargo(grads, presums):
    cargo = _Cargo()
    na, nb = len(grads), len(presums)
    nflip = len(CHIP_FLIPS)
    cargo.operands = list(grads) + list(presums)
    cargo.out_shapes = ([jax.ShapeDtypeStruct((g.shape[0], g.shape[1] // 2, g.shape[2]), g.dtype) for g in grads]
                        + [jax.ShapeDtypeStruct((nflip,) + ps.shape[1:], ps.dtype) for ps in presums])
    cargo.sem_shapes = ([pltpu.SemaphoreType.DMA((na,))] * 2 if na else []) + (
        [pltpu.SemaphoreType.DMA((nb * nflip,))] * 2 if nb else [])

    def copies(ins, outs, sems):
        x, y, c = _mesh_pos()
        out = []
        if na:
            send, recv = sems[0], sems[1]
            for a in range(na):
                hrows = grads[a].shape[1] // 2
                out.append(_remote(ins[a].at[:, pl.ds((1 - c) * hrows, hrows), :], outs[a], send.at[a], recv.at[a],
                                   (x, y, 1 - c)))
        if nb:
            send, recv = sems[-2], sems[-1]
            for b in range(nb):
                for j in range(nflip):
                    px, py = _other_chip(x, y, j)
                    out.append(_remote(ins[na + b].at[2 * px + py], outs[na + b].at[j], send.at[b * nflip + j],
                                       recv.at[b * nflip + j], (px, py, c)))
        return out

    def start(ins, outs, sems):
        for cp in copies(ins, outs, sems):
            cp.start()

    def finish(ins, outs, sems):
        for cp in copies(ins, outs, sems):
            cp.wait()

    cargo.start, cargo.finish = start, finish
    return cargo


def _run_cargo(name, cargo):
    nin, nout = len(cargo.operands), len(cargo.out_shapes)

    def body(*refs):
        ins, outs, sems = refs[:nin], refs[nin:nin + nout], refs[nin + nout:]
        cargo.start(ins, outs, sems)
        cargo.forward(ins, outs, sems)
        cargo.finish(ins, outs, sems)

    return list(pl.pallas_call(
        body, name=name,
        in_specs=_any_specs(nin), out_specs=_any_specs(nout), out_shape=cargo.out_shapes,
        input_output_aliases=dict(cargo.aliases), scratch_shapes=list(cargo.sem_shapes),
    )(*cargo.operands))


def _join_siblings(bufs):
    nb = len(bufs)
    items = [(b, layer) for b, buf in enumerate(bufs) for layer in range(buf.shape[0])]

    def body(*refs):
        outs = refs[nb:2 * nb]
        send, recv = refs[2 * nb:]
        x, y, c = _mesh_pos()

        def half(i, cc):
            b, layer = items[i]
            hrows = bufs[b].shape[1] // 2
            blk = outs[b].at[layer, pl.ds(cc * hrows, hrows), :]
            return _remote(blk, blk, send.at[i], recv.at[i], (x, y, 1 - c))

        for i in range(len(items)):
            half(i, c).start()
        for i in range(len(items)):
            half(i, c).wait_send()
            half(i, 1 - c).wait_recv()

    return list(pl.pallas_call(
        body, name="grad_sibling_join",
        in_specs=_any_specs(nb), out_specs=_any_specs(nb),
        out_shape=[jax.ShapeDtypeStruct(b.shape, b.dtype) for b in bufs],
        input_output_aliases={i: i for i in range(nb)},
        scratch_shapes=[pltpu.SemaphoreType.DMA((len(items),))] * 2,
    )(*bufs))


def _cast_place(w3, pos, dtype):
    L, rows, cols = w3.shape

    def body(pos_ref, w_ref, o_ref):
        o_ref[...] = w_ref[...].astype(dtype)

    return pl.pallas_call(
        body, name="cast_place",
        grid_spec=pltpu.PrefetchScalarGridSpec(
            num_scalar_prefetch=1, grid=(L,),
            in_specs=[pl.BlockSpec((None, rows, cols), lambda l, pos: (l, 0, 0))],
            out_specs=pl.BlockSpec((None, None, rows, cols), lambda l, pos: (l, pos[0], 0, 0))),
        out_shape=jax.ShapeDtypeStruct((L, N_CHIPS, rows, cols), dtype),
        compiler_params=_cparams(("arbitrary",)),
    )(pos, w3)


def _allreduce_small(buf):
    R = buf.shape[0]
    half = R // 2
    assert half % SUBLANES == 0, R

    def body(in_ref, out_ref, land, send, recv):
        x, y, c = _mesh_pos()
        out_ref[...] = in_ref[...]
        cp = _remote(out_ref, land.at[0], send.at[0], recv.at[0], (x, y, 1 - c))
        cp.start()
        cp.wait()
        out_ref[...] = out_ref[...] + land[0]
        along_y, along_x = (x, 1 - y, c), (1 - x, y, c)
        lo, hi = pl.ds(0, half), pl.ds(half, half)
        for stage, (peer_lo, peer_hi) in enumerate(((along_y, along_x), (along_x, along_y))):
            slot = 1 + stage
            cps = [_remote(out_ref.at[lo], land.at[slot, lo], send.at[1 + 2 * stage], recv.at[1 + 2 * stage], peer_lo),
                   _remote(out_ref.at[hi], land.at[slot, hi], send.at[2 + 2 * stage], recv.at[2 + 2 * stage], peer_hi)]
            for cp in cps:
                cp.start()
            for cp in cps:
                cp.wait()
            out_ref[...] = out_ref[...] + land[slot]

    return pl.pallas_call(
        body, name="allreduce_small",
        in_specs=[pl.BlockSpec(memory_space=pltpu.VMEM)],
        out_specs=pl.BlockSpec(memory_space=pltpu.VMEM),
        out_shape=jax.ShapeDtypeStruct((R, LANES), F32),
        scratch_shapes=[pltpu.VMEM((3, R, LANES), F32), pltpu.SemaphoreType.DMA((5,)), pltpu.SemaphoreType.DMA((5,))],
        compiler_params=pltpu.CompilerParams(vmem_limit_bytes=VMEM_LIMIT_MB * 2 ** 20),
    )(buf)


def _presum_with_sibling(grad, landed, pos):
    nchunk, rows, cols = grad.shape
    hrows = rows // 2

    def body(pos_ref, g_ref, l_ref, all_ref, own_ref):
        s = g_ref[...] + l_ref[...]
        all_ref[...] = s.astype(BF16)

        @pl.when(pl.program_id(0) == pos_ref[0])
        def _():
            own_ref[...] = s

    return pl.pallas_call(
        body, name="grad_presum",
        grid_spec=pltpu.PrefetchScalarGridSpec(
            num_scalar_prefetch=1, grid=(nchunk,),
            in_specs=[pl.BlockSpec((None, hrows, cols), lambda k, pos: (k, pos[1], 0)),
                      pl.BlockSpec((None, hrows, cols), lambda k, pos: (k, 0, 0))],
            out_specs=[pl.BlockSpec((None, hrows, cols), lambda k, pos: (k, 0, 0)),
                       pl.BlockSpec((hrows, cols), lambda k, pos: (0, 0))]),
        out_shape=[jax.ShapeDtypeStruct((nchunk, hrows, cols), BF16), jax.ShapeDtypeStruct((hrows, cols), F32)],
        compiler_params=_cparams(("arbitrary",)),
    )(pos, grad, landed)


def _sum_chips(own, landed, stacked, layer, shape3, pos):
    hrows, cols = own.shape

    def body(pos_ref, o_ref, l_ref, *rest):
        s = o_ref[...]
        for j in range(len(CHIP_FLIPS)):
            s = s + l_ref[j].astype(F32)
        rest[-1][...] = s

    in_specs = [pl.BlockSpec((hrows, cols), lambda i, pos: (0, 0)),
                pl.BlockSpec((len(CHIP_FLIPS), hrows, cols), lambda i, pos: (0, 0, 0))]
    args = [pos, own, landed]
    aliases = {}
    if stacked is not None:
        in_specs.append(pl.BlockSpec(memory_space=pl.ANY))
        args.append(stacked)
        aliases = {3: 0}
    return pl.pallas_call(
        body, name="grad_sum_chips",
        grid_spec=pltpu.PrefetchScalarGridSpec(
            num_scalar_prefetch=1, grid=(1,), in_specs=in_specs,
            out_specs=pl.BlockSpec((None, hrows, cols), lambda i, pos: (layer, pos[1], 0))),
        out_shape=jax.ShapeDtypeStruct(shape3, F32),
        input_output_aliases=aliases,
        compiler_params=_cparams(("arbitrary",)),
    )(*args)


def _adamw(w, g, m, v):
    R, C = w.shape
    rb = R
    for cand in (512, 352, 320, 256, 128, 64, 32, 16, 8):
        if R % cand == 0:
            rb = cand
            break
    c1 = 1.0 - ADAM_B1 ** ADAM_STEP
    c2 = 1.0 - ADAM_B2 ** ADAM_STEP

    def body(w_ref, g_ref, m_ref, v_ref, go_ref, d_ref, mo_ref, vo_ref):
        gv = g_ref[...]
        go_ref[...] = gv
        m2 = ADAM_B1 * m_ref[...] + (1.0 - ADAM_B1) * gv
        v2 = ADAM_B2 * v_ref[...] + (1.0 - ADAM_B2) * (gv * gv)
        mo_ref[...] = m2
        vo_ref[...] = v2
        d_ref[...] = -ADAM_LR * ((m2 / c1) / (jnp.sqrt(v2 / c2) + ADAM_EPS) + ADAM_WD * w_ref[...])

    spec = pl.BlockSpec((rb, C), lambda i: (i, 0))
    return pl.pallas_call(
        body, name="adamw",
        grid=(R // rb,),
        in_specs=[spec] * 4, out_specs=[spec] * 4,
        out_shape=[jax.ShapeDtypeStruct((R, C), F32)] * 4,
        compiler_params=_cparams(("arbitrary",)),
    )(w, g, m, v)


def _pack(parts, align=SUBLANES * LANES):
    flat = jnp.concatenate([p.reshape(-1).astype(F32) for p in parts])
    pad = (-flat.shape[0]) % align
    return jnp.pad(flat, (0, pad)).reshape(-1, LANES)


def _unpack(buf, shapes):
    flat = buf.reshape(-1)
    out, off = [], 0
    for shp in shapes:
        size = 1
        for d in shp:
            size *= d
        out.append(flat[off:off + size].reshape(shp))
        off += size
    return out


def _block_diag_gates(w_a, w_x):
    nq = D_RNN // GATE_CHUNK
    hpc = LRU_HEADS // nq
    eye = jnp.eye(hpc, dtype=F32)

    def bd(w):
        wq = w.reshape(nq, hpc, LRU_HEAD_DIM, LRU_HEAD_DIM)
        return (wq[:, :, :, None, :] * eye[None, :, None, :, None]).reshape(nq, GATE_CHUNK, GATE_CHUNK)

    return jnp.concatenate([bd(w_a), bd(w_x)], axis=2).astype(BF16)


def _block_diag_extract(dwbd):
    nq = D_RNN // GATE_CHUNK
    hpc = LRU_HEADS // nq
    eye = jnp.eye(hpc, dtype=F32)

    def ex(d):
        d5 = d.reshape(nq, hpc, LRU_HEAD_DIM, hpc, LRU_HEAD_DIM)
        return jnp.sum(d5 * eye[None, :, None, :, None], axis=3).reshape(LRU_HEADS, LRU_HEAD_DIM, LRU_HEAD_DIM)

    return ex(dwbd[:, :, :GATE_CHUNK]), ex(dwbd[:, :, GATE_CHUNK:])


BIG = ("ffn1_w_gate", "ffn1_w_up", "ffn1_w_down", "lru_w_in", "lru_w_out", "pool_w",
       "ffn2_w_gate", "ffn2_w_up", "ffn2_w_down", "ple_w_gate", "ple_w_proj")
TINY_SHARDED = ("lru_conv_w", "pool_b", "pool_scale")
REPLICATED = ("ffn1_norm", "mix_norm", "lru_conv_b", "lru_w_a", "lru_b_a", "lru_w_x", "lru_b_x", "lru_a_param",
              "ffn2_norm", "ple_norm", "final_norm")
WEIGHT_ORDER = ("ffn1_norm", "ffn1_w_gate", "ffn1_w_up", "ffn1_w_down", "mix_norm", "lru_w_in", "lru_conv_w",
                "lru_conv_b", "lru_w_a", "lru_b_a", "lru_w_x", "lru_b_x", "lru_a_param", "lru_w_out", "pool_w",
                "pool_b", "pool_scale", "ffn2_norm", "ffn2_w_gate", "ffn2_w_up", "ffn2_w_down", "ple_norm",
                "ple_w_gate", "ple_w_proj", "final_norm")


TRANSPOSED = ("ffn1_w_gate", "ffn1_w_up", "ffn2_w_gate", "ffn2_w_up")


def _stored(name, a):
    return jnp.swapaxes(a, 1, 2) if name in TRANSPOSED else a


def _as3(a):
    return a.reshape(a.shape[0], -1, a.shape[-1])


def kernel(x, p, ffn1_norm, ffn1_w_gate, ffn1_w_up, ffn1_w_down, mix_norm, lru_w_in, lru_conv_w, lru_conv_b, lru_w_a, lru_b_a, lru_w_x, lru_b_x, lru_a_param, lru_w_out, pool_w, pool_b, pool_scale, ffn2_norm, ffn2_w_gate, ffn2_w_up, ffn2_w_down, ple_norm, ple_w_gate, ple_w_proj, final_norm, loss_target, m_ffn1_norm, m_ffn1_w_gate, m_ffn1_w_up, m_ffn1_w_down, m_mix_norm, m_lru_w_in, m_lru_conv_w, m_lru_conv_b, m_lru_w_a, m_lru_b_a, m_lru_w_x, m_lru_b_x, m_lru_a_param, m_lru_w_out, m_pool_w, m_pool_b, m_pool_scale, m_ffn2_norm, m_ffn2_w_gate, m_ffn2_w_up, m_ffn2_w_down, m_ple_norm, m_ple_w_gate, m_ple_w_proj, m_final_norm, v_ffn1_norm, v_ffn1_w_gate, v_ffn1_w_up, v_ffn1_w_down, v_mix_norm, v_lru_w_in, v_lru_conv_w, v_lru_conv_b, v_lru_w_a, v_lru_b_a, v_lru_w_x, v_lru_b_x, v_lru_a_param, v_lru_w_out, v_pool_w, v_pool_b, v_pool_scale, v_ffn2_norm, v_ffn2_w_gate, v_ffn2_w_up, v_ffn2_w_down, v_ple_norm, v_ple_w_gate, v_ple_w_proj, v_final_norm):
    W = dict(ffn1_norm=ffn1_norm, ffn1_w_gate=ffn1_w_gate, ffn1_w_up=ffn1_w_up, ffn1_w_down=ffn1_w_down,
             mix_norm=mix_norm, lru_w_in=lru_w_in, lru_conv_w=lru_conv_w, lru_conv_b=lru_conv_b, lru_w_a=lru_w_a,
             lru_b_a=lru_b_a, lru_w_x=lru_w_x, lru_b_x=lru_b_x, lru_a_param=lru_a_param, lru_w_out=lru_w_out,
             pool_w=pool_w, pool_b=pool_b, pool_scale=pool_scale, ffn2_norm=ffn2_norm, ffn2_w_gate=ffn2_w_gate,
             ffn2_w_up=ffn2_w_up, ffn2_w_down=ffn2_w_down, ple_norm=ple_norm, ple_w_gate=ple_w_gate,
             ple_w_proj=ple_w_proj, final_norm=final_norm)
    M = dict(ffn1_norm=m_ffn1_norm, ffn1_w_gate=m_ffn1_w_gate, ffn1_w_up=m_ffn1_w_up, ffn1_w_down=m_ffn1_w_down,
             mix_norm=m_mix_norm, lru_w_in=m_lru_w_in, lru_conv_w=m_lru_conv_w, lru_conv_b=m_lru_conv_b,
             lru_w_a=m_lru_w_a, lru_b_a=m_lru_b_a, lru_w_x=m_lru_w_x, lru_b_x=m_lru_b_x, lru_a_param=m_lru_a_param,
             lru_w_out=m_lru_w_out, pool_w=m_pool_w, pool_b=m_pool_b, pool_scale=m_pool_scale, ffn2_norm=m_ffn2_norm,
             ffn2_w_gate=m_ffn2_w_gate, ffn2_w_up=m_ffn2_w_up, ffn2_w_down=m_ffn2_w_down, ple_norm=m_ple_norm,
             ple_w_gate=m_ple_w_gate, ple_w_proj=m_ple_w_proj, final_norm=m_final_norm)
    V = dict(ffn1_norm=v_ffn1_norm, ffn1_w_gate=v_ffn1_w_gate, ffn1_w_up=v_ffn1_w_up, ffn1_w_down=v_ffn1_w_down,
             mix_norm=v_mix_norm, lru_w_in=v_lru_w_in, lru_conv_w=v_lru_conv_w, lru_conv_b=v_lru_conv_b,
             lru_w_a=v_lru_w_a, lru_b_a=v_lru_b_a, lru_w_x=v_lru_w_x, lru_b_x=v_lru_b_x, lru_a_param=v_lru_a_param,
             lru_w_out=v_lru_w_out, pool_w=v_pool_w, pool_b=v_pool_b, pool_scale=v_pool_scale, ffn2_norm=v_ffn2_norm,
             ffn2_w_gate=v_ffn2_w_gate, ffn2_w_up=v_ffn2_w_up, ffn2_w_down=v_ffn2_w_down, ple_norm=v_ple_norm,
             ple_w_gate=v_ple_w_gate, ple_w_proj=v_ple_w_proj, final_norm=v_final_norm)

    S = x.shape[1]
    my_x, my_y, my_c = _mesh_pos()
    my_chip = 2 * my_x + my_y
    pos = jnp.stack([my_chip, my_c]).astype(jnp.int32)
    n_lru, n_pool = lru_w_in.shape[0], pool_w.shape[0]

    tiny_shapes = [W[n].shape for n in TINY_SHARDED]
    tiny_local = _pack([W[n] for n in TINY_SHARDED], align=2 * 16 * LANES)[None]
    Ws, Ms, Vs = ({n: _stored(n, d[n]) for n in BIG} for d in (W, M, V))
    bufs = {n: _cast_place(_as3(Ws[n]), pos, BF16) for n in BIG}
    bufs["tiny"] = _cast_place(tiny_local, pos, F32)

    def gather_now(name, pieces):
        cargo = _gather_cargo(bufs, pieces)
        bufs.update(zip(cargo.names, _run_cargo(name, cargo)))

    def ffn_pieces(which, layer):
        return [("%s_w_gate" % which, layer), ("%s_w_up" % which, layer), ("%s_w_down" % which, layer)]

    def mixer_pieces(layer):
        if layer % 2 == 0:
            return [("lru_w_in", layer // 2), ("lru_w_out", layer // 2)]
        return [("pool_w", layer // 2)]

    gather_now("gather_first", [("tiny", 0)] + ffn_pieces("ffn1", 0))
    tiny_by_chip = [_unpack(bufs["tiny"][0, k], tiny_shapes) for k in range(N_CHIPS)]
    conv_w_full = jnp.concatenate([tiny_by_chip[k][0] for k in range(N_CHIPS)], axis=-1)
    pool_b_full = jnp.concatenate([tiny_by_chip[k][1] for k in range(N_CHIPS)], axis=-1)
    pool_s_full = jnp.concatenate([tiny_by_chip[k][2] for k in range(N_CHIPS)], axis=-1)
    ngroup = len(POOL_WINDOWS)

    def pool_weights():
        pw5 = bufs["pool_w"].reshape(n_pool, N_CHIPS, ngroup, POOL_GROUP_DIM // N_CHIPS, POOL_GROUP_DIM)
        return pw5.transpose(0, 2, 1, 3, 4).reshape(n_pool, ngroup, POOL_GROUP_DIM, POOL_GROUP_DIM)

    lru_out = lambda: bufs["lru_w_out"].reshape(n_lru, D_RNN, D_MODEL)
    ple_gate = lambda: bufs["ple_w_gate"].reshape(DEPTH, D_MODEL, D_MODEL)
    wbd = [_block_diag_gates(lru_w_a[j], lru_w_x[j]) for j in range(n_lru)]
    ffn1_norm, mix_norm, ffn2_norm, ple_norm = (_rows3(a) for a in (ffn1_norm, mix_norm, ffn2_norm, ple_norm))
    lru_conv_b, lru_b_a, lru_b_x, lru_a_param = (_rows3(a) for a in (lru_conv_b, lru_b_a, lru_b_x, lru_a_param))
    pool_b_full, pool_s_full = _rows3(pool_b_full), _rows3(pool_s_full)

    def ffn_forward(which, h, gamma, layer, pieces):
        cargo = _gather_cargo(bufs, pieces)
        outs, updated = _ffn_fwd(h, gamma, bufs[which + "_w_gate"], bufs[which + "_w_up"], bufs[which + "_w_down"],
                                 layer, cargo)
        bufs.update(zip(cargo.names, updated))
        return outs

    h = x.reshape(S, D_MODEL)
    saved = []
    for i in range(DEPTH):
        j = i // 2
        sv = {"h0": h}
        first_mixer = mixer_pieces(0) if i == 0 else []
        h, sv["xn1"], sv["g1"], sv["u1"] = ffn_forward(
            "ffn1", h, _Rows(ffn1_norm, i), i,
            first_mixer + ffn_pieces("ffn2", i) + [("ple_w_gate", i), ("ple_w_proj", i)])
        sv["h1"] = h
        if i % 2 == 0:
            h, sv["xn_mix"], sv["z"], sv["hs"], sv["gates"] = _lru_fwd(
                h, _Rows(mix_norm, i), bufs["lru_w_in"], j, _Rows(conv_w_full, j), _Rows(lru_conv_b, j), wbd[j],
                _Rows(lru_b_a, j), _Rows(lru_b_x, j), _Rows(lru_a_param, j), lru_out())
        else:
            h, sv["u"] = _pool_fwd(h, _Rows(mix_norm, i), pool_weights(), _Rows(pool_b_full, j), _Rows(pool_s_full, j), j)
        sv["h2"] = h
        nxt = ffn_pieces("ffn1", i + 1) + mixer_pieces(i + 1) if i + 1 < DEPTH else []
        h, sv["xn2"], sv["g2"], sv["u2"] = ffn_forward("ffn2", h, _Rows(ffn2_norm, i), i, nxt)
        sv["h3"] = h
        h = _ple_fwd(h, _Rows(ple_norm, i), p, ple_gate(), bufs["ple_w_proj"], i)
        saved.append(sv)

    dh, dgam_final, loss_part = _final(h, _Rows(final_norm.reshape(1, 1, -1), 0), loss_target.reshape(S, D_MODEL))
    win, wout, wpg, wpp, pw = bufs["lru_w_in"], lru_out(), ple_gate(), bufs["ple_w_proj"], pool_weights()

    norm_grads = {n: [None] * DEPTH for n in ("ffn1_norm", "mix_norm", "ffn2_norm", "ple_norm")}
    lru_vec = [None] * n_lru
    pool_vec = [None] * n_pool
    sum8 = lambda a: jnp.sum(a, axis=-2)

    to_siblings, to_chips = [], []
    stacked = {n: None for n in BIG}

    def take_cargo(with_chips=True):
        a_items, b_items = list(to_siblings), list(to_chips) if with_chips else []
        del to_siblings[:], to_chips[:len(b_items)]
        return _reduce_cargo([it[2] for it in a_items], [it[2] for it in b_items]), a_items, b_items

    def absorb(a_items, b_items, outs):
        for (n, layer, g), landed in zip(a_items, outs[:len(a_items)]):
            all_chunks, own = _presum_with_sibling(g, landed, pos)
            to_chips.append((n, layer, all_chunks, own))
        for (n, layer, _, own), from_chips in zip(b_items, outs[len(a_items):]):
            stacked[n] = _sum_chips(own, from_chips, stacked[n], layer, _as3(Ws[n]).shape, pos)

    def ffn_backward(which, xn, dout, gg, uu, layer, h_in, gamma):
        cargo, a_items, b_items = take_cargo()
        weights = (bufs[which + "_w_gate"], bufs[which + "_w_up"], bufs[which + "_w_down"])
        (dwg, dwu, dwd, slabs), c_outs = _ffn_bwd(xn, dout, gg, uu, *weights, layer, cargo)
        absorb(a_items, b_items, c_outs)
        dh_in, dgam, dwg, dwu, dwd = _ffn_bwd_last(xn, dout, gg, uu, *weights, layer, slabs, h_in, gamma,
                                                   dwg, dwu, dwd)
        to_siblings.extend([(which + "_w_gate", layer, dwg), (which + "_w_up", layer, dwu),
                            (which + "_w_down", layer, dwd)])
        return dh_in, sum8(dgam)

    for i in reversed(range(DEPTH)):
        j = i // 2
        sv = saved[i]
        dh, dw_pg, dw_pp, dgam = _ple_bwd(dh, sv["h3"], _Rows(ple_norm, i), p, wpg, wpp, i)
        norm_grads["ple_norm"][i] = sum8(dgam)
        to_siblings.append(("ple_w_gate", i, dw_pg.reshape(N_CHIPS, D_MODEL // N_CHIPS, D_MODEL)))
        to_siblings.append(("ple_w_proj", i, dw_pp))

        dh, norm_grads["ffn2_norm"][i] = ffn_backward("ffn2", sv["xn2"], dh, sv["g2"], sv["u2"], i, sv["h2"],
                                                      _Rows(ffn2_norm, i))

        if i % 2 == 0:
            cargo, a_items, b_items = take_cargo()
            (dz, dpre, xc_b, y_b, dcw, vec), c_outs = _lru_bwd_seq(
                dh, sv["z"], sv["hs"], sv["gates"], _Rows(conv_w_full, j), wbd[j], _Rows(lru_a_param, j), wout, j, cargo)
            absorb(a_items, b_items, c_outs)
            to_siblings.append(("lru_w_out", j, _xt_dy("lru_dw_out", y_b, dh, 1, D_RNN, D_MODEL, False, False)
                                .reshape(N_CHIPS, D_RNN // N_CHIPS, D_MODEL)))
            to_siblings.append(("lru_w_in", j, _xt_dy("lru_dw_in", sv["xn_mix"], dz, N_CHIPS, D_MODEL, RNN_IN_CHUNK,
                                                      False, True)))
            dwbd = _xt_dy("lru_dw_gates", xc_b, dpre, D_RNN // GATE_CHUNK, GATE_CHUNK, 2 * GATE_CHUNK, True, True)
            dw_a, dw_x = _block_diag_extract(dwbd)
            vsum = sum8(vec)
            lru_vec[j] = (sum8(dcw), vsum[0], vsum[1], vsum[2], vsum[3], dw_a, dw_x)
            cargo, a_items, b_items = take_cargo(with_chips=False)
            (dh, dgam), c_outs = _lru_bwd_in(dz, sv["h1"], _Rows(mix_norm, i), dh, win, j, cargo)
            absorb(a_items, b_items, c_outs)
            norm_grads["mix_norm"][i] = sum8(dgam)
        else:
            dh_new, dpre_b, vec = _pool_bwd(dh, sv["h1"], sv["u"], _Rows(mix_norm, i), pw, _Rows(pool_b_full, j),
                                            _Rows(pool_s_full, j), j)
            dpw = _xt_dy("pool_dw", sv["u"], dpre_b, ngroup, POOL_GROUP_DIM, POOL_GROUP_DIM, True, True)
            dpw = dpw.reshape(ngroup, N_CHIPS, POOL_GROUP_DIM // N_CHIPS, POOL_GROUP_DIM).transpose(1, 0, 2, 3)
            to_siblings.append(("pool_w", j, dpw.reshape(N_CHIPS, POOL_GROUP_DIM, POOL_GROUP_DIM)))
            vsum = sum8(vec)
            norm_grads["mix_norm"][i] = vsum[0]
            pool_vec[j] = (vsum[1], vsum[2])
            dh = dh_new

        dh, norm_grads["ffn1_norm"][i] = ffn_backward("ffn1", sv["xn1"], dh, sv["g1"], sv["u1"], i, sv["h0"],
                                                      _Rows(ffn1_norm, i))

    grad_x = dh.reshape(1, S, D_MODEL)

    small_parts = [
        jnp.stack(norm_grads["ffn1_norm"]), jnp.stack(norm_grads["mix_norm"]),
        jnp.stack(norm_grads["ffn2_norm"]), jnp.stack(norm_grads["ple_norm"]), sum8(dgam_final),
        jnp.stack([lv[0] for lv in lru_vec]), jnp.stack([lv[1] for lv in lru_vec]),
        jnp.stack([lv[2] for lv in lru_vec]), jnp.stack([lv[3] for lv in lru_vec]),
        jnp.stack([lv[4] for lv in lru_vec]), jnp.stack([lv[5] for lv in lru_vec]),
        jnp.stack([lv[6] for lv in lru_vec]),
        jnp.stack([pv[0] for pv in pool_vec]), jnp.stack([pv[1] for pv in pool_vec]),
        jnp.sum(loss_part).reshape(1),
    ]
    small_names = ("ffn1_norm", "mix_norm", "ffn2_norm", "ple_norm", "final_norm", "lru_conv_w", "lru_conv_b",
                   "lru_b_a", "lru_b_x", "lru_a_param", "lru_w_a", "lru_w_x", "pool_b", "pool_scale", "loss")
    reduced = _unpack(_allreduce_small(_pack(small_parts, align=2 * SUBLANES * LANES)),
                      [sp.shape for sp in small_parts])
    small_grad = dict(zip(small_names, reduced))
    loss = small_grad.pop("loss").reshape(())
    for n in TINY_SHARDED:
        width = W[n].shape[-1]
        small_grad[n] = lax.dynamic_slice_in_dim(small_grad[n], my_chip * width, width, axis=-1)

    tail = 0
    while to_siblings or to_chips:
        cargo, a_items, b_items = take_cargo()
        absorb(a_items, b_items, _run_cargo("grad_exchange_tail%d" % tail, cargo))
        tail += 1
    big_final = dict(zip(BIG, _join_siblings([stacked[n] for n in BIG])))

    grads, deltas, new_m, new_v = {}, {}, {}, {}
    for n in BIG:
        shp = Ws[n].shape
        to2 = lambda a: a.reshape(-1, shp[-1])
        g2, d, m2, v2 = _adamw(to2(Ws[n]), to2(big_final[n]), to2(Ms[n]), to2(Vs[n]))
        grads[n], deltas[n], new_m[n], new_v[n] = (_stored(n, a.reshape(shp)) for a in (g2, d, m2, v2))
    small_order = TINY_SHARDED + REPLICATED
    small_shapes = [W[n].shape for n in small_order]
    pack_rows = functools.partial(_pack, align=512 * LANES)
    _, sd, sm, sv_ = _adamw(pack_rows([W[n] for n in small_order]), pack_rows([small_grad[n] for n in small_order]),
                            pack_rows([M[n] for n in small_order]), pack_rows([V[n] for n in small_order]))
    for n, d, m2, v2 in zip(small_order, _unpack(sd, small_shapes), _unpack(sm, small_shapes),
                            _unpack(sv_, small_shapes)):
        grads[n], deltas[n], new_m[n], new_v[n] = small_grad[n].reshape(W[n].shape), d, m2, v2

    return (loss, grad_x, *[grads[n] for n in WEIGHT_ORDER], *[deltas[n] for n in WEIGHT_ORDER],
            *[new_m[n] for n in WEIGHT_ORDER], *[new_v[n] for n in WEIGHT_ORDER])
```

```python
import functools

import jax
import jax.numpy as jnp
from jax import lax
from jax.experimental import pallas as pl
from jax.experimental.pallas import tpu as pltpu

F32 = jnp.float32
BF16 = jnp.bfloat16

D_MODEL = 1024
D_FF = 2816
D_RNN = 1280
DEPTH = 4
N_CHIPS = 4
FF_CHUNK = D_FF // N_CHIPS
RNN_IN_CHUNK = 2 * D_RNN // N_CHIPS
GATE_CHUNK = 640
N_GATE_PLANES = 5
LRU_HEADS = 16
LRU_HEAD_DIM = 80
CONV_WIDTH = 4
LRU_C = 8.0
POOL_WINDOWS = (2, 4, 8, 16)
POOL_GROUP_DIM = 256
PLE_DIM = 256
RMS_EPS = 1e-6
POOL_HALO = 16
SUBLANES = 8
LANES = 128

ADAM_LR = 0.001
ADAM_B1 = 0.9
ADAM_B2 = 0.999
ADAM_EPS = 1e-08
ADAM_WD = 0.01
ADAM_STEP = 10

VMEM_LIMIT_MB = 56
MESH_ID = pl.DeviceIdType.MESH
CHIP_FLIPS = ((1, 0), (0, 1), (1, 1))


def _cparams(semantics):
    return pltpu.CompilerParams(dimension_semantics=semantics, vmem_limit_bytes=VMEM_LIMIT_MB * 2 ** 20)


def _dot(a, b):
    return lax.dot_general(a, b, (((1,), (0,)), ((), ())), preferred_element_type=F32)


def _dot_nt(a, b):
    return lax.dot_general(a, b, (((1,), (1,)), ((), ())), preferred_element_type=F32)


def _dot_tn(a, b):
    return lax.dot_general(a, b, (((0,), (0,)), ((), ())), preferred_element_type=F32)


def _sigmoid(x):
    return 1.0 / (1.0 + jnp.exp(-x))


def _rms(hf, gamma):
    rstd = lax.rsqrt(jnp.mean(hf * hf, axis=-1, keepdims=True) + RMS_EPS)
    xhat = hf * rstd
    return xhat, rstd, xhat * gamma


def _rms_bwd(xhat, rstd, gamma, dxn):
    dxhat = dxn * gamma
    m = jnp.mean(dxhat * xhat, axis=-1, keepdims=True)
    return rstd * (dxhat - xhat * m), _rowsum8(dxn * xhat)


def _rowsum8(v):
    tm, n = v.shape
    return jnp.sum(v.reshape(tm // SUBLANES, SUBLANES, n), axis=0)


def _gelu(x):
    u = 0.7978845608028654 * (x + 0.044715 * x * x * x)
    return 0.5 * x * (1.0 + jnp.tanh(u))


def _gelu_and_grad(x):
    c = 0.7978845608028654
    u = c * (x + 0.044715 * x * x * x)
    th = jnp.tanh(u)
    g = 0.5 * x * (1.0 + th)
    dg = 0.5 * (1.0 + th) + 0.5 * x * (1.0 - th * th) * c * (1.0 + 3.0 * 0.044715 * x * x)
    return g, dg


def _softplus(z):
    e = jnp.exp(-jnp.abs(z))
    u = 1.0 + e
    log1p = jnp.where(u == 1.0, e, jnp.log(u) * e / jnp.where(u == 1.0, 1.0, u - 1.0))
    return jnp.maximum(z, 0.0) + log1p


def _neg_expm1(x):
    series = -x * (1.0 + x * (0.5 + x * (1.0 / 6.0)))
    return jnp.where(x > -1e-2, series, 1.0 - jnp.exp(x))


def _shift_down(ext, j, halo):
    return pltpu.roll(ext, j, 0)[halo:]


def _shift_up(ext, j, tm):
    n = ext.shape[0]
    return pltpu.roll(ext, n - j, 0)[:tm]


def _scan_causal(a, b):
    tm, n = a.shape
    head_rows = lax.broadcasted_iota(jnp.int32, (SUBLANES, n), 0)
    s = 1
    while s < min(SUBLANES, tm):
        keep = head_rows >= s
        a_r, b_r = pltpu.roll(a, s, 0), pltpu.roll(b, s, 0)
        a_sh = jnp.concatenate([jnp.where(keep, a_r[:SUBLANES], 1.0), a_r[SUBLANES:]], axis=0)
        b_sh = jnp.concatenate([jnp.where(keep, b_r[:SUBLANES], 0.0), b_r[SUBLANES:]], axis=0)
        b = a * b_sh + b
        a = a * a_sh
        s *= 2
    while s < tm:
        b = jnp.concatenate([b[:s], a[s:] * b[:tm - s] + b[s:]], axis=0)
        a = jnp.concatenate([a[:s], a[s:] * a[:tm - s]], axis=0)
        s *= 2
    return a, b


def _scan_anticausal(c, d):
    tm, n = c.shape
    body = tm - SUBLANES
    tail_rows = lax.broadcasted_iota(jnp.int32, (SUBLANES, n), 0) + body
    s = 1
    while s < min(SUBLANES, tm):
        keep = tail_rows < tm - s
        c_r, d_r = pltpu.roll(c, tm - s, 0), pltpu.roll(d, tm - s, 0)
        c_sh = jnp.concatenate([c_r[:body], jnp.where(keep, c_r[body:], 1.0)], axis=0)
        d_sh = jnp.concatenate([d_r[:body], jnp.where(keep, d_r[body:], 0.0)], axis=0)
        d = d + c * d_sh
        c = c * c_sh
        s *= 2
    while s < tm:
        d = jnp.concatenate([d[:tm - s] + c[:tm - s] * d[s:], d[tm - s:]], axis=0)
        c = jnp.concatenate([c[:tm - s] * c[s:], c[tm - s:]], axis=0)
        s *= 2
    return c, d


def _rows3(stacked):
    return stacked.reshape(stacked.shape[0], 1, stacked.shape[-1])


class _Rows:
    def __init__(self, stacked3, index):
        assert stacked3.ndim == 3, stacked3.shape
        self.array = stacked3
        self.index = index

    def spec(self):
        index = self.index
        return pl.BlockSpec((None,) + self.array.shape[1:], lambda *_: (index, 0, 0))


def _tile(n, want):
    t = min(n, want)
    assert n % t == 0, (n, t)
    return t


def _ffn_fwd(h, gamma, wg, wu, wd, layer, cargo=None):
    S = h.shape[0]
    tm = _tile(S, 1024)
    nt = S // tm
    cargo = cargo or _Cargo()
    n_in, n_out = 5, 4
    nc_in, nc_out = len(cargo.operands), len(cargo.out_shapes)

    def body(*refs):
        h_ref, g_ref, wg_ref, wu_ref, wd_ref = refs[:n_in]
        c_ins = refs[n_in:n_in + nc_in]
        ho_ref, xn_ref, gg_ref, uu_ref = refs[n_in + nc_in:n_in + nc_in + n_out]
        c_outs = refs[n_in + nc_in + n_out:n_in + nc_in + n_out + nc_out]
        xn_s, acc_s = refs[n_in + nc_in + n_out + nc_out:n_in + nc_in + n_out + nc_out + 2]
        sems = refs[n_in + nc_in + n_out + nc_out + 2:]
        t, k = pl.program_id(0), pl.program_id(1)

        @pl.when((t == 0) & (k == 0))
        def _():
            cargo.start(c_ins, c_outs, sems)

        @pl.when((t == nt - 1) & (k == 0))
        def _():
            cargo.forward(c_ins, c_outs, sems)

        @pl.when(k == 0)
        def _():
            _, _, xn = _rms(h_ref[...], g_ref[...])
            xnb = xn.astype(BF16)
            xn_s[...] = xnb
            xn_ref[...] = xnb
            acc_s[...] = jnp.zeros_like(acc_s)

        xnb = xn_s[...]
        g = _dot_nt(xnb, wg_ref[...])
        u = _dot_nt(xnb, wu_ref[...])
        gg_ref[...] = g.astype(BF16)
        uu_ref[...] = u.astype(BF16)
        hid = (g * _sigmoid(g)) * u
        acc_s[...] += _dot(hid.astype(BF16), wd_ref[...])

        @pl.when(k == N_CHIPS - 1)
        def _():
            ho_ref[...] = h_ref[...] + 0.5 * acc_s[...]

        @pl.when((t == nt - 1) & (k == N_CHIPS - 1))
        def _():
            cargo.finish(c_ins, c_outs, sems)

    outs = pl.pallas_call(
        body, name="ffn_fwd",
        grid=(nt, N_CHIPS),
        in_specs=[
            pl.BlockSpec((tm, D_MODEL), lambda t, k: (t, 0)),
            gamma.spec(),
            pl.BlockSpec((None, None, FF_CHUNK, D_MODEL), lambda t, k: (layer, k, 0, 0)),
            pl.BlockSpec((None, None, FF_CHUNK, D_MODEL), lambda t, k: (layer, k, 0, 0)),
            pl.BlockSpec((None, None, FF_CHUNK, D_MODEL), lambda t, k: (layer, k, 0, 0)),
        ] + _any_specs(nc_in),
        out_specs=[
            pl.BlockSpec((tm, D_MODEL), lambda t, k: (t, 0)),
            pl.BlockSpec((tm, D_MODEL), lambda t, k: (t, 0)),
            pl.BlockSpec((None, tm, FF_CHUNK), lambda t, k: (k, t, 0)),
            pl.BlockSpec((None, tm, FF_CHUNK), lambda t, k: (k, t, 0)),
        ] + _any_specs(nc_out),
        out_shape=[
            jax.ShapeDtypeStruct((S, D_MODEL), F32),
            jax.ShapeDtypeStruct((S, D_MODEL), BF16),
            jax.ShapeDtypeStruct((N_CHIPS, S, FF_CHUNK), BF16),
            jax.ShapeDtypeStruct((N_CHIPS, S, FF_CHUNK), BF16),
        ] + cargo.out_shapes,
        input_output_aliases={n_in + i: n_out + o for i, o in cargo.aliases.items()},
        scratch_shapes=[pltpu.VMEM((tm, D_MODEL), BF16), pltpu.VMEM((tm, D_MODEL), F32)] + cargo.sem_shapes,
        compiler_params=_cparams(("arbitrary", "arbitrary")),
    )(h, gamma.array, wg, wu, wd, *cargo.operands)
    return outs[:n_out], list(outs[n_out:])


def _ffn_bwd(xn, dout, gg, uu, wg, wu, wd, layer, cargo=None):
    S = xn.shape[0]
    tm = _tile(S, 512)
    nt = S // tm
    nchunk = N_CHIPS - 1
    cargo = cargo or _Cargo()
    n_in, n_out = 7, 4
    nc_in, nc_out = len(cargo.operands), len(cargo.out_shapes)

    def body(*refs):
        xn_ref, do_ref, gg_ref, uu_ref, wg_ref, wu_ref, wd_ref = refs[:n_in]
        c_ins = refs[n_in:n_in + nc_in]
        dwg_ref, dwu_ref, dwd_ref, slab_ref = refs[n_in + nc_in:n_in + nc_in + n_out]
        c_outs = refs[n_in + nc_in + n_out:n_in + nc_in + n_out + nc_out]
        sems = refs[n_in + nc_in + n_out + nc_out:]
        k, t = pl.program_id(0), pl.program_id(1)

        @pl.when((k == 0) & (t == 0))
        def _():
            cargo.start(c_ins, c_outs, sems)

        @pl.when(t == 0)
        def _():
            dwg_ref[...] = jnp.zeros_like(dwg_ref)
            dwu_ref[...] = jnp.zeros_like(dwu_ref)
            dwd_ref[...] = jnp.zeros_like(dwd_ref)

        xnb = xn_ref[...]
        dob = (0.5 * do_ref[...]).astype(BF16)
        g = gg_ref[...].astype(F32)
        u = uu_ref[...].astype(F32)
        s = _sigmoid(g)
        sil = g * s
        dhid = _dot_nt(dob, wd_ref[...])
        dwd_ref[...] += _dot_tn((sil * u).astype(BF16), dob)
        du = (dhid * sil).astype(BF16)
        dg = (dhid * u * (s * (1.0 + g * (1.0 - s)))).astype(BF16)
        dwg_ref[...] += _dot_tn(dg, xnb)
        dwu_ref[...] += _dot_tn(du, xnb)
        slab_ref[...] = (_dot(dg, wg_ref[...]) + _dot(du, wu_ref[...])).astype(BF16)

        @pl.when((k == nchunk - 1) & (t == nt - 1))
        def _():
            cargo.finish(c_ins, c_outs, sems)

    outs = pl.pallas_call(
        body, name="ffn_bwd",
        grid=(nchunk, nt),
        in_specs=[
            pl.BlockSpec((tm, D_MODEL), lambda k, t: (t, 0)),
            pl.BlockSpec((tm, D_MODEL), lambda k, t: (t, 0)),
            pl.BlockSpec((None, tm, FF_CHUNK), lambda k, t: (k, t, 0)),
            pl.BlockSpec((None, tm, FF_CHUNK), lambda k, t: (k, t, 0)),
            pl.BlockSpec((None, None, FF_CHUNK, D_MODEL), lambda k, t: (layer, k, 0, 0)),
            pl.BlockSpec((None, None, FF_CHUNK, D_MODEL), lambda k, t: (layer, k, 0, 0)),
            pl.BlockSpec((None, None, FF_CHUNK, D_MODEL), lambda k, t: (layer, k, 0, 0)),
        ] + _any_specs(nc_in),
        out_specs=[
            pl.BlockSpec((None, FF_CHUNK, D_MODEL), lambda k, t: (k, 0, 0)),
            pl.BlockSpec((None, FF_CHUNK, D_MODEL), lambda k, t: (k, 0, 0)),
            pl.BlockSpec((None, FF_CHUNK, D_MODEL), lambda k, t: (k, 0, 0)),
            pl.BlockSpec((None, tm, D_MODEL), lambda k, t: (k, t, 0)),
        ] + _any_specs(nc_out),
        out_shape=[
            jax.ShapeDtypeStruct((N_CHIPS, FF_CHUNK, D_MODEL), F32),
            jax.ShapeDtypeStruct((N_CHIPS, FF_CHUNK, D_MODEL), F32),
            jax.ShapeDtypeStruct((N_CHIPS, FF_CHUNK, D_MODEL), F32),
            jax.ShapeDtypeStruct((nchunk, S, D_MODEL), BF16),
        ] + cargo.out_shapes,
        input_output_aliases={n_in + i: n_out + o for i, o in cargo.aliases.items()},
        scratch_shapes=list(cargo.sem_shapes),
        compiler_params=_cparams(("arbitrary", "arbitrary")),
    )(xn, dout, gg, uu, wg, wu, wd, *cargo.operands)
    return outs[:n_out], list(outs[n_out:])


def _ffn_bwd_last(xn, dout, gg, uu, wg, wu, wd, layer, slabs, h, gamma, dwg, dwu, dwd):
    S = xn.shape[0]
    tm = _tile(S, 512)
    k = N_CHIPS - 1
    nprev = slabs.shape[0]

    def body(xn_ref, do_ref, gg_ref, uu_ref, wg_ref, wu_ref, wd_ref, slab_ref, h_ref, g_ref, _dwg, _dwu, _dwd,
             dh_ref, dgam_ref, dwg_ref, dwu_ref, dwd_ref):
        @pl.when(pl.program_id(0) == 0)
        def _():
            dgam_ref[...] = jnp.zeros_like(dgam_ref)
            dwg_ref[...] = jnp.zeros_like(dwg_ref)
            dwu_ref[...] = jnp.zeros_like(dwu_ref)
            dwd_ref[...] = jnp.zeros_like(dwd_ref)

        xnb = xn_ref[...]
        do = do_ref[...]
        dob = (0.5 * do).astype(BF16)
        g = gg_ref[...].astype(F32)
        u = uu_ref[...].astype(F32)
        s = _sigmoid(g)
        sil = g * s
        dhid = _dot_nt(dob, wd_ref[...])
        dwd_ref[...] += _dot_tn((sil * u).astype(BF16), dob)
        du = (dhid * sil).astype(BF16)
        dg = (dhid * u * (s * (1.0 + g * (1.0 - s)))).astype(BF16)
        dwg_ref[...] += _dot_tn(dg, xnb)
        dwu_ref[...] += _dot_tn(du, xnb)
        dxn = _dot(dg, wg_ref[...]) + _dot(du, wu_ref[...])
        for i in range(nprev):
            dxn = dxn + slab_ref[i].astype(F32)
        xhat, rstd, _ = _rms(h_ref[...], g_ref[...])
        dhn, dgam = _rms_bwd(xhat, rstd, g_ref[...], dxn)
        dh_ref[...] = do + dhn
        dgam_ref[...] += dgam

    tile = pl.BlockSpec((tm, D_MODEL), lambda t: (t, 0))
    hidden = pl.BlockSpec((None, tm, FF_CHUNK), lambda t: (k, t, 0))
    w_in = pl.BlockSpec((None, None, FF_CHUNK, D_MODEL), lambda t: (layer, k, 0, 0))
    dw_in = pl.BlockSpec((None, FF_CHUNK, D_MODEL), lambda t: (k, 0, 0))
    return pl.pallas_call(
        body, name="ffn_bwd_last",
        grid=(S // tm,),
        in_specs=[tile, tile, hidden, hidden, w_in, w_in,
                  pl.BlockSpec((None, None, FF_CHUNK, D_MODEL), lambda t: (layer, k, 0, 0)),
                  pl.BlockSpec((nprev, tm, D_MODEL), lambda t: (0, t, 0)), tile,
                  gamma.spec()] + _any_specs(3),
        out_specs=[tile, pl.BlockSpec((SUBLANES, D_MODEL), lambda t: (0, 0)), dw_in, dw_in,
                   pl.BlockSpec((None, FF_CHUNK, D_MODEL), lambda t: (k, 0, 0))],
        out_shape=[jax.ShapeDtypeStruct((S, D_MODEL), F32), jax.ShapeDtypeStruct((SUBLANES, D_MODEL), F32),
                   jax.ShapeDtypeStruct(dwg.shape, F32), jax.ShapeDtypeStruct(dwu.shape, F32),
                   jax.ShapeDtypeStruct(dwd.shape, F32)],
        input_output_aliases={10: 2, 11: 3, 12: 4},
        compiler_params=_cparams(("arbitrary",)),
    )(xn, dout, gg, uu, wg, wu, wd, slabs, h, gamma.array, dwg, dwu, dwd)


def _xt_dy(name, x, dy, nchunk, kb, nb, x_by_chunk, y_by_chunk):
    S = x.shape[0]
    tm = _tile(S, 2048)

    def body(x_ref, dy_ref, o_ref):
        @pl.when(pl.program_id(1) == 0)
        def _():
            o_ref[...] = jnp.zeros_like(o_ref)

        o_ref[...] += _dot_tn(x_ref[...].astype(BF16), dy_ref[...].astype(BF16))

    return pl.pallas_call(
        body, name=name,
        grid=(nchunk, S // tm),
        in_specs=[
            pl.BlockSpec((tm, kb), (lambda c, t: (t, c)) if x_by_chunk else (lambda c, t: (t, 0))),
            pl.BlockSpec((tm, nb), (lambda c, t: (t, c)) if y_by_chunk else (lambda c, t: (t, 0))),
        ],
        out_specs=pl.BlockSpec((None, kb, nb), lambda c, t: (c, 0, 0)),
        out_shape=jax.ShapeDtypeStruct((nchunk, kb, nb), F32),
        compiler_params=_cparams(("arbitrary", "arbitrary")),
    )(x, dy)


def _lru_gates(xc, wbd_ref, ba, bx, apar):
    xcb = xc.astype(BF16)
    r_parts, ig_parts = [], []
    for q in range(D_RNN // GATE_CHUNK):
        lo, hi = q * GATE_CHUNK, (q + 1) * GATE_CHUNK
        pre = _dot(xcb[:, lo:hi], wbd_ref[q])
        r_parts.append(_sigmoid(pre[:, :GATE_CHUNK] + ba[:, lo:hi]))
        ig_parts.append(_sigmoid(pre[:, GATE_CHUNK:] + bx[:, lo:hi]))
    r = jnp.concatenate(r_parts, axis=1)
    ig = jnp.concatenate(ig_parts, axis=1)
    sp = LRU_C * _softplus(-apar)
    log_a = -(r * sp)
    a = jnp.exp(log_a)
    mult = jnp.sqrt(_neg_expm1(2.0 * log_a))
    return r, ig, a, mult, sp


def _conv_causal(xb, tail, cw_ref, cb):
    ext = jnp.concatenate([tail, xb], axis=0)
    xc = cb + cw_ref[CONV_WIDTH - 1:CONV_WIDTH, :] * xb
    for j in range(1, CONV_WIDTH):
        xc = xc + cw_ref[CONV_WIDTH - 1 - j:CONV_WIDTH - j, :] * _shift_down(ext, j, SUBLANES)
    return xc, ext


def _lru_fwd(h, gamma, win, layer, convw, convb, wbd, ba, bx, apar, wout):
    S = h.shape[0]
    tm = _tile(S, 256)

    def body(h_ref, g_ref, win_ref, cw_ref, cb_ref, wbd_ref, ba_ref, bx_ref, ap_ref, wout_ref,
             ho_ref, xn_ref, z_ref, hs_ref, gates_ref, tail_s, carry_s):
        @pl.when(pl.program_id(0) == 0)
        def _():
            tail_s[...] = jnp.zeros_like(tail_s)
            carry_s[...] = jnp.zeros_like(carry_s)

        hf = h_ref[...]
        _, _, xn = _rms(hf, g_ref[...])
        xnb = xn.astype(BF16)
        xn_ref[...] = xnb
        for k in range(N_CHIPS):
            z_ref[:, k * RNN_IN_CHUNK:(k + 1) * RNN_IN_CHUNK] = _dot(xnb, win_ref[k])
        gate = z_ref[:, :D_RNN]
        xb = z_ref[:, D_RNN:]
        xc, _ = _conv_causal(xb, tail_s[...], cw_ref, cb_ref[...])
        tail_s[...] = xb[tm - SUBLANES:, :]
        r, ig, a, mult, _ = _lru_gates(xc, wbd_ref, ba_ref[...], bx_ref[...], ap_ref[...])
        for plane, val in enumerate((xc, r, ig, a, mult)):
            gates_ref[plane] = val
        big_a, big_b = _scan_causal(a, mult * (ig * xc))
        hs = big_a * carry_s[SUBLANES - 1:SUBLANES, :] + big_b
        hs_ref[...] = hs
        carry_s[...] = hs[tm - SUBLANES:, :]
        y = hs * _gelu(gate)
        ho_ref[...] = hf + _dot(y.astype(BF16), wout_ref[...])

    return pl.pallas_call(
        body, name="lru_fwd",
        grid=(S // tm,),
        in_specs=[
            pl.BlockSpec((tm, D_MODEL), lambda t: (t, 0)),
            gamma.spec(),
            pl.BlockSpec((None, N_CHIPS, D_MODEL, RNN_IN_CHUNK), lambda t: (layer, 0, 0, 0)),
            convw.spec(),
            convb.spec(),
            pl.BlockSpec((D_RNN // GATE_CHUNK, GATE_CHUNK, 2 * GATE_CHUNK), lambda t: (0, 0, 0)),
            ba.spec(), bx.spec(), apar.spec(),
            pl.BlockSpec((None, D_RNN, D_MODEL), lambda t: (layer, 0, 0)),
        ],
        out_specs=[
            pl.BlockSpec((tm, D_MODEL), lambda t: (t, 0)),
            pl.BlockSpec((tm, D_MODEL), lambda t: (t, 0)),
            pl.BlockSpec((tm, 2 * D_RNN), lambda t: (t, 0)),
            pl.BlockSpec((tm, D_RNN), lambda t: (t, 0)),
            pl.BlockSpec((N_GATE_PLANES, tm, D_RNN), lambda t: (0, t, 0)),
        ],
        out_shape=[
            jax.ShapeDtypeStruct((S, D_MODEL), F32),
            jax.ShapeDtypeStruct((S, D_MODEL), BF16),
            jax.ShapeDtypeStruct((S, 2 * D_RNN), F32),
            jax.ShapeDtypeStruct((S, D_RNN), F32),
            jax.ShapeDtypeStruct((N_GATE_PLANES, S, D_RNN), F32),
        ],
        scratch_shapes=[pltpu.VMEM((SUBLANES, D_RNN), F32), pltpu.VMEM((SUBLANES, D_RNN), F32)],
        compiler_params=_cparams(("arbitrary",)),
    )(h, gamma.array, win, convw.array, convb.array, wbd, ba.array, bx.array, apar.array, wout)


def _lru_bwd_seq(dout, z, hs, gates, convw, wbd, apar, wout, layer, cargo=None):
    S = dout.shape[0]
    tm = _tile(S, 256)
    nt = S // tm
    per8 = tm // SUBLANES
    rev = lambda i: nt - 1 - i
    prev8 = lambda i: jnp.maximum(rev(i) * per8 - 1, 0)
    cargo = cargo or _Cargo()
    n_in, n_out = 10, 6

    def body(*refs):
        ins, c_ins, outs, c_outs, scratch, sems = _cargo_refs(refs, cargo, n_in, n_out, n_scratch=3)
        do_ref, z_ref, hs_ref, gates_ref, ztail_ref, hstail_ref, cw_ref, wbd_ref, ap_ref, wout_ref = ins
        dz_ref, dpre_ref, xc_ref, y_ref, dcw_ref, vec_ref = outs
        a_first_s, g_first_s, dxc_head_s = scratch
        i = pl.program_id(0)
        first_in_time = rev(i) == 0

        @pl.when(i == 0)
        def _():
            cargo.start(c_ins, c_outs, sems)
            a_first_s[...] = jnp.zeros_like(a_first_s)
            g_first_s[...] = jnp.zeros_like(g_first_s)
            dxc_head_s[...] = jnp.zeros_like(dxc_head_s)
            dcw_ref[...] = jnp.zeros_like(dcw_ref)
            vec_ref[...] = jnp.zeros_like(vec_ref)

        gate = z_ref[:, :D_RNN]
        xb = z_ref[:, D_RNN:]
        hist = jnp.where(first_in_time, 0.0, 1.0)
        xext = jnp.concatenate([ztail_ref[:, D_RNN:] * hist, xb], axis=0)
        xc, r, ig, a, mult = (gates_ref[plane] for plane in range(N_GATE_PLANES))
        sp = LRU_C * _softplus(-ap_ref[...])
        hs = hs_ref[...]
        gel, dgel = _gelu_and_grad(gate)
        y = hs * gel
        y_ref[...] = y.astype(BF16)
        xc_ref[...] = xc.astype(BF16)

        dy = _dot_nt(do_ref[...].astype(BF16), wout_ref[...])
        dhs = dy * gel
        dgate = dy * hs * dgel

        coef = _shift_up(jnp.concatenate([a, a_first_s[...]], axis=0), 1, tm)
        big_c, big_d = _scan_anticausal(coef, dhs)
        g = big_d + big_c * g_first_s[0:1, :]
        g_first_s[...] = g[:SUBLANES, :]
        a_first_s[...] = a[:SUBLANES, :]

        hs_prev = _shift_down(jnp.concatenate([hstail_ref[...] * hist, hs], axis=0), 1, SUBLANES)
        da = g * hs_prev
        dmult = g * ig * xc
        dig = g * mult * xc
        dxc = g * mult * ig
        dlog_a = da * a - dmult * (a * a) / mult
        dr = -(dlog_a * sp)
        dpre_a = dr * r * (1.0 - r)
        dpre_x = dig * ig * (1.0 - ig)
        d_apar = dlog_a * r * (LRU_C * _sigmoid(-ap_ref[...]))

        for q in range(D_RNN // GATE_CHUNK):
            lo, hi = q * GATE_CHUNK, (q + 1) * GATE_CHUNK
            dpre_q = jnp.concatenate([dpre_a[:, lo:hi], dpre_x[:, lo:hi]], axis=1).astype(BF16)
            dpre_ref[:, 2 * lo:2 * hi] = dpre_q
            dxc_q = _dot_nt(dpre_q, wbd_ref[q])
            if q == 0:
                dxc_parts = [dxc_q]
            else:
                dxc_parts.append(dxc_q)
        dxc = dxc + jnp.concatenate(dxc_parts, axis=1)

        dext = jnp.concatenate([dxc, dxc_head_s[...]], axis=0)
        dxb = cw_ref[CONV_WIDTH - 1:CONV_WIDTH, :] * dxc
        for j in range(1, CONV_WIDTH):
            dxb = dxb + cw_ref[CONV_WIDTH - 1 - j:CONV_WIDTH - j, :] * _shift_up(dext, j, tm)
        dxc_head_s[...] = dxc[:SUBLANES, :]
        dz_ref[:, :D_RNN] = dgate.astype(BF16)
        dz_ref[:, D_RNN:] = dxb.astype(BF16)

        dcw_ref[CONV_WIDTH - 1] += _rowsum8(dxc * xb)
        for j in range(1, CONV_WIDTH):
            dcw_ref[CONV_WIDTH - 1 - j] += _rowsum8(dxc * _shift_down(xext, j, SUBLANES))
        vec_ref[0] += _rowsum8(dxc)
        vec_ref[1] += _rowsum8(dpre_a)
        vec_ref[2] += _rowsum8(dpre_x)
        vec_ref[3] += _rowsum8(d_apar)

        @pl.when(i == nt - 1)
        def _():
            cargo.finish(c_ins, c_outs, sems)

    outs = pl.pallas_call(
        body, name="lru_bwd_seq",
        grid=(nt,),
        in_specs=[
            pl.BlockSpec((tm, D_MODEL), lambda i: (rev(i), 0)),
            pl.BlockSpec((tm, 2 * D_RNN), lambda i: (rev(i), 0)),
            pl.BlockSpec((tm, D_RNN), lambda i: (rev(i), 0)),
            pl.BlockSpec((N_GATE_PLANES, tm, D_RNN), lambda i: (0, rev(i), 0)),
            pl.BlockSpec((SUBLANES, 2 * D_RNN), lambda i: (prev8(i), 0)),
            pl.BlockSpec((SUBLANES, D_RNN), lambda i: (prev8(i), 0)),
            convw.spec(),
            pl.BlockSpec((D_RNN // GATE_CHUNK, GATE_CHUNK, 2 * GATE_CHUNK), lambda i: (0, 0, 0)),
            apar.spec(),
            pl.BlockSpec((None, D_RNN, D_MODEL), lambda i: (layer, 0, 0)),
        ] + _any_specs(len(cargo.operands)),
        out_specs=[
            pl.BlockSpec((tm, 2 * D_RNN), lambda i: (rev(i), 0)),
            pl.BlockSpec((tm, 2 * D_RNN), lambda i: (rev(i), 0)),
            pl.BlockSpec((tm, D_RNN), lambda i: (rev(i), 0)),
            pl.BlockSpec((tm, D_RNN), lambda i: (rev(i), 0)),
            pl.BlockSpec((CONV_WIDTH, SUBLANES, D_RNN), lambda i: (0, 0, 0)),
            pl.BlockSpec((4, SUBLANES, D_RNN), lambda i: (0, 0, 0)),
        ] + _any_specs(len(cargo.out_shapes)),
        out_shape=[
            jax.ShapeDtypeStruct((S, 2 * D_RNN), BF16),
            jax.ShapeDtypeStruct((S, 2 * D_RNN), BF16),
            jax.ShapeDtypeStruct((S, D_RNN), BF16),
            jax.ShapeDtypeStruct((S, D_RNN), BF16),
            jax.ShapeDtypeStruct((CONV_WIDTH, SUBLANES, D_RNN), F32),
            jax.ShapeDtypeStruct((4, SUBLANES, D_RNN), F32),
        ] + cargo.out_shapes,
        input_output_aliases={n_in + i: n_out + o for i, o in cargo.aliases.items()},
        scratch_shapes=[pltpu.VMEM((SUBLANES, D_RNN), F32)] * 3 + cargo.sem_shapes,
        compiler_params=_cparams(("arbitrary",)),
    )(dout, z, hs, gates, z, hs, convw.array, wbd, apar.array, wout, *cargo.operands)
    return outs[:n_out], list(outs[n_out:])


def _lru_bwd_in(dz, h, gamma, dres, win, layer, cargo=None):
    S = h.shape[0]
    tm = _tile(S, 512)
    nt = S // tm
    cargo = cargo or _Cargo()
    n_in, n_out = 5, 2

    def body(*refs):
        (dz_ref, h_ref, g_ref, dres_ref, win_ref), c_ins, (dh_ref, dgam_ref), c_outs, _, sems = _cargo_refs(
            refs, cargo, n_in, n_out)

        @pl.when(pl.program_id(0) == 0)
        def _():
            cargo.start(c_ins, c_outs, sems)

        dxn = _dot_nt(dz_ref[:, :RNN_IN_CHUNK], win_ref[0])
        for k in range(1, N_CHIPS):
            dxn = dxn + _dot_nt(dz_ref[:, k * RNN_IN_CHUNK:(k + 1) * RNN_IN_CHUNK], win_ref[k])
        xhat, rstd, _ = _rms(h_ref[...], g_ref[...])
        dhn, dgam = _rms_bwd(xhat, rstd, g_ref[...], dxn)
        dh_ref[...] = dres_ref[...] + dhn

        @pl.when(pl.program_id(0) == 0)
        def _():
            dgam_ref[...] = jnp.zeros_like(dgam_ref)

        dgam_ref[...] += dgam

        @pl.when(pl.program_id(0) == nt - 1)
        def _():
            cargo.finish(c_ins, c_outs, sems)

    outs = pl.pallas_call(
        body, name="lru_bwd_in",
        grid=(nt,),
        in_specs=[
            pl.BlockSpec((tm, 2 * D_RNN), lambda t: (t, 0)),
            pl.BlockSpec((tm, D_MODEL), lambda t: (t, 0)),
            gamma.spec(),
            pl.BlockSpec((tm, D_MODEL), lambda t: (t, 0)),
            pl.BlockSpec((None, N_CHIPS, D_MODEL, RNN_IN_CHUNK), lambda t: (layer, 0, 0, 0)),
        ] + _any_specs(len(cargo.operands)),
        out_specs=[
            pl.BlockSpec((tm, D_MODEL), lambda t: (t, 0)),
            pl.BlockSpec((SUBLANES, D_MODEL), lambda t: (0, 0)),
        ] + _any_specs(len(cargo.out_shapes)),
        out_shape=[jax.ShapeDtypeStruct((S, D_MODEL), F32),
                   jax.ShapeDtypeStruct((SUBLANES, D_MODEL), F32)] + cargo.out_shapes,
        input_output_aliases={n_in + i: n_out + o for i, o in cargo.aliases.items()},
        scratch_shapes=list(cargo.sem_shapes),
        compiler_params=_cparams(("arbitrary",)),
    )(dz, h, gamma.array, dres, win, *cargo.operands)
    return outs[:n_out], list(outs[n_out:])


def _pool_inv_count(t_index, tm):
    rows = (lax.broadcasted_iota(jnp.int32, (tm, D_MODEL), 0) + t_index * tm + 1).astype(F32)
    col = lax.broadcasted_iota(jnp.int32, (tm, D_MODEL), 1)
    win = jnp.where(col < POOL_GROUP_DIM, float(POOL_WINDOWS[0]),
                    jnp.where(col < 2 * POOL_GROUP_DIM, float(POOL_WINDOWS[1]),
                              jnp.where(col < 3 * POOL_GROUP_DIM, float(POOL_WINDOWS[2]), float(POOL_WINDOWS[3]))))
    return 1.0 / jnp.minimum(rows, win)


def _window_sums(ext, shift, take):
    gd = POOL_GROUP_DIM
    s2 = ext + shift(ext, 1)
    s4 = s2[:, gd:] + shift(s2[:, gd:], 2)
    s8 = s4[:, gd:] + shift(s4[:, gd:], 4)
    s16 = s8[:, gd:] + shift(s8[:, gd:], 8)
    return jnp.concatenate([take(s2[:, :gd]), take(s4[:, :gd]), take(s8[:, :gd]), take(s16)], axis=1)


def _pool_fwd(h, gamma, pw, pb, pscale, layer):
    S = h.shape[0]
    tm = _tile(S, 512)

    def body(h_ref, g_ref, pw_ref, pb_ref, ps_ref, ho_ref, u_ref, tail_s):
        t = pl.program_id(0)

        @pl.when(t == 0)
        def _():
            tail_s[...] = jnp.zeros_like(tail_s)

        hf = h_ref[...]
        _, _, hn = _rms(hf, g_ref[...])
        ext = jnp.concatenate([tail_s[...], hn], axis=0)
        tail_s[...] = hn[tm - POOL_HALO:, :]
        sums = _window_sums(ext, lambda v, j: pltpu.roll(v, j, 0), lambda v: v[POOL_HALO:])
        ub = (sums * _pool_inv_count(t, tm) - hn).astype(BF16)
        u_ref[...] = ub
        ys = [_dot(ub[:, g * POOL_GROUP_DIM:(g + 1) * POOL_GROUP_DIM], pw_ref[g]) for g in range(len(POOL_WINDOWS))]
        y = jnp.concatenate(ys, axis=1)
        ho_ref[...] = hf + (y + pb_ref[...]) * ps_ref[...]

    return pl.pallas_call(
        body, name="pool_fwd",
        grid=(S // tm,),
        in_specs=[
            pl.BlockSpec((tm, D_MODEL), lambda t: (t, 0)), gamma.spec(),
            pl.BlockSpec((None, len(POOL_WINDOWS), POOL_GROUP_DIM, POOL_GROUP_DIM), lambda t: (layer, 0, 0, 0)),
            pb.spec(), pscale.spec(),
        ],
        out_specs=[pl.BlockSpec((tm, D_MODEL), lambda t: (t, 0)), pl.BlockSpec((tm, D_MODEL), lambda t: (t, 0))],
        out_shape=[jax.ShapeDtypeStruct((S, D_MODEL), F32), jax.ShapeDtypeStruct((S, D_MODEL), BF16)],
        scratch_shapes=[pltpu.VMEM((POOL_HALO, D_MODEL), F32)],
        compiler_params=_cparams(("arbitrary",)),
    )(h, gamma.array, pw, pb.array, pscale.array)


def _pool_bwd(dout, h, u, gamma, pw, pb, pscale, layer):
    S = h.shape[0]
    tm = _tile(S, 512)
    nt = S // tm
    rev = lambda i: nt - 1 - i
    ngroup = len(POOL_WINDOWS)

    def body(do_ref, h_ref, u_ref, g_ref, pw_ref, pb_ref, ps_ref, dh_ref, dpre_ref, vec_ref, head_s):
        i = pl.program_id(0)

        @pl.when(i == 0)
        def _():
            head_s[...] = jnp.zeros_like(head_s)
            vec_ref[...] = jnp.zeros_like(vec_ref)

        do = do_ref[...]
        ub = u_ref[...]
        gsl = lambda v, g: v[:, g * POOL_GROUP_DIM:(g + 1) * POOL_GROUP_DIM]
        y = jnp.concatenate([_dot(gsl(ub, g), pw_ref[g]) for g in range(ngroup)], axis=1)
        dpre = do * ps_ref[...]
        dpb = dpre.astype(BF16)
        dpre_ref[...] = dpb
        du = jnp.concatenate([_dot_nt(gsl(dpb, g), pw_ref[g]) for g in range(ngroup)], axis=1)
        v = du * _pool_inv_count(rev(i), tm)
        ext = jnp.concatenate([v, head_s[...]], axis=0)
        head_s[...] = v[:POOL_HALO, :]
        n = tm + POOL_HALO
        dhn = _window_sums(ext, lambda w, j: pltpu.roll(w, n - j, 0), lambda w: w[:tm]) - du
        xhat, rstd, _ = _rms(h_ref[...], g_ref[...])
        dh_in, dgam = _rms_bwd(xhat, rstd, g_ref[...], dhn)
        dh_ref[...] = do + dh_in
        vec_ref[0] += dgam
        vec_ref[1] += _rowsum8(dpre)
        vec_ref[2] += _rowsum8(do * (y + pb_ref[...]))

    tile = pl.BlockSpec((tm, D_MODEL), lambda i: (rev(i), 0))
    return pl.pallas_call(
        body, name="pool_bwd",
        grid=(nt,),
        in_specs=[tile, tile, tile, gamma.spec(),
                  pl.BlockSpec((None, ngroup, POOL_GROUP_DIM, POOL_GROUP_DIM), lambda i: (layer, 0, 0, 0)),
                  pb.spec(), pscale.spec()],
        out_specs=[tile, tile, pl.BlockSpec((3, SUBLANES, D_MODEL), lambda i: (0, 0, 0))],
        out_shape=[jax.ShapeDtypeStruct((S, D_MODEL), F32), jax.ShapeDtypeStruct((S, D_MODEL), BF16),
                   jax.ShapeDtypeStruct((3, SUBLANES, D_MODEL), F32)],
        scratch_shapes=[pltpu.VMEM((POOL_HALO, D_MODEL), F32)],
        compiler_params=_cparams(("arbitrary",)),
    )(dout, h, u, gamma.array, pw, pb.array, pscale.array)


def _ple_parts(hf, gamma, p_tile, wgate_ref, wproj_ref):
    xhat, rstd, xn = _rms(hf, gamma)
    xnb = xn.astype(BF16)
    gate = _sigmoid(_dot(xnb, wgate_ref[...]))
    pb = p_tile.astype(BF16)
    proj = jnp.concatenate([_dot(pb, wproj_ref[k]) for k in range(N_CHIPS)], axis=1)
    return xhat, rstd, xnb, pb, gate, proj


def _ple_fwd(h, gamma, p_l, wgate, wproj, layer):
    S = h.shape[0]
    tm = _tile(S, 1024)

    def body(h_ref, g_ref, p_ref, wgate_ref, wproj_ref, ho_ref):
        hf = h_ref[...]
        _, _, _, _, gate, proj = _ple_parts(hf, g_ref[...], p_ref[...], wgate_ref, wproj_ref)
        ho_ref[...] = hf + gate * proj

    return pl.pallas_call(
        body, name="ple_fwd",
        grid=(S // tm,),
        in_specs=[
            pl.BlockSpec((tm, D_MODEL), lambda t: (t, 0)),
            gamma.spec(),
            pl.BlockSpec((None, None, tm, PLE_DIM), lambda t: (layer, 0, t, 0)),
            pl.BlockSpec((None, D_MODEL, D_MODEL), lambda t: (layer, 0, 0)),
            pl.BlockSpec((None, N_CHIPS, PLE_DIM, PLE_DIM), lambda t: (layer, 0, 0, 0)),
        ],
        out_specs=pl.BlockSpec((tm, D_MODEL), lambda t: (t, 0)),
        out_shape=jax.ShapeDtypeStruct((S, D_MODEL), F32),
        compiler_params=_cparams(("arbitrary",)),
    )(h, gamma.array, p_l, wgate, wproj)


def _ple_bwd(dout, h, gamma, p_l, wgate, wproj, layer):
    S = h.shape[0]
    tm = _tile(S, 512)

    def body(do_ref, h_ref, g_ref, p_ref, wgate_ref, wproj_ref, dh_ref, dwg_ref, dwp_ref, dgam_ref):
        @pl.when(pl.program_id(0) == 0)
        def _():
            dgam_ref[...] = jnp.zeros_like(dgam_ref)
            dwg_ref[...] = jnp.zeros_like(dwg_ref)
            dwp_ref[...] = jnp.zeros_like(dwp_ref)

        do = do_ref[...]
        xhat, rstd, xnb, pb, gate, proj = _ple_parts(h_ref[...], g_ref[...], p_ref[...], wgate_ref, wproj_ref)
        dproj = (do * gate).astype(BF16)
        dpre = (do * proj * gate * (1.0 - gate)).astype(BF16)
        dwg_ref[...] += _dot_tn(xnb, dpre)
        for k in range(N_CHIPS):
            dwp_ref[k] += _dot_tn(pb, dproj[:, k * PLE_DIM:(k + 1) * PLE_DIM])
        dhn, dgam = _rms_bwd(xhat, rstd, g_ref[...], _dot_nt(dpre, wgate_ref[...]))
        dh_ref[...] = do + dhn
        dgam_ref[...] += dgam

    tile = pl.BlockSpec((tm, D_MODEL), lambda t: (t, 0))
    return pl.pallas_call(
        body, name="ple_bwd",
        grid=(S // tm,),
        in_specs=[
            tile, tile,
            gamma.spec(),
            pl.BlockSpec((None, None, tm, PLE_DIM), lambda t: (layer, 0, t, 0)),
            pl.BlockSpec((None, D_MODEL, D_MODEL), lambda t: (layer, 0, 0)),
            pl.BlockSpec((None, N_CHIPS, PLE_DIM, PLE_DIM), lambda t: (layer, 0, 0, 0)),
        ],
        out_specs=[tile, pl.BlockSpec((D_MODEL, D_MODEL), lambda t: (0, 0)),
                   pl.BlockSpec((N_CHIPS, PLE_DIM, PLE_DIM), lambda t: (0, 0, 0)),
                   pl.BlockSpec((SUBLANES, D_MODEL), lambda t: (0, 0))],
        out_shape=[jax.ShapeDtypeStruct((S, D_MODEL), F32), jax.ShapeDtypeStruct((D_MODEL, D_MODEL), F32),
                   jax.ShapeDtypeStruct((N_CHIPS, PLE_DIM, PLE_DIM), F32),
                   jax.ShapeDtypeStruct((SUBLANES, D_MODEL), F32)],
        compiler_params=_cparams(("arbitrary",)),
    )(dout, h, gamma.array, p_l, wgate, wproj)


def _final(h, gamma, target):
    S = h.shape[0]
    tm = _tile(S, 1024)

    def body(h_ref, g_ref, tgt_ref, dh_ref, dgam_ref, loss_ref):
        xhat, rstd, y = _rms(h_ref[...], g_ref[...])
        err = y - tgt_ref[...]
        dy = err * (1.0 / D_MODEL)
        dhn, dgam = _rms_bwd(xhat, rstd, g_ref[...], dy)
        dh_ref[...] = dhn
        sq = _rowsum8(err * err)
        part = sq[:, :LANES]
        for j in range(1, D_MODEL // LANES):
            part = part + sq[:, j * LANES:(j + 1) * LANES]

        @pl.when(pl.program_id(0) == 0)
        def _():
            dgam_ref[...] = jnp.zeros_like(dgam_ref)
            loss_ref[...] = jnp.zeros_like(loss_ref)

        dgam_ref[...] += dgam
        loss_ref[...] += part * (0.5 / D_MODEL)

    tile = pl.BlockSpec((tm, D_MODEL), lambda t: (t, 0))
    return pl.pallas_call(
        body, name="final_loss",
        grid=(S // tm,),
        in_specs=[tile, gamma.spec(), tile],
        out_specs=[tile, pl.BlockSpec((SUBLANES, D_MODEL), lambda t: (0, 0)),
                   pl.BlockSpec((SUBLANES, LANES), lambda t: (0, 0))],
        out_shape=[jax.ShapeDtypeStruct((S, D_MODEL), F32), jax.ShapeDtypeStruct((SUBLANES, D_MODEL), F32),
                   jax.ShapeDtypeStruct((SUBLANES, LANES), F32)],
        compiler_params=_cparams(("arbitrary",)),
    )(h, gamma.array, target)


def _mesh_pos():
    return lax.axis_index("x"), lax.axis_index("y"), lax.axis_index("c")


def _other_chip(x, y, j):
    fx, fy = CHIP_FLIPS[j]
    return (1 - x if fx else x), (1 - y if fy else y)


def _any_specs(n):
    return [pl.BlockSpec(memory_space=pl.ANY)] * n


class _Cargo:
    def __init__(self):
        self.operands, self.out_shapes, self.aliases, self.sem_shapes, self.names = [], [], {}, [], []
        self.start = lambda ins, outs, sems: None
        self.forward = lambda ins, outs, sems: None
        self.finish = lambda ins, outs, sems: None


def _cargo_refs(refs, cargo, n_in, n_out, n_scratch=0):
    a = n_in
    b = a + len(cargo.operands)
    c = b + n_out
    d = c + len(cargo.out_shapes)
    e = d + n_scratch
    return refs[:a], refs[a:b], refs[b:c], refs[c:d], refs[d:e], refs[e:]


def _remote(src, dst, send, recv, device):
    return pltpu.make_async_remote_copy(src_ref=src, dst_ref=dst, send_sem=send, recv_sem=recv,
                                        device_id=device, device_id_type=MESH_ID)


def _gather_cargo(bufs, pieces):
    cargo = _Cargo()
    if not pieces:
        return cargo
    plist = []
    for name, layer in pieces:
        if name not in cargo.names:
            cargo.names.append(name)
            cargo.operands.append(bufs[name])
        plist.append((cargo.names.index(name), layer, bufs[name].shape[2] // 2))
    nflip = len(CHIP_FLIPS)
    cargo.out_shapes = [jax.ShapeDtypeStruct(b.shape, b.dtype) for b in cargo.operands]
    cargo.aliases = {i: i for i in range(len(cargo.operands))}
    cargo.sem_shapes = [pltpu.SemaphoreType.DMA((len(plist) * nflip,))] * 4

    def copies(outs, sems):
        send1, recv1, send2, recv2 = sems
        x, y, c = _mesh_pos()
        k = 2 * x + y

        def blk(p, chip, cc):
            b, layer, hrows = plist[p]
            return outs[b].at[layer, chip, pl.ds(cc * hrows, hrows), :]

        def chip_of(j):
            px, py = _other_chip(x, y, j)
            return 2 * px + py

        def ici(p, j):
            px, py = _other_chip(x, y, j)
            return _remote(blk(p, k, c), blk(p, k, c), send1.at[p * nflip + j], recv1.at[p * nflip + j], (px, py, c))

        def landed(p, j):
            px, py = _other_chip(x, y, j)
            return _remote(blk(p, k, c), blk(p, chip_of(j), c), send1.at[p * nflip + j], recv1.at[p * nflip + j],
                           (px, py, c))

        def d2d(p, j, cc):
            return _remote(blk(p, chip_of(j), cc), blk(p, chip_of(j), cc), send2.at[p * nflip + j],
                           recv2.at[p * nflip + j], (x, y, 1 - c))

        return c, ici, landed, d2d

    def start(ins, outs, sems):
        _, ici, _, _ = copies(outs, sems)
        for p in range(len(plist)):
            for j in range(nflip):
                ici(p, j).start()

    def forward(ins, outs, sems):
        c, _, landed, d2d = copies(outs, sems)
        for j in range(nflip):
            for p in range(len(plist)):
                landed(p, j).wait_recv()
                d2d(p, j, c).start()

    def finish(ins, outs, sems):
        c, ici, _, d2d = copies(outs, sems)
        for p in range(len(plist)):
            for j in range(nflip):
                ici(p, j).wait_send()
                d2d(p, j, c).wait_send()
                d2d(p, j, 1 - c).wait_recv()

    cargo.start, cargo.forward, cargo.finish = start, forward, finish
    return cargo


def _reduce_cargo(grads, presums):
    cargo = _Cargo()
    na, nb = len(grads), len(presums)
    nflip = len(CHIP_FLIPS)
    cargo.operands = list(grads) + list(presums)
    cargo.out_shapes = ([jax.ShapeDtypeStruct((g.shape[0], g.shape[1] // 2, g.shape[2]), g.dtype) for g in grads]
                        + [jax.ShapeDtypeStruct((nflip,) + ps.shape[1:], ps.dtype) for ps in presums])
    cargo.sem_shapes = ([pltpu.SemaphoreType.DMA((na,))] * 2 if na else []) + (
        [pltpu.SemaphoreType.DMA((nb * nflip,))] * 2 if nb else [])

    def copies(ins, outs, sems):
        x, y, c = _mesh_pos()
        out = []
        if na:
            send, recv = sems[0], sems[1]
            for a in range(na):
                hrows = grads[a].shape[1] // 2
                out.append(_remote(ins[a].at[:, pl.ds((1 - c) * hrows, hrows), :], outs[a], send.at[a], recv.at[a],
                                   (x, y, 1 - c)))
        if nb:
            send, recv = sems[-2], sems[-1]
            for b in range(nb):
                for j in range(nflip):
                    px, py = _other_chip(x, y, j)
                    out.append(_remote(ins[na + b].at[2 * px + py], outs[na + b].at[j], send.at[b * nflip + j],
                                       recv.at[b * nflip + j], (px, py, c)))
        return out

    def start(ins, outs, sems):
        for cp in copies(ins, outs, sems):
            cp.start()

    def finish(ins, outs, sems):
        for cp in copies(ins, outs, sems):
            cp.wait()

    cargo.start, cargo.finish = start, finish
    return cargo


def _run_cargo(name, cargo):
    nin, nout = len(cargo.operands), len(cargo.out_shapes)

    def body(*refs):
        ins, outs, sems = refs[:nin], refs[nin:nin + nout], refs[nin + nout:]
        cargo.start(ins, outs, sems)
        cargo.forward(ins, outs, sems)
        cargo.finish(ins, outs, sems)

    return list(pl.pallas_call(
        body, name=name,
        in_specs=_any_specs(nin), out_specs=_any_specs(nout), out_shape=cargo.out_shapes,
        input_output_aliases=dict(cargo.aliases), scratch_shapes=list(cargo.sem_shapes),
    )(*cargo.operands))


def _join_siblings(bufs):
    nb = len(bufs)
    items = [(b, layer) for b, buf in enumerate(bufs) for layer in range(buf.shape[0])]

    def body(*refs):
        outs = refs[nb:2 * nb]
        send, recv = refs[2 * nb:]
        x, y, c = _mesh_pos()

        def half(i, cc):
            b, layer = items[i]
            hrows = bufs[b].shape[1] // 2
            blk = outs[b].at[layer, pl.ds(cc * hrows, hrows), :]
            return _remote(blk, blk, send.at[i], recv.at[i], (x, y, 1 - c))

        for i in range(len(items)):
            half(i, c).start()
        for i in range(len(items)):
            half(i, c).wait_send()
            half(i, 1 - c).wait_recv()

    return list(pl.pallas_call(
        body, name="grad_sibling_join",
        in_specs=_any_specs(nb), out_specs=_any_specs(nb),
        out_shape=[jax.ShapeDtypeStruct(b.shape, b.dtype) for b in bufs],
        input_output_aliases={i: i for i in range(nb)},
        scratch_shapes=[pltpu.SemaphoreType.DMA((len(items),))] * 2,
    )(*bufs))


def _cast_place(w3, pos, dtype):
    L, rows, cols = w3.shape

    def body(pos_ref, w_ref, o_ref):
        o_ref[...] = w_ref[...].astype(dtype)

    return pl.pallas_call(
        body, name="cast_place",
        grid_spec=pltpu.PrefetchScalarGridSpec(
            num_scalar_prefetch=1, grid=(L,),
            in_specs=[pl.BlockSpec((None, rows, cols), lambda l, pos: (l, 0, 0))],
            out_specs=pl.BlockSpec((None, None, rows, cols), lambda l, pos: (l, pos[0], 0, 0))),
        out_shape=jax.ShapeDtypeStruct((L, N_CHIPS, rows, cols), dtype),
        compiler_params=_cparams(("arbitrary",)),
    )(pos, w3)


def _allreduce_small(buf):
    R = buf.shape[0]
    half = R // 2
    assert half % SUBLANES == 0, R

    def body(in_ref, out_ref, land, send, recv):
        x, y, c = _mesh_pos()
        out_ref[...] = in_ref[...]
        cp = _remote(out_ref, land.at[0], send.at[0], recv.at[0], (x, y, 1 - c))
        cp.start()
        cp.wait()
        out_ref[...] = out_ref[...] + land[0]
        along_y, along_x = (x, 1 - y, c), (1 - x, y, c)
        lo, hi = pl.ds(0, half), pl.ds(half, half)
        for stage, (peer_lo, peer_hi) in enumerate(((along_y, along_x), (along_x, along_y))):
            slot = 1 + stage
            cps = [_remote(out_ref.at[lo], land.at[slot, lo], send.at[1 + 2 * stage], recv.at[1 + 2 * stage], peer_lo),
                   _remote(out_ref.at[hi], land.at[slot, hi], send.at[2 + 2 * stage], recv.at[2 + 2 * stage], peer_hi)]
            for cp in cps:
                cp.start()
            for cp in cps:
                cp.wait()
            out_ref[...] = out_ref[...] + land[slot]

    return pl.pallas_call(
        body, name="allreduce_small",
        in_specs=[pl.BlockSpec(memory_space=pltpu.VMEM)],
        out_specs=pl.BlockSpec(memory_space=pltpu.VMEM),
        out_shape=jax.ShapeDtypeStruct((R, LANES), F32),
        scratch_shapes=[pltpu.VMEM((3, R, LANES), F32), pltpu.SemaphoreType.DMA((5,)), pltpu.SemaphoreType.DMA((5,))],
        compiler_params=pltpu.CompilerParams(vmem_limit_bytes=VMEM_LIMIT_MB * 2 ** 20),
    )(buf)


def _presum_with_sibling(grad, landed, pos):
    nchunk, rows, cols = grad.shape
    hrows = rows // 2

    def body(pos_ref, g_ref, l_ref, all_ref, own_ref):
        s = g_ref[...] + l_ref[...]
        all_ref[...] = s.astype(BF16)

        @pl.when(pl.program_id(0) == pos_ref[0])
        def _():
            own_ref[...] = s

    return pl.pallas_call(
        body, name="grad_presum",
        grid_spec=pltpu.PrefetchScalarGridSpec(
            num_scalar_prefetch=1, grid=(nchunk,),
            in_specs=[pl.BlockSpec((None, hrows, cols), lambda k, pos: (k, pos[1], 0)),
                      pl.BlockSpec((None, hrows, cols), lambda k, pos: (k, 0, 0))],
            out_specs=[pl.BlockSpec((None, hrows, cols), lambda k, pos: (k, 0, 0)),
                       pl.BlockSpec((hrows, cols), lambda k, pos: (0, 0))]),
        out_shape=[jax.ShapeDtypeStruct((nchunk, hrows, cols), BF16), jax.ShapeDtypeStruct((hrows, cols), F32)],
        compiler_params=_cparams(("arbitrary",)),
    )(pos, grad, landed)


def _sum_chips(own, landed, stacked, layer, shape3, pos):
    hrows, cols = own.shape

    def body(pos_ref, o_ref, l_ref, *rest):
        s = o_ref[...]
        for j in range(len(CHIP_FLIPS)):
            s = s + l_ref[j].astype(F32)
        rest[-1][...] = s

    in_specs = [pl.BlockSpec((hrows, cols), lambda i, pos: (0, 0)),
                pl.BlockSpec((len(CHIP_FLIPS), hrows, cols), lambda i, pos: (0, 0, 0))]
    args = [pos, own, landed]
    aliases = {}
    if stacked is not None:
        in_specs.append(pl.BlockSpec(memory_space=pl.ANY))
        args.append(stacked)
        aliases = {3: 0}
    return pl.pallas_call(
        body, name="grad_sum_chips",
        grid_spec=pltpu.PrefetchScalarGridSpec(
            num_scalar_prefetch=1, grid=(1,), in_specs=in_specs,
            out_specs=pl.BlockSpec((None, hrows, cols), lambda i, pos: (layer, pos[1], 0))),
        out_shape=jax.ShapeDtypeStruct(shape3, F32),
        input_output_aliases=aliases,
        compiler_params=_cparams(("arbitrary",)),
    )(*args)


def _adamw(w, g, m, v):
    R, C = w.shape
    rb = R
    for cand in (512, 352, 320, 256, 128, 64, 32, 16, 8):
        if R % cand == 0:
            rb = cand
            break
    c1 = 1.0 - ADAM_B1 ** ADAM_STEP
    c2 = 1.0 - ADAM_B2 ** ADAM_STEP

    def body(w_ref, g_ref, m_ref, v_ref, go_ref, d_ref, mo_ref, vo_ref):
        gv = g_ref[...]
        go_ref[...] = gv
        m2 = ADAM_B1 * m_ref[...] + (1.0 - ADAM_B1) * gv
        v2 = ADAM_B2 * v_ref[...] + (1.0 - ADAM_B2) * (gv * gv)
        mo_ref[...] = m2
        vo_ref[...] = v2
        d_ref[...] = -ADAM_LR * ((m2 / c1) / (jnp.sqrt(v2 / c2) + ADAM_EPS) + ADAM_WD * w_ref[...])

    spec = pl.BlockSpec((rb, C), lambda i: (i, 0))
    return pl.pallas_call(
        body, name="adamw",
        grid=(R // rb,),
        in_specs=[spec] * 4, out_specs=[spec] * 4,
        out_shape=[jax.ShapeDtypeStruct((R, C), F32)] * 4,
        compiler_params=_cparams(("arbitrary",)),
    )(w, g, m, v)


def _pack(parts, align=SUBLANES * LANES):
    flat = jnp.concatenate([p.reshape(-1).astype(F32) for p in parts])
    pad = (-flat.shape[0]) % align
    return jnp.pad(flat, (0, pad)).reshape(-1, LANES)


def _unpack(buf, shapes):
    flat = buf.reshape(-1)
    out, off = [], 0
    for shp in shapes:
        size = 1
        for d in shp:
            size *= d
        out.append(flat[off:off + size].reshape(shp))
        off += size
    return out


def _block_diag_gates(w_a, w_x):
    nq = D_RNN // GATE_CHUNK
    hpc = LRU_HEADS // nq
    eye = jnp.eye(hpc, dtype=F32)

    def bd(w):
        wq = w.reshape(nq, hpc, LRU_HEAD_DIM, LRU_HEAD_DIM)
        return (wq[:, :, :, None, :] * eye[None, :, None, :, None]).reshape(nq, GATE_CHUNK, GATE_CHUNK)

    return jnp.concatenate([bd(w_a), bd(w_x)], axis=2).astype(BF16)


def _block_diag_extract(dwbd):
    nq = D_RNN // GATE_CHUNK
    hpc = LRU_HEADS // nq
    eye = jnp.eye(hpc, dtype=F32)

    def ex(d):
        d5 = d.reshape(nq, hpc, LRU_HEAD_DIM, hpc, LRU_HEAD_DIM)
        return jnp.sum(d5 * eye[None, :, None, :, None], axis=3).reshape(LRU_HEADS, LRU_HEAD_DIM, LRU_HEAD_DIM)

    return ex(dwbd[:, :, :GATE_CHUNK]), ex(dwbd[:, :, GATE_CHUNK:])


BIG = ("ffn1_w_gate", "ffn1_w_up", "ffn1_w_down", "lru_w_in", "lru_w_out", "pool_w",
       "ffn2_w_gate", "ffn2_w_up", "ffn2_w_down", "ple_w_gate", "ple_w_proj")
TINY_SHARDED = ("lru_conv_w", "pool_b", "pool_scale")
REPLICATED = ("ffn1_norm", "mix_norm", "lru_conv_b", "lru_w_a", "lru_b_a", "lru_w_x", "lru_b_x", "lru_a_param",
              "ffn2_norm", "ple_norm", "final_norm")
WEIGHT_ORDER = ("ffn1_norm", "ffn1_w_gate", "ffn1_w_up", "ffn1_w_down", "mix_norm", "lru_w_in", "lru_conv_w",
                "lru_conv_b", "lru_w_a", "lru_b_a", "lru_w_x", "lru_b_x", "lru_a_param", "lru_w_out", "pool_w",
                "pool_b", "pool_scale", "ffn2_norm", "ffn2_w_gate", "ffn2_w_up", "ffn2_w_down", "ple_norm",
                "ple_w_gate", "ple_w_proj", "final_norm")


TRANSPOSED = ("ffn1_w_gate", "ffn1_w_up", "ffn2_w_gate", "ffn2_w_up")


def _stored(name, a):
    return jnp.swapaxes(a, 1, 2) if name in TRANSPOSED else a


def _as3(a):
    return a.reshape(a.shape[0], -1, a.shape[-1])


def kernel(x, p, ffn1_norm, ffn1_w_gate, ffn1_w_up, ffn1_w_down, mix_norm, lru_w_in, lru_conv_w, lru_conv_b, lru_w_a, lru_b_a, lru_w_x, lru_b_x, lru_a_param, lru_w_out, pool_w, pool_b, pool_scale, ffn2_norm, ffn2_w_gate, ffn2_w_up, ffn2_w_down, ple_norm, ple_w_gate, ple_w_proj, final_norm, loss_target, m_ffn1_norm, m_ffn1_w_gate, m_ffn1_w_up, m_ffn1_w_down, m_mix_norm, m_lru_w_in, m_lru_conv_w, m_lru_conv_b, m_lru_w_a, m_lru_b_a, m_lru_w_x, m_lru_b_x, m_lru_a_param, m_lru_w_out, m_pool_w, m_pool_b, m_pool_scale, m_ffn2_norm, m_ffn2_w_gate, m_ffn2_w_up, m_ffn2_w_down, m_ple_norm, m_ple_w_gate, m_ple_w_proj, m_final_norm, v_ffn1_norm, v_ffn1_w_gate, v_ffn1_w_up, v_ffn1_w_down, v_mix_norm, v_lru_w_in, v_lru_conv_w, v_lru_conv_b, v_lru_w_a, v_lru_b_a, v_lru_w_x, v_lru_b_x, v_lru_a_param, v_lru_w_out, v_pool_w, v_pool_b, v_pool_scale, v_ffn2_norm, v_ffn2_w_gate, v_ffn2_w_up, v_ffn2_w_down, v_ple_norm, v_ple_w_gate, v_ple_w_proj, v_final_norm):
    W = dict(ffn1_norm=ffn1_norm, ffn1_w_gate=ffn1_w_gate, ffn1_w_up=ffn1_w_up, ffn1_w_down=ffn1_w_down,
             mix_norm=mix_norm, lru_w_in=lru_w_in, lru_conv_w=lru_conv_w, lru_conv_b=lru_conv_b, lru_w_a=lru_w_a,
             lru_b_a=lru_b_a, lru_w_x=lru_w_x, lru_b_x=lru_b_x, lru_a_param=lru_a_param, lru_w_out=lru_w_out,
             pool_w=pool_w, pool_b=pool_b, pool_scale=pool_scale, ffn2_norm=ffn2_norm, ffn2_w_gate=ffn2_w_gate,
             ffn2_w_up=ffn2_w_up, ffn2_w_down=ffn2_w_down, ple_norm=ple_norm, ple_w_gate=ple_w_gate,
             ple_w_proj=ple_w_proj, final_norm=final_norm)
    M = dict(ffn1_norm=m_ffn1_norm, ffn1_w_gate=m_ffn1_w_gate, ffn1_w_up=m_ffn1_w_up, ffn1_w_down=m_ffn1_w_down,
             mix_norm=m_mix_norm, lru_w_in=m_lru_w_in, lru_conv_w=m_lru_conv_w, lru_conv_b=m_lru_conv_b,
             lru_w_a=m_lru_w_a, lru_b_a=m_lru_b_a, lru_w_x=m_lru_w_x, lru_b_x=m_lru_b_x, lru_a_param=m_lru_a_param,
             lru_w_out=m_lru_w_out, pool_w=m_pool_w, pool_b=m_pool_b, pool_scale=m_pool_scale, ffn2_norm=m_ffn2_norm,
             ffn2_w_gate=m_ffn2_w_gate, ffn2_w_up=m_ffn2_w_up, ffn2_w_down=m_ffn2_w_down, ple_norm=m_ple_norm,
             ple_w_gate=m_ple_w_gate, ple_w_proj=m_ple_w_proj, final_norm=m_final_norm)
    V = dict(ffn1_norm=v_ffn1_norm, ffn1_w_gate=v_ffn1_w_gate, ffn1_w_up=v_ffn1_w_up, ffn1_w_down=v_ffn1_w_down,
             mix_norm=v_mix_norm, lru_w_in=v_lru_w_in, lru_conv_w=v_lru_conv_w, lru_conv_b=v_lru_conv_b,
             lru_w_a=v_lru_w_a, lru_b_a=v_lru_b_a, lru_w_x=v_lru_w_x, lru_b_x=v_lru_b_x, lru_a_param=v_lru_a_param,
             lru_w_out=v_lru_w_out, pool_w=v_pool_w, pool_b=v_pool_b, pool_scale=v_pool_scale, ffn2_norm=v_ffn2_norm,
             ffn2_w_gate=v_ffn2_w_gate, ffn2_w_up=v_ffn2_w_up, ffn2_w_down=v_ffn2_w_down, ple_norm=v_ple_norm,
             ple_w_gate=v_ple_w_gate, ple_w_proj=v_ple_w_proj, final_norm=v_final_norm)

    S = x.shape[1]
    my_x, my_y, my_c = _mesh_pos()
    my_chip = 2 * my_x + my_y
    pos = jnp.stack([my_chip, my_c]).astype(jnp.int32)
    n_lru, n_pool = lru_w_in.shape[0], pool_w.shape[0]

    tiny_shapes = [W[n].shape for n in TINY_SHARDED]
    tiny_local = _pack([W[n] for n in TINY_SHARDED], align=2 * 16 * LANES)[None]
    Ws, Ms, Vs = ({n: _stored(n, d[n]) for n in BIG} for d in (W, M, V))
    bufs = {n: _cast_place(_as3(Ws[n]), pos, BF16) for n in BIG}
    bufs["tiny"] = _cast_place(tiny_local, pos, F32)

    def gather_now(name, pieces):
        cargo = _gather_cargo(bufs, pieces)
        bufs.update(zip(cargo.names, _run_cargo(name, cargo)))

    def ffn_pieces(which, layer):
        return [("%s_w_gate" % which, layer), ("%s_w_up" % which, layer), ("%s_w_down" % which, layer)]

    def mixer_pieces(layer):
        if layer % 2 == 0:
            return [("lru_w_in", layer // 2), ("lru_w_out", layer // 2)]
        return [("pool_w", layer // 2)]

    gather_now("gather_first", [("tiny", 0)] + ffn_pieces("ffn1", 0))
    tiny_by_chip = [_unpack(bufs["tiny"][0, k], tiny_shapes) for k in range(N_CHIPS)]
    conv_w_full = jnp.concatenate([tiny_by_chip[k][0] for k in range(N_CHIPS)], axis=-1)
    pool_b_full = jnp.concatenate([tiny_by_chip[k][1] for k in range(N_CHIPS)], axis=-1)
    pool_s_full = jnp.concatenate([tiny_by_chip[k][2] for k in range(N_CHIPS)], axis=-1)
    ngroup = len(POOL_WINDOWS)

    def pool_weights():
        pw5 = bufs["pool_w"].reshape(n_pool, N_CHIPS, ngroup, POOL_GROUP_DIM // N_CHIPS, POOL_GROUP_DIM)
        return pw5.transpose(0, 2, 1, 3, 4).reshape(n_pool, ngroup, POOL_GROUP_DIM, POOL_GROUP_DIM)

    lru_out = lambda: bufs["lru_w_out"].reshape(n_lru, D_RNN, D_MODEL)
    ple_gate = lambda: bufs["ple_w_gate"].reshape(DEPTH, D_MODEL, D_MODEL)
    wbd = [_block_diag_gates(lru_w_a[j], lru_w_x[j]) for j in range(n_lru)]
    ffn1_norm, mix_norm, ffn2_norm, ple_norm = (_rows3(a) for a in (ffn1_norm, mix_norm, ffn2_norm, ple_norm))
    lru_conv_b, lru_b_a, lru_b_x, lru_a_param = (_rows3(a) for a in (lru_conv_b, lru_b_a, lru_b_x, lru_a_param))
    pool_b_full, pool_s_full = _rows3(pool_b_full), _rows3(pool_s_full)

    def ffn_forward(which, h, gamma, layer, pieces):
        cargo = _gather_cargo(bufs, pieces)
        outs, updated = _ffn_fwd(h, gamma, bufs[which + "_w_gate"], bufs[which + "_w_up"], bufs[which + "_w_down"],
                                 layer, cargo)
        bufs.update(zip(cargo.names, updated))
        return outs

    h = x.reshape(S, D_MODEL)
    saved = []
    for i in range(DEPTH):
        j = i // 2
        sv = {"h0": h}
        first_mixer = mixer_pieces(0) if i == 0 else []
        h, sv["xn1"], sv["g1"], sv["u1"] = ffn_forward(
            "ffn1", h, _Rows(ffn1_norm, i), i,
            first_mixer + ffn_pieces("ffn2", i) + [("ple_w_gate", i), ("ple_w_proj", i)])
        sv["h1"] = h
        if i % 2 == 0:
            h, sv["xn_mix"], sv["z"], sv["hs"], sv["gates"] = _lru_fwd(
                h, _Rows(mix_norm, i), bufs["lru_w_in"], j, _Rows(conv_w_full, j), _Rows(lru_conv_b, j), wbd[j],
                _Rows(lru_b_a, j), _Rows(lru_b_x, j), _Rows(lru_a_param, j), lru_out())
        else:
            h, sv["u"] = _pool_fwd(h, _Rows(mix_norm, i), pool_weights(), _Rows(pool_b_full, j), _Rows(pool_s_full, j), j)
        sv["h2"] = h
        nxt = ffn_pieces("ffn1", i + 1) + mixer_pieces(i + 1) if i + 1 < DEPTH else []
        h, sv["xn2"], sv["g2"], sv["u2"] = ffn_forward("ffn2", h, _Rows(ffn2_norm, i), i, nxt)
        sv["h3"] = h
        h = _ple_fwd(h, _Rows(ple_norm, i), p, ple_gate(), bufs["ple_w_proj"], i)
        saved.append(sv)

    dh, dgam_final, loss_part = _final(h, _Rows(final_norm.reshape(1, 1, -1), 0), loss_target.reshape(S, D_MODEL))
    win, wout, wpg, wpp, pw = bufs["lru_w_in"], lru_out(), ple_gate(), bufs["ple_w_proj"], pool_weights()

    norm_grads = {n: [None] * DEPTH for n in ("ffn1_norm", "mix_norm", "ffn2_norm", "ple_norm")}
    lru_vec = [None] * n_lru
    pool_vec = [None] * n_pool
    sum8 = lambda a: jnp.sum(a, axis=-2)

    to_siblings, to_chips = [], []
    stacked = {n: None for n in BIG}

    def take_cargo(with_chips=True):
        a_items, b_items = list(to_siblings), list(to_chips) if with_chips else []
        del to_siblings[:], to_chips[:len(b_items)]
        return _reduce_cargo([it[2] for it in a_items], [it[2] for it in b_items]), a_items, b_items

    def absorb(a_items, b_items, outs):
        for (n, layer, g), landed in zip(a_items, outs[:len(a_items)]):
            all_chunks, own = _presum_with_sibling(g, landed, pos)
            to_chips.append((n, layer, all_chunks, own))
        for (n, layer, _, own), from_chips in zip(b_items, outs[len(a_items):]):
            stacked[n] = _sum_chips(own, from_chips, stacked[n], layer, _as3(Ws[n]).shape, pos)

    def ffn_backward(which, xn, dout, gg, uu, layer, h_in, gamma):
        cargo, a_items, b_items = take_cargo()
        weights = (bufs[which + "_w_gate"], bufs[which + "_w_up"], bufs[which + "_w_down"])
        (dwg, dwu, dwd, slabs), c_outs = _ffn_bwd(xn, dout, gg, uu, *weights, layer, cargo)
        absorb(a_items, b_items, c_outs)
        dh_in, dgam, dwg, dwu, dwd = _ffn_bwd_last(xn, dout, gg, uu, *weights, layer, slabs, h_in, gamma,
                                                   dwg, dwu, dwd)
        to_siblings.extend([(which + "_w_gate", layer, dwg), (which + "_w_up", layer, dwu),
                            (which + "_w_down", layer, dwd)])
        return dh_in, sum8(dgam)

    for i in reversed(range(DEPTH)):
        j = i // 2
        sv = saved[i]
        dh, dw_pg, dw_pp, dgam = _ple_bwd(dh, sv["h3"], _Rows(ple_norm, i), p, wpg, wpp, i)
        norm_grads["ple_norm"][i] = sum8(dgam)
        to_siblings.append(("ple_w_gate", i, dw_pg.reshape(N_CHIPS, D_MODEL // N_CHIPS, D_MODEL)))
        to_siblings.append(("ple_w_proj", i, dw_pp))

        dh, norm_grads["ffn2_norm"][i] = ffn_backward("ffn2", sv["xn2"], dh, sv["g2"], sv["u2"], i, sv["h2"],
                                                      _Rows(ffn2_norm, i))

        if i % 2 == 0:
            cargo, a_items, b_items = take_cargo(with_chips=False)
            (dz, dpre, xc_b, y_b, dcw, vec), c_outs = _lru_bwd_seq(
                dh, sv["z"], sv["hs"], sv["gates"], _Rows(conv_w_full, j), wbd[j], _Rows(lru_a_param, j), wout, j, cargo)
            absorb(a_items, b_items, c_outs)
            to_siblings.append(("lru_w_out", j, _xt_dy("lru_dw_out", y_b, dh, 1, D_RNN, D_MODEL, False, False)
                                .reshape(N_CHIPS, D_RNN // N_CHIPS, D_MODEL)))
            to_siblings.append(("lru_w_in", j, _xt_dy("lru_dw_in", sv["xn_mix"], dz, N_CHIPS, D_MODEL, RNN_IN_CHUNK,
                                                      False, True)))
            dwbd = _xt_dy("lru_dw_gates", xc_b, dpre, D_RNN // GATE_CHUNK, GATE_CHUNK, 2 * GATE_CHUNK, True, True)
            dw_a, dw_x = _block_diag_extract(dwbd)
            vsum = sum8(vec)
            lru_vec[j] = (sum8(dcw), vsum[0], vsum[1], vsum[2], vsum[3], dw_a, dw_x)
            cargo, a_items, b_items = take_cargo(with_chips=False)
            (dh, dgam), c_outs = _lru_bwd_in(dz, sv["h1"], _Rows(mix_norm, i), dh, win, j, cargo)
            absorb(a_items, b_items, c_outs)
            norm_grads["mix_norm"][i] = sum8(dgam)
        else:
            dh_new, dpre_b, vec = _pool_bwd(dh, sv["h1"], sv["u"], _Rows(mix_norm, i), pw, _Rows(pool_b_full, j),
                                            _Rows(pool_s_full, j), j)
            dpw = _xt_dy("pool_dw", sv["u"], dpre_b, ngroup, POOL_GROUP_DIM, POOL_GROUP_DIM, True, True)
            dpw = dpw.reshape(ngroup, N_CHIPS, POOL_GROUP_DIM // N_CHIPS, POOL_GROUP_DIM).transpose(1, 0, 2, 3)
            to_siblings.append(("pool_w", j, dpw.reshape(N_CHIPS, POOL_GROUP_DIM, POOL_GROUP_DIM)))
            vsum = sum8(vec)
            norm_grads["mix_norm"][i] = vsum[0]
            pool_vec[j] = (vsum[1], vsum[2])
            dh = dh_new

        dh, norm_grads["ffn1_norm"][i] = ffn_backward("ffn1", sv["xn1"], dh, sv["g1"], sv["u1"], i, sv["h0"],
                                                      _Rows(ffn1_norm, i))

    grad_x = dh.reshape(1, S, D_MODEL)

    small_parts = [
        jnp.stack(norm_grads["ffn1_norm"]), jnp.stack(norm_grads["mix_norm"]),
        jnp.stack(norm_grads["ffn2_norm"]), jnp.stack(norm_grads["ple_norm"]), sum8(dgam_final),
        jnp.stack([lv[0] for lv in lru_vec]), jnp.stack([lv[1] for lv in lru_vec]),
        jnp.stack([lv[2] for lv in lru_vec]), jnp.stack([lv[3] for lv in lru_vec]),
        jnp.stack([lv[4] for lv in lru_vec]), jnp.stack([lv[5] for lv in lru_vec]),
        jnp.stack([lv[6] for lv in lru_vec]),
        jnp.stack([pv[0] for pv in pool_vec]), jnp.stack([pv[1] for pv in pool_vec]),
        jnp.sum(loss_part).reshape(1),
    ]
    small_names = ("ffn1_norm", "mix_norm", "ffn2_norm", "ple_norm", "final_norm", "lru_conv_w", "lru_conv_b",
                   "lru_b_a", "lru_b_x", "lru_a_param", "lru_w_a", "lru_w_x", "pool_b", "pool_scale", "loss")
    reduced = _unpack(_allreduce_small(_pack(small_parts, align=2 * SUBLANES * LANES)),
                      [sp.shape for sp in small_parts])
    small_grad = dict(zip(small_names, reduced))
    loss = small_grad.pop("loss").reshape(())
    for n in TINY_SHARDED:
        width = W[n].shape[-1]
        small_grad[n] = lax.dynamic_slice_in_dim(small_grad[n], my_chip * width, width, axis=-1)

    tail = 0
    while to_siblings or to_chips:
        cargo, a_items, b_items = take_cargo()
        absorb(a_items, b_items, _run_cargo("grad_exchange_tail%d" % tail, cargo))
        tail += 1
    big_final = dict(zip(BIG, _join_siblings([stacked[n] for n in BIG])))

    grads, deltas, new_m, new_v = {}, {}, {}, {}
    for n in BIG:
        shp = Ws[n].shape
        to2 = lambda a: a.reshape(-1, shp[-1])
        g2, d, m2, v2 = _adamw(to2(Ws[n]), to2(big_final[n]), to2(Ms[n]), to2(Vs[n]))
        grads[n], deltas[n], new_m[n], new_v[n] = (_stored(n, a.reshape(shp)) for a in (g2, d, m2, v2))
    small_order = TINY_SHARDED + REPLICATED
    small_shapes = [W[n].shape for n in small_order]
    pack_rows = functools.partial(_pack, align=512 * LANES)
    _, sd, sm, sv_ = _adamw(pack_rows([W[n] for n in small_order]), pack_rows([small_grad[n] for n in small_order]),
                            pack_rows([M[n] for n in small_order]), pack_rows([V[n] for n in small_order]))
    for n, d, m2, v2 in zip(small_order, _unpack(sd, small_shapes), _unpack(sm, small_shapes),
                            _unpack(sv_, small_shapes)):
        grads[n], deltas[n], new_m[n], new_v[n] = small_grad[n].reshape(W[n].shape), d, m2, v2

    return (loss, grad_x, *[grads[n] for n in WEIGHT_ORDER], *[deltas[n] for n in WEIGHT_ORDER],
            *[new_m[n] for n in WEIGHT_ORDER], *[new_v[n] for n in WEIGHT_ORDER])
```

```python
import functools

import jax
import jax.numpy as jnp
from jax import lax
from jax.experimental import pallas as pl
from jax.experimental.pallas import tpu as pltpu

F32 = jnp.float32
BF16 = jnp.bfloat16

D_MODEL = 1024
D_FF = 2816
D_RNN = 1280
DEPTH = 4
N_CHIPS = 4
FF_CHUNK = D_FF // N_CHIPS
RNN_IN_CHUNK = 2 * D_RNN // N_CHIPS
GATE_CHUNK = 640
N_GATE_PLANES = 5
LRU_HEADS = 16
LRU_HEAD_DIM = 80
CONV_WIDTH = 4
LRU_C = 8.0
POOL_WINDOWS = (2, 4, 8, 16)
POOL_GROUP_DIM = 256
PLE_DIM = 256
RMS_EPS = 1e-6
POOL_HALO = 16
SUBLANES = 8
LANES = 128

ADAM_LR = 0.001
ADAM_B1 = 0.9
ADAM_B2 = 0.999
ADAM_EPS = 1e-08
ADAM_WD = 0.01
ADAM_STEP = 10

VMEM_LIMIT_MB = 56
MESH_ID = pl.DeviceIdType.MESH
CHIP_FLIPS = ((1, 0), (0, 1), (1, 1))


def _cparams(semantics):
    return pltpu.CompilerParams(dimension_semantics=semantics, vmem_limit_bytes=VMEM_LIMIT_MB * 2 ** 20)


def _dot(a, b):
    return lax.dot_general(a, b, (((1,), (0,)), ((), ())), preferred_element_type=F32)


def _dot_nt(a, b):
    return lax.dot_general(a, b, (((1,), (1,)), ((), ())), preferred_element_type=F32)


def _dot_tn(a, b):
    return lax.dot_general(a, b, (((0,), (0,)), ((), ())), preferred_element_type=F32)


def _sigmoid(x):
    return 1.0 / (1.0 + jnp.exp(-x))


def _rms(hf, gamma):
    rstd = lax.rsqrt(jnp.mean(hf * hf, axis=-1, keepdims=True) + RMS_EPS)
    xhat = hf * rstd
    return xhat, rstd, xhat * gamma


def _rms_bwd(xhat, rstd, gamma, dxn):
    dxhat = dxn * gamma
    m = jnp.mean(dxhat * xhat, axis=-1, keepdims=True)
    return rstd * (dxhat - xhat * m), _rowsum8(dxn * xhat)


def _rowsum8(v):
    tm, n = v.shape
    return jnp.sum(v.reshape(tm // SUBLANES, SUBLANES, n), axis=0)


def _gelu(x):
    u = 0.7978845608028654 * (x + 0.044715 * x * x * x)
    return 0.5 * x * (1.0 + jnp.tanh(u))


def _gelu_and_grad(x):
    c = 0.7978845608028654
    u = c * (x + 0.044715 * x * x * x)
    th = jnp.tanh(u)
    g = 0.5 * x * (1.0 + th)
    dg = 0.5 * (1.0 + th) + 0.5 * x * (1.0 - th * th) * c * (1.0 + 3.0 * 0.044715 * x * x)
    return g, dg


def _softplus(z):
    e = jnp.exp(-jnp.abs(z))
    u = 1.0 + e
    log1p = jnp.where(u == 1.0, e, jnp.log(u) * e / jnp.where(u == 1.0, 1.0, u - 1.0))
    return jnp.maximum(z, 0.0) + log1p


def _neg_expm1(x):
    series = -x * (1.0 + x * (0.5 + x * (1.0 / 6.0)))
    return jnp.where(x > -1e-2, series, 1.0 - jnp.exp(x))


def _shift_down(ext, j, halo):
    return pltpu.roll(ext, j, 0)[halo:]


def _shift_up(ext, j, tm):
    n = ext.shape[0]
    return pltpu.roll(ext, n - j, 0)[:tm]


def _scan_causal(a, b):
    tm, n = a.shape
    head_rows = lax.broadcasted_iota(jnp.int32, (SUBLANES, n), 0)
    s = 1
    while s < min(SUBLANES, tm):
        keep = head_rows >= s
        a_r, b_r = pltpu.roll(a, s, 0), pltpu.roll(b, s, 0)
        a_sh = jnp.concatenate([jnp.where(keep, a_r[:SUBLANES], 1.0), a_r[SUBLANES:]], axis=0)
        b_sh = jnp.concatenate([jnp.where(keep, b_r[:SUBLANES], 0.0), b_r[SUBLANES:]], axis=0)
        b = a * b_sh + b
        a = a * a_sh
        s *= 2
    while s < tm:
        b = jnp.concatenate([b[:s], a[s:] * b[:tm - s] + b[s:]], axis=0)
        a = jnp.concatenate([a[:s], a[s:] * a[:tm - s]], axis=0)
        s *= 2
    return a, b


def _scan_anticausal(c, d):
    tm, n = c.shape
    body = tm - SUBLANES
    tail_rows = lax.broadcasted_iota(jnp.int32, (SUBLANES, n), 0) + body
    s = 1
    while s < min(SUBLANES, tm):
        keep = tail_rows < tm - s
        c_r, d_r = pltpu.roll(c, tm - s, 0), pltpu.roll(d, tm - s, 0)
        c_sh = jnp.concatenate([c_r[:body], jnp.where(keep, c_r[body:], 1.0)], axis=0)
        d_sh = jnp.concatenate([d_r[:body], jnp.where(keep, d_r[body:], 0.0)], axis=0)
        d = d + c * d_sh
        c = c * c_sh
        s *= 2
    while s < tm:
        d = jnp.concatenate([d[:tm - s] + c[:tm - s] * d[s:], d[tm - s:]], axis=0)
        c = jnp.concatenate([c[:tm - s] * c[s:], c[tm - s:]], axis=0)
        s *= 2
    return c, d


def _rows3(stacked):
    return stacked.reshape(stacked.shape[0], 1, stacked.shape[-1])


class _Rows:
    def __init__(self, stacked3, index):
        assert stacked3.ndim == 3, stacked3.shape
        self.array = stacked3
        self.index = index

    def spec(self):
        index = self.index
        return pl.BlockSpec((None,) + self.array.shape[1:], lambda *_: (index, 0, 0))


def _tile(n, want):
    t = min(n, want)
    assert n % t == 0, (n, t)
    return t


def _ffn_fwd(h, gamma, wg, wu, wd, layer, cargo=None):
    S = h.shape[0]
    tm = _tile(S, 1024)
    nt = S // tm
    cargo = cargo or _Cargo()
    n_in, n_out = 5, 4
    nc_in, nc_out = len(cargo.operands), len(cargo.out_shapes)

    def body(*refs):
        h_ref, g_ref, wg_ref, wu_ref, wd_ref = refs[:n_in]
        c_ins = refs[n_in:n_in + nc_in]
        ho_ref, xn_ref, gg_ref, uu_ref = refs[n_in + nc_in:n_in + nc_in + n_out]
        c_outs = refs[n_in + nc_in + n_out:n_in + nc_in + n_out + nc_out]
        xn_s, acc_s = refs[n_in + nc_in + n_out + nc_out:n_in + nc_in + n_out + nc_out + 2]
        sems = refs[n_in + nc_in + n_out + nc_out + 2:]
        t, k = pl.program_id(0), pl.program_id(1)

        @pl.when((t == 0) & (k == 0))
        def _():
            cargo.start(c_ins, c_outs, sems)

        @pl.when((t == nt - 1) & (k == 0))
        def _():
            cargo.forward(c_ins, c_outs, sems)

        @pl.when(k == 0)
        def _():
            _, _, xn = _rms(h_ref[...], g_ref[...])
            xnb = xn.astype(BF16)
            xn_s[...] = xnb
            xn_ref[...] = xnb
            acc_s[...] = jnp.zeros_like(acc_s)

        xnb = xn_s[...]
        g = _dot_nt(xnb, wg_ref[...])
        u = _dot_nt(xnb, wu_ref[...])
        gg_ref[...] = g.astype(BF16)
        uu_ref[...] = u.astype(BF16)
        hid = (g * _sigmoid(g)) * u
        acc_s[...] += _dot(hid.astype(BF16), wd_ref[...])

        @pl.when(k == N_CHIPS - 1)
        def _():
            ho_ref[...] = h_ref[...] + 0.5 * acc_s[...]

        @pl.when((t == nt - 1) & (k == N_CHIPS - 1))
        def _():
            cargo.finish(c_ins, c_outs, sems)

    outs = pl.pallas_call(
        body, name="ffn_fwd",
        grid=(nt, N_CHIPS),
        in_specs=[
            pl.BlockSpec((tm, D_MODEL), lambda t, k: (t, 0)),
            gamma.spec(),
            pl.BlockSpec((None, None, FF_CHUNK, D_MODEL), lambda t, k: (layer, k, 0, 0)),
            pl.BlockSpec((None, None, FF_CHUNK, D_MODEL), lambda t, k: (layer, k, 0, 0)),
            pl.BlockSpec((None, None, FF_CHUNK, D_MODEL), lambda t, k: (layer, k, 0, 0)),
        ] + _any_specs(nc_in),
        out_specs=[
            pl.BlockSpec((tm, D_MODEL), lambda t, k: (t, 0)),
            pl.BlockSpec((tm, D_MODEL), lambda t, k: (t, 0)),
            pl.BlockSpec((None, tm, FF_CHUNK), lambda t, k: (k, t, 0)),
            pl.BlockSpec((None, tm, FF_CHUNK), lambda t, k: (k, t, 0)),
        ] + _any_specs(nc_out),
        out_shape=[
            jax.ShapeDtypeStruct((S, D_MODEL), F32),
            jax.ShapeDtypeStruct((S, D_MODEL), BF16),
            jax.ShapeDtypeStruct((N_CHIPS, S, FF_CHUNK), BF16),
            jax.ShapeDtypeStruct((N_CHIPS, S, FF_CHUNK), BF16),
        ] + cargo.out_shapes,
        input_output_aliases={n_in + i: n_out + o for i, o in cargo.aliases.items()},
        scratch_shapes=[pltpu.VMEM((tm, D_MODEL), BF16), pltpu.VMEM((tm, D_MODEL), F32)] + cargo.sem_shapes,
        compiler_params=_cparams(("arbitrary", "arbitrary")),
    )(h, gamma.array, wg, wu, wd, *cargo.operands)
    return outs[:n_out], list(outs[n_out:])


def _ffn_bwd(xn, dout, gg, uu, wg, wu, wd, layer, cargo=None):
    S = xn.shape[0]
    tm = _tile(S, 512)
    nt = S // tm
    nchunk = N_CHIPS - 1
    cargo = cargo or _Cargo()
    n_in, n_out = 7, 4
    nc_in, nc_out = len(cargo.operands), len(cargo.out_shapes)

    def body(*refs):
        xn_ref, do_ref, gg_ref, uu_ref, wg_ref, wu_ref, wd_ref = refs[:n_in]
        c_ins = refs[n_in:n_in + nc_in]
        dwg_ref, dwu_ref, dwd_ref, slab_ref = refs[n_in + nc_in:n_in + nc_in + n_out]
        c_outs = refs[n_in + nc_in + n_out:n_in + nc_in + n_out + nc_out]
        sems = refs[n_in + nc_in + n_out + nc_out:]
        k, t = pl.program_id(0), pl.program_id(1)

        @pl.when((k == 0) & (t == 0))
        def _():
            cargo.start(c_ins, c_outs, sems)

        @pl.when(t == 0)
        def _():
            dwg_ref[...] = jnp.zeros_like(dwg_ref)
            dwu_ref[...] = jnp.zeros_like(dwu_ref)
            dwd_ref[...] = jnp.zeros_like(dwd_ref)

        xnb = xn_ref[...]
        dob = (0.5 * do_ref[...]).astype(BF16)
        g = gg_ref[...].astype(F32)
        u = uu_ref[...].astype(F32)
        s = _sigmoid(g)
        sil = g * s
        dhid = _dot_nt(dob, wd_ref[...])
        dwd_ref[...] += _dot_tn((sil * u).astype(BF16), dob)
        du = (dhid * sil).astype(BF16)
        dg = (dhid * u * (s * (1.0 + g * (1.0 - s)))).astype(BF16)
        dwg_ref[...] += _dot_tn(dg, xnb)
        dwu_ref[...] += _dot_tn(du, xnb)
        slab_ref[...] = (_dot(dg, wg_ref[...]) + _dot(du, wu_ref[...])).astype(BF16)

        @pl.when((k == nchunk - 1) & (t == nt - 1))
        def _():
            cargo.finish(c_ins, c_outs, sems)

    outs = pl.pallas_call(
        body, name="ffn_bwd",
        grid=(nchunk, nt),
        in_specs=[
            pl.BlockSpec((tm, D_MODEL), lambda k, t: (t, 0)),
            pl.BlockSpec((tm, D_MODEL), lambda k, t: (t, 0)),
            pl.BlockSpec((None, tm, FF_CHUNK), lambda k, t: (k, t, 0)),
            pl.BlockSpec((None, tm, FF_CHUNK), lambda k, t: (k, t, 0)),
            pl.BlockSpec((None, None, FF_CHUNK, D_MODEL), lambda k, t: (layer, k, 0, 0)),
            pl.BlockSpec((None, None, FF_CHUNK, D_MODEL), lambda k, t: (layer, k, 0, 0)),
            pl.BlockSpec((None, None, FF_CHUNK, D_MODEL), lambda k, t: (layer, k, 0, 0)),
        ] + _any_specs(nc_in),
        out_specs=[
            pl.BlockSpec((None, FF_CHUNK, D_MODEL), lambda k, t: (k, 0, 0)),
            pl.BlockSpec((None, FF_CHUNK, D_MODEL), lambda k, t: (k, 0, 0)),
            pl.BlockSpec((None, FF_CHUNK, D_MODEL), lambda k, t: (k, 0, 0)),
            pl.BlockSpec((None, tm, D_MODEL), lambda k, t: (k, t, 0)),
        ] + _any_specs(nc_out),
        out_shape=[
            jax.ShapeDtypeStruct((N_CHIPS, FF_CHUNK, D_MODEL), F32),
            jax.ShapeDtypeStruct((N_CHIPS, FF_CHUNK, D_MODEL), F32),
            jax.ShapeDtypeStruct((N_CHIPS, FF_CHUNK, D_MODEL), F32),
            jax.ShapeDtypeStruct((nchunk, S, D_MODEL), BF16),
        ] + cargo.out_shapes,
        input_output_aliases={n_in + i: n_out + o for i, o in cargo.aliases.items()},
        scratch_shapes=list(cargo.sem_shapes),
        compiler_params=_cparams(("arbitrary", "arbitrary")),
    )(xn, dout, gg, uu, wg, wu, wd, *cargo.operands)
    return outs[:n_out], list(outs[n_out:])


def _ffn_bwd_last(xn, dout, gg, uu, wg, wu, wd, layer, slabs, h, gamma, dwg, dwu, dwd):
    S = xn.shape[0]
    tm = _tile(S, 512)
    k = N_CHIPS - 1
    nprev = slabs.shape[0]

    def body(xn_ref, do_ref, gg_ref, uu_ref, wg_ref, wu_ref, wd_ref, slab_ref, h_ref, g_ref, _dwg, _dwu, _dwd,
             dh_ref, dgam_ref, dwg_ref, dwu_ref, dwd_ref):
        @pl.when(pl.program_id(0) == 0)
        def _():
            dgam_ref[...] = jnp.zeros_like(dgam_ref)
            dwg_ref[...] = jnp.zeros_like(dwg_ref)
            dwu_ref[...] = jnp.zeros_like(dwu_ref)
            dwd_ref[...] = jnp.zeros_like(dwd_ref)

        xnb = xn_ref[...]
        do = do_ref[...]
        dob = (0.5 * do).astype(BF16)
        g = gg_ref[...].astype(F32)
        u = uu_ref[...].astype(F32)
        s = _sigmoid(g)
        sil = g * s
        dhid = _dot_nt(dob, wd_ref[...])
        dwd_ref[...] += _dot_tn((sil * u).astype(BF16), dob)
        du = (dhid * sil).astype(BF16)
        dg = (dhid * u * (s * (1.0 + g * (1.0 - s)))).astype(BF16)
        dwg_ref[...] += _dot_tn(dg, xnb)
        dwu_ref[...] += _dot_tn(du, xnb)
        dxn = _dot(dg, wg_ref[...]) + _dot(du, wu_ref[...])
        for i in range(nprev):
            dxn = dxn + slab_ref[i].astype(F32)
        xhat, rstd, _ = _rms(h_ref[...], g_ref[...])
        dhn, dgam = _rms_bwd(xhat, rstd, g_ref[...], dxn)
        dh_ref[...] = do + dhn
        dgam_ref[...] += dgam

    tile = pl.BlockSpec((tm, D_MODEL), lambda t: (t, 0))
    hidden = pl.BlockSpec((None, tm, FF_CHUNK), lambda t: (k, t, 0))
    w_in = pl.BlockSpec((None, None, FF_CHUNK, D_MODEL), lambda t: (layer, k, 0, 0))
    dw_in = pl.BlockSpec((None, FF_CHUNK, D_MODEL), lambda t: (k, 0, 0))
    return pl.pallas_call(
        body, name="ffn_bwd_last",
        grid=(S // tm,),
        in_specs=[tile, tile, hidden, hidden, w_in, w_in,
                  pl.BlockSpec((None, None, FF_CHUNK, D_MODEL), lambda t: (layer, k, 0, 0)),
                  pl.BlockSpec((nprev, tm, D_MODEL), lambda t: (0, t, 0)), tile,
                  gamma.spec()] + _any_specs(3),
        out_specs=[tile, pl.BlockSpec((SUBLANES, D_MODEL), lambda t: (0, 0)), dw_in, dw_in,
                   pl.BlockSpec((None, FF_CHUNK, D_MODEL), lambda t: (k, 0, 0))],
        out_shape=[jax.ShapeDtypeStruct((S, D_MODEL), F32), jax.ShapeDtypeStruct((SUBLANES, D_MODEL), F32),
                   jax.ShapeDtypeStruct(dwg.shape, F32), jax.ShapeDtypeStruct(dwu.shape, F32),
                   jax.ShapeDtypeStruct(dwd.shape, F32)],
        input_output_aliases={10: 2, 11: 3, 12: 4},
        compiler_params=_cparams(("arbitrary",)),
    )(xn, dout, gg, uu, wg, wu, wd, slabs, h, gamma.array, dwg, dwu, dwd)


def _xt_dy(name, x, dy, nchunk, kb, nb, x_by_chunk, y_by_chunk):
    S = x.shape[0]
    tm = _tile(S, 2048)

    def body(x_ref, dy_ref, o_ref):
        @pl.when(pl.program_id(1) == 0)
        def _():
            o_ref[...] = jnp.zeros_like(o_ref)

        o_ref[...] += _dot_tn(x_ref[...].astype(BF16), dy_ref[...].astype(BF16))

    return pl.pallas_call(
        body, name=name,
        grid=(nchunk, S // tm),
        in_specs=[
            pl.BlockSpec((tm, kb), (lambda c, t: (t, c)) if x_by_chunk else (lambda c, t: (t, 0))),
            pl.BlockSpec((tm, nb), (lambda c, t: (t, c)) if y_by_chunk else (lambda c, t: (t, 0))),
        ],
        out_specs=pl.BlockSpec((None, kb, nb), lambda c, t: (c, 0, 0)),
        out_shape=jax.ShapeDtypeStruct((nchunk, kb, nb), F32),
        compiler_params=_cparams(("arbitrary", "arbitrary")),
    )(x, dy)


def _lru_gates(xc, wbd_ref, ba, bx, apar):
    xcb = xc.astype(BF16)
    r_parts, ig_parts = [], []
    for q in range(D_RNN // GATE_CHUNK):
        lo, hi = q * GATE_CHUNK, (q + 1) * GATE_CHUNK
        pre = _dot(xcb[:, lo:hi], wbd_ref[q])
        r_parts.append(_sigmoid(pre[:, :GATE_CHUNK] + ba[:, lo:hi]))
        ig_parts.append(_sigmoid(pre[:, GATE_CHUNK:] + bx[:, lo:hi]))
    r = jnp.concatenate(r_parts, axis=1)
    ig = jnp.concatenate(ig_parts, axis=1)
    sp = LRU_C * _softplus(-apar)
    log_a = -(r * sp)
    a = jnp.exp(log_a)
    mult = jnp.sqrt(_neg_expm1(2.0 * log_a))
    return r, ig, a, mult, sp


def _conv_causal(xb, tail, cw_ref, cb):
    ext = jnp.concatenate([tail, xb], axis=0)
    xc = cb + cw_ref[CONV_WIDTH - 1:CONV_WIDTH, :] * xb
    for j in range(1, CONV_WIDTH):
        xc = xc + cw_ref[CONV_WIDTH - 1 - j:CONV_WIDTH - j, :] * _shift_down(ext, j, SUBLANES)
    return xc, ext


def _lru_fwd(h, gamma, win, layer, convw, convb, wbd, ba, bx, apar, wout):
    S = h.shape[0]
    tm = _tile(S, 256)

    def body(h_ref, g_ref, win_ref, cw_ref, cb_ref, wbd_ref, ba_ref, bx_ref, ap_ref, wout_ref,
             ho_ref, xn_ref, z_ref, hs_ref, gates_ref, tail_s, carry_s):
        @pl.when(pl.program_id(0) == 0)
        def _():
            tail_s[...] = jnp.zeros_like(tail_s)
            carry_s[...] = jnp.zeros_like(carry_s)

        hf = h_ref[...]
        _, _, xn = _rms(hf, g_ref[...])
        xnb = xn.astype(BF16)
        xn_ref[...] = xnb
        for k in range(N_CHIPS):
            z_ref[:, k * RNN_IN_CHUNK:(k + 1) * RNN_IN_CHUNK] = _dot(xnb, win_ref[k])
        gate = z_ref[:, :D_RNN]
        xb = z_ref[:, D_RNN:]
        xc, _ = _conv_causal(xb, tail_s[...], cw_ref, cb_ref[...])
        tail_s[...] = xb[tm - SUBLANES:, :]
        r, ig, a, mult, _ = _lru_gates(xc, wbd_ref, ba_ref[...], bx_ref[...], ap_ref[...])
        for plane, val in enumerate((xc, r, ig, a, mult)):
            gates_ref[plane] = val
        big_a, big_b = _scan_causal(a, mult * (ig * xc))
        hs = big_a * carry_s[SUBLANES - 1:SUBLANES, :] + big_b
        hs_ref[...] = hs
        carry_s[...] = hs[tm - SUBLANES:, :]
        y = hs * _gelu(gate)
        ho_ref[...] = hf + _dot(y.astype(BF16), wout_ref[...])

    return pl.pallas_call(
        body, name="lru_fwd",
        grid=(S // tm,),
        in_specs=[
            pl.BlockSpec((tm, D_MODEL), lambda t: (t, 0)),
            gamma.spec(),
            pl.BlockSpec((None, N_CHIPS, D_MODEL, RNN_IN_CHUNK), lambda t: (layer, 0, 0, 0)),
            convw.spec(),
            convb.spec(),
            pl.BlockSpec((D_RNN // GATE_CHUNK, GATE_CHUNK, 2 * GATE_CHUNK), lambda t: (0, 0, 0)),
            ba.spec(), bx.spec(), apar.spec(),
            pl.BlockSpec((None, D_RNN, D_MODEL), lambda t: (layer, 0, 0)),
        ],
        out_specs=[
            pl.BlockSpec((tm, D_MODEL), lambda t: (t, 0)),
            pl.BlockSpec((tm, D_MODEL), lambda t: (t, 0)),
            pl.BlockSpec((tm, 2 * D_RNN), lambda t: (t, 0)),
            pl.BlockSpec((tm, D_RNN), lambda t: (t, 0)),
            pl.BlockSpec((N_GATE_PLANES, tm, D_RNN), lambda t: (0, t, 0)),
        ],
        out_shape=[
            jax.ShapeDtypeStruct((S, D_MODEL), F32),
            jax.ShapeDtypeStruct((S, D_MODEL), BF16),
            jax.ShapeDtypeStruct((S, 2 * D_RNN), F32),
            jax.ShapeDtypeStruct((S, D_RNN), F32),
            jax.ShapeDtypeStruct((N_GATE_PLANES, S, D_RNN), F32),
        ],
        scratch_shapes=[pltpu.VMEM((SUBLANES, D_RNN), F32), pltpu.VMEM((SUBLANES, D_RNN), F32)],
        compiler_params=_cparams(("arbitrary",)),
    )(h, gamma.array, win, convw.array, convb.array, wbd, ba.array, bx.array, apar.array, wout)


def _lru_bwd_seq(dout, z, hs, gates, convw, wbd, apar, wout, layer, cargo=None):
    S = dout.shape[0]
    tm = _tile(S, 256)
    nt = S // tm
    per8 = tm // SUBLANES
    rev = lambda i: nt - 1 - i
    prev8 = lambda i: jnp.maximum(rev(i) * per8 - 1, 0)
    cargo = cargo or _Cargo()
    n_in, n_out = 10, 6

    def body(*refs):
        ins, c_ins, outs, c_outs, scratch, sems = _cargo_refs(refs, cargo, n_in, n_out, n_scratch=3)
        do_ref, z_ref, hs_ref, gates_ref, ztail_ref, hstail_ref, cw_ref, wbd_ref, ap_ref, wout_ref = ins
        dz_ref, dpre_ref, xc_ref, y_ref, dcw_ref, vec_ref = outs
        a_first_s, g_first_s, dxc_head_s = scratch
        i = pl.program_id(0)
        first_in_time = rev(i) == 0

        @pl.when(i == 0)
        def _():
            cargo.start(c_ins, c_outs, sems)
            a_first_s[...] = jnp.zeros_like(a_first_s)
            g_first_s[...] = jnp.zeros_like(g_first_s)
            dxc_head_s[...] = jnp.zeros_like(dxc_head_s)
            dcw_ref[...] = jnp.zeros_like(dcw_ref)
            vec_ref[...] = jnp.zeros_like(vec_ref)

        gate = z_ref[:, :D_RNN]
        xb = z_ref[:, D_RNN:]
        hist = jnp.where(first_in_time, 0.0, 1.0)
        xext = jnp.concatenate([ztail_ref[:, D_RNN:] * hist, xb], axis=0)
        xc, r, ig, a, mult = (gates_ref[plane] for plane in range(N_GATE_PLANES))
        sp = LRU_C * _softplus(-ap_ref[...])
        hs = hs_ref[...]
        gel, dgel = _gelu_and_grad(gate)
        y = hs * gel
        y_ref[...] = y.astype(BF16)
        xc_ref[...] = xc.astype(BF16)

        dy = _dot_nt(do_ref[...].astype(BF16), wout_ref[...])
        dhs = dy * gel
        dgate = dy * hs * dgel

        coef = _shift_up(jnp.concatenate([a, a_first_s[...]], axis=0), 1, tm)
        big_c, big_d = _scan_anticausal(coef, dhs)
        g = big_d + big_c * g_first_s[0:1, :]
        g_first_s[...] = g[:SUBLANES, :]
        a_first_s[...] = a[:SUBLANES, :]

        hs_prev = _shift_down(jnp.concatenate([hstail_ref[...] * hist, hs], axis=0), 1, SUBLANES)
        da = g * hs_prev
        dmult = g * ig * xc
        dig = g * mult * xc
        dxc = g * mult * ig
        dlog_a = da * a - dmult * (a * a) / mult
        dr = -(dlog_a * sp)
        dpre_a = dr * r * (1.0 - r)
        dpre_x = dig * ig * (1.0 - ig)
        d_apar = dlog_a * r * (LRU_C * _sigmoid(-ap_ref[...]))

        for q in range(D_RNN // GATE_CHUNK):
            lo, hi = q * GATE_CHUNK, (q + 1) * GATE_CHUNK
            dpre_q = jnp.concatenate([dpre_a[:, lo:hi], dpre_x[:, lo:hi]], axis=1).astype(BF16)
            dpre_ref[:, 2 * lo:2 * hi] = dpre_q
            dxc_q = _dot_nt(dpre_q, wbd_ref[q])
            if q == 0:
                dxc_parts = [dxc_q]
            else:
                dxc_parts.append(dxc_q)
        dxc = dxc + jnp.concatenate(dxc_parts, axis=1)

        dext = jnp.concatenate([dxc, dxc_head_s[...]], axis=0)
        dxb = cw_ref[CONV_WIDTH - 1:CONV_WIDTH, :] * dxc
        for j in range(1, CONV_WIDTH):
            dxb = dxb + cw_ref[CONV_WIDTH - 1 - j:CONV_WIDTH - j, :] * _shift_up(dext, j, tm)
        dxc_head_s[...] = dxc[:SUBLANES, :]
        dz_ref[:, :D_RNN] = dgate.astype(BF16)
        dz_ref[:, D_RNN:] = dxb.astype(BF16)

        dcw_ref[CONV_WIDTH - 1] += _rowsum8(dxc * xb)
        for j in range(1, CONV_WIDTH):
            dcw_ref[CONV_WIDTH - 1 - j] += _rowsum8(dxc * _shift_down(xext, j, SUBLANES))
        vec_ref[0] += _rowsum8(dxc)
        vec_ref[1] += _rowsum8(dpre_a)
        vec_ref[2] += _rowsum8(dpre_x)
        vec_ref[3] += _rowsum8(d_apar)

        @pl.when(i == nt - 1)
        def _():
            cargo.finish(c_ins, c_outs, sems)

    outs = pl.pallas_call(
        body, name="lru_bwd_seq",
        grid=(nt,),
        in_specs=[
            pl.BlockSpec((tm, D_MODEL), lambda i: (rev(i), 0)),
            pl.BlockSpec((tm, 2 * D_RNN), lambda i: (rev(i), 0)),
            pl.BlockSpec((tm, D_RNN), lambda i: (rev(i), 0)),
            pl.BlockSpec((N_GATE_PLANES, tm, D_RNN), lambda i: (0, rev(i), 0)),
            pl.BlockSpec((SUBLANES, 2 * D_RNN), lambda i: (prev8(i), 0)),
            pl.BlockSpec((SUBLANES, D_RNN), lambda i: (prev8(i), 0)),
            convw.spec(),
            pl.BlockSpec((D_RNN // GATE_CHUNK, GATE_CHUNK, 2 * GATE_CHUNK), lambda i: (0, 0, 0)),
            apar.spec(),
            pl.BlockSpec((None, D_RNN, D_MODEL), lambda i: (layer, 0, 0)),
        ] + _any_specs(len(cargo.operands)),
        out_specs=[
            pl.BlockSpec((tm, 2 * D_RNN), lambda i: (rev(i), 0)),
            pl.BlockSpec((tm, 2 * D_RNN), lambda i: (rev(i), 0)),
            pl.BlockSpec((tm, D_RNN), lambda i: (rev(i), 0)),
            pl.BlockSpec((tm, D_RNN), lambda i: (rev(i), 0)),
            pl.BlockSpec((CONV_WIDTH, SUBLANES, D_RNN), lambda i: (0, 0, 0)),
            pl.BlockSpec((4, SUBLANES, D_RNN), lambda i: (0, 0, 0)),
        ] + _any_specs(len(cargo.out_shapes)),
        out_shape=[
            jax.ShapeDtypeStruct((S, 2 * D_RNN), BF16),
            jax.ShapeDtypeStruct((S, 2 * D_RNN), BF16),
            jax.ShapeDtypeStruct((S, D_RNN), BF16),
            jax.ShapeDtypeStruct((S, D_RNN), BF16),
            jax.ShapeDtypeStruct((CONV_WIDTH, SUBLANES, D_RNN), F32),
            jax.ShapeDtypeStruct((4, SUBLANES, D_RNN), F32),
        ] + cargo.out_shapes,
        input_output_aliases={n_in + i: n_out + o for i, o in cargo.aliases.items()},
        scratch_shapes=[pltpu.VMEM((SUBLANES, D_RNN), F32)] * 3 + cargo.sem_shapes,
        compiler_params=_cparams(("arbitrary",)),
    )(dout, z, hs, gates, z, hs, convw.array, wbd, apar.array, wout, *cargo.operands)
    return outs[:n_out], list(outs[n_out:])


def _lru_bwd_in(dz, h, gamma, dres, win, layer, cargo=None):
    S = h.shape[0]
    tm = _tile(S, 512)
    nt = S // tm
    cargo = cargo or _Cargo()
    n_in, n_out = 5, 2

    def body(*refs):
        (dz_ref, h_ref, g_ref, dres_ref, win_ref), c_ins, (dh_ref, dgam_ref), c_outs, _, sems = _cargo_refs(
            refs, cargo, n_in, n_out)

        @pl.when(pl.program_id(0) == 0)
        def _():
            cargo.start(c_ins, c_outs, sems)

        dxn = _dot_nt(dz_ref[:, :RNN_IN_CHUNK], win_ref[0])
        for k in range(1, N_CHIPS):
            dxn = dxn + _dot_nt(dz_ref[:, k * RNN_IN_CHUNK:(k + 1) * RNN_IN_CHUNK], win_ref[k])
        xhat, rstd, _ = _rms(h_ref[...], g_ref[...])
        dhn, dgam = _rms_bwd(xhat, rstd, g_ref[...], dxn)
        dh_ref[...] = dres_ref[...] + dhn

        @pl.when(pl.program_id(0) == 0)
        def _():
            dgam_ref[...] = jnp.zeros_like(dgam_ref)

        dgam_ref[...] += dgam

        @pl.when(pl.program_id(0) == nt - 1)
        def _():
            cargo.finish(c_ins, c_outs, sems)

    outs = pl.pallas_call(
        body, name="lru_bwd_in",
        grid=(nt,),
        in_specs=[
            pl.BlockSpec((tm, 2 * D_RNN), lambda t: (t, 0)),
            pl.BlockSpec((tm, D_MODEL), lambda t: (t, 0)),
            gamma.spec(),
            pl.BlockSpec((tm, D_MODEL), lambda t: (t, 0)),
            pl.BlockSpec((None, N_CHIPS, D_MODEL, RNN_IN_CHUNK), lambda t: (layer, 0, 0, 0)),
        ] + _any_specs(len(cargo.operands)),
        out_specs=[
            pl.BlockSpec((tm, D_MODEL), lambda t: (t, 0)),
            pl.BlockSpec((SUBLANES, D_MODEL), lambda t: (0, 0)),
        ] + _any_specs(len(cargo.out_shapes)),
        out_shape=[jax.ShapeDtypeStruct((S, D_MODEL), F32),
                   jax.ShapeDtypeStruct((SUBLANES, D_MODEL), F32)] + cargo.out_shapes,
        input_output_aliases={n_in + i: n_out + o for i, o in cargo.aliases.items()},
        scratch_shapes=list(cargo.sem_shapes),
        compiler_params=_cparams(("arbitrary",)),
    )(dz, h, gamma.array, dres, win, *cargo.operands)
    return outs[:n_out], list(outs[n_out:])


def _pool_inv_count(t_index, tm):
    rows = (lax.broadcasted_iota(jnp.int32, (tm, D_MODEL), 0) + t_index * tm + 1).astype(F32)
    col = lax.broadcasted_iota(jnp.int32, (tm, D_MODEL), 1)
    win = jnp.where(col < POOL_GROUP_DIM, float(POOL_WINDOWS[0]),
                    jnp.where(col < 2 * POOL_GROUP_DIM, float(POOL_WINDOWS[1]),
                              jnp.where(col < 3 * POOL_GROUP_DIM, float(POOL_WINDOWS[2]), float(POOL_WINDOWS[3]))))
    return 1.0 / jnp.minimum(rows, win)


def _window_sums(ext, shift, take):
    gd = POOL_GROUP_DIM
    s2 = ext + shift(ext, 1)
    s4 = s2[:, gd:] + shift(s2[:, gd:], 2)
    s8 = s4[:, gd:] + shift(s4[:, gd:], 4)
    s16 = s8[:, gd:] + shift(s8[:, gd:], 8)
    return jnp.concatenate([take(s2[:, :gd]), take(s4[:, :gd]), take(s8[:, :gd]), take(s16)], axis=1)


def _pool_fwd(h, gamma, pw, pb, pscale, layer):
    S = h.shape[0]
    tm = _tile(S, 512)

    def body(h_ref, g_ref, pw_ref, pb_ref, ps_ref, ho_ref, u_ref, tail_s):
        t = pl.program_id(0)

        @pl.when(t == 0)
        def _():
            tail_s[...] = jnp.zeros_like(tail_s)

        hf = h_ref[...]
        _, _, hn = _rms(hf, g_ref[...])
        ext = jnp.concatenate([tail_s[...], hn], axis=0)
        tail_s[...] = hn[tm - POOL_HALO:, :]
        sums = _window_sums(ext, lambda v, j: pltpu.roll(v, j, 0), lambda v: v[POOL_HALO:])
        ub = (sums * _pool_inv_count(t, tm) - hn).astype(BF16)
        u_ref[...] = ub
        ys = [_dot(ub[:, g * POOL_GROUP_DIM:(g + 1) * POOL_GROUP_DIM], pw_ref[g]) for g in range(len(POOL_WINDOWS))]
        y = jnp.concatenate(ys, axis=1)
        ho_ref[...] = hf + (y + pb_ref[...]) * ps_ref[...]

    return pl.pallas_call(
        body, name="pool_fwd",
        grid=(S // tm,),
        in_specs=[
            pl.BlockSpec((tm, D_MODEL), lambda t: (t, 0)), gamma.spec(),
            pl.BlockSpec((None, len(POOL_WINDOWS), POOL_GROUP_DIM, POOL_GROUP_DIM), lambda t: (layer, 0, 0, 0)),
            pb.spec(), pscale.spec(),
        ],
        out_specs=[pl.BlockSpec((tm, D_MODEL), lambda t: (t, 0)), pl.BlockSpec((tm, D_MODEL), lambda t: (t, 0))],
        out_shape=[jax.ShapeDtypeStruct((S, D_MODEL), F32), jax.ShapeDtypeStruct((S, D_MODEL), BF16)],
        scratch_shapes=[pltpu.VMEM((POOL_HALO, D_MODEL), F32)],
        compiler_params=_cparams(("arbitrary",)),
    )(h, gamma.array, pw, pb.array, pscale.array)


def _pool_bwd(dout, h, u, gamma, pw, pb, pscale, layer):
    S = h.shape[0]
    tm = _tile(S, 512)
    nt = S // tm
    rev = lambda i: nt - 1 - i
    ngroup = len(POOL_WINDOWS)

    def body(do_ref, h_ref, u_ref, g_ref, pw_ref, pb_ref, ps_ref, dh_ref, dpre_ref, vec_ref, head_s):
        i = pl.program_id(0)

        @pl.when(i == 0)
        def _():
            head_s[...] = jnp.zeros_like(head_s)
            vec_ref[...] = jnp.zeros_like(vec_ref)

        do = do_ref[...]
        ub = u_ref[...]
        gsl = lambda v, g: v[:, g * POOL_GROUP_DIM:(g + 1) * POOL_GROUP_DIM]
        y = jnp.concatenate([_dot(gsl(ub, g), pw_ref[g]) for g in range(ngroup)], axis=1)
        dpre = do * ps_ref[...]
        dpb = dpre.astype(BF16)
        dpre_ref[...] = dpb
        du = jnp.concatenate([_dot_nt(gsl(dpb, g), pw_ref[g]) for g in range(ngroup)], axis=1)
        v = du * _pool_inv_count(rev(i), tm)
        ext = jnp.concatenate([v, head_s[...]], axis=0)
        head_s[...] = v[:POOL_HALO, :]
        n = tm + POOL_HALO
        dhn = _window_sums(ext, lambda w, j: pltpu.roll(w, n - j, 0), lambda w: w[:tm]) - du
        xhat, rstd, _ = _rms(h_ref[...], g_ref[...])
        dh_in, dgam = _rms_bwd(xhat, rstd, g_ref[...], dhn)
        dh_ref[...] = do + dh_in
        vec_ref[0] += dgam
        vec_ref[1] += _rowsum8(dpre)
        vec_ref[2] += _rowsum8(do * (y + pb_ref[...]))

    tile = pl.BlockSpec((tm, D_MODEL), lambda i: (rev(i), 0))
    return pl.pallas_call(
        body, name="pool_bwd",
        grid=(nt,),
        in_specs=[tile, tile, tile, gamma.spec(),
                  pl.BlockSpec((None, ngroup, POOL_GROUP_DIM, POOL_GROUP_DIM), lambda i: (layer, 0, 0, 0)),
                  pb.spec(), pscale.spec()],
        out_specs=[tile, tile, pl.BlockSpec((3, SUBLANES, D_MODEL), lambda i: (0, 0, 0))],
        out_shape=[jax.ShapeDtypeStruct((S, D_MODEL), F32), jax.ShapeDtypeStruct((S, D_MODEL), BF16),
                   jax.ShapeDtypeStruct((3, SUBLANES, D_MODEL), F32)],
        scratch_shapes=[pltpu.VMEM((POOL_HALO, D_MODEL), F32)],
        compiler_params=_cparams(("arbitrary",)),
    )(dout, h, u, gamma.array, pw, pb.array, pscale.array)


def _ple_parts(hf, gamma, p_tile, wgate_ref, wproj_ref):
    xhat, rstd, xn = _rms(hf, gamma)
    xnb = xn.astype(BF16)
    gate = _sigmoid(_dot(xnb, wgate_ref[...]))
    pb = p_tile.astype(BF16)
    proj = jnp.concatenate([_dot(pb, wproj_ref[k]) for k in range(N_CHIPS)], axis=1)
    return xhat, rstd, xnb, pb, gate, proj


def _ple_fwd(h, gamma, p_l, wgate, wproj, layer):
    S = h.shape[0]
    tm = _tile(S, 1024)

    def body(h_ref, g_ref, p_ref, wgate_ref, wproj_ref, ho_ref):
        hf = h_ref[...]
        _, _, _, _, gate, proj = _ple_parts(hf, g_ref[...], p_ref[...], wgate_ref, wproj_ref)
        ho_ref[...] = hf + gate * proj

    return pl.pallas_call(
        body, name="ple_fwd",
        grid=(S // tm,),
        in_specs=[
            pl.BlockSpec((tm, D_MODEL), lambda t: (t, 0)),
            gamma.spec(),
            pl.BlockSpec((None, None, tm, PLE_DIM), lambda t: (layer, 0, t, 0)),
            pl.BlockSpec((None, D_MODEL, D_MODEL), lambda t: (layer, 0, 0)),
            pl.BlockSpec((None, N_CHIPS, PLE_DIM, PLE_DIM), lambda t: (layer, 0, 0, 0)),
        ],
        out_specs=pl.BlockSpec((tm, D_MODEL), lambda t: (t, 0)),
        out_shape=jax.ShapeDtypeStruct((S, D_MODEL), F32),
        compiler_params=_cparams(("arbitrary",)),
    )(h, gamma.array, p_l, wgate, wproj)


def _ple_bwd(dout, h, gamma, p_l, wgate, wproj, layer):
    S = h.shape[0]
    tm = _tile(S, 512)

    def body(do_ref, h_ref, g_ref, p_ref, wgate_ref, wproj_ref, dh_ref, dwg_ref, dwp_ref, dgam_ref):
        @pl.when(pl.program_id(0) == 0)
        def _():
            dgam_ref[...] = jnp.zeros_like(dgam_ref)
            dwg_ref[...] = jnp.zeros_like(dwg_ref)
            dwp_ref[...] = jnp.zeros_like(dwp_ref)

        do = do_ref[...]
        xhat, rstd, xnb, pb, gate, proj = _ple_parts(h_ref[...], g_ref[...], p_ref[...], wgate_ref, wproj_ref)
        dproj = (do * gate).astype(BF16)
        dpre = (do * proj * gate * (1.0 - gate)).astype(BF16)
        dwg_ref[...] += _dot_tn(xnb, dpre)
        for k in range(N_CHIPS):
            dwp_ref[k] += _dot_tn(pb, dproj[:, k * PLE_DIM:(k + 1) * PLE_DIM])
        dhn, dgam = _rms_bwd(xhat, rstd, g_ref[...], _dot_nt(dpre, wgate_ref[...]))
        dh_ref[...] = do + dhn
        dgam_ref[...] += dgam

    tile = pl.BlockSpec((tm, D_MODEL), lambda t: (t, 0))
    return pl.pallas_call(
        body, name="ple_bwd",
        grid=(S // tm,),
        in_specs=[
            tile, tile,
            gamma.spec(),
            pl.BlockSpec((None, None, tm, PLE_DIM), lambda t: (layer, 0, t, 0)),
            pl.BlockSpec((None, D_MODEL, D_MODEL), lambda t: (layer, 0, 0)),
            pl.BlockSpec((None, N_CHIPS, PLE_DIM, PLE_DIM), lambda t: (layer, 0, 0, 0)),
        ],
        out_specs=[tile, pl.BlockSpec((D_MODEL, D_MODEL), lambda t: (0, 0)),
                   pl.BlockSpec((N_CHIPS, PLE_DIM, PLE_DIM), lambda t: (0, 0, 0)),
                   pl.BlockSpec((SUBLANES, D_MODEL), lambda t: (0, 0))],
        out_shape=[jax.ShapeDtypeStruct((S, D_MODEL), F32), jax.ShapeDtypeStruct((D_MODEL, D_MODEL), F32),
                   jax.ShapeDtypeStruct((N_CHIPS, PLE_DIM, PLE_DIM), F32),
                   jax.ShapeDtypeStruct((SUBLANES, D_MODEL), F32)],
        compiler_params=_cparams(("arbitrary",)),
    )(dout, h, gamma.array, p_l, wgate, wproj)


def _final(h, gamma, target):
    S = h.shape[0]
    tm = _tile(S, 1024)

    def body(h_ref, g_ref, tgt_ref, dh_ref, dgam_ref, loss_ref):
        xhat, rstd, y = _rms(h_ref[...], g_ref[...])
        err = y - tgt_ref[...]
        dy = err * (1.0 / D_MODEL)
        dhn, dgam = _rms_bwd(xhat, rstd, g_ref[...], dy)
        dh_ref[...] = dhn
        sq = _rowsum8(err * err)
        part = sq[:, :LANES]
        for j in range(1, D_MODEL // LANES):
            part = part + sq[:, j * LANES:(j + 1) * LANES]

        @pl.when(pl.program_id(0) == 0)
        def _():
            dgam_ref[...] = jnp.zeros_like(dgam_ref)
            loss_ref[...] = jnp.zeros_like(loss_ref)

        dgam_ref[...] += dgam
        loss_ref[...] += part * (0.5 / D_MODEL)

    tile = pl.BlockSpec((tm, D_MODEL), lambda t: (t, 0))
    return pl.pallas_call(
        body, name="final_loss",
        grid=(S // tm,),
        in_specs=[tile, gamma.spec(), tile],
        out_specs=[tile, pl.BlockSpec((SUBLANES, D_MODEL), lambda t: (0, 0)),
                   pl.BlockSpec((SUBLANES, LANES), lambda t: (0, 0))],
        out_shape=[jax.ShapeDtypeStruct((S, D_MODEL), F32), jax.ShapeDtypeStruct((SUBLANES, D_MODEL), F32),
                   jax.ShapeDtypeStruct((SUBLANES, LANES), F32)],
        compiler_params=_cparams(("arbitrary",)),
    )(h, gamma.array, target)


def _mesh_pos():
    return lax.axis_index("x"), lax.axis_index("y"), lax.axis_index("c")


def _other_chip(x, y, j):
    fx, fy = CHIP_FLIPS[j]
    return (1 - x if fx else x), (1 - y if fy else y)


def _any_specs(n):
    return [pl.BlockSpec(memory_space=pl.ANY)] * n


class _Cargo:
    def __init__(self):
        self.operands, self.out_shapes, self.aliases, self.sem_shapes, self.names = [], [], {}, [], []
        self.start = lambda ins, outs, sems: None
        self.forward = lambda ins, outs, sems: None
        self.finish = lambda ins, outs, sems: None


def _cargo_refs(refs, cargo, n_in, n_out, n_scratch=0):
    a = n_in
    b = a + len(cargo.operands)
    c = b + n_out
    d = c + len(cargo.out_shapes)
    e = d + n_scratch
    return refs[:a], refs[a:b], refs[b:c], refs[c:d], refs[d:e], refs[e:]


def _remote(src, dst, send, recv, device):
    return pltpu.make_async_remote_copy(src_ref=src, dst_ref=dst, send_sem=send, recv_sem=recv,
                                        device_id=device, device_id_type=MESH_ID)


def _gather_cargo(bufs, pieces):
    cargo = _Cargo()
    if not pieces:
        return cargo
    plist = []
    for name, layer in pieces:
        if name not in cargo.names:
            cargo.names.append(name)
            cargo.operands.append(bufs[name])
        plist.append((cargo.names.index(name), layer, bufs[name].shape[2] // 2))
    nflip = len(CHIP_FLIPS)
    cargo.out_shapes = [jax.ShapeDtypeStruct(b.shape, b.dtype) for b in cargo.operands]
    cargo.aliases = {i: i for i in range(len(cargo.operands))}
    cargo.sem_shapes = [pltpu.SemaphoreType.DMA((len(plist) * nflip,))] * 4

    def copies(outs, sems):
        send1, recv1, send2, recv2 = sems
        x, y, c = _mesh_pos()
        k = 2 * x + y

        def blk(p, chip, cc):
            b, layer, hrows = plist[p]
            return outs[b].at[layer, chip, pl.ds(cc * hrows, hrows), :]

        def chip_of(j):
            px, py = _other_chip(x, y, j)
            return 2 * px + py

        def ici(p, j):
            px, py = _other_chip(x, y, j)
            return _remote(blk(p, k, c), blk(p, k, c), send1.at[p * nflip + j], recv1.at[p * nflip + j], (px, py, c))

        def landed(p, j):
            px, py = _other_chip(x, y, j)
            return _remote(blk(p, k, c), blk(p, chip_of(j), c), send1.at[p * nflip + j], recv1.at[p * nflip + j],
                           (px, py, c))

        def d2d(p, j, cc):
            return _remote(blk(p, chip_of(j), cc), blk(p, chip_of(j), cc), send2.at[p * nflip + j],
                           recv2.at[p * nflip + j], (x, y, 1 - c))

        return c, ici, landed, d2d

    def start(ins, outs, sems):
        _, ici, _, _ = copies(outs, sems)
        for p in range(len(plist)):
            for j in range(nflip):
                ici(p, j).start()

    def forward(ins, outs, sems):
        c, _, landed, d2d = copies(outs, sems)
        for j in range(nflip):
            for p in range(len(plist)):
                landed(p, j).wait_recv()
                d2d(p, j, c).start()

    def finish(ins, outs, sems):
        c, ici, _, d2d = copies(outs, sems)
        for p in range(len(plist)):
            for j in range(nflip):
                ici(p, j).wait_send()
                d2d(p, j, c).wait_send()
                d2d(p, j, 1 - c).wait_recv()

    cargo.start, cargo.forward, cargo.finish = start, forward, finish
    return cargo


def _reduce_cargo(grads, presums):
    cargo = _Cargo()
    na, nb = len(grads), len(presums)
    nflip = len(CHIP_FLIPS)
    cargo.operands = list(grads) + list(presums)
    cargo.out_shapes = ([jax.ShapeDtypeStruct((g.shape[0], g.shape[1] // 2, g.shape[2]), g.dtype) for g in grads]
                        + [jax.ShapeDtypeStruct((nflip,) + ps.shape[1:], ps.dtype) for ps in presums])
    cargo.sem_shapes = ([pltpu.SemaphoreType.DMA((na,))] * 2 if na else []) + (
        [pltpu.SemaphoreType.DMA((nb * nflip,))] * 2 if nb else [])

    def copies(ins, outs, sems):
        x, y, c = _mesh_pos()
        out = []
        if na:
            send, recv = sems[0], sems[1]
            for a in range(na):
                hrows = grads[a].shape[1] // 2
                out.append(_remote(ins[a].at[:, pl.ds((1 - c) * hrows, hrows), :], outs[a], send.at[a], recv.at[a],
                                   (x, y, 1 - c)))
        if nb:
            send, recv = sems[-2], sems[-1]
            for b in range(nb):
                for j in range(nflip):
                    px, py = _other_chip(x, y, j)
                    out.append(_remote(ins[na + b].at[2 * px + py], outs[na + b].at[j], send.at[b * nflip + j],
                                       recv.at[b * nflip + j], (px, py, c)))
        return out

    def start(ins, outs, sems):
        for cp in copies(ins, outs, sems):
            cp.start()

    def finish(ins, outs, sems):
        for cp in copies(ins, outs, sems):
            cp.wait()

    cargo.start, cargo.finish = start, finish
    return cargo


def _run_cargo(name, cargo):
    nin, nout = len(cargo.operands), len(cargo.out_shapes)

    def body(*refs):
        ins, outs, sems = refs[:nin], refs[nin:nin + nout], refs[nin + nout:]
        cargo.start(ins, outs, sems)
        cargo.forward(ins, outs, sems)
        cargo.finish(ins, outs, sems)

    return list(pl.pallas_call(
        body, name=name,
        in_specs=_any_specs(nin), out_specs=_any_specs(nout), out_shape=cargo.out_shapes,
        input_output_aliases=dict(cargo.aliases), scratch_shapes=list(cargo.sem_shapes),
    )(*cargo.operands))


def _join_siblings(bufs):
    nb = len(bufs)
    items = [(b, layer) for b, buf in enumerate(bufs) for layer in range(buf.shape[0])]

    def body(*refs):
        outs = refs[nb:2 * nb]
        send, recv = refs[2 * nb:]
        x, y, c = _mesh_pos()

        def half(i, cc):
            b, layer = items[i]
            hrows = bufs[b].shape[1] // 2
            blk = outs[b].at[layer, pl.ds(cc * hrows, hrows), :]
            return _remote(blk, blk, send.at[i], recv.at[i], (x, y, 1 - c))

        for i in range(len(items)):
            half(i, c).start()
        for i in range(len(items)):
            half(i, c).wait_send()
            half(i, 1 - c).wait_recv()

    return list(pl.pallas_call(
        body, name="grad_sibling_join",
        in_specs=_any_specs(nb), out_specs=_any_specs(nb),
        out_shape=[jax.ShapeDtypeStruct(b.shape, b.dtype) for b in bufs],
        input_output_aliases={i: i for i in range(nb)},
        scratch_shapes=[pltpu.SemaphoreType.DMA((len(items),))] * 2,
    )(*bufs))


def _cast_place(w3, pos, dtype):
    L, rows, cols = w3.shape

    def body(pos_ref, w_ref, o_ref):
        o_ref[...] = w_ref[...].astype(dtype)

    return pl.pallas_call(
        body, name="cast_place",
        grid_spec=pltpu.PrefetchScalarGridSpec(
            num_scalar_prefetch=1, grid=(L,),
            in_specs=[pl.BlockSpec((None, rows, cols), lambda l, pos: (l, 0, 0))],
            out_specs=pl.BlockSpec((None, None, rows, cols), lambda l, pos: (l, pos[0], 0, 0))),
        out_shape=jax.ShapeDtypeStruct((L, N_CHIPS, rows, cols), dtype),
        compiler_params=_cparams(("arbitrary",)),
    )(pos, w3)


def _allreduce_small(buf, cargo=None):
    R = buf.shape[0]
    half = R // 2
    assert half % SUBLANES == 0, R
    cargo = cargo or _Cargo()

    def body(*refs):
        (in_ref,), c_ins, (out_ref,), c_outs, (land, send, recv), sems = _cargo_refs(refs, cargo, 1, 1, n_scratch=3)
        cargo.start(c_ins, c_outs, sems)
        x, y, c = _mesh_pos()
        out_ref[...] = in_ref[...]
        cp = _remote(out_ref, land.at[0], send.at[0], recv.at[0], (x, y, 1 - c))
        cp.start()
        cp.wait()
        out_ref[...] = out_ref[...] + land[0]
        along_y, along_x = (x, 1 - y, c), (1 - x, y, c)
        lo, hi = pl.ds(0, half), pl.ds(half, half)
        for stage, (peer_lo, peer_hi) in enumerate(((along_y, along_x), (along_x, along_y))):
            slot = 1 + stage
            cps = [_remote(out_ref.at[lo], land.at[slot, lo], send.at[1 + 2 * stage], recv.at[1 + 2 * stage], peer_lo),
                   _remote(out_ref.at[hi], land.at[slot, hi], send.at[2 + 2 * stage], recv.at[2 + 2 * stage], peer_hi)]
            for cp in cps:
                cp.start()
            for cp in cps:
                cp.wait()
            out_ref[...] = out_ref[...] + land[slot]
        cargo.finish(c_ins, c_outs, sems)

    outs = pl.pallas_call(
        body, name="allreduce_small",
        in_specs=[pl.BlockSpec(memory_space=pltpu.VMEM)] + _any_specs(len(cargo.operands)),
        out_specs=[pl.BlockSpec(memory_space=pltpu.VMEM)] + _any_specs(len(cargo.out_shapes)),
        out_shape=[jax.ShapeDtypeStruct((R, LANES), F32)] + cargo.out_shapes,
        input_output_aliases={1 + i: 1 + o for i, o in cargo.aliases.items()},
        scratch_shapes=[pltpu.VMEM((3, R, LANES), F32), pltpu.SemaphoreType.DMA((5,)),
                        pltpu.SemaphoreType.DMA((5,))] + cargo.sem_shapes,
        compiler_params=pltpu.CompilerParams(vmem_limit_bytes=VMEM_LIMIT_MB * 2 ** 20),
    )(buf, *cargo.operands)
    return outs[0], list(outs[1:])


def _presum_with_sibling(grad, landed, pos):
    nchunk, rows, cols = grad.shape
    hrows = rows // 2

    def body(pos_ref, g_ref, l_ref, all_ref, own_ref):
        s = g_ref[...] + l_ref[...]
        all_ref[...] = s.astype(BF16)

        @pl.when(pl.program_id(0) == pos_ref[0])
        def _():
            own_ref[...] = s

    return pl.pallas_call(
        body, name="grad_presum",
        grid_spec=pltpu.PrefetchScalarGridSpec(
            num_scalar_prefetch=1, grid=(nchunk,),
            in_specs=[pl.BlockSpec((None, hrows, cols), lambda k, pos: (k, pos[1], 0)),
                      pl.BlockSpec((None, hrows, cols), lambda k, pos: (k, 0, 0))],
            out_specs=[pl.BlockSpec((None, hrows, cols), lambda k, pos: (k, 0, 0)),
                       pl.BlockSpec((hrows, cols), lambda k, pos: (0, 0))]),
        out_shape=[jax.ShapeDtypeStruct((nchunk, hrows, cols), BF16), jax.ShapeDtypeStruct((hrows, cols), F32)],
        compiler_params=_cparams(("arbitrary",)),
    )(pos, grad, landed)


def _sum_chips(own, landed, stacked, layer, shape3, pos):
    hrows, cols = own.shape

    def body(pos_ref, o_ref, l_ref, *rest):
        s = o_ref[...]
        for j in range(len(CHIP_FLIPS)):
            s = s + l_ref[j].astype(F32)
        rest[-1][...] = s

    in_specs = [pl.BlockSpec((hrows, cols), lambda i, pos: (0, 0)),
                pl.BlockSpec((len(CHIP_FLIPS), hrows, cols), lambda i, pos: (0, 0, 0))]
    args = [pos, own, landed]
    aliases = {}
    if stacked is not None:
        in_specs.append(pl.BlockSpec(memory_space=pl.ANY))
        args.append(stacked)
        aliases = {3: 0}
    return pl.pallas_call(
        body, name="grad_sum_chips",
        grid_spec=pltpu.PrefetchScalarGridSpec(
            num_scalar_prefetch=1, grid=(1,), in_specs=in_specs,
            out_specs=pl.BlockSpec((None, hrows, cols), lambda i, pos: (layer, pos[1], 0))),
        out_shape=jax.ShapeDtypeStruct(shape3, F32),
        input_output_aliases=aliases,
        compiler_params=_cparams(("arbitrary",)),
    )(*args)


def _adamw(w, g, m, v):
    R, C = w.shape
    rb = R
    for cand in (512, 352, 320, 256, 128, 64, 32, 16, 8):
        if R % cand == 0:
            rb = cand
            break
    c1 = 1.0 - ADAM_B1 ** ADAM_STEP
    c2 = 1.0 - ADAM_B2 ** ADAM_STEP

    def body(w_ref, g_ref, m_ref, v_ref, go_ref, d_ref, mo_ref, vo_ref):
        gv = g_ref[...]
        go_ref[...] = gv
        m2 = ADAM_B1 * m_ref[...] + (1.0 - ADAM_B1) * gv
        v2 = ADAM_B2 * v_ref[...] + (1.0 - ADAM_B2) * (gv * gv)
        mo_ref[...] = m2
        vo_ref[...] = v2
        d_ref[...] = -ADAM_LR * ((m2 / c1) / (jnp.sqrt(v2 / c2) + ADAM_EPS) + ADAM_WD * w_ref[...])

    spec = pl.BlockSpec((rb, C), lambda i: (i, 0))
    return pl.pallas_call(
        body, name="adamw",
        grid=(R // rb,),
        in_specs=[spec] * 4, out_specs=[spec] * 4,
        out_shape=[jax.ShapeDtypeStruct((R, C), F32)] * 4,
        compiler_params=_cparams(("arbitrary",)),
    )(w, g, m, v)


def _pack(parts, align=SUBLANES * LANES):
    flat = jnp.concatenate([p.reshape(-1).astype(F32) for p in parts])
    pad = (-flat.shape[0]) % align
    return jnp.pad(flat, (0, pad)).reshape(-1, LANES)


def _unpack(buf, shapes):
    flat = buf.reshape(-1)
    out, off = [], 0
    for shp in shapes:
        size = 1
        for d in shp:
            size *= d
        out.append(flat[off:off + size].reshape(shp))
        off += size
    return out


def _block_diag_gates(w_a, w_x):
    nq = D_RNN // GATE_CHUNK
    hpc = LRU_HEADS // nq
    eye = jnp.eye(hpc, dtype=F32)

    def bd(w):
        wq = w.reshape(nq, hpc, LRU_HEAD_DIM, LRU_HEAD_DIM)
        return (wq[:, :, :, None, :] * eye[None, :, None, :, None]).reshape(nq, GATE_CHUNK, GATE_CHUNK)

    return jnp.concatenate([bd(w_a), bd(w_x)], axis=2).astype(BF16)


def _block_diag_extract(dwbd):
    nq = D_RNN // GATE_CHUNK
    hpc = LRU_HEADS // nq
    eye = jnp.eye(hpc, dtype=F32)

    def ex(d):
        d5 = d.reshape(nq, hpc, LRU_HEAD_DIM, hpc, LRU_HEAD_DIM)
        return jnp.sum(d5 * eye[None, :, None, :, None], axis=3).reshape(LRU_HEADS, LRU_HEAD_DIM, LRU_HEAD_DIM)

    return ex(dwbd[:, :, :GATE_CHUNK]), ex(dwbd[:, :, GATE_CHUNK:])


BIG = ("ffn1_w_gate", "ffn1_w_up", "ffn1_w_down", "lru_w_in", "lru_w_out", "pool_w",
       "ffn2_w_gate", "ffn2_w_up", "ffn2_w_down", "ple_w_gate", "ple_w_proj")
TINY_SHARDED = ("lru_conv_w", "pool_b", "pool_scale")
REPLICATED = ("ffn1_norm", "mix_norm", "lru_conv_b", "lru_w_a", "lru_b_a", "lru_w_x", "lru_b_x", "lru_a_param",
              "ffn2_norm", "ple_norm", "final_norm")
WEIGHT_ORDER = ("ffn1_norm", "ffn1_w_gate", "ffn1_w_up", "ffn1_w_down", "mix_norm", "lru_w_in", "lru_conv_w",
                "lru_conv_b", "lru_w_a", "lru_b_a", "lru_w_x", "lru_b_x", "lru_a_param", "lru_w_out", "pool_w",
                "pool_b", "pool_scale", "ffn2_norm", "ffn2_w_gate", "ffn2_w_up", "ffn2_w_down", "ple_norm",
                "ple_w_gate", "ple_w_proj", "final_norm")


TRANSPOSED = ("ffn1_w_gate", "ffn1_w_up", "ffn2_w_gate", "ffn2_w_up")


def _stored(name, a):
    return jnp.swapaxes(a, 1, 2) if name in TRANSPOSED else a


def _as3(a):
    return a.reshape(a.shape[0], -1, a.shape[-1])


def kernel(x, p, ffn1_norm, ffn1_w_gate, ffn1_w_up, ffn1_w_down, mix_norm, lru_w_in, lru_conv_w, lru_conv_b, lru_w_a, lru_b_a, lru_w_x, lru_b_x, lru_a_param, lru_w_out, pool_w, pool_b, pool_scale, ffn2_norm, ffn2_w_gate, ffn2_w_up, ffn2_w_down, ple_norm, ple_w_gate, ple_w_proj, final_norm, loss_target, m_ffn1_norm, m_ffn1_w_gate, m_ffn1_w_up, m_ffn1_w_down, m_mix_norm, m_lru_w_in, m_lru_conv_w, m_lru_conv_b, m_lru_w_a, m_lru_b_a, m_lru_w_x, m_lru_b_x, m_lru_a_param, m_lru_w_out, m_pool_w, m_pool_b, m_pool_scale, m_ffn2_norm, m_ffn2_w_gate, m_ffn2_w_up, m_ffn2_w_down, m_ple_norm, m_ple_w_gate, m_ple_w_proj, m_final_norm, v_ffn1_norm, v_ffn1_w_gate, v_ffn1_w_up, v_ffn1_w_down, v_mix_norm, v_lru_w_in, v_lru_conv_w, v_lru_conv_b, v_lru_w_a, v_lru_b_a, v_lru_w_x, v_lru_b_x, v_lru_a_param, v_lru_w_out, v_pool_w, v_pool_b, v_pool_scale, v_ffn2_norm, v_ffn2_w_gate, v_ffn2_w_up, v_ffn2_w_down, v_ple_norm, v_ple_w_gate, v_ple_w_proj, v_final_norm):
    W = dict(ffn1_norm=ffn1_norm, ffn1_w_gate=ffn1_w_gate, ffn1_w_up=ffn1_w_up, ffn1_w_down=ffn1_w_down,
             mix_norm=mix_norm, lru_w_in=lru_w_in, lru_conv_w=lru_conv_w, lru_conv_b=lru_conv_b, lru_w_a=lru_w_a,
             lru_b_a=lru_b_a, lru_w_x=lru_w_x, lru_b_x=lru_b_x, lru_a_param=lru_a_param, lru_w_out=lru_w_out,
             pool_w=pool_w, pool_b=pool_b, pool_scale=pool_scale, ffn2_norm=ffn2_norm, ffn2_w_gate=ffn2_w_gate,
             ffn2_w_up=ffn2_w_up, ffn2_w_down=ffn2_w_down, ple_norm=ple_norm, ple_w_gate=ple_w_gate,
             ple_w_proj=ple_w_proj, final_norm=final_norm)
    M = dict(ffn1_norm=m_ffn1_norm, ffn1_w_gate=m_ffn1_w_gate, ffn1_w_up=m_ffn1_w_up, ffn1_w_down=m_ffn1_w_down,
             mix_norm=m_mix_norm, lru_w_in=m_lru_w_in, lru_conv_w=m_lru_conv_w, lru_conv_b=m_lru_conv_b,
             lru_w_a=m_lru_w_a, lru_b_a=m_lru_b_a, lru_w_x=m_lru_w_x, lru_b_x=m_lru_b_x, lru_a_param=m_lru_a_param,
             lru_w_out=m_lru_w_out, pool_w=m_pool_w, pool_b=m_pool_b, pool_scale=m_pool_scale, ffn2_norm=m_ffn2_norm,
             ffn2_w_gate=m_ffn2_w_gate, ffn2_w_up=m_ffn2_w_up, ffn2_w_down=m_ffn2_w_down, ple_norm=m_ple_norm,
             ple_w_gate=m_ple_w_gate, ple_w_proj=m_ple_w_proj, final_norm=m_final_norm)
    V = dict(ffn1_norm=v_ffn1_norm, ffn1_w_gate=v_ffn1_w_gate, ffn1_w_up=v_ffn1_w_up, ffn1_w_down=v_ffn1_w_down,
             mix_norm=v_mix_norm, lru_w_in=v_lru_w_in, lru_conv_w=v_lru_conv_w, lru_conv_b=v_lru_conv_b,
             lru_w_a=v_lru_w_a, lru_b_a=v_lru_b_a, lru_w_x=v_lru_w_x, lru_b_x=v_lru_b_x, lru_a_param=v_lru_a_param,
             lru_w_out=v_lru_w_out, pool_w=v_pool_w, pool_b=v_pool_b, pool_scale=v_pool_scale, ffn2_norm=v_ffn2_norm,
             ffn2_w_gate=v_ffn2_w_gate, ffn2_w_up=v_ffn2_w_up, ffn2_w_down=v_ffn2_w_down, ple_norm=v_ple_norm,
             ple_w_gate=v_ple_w_gate, ple_w_proj=v_ple_w_proj, final_norm=v_final_norm)

    S = x.shape[1]
    my_x, my_y, my_c = _mesh_pos()
    my_chip = 2 * my_x + my_y
    pos = jnp.stack([my_chip, my_c]).astype(jnp.int32)
    n_lru, n_pool = lru_w_in.shape[0], pool_w.shape[0]

    tiny_shapes = [W[n].shape for n in TINY_SHARDED]
    tiny_local = _pack([W[n] for n in TINY_SHARDED], align=2 * 16 * LANES)[None]
    Ws, Ms, Vs = ({n: _stored(n, d[n]) for n in BIG} for d in (W, M, V))
    bufs = {n: _cast_place(_as3(Ws[n]), pos, BF16) for n in BIG}
    bufs["tiny"] = _cast_place(tiny_local, pos, F32)

    def gather_now(name, pieces):
        cargo = _gather_cargo(bufs, pieces)
        bufs.update(zip(cargo.names, _run_cargo(name, cargo)))

    def ffn_pieces(which, layer):
        return [("%s_w_gate" % which, layer), ("%s_w_up" % which, layer), ("%s_w_down" % which, layer)]

    def mixer_pieces(layer):
        if layer % 2 == 0:
            return [("lru_w_in", layer // 2), ("lru_w_out", layer // 2)]
        return [("pool_w", layer // 2)]

    gather_now("gather_first", [("tiny", 0)] + ffn_pieces("ffn1", 0))
    tiny_by_chip = [_unpack(bufs["tiny"][0, k], tiny_shapes) for k in range(N_CHIPS)]
    conv_w_full = jnp.concatenate([tiny_by_chip[k][0] for k in range(N_CHIPS)], axis=-1)
    pool_b_full = jnp.concatenate([tiny_by_chip[k][1] for k in range(N_CHIPS)], axis=-1)
    pool_s_full = jnp.concatenate([tiny_by_chip[k][2] for k in range(N_CHIPS)], axis=-1)
    ngroup = len(POOL_WINDOWS)

    def pool_weights():
        pw5 = bufs["pool_w"].reshape(n_pool, N_CHIPS, ngroup, POOL_GROUP_DIM // N_CHIPS, POOL_GROUP_DIM)
        return pw5.transpose(0, 2, 1, 3, 4).reshape(n_pool, ngroup, POOL_GROUP_DIM, POOL_GROUP_DIM)

    lru_out = lambda: bufs["lru_w_out"].reshape(n_lru, D_RNN, D_MODEL)
    ple_gate = lambda: bufs["ple_w_gate"].reshape(DEPTH, D_MODEL, D_MODEL)
    wbd = [_block_diag_gates(lru_w_a[j], lru_w_x[j]) for j in range(n_lru)]
    ffn1_norm, mix_norm, ffn2_norm, ple_norm = (_rows3(a) for a in (ffn1_norm, mix_norm, ffn2_norm, ple_norm))
    lru_conv_b, lru_b_a, lru_b_x, lru_a_param = (_rows3(a) for a in (lru_conv_b, lru_b_a, lru_b_x, lru_a_param))
    pool_b_full, pool_s_full = _rows3(pool_b_full), _rows3(pool_s_full)

    def ffn_forward(which, h, gamma, layer, pieces):
        cargo = _gather_cargo(bufs, pieces)
        outs, updated = _ffn_fwd(h, gamma, bufs[which + "_w_gate"], bufs[which + "_w_up"], bufs[which + "_w_down"],
                                 layer, cargo)
        bufs.update(zip(cargo.names, updated))
        return outs

    h = x.reshape(S, D_MODEL)
    saved = []
    for i in range(DEPTH):
        j = i // 2
        sv = {"h0": h}
        first_mixer = mixer_pieces(0) if i == 0 else []
        h, sv["xn1"], sv["g1"], sv["u1"] = ffn_forward(
            "ffn1", h, _Rows(ffn1_norm, i), i,
            first_mixer + ffn_pieces("ffn2", i) + [("ple_w_gate", i), ("ple_w_proj", i)])
        sv["h1"] = h
        if i % 2 == 0:
            h, sv["xn_mix"], sv["z"], sv["hs"], sv["gates"] = _lru_fwd(
                h, _Rows(mix_norm, i), bufs["lru_w_in"], j, _Rows(conv_w_full, j), _Rows(lru_conv_b, j), wbd[j],
                _Rows(lru_b_a, j), _Rows(lru_b_x, j), _Rows(lru_a_param, j), lru_out())
        else:
            h, sv["u"] = _pool_fwd(h, _Rows(mix_norm, i), pool_weights(), _Rows(pool_b_full, j), _Rows(pool_s_full, j), j)
        sv["h2"] = h
        nxt = ffn_pieces("ffn1", i + 1) + mixer_pieces(i + 1) if i + 1 < DEPTH else []
        h, sv["xn2"], sv["g2"], sv["u2"] = ffn_forward("ffn2", h, _Rows(ffn2_norm, i), i, nxt)
        sv["h3"] = h
        h = _ple_fwd(h, _Rows(ple_norm, i), p, ple_gate(), bufs["ple_w_proj"], i)
        saved.append(sv)

    dh, dgam_final, loss_part = _final(h, _Rows(final_norm.reshape(1, 1, -1), 0), loss_target.reshape(S, D_MODEL))
    win, wout, wpg, wpp, pw = bufs["lru_w_in"], lru_out(), ple_gate(), bufs["ple_w_proj"], pool_weights()

    norm_grads = {n: [None] * DEPTH for n in ("ffn1_norm", "mix_norm", "ffn2_norm", "ple_norm")}
    lru_vec = [None] * n_lru
    pool_vec = [None] * n_pool
    sum8 = lambda a: jnp.sum(a, axis=-2)

    to_siblings, to_chips = [], []
    stacked = {n: None for n in BIG}

    def take_cargo(with_chips=True):
        a_items, b_items = list(to_siblings), list(to_chips) if with_chips else []
        del to_siblings[:], to_chips[:len(b_items)]
        return _reduce_cargo([it[2] for it in a_items], [it[2] for it in b_items]), a_items, b_items

    def absorb(a_items, b_items, outs):
        for (n, layer, g), landed in zip(a_items, outs[:len(a_items)]):
            all_chunks, own = _presum_with_sibling(g, landed, pos)
            to_chips.append((n, layer, all_chunks, own))
        for (n, layer, _, own), from_chips in zip(b_items, outs[len(a_items):]):
            stacked[n] = _sum_chips(own, from_chips, stacked[n], layer, _as3(Ws[n]).shape, pos)

    def ffn_backward(which, xn, dout, gg, uu, layer, h_in, gamma):
        cargo, a_items, b_items = take_cargo()
        weights = (bufs[which + "_w_gate"], bufs[which + "_w_up"], bufs[which + "_w_down"])
        (dwg, dwu, dwd, slabs), c_outs = _ffn_bwd(xn, dout, gg, uu, *weights, layer, cargo)
        absorb(a_items, b_items, c_outs)
        dh_in, dgam, dwg, dwu, dwd = _ffn_bwd_last(xn, dout, gg, uu, *weights, layer, slabs, h_in, gamma,
                                                   dwg, dwu, dwd)
        to_siblings.extend([(which + "_w_gate", layer, dwg), (which + "_w_up", layer, dwu),
                            (which + "_w_down", layer, dwd)])
        return dh_in, sum8(dgam)

    for i in reversed(range(DEPTH)):
        j = i // 2
        sv = saved[i]
        dh, dw_pg, dw_pp, dgam = _ple_bwd(dh, sv["h3"], _Rows(ple_norm, i), p, wpg, wpp, i)
        norm_grads["ple_norm"][i] = sum8(dgam)
        to_siblings.append(("ple_w_gate", i, dw_pg.reshape(N_CHIPS, D_MODEL // N_CHIPS, D_MODEL)))
        to_siblings.append(("ple_w_proj", i, dw_pp))

        dh, norm_grads["ffn2_norm"][i] = ffn_backward("ffn2", sv["xn2"], dh, sv["g2"], sv["u2"], i, sv["h2"],
                                                      _Rows(ffn2_norm, i))

        if i % 2 == 0:
            cargo, a_items, b_items = take_cargo(with_chips=False)
            (dz, dpre, xc_b, y_b, dcw, vec), c_outs = _lru_bwd_seq(
                dh, sv["z"], sv["hs"], sv["gates"], _Rows(conv_w_full, j), wbd[j], _Rows(lru_a_param, j), wout, j, cargo)
            absorb(a_items, b_items, c_outs)
            to_siblings.append(("lru_w_out", j, _xt_dy("lru_dw_out", y_b, dh, 1, D_RNN, D_MODEL, False, False)
                                .reshape(N_CHIPS, D_RNN // N_CHIPS, D_MODEL)))
            to_siblings.append(("lru_w_in", j, _xt_dy("lru_dw_in", sv["xn_mix"], dz, N_CHIPS, D_MODEL, RNN_IN_CHUNK,
                                                      False, True)))
            dwbd = _xt_dy("lru_dw_gates", xc_b, dpre, D_RNN // GATE_CHUNK, GATE_CHUNK, 2 * GATE_CHUNK, True, True)
            dw_a, dw_x = _block_diag_extract(dwbd)
            vsum = sum8(vec)
            lru_vec[j] = (sum8(dcw), vsum[0], vsum[1], vsum[2], vsum[3], dw_a, dw_x)
            cargo, a_items, b_items = take_cargo(with_chips=False)
            (dh, dgam), c_outs = _lru_bwd_in(dz, sv["h1"], _Rows(mix_norm, i), dh, win, j, cargo)
            absorb(a_items, b_items, c_outs)
            norm_grads["mix_norm"][i] = sum8(dgam)
        else:
            dh_new, dpre_b, vec = _pool_bwd(dh, sv["h1"], sv["u"], _Rows(mix_norm, i), pw, _Rows(pool_b_full, j),
                                            _Rows(pool_s_full, j), j)
            dpw = _xt_dy("pool_dw", sv["u"], dpre_b, ngroup, POOL_GROUP_DIM, POOL_GROUP_DIM, True, True)
            dpw = dpw.reshape(ngroup, N_CHIPS, POOL_GROUP_DIM // N_CHIPS, POOL_GROUP_DIM).transpose(1, 0, 2, 3)
            to_siblings.append(("pool_w", j, dpw.reshape(N_CHIPS, POOL_GROUP_DIM, POOL_GROUP_DIM)))
            vsum = sum8(vec)
            norm_grads["mix_norm"][i] = vsum[0]
            pool_vec[j] = (vsum[1], vsum[2])
            dh = dh_new

        dh, norm_grads["ffn1_norm"][i] = ffn_backward("ffn1", sv["xn1"], dh, sv["g1"], sv["u1"], i, sv["h0"],
                                                      _Rows(ffn1_norm, i))

    grad_x = dh.reshape(1, S, D_MODEL)

    tail = 0
    while to_siblings:
        cargo, a_items, b_items = take_cargo(with_chips=False)
        absorb(a_items, b_items, _run_cargo("grad_exchange_tail%d" % tail, cargo))
        tail += 1
    small_parts = [
        jnp.stack(norm_grads["ffn1_norm"]), jnp.stack(norm_grads["mix_norm"]),
        jnp.stack(norm_grads["ffn2_norm"]), jnp.stack(norm_grads["ple_norm"]), sum8(dgam_final),
        jnp.stack([lv[0] for lv in lru_vec]), jnp.stack([lv[1] for lv in lru_vec]),
        jnp.stack([lv[2] for lv in lru_vec]), jnp.stack([lv[3] for lv in lru_vec]),
        jnp.stack([lv[4] for lv in lru_vec]), jnp.stack([lv[5] for lv in lru_vec]),
        jnp.stack([lv[6] for lv in lru_vec]),
        jnp.stack([pv[0] for pv in pool_vec]), jnp.stack([pv[1] for pv in pool_vec]),
        jnp.sum(loss_part).reshape(1),
    ]
    small_names = ("ffn1_norm", "mix_norm", "ffn2_norm", "ple_norm", "final_norm", "lru_conv_w", "lru_conv_b",
                   "lru_b_a", "lru_b_x", "lru_a_param", "lru_w_a", "lru_w_x", "pool_b", "pool_scale", "loss")
    cargo, a_items, b_items = take_cargo()
    reduced_buf, c_outs = _allreduce_small(_pack(small_parts, align=2 * SUBLANES * LANES), cargo)
    absorb(a_items, b_items, c_outs)
    reduced = _unpack(reduced_buf, [sp.shape for sp in small_parts])
    small_grad = dict(zip(small_names, reduced))
    loss = small_grad.pop("loss").reshape(())
    for n in TINY_SHARDED:
        width = W[n].shape[-1]
        small_grad[n] = lax.dynamic_slice_in_dim(small_grad[n], my_chip * width, width, axis=-1)

    while to_siblings or to_chips:
        cargo, a_items, b_items = take_cargo()
        absorb(a_items, b_items, _run_cargo("grad_exchange_tail%d" % tail, cargo))
        tail += 1
    big_final = dict(zip(BIG, _join_siblings([stacked[n] for n in BIG])))

    grads, deltas, new_m, new_v = {}, {}, {}, {}
    for n in BIG:
        shp = Ws[n].shape
        to2 = lambda a: a.reshape(-1, shp[-1])
        g2, d, m2, v2 = _adamw(to2(Ws[n]), to2(big_final[n]), to2(Ms[n]), to2(Vs[n]))
        grads[n], deltas[n], new_m[n], new_v[n] = (_stored(n, a.reshape(shp)) for a in (g2, d, m2, v2))
    small_order = TINY_SHARDED + REPLICATED
    small_shapes = [W[n].shape for n in small_order]
    pack_rows = functools.partial(_pack, align=512 * LANES)
    _, sd, sm, sv_ = _adamw(pack_rows([W[n] for n in small_order]), pack_rows([small_grad[n] for n in small_order]),
                            pack_rows([M[n] for n in small_order]), pack_rows([V[n] for n in small_order]))
    for n, d, m2, v2 in zip(small_order, _unpack(sd, small_shapes), _unpack(sm, small_shapes),
                            _unpack(sv_, small_shapes)):
        grads[n], deltas[n], new_m[n], new_v[n] = small_grad[n].reshape(W[n].shape), d, m2, v2

    return (loss, grad_x, *[grads[n] for n in WEIGHT_ORDER], *[deltas[n] for n in WEIGHT_ORDER],
            *[new_m[n] for n in WEIGHT_ORDER], *[new_v[n] for n in WEIGHT_ORDER])
```
